```python
import math
import jax, jax.numpy as jnp
from jax import lax
import numpy as np

D_MODEL = 1024
BATCH = 16
SEQ = 2048
DEPTH = 1
DEC_BATCH = 16
DEC_SEQ = 64
PAST_LEN = 4096

CHUNK = 64
N_Q_HEADS = 8
N_KV_HEADS = 2
GQA = N_Q_HEADS // N_KV_HEADS
HEAD_DIM = 64
WINDOW = 128
WINDOW_BLOCKS = WINDOW // CHUNK
BAND = (WINDOW_BLOCKS + 1) * CHUNK
ATT_CACHE = min(WINDOW, PAST_LEN)
ROPE_DIM = HEAD_DIM // 4
ROPE_THETA = 500000.0
ATT_WIDTH = N_Q_HEADS * HEAD_DIM
KV_WIDTH = N_KV_HEADS * HEAD_DIM
SSM_GROUP = 16
SSM_WIDTH = D_MODEL // 2
N_SSM_GROUPS = SSM_WIDTH // SSM_GROUP
SSM_STATE = 64
MIX_WIDTH = ATT_WIDTH + SSM_WIDTH
IN_WIDTH = ATT_WIDTH + 2 * KV_WIDTH + SSM_WIDTH
N_MEM = 256
CA_HEADS = 4
CA_HEAD_DIM = 128
CA_WIDTH = CA_HEADS * CA_HEAD_DIM
N_EXPERT_GROUPS = 4
EXPERTS_PER_GROUP = 8
N_EXPERTS = N_EXPERT_GROUPS * EXPERTS_PER_GROUP
TOP_K = 2
D_EXPERT = 512
MOE_BLOCK = 256
EPS = 1e-6
NEG = -1e30
F32 = jnp.float32

kernel_name = 'hymba_style_streaming_swa_s5_hmoe_step'


def rms_norm(x, g):
    xf = x.astype(F32)
    y = xf * lax.rsqrt(jnp.mean(xf * xf, axis=-1, keepdims=True) + EPS)
    return (y * g.astype(F32)).astype(x.dtype)


def partial_rope(x, pos):
    half = ROPE_DIM // 2
    inv = ROPE_THETA ** (-jnp.arange(0, ROPE_DIM, 2, dtype=F32) / ROPE_DIM)
    ang = pos.astype(F32)[:, None] * inv[None, :]
    cos = jnp.cos(ang)[:, None, :]
    sin = jnp.sin(ang)[:, None, :]
    xr = x[..., :ROPE_DIM].astype(F32)
    x1, x2 = xr[..., :half], xr[..., half:]
    rot = jnp.concatenate([x1 * cos - x2 * sin, x2 * cos + x1 * sin], axis=-1)
    return jnp.concatenate([rot.astype(x.dtype), x[..., ROPE_DIM:]], axis=-1)


def sink_attention(q, k, v, mask, sink):
    s = jnp.einsum('...qhgd,...khd->...hgqk', q.astype(F32), k.astype(F32)) * (HEAD_DIM ** -0.5)
    s = jnp.where(mask[..., None, None, :, :], s, NEG)
    sk = sink.astype(F32).reshape(N_KV_HEADS, GQA, 1, 1)
    m = jnp.maximum(jnp.max(s, axis=-1, keepdims=True), sk)
    p = jnp.exp(s - m)
    den = jnp.sum(p, axis=-1, keepdims=True) + jnp.exp(sk - m)
    o = jnp.einsum('...hgqk,...khd->...qhgd', p / den, v.astype(F32))
    return o.astype(q.dtype)


def window_attn_prompt(q, k, v, sink):
    B, S = q.shape[:2]
    nb = S // CHUNK
    qb = q.reshape(B, nb, CHUNK, N_KV_HEADS, GQA, HEAD_DIM)
    pad = ((0, 0), (WINDOW_BLOCKS * CHUNK, 0), (0, 0), (0, 0))
    kb = jnp.pad(k, pad).reshape(B, nb + WINDOW_BLOCKS, CHUNK, N_KV_HEADS, HEAD_DIM)
    vb = jnp.pad(v, pad).reshape(B, nb + WINDOW_BLOCKS, CHUNK, N_KV_HEADS, HEAD_DIM)
    kband = jnp.concatenate([kb[:, j:j + nb] for j in range(WINDOW_BLOCKS + 1)], axis=2)
    vband = jnp.concatenate([vb[:, j:j + nb] for j in range(WINDOW_BLOCKS + 1)], axis=2)
    src = jnp.arange(nb)[:, None] + jnp.arange(WINDOW_BLOCKS + 1)[None, :] - WINDOW_BLOCKS
    valid = jnp.repeat(src >= 0, CHUNK, axis=1)
    mask = jnp.broadcast_to(valid[:, None, :], (nb, CHUNK, BAND))
    o = sink_attention(qb, kband, vband, mask, sink)
    return o.reshape(B, S, ATT_WIDTH)


def window_attn_sample(q, k, v, ck, cv, sink):
    B, L = q.shape[:2]
    kk = jnp.concatenate([ck.astype(k.dtype), k], axis=1)
    vv = jnp.concatenate([cv.astype(v.dtype), v], axis=1)
    mask = jnp.ones((L, kk.shape[1]), dtype=bool)
    o = sink_attention(q.reshape(B, L, N_KV_HEADS, GQA, HEAD_DIM), kk, vv, mask, sink)
    return o.reshape(B, L, ATT_WIDTH)


def _lin_rec(e1, e2):
    a1, b1 = e1
    a2, b2 = e2
    return a1 * a2, a2 * b1 + b2


def s5_branch(u, h0, lam_re, lam_im, log_dt, b_re, b_im, c_re, c_im, d, w_glu, b_glu):
    B, L, _ = u.shape
    ug = u.reshape(B, L, N_SSM_GROUPS, SSM_GROUP).astype(F32)
    lam = lax.complex(lam_re.astype(F32), lam_im.astype(F32))
    dt = jnp.exp(log_dt.astype(F32))[:, None]
    lam_bar = jnp.exp(lam * dt)
    bmat = lax.complex(b_re.astype(F32), b_im.astype(F32))
    b_bar = ((lam_bar - 1.0) / lam)[..., None] * bmat
    cmat = lax.complex(c_re.astype(F32), c_im.astype(F32))
    blk = CHUNK if L % CHUNK == 0 else L
    nblk = L // blk
    us = ug.reshape(B, nblk, blk, N_SSM_GROUPS, SSM_GROUP).transpose(1, 0, 2, 3, 4)

    def step(h, u_blk):
        bu = jnp.einsum('gnp,bsgp->bsgn', b_bar, u_blk.astype(jnp.complex64))
        bu = bu.at[:, 0].add(lam_bar * h)
        a = jnp.broadcast_to(lam_bar, bu.shape)
        _, hs = lax.associative_scan(_lin_rec, (a, bu), axis=1)
        y = jnp.einsum('gpn,bsgn->bsgp', cmat, hs).real
        return hs[:, -1], y

    h_last, ys = lax.scan(step, h0, us)
    y = ys.transpose(1, 0, 2, 3, 4).reshape(B, L, N_SSM_GROUPS, SSM_GROUP) + d.astype(F32) * ug
    g = jax.nn.gelu(y)
    out = g * jax.nn.sigmoid(jnp.einsum('bsgp,gpq->bsgq', g, w_glu.astype(F32)) + b_glu.astype(F32))
    return out.reshape(B, L, SSM_WIDTH).astype(u.dtype), h_last


def token_mixer(x, pos, h0, ck, cv, g_mix, w_in, g_q, g_k, sink, ssm_p, g_ao, g_so, w_out):
    B, L, _ = x.shape
    hin = rms_norm(x, g_mix) @ w_in
    o1 = ATT_WIDTH
    o2 = o1 + KV_WIDTH
    o3 = o2 + KV_WIDTH
    q = hin[..., :o1].reshape(B, L, N_Q_HEADS, HEAD_DIM)
    k = hin[..., o1:o2].reshape(B, L, N_KV_HEADS, HEAD_DIM)
    v = hin[..., o2:o3].reshape(B, L, N_KV_HEADS, HEAD_DIM)
    u = hin[..., o3:]
    q = partial_rope(rms_norm(q, g_q), pos)
    k = partial_rope(rms_norm(k, g_k), pos)
    if ck is None:
        att = window_attn_prompt(q, k, v, sink)
    else:
        att = window_attn_sample(q, k, v, ck, cv, sink)
    ssm, h_last = s5_branch(u, h0, *ssm_p)
    mixed = jnp.concatenate([rms_norm(att, g_ao), rms_norm(ssm, g_so)], axis=-1) @ w_out
    return x + mixed, k, v, h_last


def memory_kv(mem, g_mem, w_ck, w_cv, g_k):
    B, M = mem.shape[:2]
    m = rms_norm(mem, g_mem)
    k = rms_norm((m @ w_ck).reshape(B, M, CA_HEADS, CA_HEAD_DIM), g_k)
    v = (m @ w_cv).reshape(B, M, CA_HEADS, CA_HEAD_DIM)
    return k, v


def cross_attention(x, mk, mv, g_x, w_cq, g_q, w_co):
    B, L = x.shape[:2]
    q = rms_norm((rms_norm(x, g_x) @ w_cq).reshape(B, L, CA_HEADS, CA_HEAD_DIM), g_q)
    s = jnp.einsum('bqhd,bkhd->bhqk', q.astype(F32), mk.astype(F32)) * (CA_HEAD_DIM ** -0.5)
    p = jax.nn.softmax(s, axis=-1)
    o = jnp.einsum('bhqk,bkhd->bqhd', p, mv.astype(F32)).astype(x.dtype).reshape(B, L, CA_WIDTH)
    return o @ w_co


def hier_moe(x, w_rg, b_rg, w_re, b_re, w_gate, w_up, w_down):
    shp = x.shape
    h = x.reshape(-1, D_MODEL)
    T = h.shape[0]
    lg = (h @ w_rg).astype(F32) + b_rg.astype(F32)
    pg = jax.nn.softmax(lg, axis=-1)
    pg_top, grp = lax.top_k(pg, 1)
    le = ((h @ w_re).astype(F32) + b_re.astype(F32)).reshape(T, N_EXPERT_GROUPS, EXPERTS_PER_GROUP)
    le_sel = jnp.einsum('tn,tne->te', jax.nn.one_hot(grp[:, 0], N_EXPERT_GROUPS, dtype=F32), le)
    pe = jax.nn.softmax(le_sel, axis=-1)
    w_top, i_top = lax.top_k(pe, TOP_K)
    gate = pg_top * w_top / jnp.sum(w_top, axis=-1, keepdims=True)
    eid = grp * EXPERTS_PER_GROUP + i_top
    n_pairs = T * TOP_K
    flat_e = eid.reshape(-1).astype(jnp.int32)
    flat_w = gate.reshape(-1)
    flat_tok = jnp.repeat(jnp.arange(T, dtype=jnp.int32), TOP_K)
    order = jnp.argsort(flat_e)
    se = flat_e[order]
    counts = jnp.bincount(flat_e, length=N_EXPERTS)
    padded = (counts + MOE_BLOCK - 1) // MOE_BLOCK * MOE_BLOCK
    pad_end = jnp.cumsum(padded)
    pad_start = pad_end - padded
    start = jnp.cumsum(counts) - counts
    dest = pad_start[se] + jnp.arange(n_pairs, dtype=jnp.int32) - start[se]
    n_blocks = (n_pairs + N_EXPERTS * (MOE_BLOCK - 1)) // MOE_BLOCK + 1
    rows = n_blocks * MOE_BLOCK
    row_tok = jnp.full((rows,), T, jnp.int32).at[dest].set(flat_tok[order])
    row_w = jnp.zeros((rows,), F32).at[dest].set(flat_w[order])
    block_e = jnp.minimum(jnp.searchsorted(pad_end, jnp.arange(n_blocks) * MOE_BLOCK, side='right'), N_EXPERTS - 1)
    xb = jnp.concatenate([h, jnp.zeros((1, D_MODEL), h.dtype)], axis=0)[row_tok].reshape(n_blocks, MOE_BLOCK, D_MODEL)

    def expert_block(args):
        xe, e = args
        return (jax.nn.silu(xe @ w_gate[e]) * (xe @ w_up[e])) @ w_down[e]

    yb = lax.map(expert_block, (xb, block_e)).reshape(rows, D_MODEL)
    out = jnp.zeros((T + 1, D_MODEL), F32).at[row_tok].add(yb.astype(F32) * row_w[:, None])[:T]
    return out.astype(x.dtype).reshape(shp)


def setup_inputs(seed: int = 0) -> dict:
    key = jax.random.key(seed)
    ks = iter(jax.random.split(key, 64))

    def nrm(shape, scale):
        return jax.random.normal(next(ks), shape, F32) * scale

    def gain(shape):
        return 1.0 + nrm(shape, 0.02)

    G, N, P = N_SSM_GROUPS, SSM_STATE, SSM_GROUP
    return {
        'x_prompt': nrm((BATCH, SEQ, D_MODEL), 1.0),
        'x_sample': nrm((DEC_BATCH, DEC_SEQ, D_MODEL), 1.0),
        'cache_attn_k': nrm((DEPTH, DEC_BATCH, ATT_CACHE, N_KV_HEADS, HEAD_DIM), 1.0),
        'cache_attn_v': nrm((DEPTH, DEC_BATCH, ATT_CACHE, N_KV_HEADS, HEAD_DIM), 1.0),
        'state_ssm_re': nrm((DEPTH, DEC_BATCH, G, N), 0.1),
        'state_ssm_im': nrm((DEPTH, DEC_BATCH, G, N), 0.1),
        'cache_mem_k': nrm((DEPTH, DEC_BATCH, N_MEM, CA_HEADS, CA_HEAD_DIM), 1.0),
        'cache_mem_v': nrm((DEPTH, DEC_BATCH, N_MEM, CA_HEADS, CA_HEAD_DIM), 1.0),
        'mem_prompt': nrm((BATCH, N_MEM, D_MODEL), 1.0),
        'norm_mix': gain((DEPTH, D_MODEL)),
        'w_in': nrm((DEPTH, D_MODEL, IN_WIDTH), D_MODEL ** -0.5),
        'q_norm': gain((DEPTH, HEAD_DIM)),
        'k_norm': gain((DEPTH, HEAD_DIM)),
        'attn_sink': nrm((DEPTH, N_Q_HEADS), 0.5),
        'ssm_lambda_re': -0.5 + nrm((DEPTH, G, N), 0.01),
        'ssm_lambda_im': math.pi * jnp.arange(N, dtype=F32)[None, None, :] + nrm((DEPTH, G, N), 0.01),
        'ssm_log_dt': jax.random.uniform(next(ks), (DEPTH, G), F32, math.log(1e-3), math.log(1e-1)),
        'ssm_b_re': nrm((DEPTH, G, N, P), (2.0 * P) ** -0.5),
        'ssm_b_im': nrm((DEPTH, G, N, P), (2.0 * P) ** -0.5),
        'ssm_c_re': nrm((DEPTH, G, P, N), (2.0 * N) ** -0.5),
        'ssm_c_im': nrm((DEPTH, G, P, N), (2.0 * N) ** -0.5),
        'ssm_d': nrm((DEPTH, G, P), 1.0),
        'ssm_w_glu': nrm((DEPTH, G, P, P), P ** -0.5),
        'ssm_b_glu': nrm((DEPTH, G, P), 0.02),
        'norm_attn_out': gain((DEPTH, ATT_WIDTH)),
        'norm_ssm_out': gain((DEPTH, SSM_WIDTH)),
        'w_out': nrm((DEPTH, MIX_WIDTH, D_MODEL), MIX_WIDTH ** -0.5),
        'norm_cross': gain((DEPTH, D_MODEL)),
        'norm_mem': gain((DEPTH, D_MODEL)),
        'w_cq': nrm((DEPTH, D_MODEL, CA_WIDTH), D_MODEL ** -0.5),
        'w_ck': nrm((DEPTH, D_MODEL, CA_WIDTH), D_MODEL ** -0.5),
        'w_cv': nrm((DEPTH, D_MODEL, CA_WIDTH), D_MODEL ** -0.5),
        'cq_norm': gain((DEPTH, CA_HEAD_DIM)),
        'ck_norm': gain((DEPTH, CA_HEAD_DIM)),
        'w_co': nrm((DEPTH, CA_WIDTH, D_MODEL), CA_WIDTH ** -0.5),
        'norm_ffn': gain((DEPTH, D_MODEL)),
        'w_router_group': nrm((DEPTH, D_MODEL, N_EXPERT_GROUPS), D_MODEL ** -0.5),
        'b_router_group': nrm((DEPTH, N_EXPERT_GROUPS), 0.01),
        'w_router_expert': nrm((DEPTH, D_MODEL, N_EXPERTS), D_MODEL ** -0.5),
        'b_router_expert': nrm((DEPTH, N_EXPERTS), 0.01),
        'w_e_gate': nrm((DEPTH, N_EXPERTS, D_MODEL, D_EXPERT), D_MODEL ** -0.5),
        'w_e_up': nrm((DEPTH, N_EXPERTS, D_MODEL, D_EXPERT), D_MODEL ** -0.5),
        'w_e_down': nrm((DEPTH, N_EXPERTS, D_EXPERT, D_MODEL), D_EXPERT ** -0.5),
    }


def reference(x_prompt, x_sample, cache_attn_k, cache_attn_v, state_ssm_re, state_ssm_im,
              cache_mem_k, cache_mem_v, mem_prompt,
              norm_mix, w_in, q_norm, k_norm, attn_sink,
              ssm_lambda_re, ssm_lambda_im, ssm_log_dt, ssm_b_re, ssm_b_im, ssm_c_re, ssm_c_im,
              ssm_d, ssm_w_glu, ssm_b_glu, norm_attn_out, norm_ssm_out, w_out,
              norm_cross, norm_mem, w_cq, w_ck, w_cv, cq_norm, ck_norm, w_co,
              norm_ffn, w_router_group, b_router_group, w_router_expert, b_router_expert,
              w_e_gate, w_e_up, w_e_down):
    yp, ys = x_prompt, x_sample
    Bp, Lp = yp.shape[:2]
    Ls = ys.shape[1]
    pos_p = jnp.arange(Lp, dtype=jnp.int32)
    pos_s = PAST_LEN + jnp.arange(Ls, dtype=jnp.int32)
    kp_l, vp_l, hpr_l, hpi_l, mkp_l, mvp_l = [], [], [], [], [], []
    ks_l, vs_l, hsr_l, hsi_l = [], [], [], []
    for l in range(DEPTH):
        ssm_p = (ssm_lambda_re[l], ssm_lambda_im[l], ssm_log_dt[l], ssm_b_re[l], ssm_b_im[l],
                 ssm_c_re[l], ssm_c_im[l], ssm_d[l], ssm_w_glu[l], ssm_b_glu[l])
        h0p = jnp.zeros((Bp, N_SSM_GROUPS, SSM_STATE), jnp.complex64)
        h0s = lax.complex(state_ssm_re[l].astype(F32), state_ssm_im[l].astype(F32))
        yp, kp, vp, hp = token_mixer(yp, pos_p, h0p, None, None, norm_mix[l], w_in[l], q_norm[l], k_norm[l],
                                     attn_sink[l], ssm_p, norm_attn_out[l], norm_ssm_out[l], w_out[l])
        ys, kn, vn, hn = token_mixer(ys, pos_s, h0s, cache_attn_k[l], cache_attn_v[l], norm_mix[l], w_in[l],
                                     q_norm[l], k_norm[l], attn_sink[l], ssm_p, norm_attn_out[l],
                                     norm_ssm_out[l], w_out[l])
        mkp, mvp = memory_kv(mem_prompt, norm_mem[l], w_ck[l], w_cv[l], ck_norm[l])
        yp = yp + cross_attention(yp, mkp, mvp, norm_cross[l], w_cq[l], cq_norm[l], w_co[l])
        ys = ys + cross_attention(ys, cache_mem_k[l], cache_mem_v[l], norm_cross[l], w_cq[l], cq_norm[l], w_co[l])
        yp = yp + hier_moe(rms_norm(yp, norm_ffn[l]), w_router_group[l], b_router_group[l], w_router_expert[l],
                           b_router_expert[l], w_e_gate[l], w_e_up[l], w_e_down[l])
        ys = ys + hier_moe(rms_norm(ys, norm_ffn[l]), w_router_group[l], b_router_group[l], w_router_expert[l],
                           b_router_expert[l], w_e_gate[l], w_e_up[l], w_e_down[l])
        kp_l.append(kp[:, Lp - ATT_CACHE:])
        vp_l.append(vp[:, Lp - ATT_CACHE:])
        hpr_l.append(hp.real)
        hpi_l.append(hp.imag)
        mkp_l.append(mkp)
        mvp_l.append(mvp)
        ks_l.append(kn)
        vs_l.append(vn)
        hsr_l.append(hn.real)
        hsi_l.append(hn.imag)
    return (yp, ys,
            jnp.stack(kp_l), jnp.stack(vp_l), jnp.stack(hpr_l), jnp.stack(hpi_l),
            jnp.stack(mkp_l), jnp.stack(mvp_l),
            jnp.stack(ks_l), jnp.stack(vs_l), jnp.stack(hsr_l), jnp.stack(hsi_l))
```

```python
import functools
import math

import jax
import jax.numpy as jnp
from jax import lax
from jax.experimental import pallas as pl
from jax.experimental.pallas import tpu as pltpu

F32 = jnp.float32
BF16 = jnp.bfloat16

D_MODEL = 1024
CHUNK = 64
N_Q_HEADS = 8
N_KV_HEADS = 2
GQA = N_Q_HEADS // N_KV_HEADS
HEAD_DIM = 64
WINDOW = 128
BAND = WINDOW + CHUNK
ROPE_DIM = HEAD_DIM // 4
ROPE_THETA = 500000.0
ATT_WIDTH = N_Q_HEADS * HEAD_DIM
KV_WIDTH = N_KV_HEADS * HEAD_DIM
SSM_GROUP = 16
SSM_WIDTH = D_MODEL // 2
N_SSM_GROUPS = SSM_WIDTH // SSM_GROUP
SSM_STATE = 64
SSM_COLS = N_SSM_GROUPS * SSM_STATE
IN_WIDTH = ATT_WIDTH + 2 * KV_WIDTH + SSM_WIDTH
N_MEM = 256
CA_HEADS = 4
CA_HEAD_DIM = 128
CA_WIDTH = CA_HEADS * CA_HEAD_DIM
N_EXPERT_GROUPS = 4
EXPERTS_PER_GROUP = 8
N_EXPERTS = N_EXPERT_GROUPS * EXPERTS_PER_GROUP
D_EXPERT = 512
MOE_BLOCK = 256
EPS = 1e-6
NEG = -1e30
PAST_LEN = 4096

LANES = 128
ROUTER_COL0 = N_EXPERT_GROUPS
VMEM_LIMIT = 48 * 1024 * 1024


def _rms(x, g):
    ms = jnp.mean(x * x, axis=-1, keepdims=True)
    return (x * lax.rsqrt(ms + EPS)) * g


def _mm(a, b):
    return jnp.dot(a, b, preferred_element_type=F32)


def _in_proj_kernel(x_ref, g_ref, w_ref, gq_ref, gk_ref, rope_ref,
                    q_ref, k_ref, v_ref, u_ref):
    tm = x_ref.shape[0]
    h = _rms(x_ref[...], g_ref[...])
    hin = _mm(h.astype(BF16), w_ref[...])
    rope = rope_ref[...]
    cos = rope[:, 0:LANES]
    sin_lo = rope[:, LANES:2 * LANES]
    sin_hi = rope[:, 2 * LANES:3 * LANES]
    lane = lax.broadcasted_iota(jnp.int32, (tm, LANES), 1)
    left = lane < HEAD_DIM

    def norm_rope(z, g):
        sq = z * z
        lsum = jnp.sum(jnp.where(left, sq, 0.0), axis=-1, keepdims=True)
        rsum = jnp.sum(jnp.where(left, 0.0, sq), axis=-1, keepdims=True)
        ms = jnp.where(left, lsum, rsum) * (1.0 / HEAD_DIM)
        zn = (z * lax.rsqrt(ms + EPS)) * g
        half = ROPE_DIM // 2
        return (zn * cos + pltpu.roll(zn, LANES - half, 1) * sin_lo
                + pltpu.roll(zn, half, 1) * sin_hi)

    for j in range(ATT_WIDTH // LANES):
        sl = slice(j * LANES, (j + 1) * LANES)
        q_ref[:, sl] = norm_rope(hin[:, sl], gq_ref[...])
    k_ref[...] = norm_rope(hin[:, ATT_WIDTH:ATT_WIDTH + KV_WIDTH], gk_ref[...])
    v_ref[...] = hin[:, ATT_WIDTH + KV_WIDTH:ATT_WIDTH + 2 * KV_WIDTH]
    u_ref[...] = hin[:, ATT_WIDTH + 2 * KV_WIDTH:]


def _in_proj(x2d, g, w_bf, gq, gk, rope, tm):
    T = x2d.shape[0]
    nper = rope.shape[0] // tm
    full = lambda i: (0, 0)
    return pl.pallas_call(
        _in_proj_kernel,
        grid=(T // tm,),
        in_specs=[
            pl.BlockSpec((tm, D_MODEL), lambda i: (i, 0)),
            pl.BlockSpec((1, D_MODEL), full),
            pl.BlockSpec((D_MODEL, IN_WIDTH), full),
            pl.BlockSpec((1, LANES), full),
            pl.BlockSpec((1, LANES), full),
            pl.BlockSpec((tm, 3 * LANES), lambda i: (i % nper, 0)),
        ],
        out_specs=[
            pl.BlockSpec((tm, ATT_WIDTH), lambda i: (i, 0)),
            pl.BlockSpec((tm, KV_WIDTH), lambda i: (i, 0)),
            pl.BlockSpec((tm, KV_WIDTH), lambda i: (i, 0)),
            pl.BlockSpec((tm, SSM_WIDTH), lambda i: (i, 0)),
        ],
        out_shape=[
            jax.ShapeDtypeStruct((T, ATT_WIDTH), F32),
            jax.ShapeDtypeStruct((T, KV_WIDTH), F32),
            jax.ShapeDtypeStruct((T, KV_WIDTH), F32),
            jax.ShapeDtypeStruct((T, SSM_WIDTH), F32),
        ],
        compiler_params=pltpu.CompilerParams(
            dimension_semantics=("arbitrary",), vmem_limit_bytes=VMEM_LIMIT),
        name="in_proj",
    )(x2d, g, w_bf, gq, gk, rope)


def _swa_kernel(sink_ref, q_ref, k_ref, v_ref, o_ref, *, mask_context):
    tq = q_ref.shape[1]
    i = pl.program_id(1)
    nch = tq // CHUNK
    for c in range(nch):
        start = pl.multiple_of((i * nch + c) * CHUNK, CHUNK)
        kb = k_ref[0, pl.ds(start, BAND), :]
        vb = v_ref[0, pl.ds(start, BAND), :]
        qc = q_ref[0, c * CHUNK:(c + 1) * CHUNK, :]
        if mask_context:
            kidx = start + lax.broadcasted_iota(jnp.int32, (1, BAND), 1)
            valid = kidx >= WINDOW
        outs = []
        for kvh in range(N_KV_HEADS):
            hs = slice(kvh * HEAD_DIM, (kvh + 1) * HEAD_DIM)
            kh = kb[:, hs].astype(BF16)
            vh = vb[:, hs].astype(BF16)
            heads = [kvh * GQA + j for j in range(GQA)]
            qs = jnp.concatenate(
                [qc[:, hq * HEAD_DIM:(hq + 1) * HEAD_DIM] for hq in heads], axis=0)
            s = lax.dot_general(qs.astype(BF16), kh, (((1,), (1,)), ((), ())),
                                preferred_element_type=F32) * (HEAD_DIM ** -0.5)
            if mask_context:
                s = jnp.where(valid, s, NEG)
            sk = jnp.concatenate(
                [jnp.full((CHUNK, 1), sink_ref[hq], F32) for hq in heads], axis=0)
            m = jnp.maximum(jnp.max(s, axis=-1, keepdims=True), sk)
            p = jnp.exp(s - m)
            den = jnp.sum(p, axis=-1, keepdims=True) + jnp.exp(sk - m)
            o = _mm(p.astype(BF16), vh) / den
            outs += [o[j * CHUNK:(j + 1) * CHUNK] for j in range(GQA)]
        o_ref[0, c * CHUNK:(c + 1) * CHUNK, :] = jnp.concatenate(outs, axis=1)


def _swa(sink, q, kctx, vctx, tq, mask_context):
    B, Sq, _ = q.shape
    Sk = kctx.shape[1]
    return pl.pallas_call(
        functools.partial(_swa_kernel, mask_context=mask_context),
        grid=(B, Sq // tq),
        in_specs=[
            pl.BlockSpec(memory_space=pltpu.SMEM),
            pl.BlockSpec((1, tq, ATT_WIDTH), lambda b, i: (b, i, 0)),
            pl.BlockSpec((1, Sk, KV_WIDTH), lambda b, i: (b, 0, 0)),
            pl.BlockSpec((1, Sk, KV_WIDTH), lambda b, i: (b, 0, 0)),
        ],
        out_specs=pl.BlockSpec((1, tq, ATT_WIDTH), lambda b, i: (b, i, 0)),
        out_shape=jax.ShapeDtypeStruct((B, Sq, ATT_WIDTH), F32),
        compiler_params=pltpu.CompilerParams(
            dimension_semantics=("arbitrary", "arbitrary"),
            vmem_limit_bytes=VMEM_LIMIT),
        name="swa",
    )(sink, q, kctx, vctx)


def _ssm_kernel(u_ref, h0r_ref, h0i_ref, lam_ref, bre_ref, bim_ref, cre_ref, cim_ref,
                d_ref, wglu_ref, bglu_ref,
                y_ref, hr_out, hi_out, sr, si, hr_s, hi_s):
    B, L, _ = u_ref.shape
    rows = B * L
    half_w = SSM_WIDTH // 2
    half_c = SSM_COLS // 2

    @pl.when(pl.program_id(0) == 0)
    def _():
        hr_s[...] = h0r_ref[...]
        hi_s[...] = h0i_ref[...]

    u = u_ref[...].reshape(rows, SSM_WIDTH)
    ub = u.astype(BF16)
    nlb = half_c // LANES
    for hf in range(2):
        uh = ub[:, hf * half_w:(hf + 1) * half_w]
        pr = _mm(uh, bre_ref[hf])
        pi = _mm(uh, bim_ref[hf])
        for j in range(nlb):
            sr[hf * nlb + j] = pr[:, j * LANES:(j + 1) * LANES]
            si[hf * nlb + j] = pi[:, j * LANES:(j + 1) * LANES]

    grp = 4
    for cc in range(2 * nlb // grp):
        blocks = [cc * grp + j for j in range(grp)]
        lam_r = [jnp.broadcast_to(lam_ref[0:1, b * LANES:(b + 1) * LANES], (B, LANES))
                 for b in blocks]
        lam_i = [jnp.broadcast_to(lam_ref[1:2, b * LANES:(b + 1) * LANES], (B, LANES))
                 for b in blocks]

        def body(t, carry):
            at_t = pl.ds(t, B, stride=L)
            new = []
            for j, b in enumerate(blocks):
                hr, hi = carry[2 * j], carry[2 * j + 1]
                nr = lam_r[j] * hr - lam_i[j] * hi + sr[b, at_t, :]
                ni = lam_r[j] * hi + lam_i[j] * hr + si[b, at_t, :]
                sr[b, at_t, :] = nr
                si[b, at_t, :] = ni
                new += [nr, ni]
            return tuple(new)

        init = []
        for b in blocks:
            init += [hr_s[:, b * LANES:(b + 1) * LANES], hi_s[:, b * LANES:(b + 1) * LANES]]
        fin = lax.fori_loop(0, L, body, tuple(init))
        for j, b in enumerate(blocks):
            hr_s[:, b * LANES:(b + 1) * LANES] = fin[2 * j]
            hi_s[:, b * LANES:(b + 1) * LANES] = fin[2 * j + 1]

    ys = []
    for hf in range(2):
        hsr = jnp.concatenate([sr[hf * nlb + j].astype(BF16) for j in range(nlb)], axis=1)
        hsi = jnp.concatenate([si[hf * nlb + j].astype(BF16) for j in range(nlb)], axis=1)
        ys.append(_mm(hsr, cre_ref[hf]) + _mm(hsi, cim_ref[hf]))
    y = jnp.concatenate(ys, axis=1) + d_ref[...] * u
    g = 0.5 * y * (1.0 + jnp.tanh(math.sqrt(2.0 / math.pi) * (y + 0.044715 * (y * y * y))))
    gb = g.astype(BF16)
    z = jnp.concatenate(
        [_mm(gb[:, hf * half_w:(hf + 1) * half_w], wglu_ref[hf]) for hf in range(2)],
        axis=1) + bglu_ref[...]
    out = g * (1.0 / (1.0 + jnp.exp(-z)))
    y_ref[...] = out.reshape(B, L, SSM_WIDTH)
    hr_out[...] = hr_s[...]
    hi_out[...] = hi_s[...]


def _ssm(u, h0r, h0i, sp, L):
    B, S, _ = u.shape
    c2 = lambda i: (0, 0)
    c3 = lambda i: (0, 0, 0)
    return pl.pallas_call(
        _ssm_kernel,
        grid=(S // L,),
        in_specs=[
            pl.BlockSpec((B, L, SSM_WIDTH), lambda i: (0, i, 0)),
            pl.BlockSpec((B, SSM_COLS), c2),
            pl.BlockSpec((B, SSM_COLS), c2),
            pl.BlockSpec((2, SSM_COLS), c2),
            pl.BlockSpec((2, SSM_WIDTH // 2, SSM_COLS // 2), c3),
            pl.BlockSpec((2, SSM_WIDTH // 2, SSM_COLS // 2), c3),
            pl.BlockSpec((2, SSM_COLS // 2, SSM_WIDTH // 2), c3),
            pl.BlockSpec((2, SSM_COLS // 2, SSM_WIDTH // 2), c3),
            pl.BlockSpec((1, SSM_WIDTH), c2),
            pl.BlockSpec((2, SSM_WIDTH // 2, SSM_WIDTH // 2), c3),
            pl.BlockSpec((1, SSM_WIDTH), c2),
        ],
        out_specs=[
            pl.BlockSpec((B, L, SSM_WIDTH), lambda i: (0, i, 0)),
            pl.BlockSpec((B, SSM_COLS), c2),
            pl.BlockSpec((B, SSM_COLS), c2),
        ],
        out_shape=[
            jax.ShapeDtypeStruct((B, S, SSM_WIDTH), F32),
            jax.ShapeDtypeStruct((B, SSM_COLS), F32),
            jax.ShapeDtypeStruct((B, SSM_COLS), F32),
        ],
        scratch_shapes=[
            pltpu.VMEM((SSM_COLS // LANES, B * L, LANES), F32),
            pltpu.VMEM((SSM_COLS // LANES, B * L, LANES), F32),
            pltpu.VMEM((B, SSM_COLS), F32),
            pltpu.VMEM((B, SSM_COLS), F32),
        ],
        compiler_params=pltpu.CompilerParams(
            dimension_semantics=("arbitrary",), vmem_limit_bytes=VMEM_LIMIT),
        name="ssm",
    )(u, h0r, h0i, sp["lam"], sp["bre"], sp["bim"], sp["cre"], sp["cim"],
      sp["d"], sp["wglu"], sp["bglu"])


def _block_diag(blocks):
    G, r, c = blocks.shape
    eye = jnp.eye(G, dtype=blocks.dtype)
    return jnp.einsum("grc,gh->grhc", blocks, eye).reshape(G * r, G * c)


def _ssm_params(lam_re, lam_im, log_dt, b_re, b_im, c_re, c_im, d, w_glu, b_glu):
    lam = lax.complex(lam_re.astype(F32), lam_im.astype(F32))
    dt = jnp.exp(log_dt.astype(F32))[:, None]
    lam_bar = jnp.exp(lam * dt)
    bmat = lax.complex(b_re.astype(F32), b_im.astype(F32))
    b_bar = ((lam_bar - 1.0) / lam)[..., None] * bmat
    lam2 = jnp.stack([lam_bar.real.reshape(-1), lam_bar.imag.reshape(-1)])
    bt = jnp.swapaxes(b_bar, 1, 2)
    hw, hc = SSM_WIDTH // 2, SSM_COLS // 2
    split_b = lambda m: jnp.stack([m[:hw, :hc], m[hw:, hc:]]).astype(BF16)
    split_c = lambda m: jnp.stack([m[:hc, :hw], m[hc:, hw:]]).astype(BF16)
    ct_re = jnp.swapaxes(c_re.astype(F32), 1, 2)
    ct_im = jnp.swapaxes(c_im.astype(F32), 1, 2)
    wg = _block_diag(w_glu.astype(F32))
    return {
        "lam": lam2,
        "bre": split_b(_block_diag(bt.real)),
        "bim": split_b(_block_diag(bt.imag)),
        "cre": split_c(_block_diag(ct_re)),
        "cim": split_c(_block_diag(-ct_im)),
        "d": d.astype(F32).reshape(1, SSM_WIDTH),
        "wglu": jnp.stack([wg[:hw, :hw], wg[hw:, hw:]]).astype(BF16),
        "bglu": b_glu.astype(F32).reshape(1, SSM_WIDTH),
    }


def _memkv_kernel(m_ref, g_ref, w_ref, gk_ref, k_ref, v_ref):
    m = _rms(m_ref[...], g_ref[...])
    kv = _mm(m.astype(BF16), w_ref[...])
    for h in range(CA_HEADS):
        sl = slice(h * CA_HEAD_DIM, (h + 1) * CA_HEAD_DIM)
        k_ref[:, sl] = _rms(kv[:, sl], gk_ref[...])
    v_ref[...] = kv[:, CA_WIDTH:]


def _memkv(mem2d, g, w_bf, gk, tm):
    T = mem2d.shape[0]
    full = lambda i: (0, 0)
    return pl.pallas_call(
        _memkv_kernel,
        grid=(T // tm,),
        in_specs=[
            pl.BlockSpec((tm, D_MODEL), lambda i: (i, 0)),
            pl.BlockSpec((1, D_MODEL), full),
            pl.BlockSpec((D_MODEL, 2 * CA_WIDTH), full),
            pl.BlockSpec((1, CA_HEAD_DIM), full),
        ],
        out_specs=[
            pl.BlockSpec((tm, CA_WIDTH), lambda i: (i, 0)),
            pl.BlockSpec((tm, CA_WIDTH), lambda i: (i, 0)),
        ],
        out_shape=[
            jax.ShapeDtypeStruct((T, CA_WIDTH), F32),
            jax.ShapeDtypeStruct((T, CA_WIDTH), F32),
        ],
        compiler_params=pltpu.CompilerParams(
            dimension_semantics=("arbitrary",), vmem_limit_bytes=VMEM_LIMIT),
        name="memkv",
    )(mem2d, g, w_bf, gk)


def _mid_kernel(x_ref, att_ref, ssm_ref, mk_ref, mv_ref,
                gao_ref, gso_ref, wout_ref, gx_ref, wcq_ref, gcq_ref, wco_ref,
                gffn_ref, wr_ref, br_ref,
                x2_ref, hn_ref, rt_ref, cnt_ref, base_s):
    tm = x_ref.shape[1]

    @pl.when((pl.program_id(0) == 0) & (pl.program_id(1) == 0))
    def _():
        base_s[...] = jnp.zeros_like(base_s)

    a = _rms(att_ref[0], gao_ref[...]).astype(BF16)
    s = _rms(ssm_ref[0], gso_ref[...]).astype(BF16)
    x1 = (x_ref[0] + _mm(a, wout_ref[0:ATT_WIDTH, :])
          + _mm(s, wout_ref[ATT_WIDTH:, :]))

    qx = _mm(_rms(x1, gx_ref[...]).astype(BF16), wcq_ref[...])
    heads = []
    for h in range(CA_HEADS):
        sl = slice(h * CA_HEAD_DIM, (h + 1) * CA_HEAD_DIM)
        qh = _rms(qx[:, sl], gcq_ref[...]).astype(BF16)
        kh = mk_ref[0, :, sl].astype(BF16)
        vh = mv_ref[0, :, sl].astype(BF16)
        sc = lax.dot_general(qh, kh, (((1,), (1,)), ((), ())),
                             preferred_element_type=F32) * (CA_HEAD_DIM ** -0.5)
        p = jnp.exp(sc - jnp.max(sc, axis=-1, keepdims=True))
        p = p / jnp.sum(p, axis=-1, keepdims=True)
        heads.append(_mm(p.astype(BF16), vh))
    o = jnp.concatenate(heads, axis=1).astype(BF16)
    x2 = x1 + _mm(o, wco_ref[...])
    x2_ref[0] = x2

    hn = _rms(x2, gffn_ref[...])
    hn_ref[0] = hn

    h_hi = hn.astype(BF16)
    h_lo = (hn - h_hi.astype(F32)).astype(BF16)
    r1 = _mm(h_hi, wr_ref[...])
    lg = (r1[:, :LANES] + r1[:, LANES:] + _mm(h_lo, wr_ref[:, 0:LANES])
          + br_ref[...])

    col = lax.broadcasted_iota(jnp.int32, (tm, LANES), 1)
    big = jnp.int32(4 * LANES)
    gmask = col < N_EXPERT_GROUPS
    lgg = jnp.where(gmask, lg, NEG)
    mg = jnp.max(lgg, axis=-1, keepdims=True)
    grp = jnp.min(jnp.where(gmask & (lgg == mg), col, big), axis=-1, keepdims=True)
    pg_top = 1.0 / jnp.sum(jnp.where(gmask, jnp.exp(lgg - mg), 0.0), axis=-1, keepdims=True)

    ecol = col - ROUTER_COL0
    emask = ((ecol >= 0) & (ecol < N_EXPERTS)
             & (lax.shift_right_arithmetic(ecol, 3) == grp))
    le = jnp.where(emask, lg, NEG)
    m1 = jnp.max(le, axis=-1, keepdims=True)
    i1 = jnp.min(jnp.where(emask & (le == m1), col, big), axis=-1, keepdims=True)
    rest = emask & (col != i1)
    le2 = jnp.where(rest, lg, NEG)
    m2 = jnp.max(le2, axis=-1, keepdims=True)
    i2 = jnp.min(jnp.where(rest & (le2 == m2), col, big), axis=-1, keepdims=True)
    den = jnp.sum(jnp.where(emask, jnp.exp(le - m1), 0.0), axis=-1, keepdims=True)
    p1 = 1.0 / den
    p2 = jnp.exp(m2 - m1) / den
    gate1 = pg_top * p1 / (p1 + p2)
    gate2 = pg_top * p2 / (p1 + p2)

    sel1 = col == i1
    sel2 = col == i2
    oh = jnp.where(sel1 | sel2, 1.0, 0.0)
    r_i = lax.broadcasted_iota(jnp.int32, (tm, tm), 0)
    c_i = lax.broadcasted_iota(jnp.int32, (tm, tm), 1)
    tri = jnp.where(r_i > c_i, 1.0, 0.0).astype(BF16)
    tot = base_s[...] + _mm(tri, oh.astype(BF16))
    rank1 = jnp.sum(jnp.where(sel1, tot, 0.0), axis=-1, keepdims=True)
    rank2 = jnp.sum(jnp.where(sel2, tot, 0.0), axis=-1, keepdims=True)
    base_s[...] = base_s[...] + jnp.sum(oh, axis=0, keepdims=True)
    cnt_ref[...] = base_s[...]

    e1 = (i1 - ROUTER_COL0).astype(F32)
    e2 = (i2 - ROUTER_COL0).astype(F32)
    rt = jnp.zeros((tm, LANES), F32)
    for k, val in enumerate((e1, e2, gate1, gate2, rank1, rank2)):
        rt = jnp.where(col == k, val, rt)
    rt_ref[0] = rt


def _mid(x, att, ssm, mk, mv, wp, tm):
    B, S, _ = x.shape
    c2 = lambda b, i: (0, 0)
    tile = lambda w: pl.BlockSpec((1, tm, w), lambda b, i: (b, i, 0))
    return pl.pallas_call(
        _mid_kernel,
        grid=(B, S // tm),
        in_specs=[
            tile(D_MODEL), tile(ATT_WIDTH), tile(SSM_WIDTH),
            pl.BlockSpec((1, N_MEM, CA_WIDTH), lambda b, i: (b, 0, 0)),
            pl.BlockSpec((1, N_MEM, CA_WIDTH), lambda b, i: (b, 0, 0)),
            pl.BlockSpec((1, ATT_WIDTH), c2),
            pl.BlockSpec((1, SSM_WIDTH), c2),
            pl.BlockSpec((ATT_WIDTH + SSM_WIDTH, D_MODEL), c2),
            pl.BlockSpec((1, D_MODEL), c2),
            pl.BlockSpec((D_MODEL, CA_WIDTH), c2),
            pl.BlockSpec((1, CA_HEAD_DIM), c2),
            pl.BlockSpec((CA_WIDTH, D_MODEL), c2),
            pl.BlockSpec((1, D_MODEL), c2),
            pl.BlockSpec((D_MODEL, 2 * LANES), c2),
            pl.BlockSpec((1, LANES), c2),
        ],
        out_specs=[
            tile(D_MODEL), tile(D_MODEL), tile(LANES),
            pl.BlockSpec((1, LANES), c2),
        ],
        out_shape=[
            jax.ShapeDtypeStruct((B, S, D_MODEL), F32),
            jax.ShapeDtypeStruct((B, S, D_MODEL), F32),
            jax.ShapeDtypeStruct((B, S, LANES), F32),
            jax.ShapeDtypeStruct((1, LANES), F32),
        ],
        scratch_shapes=[pltpu.VMEM((1, LANES), F32)],
        compiler_params=pltpu.CompilerParams(
            dimension_semantics=("arbitrary", "arbitrary"),
            vmem_limit_bytes=VMEM_LIMIT),
        name="mid",
    )(x, att, ssm, mk, mv, wp["gao"], wp["gso"], wp["wout"], wp["gx"], wp["wcq"],
      wp["gcq"], wp["wco"], wp["gffn"], wp["wr"], wp["br"])


def _moe_kernel(be_ref, nu_ref, tok_ref, hn_hbm, wg_ref, wu_ref, wd_ref, yb_ref,
                xbuf, sem):
    i = pl.program_id(0)

    @pl.when(i < nu_ref[0])
    def _():
        def issue(r, carry):
            pltpu.make_async_copy(hn_hbm.at[pl.ds(tok_ref[0, 0, r], 1), :],
                                  xbuf.at[pl.ds(r, 1), :], sem).start()
            return carry

        lax.fori_loop(0, MOE_BLOCK, issue, 0)
        pltpu.make_async_copy(hn_hbm.at[pl.ds(0, MOE_BLOCK), :], xbuf, sem).wait()
        xe = xbuf[...].astype(BF16)
        g = _mm(xe, wg_ref[0])
        u = _mm(xe, wu_ref[0])
        hmid = (g * (1.0 / (1.0 + jnp.exp(-g)))) * u
        yb_ref[...] = _mm(hmid.astype(BF16), wd_ref[0])

    @pl.when(i >= nu_ref[0])
    def _():
        yb_ref[...] = jnp.zeros_like(yb_ref)


def _moe(block_e, n_used, row_tok, hn2d, wg_bf, wu_bf, wd_bf):
    n_blocks = block_e.shape[0]
    grid_spec = pltpu.PrefetchScalarGridSpec(
        num_scalar_prefetch=2,
        grid=(n_blocks,),
        in_specs=[
            pl.BlockSpec((1, 1, MOE_BLOCK), lambda i, be, nu: (i, 0, 0),
                         memory_space=pltpu.SMEM),
            pl.BlockSpec(memory_space=pl.ANY),
            pl.BlockSpec((1, D_MODEL, D_EXPERT), lambda i, be, nu: (be[i], 0, 0)),
            pl.BlockSpec((1, D_MODEL, D_EXPERT), lambda i, be, nu: (be[i], 0, 0)),
            pl.BlockSpec((1, D_EXPERT, D_MODEL), lambda i, be, nu: (be[i], 0, 0)),
        ],
        out_specs=pl.BlockSpec((MOE_BLOCK, D_MODEL), lambda i, be, nu: (i, 0)),
        scratch_shapes=[
            pltpu.VMEM((MOE_BLOCK, D_MODEL), F32),
            pltpu.SemaphoreType.DMA(()),
        ],
    )
    return pl.pallas_call(
        _moe_kernel,
        grid_spec=grid_spec,
        out_shape=jax.ShapeDtypeStruct((n_blocks * MOE_BLOCK, D_MODEL), F32),
        compiler_params=pltpu.CompilerParams(
            dimension_semantics=("arbitrary",), vmem_limit_bytes=VMEM_LIMIT),
        name="moe",
    )(block_e, n_used, row_tok.reshape(n_blocks, 1, MOE_BLOCK), hn2d, wg_bf, wu_bf, wd_bf)


def _combine_kernel(dest_ref, x2_ref, rt_ref, yb_hbm, o_ref, buf, sem):
    tm = x2_ref.shape[0]

    def issue(r, carry):
        for k in range(2):
            pltpu.make_async_copy(yb_hbm.at[pl.ds(dest_ref[0, 0, k * tm + r], 1), :],
                                  buf.at[k, pl.ds(r, 1), :], sem).start()
        return carry

    lax.fori_loop(0, tm, issue, 0)
    for k in range(2):
        pltpu.make_async_copy(yb_hbm.at[pl.ds(0, tm), :], buf.at[k], sem).wait()
    rt = rt_ref[...]
    o_ref[...] = x2_ref[...] + rt[:, 2:3] * buf[0] + rt[:, 3:4] * buf[1]


def _combine(dest, x2, rt, yb, tm):
    T = x2.shape[0]
    nt = T // tm
    return pl.pallas_call(
        _combine_kernel,
        grid=(nt,),
        in_specs=[
            pl.BlockSpec((1, 1, 2 * tm), lambda i: (i, 0, 0), memory_space=pltpu.SMEM),
            pl.BlockSpec((tm, D_MODEL), lambda i: (i, 0)),
            pl.BlockSpec((tm, LANES), lambda i: (i, 0)),
            pl.BlockSpec(memory_space=pl.ANY),
        ],
        out_specs=pl.BlockSpec((tm, D_MODEL), lambda i: (i, 0)),
        out_shape=jax.ShapeDtypeStruct((T, D_MODEL), F32),
        scratch_shapes=[
            pltpu.VMEM((2, tm, D_MODEL), F32),
            pltpu.SemaphoreType.DMA(()),
        ],
        compiler_params=pltpu.CompilerParams(
            dimension_semantics=("arbitrary",), vmem_limit_bytes=VMEM_LIMIT),
        name="combine",
    )(dest, x2, rt, yb)


def _hier_moe(x2, hn, rt, cnt, wg_bf, wu_bf, wd_bf, tm):
    T = x2.shape[0]
    counts = cnt[0, ROUTER_COL0:ROUTER_COL0 + N_EXPERTS].astype(jnp.int32)
    padded = (counts + MOE_BLOCK - 1) // MOE_BLOCK * MOE_BLOCK
    pad_end = jnp.cumsum(padded)
    pad_start = pad_end - padded
    eid = rt[:, 0:2].astype(jnp.int32)
    rank = rt[:, 4:6].astype(jnp.int32)
    dest = pad_start[eid] + rank
    n_pairs = 2 * T
    n_blocks = (n_pairs + N_EXPERTS * (MOE_BLOCK - 1)) // MOE_BLOCK + 1
    rows = n_blocks * MOE_BLOCK
    tok = jnp.broadcast_to(jnp.arange(T, dtype=jnp.int32)[:, None], (T, 2))
    row_tok = jnp.zeros((rows,), jnp.int32).at[dest.reshape(-1)].set(tok.reshape(-1))
    block_e = jnp.minimum(
        jnp.searchsorted(pad_end, jnp.arange(n_blocks, dtype=jnp.int32) * MOE_BLOCK,
                         side="right"), N_EXPERTS - 1).astype(jnp.int32)
    n_used = (pad_end[-1] // MOE_BLOCK).astype(jnp.int32).reshape(1)
    yb = _moe(block_e, n_used, row_tok, hn, wg_bf, wu_bf, wd_bf)
    nt = T // tm
    dest_t = dest.reshape(nt, tm, 2).transpose(0, 2, 1).reshape(nt, 1, 2 * tm)
    return _combine(dest_t, x2, rt, yb, tm)


def _rope_table(pos):
    half = ROPE_DIM // 2
    inv = ROPE_THETA ** (-jnp.arange(0, ROPE_DIM, 2, dtype=F32) / ROPE_DIM)
    ang = pos.astype(F32)[:, None] * inv[None, :]
    cos, sin = jnp.cos(ang), jnp.sin(ang)
    L = pos.shape[0]
    pad = jnp.zeros((L, HEAD_DIM - ROPE_DIM), F32)
    zero = jnp.zeros((L, half), F32)
    c64 = jnp.concatenate([cos, cos, pad + 1.0], axis=1)
    lo64 = jnp.concatenate([-sin, zero, pad], axis=1)
    hi64 = jnp.concatenate([zero, sin, pad], axis=1)
    two = lambda t: jnp.concatenate([t, t], axis=1)
    return jnp.concatenate([two(c64), two(lo64), two(hi64)], axis=1)


def _layer(x, pos_rope, kctx_prev, vctx_prev, h0r, h0i, mk, mv, wp, sp, ew, *,
           tm_in, tq, ssm_l, tm_mid, tm_comb, mask_context):
    B, S, _ = x.shape
    T = B * S
    q, k, v, u = _in_proj(x.reshape(T, D_MODEL), wp["gmix"], wp["win"], wp["gq"],
                          wp["gk"], pos_rope, tm_in)
    k3 = k.reshape(B, S, KV_WIDTH)
    v3 = v.reshape(B, S, KV_WIDTH)
    kctx = jnp.concatenate([kctx_prev, k3], axis=1)
    vctx = jnp.concatenate([vctx_prev, v3], axis=1)
    att = _swa(wp["sink"], q.reshape(B, S, ATT_WIDTH), kctx, vctx, tq, mask_context)
    ssm, hr, hi = _ssm(u.reshape(B, S, SSM_WIDTH), h0r, h0i, sp, ssm_l)
    x2, hn, rt, cnt = _mid(x, att, ssm, mk, mv, wp, tm_mid)
    y = _hier_moe(x2.reshape(T, D_MODEL), hn.reshape(T, D_MODEL),
                  rt.reshape(T, LANES), cnt, *ew, tm_comb)
    return y.reshape(B, S, D_MODEL), k3, v3, hr, hi


def kernel(x_prompt, x_sample, cache_attn_k, cache_attn_v, state_ssm_re, state_ssm_im, cache_mem_k, cache_mem_v, mem_prompt, norm_mix, w_in, q_norm, k_norm, attn_sink, ssm_lambda_re, ssm_lambda_im, ssm_log_dt, ssm_b_re, ssm_b_im, ssm_c_re, ssm_c_im, ssm_d, ssm_w_glu, ssm_b_glu, norm_attn_out, norm_ssm_out, w_out, norm_cross, norm_mem, w_cq, w_ck, w_cv, cq_norm, ck_norm, w_co, norm_ffn, w_router_group, b_router_group, w_router_expert, b_router_expert, w_e_gate, w_e_up, w_e_down):
    depth = norm_mix.shape[0]
    Bp, Lp, _ = x_prompt.shape
    Bs, Ls, _ = x_sample.shape
    yp, ys = x_prompt, x_sample
    rope_p = _rope_table(jnp.arange(Lp, dtype=jnp.int32))
    rope_s = jnp.tile(_rope_table(PAST_LEN + jnp.arange(Ls, dtype=jnp.int32)), (Bs, 1))
    outs = [[] for _ in range(10)]
    for l in range(depth):
        row = lambda a: a[l].astype(F32).reshape(1, -1)
        w_r = jnp.zeros((D_MODEL, LANES), F32)
        w_r = w_r.at[:, 0:N_EXPERT_GROUPS].set(w_router_group[l].astype(F32))
        w_r = w_r.at[:, ROUTER_COL0:ROUTER_COL0 + N_EXPERTS].set(w_router_expert[l].astype(F32))
        w_r_hi = w_r.astype(BF16)
        w_r_lo = (w_r - w_r_hi.astype(F32)).astype(BF16)
        b_r = jnp.zeros((1, LANES), F32)
        b_r = b_r.at[0, 0:N_EXPERT_GROUPS].set(b_router_group[l].astype(F32))
        b_r = b_r.at[0, ROUTER_COL0:ROUTER_COL0 + N_EXPERTS].set(b_router_expert[l].astype(F32))
        wp = {
            "gmix": row(norm_mix), "win": w_in[l].astype(BF16),
            "gq": jnp.tile(row(q_norm), (1, LANES // HEAD_DIM)),
            "gk": jnp.tile(row(k_norm), (1, LANES // HEAD_DIM)),
            "sink": attn_sink[l].astype(F32),
            "gao": row(norm_attn_out), "gso": row(norm_ssm_out),
            "wout": w_out[l].astype(BF16), "gx": row(norm_cross),
            "wcq": w_cq[l].astype(BF16), "gcq": row(cq_norm),
            "wco": w_co[l].astype(BF16), "gffn": row(norm_ffn),
            "wr": jnp.concatenate([w_r_hi, w_r_lo], axis=1), "br": b_r,
        }
        sp = _ssm_params(ssm_lambda_re[l], ssm_lambda_im[l], ssm_log_dt[l], ssm_b_re[l],
                         ssm_b_im[l], ssm_c_re[l], ssm_c_im[l], ssm_d[l], ssm_w_glu[l],
                         ssm_b_glu[l])
        ew = (w_e_gate[l].astype(BF16), w_e_up[l].astype(BF16), w_e_down[l].astype(BF16))

        w_ckv = jnp.concatenate([w_ck[l], w_cv[l]], axis=1).astype(BF16)
        mkp, mvp = _memkv(mem_prompt.reshape(Bp * N_MEM, D_MODEL), row(norm_mem), w_ckv,
                          row(ck_norm), 512)
        mkp = mkp.reshape(Bp, N_MEM, CA_WIDTH)
        mvp = mvp.reshape(Bp, N_MEM, CA_WIDTH)

        zctx = jnp.zeros((Bp, WINDOW, KV_WIDTH), F32)
        zst = jnp.zeros((Bp, SSM_COLS), F32)
        yp, kp, vp, hpr, hpi = _layer(
            yp, rope_p, zctx, zctx, zst, zst, mkp, mvp, wp, sp, ew,
            tm_in=512, tq=256, ssm_l=64, tm_mid=256, tm_comb=256, mask_context=True)
        ys, kn, vn, hsr, hsi = _layer(
            ys, rope_s, cache_attn_k[l].reshape(Bs, WINDOW, KV_WIDTH).astype(F32),
            cache_attn_v[l].reshape(Bs, WINDOW, KV_WIDTH).astype(F32),
            state_ssm_re[l].astype(F32).reshape(Bs, SSM_COLS),
            state_ssm_im[l].astype(F32).reshape(Bs, SSM_COLS),
            cache_mem_k[l].astype(F32).reshape(Bs, N_MEM, CA_WIDTH),
            cache_mem_v[l].astype(F32).reshape(Bs, N_MEM, CA_WIDTH), wp, sp, ew,
            tm_in=512, tq=CHUNK, ssm_l=Ls, tm_mid=Ls, tm_comb=256, mask_context=False)

        sg = (N_SSM_GROUPS, SSM_STATE)
        kvs = (N_KV_HEADS, HEAD_DIM)
        vals = (kp[:, Lp - WINDOW:].reshape(Bp, WINDOW, *kvs),
                vp[:, Lp - WINDOW:].reshape(Bp, WINDOW, *kvs),
                hpr.reshape(Bp, *sg), hpi.reshape(Bp, *sg),
                mkp.reshape(Bp, N_MEM, CA_HEADS, CA_HEAD_DIM),
                mvp.reshape(Bp, N_MEM, CA_HEADS, CA_HEAD_DIM),
                kn.reshape(Bs, Ls, *kvs), vn.reshape(Bs, Ls, *kvs),
                hsr.reshape(Bs, *sg), hsi.reshape(Bs, *sg))
        for lst, val in zip(outs, vals):
            lst.append(val)
    return (yp, ys) + tuple(jnp.stack(lst) for lst in outs)
```

```python
import functools
import math

import jax
import jax.numpy as jnp
from jax import lax
from jax.experimental import pallas as pl
from jax.experimental.pallas import tpu as pltpu

F32 = jnp.float32
BF16 = jnp.bfloat16

D_MODEL = 1024
CHUNK = 64
N_Q_HEADS = 8
N_KV_HEADS = 2
GQA = N_Q_HEADS // N_KV_HEADS
HEAD_DIM = 64
WINDOW = 128
BAND = WINDOW + CHUNK
ROPE_DIM = HEAD_DIM // 4
ROPE_THETA = 500000.0
ATT_WIDTH = N_Q_HEADS * HEAD_DIM
KV_WIDTH = N_KV_HEADS * HEAD_DIM
SSM_GROUP = 16
SSM_WIDTH = D_MODEL // 2
N_SSM_GROUPS = SSM_WIDTH // SSM_GROUP
SSM_STATE = 64
SSM_COLS = N_SSM_GROUPS * SSM_STATE
IN_WIDTH = ATT_WIDTH + 2 * KV_WIDTH + SSM_WIDTH
N_MEM = 256
CA_HEADS = 4
CA_HEAD_DIM = 128
CA_WIDTH = CA_HEADS * CA_HEAD_DIM
N_EXPERT_GROUPS = 4
EXPERTS_PER_GROUP = 8
N_EXPERTS = N_EXPERT_GROUPS * EXPERTS_PER_GROUP
D_EXPERT = 512
MOE_BLOCK = 256
EPS = 1e-6
NEG = -1e30
PAST_LEN = 4096

LANES = 128
ROUTER_COL0 = N_EXPERT_GROUPS
VMEM_LIMIT = 48 * 1024 * 1024


def _rms(x, g):
    ms = jnp.mean(x * x, axis=-1, keepdims=True)
    return (x * lax.rsqrt(ms + EPS)) * g


def _mm(a, b):
    return jnp.dot(a, b, preferred_element_type=F32)


def _in_proj_kernel(x_ref, g_ref, w_ref, gq_ref, gk_ref, rope_ref,
                    q_ref, k_ref, v_ref, u_ref):
    tm = x_ref.shape[1]
    h = _rms(x_ref[0], g_ref[...])
    hin = _mm(h.astype(BF16), w_ref[...])
    rope = rope_ref[...]
    cos = rope[:, 0:LANES]
    sin_lo = rope[:, LANES:2 * LANES]
    sin_hi = rope[:, 2 * LANES:3 * LANES]
    lane = lax.broadcasted_iota(jnp.int32, (tm, LANES), 1)
    left = lane < HEAD_DIM

    def norm_rope(z, g):
        sq = z * z
        lsum = jnp.sum(jnp.where(left, sq, 0.0), axis=-1, keepdims=True)
        rsum = jnp.sum(jnp.where(left, 0.0, sq), axis=-1, keepdims=True)
        ms = jnp.where(left, lsum, rsum) * (1.0 / HEAD_DIM)
        zn = (z * lax.rsqrt(ms + EPS)) * g
        half = ROPE_DIM // 2
        return (zn * cos + pltpu.roll(zn, LANES - half, 1) * sin_lo
                + pltpu.roll(zn, half, 1) * sin_hi)

    for j in range(ATT_WIDTH // LANES):
        sl = slice(j * LANES, (j + 1) * LANES)
        q_ref[0, :, sl] = norm_rope(hin[:, sl], gq_ref[...])
    k_ref[0] = norm_rope(hin[:, ATT_WIDTH:ATT_WIDTH + KV_WIDTH], gk_ref[...])
    v_ref[0] = hin[:, ATT_WIDTH + KV_WIDTH:ATT_WIDTH + 2 * KV_WIDTH]
    u_ref[...] = hin[:, ATT_WIDTH + 2 * KV_WIDTH:]


def _in_proj(x, g, w_bf, gq, gk, rope, tm):
    B, S, _ = x.shape
    full = lambda b, i: (0, 0)
    tile = lambda w: pl.BlockSpec((1, tm, w), lambda b, i: (b, i, 0))
    return pl.pallas_call(
        _in_proj_kernel,
        grid=(B, S // tm),
        in_specs=[
            tile(D_MODEL),
            pl.BlockSpec((1, D_MODEL), full),
            pl.BlockSpec((D_MODEL, IN_WIDTH), full),
            pl.BlockSpec((1, LANES), full),
            pl.BlockSpec((1, LANES), full),
            pl.BlockSpec((tm, 3 * LANES), lambda b, i: (i, 0)),
        ],
        out_specs=[
            tile(ATT_WIDTH), tile(KV_WIDTH), tile(KV_WIDTH),
            pl.BlockSpec((tm, SSM_WIDTH), lambda b, i: (i, b)),
        ],
        out_shape=[
            jax.ShapeDtypeStruct((B, S, ATT_WIDTH), F32),
            jax.ShapeDtypeStruct((B, S, KV_WIDTH), F32),
            jax.ShapeDtypeStruct((B, S, KV_WIDTH), F32),
            jax.ShapeDtypeStruct((S, B * SSM_WIDTH), F32),
        ],
        compiler_params=pltpu.CompilerParams(
            dimension_semantics=("arbitrary", "arbitrary"),
            vmem_limit_bytes=VMEM_LIMIT),
        name="in_proj",
    )(x, g, w_bf, gq, gk, rope)


def _swa_kernel(sink_ref, q_ref, k_ref, v_ref, o_ref, *, mask_context):
    tq = q_ref.shape[1]
    i = pl.program_id(1)
    nch = tq // CHUNK
    for c in range(nch):
        start = pl.multiple_of((i * nch + c) * CHUNK, CHUNK)
        kb = k_ref[0, pl.ds(start, BAND), :]
        vb = v_ref[0, pl.ds(start, BAND), :]
        qc = q_ref[0, c * CHUNK:(c + 1) * CHUNK, :]
        if mask_context:
            kidx = start + lax.broadcasted_iota(jnp.int32, (1, BAND), 1)
            valid = kidx >= WINDOW
        outs = []
        for kvh in range(N_KV_HEADS):
            hs = slice(kvh * HEAD_DIM, (kvh + 1) * HEAD_DIM)
            kh = kb[:, hs].astype(BF16)
            vh = vb[:, hs].astype(BF16)
            heads = [kvh * GQA + j for j in range(GQA)]
            qs = jnp.concatenate(
                [qc[:, hq * HEAD_DIM:(hq + 1) * HEAD_DIM] for hq in heads], axis=0)
            s = lax.dot_general(qs.astype(BF16), kh, (((1,), (1,)), ((), ())),
                                preferred_element_type=F32) * (HEAD_DIM ** -0.5)
            if mask_context:
                s = jnp.where(valid, s, NEG)
            sk = jnp.concatenate(
                [jnp.full((CHUNK, 1), sink_ref[hq], F32) for hq in heads], axis=0)
            m = jnp.maximum(jnp.max(s, axis=-1, keepdims=True), sk)
            p = jnp.exp(s - m)
            den = jnp.sum(p, axis=-1, keepdims=True) + jnp.exp(sk - m)
            o = _mm(p.astype(BF16), vh) / den
            outs += [o[j * CHUNK:(j + 1) * CHUNK] for j in range(GQA)]
        o_ref[0, c * CHUNK:(c + 1) * CHUNK, :] = jnp.concatenate(outs, axis=1)


def _swa(sink, q, kctx, vctx, tq, mask_context):
    B, Sq, _ = q.shape
    Sk = kctx.shape[1]
    return pl.pallas_call(
        functools.partial(_swa_kernel, mask_context=mask_context),
        grid=(B, Sq // tq),
        in_specs=[
            pl.BlockSpec(memory_space=pltpu.SMEM),
            pl.BlockSpec((1, tq, ATT_WIDTH), lambda b, i: (b, i, 0)),
            pl.BlockSpec((1, Sk, KV_WIDTH), lambda b, i: (b, 0, 0)),
            pl.BlockSpec((1, Sk, KV_WIDTH), lambda b, i: (b, 0, 0)),
        ],
        out_specs=pl.BlockSpec((1, tq, ATT_WIDTH), lambda b, i: (b, i, 0)),
        out_shape=jax.ShapeDtypeStruct((B, Sq, ATT_WIDTH), F32),
        compiler_params=pltpu.CompilerParams(
            dimension_semantics=("arbitrary", "arbitrary"),
            vmem_limit_bytes=VMEM_LIMIT),
        name="swa",
    )(sink, q, kctx, vctx)


def _ssm_kernel(u_ref, h0r_ref, h0i_ref, lam_ref, bre_ref, bim_ref, cre_ref, cim_ref,
                d_ref, wglu_ref, bglu_ref,
                y_ref, hr_out, hi_out, sr, si, hr_s, hi_s):
    L, B, _ = u_ref.shape
    rows = L * B
    half_w = SSM_WIDTH // 2
    half_c = SSM_COLS // 2

    @pl.when(pl.program_id(0) == 0)
    def _():
        hr_s[...] = h0r_ref[...]
        hi_s[...] = h0i_ref[...]

    u = u_ref[...].reshape(rows, SSM_WIDTH)
    ub = u.astype(BF16)
    for hf in range(2):
        uh = ub[:, hf * half_w:(hf + 1) * half_w]
        sr[:, hf * half_c:(hf + 1) * half_c] = _mm(uh, bre_ref[hf])
        si[:, hf * half_c:(hf + 1) * half_c] = _mm(uh, bim_ref[hf])

    cw = 4 * LANES
    for cc in range(SSM_COLS // cw):
        cols = slice(cc * cw, (cc + 1) * cw)
        lr = jnp.broadcast_to(lam_ref[0:1, cols], (B, cw))
        li = jnp.broadcast_to(lam_ref[1:2, cols], (B, cw))

        def body(t, carry):
            hr, hi = carry
            at_t = pl.ds(pl.multiple_of(t * B, B), B)
            nr = lr * hr - li * hi + sr[at_t, cols]
            ni = lr * hi + li * hr + si[at_t, cols]
            sr[at_t, cols] = nr
            si[at_t, cols] = ni
            return nr, ni

        hr, hi = lax.fori_loop(0, L, body, (hr_s[:, cols], hi_s[:, cols]), unroll=2)
        hr_s[:, cols] = hr
        hi_s[:, cols] = hi

    ys = []
    for hf in range(2):
        cs = slice(hf * half_c, (hf + 1) * half_c)
        ys.append(_mm(sr[:, cs].astype(BF16), cre_ref[hf])
                  + _mm(si[:, cs].astype(BF16), cim_ref[hf]))
    y = jnp.concatenate(ys, axis=1) + d_ref[...] * u
    g = 0.5 * y * (1.0 + jnp.tanh(math.sqrt(2.0 / math.pi) * (y + 0.044715 * (y * y * y))))
    gb = g.astype(BF16)
    z = jnp.concatenate(
        [_mm(gb[:, hf * half_w:(hf + 1) * half_w], wglu_ref[hf]) for hf in range(2)],
        axis=1) + bglu_ref[...]
    out = g * (1.0 / (1.0 + jnp.exp(-z)))
    y_ref[...] = out.reshape(L, B, SSM_WIDTH)
    hr_out[...] = hr_s[...]
    hi_out[...] = hi_s[...]


def _ssm(u, h0r, h0i, sp, L):
    S, B, _ = u.shape
    c2 = lambda i: (0, 0)
    c3 = lambda i: (0, 0, 0)
    return pl.pallas_call(
        _ssm_kernel,
        grid=(S // L,),
        in_specs=[
            pl.BlockSpec((L, B, SSM_WIDTH), lambda i: (i, 0, 0)),
            pl.BlockSpec((B, SSM_COLS), c2),
            pl.BlockSpec((B, SSM_COLS), c2),
            pl.BlockSpec((2, SSM_COLS), c2),
            pl.BlockSpec((2, SSM_WIDTH // 2, SSM_COLS // 2), c3),
            pl.BlockSpec((2, SSM_WIDTH // 2, SSM_COLS // 2), c3),
            pl.BlockSpec((2, SSM_COLS // 2, SSM_WIDTH // 2), c3),
            pl.BlockSpec((2, SSM_COLS // 2, SSM_WIDTH // 2), c3),
            pl.BlockSpec((1, SSM_WIDTH), c2),
            pl.BlockSpec((2, SSM_WIDTH // 2, SSM_WIDTH // 2), c3),
            pl.BlockSpec((1, SSM_WIDTH), c2),
        ],
        out_specs=[
            pl.BlockSpec((L, B, SSM_WIDTH), lambda i: (i, 0, 0)),
            pl.BlockSpec((B, SSM_COLS), c2),
            pl.BlockSpec((B, SSM_COLS), c2),
        ],
        out_shape=[
            jax.ShapeDtypeStruct((S, B, SSM_WIDTH), F32),
            jax.ShapeDtypeStruct((B, SSM_COLS), F32),
            jax.ShapeDtypeStruct((B, SSM_COLS), F32),
        ],
        scratch_shapes=[
            pltpu.VMEM((L * B, SSM_COLS), F32),
            pltpu.VMEM((L * B, SSM_COLS), F32),
            pltpu.VMEM((B, SSM_COLS), F32),
            pltpu.VMEM((B, SSM_COLS), F32),
        ],
        compiler_params=pltpu.CompilerParams(
            dimension_semantics=("arbitrary",), vmem_limit_bytes=VMEM_LIMIT),
        name="ssm",
    )(u, h0r, h0i, sp["lam"], sp["bre"], sp["bim"], sp["cre"], sp["cim"],
      sp["d"], sp["wglu"], sp["bglu"])


def _block_diag(blocks):
    G, r, c = blocks.shape
    eye = jnp.eye(G, dtype=blocks.dtype)
    return jnp.einsum("grc,gh->grhc", blocks, eye).reshape(G * r, G * c)


def _ssm_params(lam_re, lam_im, log_dt, b_re, b_im, c_re, c_im, d, w_glu, b_glu):
    lam = lax.complex(lam_re.astype(F32), lam_im.astype(F32))
    dt = jnp.exp(log_dt.astype(F32))[:, None]
    lam_bar = jnp.exp(lam * dt)
    bmat = lax.complex(b_re.astype(F32), b_im.astype(F32))
    b_bar = ((lam_bar - 1.0) / lam)[..., None] * bmat
    lam2 = jnp.stack([lam_bar.real.reshape(-1), lam_bar.imag.reshape(-1)])
    bt = jnp.swapaxes(b_bar, 1, 2)
    hw, hc = SSM_WIDTH // 2, SSM_COLS // 2
    split_b = lambda m: jnp.stack([m[:hw, :hc], m[hw:, hc:]]).astype(BF16)
    split_c = lambda m: jnp.stack([m[:hc, :hw], m[hc:, hw:]]).astype(BF16)
    ct_re = jnp.swapaxes(c_re.astype(F32), 1, 2)
    ct_im = jnp.swapaxes(c_im.astype(F32), 1, 2)
    wg = _block_diag(w_glu.astype(F32))
    return {
        "lam": lam2,
        "bre": split_b(_block_diag(bt.real)),
        "bim": split_b(_block_diag(bt.imag)),
        "cre": split_c(_block_diag(ct_re)),
        "cim": split_c(_block_diag(-ct_im)),
        "d": d.astype(F32).reshape(1, SSM_WIDTH),
        "wglu": jnp.stack([wg[:hw, :hw], wg[hw:, hw:]]).astype(BF16),
        "bglu": b_glu.astype(F32).reshape(1, SSM_WIDTH),
    }


def _memkv_kernel(m_ref, g_ref, w_ref, gk_ref, k_ref, v_ref):
    m = _rms(m_ref[...], g_ref[...])
    kv = _mm(m.astype(BF16), w_ref[...])
    for h in range(CA_HEADS):
        sl = slice(h * CA_HEAD_DIM, (h + 1) * CA_HEAD_DIM)
        k_ref[:, sl] = _rms(kv[:, sl], gk_ref[...])
    v_ref[...] = kv[:, CA_WIDTH:]


def _memkv(mem2d, g, w_bf, gk, tm):
    T = mem2d.shape[0]
    full = lambda i: (0, 0)
    return pl.pallas_call(
        _memkv_kernel,
        grid=(T // tm,),
        in_specs=[
            pl.BlockSpec((tm, D_MODEL), lambda i: (i, 0)),
            pl.BlockSpec((1, D_MODEL), full),
            pl.BlockSpec((D_MODEL, 2 * CA_WIDTH), full),
            pl.BlockSpec((1, CA_HEAD_DIM), full),
        ],
        out_specs=[
            pl.BlockSpec((tm, CA_WIDTH), lambda i: (i, 0)),
            pl.BlockSpec((tm, CA_WIDTH), lambda i: (i, 0)),
        ],
        out_shape=[
            jax.ShapeDtypeStruct((T, CA_WIDTH), F32),
            jax.ShapeDtypeStruct((T, CA_WIDTH), F32),
        ],
        compiler_params=pltpu.CompilerParams(
            dimension_semantics=("arbitrary",), vmem_limit_bytes=VMEM_LIMIT),
        name="memkv",
    )(mem2d, g, w_bf, gk)


def _mid_kernel(x_ref, att_ref, ssm_ref, mk_ref, mv_ref,
                gao_ref, gso_ref, wout_ref, gx_ref, wcq_ref, gcq_ref, wco_ref,
                gffn_ref, wr_ref, br_ref,
                x2_ref, hn_ref, rt_ref, rtt_ref, cnt_ref, base_s):
    tm = x_ref.shape[1]

    @pl.when((pl.program_id(0) == 0) & (pl.program_id(1) == 0))
    def _():
        base_s[...] = jnp.zeros_like(base_s)

    a = _rms(att_ref[0], gao_ref[...]).astype(BF16)
    s = _rms(ssm_ref[...], gso_ref[...]).astype(BF16)
    x1 = (x_ref[0] + _mm(a, wout_ref[0:ATT_WIDTH, :])
          + _mm(s, wout_ref[ATT_WIDTH:, :]))

    qx = _mm(_rms(x1, gx_ref[...]).astype(BF16), wcq_ref[...])
    heads = []
    for h in range(CA_HEADS):
        sl = slice(h * CA_HEAD_DIM, (h + 1) * CA_HEAD_DIM)
        qh = _rms(qx[:, sl], gcq_ref[...]).astype(BF16)
        kh = mk_ref[0, :, sl].astype(BF16)
        vh = mv_ref[0, :, sl].astype(BF16)
        sc = lax.dot_general(qh, kh, (((1,), (1,)), ((), ())),
                             preferred_element_type=F32) * (CA_HEAD_DIM ** -0.5)
        p = jnp.exp(sc - jnp.max(sc, axis=-1, keepdims=True))
        p = p / jnp.sum(p, axis=-1, keepdims=True)
        heads.append(_mm(p.astype(BF16), vh))
    o = jnp.concatenate(heads, axis=1).astype(BF16)
    x2 = x1 + _mm(o, wco_ref[...])
    x2_ref[0] = x2

    hn = _rms(x2, gffn_ref[...])
    hn_ref[0] = hn

    h_hi = hn.astype(BF16)
    h_lo = (hn - h_hi.astype(F32)).astype(BF16)
    r1 = _mm(h_hi, wr_ref[...])
    lg = (r1[:, :LANES] + r1[:, LANES:] + _mm(h_lo, wr_ref[:, 0:LANES])
          + br_ref[...])

    col = lax.broadcasted_iota(jnp.int32, (tm, LANES), 1)
    big = jnp.int32(4 * LANES)
    gmask = col < N_EXPERT_GROUPS
    lgg = jnp.where(gmask, lg, NEG)
    mg = jnp.max(lgg, axis=-1, keepdims=True)
    grp = jnp.min(jnp.where(gmask & (lgg == mg), col, big), axis=-1, keepdims=True)
    pg_top = 1.0 / jnp.sum(jnp.where(gmask, jnp.exp(lgg - mg), 0.0), axis=-1, keepdims=True)

    ecol = col - ROUTER_COL0
    emask = ((ecol >= 0) & (ecol < N_EXPERTS)
             & (lax.shift_right_arithmetic(ecol, 3) == grp))
    le = jnp.where(emask, lg, NEG)
    m1 = jnp.max(le, axis=-1, keepdims=True)
    i1 = jnp.min(jnp.where(emask & (le == m1), col, big), axis=-1, keepdims=True)
    rest = emask & (col != i1)
    le2 = jnp.where(rest, lg, NEG)
    m2 = jnp.max(le2, axis=-1, keepdims=True)
    i2 = jnp.min(jnp.where(rest & (le2 == m2), col, big), axis=-1, keepdims=True)
    den = jnp.sum(jnp.where(emask, jnp.exp(le - m1), 0.0), axis=-1, keepdims=True)
    p1 = 1.0 / den
    p2 = jnp.exp(m2 - m1) / den
    gate1 = pg_top * p1 / (p1 + p2)
    gate2 = pg_top * p2 / (p1 + p2)

    sel1 = col == i1
    sel2 = col == i2
    oh = jnp.where(sel1 | sel2, 1.0, 0.0)
    r_i = lax.broadcasted_iota(jnp.int32, (tm, tm), 0)
    c_i = lax.broadcasted_iota(jnp.int32, (tm, tm), 1)
    tri = jnp.where(r_i > c_i, 1.0, 0.0).astype(BF16)
    tot = base_s[...] + _mm(tri, oh.astype(BF16))
    rank1 = jnp.sum(jnp.where(sel1, tot, 0.0), axis=-1, keepdims=True)
    rank2 = jnp.sum(jnp.where(sel2, tot, 0.0), axis=-1, keepdims=True)
    base_s[...] = base_s[...] + jnp.sum(oh, axis=0, keepdims=True)
    cnt_ref[...] = base_s[...]

    e1 = (i1 - ROUTER_COL0).astype(F32)
    e2 = (i2 - ROUTER_COL0).astype(F32)
    rt = jnp.zeros((tm, LANES), F32)
    for k, val in enumerate((e1, e2, gate1, gate2, rank1, rank2)):
        rt = jnp.where(col == k, val, rt)
    rt_ref[0] = rt
    rtt_ref[0] = rt.T[0:8, :]


def _mid(x, att, ssm_tm, mk, mv, wp, tm):
    B, S, _ = x.shape
    c2 = lambda b, i: (0, 0)
    tile = lambda w: pl.BlockSpec((1, tm, w), lambda b, i: (b, i, 0))
    return pl.pallas_call(
        _mid_kernel,
        grid=(B, S // tm),
        in_specs=[
            tile(D_MODEL), tile(ATT_WIDTH),
            pl.BlockSpec((tm, SSM_WIDTH), lambda b, i: (i, b)),
            pl.BlockSpec((1, N_MEM, CA_WIDTH), lambda b, i: (b, 0, 0)),
            pl.BlockSpec((1, N_MEM, CA_WIDTH), lambda b, i: (b, 0, 0)),
            pl.BlockSpec((1, ATT_WIDTH), c2),
            pl.BlockSpec((1, SSM_WIDTH), c2),
            pl.BlockSpec((ATT_WIDTH + SSM_WIDTH, D_MODEL), c2),
            pl.BlockSpec((1, D_MODEL), c2),
            pl.BlockSpec((D_MODEL, CA_WIDTH), c2),
            pl.BlockSpec((1, CA_HEAD_DIM), c2),
            pl.BlockSpec((CA_WIDTH, D_MODEL), c2),
            pl.BlockSpec((1, D_MODEL), c2),
            pl.BlockSpec((D_MODEL, 2 * LANES), c2),
            pl.BlockSpec((1, LANES), c2),
        ],
        out_specs=[
            tile(D_MODEL), tile(D_MODEL), tile(LANES),
            pl.BlockSpec((1, 8, tm), lambda b, i: (b, 0, i)),
            pl.BlockSpec((1, LANES), c2),
        ],
        out_shape=[
            jax.ShapeDtypeStruct((B, S, D_MODEL), F32),
            jax.ShapeDtypeStruct((B, S, D_MODEL), F32),
            jax.ShapeDtypeStruct((B, S, LANES), F32),
            jax.ShapeDtypeStruct((B, 8, S), F32),
            jax.ShapeDtypeStruct((1, LANES), F32),
        ],
        scratch_shapes=[pltpu.VMEM((1, LANES), F32)],
        compiler_params=pltpu.CompilerParams(
            dimension_semantics=("arbitrary", "arbitrary"),
            vmem_limit_bytes=VMEM_LIMIT),
        name="mid",
    )(x, att, ssm_tm, mk, mv, wp["gao"], wp["gso"], wp["wout"], wp["gx"], wp["wcq"],
      wp["gcq"], wp["wco"], wp["gffn"], wp["wr"], wp["br"])


def _row_gather(idx_ref, idx_base, n_rows, src_hbm, dst, sem):
    def issue(r, carry):
        pltpu.make_async_copy(src_hbm.at[pl.ds(idx_ref[0, 0, idx_base + r], 1), :],
                              dst.at[pl.ds(r, 1), :], sem).start()
        return carry

    lax.fori_loop(0, n_rows, issue, 0, unroll=8)


def _row_gather_wait(n_rows, src_hbm, dst, sem):
    pltpu.make_async_copy(src_hbm.at[pl.ds(0, n_rows), :], dst, sem).wait()


def _moe_kernel(be_ref, nu_ref, tok_ref, tok_next_ref, hn_hbm, wg_ref, wu_ref, wd_ref,
                yb_ref, xbuf, wg_s, wu_s, wd_s, sem):
    i = pl.program_id(0)
    n_used = nu_ref[0]
    slot = lax.rem(i, 2)

    @pl.when((i == 0) & (n_used > 0))
    def _():
        _row_gather(tok_ref, 0, MOE_BLOCK, hn_hbm, xbuf.at[0], sem.at[0])

    @pl.when(i + 1 < n_used)
    def _():
        _row_gather(tok_next_ref, 0, MOE_BLOCK, hn_hbm, xbuf.at[1 - slot], sem.at[1 - slot])

    @pl.when(i < n_used)
    def _():
        new_expert = (i == 0) | (be_ref[i] != be_ref[jnp.maximum(i - 1, 0)])

        @pl.when(new_expert)
        def _():
            wg_s[...] = wg_ref[0].astype(BF16)
            wu_s[...] = wu_ref[0].astype(BF16)
            wd_s[...] = wd_ref[0].astype(BF16)

        _row_gather_wait(MOE_BLOCK, hn_hbm, xbuf.at[slot], sem.at[slot])
        xe = xbuf[slot].astype(BF16)
        g = _mm(xe, wg_s[...])
        u = _mm(xe, wu_s[...])
        hmid = (g * (1.0 / (1.0 + jnp.exp(-g)))) * u
        yb_ref[...] = _mm(hmid.astype(BF16), wd_s[...])

    @pl.when(i >= n_used)
    def _():
        yb_ref[...] = jnp.zeros_like(yb_ref)


def _moe(block_e, n_used, row_tok, hn2d, w_gate, w_up, w_down):
    n_blocks = block_e.shape[0]
    tok3 = row_tok.reshape(n_blocks, 1, MOE_BLOCK)
    grid_spec = pltpu.PrefetchScalarGridSpec(
        num_scalar_prefetch=2,
        grid=(n_blocks,),
        in_specs=[
            pl.BlockSpec((1, 1, MOE_BLOCK), lambda i, be, nu: (i, 0, 0),
                         memory_space=pltpu.SMEM),
            pl.BlockSpec((1, 1, MOE_BLOCK),
                         lambda i, be, nu: (jnp.minimum(i + 1, n_blocks - 1), 0, 0),
                         memory_space=pltpu.SMEM),
            pl.BlockSpec(memory_space=pl.ANY),
            pl.BlockSpec((1, D_MODEL, D_EXPERT), lambda i, be, nu: (be[i], 0, 0)),
            pl.BlockSpec((1, D_MODEL, D_EXPERT), lambda i, be, nu: (be[i], 0, 0)),
            pl.BlockSpec((1, D_EXPERT, D_MODEL), lambda i, be, nu: (be[i], 0, 0)),
        ],
        out_specs=pl.BlockSpec((MOE_BLOCK, D_MODEL), lambda i, be, nu: (i, 0)),
        scratch_shapes=[
            pltpu.VMEM((2, MOE_BLOCK, D_MODEL), F32),
            pltpu.VMEM((D_MODEL, D_EXPERT), BF16),
            pltpu.VMEM((D_MODEL, D_EXPERT), BF16),
            pltpu.VMEM((D_EXPERT, D_MODEL), BF16),
            pltpu.SemaphoreType.DMA((2,)),
        ],
    )
    return pl.pallas_call(
        _moe_kernel,
        grid_spec=grid_spec,
        out_shape=jax.ShapeDtypeStruct((n_blocks * MOE_BLOCK, D_MODEL), F32),
        compiler_params=pltpu.CompilerParams(
            dimension_semantics=("arbitrary",), vmem_limit_bytes=VMEM_LIMIT),
        name="moe",
    )(block_e, n_used, tok3, tok3, hn2d, w_gate, w_up, w_down)


def _combine_kernel(dest_ref, dest_next_ref, x2_ref, rt_ref, yb_hbm, o_ref, buf, sem):
    tm = x2_ref.shape[0]
    i = pl.program_id(0)
    slot = lax.rem(i, 2)

    def gather(d_ref, s):
        for k in range(2):
            _row_gather(d_ref, k * tm, tm, yb_hbm, buf.at[s, k], sem.at[s])

    @pl.when(i == 0)
    def _():
        gather(dest_ref, 0)

    @pl.when(i + 1 < pl.num_programs(0))
    def _():
        gather(dest_next_ref, 1 - slot)

    for k in range(2):
        _row_gather_wait(tm, yb_hbm, buf.at[slot, k], sem.at[slot])
    rt = rt_ref[...]
    o_ref[...] = x2_ref[...] + rt[:, 2:3] * buf[slot, 0] + rt[:, 3:4] * buf[slot, 1]


def _combine(dest, x2, rt, yb, tm):
    T = x2.shape[0]
    nt = T // tm
    return pl.pallas_call(
        _combine_kernel,
        grid=(nt,),
        in_specs=[
            pl.BlockSpec((1, 1, 2 * tm), lambda i: (i, 0, 0), memory_space=pltpu.SMEM),
            pl.BlockSpec((1, 1, 2 * tm), lambda i: (jnp.minimum(i + 1, nt - 1), 0, 0),
                         memory_space=pltpu.SMEM),
            pl.BlockSpec((tm, D_MODEL), lambda i: (i, 0)),
            pl.BlockSpec((tm, LANES), lambda i: (i, 0)),
            pl.BlockSpec(memory_space=pl.ANY),
        ],
        out_specs=pl.BlockSpec((tm, D_MODEL), lambda i: (i, 0)),
        out_shape=jax.ShapeDtypeStruct((T, D_MODEL), F32),
        scratch_shapes=[
            pltpu.VMEM((2, 2, tm, D_MODEL), F32),
            pltpu.SemaphoreType.DMA((2,)),
        ],
        compiler_params=pltpu.CompilerParams(
            dimension_semantics=("arbitrary",), vmem_limit_bytes=VMEM_LIMIT),
        name="combine",
    )(dest, dest, x2, rt, yb)


def _hier_moe(x2, hn, rt, rtt, cnt, w_gate, w_up, w_down, tm):
    T = x2.shape[0]
    counts = cnt[0, ROUTER_COL0:ROUTER_COL0 + N_EXPERTS].astype(jnp.int32)
    padded = (counts + MOE_BLOCK - 1) // MOE_BLOCK * MOE_BLOCK
    pad_end = jnp.cumsum(padded)
    pad_start = pad_end - padded
    flat = lambda a: jnp.swapaxes(a, 0, 1).reshape(a.shape[1], T)
    eid = flat(rtt[:, 0:2, :]).astype(jnp.int32)
    rank = flat(rtt[:, 4:6, :]).astype(jnp.int32)
    dest = pad_start[eid] + rank
    n_blocks = (2 * T + N_EXPERTS * (MOE_BLOCK - 1)) // MOE_BLOCK + 1
    rows = n_blocks * MOE_BLOCK
    tok = jnp.broadcast_to(jnp.arange(T, dtype=jnp.int32)[None, :], (2, T))
    row_tok = jnp.zeros((rows,), jnp.int32).at[dest.reshape(-1)].set(tok.reshape(-1))
    blk_row0 = jnp.arange(n_blocks, dtype=jnp.int32) * MOE_BLOCK
    block_e = jnp.minimum(
        jnp.sum((pad_end[None, :] <= blk_row0[:, None]).astype(jnp.int32), axis=1),
        N_EXPERTS - 1)
    n_used = (pad_end[-1] // MOE_BLOCK).astype(jnp.int32).reshape(1)
    yb = _moe(block_e, n_used, row_tok, hn, w_gate, w_up, w_down)
    nt = T // tm
    dest_t = dest.reshape(2, nt, tm).transpose(1, 0, 2).reshape(nt, 1, 2 * tm)
    return _combine(dest_t, x2, rt, yb, tm)


def _rope_table(pos):
    half = ROPE_DIM // 2
    inv = ROPE_THETA ** (-jnp.arange(0, ROPE_DIM, 2, dtype=F32) / ROPE_DIM)
    ang = pos.astype(F32)[:, None] * inv[None, :]
    cos, sin = jnp.cos(ang), jnp.sin(ang)
    L = pos.shape[0]
    pad = jnp.zeros((L, HEAD_DIM - ROPE_DIM), F32)
    zero = jnp.zeros((L, half), F32)
    c64 = jnp.concatenate([cos, cos, pad + 1.0], axis=1)
    lo64 = jnp.concatenate([-sin, zero, pad], axis=1)
    hi64 = jnp.concatenate([zero, sin, pad], axis=1)
    two = lambda t: jnp.concatenate([t, t], axis=1)
    return jnp.concatenate([two(c64), two(lo64), two(hi64)], axis=1)


def _layer(x, pos_rope, kctx_prev, vctx_prev, h0r, h0i, mk, mv, wp, sp, ew, *,
           tm_in, tq, ssm_l, tm_mid, tm_comb, mask_context):
    B, S, _ = x.shape
    T = B * S
    q, k3, v3, u_tm = _in_proj(x, wp["gmix"], wp["win"], wp["gq"], wp["gk"], pos_rope, tm_in)
    kctx = jnp.concatenate([kctx_prev, k3], axis=1)
    vctx = jnp.concatenate([vctx_prev, v3], axis=1)
    att = _swa(wp["sink"], q, kctx, vctx, tq, mask_context)
    ssm_tm, hr, hi = _ssm(u_tm.reshape(S, B, SSM_WIDTH), h0r, h0i, sp, ssm_l)
    x2, hn, rt, rtt, cnt = _mid(x, att, ssm_tm.reshape(S, B * SSM_WIDTH), mk, mv, wp, tm_mid)
    y = _hier_moe(x2.reshape(T, D_MODEL), hn.reshape(T, D_MODEL),
                  rt.reshape(T, LANES), rtt, cnt, *ew, tm_comb)
    return y.reshape(B, S, D_MODEL), k3, v3, hr, hi


def kernel(x_prompt, x_sample, cache_attn_k, cache_attn_v, state_ssm_re, state_ssm_im, cache_mem_k, cache_mem_v, mem_prompt, norm_mix, w_in, q_norm, k_norm, attn_sink, ssm_lambda_re, ssm_lambda_im, ssm_log_dt, ssm_b_re, ssm_b_im, ssm_c_re, ssm_c_im, ssm_d, ssm_w_glu, ssm_b_glu, norm_attn_out, norm_ssm_out, w_out, norm_cross, norm_mem, w_cq, w_ck, w_cv, cq_norm, ck_norm, w_co, norm_ffn, w_router_group, b_router_group, w_router_expert, b_router_expert, w_e_gate, w_e_up, w_e_down):
    depth = norm_mix.shape[0]
    Bp, Lp, _ = x_prompt.shape
    Bs, Ls, _ = x_sample.shape
    yp, ys = x_prompt, x_sample
    rope_p = _rope_table(jnp.arange(Lp, dtype=jnp.int32))
    rope_s = _rope_table(PAST_LEN + jnp.arange(Ls, dtype=jnp.int32))
    outs = [[] for _ in range(10)]
    n_router = N_EXPERT_GROUPS + N_EXPERTS
    for l in range(depth):
        row = lambda a: a[l].astype(F32).reshape(1, -1)
        w_r = jnp.pad(jnp.concatenate([w_router_group[l], w_router_expert[l]], axis=1).astype(F32),
                      ((0, 0), (0, LANES - n_router)))
        w_r_hi = w_r.astype(BF16)
        w_r_lo = (w_r - w_r_hi.astype(F32)).astype(BF16)
        b_r = jnp.pad(jnp.concatenate([b_router_group[l], b_router_expert[l]]).astype(F32),
                      (0, LANES - n_router)).reshape(1, LANES)
        wp = {
            "gmix": row(norm_mix), "win": w_in[l].astype(BF16),
            "gq": jnp.tile(row(q_norm), (1, LANES // HEAD_DIM)),
            "gk": jnp.tile(row(k_norm), (1, LANES // HEAD_DIM)),
            "sink": attn_sink[l].astype(F32),
            "gao": row(norm_attn_out), "gso": row(norm_ssm_out),
            "wout": w_out[l].astype(BF16), "gx": row(norm_cross),
            "wcq": w_cq[l].astype(BF16), "gcq": row(cq_norm),
            "wco": w_co[l].astype(BF16), "gffn": row(norm_ffn),
            "wr": jnp.concatenate([w_r_hi, w_r_lo], axis=1), "br": b_r,
        }
        sp = _ssm_params(ssm_lambda_re[l], ssm_lambda_im[l], ssm_log_dt[l], ssm_b_re[l],
                         ssm_b_im[l], ssm_c_re[l], ssm_c_im[l], ssm_d[l], ssm_w_glu[l],
                         ssm_b_glu[l])
        ew = (w_e_gate[l].astype(F32), w_e_up[l].astype(F32), w_e_down[l].astype(F32))

        w_ckv = jnp.concatenate([w_ck[l], w_cv[l]], axis=1).astype(BF16)
        mkp, mvp = _memkv(mem_prompt.reshape(Bp * N_MEM, D_MODEL), row(norm_mem), w_ckv,
                          row(ck_norm), 512)
        mkp = mkp.reshape(Bp, N_MEM, CA_WIDTH)
        mvp = mvp.reshape(Bp, N_MEM, CA_WIDTH)

        zctx = jnp.zeros((Bp, WINDOW, KV_WIDTH), F32)
        zst = jnp.zeros((Bp, SSM_COLS), F32)
        yp, kp, vp, hpr, hpi = _layer(
            yp, rope_p, zctx, zctx, zst, zst, mkp, mvp, wp, sp, ew,
            tm_in=512, tq=256, ssm_l=64, tm_mid=256, tm_comb=256, mask_context=True)
        ys, kn, vn, hsr, hsi = _layer(
            ys, rope_s, cache_attn_k[l].reshape(Bs, WINDOW, KV_WIDTH).astype(F32),
            cache_attn_v[l].reshape(Bs, WINDOW, KV_WIDTH).astype(F32),
            state_ssm_re[l].astype(F32).reshape(Bs, SSM_COLS),
            state_ssm_im[l].astype(F32).reshape(Bs, SSM_COLS),
            cache_mem_k[l].astype(F32).reshape(Bs, N_MEM, CA_WIDTH),
            cache_mem_v[l].astype(F32).reshape(Bs, N_MEM, CA_WIDTH), wp, sp, ew,
            tm_in=Ls, tq=CHUNK, ssm_l=Ls, tm_mid=Ls, tm_comb=256, mask_context=False)

        sg = (N_SSM_GROUPS, SSM_STATE)
        kvs = (N_KV_HEADS, HEAD_DIM)
        vals = (kp[:, Lp - WINDOW:].reshape(Bp, WINDOW, *kvs),
                vp[:, Lp - WINDOW:].reshape(Bp, WINDOW, *kvs),
                hpr.reshape(Bp, *sg), hpi.reshape(Bp, *sg),
                mkp.reshape(Bp, N_MEM, CA_HEADS, CA_HEAD_DIM),
                mvp.reshape(Bp, N_MEM, CA_HEADS, CA_HEAD_DIM),
                kn.reshape(Bs, Ls, *kvs), vn.reshape(Bs, Ls, *kvs),
                hsr.reshape(Bs, *sg), hsi.reshape(Bs, *sg))
        for lst, val in zip(outs, vals):
            lst.append(val)
    return (yp, ys) + tuple(jnp.stack(lst) for lst in outs)
```

```python
import functools
import math

import jax
import jax.numpy as jnp
from jax import lax
from jax.experimental import pallas as pl
from jax.experimental.pallas import tpu as pltpu

F32 = jnp.float32
BF16 = jnp.bfloat16

D_MODEL = 1024
CHUNK = 64
N_Q_HEADS = 8
N_KV_HEADS = 2
GQA = N_Q_HEADS // N_KV_HEADS
HEAD_DIM = 64
WINDOW = 128
BAND = WINDOW + CHUNK
ROPE_DIM = HEAD_DIM // 4
ROPE_THETA = 500000.0
ATT_WIDTH = N_Q_HEADS * HEAD_DIM
KV_WIDTH = N_KV_HEADS * HEAD_DIM
SSM_GROUP = 16
SSM_WIDTH = D_MODEL // 2
N_SSM_GROUPS = SSM_WIDTH // SSM_GROUP
SSM_STATE = 64
SSM_COLS = N_SSM_GROUPS * SSM_STATE
IN_WIDTH = ATT_WIDTH + 2 * KV_WIDTH + SSM_WIDTH
N_MEM = 256
CA_HEADS = 4
CA_HEAD_DIM = 128
CA_WIDTH = CA_HEADS * CA_HEAD_DIM
N_EXPERT_GROUPS = 4
EXPERTS_PER_GROUP = 8
N_EXPERTS = N_EXPERT_GROUPS * EXPERTS_PER_GROUP
D_EXPERT = 512
MOE_BLOCK = 256
EPS = 1e-6
NEG = -1e30
PAST_LEN = 4096

LANES = 128
ROUTER_COL0 = N_EXPERT_GROUPS
VMEM_LIMIT = 48 * 1024 * 1024


def _rms(x, g):
    ms = jnp.mean(x * x, axis=-1, keepdims=True)
    return (x * lax.rsqrt(ms + EPS)) * g


def _mm(a, b):
    return jnp.dot(a, b, preferred_element_type=F32)


def _in_proj_kernel(x_ref, g_ref, w_ref, gq_ref, gk_ref, rope_ref,
                    q_ref, k_ref, v_ref, u_ref):
    tm = x_ref.shape[1]
    h = _rms(x_ref[0], g_ref[...])
    hin = _mm(h.astype(BF16), w_ref[...])
    rope = rope_ref[...]
    cos = rope[:, 0:LANES]
    sin_lo = rope[:, LANES:2 * LANES]
    sin_hi = rope[:, 2 * LANES:3 * LANES]
    lane = lax.broadcasted_iota(jnp.int32, (tm, LANES), 1)
    left = lane < HEAD_DIM

    def norm_rope(z, g):
        sq = z * z
        lsum = jnp.sum(jnp.where(left, sq, 0.0), axis=-1, keepdims=True)
        rsum = jnp.sum(jnp.where(left, 0.0, sq), axis=-1, keepdims=True)
        ms = jnp.where(left, lsum, rsum) * (1.0 / HEAD_DIM)
        zn = (z * lax.rsqrt(ms + EPS)) * g
        half = ROPE_DIM // 2
        return (zn * cos + pltpu.roll(zn, LANES - half, 1) * sin_lo
                + pltpu.roll(zn, half, 1) * sin_hi)

    for j in range(ATT_WIDTH // LANES):
        sl = slice(j * LANES, (j + 1) * LANES)
        q_ref[0, :, sl] = norm_rope(hin[:, sl], gq_ref[...])
    k_ref[0] = norm_rope(hin[:, ATT_WIDTH:ATT_WIDTH + KV_WIDTH], gk_ref[...])
    v_ref[0] = hin[:, ATT_WIDTH + KV_WIDTH:ATT_WIDTH + 2 * KV_WIDTH]
    u_ref[...] = hin[:, ATT_WIDTH + 2 * KV_WIDTH:]


def _in_proj(x, g, w_bf, gq, gk, rope, tm):
    B, S, _ = x.shape
    full = lambda b, i: (0, 0)
    tile = lambda w: pl.BlockSpec((1, tm, w), lambda b, i: (b, i, 0))
    return pl.pallas_call(
        _in_proj_kernel,
        grid=(B, S // tm),
        in_specs=[
            tile(D_MODEL),
            pl.BlockSpec((1, D_MODEL), full),
            pl.BlockSpec((D_MODEL, IN_WIDTH), full),
            pl.BlockSpec((1, LANES), full),
            pl.BlockSpec((1, LANES), full),
            pl.BlockSpec((tm, 3 * LANES), lambda b, i: (i, 0)),
        ],
        out_specs=[
            tile(ATT_WIDTH), tile(KV_WIDTH), tile(KV_WIDTH),
            pl.BlockSpec((tm, SSM_WIDTH), lambda b, i: (i, b)),
        ],
        out_shape=[
            jax.ShapeDtypeStruct((B, S, ATT_WIDTH), F32),
            jax.ShapeDtypeStruct((B, S, KV_WIDTH), F32),
            jax.ShapeDtypeStruct((B, S, KV_WIDTH), F32),
            jax.ShapeDtypeStruct((S, B * SSM_WIDTH), F32),
        ],
        compiler_params=pltpu.CompilerParams(
            dimension_semantics=("arbitrary", "arbitrary"),
            vmem_limit_bytes=VMEM_LIMIT),
        name="in_proj",
    )(x, g, w_bf, gq, gk, rope)


def _swa_kernel(sink_ref, q_ref, k_ref, v_ref, o_ref, *, mask_context):
    tq = q_ref.shape[1]
    i = pl.program_id(1)
    nch = tq // CHUNK
    lane = lax.broadcasted_iota(jnp.int32, (BAND, LANES), 1)
    lo_half = lane < HEAD_DIM
    slabs_per_kv = GQA * HEAD_DIM // LANES

    units = []
    scores = []
    vpads = {}
    valids = {}
    for c in range(nch):
        start = pl.multiple_of((i * nch + c) * CHUNK, CHUNK)
        kb = k_ref[0, pl.ds(start, BAND), :]
        vb = v_ref[0, pl.ds(start, BAND), :]
        kb_sw = pltpu.roll(kb, HEAD_DIM, 1)
        vb_sw = pltpu.roll(vb, HEAD_DIM, 1)
        if mask_context:
            kidx = start + lax.broadcasted_iota(jnp.int32, (1, BAND), 1)
            valids[c] = kidx >= WINDOW
        for kvh in range(N_KV_HEADS):
            k_own, k_oth = (kb, kb_sw) if kvh == 0 else (kb_sw, kb)
            v_own, v_oth = (vb, vb_sw) if kvh == 0 else (vb_sw, vb)
            kpad = (jnp.where(lo_half, k_own, 0.0).astype(BF16),
                    jnp.where(lo_half, 0.0, k_oth).astype(BF16))
            vpads[(c, kvh)] = (jnp.where(lo_half, v_own, 0.0).astype(BF16),
                               jnp.where(lo_half, 0.0, v_oth).astype(BF16))
            col0 = kvh * GQA * HEAD_DIM
            q2 = jnp.concatenate(
                [q_ref[0, c * CHUNK:(c + 1) * CHUNK, col0 + m * LANES:col0 + (m + 1) * LANES]
                 for m in range(slabs_per_kv)], axis=0).astype(BF16)
            for side in range(2):
                s = lax.dot_general(q2, kpad[side], (((1,), (1,)), ((), ())),
                                    preferred_element_type=F32) * (HEAD_DIM ** -0.5)
                if mask_context:
                    s = jnp.where(valids[c], s, NEG)
                units.append((c, kvh, side))
                scores.append(s)

    s_all = jnp.concatenate(scores, axis=0)
    sk = jnp.concatenate(
        [jnp.full((CHUNK, 1), sink_ref[kvh * GQA + 2 * m + side], F32)
         for (_, kvh, side) in units for m in range(slabs_per_kv)], axis=0)
    mx = jnp.maximum(jnp.max(s_all, axis=-1, keepdims=True), sk)
    p_all = jnp.exp(s_all - mx)
    den = jnp.sum(p_all, axis=-1, keepdims=True) + jnp.exp(sk - mx)
    p_all = (p_all / den).astype(BF16)
    rows_u = slabs_per_kv * CHUNK
    probs = [p_all[n * rows_u:(n + 1) * rows_u] for n in range(len(units))]

    for n in range(0, len(units), 2):
        c, kvh, _ = units[n]
        vp = vpads[(c, kvh)]
        o = _mm(probs[n], vp[0]) + _mm(probs[n + 1], vp[1])
        col0 = kvh * GQA * HEAD_DIM
        for m in range(slabs_per_kv):
            o_ref[0, c * CHUNK:(c + 1) * CHUNK, col0 + m * LANES:col0 + (m + 1) * LANES] = (
                o[m * CHUNK:(m + 1) * CHUNK])


def _swa(sink, q, kctx, vctx, tq, mask_context):
    B, Sq, _ = q.shape
    Sk = kctx.shape[1]
    return pl.pallas_call(
        functools.partial(_swa_kernel, mask_context=mask_context),
        grid=(B, Sq // tq),
        in_specs=[
            pl.BlockSpec(memory_space=pltpu.SMEM),
            pl.BlockSpec((1, tq, ATT_WIDTH), lambda b, i: (b, i, 0)),
            pl.BlockSpec((1, Sk, KV_WIDTH), lambda b, i: (b, 0, 0)),
            pl.BlockSpec((1, Sk, KV_WIDTH), lambda b, i: (b, 0, 0)),
        ],
        out_specs=pl.BlockSpec((1, tq, ATT_WIDTH), lambda b, i: (b, i, 0)),
        out_shape=jax.ShapeDtypeStruct((B, Sq, ATT_WIDTH), F32),
        compiler_params=pltpu.CompilerParams(
            dimension_semantics=("arbitrary", "arbitrary"),
            vmem_limit_bytes=VMEM_LIMIT),
        name="swa",
    )(sink, q, kctx, vctx)


def _ssm_kernel(u_ref, h0r_ref, h0i_ref, lam_ref, bre_ref, bim_ref, cre_ref, cim_ref,
                d_ref, wglu_ref, bglu_ref,
                y_ref, hr_out, hi_out, sr, si, hr_s, hi_s):
    L, B, _ = u_ref.shape
    rows = L * B
    half_w = SSM_WIDTH // 2
    half_c = SSM_COLS // 2

    @pl.when(pl.program_id(0) == 0)
    def _():
        hr_s[...] = h0r_ref[...]
        hi_s[...] = h0i_ref[...]

    u = u_ref[...].reshape(rows, SSM_WIDTH)
    ub = u.astype(BF16)
    for hf in range(2):
        uh = ub[:, hf * half_w:(hf + 1) * half_w]
        sr[:, hf * half_c:(hf + 1) * half_c] = _mm(uh, bre_ref[hf])
        si[:, hf * half_c:(hf + 1) * half_c] = _mm(uh, bim_ref[hf])

    cw = 4 * LANES
    for cc in range(SSM_COLS // cw):
        cols = slice(cc * cw, (cc + 1) * cw)
        lr = jnp.broadcast_to(lam_ref[0:1, cols], (B, cw))
        li = jnp.broadcast_to(lam_ref[1:2, cols], (B, cw))

        def body(t, carry):
            hr, hi = carry
            at_t = pl.ds(pl.multiple_of(t * B, B), B)
            nr = lr * hr - li * hi + sr[at_t, cols]
            ni = lr * hi + li * hr + si[at_t, cols]
            sr[at_t, cols] = nr
            si[at_t, cols] = ni
            return nr, ni

        hr, hi = lax.fori_loop(0, L, body, (hr_s[:, cols], hi_s[:, cols]), unroll=2)
        hr_s[:, cols] = hr
        hi_s[:, cols] = hi

    ys = []
    for hf in range(2):
        cs = slice(hf * half_c, (hf + 1) * half_c)
        ys.append(_mm(sr[:, cs].astype(BF16), cre_ref[hf])
                  + _mm(si[:, cs].astype(BF16), cim_ref[hf]))
    y = jnp.concatenate(ys, axis=1) + d_ref[...] * u
    g = 0.5 * y * (1.0 + jnp.tanh(math.sqrt(2.0 / math.pi) * (y + 0.044715 * (y * y * y))))
    gb = g.astype(BF16)
    z = jnp.concatenate(
        [_mm(gb[:, hf * half_w:(hf + 1) * half_w], wglu_ref[hf]) for hf in range(2)],
        axis=1) + bglu_ref[...]
    out = g * (1.0 / (1.0 + jnp.exp(-z)))
    y_ref[...] = out.reshape(L, B, SSM_WIDTH)
    hr_out[...] = hr_s[...]
    hi_out[...] = hi_s[...]


def _ssm(u, h0r, h0i, sp, L):
    S, B, _ = u.shape
    c2 = lambda i: (0, 0)
    c3 = lambda i: (0, 0, 0)
    return pl.pallas_call(
        _ssm_kernel,
        grid=(S // L,),
        in_specs=[
            pl.BlockSpec((L, B, SSM_WIDTH), lambda i: (i, 0, 0)),
            pl.BlockSpec((B, SSM_COLS), c2),
            pl.BlockSpec((B, SSM_COLS), c2),
            pl.BlockSpec((2, SSM_COLS), c2),
            pl.BlockSpec((2, SSM_WIDTH // 2, SSM_COLS // 2), c3),
            pl.BlockSpec((2, SSM_WIDTH // 2, SSM_COLS // 2), c3),
            pl.BlockSpec((2, SSM_COLS // 2, SSM_WIDTH // 2), c3),
            pl.BlockSpec((2, SSM_COLS // 2, SSM_WIDTH // 2), c3),
            pl.BlockSpec((1, SSM_WIDTH), c2),
            pl.BlockSpec((2, SSM_WIDTH // 2, SSM_WIDTH // 2), c3),
            pl.BlockSpec((1, SSM_WIDTH), c2),
        ],
        out_specs=[
            pl.BlockSpec((L, B, SSM_WIDTH), lambda i: (i, 0, 0)),
            pl.BlockSpec((B, SSM_COLS), c2),
            pl.BlockSpec((B, SSM_COLS), c2),
        ],
        out_shape=[
            jax.ShapeDtypeStruct((S, B, SSM_WIDTH), F32),
            jax.ShapeDtypeStruct((B, SSM_COLS), F32),
            jax.ShapeDtypeStruct((B, SSM_COLS), F32),
        ],
        scratch_shapes=[
            pltpu.VMEM((L * B, SSM_COLS), F32),
            pltpu.VMEM((L * B, SSM_COLS), F32),
            pltpu.VMEM((B, SSM_COLS), F32),
            pltpu.VMEM((B, SSM_COLS), F32),
        ],
        compiler_params=pltpu.CompilerParams(
            dimension_semantics=("arbitrary",), vmem_limit_bytes=VMEM_LIMIT),
        name="ssm",
    )(u, h0r, h0i, sp["lam"], sp["bre"], sp["bim"], sp["cre"], sp["cim"],
      sp["d"], sp["wglu"], sp["bglu"])


def _block_diag(blocks):
    G, r, c = blocks.shape
    eye = jnp.eye(G, dtype=blocks.dtype)
    return jnp.einsum("grc,gh->grhc", blocks, eye).reshape(G * r, G * c)


def _ssm_params(lam_re, lam_im, log_dt, b_re, b_im, c_re, c_im, d, w_glu, b_glu):
    lam = lax.complex(lam_re.astype(F32), lam_im.astype(F32))
    dt = jnp.exp(log_dt.astype(F32))[:, None]
    lam_bar = jnp.exp(lam * dt)
    bmat = lax.complex(b_re.astype(F32), b_im.astype(F32))
    b_bar = ((lam_bar - 1.0) / lam)[..., None] * bmat
    lam2 = jnp.stack([lam_bar.real.reshape(-1), lam_bar.imag.reshape(-1)])
    bt = jnp.swapaxes(b_bar, 1, 2)
    hw, hc = SSM_WIDTH // 2, SSM_COLS // 2
    split_b = lambda m: jnp.stack([m[:hw, :hc], m[hw:, hc:]]).astype(BF16)
    split_c = lambda m: jnp.stack([m[:hc, :hw], m[hc:, hw:]]).astype(BF16)
    ct_re = jnp.swapaxes(c_re.astype(F32), 1, 2)
    ct_im = jnp.swapaxes(c_im.astype(F32), 1, 2)
    wg = _block_diag(w_glu.astype(F32))
    return {
        "lam": lam2,
        "bre": split_b(_block_diag(bt.real)),
        "bim": split_b(_block_diag(bt.imag)),
        "cre": split_c(_block_diag(ct_re)),
        "cim": split_c(_block_diag(-ct_im)),
        "d": d.astype(F32).reshape(1, SSM_WIDTH),
        "wglu": jnp.stack([wg[:hw, :hw], wg[hw:, hw:]]).astype(BF16),
        "bglu": b_glu.astype(F32).reshape(1, SSM_WIDTH),
    }


def _memkv_kernel(m_ref, g_ref, w_ref, gk_ref, k_ref, v_ref):
    m = _rms(m_ref[...], g_ref[...])
    kv = _mm(m.astype(BF16), w_ref[...])
    for h in range(CA_HEADS):
        sl = slice(h * CA_HEAD_DIM, (h + 1) * CA_HEAD_DIM)
        k_ref[:, sl] = _rms(kv[:, sl], gk_ref[...])
    v_ref[...] = kv[:, CA_WIDTH:]


def _memkv(mem2d, g, w_bf, gk, tm):
    T = mem2d.shape[0]
    full = lambda i: (0, 0)
    return pl.pallas_call(
        _memkv_kernel,
        grid=(T // tm,),
        in_specs=[
            pl.BlockSpec((tm, D_MODEL), lambda i: (i, 0)),
            pl.BlockSpec((1, D_MODEL), full),
            pl.BlockSpec((D_MODEL, 2 * CA_WIDTH), full),
            pl.BlockSpec((1, CA_HEAD_DIM), full),
        ],
        out_specs=[
            pl.BlockSpec((tm, CA_WIDTH), lambda i: (i, 0)),
            pl.BlockSpec((tm, CA_WIDTH), lambda i: (i, 0)),
        ],
        out_shape=[
            jax.ShapeDtypeStruct((T, CA_WIDTH), F32),
            jax.ShapeDtypeStruct((T, CA_WIDTH), F32),
        ],
        compiler_params=pltpu.CompilerParams(
            dimension_semantics=("arbitrary",), vmem_limit_bytes=VMEM_LIMIT),
        name="memkv",
    )(mem2d, g, w_bf, gk)


def _mid_kernel(x_ref, att_ref, ssm_ref, mk_ref, mv_ref,
                gao_ref, gso_ref, wout_ref, gx_ref, wcq_ref, gcq_ref, wco_ref,
                gffn_ref, wr_ref, br_ref,
                x2_ref, hn_ref, rt_ref, rtt_ref, cnt_ref, base_s):
    tm = x_ref.shape[1]

    @pl.when((pl.program_id(0) == 0) & (pl.program_id(1) == 0))
    def _():
        base_s[...] = jnp.zeros_like(base_s)

    a = _rms(att_ref[0], gao_ref[...]).astype(BF16)
    s = _rms(ssm_ref[...], gso_ref[...]).astype(BF16)
    x1 = (x_ref[0] + _mm(a, wout_ref[0:ATT_WIDTH, :])
          + _mm(s, wout_ref[ATT_WIDTH:, :]))

    qx = _mm(_rms(x1, gx_ref[...]).astype(BF16), wcq_ref[...])
    heads = []
    for h in range(CA_HEADS):
        sl = slice(h * CA_HEAD_DIM, (h + 1) * CA_HEAD_DIM)
        qh = _rms(qx[:, sl], gcq_ref[...]).astype(BF16)
        kh = mk_ref[0, :, sl].astype(BF16)
        vh = mv_ref[0, :, sl].astype(BF16)
        sc = lax.dot_general(qh, kh, (((1,), (1,)), ((), ())),
                             preferred_element_type=F32) * (CA_HEAD_DIM ** -0.5)
        p = jnp.exp(sc - jnp.max(sc, axis=-1, keepdims=True))
        p = p / jnp.sum(p, axis=-1, keepdims=True)
        heads.append(_mm(p.astype(BF16), vh))
    o = jnp.concatenate(heads, axis=1).astype(BF16)
    x2 = x1 + _mm(o, wco_ref[...])
    x2_ref[0] = x2

    hn = _rms(x2, gffn_ref[...])
    hn_ref[0] = hn

    h_hi = hn.astype(BF16)
    h_lo = (hn - h_hi.astype(F32)).astype(BF16)
    r1 = _mm(h_hi, wr_ref[...])
    lg = (r1[:, :LANES] + r1[:, LANES:] + _mm(h_lo, wr_ref[:, 0:LANES])
          + br_ref[...])

    col = lax.broadcasted_iota(jnp.int32, (tm, LANES), 1)
    big = jnp.int32(4 * LANES)
    gmask = col < N_EXPERT_GROUPS
    lgg = jnp.where(gmask, lg, NEG)
    mg = jnp.max(lgg, axis=-1, keepdims=True)
    grp = jnp.min(jnp.where(gmask & (lgg == mg), col, big), axis=-1, keepdims=True)
    pg_top = 1.0 / jnp.sum(jnp.where(gmask, jnp.exp(lgg - mg), 0.0), axis=-1, keepdims=True)

    ecol = col - ROUTER_COL0
    emask = ((ecol >= 0) & (ecol < N_EXPERTS)
             & (lax.shift_right_arithmetic(ecol, 3) == grp))
    le = jnp.where(emask, lg, NEG)
    m1 = jnp.max(le, axis=-1, keepdims=True)
    i1 = jnp.min(jnp.where(emask & (le == m1), col, big), axis=-1, keepdims=True)
    rest = emask & (col != i1)
    le2 = jnp.where(rest, lg, NEG)
    m2 = jnp.max(le2, axis=-1, keepdims=True)
    i2 = jnp.min(jnp.where(rest & (le2 == m2), col, big), axis=-1, keepdims=True)
    den = jnp.sum(jnp.where(emask, jnp.exp(le - m1), 0.0), axis=-1, keepdims=True)
    p1 = 1.0 / den
    p2 = jnp.exp(m2 - m1) / den
    gate1 = pg_top * p1 / (p1 + p2)
    gate2 = pg_top * p2 / (p1 + p2)

    sel1 = col == i1
    sel2 = col == i2
    oh = jnp.where(sel1 | sel2, 1.0, 0.0)
    r_i = lax.broadcasted_iota(jnp.int32, (tm, tm), 0)
    c_i = lax.broadcasted_iota(jnp.int32, (tm, tm), 1)
    tri = jnp.where(r_i > c_i, 1.0, 0.0).astype(BF16)
    tot = base_s[...] + _mm(tri, oh.astype(BF16))
    rank1 = jnp.sum(jnp.where(sel1, tot, 0.0), axis=-1, keepdims=True)
    rank2 = jnp.sum(jnp.where(sel2, tot, 0.0), axis=-1, keepdims=True)
    base_s[...] = base_s[...] + jnp.sum(oh, axis=0, keepdims=True)
    cnt_ref[...] = base_s[...]

    e1 = (i1 - ROUTER_COL0).astype(F32)
    e2 = (i2 - ROUTER_COL0).astype(F32)
    rt = jnp.zeros((tm, LANES), F32)
    for k, val in enumerate((e1, e2, gate1, gate2, rank1, rank2)):
        rt = jnp.where(col == k, val, rt)
    rt_ref[0] = rt
    rtt_ref[0] = rt.T[0:8, :]


def _mid(x, att, ssm_tm, mk, mv, wp, tm):
    B, S, _ = x.shape
    c2 = lambda b, i: (0, 0)
    tile = lambda w: pl.BlockSpec((1, tm, w), lambda b, i: (b, i, 0))
    return pl.pallas_call(
        _mid_kernel,
        grid=(B, S // tm),
        in_specs=[
            tile(D_MODEL), tile(ATT_WIDTH),
            pl.BlockSpec((tm, SSM_WIDTH), lambda b, i: (i, b)),
            pl.BlockSpec((1, N_MEM, CA_WIDTH), lambda b, i: (b, 0, 0)),
            pl.BlockSpec((1, N_MEM, CA_WIDTH), lambda b, i: (b, 0, 0)),
            pl.BlockSpec((1, ATT_WIDTH), c2),
            pl.BlockSpec((1, SSM_WIDTH), c2),
            pl.BlockSpec((ATT_WIDTH + SSM_WIDTH, D_MODEL), c2),
            pl.BlockSpec((1, D_MODEL), c2),
            pl.BlockSpec((D_MODEL, CA_WIDTH), c2),
            pl.BlockSpec((1, CA_HEAD_DIM), c2),
            pl.BlockSpec((CA_WIDTH, D_MODEL), c2),
            pl.BlockSpec((1, D_MODEL), c2),
            pl.BlockSpec((D_MODEL, 2 * LANES), c2),
            pl.BlockSpec((1, LANES), c2),
        ],
        out_specs=[
            tile(D_MODEL), tile(D_MODEL), tile(LANES),
            pl.BlockSpec((1, 8, tm), lambda b, i: (b, 0, i)),
            pl.BlockSpec((1, LANES), c2),
        ],
        out_shape=[
            jax.ShapeDtypeStruct((B, S, D_MODEL), F32),
            jax.ShapeDtypeStruct((B, S, D_MODEL), F32),
            jax.ShapeDtypeStruct((B, S, LANES), F32),
            jax.ShapeDtypeStruct((B, 8, S), F32),
            jax.ShapeDtypeStruct((1, LANES), F32),
        ],
        scratch_shapes=[pltpu.VMEM((1, LANES), F32)],
        compiler_params=pltpu.CompilerParams(
            dimension_semantics=("arbitrary", "arbitrary"),
            vmem_limit_bytes=VMEM_LIMIT),
        name="mid",
    )(x, att, ssm_tm, mk, mv, wp["gao"], wp["gso"], wp["wout"], wp["gx"], wp["wcq"],
      wp["gcq"], wp["wco"], wp["gffn"], wp["wr"], wp["br"])


def _row_gather(idx_ref, idx_base, n_rows, src_hbm, dst, sem):
    def issue(r, carry):
        pltpu.make_async_copy(src_hbm.at[pl.ds(idx_ref[0, 0, idx_base + r], 1), :],
                              dst.at[pl.ds(r, 1), :], sem).start()
        return carry

    lax.fori_loop(0, n_rows, issue, 0, unroll=8)


def _row_gather_wait(n_rows, src_hbm, dst, sem):
    pltpu.make_async_copy(src_hbm.at[pl.ds(0, n_rows), :], dst, sem).wait()


def _moe_kernel(be_ref, nu_ref, tok_ref, tok_next_ref, hn_hbm, wg_ref, wu_ref, wd_ref,
                yb_ref, xbuf, wg_s, wu_s, wd_s, sem):
    i = pl.program_id(0)
    n_used = nu_ref[0]
    slot = lax.rem(i, 2)

    @pl.when((i == 0) & (n_used > 0))
    def _():
        _row_gather(tok_ref, 0, MOE_BLOCK, hn_hbm, xbuf.at[0], sem.at[0])

    @pl.when(i + 1 < n_used)
    def _():
        _row_gather(tok_next_ref, 0, MOE_BLOCK, hn_hbm, xbuf.at[1 - slot], sem.at[1 - slot])

    @pl.when(i < n_used)
    def _():
        new_expert = (i == 0) | (be_ref[i] != be_ref[jnp.maximum(i - 1, 0)])

        @pl.when(new_expert)
        def _():
            wg_s[...] = wg_ref[0].astype(BF16)
            wu_s[...] = wu_ref[0].astype(BF16)
            wd_s[...] = wd_ref[0].astype(BF16)

        _row_gather_wait(MOE_BLOCK, hn_hbm, xbuf.at[slot], sem.at[slot])
        xe = xbuf[slot].astype(BF16)
        g = _mm(xe, wg_s[...])
        u = _mm(xe, wu_s[...])
        hmid = (g * (1.0 / (1.0 + jnp.exp(-g)))) * u
        yb_ref[...] = _mm(hmid.astype(BF16), wd_s[...])

    @pl.when(i >= n_used)
    def _():
        yb_ref[...] = jnp.zeros_like(yb_ref)


def _moe(block_e, n_used, row_tok, hn2d, w_gate, w_up, w_down):
    n_blocks = block_e.shape[0]
    tok3 = row_tok.reshape(n_blocks, 1, MOE_BLOCK)
    grid_spec = pltpu.PrefetchScalarGridSpec(
        num_scalar_prefetch=2,
        grid=(n_blocks,),
        in_specs=[
            pl.BlockSpec((1, 1, MOE_BLOCK), lambda i, be, nu: (i, 0, 0),
                         memory_space=pltpu.SMEM),
            pl.BlockSpec((1, 1, MOE_BLOCK),
                         lambda i, be, nu: (jnp.minimum(i + 1, n_blocks - 1), 0, 0),
                         memory_space=pltpu.SMEM),
            pl.BlockSpec(memory_space=pl.ANY),
            pl.BlockSpec((1, D_MODEL, D_EXPERT), lambda i, be, nu: (be[i], 0, 0)),
            pl.BlockSpec((1, D_MODEL, D_EXPERT), lambda i, be, nu: (be[i], 0, 0)),
            pl.BlockSpec((1, D_EXPERT, D_MODEL), lambda i, be, nu: (be[i], 0, 0)),
        ],
        out_specs=pl.BlockSpec((MOE_BLOCK, D_MODEL), lambda i, be, nu: (i, 0)),
        scratch_shapes=[
            pltpu.VMEM((2, MOE_BLOCK, D_MODEL), F32),
            pltpu.VMEM((D_MODEL, D_EXPERT), BF16),
            pltpu.VMEM((D_MODEL, D_EXPERT), BF16),
            pltpu.VMEM((D_EXPERT, D_MODEL), BF16),
            pltpu.SemaphoreType.DMA((2,)),
        ],
    )
    return pl.pallas_call(
        _moe_kernel,
        grid_spec=grid_spec,
        out_shape=jax.ShapeDtypeStruct((n_blocks * MOE_BLOCK, D_MODEL), F32),
        compiler_params=pltpu.CompilerParams(
            dimension_semantics=("arbitrary",), vmem_limit_bytes=VMEM_LIMIT),
        name="moe",
    )(block_e, n_used, tok3, tok3, hn2d, w_gate, w_up, w_down)


def _combine_kernel(dest_ref, dest_next_ref, x2_ref, rt_ref, yb_hbm, o_ref, buf, sem):
    tm = x2_ref.shape[0]
    i = pl.program_id(0)
    slot = lax.rem(i, 2)

    def gather(d_ref, s):
        for k in range(2):
            _row_gather(d_ref, k * tm, tm, yb_hbm, buf.at[s, k], sem.at[s])

    @pl.when(i == 0)
    def _():
        gather(dest_ref, 0)

    @pl.when(i + 1 < pl.num_programs(0))
    def _():
        gather(dest_next_ref, 1 - slot)

    for k in range(2):
        _row_gather_wait(tm, yb_hbm, buf.at[slot, k], sem.at[slot])
    rt = rt_ref[...]
    o_ref[...] = x2_ref[...] + rt[:, 2:3] * buf[slot, 0] + rt[:, 3:4] * buf[slot, 1]


def _combine(dest, x2, rt, yb, tm):
    T = x2.shape[0]
    nt = T // tm
    return pl.pallas_call(
        _combine_kernel,
        grid=(nt,),
        in_specs=[
            pl.BlockSpec((1, 1, 2 * tm), lambda i: (i, 0, 0), memory_space=pltpu.SMEM),
            pl.BlockSpec((1, 1, 2 * tm), lambda i: (jnp.minimum(i + 1, nt - 1), 0, 0),
                         memory_space=pltpu.SMEM),
            pl.BlockSpec((tm, D_MODEL), lambda i: (i, 0)),
            pl.BlockSpec((tm, LANES), lambda i: (i, 0)),
            pl.BlockSpec(memory_space=pl.ANY),
        ],
        out_specs=pl.BlockSpec((tm, D_MODEL), lambda i: (i, 0)),
        out_shape=jax.ShapeDtypeStruct((T, D_MODEL), F32),
        scratch_shapes=[
            pltpu.VMEM((2, 2, tm, D_MODEL), F32),
            pltpu.SemaphoreType.DMA((2,)),
        ],
        compiler_params=pltpu.CompilerParams(
            dimension_semantics=("arbitrary",), vmem_limit_bytes=VMEM_LIMIT),
        name="combine",
    )(dest, dest, x2, rt, yb)


def _hier_moe(x2, hn, rt, rtt, cnt, w_gate, w_up, w_down, tm):
    T = x2.shape[0]
    counts = cnt[0, ROUTER_COL0:ROUTER_COL0 + N_EXPERTS].astype(jnp.int32)
    padded = (counts + MOE_BLOCK - 1) // MOE_BLOCK * MOE_BLOCK
    pad_end = jnp.cumsum(padded)
    pad_start = pad_end - padded
    flat = lambda a: jnp.swapaxes(a, 0, 1).reshape(a.shape[1], T)
    eid = flat(rtt[:, 0:2, :]).astype(jnp.int32)
    rank = flat(rtt[:, 4:6, :]).astype(jnp.int32)
    experts = jnp.arange(N_EXPERTS, dtype=jnp.int32)[:, None, None]
    dest = rank + jnp.sum(jnp.where(eid[None] == experts, pad_start[:, None, None], 0), axis=0)
    n_blocks = (2 * T + N_EXPERTS * (MOE_BLOCK - 1)) // MOE_BLOCK + 1
    rows = n_blocks * MOE_BLOCK
    tok = jnp.broadcast_to(jnp.arange(T, dtype=jnp.int32)[None, :], (2, T))
    row_tok = jnp.zeros((rows,), jnp.int32).at[dest.reshape(-1)].set(
        tok.reshape(-1), unique_indices=True)
    blk_row0 = jnp.arange(n_blocks, dtype=jnp.int32) * MOE_BLOCK
    block_e = jnp.minimum(
        jnp.sum((pad_end[None, :] <= blk_row0[:, None]).astype(jnp.int32), axis=1),
        N_EXPERTS - 1)
    n_used = (pad_end[-1] // MOE_BLOCK).astype(jnp.int32).reshape(1)
    yb = _moe(block_e, n_used, row_tok, hn, w_gate, w_up, w_down)
    nt = T // tm
    dest_t = dest.reshape(2, nt, tm).transpose(1, 0, 2).reshape(nt, 1, 2 * tm)
    return _combine(dest_t, x2, rt, yb, tm)


def _rope_table(pos):
    half = ROPE_DIM // 2
    inv = ROPE_THETA ** (-jnp.arange(0, ROPE_DIM, 2, dtype=F32) / ROPE_DIM)
    ang = pos.astype(F32)[:, None] * inv[None, :]
    cos, sin = jnp.cos(ang), jnp.sin(ang)
    L = pos.shape[0]
    pad = jnp.zeros((L, HEAD_DIM - ROPE_DIM), F32)
    zero = jnp.zeros((L, half), F32)
    c64 = jnp.concatenate([cos, cos, pad + 1.0], axis=1)
    lo64 = jnp.concatenate([-sin, zero, pad], axis=1)
    hi64 = jnp.concatenate([zero, sin, pad], axis=1)
    two = lambda t: jnp.concatenate([t, t], axis=1)
    return jnp.concatenate([two(c64), two(lo64), two(hi64)], axis=1)


def _layer(x, pos_rope, kctx_prev, vctx_prev, h0r, h0i, mk, mv, wp, sp, ew, *,
           tm_in, tq, ssm_l, tm_mid, tm_comb, mask_context):
    B, S, _ = x.shape
    T = B * S
    q, k3, v3, u_tm = _in_proj(x, wp["gmix"], wp["win"], wp["gq"], wp["gk"], pos_rope, tm_in)
    kctx = jnp.concatenate([kctx_prev, k3], axis=1)
    vctx = jnp.concatenate([vctx_prev, v3], axis=1)
    att = _swa(wp["sink"], q, kctx, vctx, tq, mask_context)
    ssm_tm, hr, hi = _ssm(u_tm.reshape(S, B, SSM_WIDTH), h0r, h0i, sp, ssm_l)
    x2, hn, rt, rtt, cnt = _mid(x, att, ssm_tm.reshape(S, B * SSM_WIDTH), mk, mv, wp, tm_mid)
    y = _hier_moe(x2.reshape(T, D_MODEL), hn.reshape(T, D_MODEL),
                  rt.reshape(T, LANES), rtt, cnt, *ew, tm_comb)
    return y.reshape(B, S, D_MODEL), k3, v3, hr, hi


def kernel(x_prompt, x_sample, cache_attn_k, cache_attn_v, state_ssm_re, state_ssm_im, cache_mem_k, cache_mem_v, mem_prompt, norm_mix, w_in, q_norm, k_norm, attn_sink, ssm_lambda_re, ssm_lambda_im, ssm_log_dt, ssm_b_re, ssm_b_im, ssm_c_re, ssm_c_im, ssm_d, ssm_w_glu, ssm_b_glu, norm_attn_out, norm_ssm_out, w_out, norm_cross, norm_mem, w_cq, w_ck, w_cv, cq_norm, ck_norm, w_co, norm_ffn, w_router_group, b_router_group, w_router_expert, b_router_expert, w_e_gate, w_e_up, w_e_down):
    depth = norm_mix.shape[0]
    Bp, Lp, _ = x_prompt.shape
    Bs, Ls, _ = x_sample.shape
    yp, ys = x_prompt, x_sample
    rope_p = _rope_table(jnp.arange(Lp, dtype=jnp.int32))
    rope_s = _rope_table(PAST_LEN + jnp.arange(Ls, dtype=jnp.int32))
    outs = [[] for _ in range(10)]
    n_router = N_EXPERT_GROUPS + N_EXPERTS
    for l in range(depth):
        row = lambda a: a[l].astype(F32).reshape(1, -1)
        w_r = jnp.pad(jnp.concatenate([w_router_group[l], w_router_expert[l]], axis=1).astype(F32),
                      ((0, 0), (0, LANES - n_router)))
        w_r_hi = w_r.astype(BF16)
        w_r_lo = (w_r - w_r_hi.astype(F32)).astype(BF16)
        b_r = jnp.pad(jnp.concatenate([b_router_group[l], b_router_expert[l]]).astype(F32),
                      (0, LANES - n_router)).reshape(1, LANES)
        wp = {
            "gmix": row(norm_mix), "win": w_in[l].astype(BF16),
            "gq": jnp.tile(row(q_norm), (1, LANES // HEAD_DIM)),
            "gk": jnp.tile(row(k_norm), (1, LANES // HEAD_DIM)),
            "sink": attn_sink[l].astype(F32),
            "gao": row(norm_attn_out), "gso": row(norm_ssm_out),
            "wout": w_out[l].astype(BF16), "gx": row(norm_cross),
            "wcq": w_cq[l].astype(BF16), "gcq": row(cq_norm),
            "wco": w_co[l].astype(BF16), "gffn": row(norm_ffn),
            "wr": jnp.concatenate([w_r_hi, w_r_lo], axis=1), "br": b_r,
        }
        sp = _ssm_params(ssm_lambda_re[l], ssm_lambda_im[l], ssm_log_dt[l], ssm_b_re[l],
                         ssm_b_im[l], ssm_c_re[l], ssm_c_im[l], ssm_d[l], ssm_w_glu[l],
                         ssm_b_glu[l])
        ew = (w_e_gate[l].astype(F32), w_e_up[l].astype(F32), w_e_down[l].astype(F32))

        w_ckv = jnp.concatenate([w_ck[l], w_cv[l]], axis=1).astype(BF16)
        mkp, mvp = _memkv(mem_prompt.reshape(Bp * N_MEM, D_MODEL), row(norm_mem), w_ckv,
                          row(ck_norm), 512)
        mkp = mkp.reshape(Bp, N_MEM, CA_WIDTH)
        mvp = mvp.reshape(Bp, N_MEM, CA_WIDTH)

        zctx = jnp.zeros((Bp, WINDOW, KV_WIDTH), F32)
        zst = jnp.zeros((Bp, SSM_COLS), F32)
        yp, kp, vp, hpr, hpi = _layer(
            yp, rope_p, zctx, zctx, zst, zst, mkp, mvp, wp, sp, ew,
            tm_in=512, tq=256, ssm_l=64, tm_mid=512, tm_comb=256, mask_context=True)
        ys, kn, vn, hsr, hsi = _layer(
            ys, rope_s, cache_attn_k[l].reshape(Bs, WINDOW, KV_WIDTH).astype(F32),
            cache_attn_v[l].reshape(Bs, WINDOW, KV_WIDTH).astype(F32),
            state_ssm_re[l].astype(F32).reshape(Bs, SSM_COLS),
            state_ssm_im[l].astype(F32).reshape(Bs, SSM_COLS),
            cache_mem_k[l].astype(F32).reshape(Bs, N_MEM, CA_WIDTH),
            cache_mem_v[l].astype(F32).reshape(Bs, N_MEM, CA_WIDTH), wp, sp, ew,
            tm_in=Ls, tq=CHUNK, ssm_l=Ls, tm_mid=Ls, tm_comb=256, mask_context=False)

        sg = (N_SSM_GROUPS, SSM_STATE)
        kvs = (N_KV_HEADS, HEAD_DIM)
        vals = (kp[:, Lp - WINDOW:].reshape(Bp, WINDOW, *kvs),
                vp[:, Lp - WINDOW:].reshape(Bp, WINDOW, *kvs),
                hpr.reshape(Bp, *sg), hpi.reshape(Bp, *sg),
                mkp.reshape(Bp, N_MEM, CA_HEADS, CA_HEAD_DIM),
                mvp.reshape(Bp, N_MEM, CA_HEADS, CA_HEAD_DIM),
                kn.reshape(Bs, Ls, *kvs), vn.reshape(Bs, Ls, *kvs),
                hsr.reshape(Bs, *sg), hsi.reshape(Bs, *sg))
        for lst, val in zip(outs, vals):
            lst.append(val)
    return (yp, ys) + tuple(jnp.stack(lst) for lst in outs)
```

```python
import functools
import math

import jax
import jax.numpy as jnp
from jax import lax
from jax.experimental import pallas as pl
from jax.experimental.pallas import tpu as pltpu

F32 = jnp.float32
BF16 = jnp.bfloat16

D_MODEL = 1024
CHUNK = 64
N_Q_HEADS = 8
N_KV_HEADS = 2
GQA = N_Q_HEADS // N_KV_HEADS
HEAD_DIM = 64
WINDOW = 128
BAND = WINDOW + CHUNK
ROPE_DIM = HEAD_DIM // 4
ROPE_THETA = 500000.0
ATT_WIDTH = N_Q_HEADS * HEAD_DIM
KV_WIDTH = N_KV_HEADS * HEAD_DIM
SSM_GROUP = 16
SSM_WIDTH = D_MODEL // 2
N_SSM_GROUPS = SSM_WIDTH // SSM_GROUP
SSM_STATE = 64
SSM_COLS = N_SSM_GROUPS * SSM_STATE
IN_WIDTH = ATT_WIDTH + 2 * KV_WIDTH + SSM_WIDTH
N_MEM = 256
CA_HEADS = 4
CA_HEAD_DIM = 128
CA_WIDTH = CA_HEADS * CA_HEAD_DIM
N_EXPERT_GROUPS = 4
EXPERTS_PER_GROUP = 8
N_EXPERTS = N_EXPERT_GROUPS * EXPERTS_PER_GROUP
D_EXPERT = 512
MOE_BLOCK = 256
EPS = 1e-6
NEG = -1e30
PAST_LEN = 4096

LANES = 128
ROUTER_COL0 = N_EXPERT_GROUPS
VMEM_LIMIT = 48 * 1024 * 1024


def _rms(x, g):
    ms = jnp.mean(x * x, axis=-1, keepdims=True)
    return (x * lax.rsqrt(ms + EPS)) * g


def _mm(a, b):
    return jnp.dot(a, b, preferred_element_type=F32)


def _in_proj_kernel(x_ref, g_ref, w_ref, gq_ref, gk_ref, rope_ref,
                    q_ref, k_ref, v_ref, u_ref):
    tm = x_ref.shape[1]
    h = _rms(x_ref[0], g_ref[...])
    hin = _mm(h.astype(BF16), w_ref[...])
    rope = rope_ref[...]
    cos = rope[:, 0:LANES]
    sin_lo = rope[:, LANES:2 * LANES]
    sin_hi = rope[:, 2 * LANES:3 * LANES]
    lane = lax.broadcasted_iota(jnp.int32, (tm, LANES), 1)
    left = lane < HEAD_DIM

    def norm_rope(z, g):
        sq = z * z
        lsum = jnp.sum(jnp.where(left, sq, 0.0), axis=-1, keepdims=True)
        rsum = jnp.sum(jnp.where(left, 0.0, sq), axis=-1, keepdims=True)
        ms = jnp.where(left, lsum, rsum) * (1.0 / HEAD_DIM)
        zn = (z * lax.rsqrt(ms + EPS)) * g
        half = ROPE_DIM // 2
        return (zn * cos + pltpu.roll(zn, LANES - half, 1) * sin_lo
                + pltpu.roll(zn, half, 1) * sin_hi)

    for j in range(ATT_WIDTH // LANES):
        sl = slice(j * LANES, (j + 1) * LANES)
        q_ref[0, :, sl] = norm_rope(hin[:, sl], gq_ref[...])
    k_ref[0] = norm_rope(hin[:, ATT_WIDTH:ATT_WIDTH + KV_WIDTH], gk_ref[...])
    v_ref[0] = hin[:, ATT_WIDTH + KV_WIDTH:ATT_WIDTH + 2 * KV_WIDTH]
    u_ref[...] = hin[:, ATT_WIDTH + 2 * KV_WIDTH:]


def _in_proj(x, g, w_bf, gq, gk, rope, tm):
    B, S, _ = x.shape
    full = lambda b, i: (0, 0)
    tile = lambda w: pl.BlockSpec((1, tm, w), lambda b, i: (b, i, 0))
    return pl.pallas_call(
        _in_proj_kernel,
        grid=(B, S // tm),
        in_specs=[
            tile(D_MODEL),
            pl.BlockSpec((1, D_MODEL), full),
            pl.BlockSpec((D_MODEL, IN_WIDTH), full),
            pl.BlockSpec((1, LANES), full),
            pl.BlockSpec((1, LANES), full),
            pl.BlockSpec((tm, 3 * LANES), lambda b, i: (i, 0)),
        ],
        out_specs=[
            tile(ATT_WIDTH), tile(KV_WIDTH), tile(KV_WIDTH),
            pl.BlockSpec((tm, SSM_WIDTH), lambda b, i: (i, b)),
        ],
        out_shape=[
            jax.ShapeDtypeStruct((B, S, ATT_WIDTH), F32),
            jax.ShapeDtypeStruct((B, S, KV_WIDTH), F32),
            jax.ShapeDtypeStruct((B, S, KV_WIDTH), F32),
            jax.ShapeDtypeStruct((S, B * SSM_WIDTH), F32),
        ],
        compiler_params=pltpu.CompilerParams(
            dimension_semantics=("arbitrary", "arbitrary"),
            vmem_limit_bytes=VMEM_LIMIT),
        name="in_proj",
    )(x, g, w_bf, gq, gk, rope)


def _swa_kernel(sink_ref, q_ref, k_ref, v_ref, o_ref, *, mask_context):
    tq = q_ref.shape[1]
    i = pl.program_id(1)
    nch = tq // CHUNK
    lane = lax.broadcasted_iota(jnp.int32, (BAND, LANES), 1)
    lo_half = lane < HEAD_DIM
    slabs_per_kv = GQA * HEAD_DIM // LANES

    units = []
    scores = []
    vpads = {}
    valids = {}
    for c in range(nch):
        start = pl.multiple_of((i * nch + c) * CHUNK, CHUNK)
        kb = k_ref[0, pl.ds(start, BAND), :]
        vb = v_ref[0, pl.ds(start, BAND), :]
        kb_sw = pltpu.roll(kb, HEAD_DIM, 1)
        vb_sw = pltpu.roll(vb, HEAD_DIM, 1)
        if mask_context:
            kidx = start + lax.broadcasted_iota(jnp.int32, (1, BAND), 1)
            valids[c] = kidx >= WINDOW
        for kvh in range(N_KV_HEADS):
            k_own, k_oth = (kb, kb_sw) if kvh == 0 else (kb_sw, kb)
            v_own, v_oth = (vb, vb_sw) if kvh == 0 else (vb_sw, vb)
            kpad = (jnp.where(lo_half, k_own, 0.0).astype(BF16),
                    jnp.where(lo_half, 0.0, k_oth).astype(BF16))
            vpads[(c, kvh)] = (jnp.where(lo_half, v_own, 0.0).astype(BF16),
                               jnp.where(lo_half, 0.0, v_oth).astype(BF16))
            col0 = kvh * GQA * HEAD_DIM
            q2 = jnp.concatenate(
                [q_ref[0, c * CHUNK:(c + 1) * CHUNK, col0 + m * LANES:col0 + (m + 1) * LANES]
                 for m in range(slabs_per_kv)], axis=0).astype(BF16)
            for side in range(2):
                s = lax.dot_general(q2, kpad[side], (((1,), (1,)), ((), ())),
                                    preferred_element_type=F32) * (HEAD_DIM ** -0.5)
                if mask_context:
                    s = jnp.where(valids[c], s, NEG)
                units.append((c, kvh, side))
                scores.append(s)

    s_all = jnp.concatenate(scores, axis=0)
    sk = jnp.concatenate(
        [jnp.full((CHUNK, 1), sink_ref[kvh * GQA + 2 * m + side], F32)
         for (_, kvh, side) in units for m in range(slabs_per_kv)], axis=0)
    mx = jnp.maximum(jnp.max(s_all, axis=-1, keepdims=True), sk)
    p_all = jnp.exp(s_all - mx)
    den = jnp.sum(p_all, axis=-1, keepdims=True) + jnp.exp(sk - mx)
    p_all = (p_all / den).astype(BF16)
    rows_u = slabs_per_kv * CHUNK
    probs = [p_all[n * rows_u:(n + 1) * rows_u] for n in range(len(units))]

    for n in range(0, len(units), 2):
        c, kvh, _ = units[n]
        vp = vpads[(c, kvh)]
        o = _mm(probs[n], vp[0]) + _mm(probs[n + 1], vp[1])
        col0 = kvh * GQA * HEAD_DIM
        for m in range(slabs_per_kv):
            o_ref[0, c * CHUNK:(c + 1) * CHUNK, col0 + m * LANES:col0 + (m + 1) * LANES] = (
                o[m * CHUNK:(m + 1) * CHUNK])


def _swa(sink, q, kctx, vctx, tq, mask_context):
    B, Sq, _ = q.shape
    Sk = kctx.shape[1]
    return pl.pallas_call(
        functools.partial(_swa_kernel, mask_context=mask_context),
        grid=(B, Sq // tq),
        in_specs=[
            pl.BlockSpec(memory_space=pltpu.SMEM),
            pl.BlockSpec((1, tq, ATT_WIDTH), lambda b, i: (b, i, 0)),
            pl.BlockSpec((1, Sk, KV_WIDTH), lambda b, i: (b, 0, 0)),
            pl.BlockSpec((1, Sk, KV_WIDTH), lambda b, i: (b, 0, 0)),
        ],
        out_specs=pl.BlockSpec((1, tq, ATT_WIDTH), lambda b, i: (b, i, 0)),
        out_shape=jax.ShapeDtypeStruct((B, Sq, ATT_WIDTH), F32),
        compiler_params=pltpu.CompilerParams(
            dimension_semantics=("arbitrary", "arbitrary"),
            vmem_limit_bytes=VMEM_LIMIT),
        name="swa",
    )(sink, q, kctx, vctx)


def _ssm_kernel(u_ref, h0r_ref, h0i_ref, lam_ref, bre_ref, bim_ref, cre_ref, cim_ref,
                d_ref, wglu_ref, bglu_ref,
                y_ref, hr_out, hi_out, sr, si, hr_s, hi_s):
    L, B, _ = u_ref.shape
    rows = L * B
    half_w = SSM_WIDTH // 2
    half_c = SSM_COLS // 2

    @pl.when(pl.program_id(0) == 0)
    def _():
        hr_s[...] = h0r_ref[...]
        hi_s[...] = h0i_ref[...]

    u = u_ref[...].reshape(rows, SSM_WIDTH)
    ub = u.astype(BF16)
    for hf in range(2):
        uh = ub[:, hf * half_w:(hf + 1) * half_w]
        sr[:, hf * half_c:(hf + 1) * half_c] = _mm(uh, bre_ref[hf])
        si[:, hf * half_c:(hf + 1) * half_c] = _mm(uh, bim_ref[hf])

    cw = 4 * LANES
    for cc in range(SSM_COLS // cw):
        cols = slice(cc * cw, (cc + 1) * cw)
        lr = jnp.broadcast_to(lam_ref[0:1, cols], (B, cw))
        li = jnp.broadcast_to(lam_ref[1:2, cols], (B, cw))

        def body(t, carry):
            hr, hi = carry
            at_t = pl.ds(pl.multiple_of(t * B, B), B)
            nr = lr * hr - li * hi + sr[at_t, cols]
            ni = lr * hi + li * hr + si[at_t, cols]
            sr[at_t, cols] = nr
            si[at_t, cols] = ni
            return nr, ni

        hr, hi = lax.fori_loop(0, L, body, (hr_s[:, cols], hi_s[:, cols]), unroll=2)
        hr_s[:, cols] = hr
        hi_s[:, cols] = hi

    ys = []
    for hf in range(2):
        cs = slice(hf * half_c, (hf + 1) * half_c)
        ys.append(_mm(sr[:, cs].astype(BF16), cre_ref[hf])
                  + _mm(si[:, cs].astype(BF16), cim_ref[hf]))
    y = jnp.concatenate(ys, axis=1) + d_ref[...] * u
    g = 0.5 * y * (1.0 + jnp.tanh(math.sqrt(2.0 / math.pi) * (y + 0.044715 * (y * y * y))))
    gb = g.astype(BF16)
    z = jnp.concatenate(
        [_mm(gb[:, hf * half_w:(hf + 1) * half_w], wglu_ref[hf]) for hf in range(2)],
        axis=1) + bglu_ref[...]
    out = g * (1.0 / (1.0 + jnp.exp(-z)))
    y_ref[...] = out.reshape(L, B, SSM_WIDTH)
    hr_out[...] = hr_s[...]
    hi_out[...] = hi_s[...]


def _ssm(u, h0r, h0i, sp, L):
    S, B, _ = u.shape
    c2 = lambda i: (0, 0)
    c3 = lambda i: (0, 0, 0)
    return pl.pallas_call(
        _ssm_kernel,
        grid=(S // L,),
        in_specs=[
            pl.BlockSpec((L, B, SSM_WIDTH), lambda i: (i, 0, 0)),
            pl.BlockSpec((B, SSM_COLS), c2),
            pl.BlockSpec((B, SSM_COLS), c2),
            pl.BlockSpec((2, SSM_COLS), c2),
            pl.BlockSpec((2, SSM_WIDTH // 2, SSM_COLS // 2), c3),
            pl.BlockSpec((2, SSM_WIDTH // 2, SSM_COLS // 2), c3),
            pl.BlockSpec((2, SSM_COLS // 2, SSM_WIDTH // 2), c3),
            pl.BlockSpec((2, SSM_COLS // 2, SSM_WIDTH // 2), c3),
            pl.BlockSpec((1, SSM_WIDTH), c2),
            pl.BlockSpec((2, SSM_WIDTH // 2, SSM_WIDTH // 2), c3),
            pl.BlockSpec((1, SSM_WIDTH), c2),
        ],
        out_specs=[
            pl.BlockSpec((L, B, SSM_WIDTH), lambda i: (i, 0, 0)),
            pl.BlockSpec((B, SSM_COLS), c2),
            pl.BlockSpec((B, SSM_COLS), c2),
        ],
        out_shape=[
            jax.ShapeDtypeStruct((S, B, SSM_WIDTH), F32),
            jax.ShapeDtypeStruct((B, SSM_COLS), F32),
            jax.ShapeDtypeStruct((B, SSM_COLS), F32),
        ],
        scratch_shapes=[
            pltpu.VMEM((L * B, SSM_COLS), F32),
            pltpu.VMEM((L * B, SSM_COLS), F32),
            pltpu.VMEM((B, SSM_COLS), F32),
            pltpu.VMEM((B, SSM_COLS), F32),
        ],
        compiler_params=pltpu.CompilerParams(
            dimension_semantics=("arbitrary",), vmem_limit_bytes=VMEM_LIMIT),
        name="ssm",
    )(u, h0r, h0i, sp["lam"], sp["bre"], sp["bim"], sp["cre"], sp["cim"],
      sp["d"], sp["wglu"], sp["bglu"])


def _block_diag(blocks):
    G, r, c = blocks.shape
    eye = jnp.eye(G, dtype=blocks.dtype)
    return jnp.einsum("grc,gh->grhc", blocks, eye).reshape(G * r, G * c)


def _ssm_params(lam_re, lam_im, log_dt, b_re, b_im, c_re, c_im, d, w_glu, b_glu):
    lam = lax.complex(lam_re.astype(F32), lam_im.astype(F32))
    dt = jnp.exp(log_dt.astype(F32))[:, None]
    lam_bar = jnp.exp(lam * dt)
    bmat = lax.complex(b_re.astype(F32), b_im.astype(F32))
    b_bar = ((lam_bar - 1.0) / lam)[..., None] * bmat
    lam2 = jnp.stack([lam_bar.real.reshape(-1), lam_bar.imag.reshape(-1)])
    bt = jnp.swapaxes(b_bar, 1, 2)
    hw, hc = SSM_WIDTH // 2, SSM_COLS // 2
    split_b = lambda m: jnp.stack([m[:hw, :hc], m[hw:, hc:]]).astype(BF16)
    split_c = lambda m: jnp.stack([m[:hc, :hw], m[hc:, hw:]]).astype(BF16)
    ct_re = jnp.swapaxes(c_re.astype(F32), 1, 2)
    ct_im = jnp.swapaxes(c_im.astype(F32), 1, 2)
    wg = _block_diag(w_glu.astype(F32))
    return {
        "lam": lam2,
        "bre": split_b(_block_diag(bt.real)),
        "bim": split_b(_block_diag(bt.imag)),
        "cre": split_c(_block_diag(ct_re)),
        "cim": split_c(_block_diag(-ct_im)),
        "d": d.astype(F32).reshape(1, SSM_WIDTH),
        "wglu": jnp.stack([wg[:hw, :hw], wg[hw:, hw:]]).astype(BF16),
        "bglu": b_glu.astype(F32).reshape(1, SSM_WIDTH),
    }


def _memkv_kernel(m_ref, g_ref, w_ref, gk_ref, k_ref, v_ref):
    m = _rms(m_ref[...], g_ref[...])
    kv = _mm(m.astype(BF16), w_ref[...])
    for h in range(CA_HEADS):
        sl = slice(h * CA_HEAD_DIM, (h + 1) * CA_HEAD_DIM)
        k_ref[:, sl] = _rms(kv[:, sl], gk_ref[...])
    v_ref[...] = kv[:, CA_WIDTH:]


def _memkv(mem2d, g, w_bf, gk, tm):
    T = mem2d.shape[0]
    full = lambda i: (0, 0)
    return pl.pallas_call(
        _memkv_kernel,
        grid=(T // tm,),
        in_specs=[
            pl.BlockSpec((tm, D_MODEL), lambda i: (i, 0)),
            pl.BlockSpec((1, D_MODEL), full),
            pl.BlockSpec((D_MODEL, 2 * CA_WIDTH), full),
            pl.BlockSpec((1, CA_HEAD_DIM), full),
        ],
        out_specs=[
            pl.BlockSpec((tm, CA_WIDTH), lambda i: (i, 0)),
            pl.BlockSpec((tm, CA_WIDTH), lambda i: (i, 0)),
        ],
        out_shape=[
            jax.ShapeDtypeStruct((T, CA_WIDTH), F32),
            jax.ShapeDtypeStruct((T, CA_WIDTH), F32),
        ],
        compiler_params=pltpu.CompilerParams(
            dimension_semantics=("arbitrary",), vmem_limit_bytes=VMEM_LIMIT),
        name="memkv",
    )(mem2d, g, w_bf, gk)


def _mid_kernel(x_ref, att_ref, ssm_ref, mk_ref, mv_ref,
                gao_ref, gso_ref, wout_ref, gx_ref, wcq_ref, gcq_ref, wco_ref,
                gffn_ref, wr_ref, br_ref,
                x2_ref, hn_ref, rt_ref, rtt_ref, cnt_ref, base_s):
    tm = x_ref.shape[1]

    @pl.when((pl.program_id(0) == 0) & (pl.program_id(1) == 0))
    def _():
        base_s[...] = jnp.zeros_like(base_s)

    a = _rms(att_ref[0], gao_ref[...]).astype(BF16)
    s = _rms(ssm_ref[...], gso_ref[...]).astype(BF16)
    x1 = (x_ref[0] + _mm(a, wout_ref[0:ATT_WIDTH, :])
          + _mm(s, wout_ref[ATT_WIDTH:, :]))

    qx = _mm(_rms(x1, gx_ref[...]).astype(BF16), wcq_ref[...])
    heads = []
    for h in range(CA_HEADS):
        sl = slice(h * CA_HEAD_DIM, (h + 1) * CA_HEAD_DIM)
        qh = _rms(qx[:, sl], gcq_ref[...]).astype(BF16)
        kh = mk_ref[0, :, sl].astype(BF16)
        vh = mv_ref[0, :, sl].astype(BF16)
        sc = lax.dot_general(qh, kh, (((1,), (1,)), ((), ())),
                             preferred_element_type=F32) * (CA_HEAD_DIM ** -0.5)
        p = jnp.exp(sc - jnp.max(sc, axis=-1, keepdims=True))
        p = p / jnp.sum(p, axis=-1, keepdims=True)
        heads.append(_mm(p.astype(BF16), vh))
    o = jnp.concatenate(heads, axis=1).astype(BF16)
    x2 = x1 + _mm(o, wco_ref[...])
    x2_ref[0] = x2

    hn = _rms(x2, gffn_ref[...])
    hn_ref[0] = hn

    h_hi = hn.astype(BF16)
    h_lo = (hn - h_hi.astype(F32)).astype(BF16)
    r1 = _mm(h_hi, wr_ref[...])
    lg = (r1[:, :LANES] + r1[:, LANES:] + _mm(h_lo, wr_ref[:, 0:LANES])
          + br_ref[...])

    col = lax.broadcasted_iota(jnp.int32, (tm, LANES), 1)
    big = jnp.int32(4 * LANES)
    gmask = col < N_EXPERT_GROUPS
    lgg = jnp.where(gmask, lg, NEG)
    mg = jnp.max(lgg, axis=-1, keepdims=True)
    grp = jnp.min(jnp.where(gmask & (lgg == mg), col, big), axis=-1, keepdims=True)
    pg_top = 1.0 / jnp.sum(jnp.where(gmask, jnp.exp(lgg - mg), 0.0), axis=-1, keepdims=True)

    ecol = col - ROUTER_COL0
    emask = ((ecol >= 0) & (ecol < N_EXPERTS)
             & (lax.shift_right_arithmetic(ecol, 3) == grp))
    le = jnp.where(emask, lg, NEG)
    m1 = jnp.max(le, axis=-1, keepdims=True)
    i1 = jnp.min(jnp.where(emask & (le == m1), col, big), axis=-1, keepdims=True)
    rest = emask & (col != i1)
    le2 = jnp.where(rest, lg, NEG)
    m2 = jnp.max(le2, axis=-1, keepdims=True)
    i2 = jnp.min(jnp.where(rest & (le2 == m2), col, big), axis=-1, keepdims=True)
    den = jnp.sum(jnp.where(emask, jnp.exp(le - m1), 0.0), axis=-1, keepdims=True)
    p1 = 1.0 / den
    p2 = jnp.exp(m2 - m1) / den
    gate1 = pg_top * p1 / (p1 + p2)
    gate2 = pg_top * p2 / (p1 + p2)

    sel1 = col == i1
    sel2 = col == i2
    oh = jnp.where(sel1 | sel2, 1.0, 0.0)
    r_i = lax.broadcasted_iota(jnp.int32, (tm, tm), 0)
    c_i = lax.broadcasted_iota(jnp.int32, (tm, tm), 1)
    tri = jnp.where(r_i > c_i, 1.0, 0.0).astype(BF16)
    tot = base_s[...] + _mm(tri, oh.astype(BF16))
    rank1 = jnp.sum(jnp.where(sel1, tot, 0.0), axis=-1, keepdims=True)
    rank2 = jnp.sum(jnp.where(sel2, tot, 0.0), axis=-1, keepdims=True)
    base_s[...] = base_s[...] + jnp.sum(oh, axis=0, keepdims=True)
    cnt_ref[...] = base_s[...]

    e1 = (i1 - ROUTER_COL0).astype(F32)
    e2 = (i2 - ROUTER_COL0).astype(F32)
    rt = jnp.zeros((tm, LANES), F32)
    for k, val in enumerate((e1, e2, gate1, gate2, rank1, rank2)):
        rt = jnp.where(col == k, val, rt)
    rt_ref[0] = rt
    rtt_ref[0] = rt.T[0:8, :]


def _mid(x, att, ssm_tm, mk, mv, wp, tm):
    B, S, _ = x.shape
    c2 = lambda b, i: (0, 0)
    tile = lambda w: pl.BlockSpec((1, tm, w), lambda b, i: (b, i, 0))
    return pl.pallas_call(
        _mid_kernel,
        grid=(B, S // tm),
        in_specs=[
            tile(D_MODEL), tile(ATT_WIDTH),
            pl.BlockSpec((tm, SSM_WIDTH), lambda b, i: (i, b)),
            pl.BlockSpec((1, N_MEM, CA_WIDTH), lambda b, i: (b, 0, 0)),
            pl.BlockSpec((1, N_MEM, CA_WIDTH), lambda b, i: (b, 0, 0)),
            pl.BlockSpec((1, ATT_WIDTH), c2),
            pl.BlockSpec((1, SSM_WIDTH), c2),
            pl.BlockSpec((ATT_WIDTH + SSM_WIDTH, D_MODEL), c2),
            pl.BlockSpec((1, D_MODEL), c2),
            pl.BlockSpec((D_MODEL, CA_WIDTH), c2),
            pl.BlockSpec((1, CA_HEAD_DIM), c2),
            pl.BlockSpec((CA_WIDTH, D_MODEL), c2),
            pl.BlockSpec((1, D_MODEL), c2),
            pl.BlockSpec((D_MODEL, 2 * LANES), c2),
            pl.BlockSpec((1, LANES), c2),
        ],
        out_specs=[
            tile(D_MODEL), tile(D_MODEL), tile(LANES),
            pl.BlockSpec((1, 8, tm), lambda b, i: (b, 0, i)),
            pl.BlockSpec((1, LANES), c2),
        ],
        out_shape=[
            jax.ShapeDtypeStruct((B, S, D_MODEL), F32),
            jax.ShapeDtypeStruct((B, S, D_MODEL), F32),
            jax.ShapeDtypeStruct((B, S, LANES), F32),
            jax.ShapeDtypeStruct((B, 8, S), F32),
            jax.ShapeDtypeStruct((1, LANES), F32),
        ],
        scratch_shapes=[pltpu.VMEM((1, LANES), F32)],
        compiler_params=pltpu.CompilerParams(
            dimension_semantics=("arbitrary", "arbitrary"),
            vmem_limit_bytes=VMEM_LIMIT),
        name="mid",
    )(x, att, ssm_tm, mk, mv, wp["gao"], wp["gso"], wp["wout"], wp["gx"], wp["wcq"],
      wp["gcq"], wp["wco"], wp["gffn"], wp["wr"], wp["br"])


def _moe_kernel(be_ref, nu_ref, sid_prev_ref, sid_ref, sid_next_ref, hn_hbm,
                wg_ref, wu_ref, wd_ref, ys_hbm,
                xbuf, ybuf, wg_s, wu_s, wd_s, gsem, ssem, *, tok_mask):
    i = pl.program_id(0)
    last = pl.num_programs(0) - 1
    n_used = nu_ref[0]
    slot = lax.rem(i, 2)
    other = 1 - slot

    def gather(sid, dst, sem, lo, hi):
        for r in range(lo, hi):
            tok = jnp.bitwise_and(sid[0, 0, r], tok_mask)
            pltpu.make_async_copy(hn_hbm.at[pl.ds(tok, 1), :], dst.at[pl.ds(r, 1), :],
                                  sem).start()

    def scatter(sid, src, sem, lo, hi):
        for r in range(lo, hi):
            pltpu.make_async_copy(src.at[pl.ds(r, 1), :], ys_hbm.at[pl.ds(sid[0, 0, r], 1), :],
                                  sem).start()

    def wait_gather(s):
        pltpu.make_async_copy(hn_hbm.at[pl.ds(0, MOE_BLOCK), :], xbuf.at[s], gsem.at[s]).wait()

    def wait_scatter(s):
        pltpu.make_async_copy(ybuf.at[s], ys_hbm.at[pl.ds(0, MOE_BLOCK), :], ssem.at[s]).wait()

    def compute(s, between):
        xe = xbuf[s].astype(BF16)
        between[0]()
        g = _mm(xe, wg_s[...])
        between[1]()
        u = _mm(xe, wu_s[...])
        between[2]()
        hmid = ((g * (1.0 / (1.0 + jnp.exp(-g)))) * u).astype(BF16)
        y = _mm(hmid, wd_s[...])
        between[3]()
        ybuf[s] = y

    half = MOE_BLOCK // 2

    @pl.when(i == 0)
    def _():
        gather(sid_ref, xbuf.at[0], gsem.at[0], 0, MOE_BLOCK)
        ybuf[1] = jnp.zeros((MOE_BLOCK, D_MODEL), F32)
        fill = pltpu.make_async_copy(
            ybuf.at[1], ys_hbm.at[pl.ds(2 * (tok_mask + 1), MOE_BLOCK), :], ssem.at[1])
        fill.start()
        fill.wait()

    wait_gather(slot)

    @pl.when((i >= 2) & (i <= n_used + 1))
    def _():
        wait_scatter(slot)

    @pl.when(i < n_used)
    def _():
        @pl.when((i == 0) | (be_ref[i] != be_ref[jnp.maximum(i - 1, 0)]))
        def _():
            wg_s[...] = wg_ref[0].astype(BF16)
            wu_s[...] = wu_ref[0].astype(BF16)
            wd_s[...] = wd_ref[0].astype(BF16)

        nothing = lambda: None

        @pl.when(i == 0)
        def _():
            compute(0, (
                lambda: gather(sid_next_ref, xbuf.at[1], gsem.at[1], 0, half),
                lambda: gather(sid_next_ref, xbuf.at[1], gsem.at[1], half, MOE_BLOCK),
                nothing, nothing))

        @pl.when(i >= 1)
        def _():
            xo, yo = xbuf.at[other], ybuf.at[other]
            compute(slot, (
                lambda: gather(sid_next_ref, xo, gsem.at[other], 0, half),
                lambda: gather(sid_next_ref, xo, gsem.at[other], half, MOE_BLOCK),
                lambda: scatter(sid_prev_ref, yo, ssem.at[other], 0, half),
                lambda: scatter(sid_prev_ref, yo, ssem.at[other], half, MOE_BLOCK)))

    @pl.when(i >= n_used)
    def _():
        @pl.when((i == n_used) & (i >= 1))
        def _():
            scatter(sid_prev_ref, ybuf.at[other], ssem.at[other], 0, MOE_BLOCK)

            @pl.when(i == last)
            def _():
                wait_scatter(other)

        @pl.when(i < last)
        def _():
            gather(sid_next_ref, xbuf.at[other], gsem.at[other], 0, MOE_BLOCK)


def _moe(block_e, n_used, row_sid, hn2d, w_gate, w_up, w_down):
    n_blocks = block_e.shape[0]
    T = hn2d.shape[0]
    assert T & (T - 1) == 0 and T >= MOE_BLOCK
    sid3 = row_sid.reshape(n_blocks, 1, MOE_BLOCK)
    sid_spec = lambda f: pl.BlockSpec((1, 1, MOE_BLOCK), lambda i, be, nu: (f(i), 0, 0),
                                      memory_space=pltpu.SMEM)
    grid_spec = pltpu.PrefetchScalarGridSpec(
        num_scalar_prefetch=2,
        grid=(n_blocks,),
        in_specs=[
            sid_spec(lambda i: jnp.maximum(i - 1, 0)),
            sid_spec(lambda i: i),
            sid_spec(lambda i: jnp.minimum(i + 1, n_blocks - 1)),
            pl.BlockSpec(memory_space=pl.ANY),
            pl.BlockSpec((1, D_MODEL, D_EXPERT), lambda i, be, nu: (be[i], 0, 0)),
            pl.BlockSpec((1, D_MODEL, D_EXPERT), lambda i, be, nu: (be[i], 0, 0)),
            pl.BlockSpec((1, D_EXPERT, D_MODEL), lambda i, be, nu: (be[i], 0, 0)),
        ],
        out_specs=pl.BlockSpec(memory_space=pl.ANY),
        scratch_shapes=[
            pltpu.VMEM((2, MOE_BLOCK, D_MODEL), F32),
            pltpu.VMEM((2, MOE_BLOCK, D_MODEL), F32),
            pltpu.VMEM((D_MODEL, D_EXPERT), BF16),
            pltpu.VMEM((D_MODEL, D_EXPERT), BF16),
            pltpu.VMEM((D_EXPERT, D_MODEL), BF16),
            pltpu.SemaphoreType.DMA((2,)),
            pltpu.SemaphoreType.DMA((2,)),
        ],
    )
    return pl.pallas_call(
        functools.partial(_moe_kernel, tok_mask=T - 1),
        grid_spec=grid_spec,
        out_shape=jax.ShapeDtypeStruct((2 * T + MOE_BLOCK, D_MODEL), F32),
        compiler_params=pltpu.CompilerParams(
            dimension_semantics=("arbitrary",), vmem_limit_bytes=VMEM_LIMIT),
        name="moe",
    )(block_e, n_used, sid3, sid3, sid3, hn2d, w_gate, w_up, w_down)


def _combine_kernel(x2_ref, rt_ref, y0_ref, y1_ref, o_ref):
    rt = rt_ref[...]
    o_ref[...] = x2_ref[...] + rt[:, 2:3] * y0_ref[...] + rt[:, 3:4] * y1_ref[...]


def _combine(x2, rt, ys, tm):
    T = x2.shape[0]
    nt = T // tm
    return pl.pallas_call(
        _combine_kernel,
        grid=(nt,),
        in_specs=[
            pl.BlockSpec((tm, D_MODEL), lambda i: (i, 0)),
            pl.BlockSpec((tm, LANES), lambda i: (i, 0)),
            pl.BlockSpec((tm, D_MODEL), lambda i: (i, 0)),
            pl.BlockSpec((tm, D_MODEL), lambda i: (nt + i, 0)),
        ],
        out_specs=pl.BlockSpec((tm, D_MODEL), lambda i: (i, 0)),
        out_shape=jax.ShapeDtypeStruct((T, D_MODEL), F32),
        compiler_params=pltpu.CompilerParams(
            dimension_semantics=("arbitrary",), vmem_limit_bytes=VMEM_LIMIT),
        name="combine",
    )(x2, rt, ys, ys)


def _hier_moe(x2, hn, rt, rtt, cnt, w_gate, w_up, w_down, tm):
    T = x2.shape[0]
    counts = cnt[0, ROUTER_COL0:ROUTER_COL0 + N_EXPERTS].astype(jnp.int32)
    padded = (counts + MOE_BLOCK - 1) // MOE_BLOCK * MOE_BLOCK
    pad_end = jnp.cumsum(padded)
    pad_start = pad_end - padded
    flat = lambda a: jnp.swapaxes(a, 0, 1).reshape(a.shape[1], T)
    eid = flat(rtt[:, 0:2, :]).astype(jnp.int32)
    rank = flat(rtt[:, 4:6, :]).astype(jnp.int32)
    experts = jnp.arange(N_EXPERTS, dtype=jnp.int32)[:, None, None]
    dest = rank + jnp.sum(jnp.where(eid[None] == experts, pad_start[:, None, None], 0), axis=0)
    n_blocks = (2 * T + N_EXPERTS * (MOE_BLOCK - 1)) // MOE_BLOCK + 1
    rows = n_blocks * MOE_BLOCK
    sid = jnp.arange(2 * T, dtype=jnp.int32)
    trash = 2 * T + jnp.arange(rows, dtype=jnp.int32) % MOE_BLOCK
    row_sid = trash.at[dest.reshape(-1)].set(sid, unique_indices=True)
    blk_row0 = jnp.arange(n_blocks, dtype=jnp.int32) * MOE_BLOCK
    block_e = jnp.minimum(
        jnp.sum((pad_end[None, :] <= blk_row0[:, None]).astype(jnp.int32), axis=1),
        N_EXPERTS - 1)
    n_used = (pad_end[-1] // MOE_BLOCK).astype(jnp.int32).reshape(1)
    ys = _moe(block_e, n_used, row_sid, hn, w_gate, w_up, w_down)
    return _combine(x2, rt, ys, tm)


def _rope_table(pos):
    half = ROPE_DIM // 2
    inv = ROPE_THETA ** (-jnp.arange(0, ROPE_DIM, 2, dtype=F32) / ROPE_DIM)
    ang = pos.astype(F32)[:, None] * inv[None, :]
    cos, sin = jnp.cos(ang), jnp.sin(ang)
    L = pos.shape[0]
    pad = jnp.zeros((L, HEAD_DIM - ROPE_DIM), F32)
    zero = jnp.zeros((L, half), F32)
    c64 = jnp.concatenate([cos, cos, pad + 1.0], axis=1)
    lo64 = jnp.concatenate([-sin, zero, pad], axis=1)
    hi64 = jnp.concatenate([zero, sin, pad], axis=1)
    two = lambda t: jnp.concatenate([t, t], axis=1)
    return jnp.concatenate([two(c64), two(lo64), two(hi64)], axis=1)


def _layer(x, pos_rope, kctx_prev, vctx_prev, h0r, h0i, mk, mv, wp, sp, ew, *,
           tm_in, tq, ssm_l, tm_mid, tm_comb, mask_context):
    B, S, _ = x.shape
    T = B * S
    q, k3, v3, u_tm = _in_proj(x, wp["gmix"], wp["win"], wp["gq"], wp["gk"], pos_rope, tm_in)
    kctx = jnp.concatenate([kctx_prev, k3], axis=1)
    vctx = jnp.concatenate([vctx_prev, v3], axis=1)
    att = _swa(wp["sink"], q, kctx, vctx, tq, mask_context)
    ssm_tm, hr, hi = _ssm(u_tm.reshape(S, B, SSM_WIDTH), h0r, h0i, sp, ssm_l)
    x2, hn, rt, rtt, cnt = _mid(x, att, ssm_tm.reshape(S, B * SSM_WIDTH), mk, mv, wp, tm_mid)
    y = _hier_moe(x2.reshape(T, D_MODEL), hn.reshape(T, D_MODEL),
                  rt.reshape(T, LANES), rtt, cnt, *ew, tm_comb)
    return y.reshape(B, S, D_MODEL), k3, v3, hr, hi


def kernel(x_prompt, x_sample, cache_attn_k, cache_attn_v, state_ssm_re, state_ssm_im, cache_mem_k, cache_mem_v, mem_prompt, norm_mix, w_in, q_norm, k_norm, attn_sink, ssm_lambda_re, ssm_lambda_im, ssm_log_dt, ssm_b_re, ssm_b_im, ssm_c_re, ssm_c_im, ssm_d, ssm_w_glu, ssm_b_glu, norm_attn_out, norm_ssm_out, w_out, norm_cross, norm_mem, w_cq, w_ck, w_cv, cq_norm, ck_norm, w_co, norm_ffn, w_router_group, b_router_group, w_router_expert, b_router_expert, w_e_gate, w_e_up, w_e_down):
    depth = norm_mix.shape[0]
    Bp, Lp, _ = x_prompt.shape
    Bs, Ls, _ = x_sample.shape
    yp, ys = x_prompt, x_sample
    rope_p = _rope_table(jnp.arange(Lp, dtype=jnp.int32))
    rope_s = _rope_table(PAST_LEN + jnp.arange(Ls, dtype=jnp.int32))
    outs = [[] for _ in range(10)]
    n_router = N_EXPERT_GROUPS + N_EXPERTS
    for l in range(depth):
        row = lambda a: a[l].astype(F32).reshape(1, -1)
        w_r = jnp.pad(jnp.concatenate([w_router_group[l], w_router_expert[l]], axis=1).astype(F32),
                      ((0, 0), (0, LANES - n_router)))
        w_r_hi = w_r.astype(BF16)
        w_r_lo = (w_r - w_r_hi.astype(F32)).astype(BF16)
        b_r = jnp.pad(jnp.concatenate([b_router_group[l], b_router_expert[l]]).astype(F32),
                      (0, LANES - n_router)).reshape(1, LANES)
        wp = {
            "gmix": row(norm_mix), "win": w_in[l].astype(BF16),
            "gq": jnp.tile(row(q_norm), (1, LANES // HEAD_DIM)),
            "gk": jnp.tile(row(k_norm), (1, LANES // HEAD_DIM)),
            "sink": attn_sink[l].astype(F32),
            "gao": row(norm_attn_out), "gso": row(norm_ssm_out),
            "wout": w_out[l].astype(BF16), "gx": row(norm_cross),
            "wcq": w_cq[l].astype(BF16), "gcq": row(cq_norm),
            "wco": w_co[l].astype(BF16), "gffn": row(norm_ffn),
            "wr": jnp.concatenate([w_r_hi, w_r_lo], axis=1), "br": b_r,
        }
        sp = _ssm_params(ssm_lambda_re[l], ssm_lambda_im[l], ssm_log_dt[l], ssm_b_re[l],
                         ssm_b_im[l], ssm_c_re[l], ssm_c_im[l], ssm_d[l], ssm_w_glu[l],
                         ssm_b_glu[l])
        ew = (w_e_gate[l].astype(F32), w_e_up[l].astype(F32), w_e_down[l].astype(F32))

        w_ckv = jnp.concatenate([w_ck[l], w_cv[l]], axis=1).astype(BF16)
        mkp, mvp = _memkv(mem_prompt.reshape(Bp * N_MEM, D_MODEL), row(norm_mem), w_ckv,
                          row(ck_norm), 512)
        mkp = mkp.reshape(Bp, N_MEM, CA_WIDTH)
        mvp = mvp.reshape(Bp, N_MEM, CA_WIDTH)

        zctx = jnp.zeros((Bp, WINDOW, KV_WIDTH), F32)
        zst = jnp.zeros((Bp, SSM_COLS), F32)
        yp, kp, vp, hpr, hpi = _layer(
            yp, rope_p, zctx, zctx, zst, zst, mkp, mvp, wp, sp, ew,
            tm_in=512, tq=256, ssm_l=64, tm_mid=512, tm_comb=512, mask_context=True)
        ys, kn, vn, hsr, hsi = _layer(
            ys, rope_s, cache_attn_k[l].reshape(Bs, WINDOW, KV_WIDTH).astype(F32),
            cache_attn_v[l].reshape(Bs, WINDOW, KV_WIDTH).astype(F32),
            state_ssm_re[l].astype(F32).reshape(Bs, SSM_COLS),
            state_ssm_im[l].astype(F32).reshape(Bs, SSM_COLS),
            cache_mem_k[l].astype(F32).reshape(Bs, N_MEM, CA_WIDTH),
            cache_mem_v[l].astype(F32).reshape(Bs, N_MEM, CA_WIDTH), wp, sp, ew,
            tm_in=Ls, tq=CHUNK, ssm_l=Ls, tm_mid=Ls, tm_comb=256, mask_context=False)

        sg = (N_SSM_GROUPS, SSM_STATE)
        kvs = (N_KV_HEADS, HEAD_DIM)
        vals = (kp[:, Lp - WINDOW:].reshape(Bp, WINDOW, *kvs),
                vp[:, Lp - WINDOW:].reshape(Bp, WINDOW, *kvs),
                hpr.reshape(Bp, *sg), hpi.reshape(Bp, *sg),
                mkp.reshape(Bp, N_MEM, CA_HEADS, CA_HEAD_DIM),
                mvp.reshape(Bp, N_MEM, CA_HEADS, CA_HEAD_DIM),
                kn.reshape(Bs, Ls, *kvs), vn.reshape(Bs, Ls, *kvs),
                hsr.reshape(Bs, *sg), hsi.reshape(Bs, *sg))
        for lst, val in zip(outs, vals):
            lst.append(val)
    return (yp, ys) + tuple(jnp.stack(lst) for lst in outs)
```

```python
import functools
import math

import jax
import jax.numpy as jnp
from jax import lax
from jax.experimental import pallas as pl
from jax.experimental.pallas import tpu as pltpu

F32 = jnp.float32
BF16 = jnp.bfloat16

D_MODEL = 1024
CHUNK = 64
N_Q_HEADS = 8
N_KV_HEADS = 2
GQA = N_Q_HEADS // N_KV_HEADS
HEAD_DIM = 64
WINDOW = 128
BAND = WINDOW + CHUNK
ROPE_DIM = HEAD_DIM // 4
ROPE_THETA = 500000.0
ATT_WIDTH = N_Q_HEADS * HEAD_DIM
KV_WIDTH = N_KV_HEADS * HEAD_DIM
SSM_GROUP = 16
SSM_WIDTH = D_MODEL // 2
N_SSM_GROUPS = SSM_WIDTH // SSM_GROUP
SSM_STATE = 64
SSM_COLS = N_SSM_GROUPS * SSM_STATE
IN_WIDTH = ATT_WIDTH + 2 * KV_WIDTH + SSM_WIDTH
N_MEM = 256
CA_HEADS = 4
CA_HEAD_DIM = 128
CA_WIDTH = CA_HEADS * CA_HEAD_DIM
N_EXPERT_GROUPS = 4
EXPERTS_PER_GROUP = 8
N_EXPERTS = N_EXPERT_GROUPS * EXPERTS_PER_GROUP
D_EXPERT = 512
MOE_BLOCK = 256
EPS = 1e-6
NEG = -1e30
PAST_LEN = 4096

LANES = 128
ROUTER_COL0 = N_EXPERT_GROUPS
VMEM_LIMIT = 48 * 1024 * 1024


def _rms(x, g):
    ms = jnp.mean(x * x, axis=-1, keepdims=True)
    return (x * lax.rsqrt(ms + EPS)) * g


def _mm(a, b):
    return jnp.dot(a, b, preferred_element_type=F32)


def _in_proj_kernel(x_ref, g_ref, w_ref, gq_ref, gk_ref, rope_ref,
                    q_ref, k_ref, v_ref, u_ref):
    tm = x_ref.shape[1]
    h = _rms(x_ref[0], g_ref[...])
    hin = _mm(h.astype(BF16), w_ref[...])
    rope = rope_ref[...]
    cos = rope[:, 0:LANES]
    sin_lo = rope[:, LANES:2 * LANES]
    sin_hi = rope[:, 2 * LANES:3 * LANES]
    lane = lax.broadcasted_iota(jnp.int32, (tm, LANES), 1)
    left = lane < HEAD_DIM

    def norm_rope(z, g):
        sq = z * z
        lsum = jnp.sum(jnp.where(left, sq, 0.0), axis=-1, keepdims=True)
        rsum = jnp.sum(jnp.where(left, 0.0, sq), axis=-1, keepdims=True)
        ms = jnp.where(left, lsum, rsum) * (1.0 / HEAD_DIM)
        zn = (z * lax.rsqrt(ms + EPS)) * g
        half = ROPE_DIM // 2
        return (zn * cos + pltpu.roll(zn, LANES - half, 1) * sin_lo
                + pltpu.roll(zn, half, 1) * sin_hi)

    for j in range(ATT_WIDTH // LANES):
        sl = slice(j * LANES, (j + 1) * LANES)
        q_ref[0, :, sl] = norm_rope(hin[:, sl], gq_ref[...])
    k_ref[0] = norm_rope(hin[:, ATT_WIDTH:ATT_WIDTH + KV_WIDTH], gk_ref[...])
    v_ref[0] = hin[:, ATT_WIDTH + KV_WIDTH:ATT_WIDTH + 2 * KV_WIDTH]
    u_ref[...] = hin[:, ATT_WIDTH + 2 * KV_WIDTH:]


def _in_proj(x, g, w_bf, gq, gk, rope, tm):
    B, S, _ = x.shape
    full = lambda b, i: (0, 0)
    tile = lambda w: pl.BlockSpec((1, tm, w), lambda b, i: (b, i, 0))
    return pl.pallas_call(
        _in_proj_kernel,
        grid=(B, S // tm),
        in_specs=[
            tile(D_MODEL),
            pl.BlockSpec((1, D_MODEL), full),
            pl.BlockSpec((D_MODEL, IN_WIDTH), full),
            pl.BlockSpec((1, LANES), full),
            pl.BlockSpec((1, LANES), full),
            pl.BlockSpec((tm, 3 * LANES), lambda b, i: (i, 0)),
        ],
        out_specs=[
            tile(ATT_WIDTH), tile(KV_WIDTH), tile(KV_WIDTH),
            pl.BlockSpec((tm, SSM_WIDTH), lambda b, i: (i, b)),
        ],
        out_shape=[
            jax.ShapeDtypeStruct((B, S, ATT_WIDTH), F32),
            jax.ShapeDtypeStruct((B, S, KV_WIDTH), F32),
            jax.ShapeDtypeStruct((B, S, KV_WIDTH), F32),
            jax.ShapeDtypeStruct((S, B * SSM_WIDTH), F32),
        ],
        compiler_params=pltpu.CompilerParams(
            dimension_semantics=("arbitrary", "arbitrary"),
            vmem_limit_bytes=VMEM_LIMIT),
        name="in_proj",
    )(x, g, w_bf, gq, gk, rope)


def _swa_kernel(sink_ref, q_ref, k_ref, v_ref, o_ref, *, mask_context):
    tq = q_ref.shape[1]
    i = pl.program_id(1)
    nch = tq // CHUNK
    lane = lax.broadcasted_iota(jnp.int32, (BAND, LANES), 1)
    lo_half = lane < HEAD_DIM
    slabs_per_kv = GQA * HEAD_DIM // LANES

    units = []
    scores = []
    vpads = {}
    valids = {}
    for c in range(nch):
        start = pl.multiple_of((i * nch + c) * CHUNK, CHUNK)
        kb = k_ref[0, pl.ds(start, BAND), :]
        vb = v_ref[0, pl.ds(start, BAND), :]
        kb_sw = pltpu.roll(kb, HEAD_DIM, 1)
        vb_sw = pltpu.roll(vb, HEAD_DIM, 1)
        if mask_context:
            kidx = start + lax.broadcasted_iota(jnp.int32, (1, BAND), 1)
            valids[c] = kidx >= WINDOW
        for kvh in range(N_KV_HEADS):
            k_own, k_oth = (kb, kb_sw) if kvh == 0 else (kb_sw, kb)
            v_own, v_oth = (vb, vb_sw) if kvh == 0 else (vb_sw, vb)
            kpad = (jnp.where(lo_half, k_own, 0.0).astype(BF16),
                    jnp.where(lo_half, 0.0, k_oth).astype(BF16))
            vpads[(c, kvh)] = (jnp.where(lo_half, v_own, 0.0).astype(BF16),
                               jnp.where(lo_half, 0.0, v_oth).astype(BF16))
            col0 = kvh * GQA * HEAD_DIM
            q2 = jnp.concatenate(
                [q_ref[0, c * CHUNK:(c + 1) * CHUNK, col0 + m * LANES:col0 + (m + 1) * LANES]
                 for m in range(slabs_per_kv)], axis=0).astype(BF16)
            for side in range(2):
                s = lax.dot_general(q2, kpad[side], (((1,), (1,)), ((), ())),
                                    preferred_element_type=F32) * (HEAD_DIM ** -0.5)
                if mask_context:
                    s = jnp.where(valids[c], s, NEG)
                units.append((c, kvh, side))
                scores.append(s)

    s_all = jnp.concatenate(scores, axis=0)
    sk = jnp.concatenate(
        [jnp.full((CHUNK, 1), sink_ref[kvh * GQA + 2 * m + side], F32)
         for (_, kvh, side) in units for m in range(slabs_per_kv)], axis=0)
    mx = jnp.maximum(jnp.max(s_all, axis=-1, keepdims=True), sk)
    p_all = jnp.exp(s_all - mx)
    den = jnp.sum(p_all, axis=-1, keepdims=True) + jnp.exp(sk - mx)
    p_all = (p_all / den).astype(BF16)
    rows_u = slabs_per_kv * CHUNK
    probs = [p_all[n * rows_u:(n + 1) * rows_u] for n in range(len(units))]

    for n in range(0, len(units), 2):
        c, kvh, _ = units[n]
        vp = vpads[(c, kvh)]
        o = _mm(probs[n], vp[0]) + _mm(probs[n + 1], vp[1])
        col0 = kvh * GQA * HEAD_DIM
        for m in range(slabs_per_kv):
            o_ref[0, c * CHUNK:(c + 1) * CHUNK, col0 + m * LANES:col0 + (m + 1) * LANES] = (
                o[m * CHUNK:(m + 1) * CHUNK])


def _swa(sink, q, kctx, vctx, tq, mask_context):
    B, Sq, _ = q.shape
    Sk = kctx.shape[1]
    return pl.pallas_call(
        functools.partial(_swa_kernel, mask_context=mask_context),
        grid=(B, Sq // tq),
        in_specs=[
            pl.BlockSpec(memory_space=pltpu.SMEM),
            pl.BlockSpec((1, tq, ATT_WIDTH), lambda b, i: (b, i, 0)),
            pl.BlockSpec((1, Sk, KV_WIDTH), lambda b, i: (b, 0, 0)),
            pl.BlockSpec((1, Sk, KV_WIDTH), lambda b, i: (b, 0, 0)),
        ],
        out_specs=pl.BlockSpec((1, tq, ATT_WIDTH), lambda b, i: (b, i, 0)),
        out_shape=jax.ShapeDtypeStruct((B, Sq, ATT_WIDTH), F32),
        compiler_params=pltpu.CompilerParams(
            dimension_semantics=("arbitrary", "arbitrary"),
            vmem_limit_bytes=VMEM_LIMIT),
        name="swa",
    )(sink, q, kctx, vctx)


def _ssm_kernel(u_ref, h0r_ref, h0i_ref, lam_ref, bre_ref, bim_ref, cre_ref, cim_ref,
                d_ref, wglu_ref, bglu_ref,
                y_ref, hr_out, hi_out, sr, si, hr_s, hi_s):
    L, B, _ = u_ref.shape
    rows = L * B
    half_w = SSM_WIDTH // 2
    half_c = SSM_COLS // 2

    @pl.when(pl.program_id(0) == 0)
    def _():
        hr_s[...] = h0r_ref[...]
        hi_s[...] = h0i_ref[...]

    u = u_ref[...].reshape(rows, SSM_WIDTH)
    ub = u.astype(BF16)
    for hf in range(2):
        uh = ub[:, hf * half_w:(hf + 1) * half_w]
        sr[:, hf * half_c:(hf + 1) * half_c] = _mm(uh, bre_ref[hf])
        si[:, hf * half_c:(hf + 1) * half_c] = _mm(uh, bim_ref[hf])

    cw = 4 * LANES
    for cc in range(SSM_COLS // cw):
        cols = slice(cc * cw, (cc + 1) * cw)
        lr = jnp.broadcast_to(lam_ref[0:1, cols], (B, cw))
        li = jnp.broadcast_to(lam_ref[1:2, cols], (B, cw))

        def body(t, carry):
            hr, hi = carry
            at_t = pl.ds(pl.multiple_of(t * B, B), B)
            nr = lr * hr - li * hi + sr[at_t, cols]
            ni = lr * hi + li * hr + si[at_t, cols]
            sr[at_t, cols] = nr
            si[at_t, cols] = ni
            return nr, ni

        hr, hi = lax.fori_loop(0, L, body, (hr_s[:, cols], hi_s[:, cols]), unroll=2)
        hr_s[:, cols] = hr
        hi_s[:, cols] = hi

    ys = []
    for hf in range(2):
        cs = slice(hf * half_c, (hf + 1) * half_c)
        ys.append(_mm(sr[:, cs].astype(BF16), cre_ref[hf])
                  + _mm(si[:, cs].astype(BF16), cim_ref[hf]))
    y = jnp.concatenate(ys, axis=1) + d_ref[...] * u
    g = 0.5 * y * (1.0 + jnp.tanh(math.sqrt(2.0 / math.pi) * (y + 0.044715 * (y * y * y))))
    gb = g.astype(BF16)
    z = jnp.concatenate(
        [_mm(gb[:, hf * half_w:(hf + 1) * half_w], wglu_ref[hf]) for hf in range(2)],
        axis=1) + bglu_ref[...]
    out = g * (1.0 / (1.0 + jnp.exp(-z)))
    y_ref[...] = out.reshape(L, B, SSM_WIDTH)
    hr_out[...] = hr_s[...]
    hi_out[...] = hi_s[...]


def _ssm(u, h0r, h0i, sp, L):
    S, B, _ = u.shape
    c2 = lambda i: (0, 0)
    c3 = lambda i: (0, 0, 0)
    return pl.pallas_call(
        _ssm_kernel,
        grid=(S // L,),
        in_specs=[
            pl.BlockSpec((L, B, SSM_WIDTH), lambda i: (i, 0, 0)),
            pl.BlockSpec((B, SSM_COLS), c2),
            pl.BlockSpec((B, SSM_COLS), c2),
            pl.BlockSpec((2, SSM_COLS), c2),
            pl.BlockSpec((2, SSM_WIDTH // 2, SSM_COLS // 2), c3),
            pl.BlockSpec((2, SSM_WIDTH // 2, SSM_COLS // 2), c3),
            pl.BlockSpec((2, SSM_COLS // 2, SSM_WIDTH // 2), c3),
            pl.BlockSpec((2, SSM_COLS // 2, SSM_WIDTH // 2), c3),
            pl.BlockSpec((1, SSM_WIDTH), c2),
            pl.BlockSpec((2, SSM_WIDTH // 2, SSM_WIDTH // 2), c3),
            pl.BlockSpec((1, SSM_WIDTH), c2),
        ],
        out_specs=[
            pl.BlockSpec((L, B, SSM_WIDTH), lambda i: (i, 0, 0)),
            pl.BlockSpec((B, SSM_COLS), c2),
            pl.BlockSpec((B, SSM_COLS), c2),
        ],
        out_shape=[
            jax.ShapeDtypeStruct((S, B, SSM_WIDTH), F32),
            jax.ShapeDtypeStruct((B, SSM_COLS), F32),
            jax.ShapeDtypeStruct((B, SSM_COLS), F32),
        ],
        scratch_shapes=[
            pltpu.VMEM((L * B, SSM_COLS), F32),
            pltpu.VMEM((L * B, SSM_COLS), F32),
            pltpu.VMEM((B, SSM_COLS), F32),
            pltpu.VMEM((B, SSM_COLS), F32),
        ],
        compiler_params=pltpu.CompilerParams(
            dimension_semantics=("arbitrary",), vmem_limit_bytes=VMEM_LIMIT),
        name="ssm",
    )(u, h0r, h0i, sp["lam"], sp["bre"], sp["bim"], sp["cre"], sp["cim"],
      sp["d"], sp["wglu"], sp["bglu"])


def _block_diag(blocks):
    G, r, c = blocks.shape
    eye = jnp.eye(G, dtype=blocks.dtype)
    return jnp.einsum("grc,gh->grhc", blocks, eye).reshape(G * r, G * c)


def _ssm_params(lam_re, lam_im, log_dt, b_re, b_im, c_re, c_im, d, w_glu, b_glu):
    lam = lax.complex(lam_re.astype(F32), lam_im.astype(F32))
    dt = jnp.exp(log_dt.astype(F32))[:, None]
    lam_bar = jnp.exp(lam * dt)
    bmat = lax.complex(b_re.astype(F32), b_im.astype(F32))
    b_bar = ((lam_bar - 1.0) / lam)[..., None] * bmat
    lam2 = jnp.stack([lam_bar.real.reshape(-1), lam_bar.imag.reshape(-1)])
    bt = jnp.swapaxes(b_bar, 1, 2)
    hw, hc = SSM_WIDTH // 2, SSM_COLS // 2
    split_b = lambda m: jnp.stack([m[:hw, :hc], m[hw:, hc:]]).astype(BF16)
    split_c = lambda m: jnp.stack([m[:hc, :hw], m[hc:, hw:]]).astype(BF16)
    ct_re = jnp.swapaxes(c_re.astype(F32), 1, 2)
    ct_im = jnp.swapaxes(c_im.astype(F32), 1, 2)
    wg = _block_diag(w_glu.astype(F32))
    return {
        "lam": lam2,
        "bre": split_b(_block_diag(bt.real)),
        "bim": split_b(_block_diag(bt.imag)),
        "cre": split_c(_block_diag(ct_re)),
        "cim": split_c(_block_diag(-ct_im)),
        "d": d.astype(F32).reshape(1, SSM_WIDTH),
        "wglu": jnp.stack([wg[:hw, :hw], wg[hw:, hw:]]).astype(BF16),
        "bglu": b_glu.astype(F32).reshape(1, SSM_WIDTH),
    }


def _memkv_kernel(m_ref, g_ref, w_ref, gk_ref, k_ref, v_ref):
    m = _rms(m_ref[...], g_ref[...])
    kv = _mm(m.astype(BF16), w_ref[...])
    for h in range(CA_HEADS):
        sl = slice(h * CA_HEAD_DIM, (h + 1) * CA_HEAD_DIM)
        k_ref[:, sl] = _rms(kv[:, sl], gk_ref[...])
    v_ref[...] = kv[:, CA_WIDTH:]


def _memkv(mem2d, g, w_bf, gk, tm):
    T = mem2d.shape[0]
    full = lambda i: (0, 0)
    return pl.pallas_call(
        _memkv_kernel,
        grid=(T // tm,),
        in_specs=[
            pl.BlockSpec((tm, D_MODEL), lambda i: (i, 0)),
            pl.BlockSpec((1, D_MODEL), full),
            pl.BlockSpec((D_MODEL, 2 * CA_WIDTH), full),
            pl.BlockSpec((1, CA_HEAD_DIM), full),
        ],
        out_specs=[
            pl.BlockSpec((tm, CA_WIDTH), lambda i: (i, 0)),
            pl.BlockSpec((tm, CA_WIDTH), lambda i: (i, 0)),
        ],
        out_shape=[
            jax.ShapeDtypeStruct((T, CA_WIDTH), F32),
            jax.ShapeDtypeStruct((T, CA_WIDTH), F32),
        ],
        compiler_params=pltpu.CompilerParams(
            dimension_semantics=("arbitrary",), vmem_limit_bytes=VMEM_LIMIT),
        name="memkv",
    )(mem2d, g, w_bf, gk)


def _mid_kernel(x_ref, att_ref, ssm_ref, mk_ref, mv_ref,
                gao_ref, gso_ref, wout_ref, gx_ref, wcq_ref, gcq_ref, wco_ref,
                gffn_ref, wr_ref, br_ref,
                x2_ref, hn_ref, rt_ref, rtt_ref, cnt_ref, base_s):
    tm = x_ref.shape[1]

    @pl.when((pl.program_id(0) == 0) & (pl.program_id(1) == 0))
    def _():
        base_s[...] = jnp.zeros_like(base_s)

    a = _rms(att_ref[0], gao_ref[...]).astype(BF16)
    s = _rms(ssm_ref[...], gso_ref[...]).astype(BF16)
    x1 = (x_ref[0] + _mm(a, wout_ref[0:ATT_WIDTH, :])
          + _mm(s, wout_ref[ATT_WIDTH:, :]))

    qx = _mm(_rms(x1, gx_ref[...]).astype(BF16), wcq_ref[...])
    heads = []
    for h in range(CA_HEADS):
        sl = slice(h * CA_HEAD_DIM, (h + 1) * CA_HEAD_DIM)
        qh = _rms(qx[:, sl], gcq_ref[...]).astype(BF16)
        kh = mk_ref[0, :, sl].astype(BF16)
        vh = mv_ref[0, :, sl].astype(BF16)
        sc = lax.dot_general(qh, kh, (((1,), (1,)), ((), ())),
                             preferred_element_type=F32) * (CA_HEAD_DIM ** -0.5)
        p = jnp.exp(sc - jnp.max(sc, axis=-1, keepdims=True))
        p = p / jnp.sum(p, axis=-1, keepdims=True)
        heads.append(_mm(p.astype(BF16), vh))
    o = jnp.concatenate(heads, axis=1).astype(BF16)
    x2 = x1 + _mm(o, wco_ref[...])
    x2_ref[0] = x2

    hn = _rms(x2, gffn_ref[...])
    hn_ref[0] = hn

    h_hi = hn.astype(BF16)
    h_lo = (hn - h_hi.astype(F32)).astype(BF16)
    r1 = _mm(h_hi, wr_ref[...])
    lg = (r1[:, :LANES] + r1[:, LANES:] + _mm(h_lo, wr_ref[:, 0:LANES])
          + br_ref[...])

    col = lax.broadcasted_iota(jnp.int32, (tm, LANES), 1)
    big = jnp.int32(4 * LANES)
    gmask = col < N_EXPERT_GROUPS
    lgg = jnp.where(gmask, lg, NEG)
    mg = jnp.max(lgg, axis=-1, keepdims=True)
    grp = jnp.min(jnp.where(gmask & (lgg == mg), col, big), axis=-1, keepdims=True)
    pg_top = 1.0 / jnp.sum(jnp.where(gmask, jnp.exp(lgg - mg), 0.0), axis=-1, keepdims=True)

    ecol = col - ROUTER_COL0
    emask = ((ecol >= 0) & (ecol < N_EXPERTS)
             & (lax.shift_right_arithmetic(ecol, 3) == grp))
    le = jnp.where(emask, lg, NEG)
    m1 = jnp.max(le, axis=-1, keepdims=True)
    i1 = jnp.min(jnp.where(emask & (le == m1), col, big), axis=-1, keepdims=True)
    rest = emask & (col != i1)
    le2 = jnp.where(rest, lg, NEG)
    m2 = jnp.max(le2, axis=-1, keepdims=True)
    i2 = jnp.min(jnp.where(rest & (le2 == m2), col, big), axis=-1, keepdims=True)
    den = jnp.sum(jnp.where(emask, jnp.exp(le - m1), 0.0), axis=-1, keepdims=True)
    p1 = 1.0 / den
    p2 = jnp.exp(m2 - m1) / den
    gate1 = pg_top * p1 / (p1 + p2)
    gate2 = pg_top * p2 / (p1 + p2)

    sel1 = col == i1
    sel2 = col == i2
    oh = jnp.where(sel1 | sel2, 1.0, 0.0)
    r_i = lax.broadcasted_iota(jnp.int32, (tm, tm), 0)
    c_i = lax.broadcasted_iota(jnp.int32, (tm, tm), 1)
    tri = jnp.where(r_i > c_i, 1.0, 0.0).astype(BF16)
    tot = base_s[...] + _mm(tri, oh.astype(BF16))
    rank1 = jnp.sum(jnp.where(sel1, tot, 0.0), axis=-1, keepdims=True)
    rank2 = jnp.sum(jnp.where(sel2, tot, 0.0), axis=-1, keepdims=True)
    base_s[...] = base_s[...] + jnp.sum(oh, axis=0, keepdims=True)
    cnt_ref[...] = base_s[...]

    e1 = (i1 - ROUTER_COL0).astype(F32)
    e2 = (i2 - ROUTER_COL0).astype(F32)
    rt = jnp.zeros((tm, LANES), F32)
    for k, val in enumerate((e1, e2, gate1, gate2, rank1, rank2)):
        rt = jnp.where(col == k, val, rt)
    rt_ref[0] = rt
    rtt_ref[0] = rt.T[0:8, :]


def _mid(x, att, ssm_tm, mk, mv, wp, tm):
    B, S, _ = x.shape
    c2 = lambda b, i: (0, 0)
    tile = lambda w: pl.BlockSpec((1, tm, w), lambda b, i: (b, i, 0))
    return pl.pallas_call(
        _mid_kernel,
        grid=(B, S // tm),
        in_specs=[
            tile(D_MODEL), tile(ATT_WIDTH),
            pl.BlockSpec((tm, SSM_WIDTH), lambda b, i: (i, b)),
            pl.BlockSpec((1, N_MEM, CA_WIDTH), lambda b, i: (b, 0, 0)),
            pl.BlockSpec((1, N_MEM, CA_WIDTH), lambda b, i: (b, 0, 0)),
            pl.BlockSpec((1, ATT_WIDTH), c2),
            pl.BlockSpec((1, SSM_WIDTH), c2),
            pl.BlockSpec((ATT_WIDTH + SSM_WIDTH, D_MODEL), c2),
            pl.BlockSpec((1, D_MODEL), c2),
            pl.BlockSpec((D_MODEL, CA_WIDTH), c2),
            pl.BlockSpec((1, CA_HEAD_DIM), c2),
            pl.BlockSpec((CA_WIDTH, D_MODEL), c2),
            pl.BlockSpec((1, D_MODEL), c2),
            pl.BlockSpec((D_MODEL, 2 * LANES), c2),
            pl.BlockSpec((1, LANES), c2),
        ],
        out_specs=[
            tile(D_MODEL), tile(D_MODEL), tile(LANES),
            pl.BlockSpec((1, 8, tm), lambda b, i: (b, 0, i)),
            pl.BlockSpec((1, LANES), c2),
        ],
        out_shape=[
            jax.ShapeDtypeStruct((B, S, D_MODEL), F32),
            jax.ShapeDtypeStruct((B, S, D_MODEL), F32),
            jax.ShapeDtypeStruct((B, S, LANES), F32),
            jax.ShapeDtypeStruct((B, 8, S), F32),
            jax.ShapeDtypeStruct((1, LANES), F32),
        ],
        scratch_shapes=[pltpu.VMEM((1, LANES), F32)],
        compiler_params=pltpu.CompilerParams(
            dimension_semantics=("arbitrary", "arbitrary"),
            vmem_limit_bytes=VMEM_LIMIT),
        name="mid",
    )(x, att, ssm_tm, mk, mv, wp["gao"], wp["gso"], wp["wout"], wp["gx"], wp["wcq"],
      wp["gcq"], wp["wco"], wp["gffn"], wp["wr"], wp["br"])


def _moe_kernel(be_ref, nu_ref, sid_prev_ref, sid_ref, sid_next_ref, hn_hbm,
                wg_ref, wu_ref, wd_ref, ys_hbm,
                xbuf, ybuf, wg_s, wu_s, wd_s, gsem, ssem, *, tok_mask):
    i = pl.program_id(0)
    last = pl.num_programs(0) - 1
    n_used = nu_ref[0]
    slot = lax.rem(i, 2)
    other = 1 - slot

    def gather(sid, dst, sem, lo, hi):
        for r in range(lo, hi):
            tok = jnp.bitwise_and(sid[0, 0, r], tok_mask)
            pltpu.make_async_copy(hn_hbm.at[pl.ds(tok, 1), :], dst.at[pl.ds(r, 1), :],
                                  sem).start(priority=r % 2)

    def scatter(sid, src, sem, lo, hi):
        for r in range(lo, hi):
            pltpu.make_async_copy(src.at[pl.ds(r, 1), :], ys_hbm.at[pl.ds(sid[0, 0, r], 1), :],
                                  sem).start(priority=r % 2)

    def wait_gather(s):
        pltpu.make_async_copy(hn_hbm.at[pl.ds(0, MOE_BLOCK), :], xbuf.at[s], gsem.at[s]).wait()

    def wait_scatter(s):
        pltpu.make_async_copy(ybuf.at[s], ys_hbm.at[pl.ds(0, MOE_BLOCK), :], ssem.at[s]).wait()

    def compute(s, between):
        xe = xbuf[s].astype(BF16)
        between[0]()
        g = _mm(xe, wg_s[...])
        between[1]()
        u = _mm(xe, wu_s[...])
        between[2]()
        hmid = ((g * (1.0 / (1.0 + jnp.exp(-g)))) * u).astype(BF16)
        y = _mm(hmid, wd_s[...])
        between[3]()
        ybuf[s] = y

    half = MOE_BLOCK // 2

    @pl.when(i == 0)
    def _():
        gather(sid_ref, xbuf.at[0], gsem.at[0], 0, MOE_BLOCK)
        ybuf[1] = jnp.zeros((MOE_BLOCK, D_MODEL), F32)
        fill = pltpu.make_async_copy(
            ybuf.at[1], ys_hbm.at[pl.ds(2 * (tok_mask + 1), MOE_BLOCK), :], ssem.at[1])
        fill.start()
        fill.wait()

    wait_gather(slot)

    @pl.when((i >= 2) & (i <= n_used + 1))
    def _():
        wait_scatter(slot)

    @pl.when(i < n_used)
    def _():
        @pl.when((i == 0) | (be_ref[i] != be_ref[jnp.maximum(i - 1, 0)]))
        def _():
            wg_s[...] = wg_ref[0].astype(BF16)
            wu_s[...] = wu_ref[0].astype(BF16)
            wd_s[...] = wd_ref[0].astype(BF16)

        nothing = lambda: None

        @pl.when(i == 0)
        def _():
            compute(0, (
                lambda: gather(sid_next_ref, xbuf.at[1], gsem.at[1], 0, half),
                lambda: gather(sid_next_ref, xbuf.at[1], gsem.at[1], half, MOE_BLOCK),
                nothing, nothing))

        @pl.when(i >= 1)
        def _():
            xo, yo = xbuf.at[other], ybuf.at[other]
            compute(slot, (
                lambda: gather(sid_next_ref, xo, gsem.at[other], 0, half),
                lambda: gather(sid_next_ref, xo, gsem.at[other], half, MOE_BLOCK),
                lambda: scatter(sid_prev_ref, yo, ssem.at[other], 0, half),
                lambda: scatter(sid_prev_ref, yo, ssem.at[other], half, MOE_BLOCK)))

    @pl.when(i >= n_used)
    def _():
        @pl.when((i == n_used) & (i >= 1))
        def _():
            scatter(sid_prev_ref, ybuf.at[other], ssem.at[other], 0, MOE_BLOCK)

            @pl.when(i == last)
            def _():
                wait_scatter(other)

        @pl.when(i < last)
        def _():
            gather(sid_next_ref, xbuf.at[other], gsem.at[other], 0, MOE_BLOCK)


def _moe(block_e, n_used, row_sid, hn2d, w_gate, w_up, w_down):
    n_blocks = block_e.shape[0]
    T = hn2d.shape[0]
    assert T & (T - 1) == 0 and T >= MOE_BLOCK
    sid3 = row_sid.reshape(n_blocks, 1, MOE_BLOCK)
    sid_spec = lambda f: pl.BlockSpec((1, 1, MOE_BLOCK), lambda i, be, nu: (f(i), 0, 0),
                                      memory_space=pltpu.SMEM)
    grid_spec = pltpu.PrefetchScalarGridSpec(
        num_scalar_prefetch=2,
        grid=(n_blocks,),
        in_specs=[
            sid_spec(lambda i: jnp.maximum(i - 1, 0)),
            sid_spec(lambda i: i),
            sid_spec(lambda i: jnp.minimum(i + 1, n_blocks - 1)),
            pl.BlockSpec(memory_space=pl.ANY),
            pl.BlockSpec((1, D_MODEL, D_EXPERT), lambda i, be, nu: (be[i], 0, 0)),
            pl.BlockSpec((1, D_MODEL, D_EXPERT), lambda i, be, nu: (be[i], 0, 0)),
            pl.BlockSpec((1, D_EXPERT, D_MODEL), lambda i, be, nu: (be[i], 0, 0)),
        ],
        out_specs=pl.BlockSpec(memory_space=pl.ANY),
        scratch_shapes=[
            pltpu.VMEM((2, MOE_BLOCK, D_MODEL), F32),
            pltpu.VMEM((2, MOE_BLOCK, D_MODEL), F32),
            pltpu.VMEM((D_MODEL, D_EXPERT), BF16),
            pltpu.VMEM((D_MODEL, D_EXPERT), BF16),
            pltpu.VMEM((D_EXPERT, D_MODEL), BF16),
            pltpu.SemaphoreType.DMA((2,)),
            pltpu.SemaphoreType.DMA((2,)),
        ],
    )
    return pl.pallas_call(
        functools.partial(_moe_kernel, tok_mask=T - 1),
        grid_spec=grid_spec,
        out_shape=jax.ShapeDtypeStruct((2 * T + MOE_BLOCK, D_MODEL), F32),
        compiler_params=pltpu.CompilerParams(
            dimension_semantics=("arbitrary",), vmem_limit_bytes=VMEM_LIMIT),
        name="moe",
    )(block_e, n_used, sid3, sid3, sid3, hn2d, w_gate, w_up, w_down)


def _combine_kernel(x2_ref, rt_ref, y0_ref, y1_ref, o_ref):
    rt = rt_ref[...]
    o_ref[...] = x2_ref[...] + rt[:, 2:3] * y0_ref[...] + rt[:, 3:4] * y1_ref[...]


def _combine(x2, rt, ys, tm):
    T = x2.shape[0]
    nt = T // tm
    return pl.pallas_call(
        _combine_kernel,
        grid=(nt,),
        in_specs=[
            pl.BlockSpec((tm, D_MODEL), lambda i: (i, 0)),
            pl.BlockSpec((tm, LANES), lambda i: (i, 0)),
            pl.BlockSpec((tm, D_MODEL), lambda i: (i, 0)),
            pl.BlockSpec((tm, D_MODEL), lambda i: (nt + i, 0)),
        ],
        out_specs=pl.BlockSpec((tm, D_MODEL), lambda i: (i, 0)),
        out_shape=jax.ShapeDtypeStruct((T, D_MODEL), F32),
        compiler_params=pltpu.CompilerParams(
            dimension_semantics=("arbitrary",), vmem_limit_bytes=VMEM_LIMIT),
        name="combine",
    )(x2, rt, ys, ys)


def _hier_moe(x2, hn, rt, rtt, cnt, w_gate, w_up, w_down, tm):
    T = x2.shape[0]
    counts = cnt[0, ROUTER_COL0:ROUTER_COL0 + N_EXPERTS].astype(jnp.int32)
    padded = (counts + MOE_BLOCK - 1) // MOE_BLOCK * MOE_BLOCK
    pad_end = jnp.cumsum(padded)
    pad_start = pad_end - padded
    flat = lambda a: jnp.swapaxes(a, 0, 1).reshape(a.shape[1], T)
    eid = flat(rtt[:, 0:2, :]).astype(jnp.int32)
    rank = flat(rtt[:, 4:6, :]).astype(jnp.int32)
    experts = jnp.arange(N_EXPERTS, dtype=jnp.int32)[:, None, None]
    dest = rank + jnp.sum(jnp.where(eid[None] == experts, pad_start[:, None, None], 0), axis=0)
    n_blocks = (2 * T + N_EXPERTS * (MOE_BLOCK - 1)) // MOE_BLOCK + 1
    rows = n_blocks * MOE_BLOCK
    sid = jnp.arange(2 * T, dtype=jnp.int32)
    trash = 2 * T + jnp.arange(rows, dtype=jnp.int32) % MOE_BLOCK
    row_sid = trash.at[dest.reshape(-1)].set(sid, unique_indices=True)
    blk_row0 = jnp.arange(n_blocks, dtype=jnp.int32) * MOE_BLOCK
    block_e = jnp.minimum(
        jnp.sum((pad_end[None, :] <= blk_row0[:, None]).astype(jnp.int32), axis=1),
        N_EXPERTS - 1)
    n_used = (pad_end[-1] // MOE_BLOCK).astype(jnp.int32).reshape(1)
    ys = _moe(block_e, n_used, row_sid, hn, w_gate, w_up, w_down)
    return _combine(x2, rt, ys, tm)


def _rope_table(pos):
    half = ROPE_DIM // 2
    inv = ROPE_THETA ** (-jnp.arange(0, ROPE_DIM, 2, dtype=F32) / ROPE_DIM)
    ang = pos.astype(F32)[:, None] * inv[None, :]
    cos, sin = jnp.cos(ang), jnp.sin(ang)
    L = pos.shape[0]
    pad = jnp.zeros((L, HEAD_DIM - ROPE_DIM), F32)
    zero = jnp.zeros((L, half), F32)
    c64 = jnp.concatenate([cos, cos, pad + 1.0], axis=1)
    lo64 = jnp.concatenate([-sin, zero, pad], axis=1)
    hi64 = jnp.concatenate([zero, sin, pad], axis=1)
    two = lambda t: jnp.concatenate([t, t], axis=1)
    return jnp.concatenate([two(c64), two(lo64), two(hi64)], axis=1)


def _layer(x, pos_rope, kctx_prev, vctx_prev, h0r, h0i, mk, mv, wp, sp, ew, *,
           tm_in, tq, ssm_l, tm_mid, tm_comb, mask_context):
    B, S, _ = x.shape
    T = B * S
    q, k3, v3, u_tm = _in_proj(x, wp["gmix"], wp["win"], wp["gq"], wp["gk"], pos_rope, tm_in)
    kctx = jnp.concatenate([kctx_prev, k3], axis=1)
    vctx = jnp.concatenate([vctx_prev, v3], axis=1)
    att = _swa(wp["sink"], q, kctx, vctx, tq, mask_context)
    ssm_tm, hr, hi = _ssm(u_tm.reshape(S, B, SSM_WIDTH), h0r, h0i, sp, ssm_l)
    x2, hn, rt, rtt, cnt = _mid(x, att, ssm_tm.reshape(S, B * SSM_WIDTH), mk, mv, wp, tm_mid)
    y = _hier_moe(x2.reshape(T, D_MODEL), hn.reshape(T, D_MODEL),
                  rt.reshape(T, LANES), rtt, cnt, *ew, tm_comb)
    return y.reshape(B, S, D_MODEL), k3, v3, hr, hi


def kernel(x_prompt, x_sample, cache_attn_k, cache_attn_v, state_ssm_re, state_ssm_im, cache_mem_k, cache_mem_v, mem_prompt, norm_mix, w_in, q_norm, k_norm, attn_sink, ssm_lambda_re, ssm_lambda_im, ssm_log_dt, ssm_b_re, ssm_b_im, ssm_c_re, ssm_c_im, ssm_d, ssm_w_glu, ssm_b_glu, norm_attn_out, norm_ssm_out, w_out, norm_cross, norm_mem, w_cq, w_ck, w_cv, cq_norm, ck_norm, w_co, norm_ffn, w_router_group, b_router_group, w_router_expert, b_router_expert, w_e_gate, w_e_up, w_e_down):
    depth = norm_mix.shape[0]
    Bp, Lp, _ = x_prompt.shape
    Bs, Ls, _ = x_sample.shape
    yp, ys = x_prompt, x_sample
    rope_p = _rope_table(jnp.arange(Lp, dtype=jnp.int32))
    rope_s = _rope_table(PAST_LEN + jnp.arange(Ls, dtype=jnp.int32))
    outs = [[] for _ in range(10)]
    n_router = N_EXPERT_GROUPS + N_EXPERTS
    for l in range(depth):
        row = lambda a: a[l].astype(F32).reshape(1, -1)
        w_r = jnp.pad(jnp.concatenate([w_router_group[l], w_router_expert[l]], axis=1).astype(F32),
                      ((0, 0), (0, LANES - n_router)))
        w_r_hi = w_r.astype(BF16)
        w_r_lo = (w_r - w_r_hi.astype(F32)).astype(BF16)
        b_r = jnp.pad(jnp.concatenate([b_router_group[l], b_router_expert[l]]).astype(F32),
                      (0, LANES - n_router)).reshape(1, LANES)
        wp = {
            "gmix": row(norm_mix), "win": w_in[l].astype(BF16),
            "gq": jnp.tile(row(q_norm), (1, LANES // HEAD_DIM)),
            "gk": jnp.tile(row(k_norm), (1, LANES // HEAD_DIM)),
            "sink": attn_sink[l].astype(F32),
            "gao": row(norm_attn_out), "gso": row(norm_ssm_out),
            "wout": w_out[l].astype(BF16), "gx": row(norm_cross),
            "wcq": w_cq[l].astype(BF16), "gcq": row(cq_norm),
            "wco": w_co[l].astype(BF16), "gffn": row(norm_ffn),
            "wr": jnp.concatenate([w_r_hi, w_r_lo], axis=1), "br": b_r,
        }
        sp = _ssm_params(ssm_lambda_re[l], ssm_lambda_im[l], ssm_log_dt[l], ssm_b_re[l],
                         ssm_b_im[l], ssm_c_re[l], ssm_c_im[l], ssm_d[l], ssm_w_glu[l],
                         ssm_b_glu[l])
        ew = (w_e_gate[l].astype(F32), w_e_up[l].astype(F32), w_e_down[l].astype(F32))

        w_ckv = jnp.concatenate([w_ck[l], w_cv[l]], axis=1).astype(BF16)
        mkp, mvp = _memkv(mem_prompt.reshape(Bp * N_MEM, D_MODEL), row(norm_mem), w_ckv,
                          row(ck_norm), 512)
        mkp = mkp.reshape(Bp, N_MEM, CA_WIDTH)
        mvp = mvp.reshape(Bp, N_MEM, CA_WIDTH)

        zctx = jnp.zeros((Bp, WINDOW, KV_WIDTH), F32)
        zst = jnp.zeros((Bp, SSM_COLS), F32)
        yp, kp, vp, hpr, hpi = _layer(
            yp, rope_p, zctx, zctx, zst, zst, mkp, mvp, wp, sp, ew,
            tm_in=512, tq=256, ssm_l=64, tm_mid=512, tm_comb=512, mask_context=True)
        ys, kn, vn, hsr, hsi = _layer(
            ys, rope_s, cache_attn_k[l].reshape(Bs, WINDOW, KV_WIDTH).astype(F32),
            cache_attn_v[l].reshape(Bs, WINDOW, KV_WIDTH).astype(F32),
            state_ssm_re[l].astype(F32).reshape(Bs, SSM_COLS),
            state_ssm_im[l].astype(F32).reshape(Bs, SSM_COLS),
            cache_mem_k[l].astype(F32).reshape(Bs, N_MEM, CA_WIDTH),
            cache_mem_v[l].astype(F32).reshape(Bs, N_MEM, CA_WIDTH), wp, sp, ew,
            tm_in=Ls, tq=CHUNK, ssm_l=Ls, tm_mid=Ls, tm_comb=256, mask_context=False)

        sg = (N_SSM_GROUPS, SSM_STATE)
        kvs = (N_KV_HEADS, HEAD_DIM)
        vals = (kp[:, Lp - WINDOW:].reshape(Bp, WINDOW, *kvs),
                vp[:, Lp - WINDOW:].reshape(Bp, WINDOW, *kvs),
                hpr.reshape(Bp, *sg), hpi.reshape(Bp, *sg),
                mkp.reshape(Bp, N_MEM, CA_HEADS, CA_HEAD_DIM),
                mvp.reshape(Bp, N_MEM, CA_HEADS, CA_HEAD_DIM),
                kn.reshape(Bs, Ls, *kvs), vn.reshape(Bs, Ls, *kvs),
                hsr.reshape(Bs, *sg), hsi.reshape(Bs, *sg))
        for lst, val in zip(outs, vals):
            lst.append(val)
    return (yp, ys) + tuple(jnp.stack(lst) for lst in outs)
```

```python
import functools
import math

import jax
import jax.numpy as jnp
from jax import lax
from jax.experimental import pallas as pl
from jax.experimental.pallas import tpu as pltpu

F32 = jnp.float32
BF16 = jnp.bfloat16

D_MODEL = 1024
CHUNK = 64
N_Q_HEADS = 8
N_KV_HEADS = 2
GQA = N_Q_HEADS // N_KV_HEADS
HEAD_DIM = 64
WINDOW = 128
BAND = WINDOW + CHUNK
ROPE_DIM = HEAD_DIM // 4
ROPE_THETA = 500000.0
ATT_WIDTH = N_Q_HEADS * HEAD_DIM
KV_WIDTH = N_KV_HEADS * HEAD_DIM
SSM_GROUP = 16
SSM_WIDTH = D_MODEL // 2
N_SSM_GROUPS = SSM_WIDTH // SSM_GROUP
SSM_STATE = 64
SSM_COLS = N_SSM_GROUPS * SSM_STATE
IN_WIDTH = ATT_WIDTH + 2 * KV_WIDTH + SSM_WIDTH
N_MEM = 256
CA_HEADS = 4
CA_HEAD_DIM = 128
CA_WIDTH = CA_HEADS * CA_HEAD_DIM
N_EXPERT_GROUPS = 4
EXPERTS_PER_GROUP = 8
N_EXPERTS = N_EXPERT_GROUPS * EXPERTS_PER_GROUP
D_EXPERT = 512
MOE_BLOCK = 256
EPS = 1e-6
NEG = -1e30
PAST_LEN = 4096

LANES = 128
ROUTER_COL0 = N_EXPERT_GROUPS
VMEM_LIMIT = 48 * 1024 * 1024


def _rms(x, g):
    ms = jnp.mean(x * x, axis=-1, keepdims=True)
    return (x * lax.rsqrt(ms + EPS)) * g


def _mm(a, b):
    return jnp.dot(a, b, preferred_element_type=F32)


SUBLANES = 8
ROW_TILES = D_MODEL // LANES


def _rows4_shape(n_rows):
    return (n_rows // SUBLANES, ROW_TILES, SUBLANES, LANES)


def _store_rows4(ref4, x):
    for i in range(x.shape[0] // SUBLANES):
        for j in range(ROW_TILES):
            ref4[i, j] = x[i * SUBLANES:(i + 1) * SUBLANES, j * LANES:(j + 1) * LANES]


def _load_rows4(ref4):
    n = ref4.shape[0]
    return jnp.concatenate(
        [jnp.concatenate([ref4[i, j] for j in range(ROW_TILES)], axis=1) for i in range(n)],
        axis=0)


def _row4(ref4, r):
    return ref4.at[r // SUBLANES, :, r % SUBLANES, :]


def _in_proj_kernel(x_ref, g_ref, w_ref, gq_ref, gk_ref, rope_ref,
                    q_ref, k_ref, v_ref, u_ref):
    tm = x_ref.shape[1]
    h = _rms(x_ref[0], g_ref[...])
    hin = _mm(h.astype(BF16), w_ref[...])
    rope = rope_ref[...]
    cos = rope[:, 0:LANES]
    sin_lo = rope[:, LANES:2 * LANES]
    sin_hi = rope[:, 2 * LANES:3 * LANES]
    lane = lax.broadcasted_iota(jnp.int32, (tm, LANES), 1)
    left = lane < HEAD_DIM

    def norm_rope(z, g):
        sq = z * z
        lsum = jnp.sum(jnp.where(left, sq, 0.0), axis=-1, keepdims=True)
        rsum = jnp.sum(jnp.where(left, 0.0, sq), axis=-1, keepdims=True)
        ms = jnp.where(left, lsum, rsum) * (1.0 / HEAD_DIM)
        zn = (z * lax.rsqrt(ms + EPS)) * g
        half = ROPE_DIM // 2
        return (zn * cos + pltpu.roll(zn, LANES - half, 1) * sin_lo
                + pltpu.roll(zn, half, 1) * sin_hi)

    for j in range(ATT_WIDTH // LANES):
        sl = slice(j * LANES, (j + 1) * LANES)
        q_ref[0, :, sl] = norm_rope(hin[:, sl], gq_ref[...])
    k_ref[0] = norm_rope(hin[:, ATT_WIDTH:ATT_WIDTH + KV_WIDTH], gk_ref[...])
    v_ref[0] = hin[:, ATT_WIDTH + KV_WIDTH:ATT_WIDTH + 2 * KV_WIDTH]
    u_ref[...] = hin[:, ATT_WIDTH + 2 * KV_WIDTH:]


def _in_proj(x, g, w_bf, gq, gk, rope, tm):
    B, S, _ = x.shape
    full = lambda b, i: (0, 0)
    tile = lambda w: pl.BlockSpec((1, tm, w), lambda b, i: (b, i, 0))
    return pl.pallas_call(
        _in_proj_kernel,
        grid=(B, S // tm),
        in_specs=[
            tile(D_MODEL),
            pl.BlockSpec((1, D_MODEL), full),
            pl.BlockSpec((D_MODEL, IN_WIDTH), full),
            pl.BlockSpec((1, LANES), full),
            pl.BlockSpec((1, LANES), full),
            pl.BlockSpec((tm, 3 * LANES), lambda b, i: (i, 0)),
        ],
        out_specs=[
            tile(ATT_WIDTH), tile(KV_WIDTH), tile(KV_WIDTH),
            pl.BlockSpec((tm, SSM_WIDTH), lambda b, i: (i, b)),
        ],
        out_shape=[
            jax.ShapeDtypeStruct((B, S, ATT_WIDTH), F32),
            jax.ShapeDtypeStruct((B, S, KV_WIDTH), F32),
            jax.ShapeDtypeStruct((B, S, KV_WIDTH), F32),
            jax.ShapeDtypeStruct((S, B * SSM_WIDTH), F32),
        ],
        compiler_params=pltpu.CompilerParams(
            dimension_semantics=("arbitrary", "arbitrary"),
            vmem_limit_bytes=VMEM_LIMIT),
        name="in_proj",
    )(x, g, w_bf, gq, gk, rope)


def _swa_kernel(sink_ref, q_ref, k_ref, v_ref, o_ref, *, mask_context):
    tq = q_ref.shape[1]
    i = pl.program_id(1)
    nch = tq // CHUNK
    lane = lax.broadcasted_iota(jnp.int32, (BAND, LANES), 1)
    lo_half = lane < HEAD_DIM
    slabs_per_kv = GQA * HEAD_DIM // LANES

    units = []
    scores = []
    vpads = {}
    valids = {}
    for c in range(nch):
        start = pl.multiple_of((i * nch + c) * CHUNK, CHUNK)
        kb = k_ref[0, pl.ds(start, BAND), :]
        vb = v_ref[0, pl.ds(start, BAND), :]
        kb_sw = pltpu.roll(kb, HEAD_DIM, 1)
        vb_sw = pltpu.roll(vb, HEAD_DIM, 1)
        if mask_context:
            kidx = start + lax.broadcasted_iota(jnp.int32, (1, BAND), 1)
            valids[c] = kidx >= WINDOW
        for kvh in range(N_KV_HEADS):
            k_own, k_oth = (kb, kb_sw) if kvh == 0 else (kb_sw, kb)
            v_own, v_oth = (vb, vb_sw) if kvh == 0 else (vb_sw, vb)
            kpad = (jnp.where(lo_half, k_own, 0.0).astype(BF16),
                    jnp.where(lo_half, 0.0, k_oth).astype(BF16))
            vpads[(c, kvh)] = (jnp.where(lo_half, v_own, 0.0).astype(BF16),
                               jnp.where(lo_half, 0.0, v_oth).astype(BF16))
            col0 = kvh * GQA * HEAD_DIM
            q2 = jnp.concatenate(
                [q_ref[0, c * CHUNK:(c + 1) * CHUNK, col0 + m * LANES:col0 + (m + 1) * LANES]
                 for m in range(slabs_per_kv)], axis=0).astype(BF16)
            for side in range(2):
                s = lax.dot_general(q2, kpad[side], (((1,), (1,)), ((), ())),
                                    preferred_element_type=F32) * (HEAD_DIM ** -0.5)
                if mask_context:
                    s = jnp.where(valids[c], s, NEG)
                units.append((c, kvh, side))
                scores.append(s)

    s_all = jnp.concatenate(scores, axis=0)
    sk = jnp.concatenate(
        [jnp.full((CHUNK, 1), sink_ref[kvh * GQA + 2 * m + side], F32)
         for (_, kvh, side) in units for m in range(slabs_per_kv)], axis=0)
    mx = jnp.maximum(jnp.max(s_all, axis=-1, keepdims=True), sk)
    p_all = jnp.exp(s_all - mx)
    den = jnp.sum(p_all, axis=-1, keepdims=True) + jnp.exp(sk - mx)
    p_all = (p_all / den).astype(BF16)
    rows_u = slabs_per_kv * CHUNK
    probs = [p_all[n * rows_u:(n + 1) * rows_u] for n in range(len(units))]

    for n in range(0, len(units), 2):
        c, kvh, _ = units[n]
        vp = vpads[(c, kvh)]
        o = _mm(probs[n], vp[0]) + _mm(probs[n + 1], vp[1])
        col0 = kvh * GQA * HEAD_DIM
        for m in range(slabs_per_kv):
            o_ref[0, c * CHUNK:(c + 1) * CHUNK, col0 + m * LANES:col0 + (m + 1) * LANES] = (
                o[m * CHUNK:(m + 1) * CHUNK])


def _swa(sink, q, kctx, vctx, tq, mask_context):
    B, Sq, _ = q.shape
    Sk = kctx.shape[1]
    return pl.pallas_call(
        functools.partial(_swa_kernel, mask_context=mask_context),
        grid=(B, Sq // tq),
        in_specs=[
            pl.BlockSpec(memory_space=pltpu.SMEM),
            pl.BlockSpec((1, tq, ATT_WIDTH), lambda b, i: (b, i, 0)),
            pl.BlockSpec((1, Sk, KV_WIDTH), lambda b, i: (b, 0, 0)),
            pl.BlockSpec((1, Sk, KV_WIDTH), lambda b, i: (b, 0, 0)),
        ],
        out_specs=pl.BlockSpec((1, tq, ATT_WIDTH), lambda b, i: (b, i, 0)),
        out_shape=jax.ShapeDtypeStruct((B, Sq, ATT_WIDTH), F32),
        compiler_params=pltpu.CompilerParams(
            dimension_semantics=("arbitrary", "arbitrary"),
            vmem_limit_bytes=VMEM_LIMIT),
        name="swa",
    )(sink, q, kctx, vctx)


def _ssm_kernel(u_ref, h0r_ref, h0i_ref, lam_ref, bre_ref, bim_ref, cre_ref, cim_ref,
                d_ref, wglu_ref, bglu_ref,
                y_ref, hr_out, hi_out, sr, si, hr_s, hi_s):
    L, B, _ = u_ref.shape
    rows = L * B
    half_w = SSM_WIDTH // 2
    half_c = SSM_COLS // 2

    @pl.when(pl.program_id(0) == 0)
    def _():
        hr_s[...] = h0r_ref[...]
        hi_s[...] = h0i_ref[...]

    u = u_ref[...].reshape(rows, SSM_WIDTH)
    ub = u.astype(BF16)
    for hf in range(2):
        uh = ub[:, hf * half_w:(hf + 1) * half_w]
        sr[:, hf * half_c:(hf + 1) * half_c] = _mm(uh, bre_ref[hf])
        si[:, hf * half_c:(hf + 1) * half_c] = _mm(uh, bim_ref[hf])

    cw = 4 * LANES
    for cc in range(SSM_COLS // cw):
        cols = slice(cc * cw, (cc + 1) * cw)
        lr = jnp.broadcast_to(lam_ref[0:1, cols], (B, cw))
        li = jnp.broadcast_to(lam_ref[1:2, cols], (B, cw))

        def body(t, carry):
            hr, hi = carry
            at_t = pl.ds(pl.multiple_of(t * B, B), B)
            nr = lr * hr - li * hi + sr[at_t, cols]
            ni = lr * hi + li * hr + si[at_t, cols]
            sr[at_t, cols] = nr
            si[at_t, cols] = ni
            return nr, ni

        hr, hi = lax.fori_loop(0, L, body, (hr_s[:, cols], hi_s[:, cols]), unroll=2)
        hr_s[:, cols] = hr
        hi_s[:, cols] = hi

    ys = []
    for hf in range(2):
        cs = slice(hf * half_c, (hf + 1) * half_c)
        ys.append(_mm(sr[:, cs].astype(BF16), cre_ref[hf])
                  + _mm(si[:, cs].astype(BF16), cim_ref[hf]))
    y = jnp.concatenate(ys, axis=1) + d_ref[...] * u
    g = 0.5 * y * (1.0 + jnp.tanh(math.sqrt(2.0 / math.pi) * (y + 0.044715 * (y * y * y))))
    gb = g.astype(BF16)
    z = jnp.concatenate(
        [_mm(gb[:, hf * half_w:(hf + 1) * half_w], wglu_ref[hf]) for hf in range(2)],
        axis=1) + bglu_ref[...]
    out = g * (1.0 / (1.0 + jnp.exp(-z)))
    y_ref[...] = out.reshape(L, B, SSM_WIDTH)
    hr_out[...] = hr_s[...]
    hi_out[...] = hi_s[...]


def _ssm(u, h0r, h0i, sp, L):
    S, B, _ = u.shape
    c2 = lambda i: (0, 0)
    c3 = lambda i: (0, 0, 0)
    return pl.pallas_call(
        _ssm_kernel,
        grid=(S // L,),
        in_specs=[
            pl.BlockSpec((L, B, SSM_WIDTH), lambda i: (i, 0, 0)),
            pl.BlockSpec((B, SSM_COLS), c2),
            pl.BlockSpec((B, SSM_COLS), c2),
            pl.BlockSpec((2, SSM_COLS), c2),
            pl.BlockSpec((2, SSM_WIDTH // 2, SSM_COLS // 2), c3),
            pl.BlockSpec((2, SSM_WIDTH // 2, SSM_COLS // 2), c3),
            pl.BlockSpec((2, SSM_COLS // 2, SSM_WIDTH // 2), c3),
            pl.BlockSpec((2, SSM_COLS // 2, SSM_WIDTH // 2), c3),
            pl.BlockSpec((1, SSM_WIDTH), c2),
            pl.BlockSpec((2, SSM_WIDTH // 2, SSM_WIDTH // 2), c3),
            pl.BlockSpec((1, SSM_WIDTH), c2),
        ],
        out_specs=[
            pl.BlockSpec((L, B, SSM_WIDTH), lambda i: (i, 0, 0)),
            pl.BlockSpec((B, SSM_COLS), c2),
            pl.BlockSpec((B, SSM_COLS), c2),
        ],
        out_shape=[
            jax.ShapeDtypeStruct((S, B, SSM_WIDTH), F32),
            jax.ShapeDtypeStruct((B, SSM_COLS), F32),
            jax.ShapeDtypeStruct((B, SSM_COLS), F32),
        ],
        scratch_shapes=[
            pltpu.VMEM((L * B, SSM_COLS), F32),
            pltpu.VMEM((L * B, SSM_COLS), F32),
            pltpu.VMEM((B, SSM_COLS), F32),
            pltpu.VMEM((B, SSM_COLS), F32),
        ],
        compiler_params=pltpu.CompilerParams(
            dimension_semantics=("arbitrary",), vmem_limit_bytes=VMEM_LIMIT),
        name="ssm",
    )(u, h0r, h0i, sp["lam"], sp["bre"], sp["bim"], sp["cre"], sp["cim"],
      sp["d"], sp["wglu"], sp["bglu"])


def _block_diag(blocks):
    G, r, c = blocks.shape
    eye = jnp.eye(G, dtype=blocks.dtype)
    return jnp.einsum("grc,gh->grhc", blocks, eye).reshape(G * r, G * c)


def _ssm_params(lam_re, lam_im, log_dt, b_re, b_im, c_re, c_im, d, w_glu, b_glu):
    lam = lax.complex(lam_re.astype(F32), lam_im.astype(F32))
    dt = jnp.exp(log_dt.astype(F32))[:, None]
    lam_bar = jnp.exp(lam * dt)
    bmat = lax.complex(b_re.astype(F32), b_im.astype(F32))
    b_bar = ((lam_bar - 1.0) / lam)[..., None] * bmat
    lam2 = jnp.stack([lam_bar.real.reshape(-1), lam_bar.imag.reshape(-1)])
    bt = jnp.swapaxes(b_bar, 1, 2)
    hw, hc = SSM_WIDTH // 2, SSM_COLS // 2
    split_b = lambda m: jnp.stack([m[:hw, :hc], m[hw:, hc:]]).astype(BF16)
    split_c = lambda m: jnp.stack([m[:hc, :hw], m[hc:, hw:]]).astype(BF16)
    ct_re = jnp.swapaxes(c_re.astype(F32), 1, 2)
    ct_im = jnp.swapaxes(c_im.astype(F32), 1, 2)
    wg = _block_diag(w_glu.astype(F32))
    return {
        "lam": lam2,
        "bre": split_b(_block_diag(bt.real)),
        "bim": split_b(_block_diag(bt.imag)),
        "cre": split_c(_block_diag(ct_re)),
        "cim": split_c(_block_diag(-ct_im)),
        "d": d.astype(F32).reshape(1, SSM_WIDTH),
        "wglu": jnp.stack([wg[:hw, :hw], wg[hw:, hw:]]).astype(BF16),
        "bglu": b_glu.astype(F32).reshape(1, SSM_WIDTH),
    }


def _memkv_kernel(m_ref, g_ref, w_ref, gk_ref, k_ref, v_ref):
    m = _rms(m_ref[...], g_ref[...])
    kv = _mm(m.astype(BF16), w_ref[...])
    for h in range(CA_HEADS):
        sl = slice(h * CA_HEAD_DIM, (h + 1) * CA_HEAD_DIM)
        k_ref[:, sl] = _rms(kv[:, sl], gk_ref[...])
    v_ref[...] = kv[:, CA_WIDTH:]


def _memkv(mem2d, g, w_bf, gk, tm):
    T = mem2d.shape[0]
    full = lambda i: (0, 0)
    return pl.pallas_call(
        _memkv_kernel,
        grid=(T // tm,),
        in_specs=[
            pl.BlockSpec((tm, D_MODEL), lambda i: (i, 0)),
            pl.BlockSpec((1, D_MODEL), full),
            pl.BlockSpec((D_MODEL, 2 * CA_WIDTH), full),
            pl.BlockSpec((1, CA_HEAD_DIM), full),
        ],
        out_specs=[
            pl.BlockSpec((tm, CA_WIDTH), lambda i: (i, 0)),
            pl.BlockSpec((tm, CA_WIDTH), lambda i: (i, 0)),
        ],
        out_shape=[
            jax.ShapeDtypeStruct((T, CA_WIDTH), F32),
            jax.ShapeDtypeStruct((T, CA_WIDTH), F32),
        ],
        compiler_params=pltpu.CompilerParams(
            dimension_semantics=("arbitrary",), vmem_limit_bytes=VMEM_LIMIT),
        name="memkv",
    )(mem2d, g, w_bf, gk)


def _mid_kernel(x_ref, att_ref, ssm_ref, mk_ref, mv_ref,
                gao_ref, gso_ref, wout_ref, gx_ref, wcq_ref, gcq_ref, wco_ref,
                gffn_ref, wr_ref, br_ref,
                x2_ref, hn_ref, rt_ref, rtt_ref, cnt_ref, base_s):
    tm = x_ref.shape[1]

    @pl.when((pl.program_id(0) == 0) & (pl.program_id(1) == 0))
    def _():
        base_s[...] = jnp.zeros_like(base_s)

    a = _rms(att_ref[0], gao_ref[...]).astype(BF16)
    s = _rms(ssm_ref[...], gso_ref[...]).astype(BF16)
    x1 = (x_ref[0] + _mm(a, wout_ref[0:ATT_WIDTH, :])
          + _mm(s, wout_ref[ATT_WIDTH:, :]))

    qx = _mm(_rms(x1, gx_ref[...]).astype(BF16), wcq_ref[...])
    heads = []
    for h in range(CA_HEADS):
        sl = slice(h * CA_HEAD_DIM, (h + 1) * CA_HEAD_DIM)
        qh = _rms(qx[:, sl], gcq_ref[...]).astype(BF16)
        kh = mk_ref[0, :, sl].astype(BF16)
        vh = mv_ref[0, :, sl].astype(BF16)
        sc = lax.dot_general(qh, kh, (((1,), (1,)), ((), ())),
                             preferred_element_type=F32) * (CA_HEAD_DIM ** -0.5)
        p = jnp.exp(sc - jnp.max(sc, axis=-1, keepdims=True))
        p = p / jnp.sum(p, axis=-1, keepdims=True)
        heads.append(_mm(p.astype(BF16), vh))
    o = jnp.concatenate(heads, axis=1).astype(BF16)
    x2 = x1 + _mm(o, wco_ref[...])
    x2_ref[0] = x2

    hn = _rms(x2, gffn_ref[...])
    _store_rows4(hn_ref.at[0], hn)

    h_hi = hn.astype(BF16)
    h_lo = (hn - h_hi.astype(F32)).astype(BF16)
    r1 = _mm(h_hi, wr_ref[...])
    lg = (r1[:, :LANES] + r1[:, LANES:] + _mm(h_lo, wr_ref[:, 0:LANES])
          + br_ref[...])

    col = lax.broadcasted_iota(jnp.int32, (tm, LANES), 1)
    big = jnp.int32(4 * LANES)
    gmask = col < N_EXPERT_GROUPS
    lgg = jnp.where(gmask, lg, NEG)
    mg = jnp.max(lgg, axis=-1, keepdims=True)
    grp = jnp.min(jnp.where(gmask & (lgg == mg), col, big), axis=-1, keepdims=True)
    pg_top = 1.0 / jnp.sum(jnp.where(gmask, jnp.exp(lgg - mg), 0.0), axis=-1, keepdims=True)

    ecol = col - ROUTER_COL0
    emask = ((ecol >= 0) & (ecol < N_EXPERTS)
             & (lax.shift_right_arithmetic(ecol, 3) == grp))
    le = jnp.where(emask, lg, NEG)
    m1 = jnp.max(le, axis=-1, keepdims=True)
    i1 = jnp.min(jnp.where(emask & (le == m1), col, big), axis=-1, keepdims=True)
    rest = emask & (col != i1)
    le2 = jnp.where(rest, lg, NEG)
    m2 = jnp.max(le2, axis=-1, keepdims=True)
    i2 = jnp.min(jnp.where(rest & (le2 == m2), col, big), axis=-1, keepdims=True)
    den = jnp.sum(jnp.where(emask, jnp.exp(le - m1), 0.0), axis=-1, keepdims=True)
    p1 = 1.0 / den
    p2 = jnp.exp(m2 - m1) / den
    gate1 = pg_top * p1 / (p1 + p2)
    gate2 = pg_top * p2 / (p1 + p2)

    sel1 = col == i1
    sel2 = col == i2
    oh = jnp.where(sel1 | sel2, 1.0, 0.0)
    r_i = lax.broadcasted_iota(jnp.int32, (tm, tm), 0)
    c_i = lax.broadcasted_iota(jnp.int32, (tm, tm), 1)
    tri = jnp.where(r_i > c_i, 1.0, 0.0).astype(BF16)
    tot = base_s[...] + _mm(tri, oh.astype(BF16))
    rank1 = jnp.sum(jnp.where(sel1, tot, 0.0), axis=-1, keepdims=True)
    rank2 = jnp.sum(jnp.where(sel2, tot, 0.0), axis=-1, keepdims=True)
    base_s[...] = base_s[...] + jnp.sum(oh, axis=0, keepdims=True)
    cnt_ref[...] = base_s[...]

    e1 = (i1 - ROUTER_COL0).astype(F32)
    e2 = (i2 - ROUTER_COL0).astype(F32)
    rt = jnp.zeros((tm, LANES), F32)
    for k, val in enumerate((e1, e2, gate1, gate2, rank1, rank2)):
        rt = jnp.where(col == k, val, rt)
    rt_ref[0] = rt
    rtt_ref[0] = rt.T[0:8, :]


def _mid(x, att, ssm_tm, mk, mv, wp, tm):
    B, S, _ = x.shape
    c2 = lambda b, i: (0, 0)
    tile = lambda w: pl.BlockSpec((1, tm, w), lambda b, i: (b, i, 0))
    return pl.pallas_call(
        _mid_kernel,
        grid=(B, S // tm),
        in_specs=[
            tile(D_MODEL), tile(ATT_WIDTH),
            pl.BlockSpec((tm, SSM_WIDTH), lambda b, i: (i, b)),
            pl.BlockSpec((1, N_MEM, CA_WIDTH), lambda b, i: (b, 0, 0)),
            pl.BlockSpec((1, N_MEM, CA_WIDTH), lambda b, i: (b, 0, 0)),
            pl.BlockSpec((1, ATT_WIDTH), c2),
            pl.BlockSpec((1, SSM_WIDTH), c2),
            pl.BlockSpec((ATT_WIDTH + SSM_WIDTH, D_MODEL), c2),
            pl.BlockSpec((1, D_MODEL), c2),
            pl.BlockSpec((D_MODEL, CA_WIDTH), c2),
            pl.BlockSpec((1, CA_HEAD_DIM), c2),
            pl.BlockSpec((CA_WIDTH, D_MODEL), c2),
            pl.BlockSpec((1, D_MODEL), c2),
            pl.BlockSpec((D_MODEL, 2 * LANES), c2),
            pl.BlockSpec((1, LANES), c2),
        ],
        out_specs=[
            tile(D_MODEL),
            pl.BlockSpec((1,) + _rows4_shape(tm), lambda b, i: (b, i, 0, 0, 0)),
            tile(LANES),
            pl.BlockSpec((1, 8, tm), lambda b, i: (b, 0, i)),
            pl.BlockSpec((1, LANES), c2),
        ],
        out_shape=[
            jax.ShapeDtypeStruct((B, S, D_MODEL), F32),
            jax.ShapeDtypeStruct((B,) + _rows4_shape(S), F32),
            jax.ShapeDtypeStruct((B, S, LANES), F32),
            jax.ShapeDtypeStruct((B, 8, S), F32),
            jax.ShapeDtypeStruct((1, LANES), F32),
        ],
        scratch_shapes=[pltpu.VMEM((1, LANES), F32)],
        compiler_params=pltpu.CompilerParams(
            dimension_semantics=("arbitrary", "arbitrary"),
            vmem_limit_bytes=VMEM_LIMIT),
        name="mid",
    )(x, att, ssm_tm, mk, mv, wp["gao"], wp["gso"], wp["wout"], wp["gx"], wp["wcq"],
      wp["gcq"], wp["wco"], wp["gffn"], wp["wr"], wp["br"])


def _dispatch_kernel(pend_ref, padded_ref, dest_ref, hn_ref, xs_hbm, zbuf, sem):
    tm = hn_ref.shape[0] * SUBLANES

    @pl.when(pl.program_id(0) == 0)
    def _():
        zbuf[...] = jnp.zeros_like(zbuf)
        for e in range(N_EXPERTS):
            @pl.when(padded_ref[e] > 0)
            def _():
                fill = pltpu.make_async_copy(
                    zbuf, xs_hbm.at[pl.ds(pend_ref[e] - MOE_BLOCK, MOE_BLOCK)], sem.at[1])
                fill.start()
                fill.wait()

        def fill_tail(b, carry):
            fill = pltpu.make_async_copy(
                zbuf, xs_hbm.at[pl.ds(pl.multiple_of(b * MOE_BLOCK, MOE_BLOCK), MOE_BLOCK)],
                sem.at[1])
            fill.start()
            fill.wait()
            return carry

        lax.fori_loop(pend_ref[N_EXPERTS - 1] // MOE_BLOCK, xs_hbm.shape[0] // MOE_BLOCK,
                      fill_tail, 0)

    for k in range(2):
        for r in range(tm):
            pltpu.make_async_copy(_row4(hn_ref, r), xs_hbm.at[dest_ref[0, 0, k * tm + r]],
                                  sem.at[0]).start(priority=r % 2)
    for k in range(2):
        pltpu.make_async_copy(zbuf, xs_hbm.at[pl.ds(0, tm)], sem.at[0]).wait()


def _dispatch(pad_end, padded, dest_t, hn4, rows, tm):
    nt = hn4.shape[0] * SUBLANES // tm
    grid_spec = pltpu.PrefetchScalarGridSpec(
        num_scalar_prefetch=2,
        grid=(nt,),
        in_specs=[
            pl.BlockSpec((1, 1, 2 * tm), lambda i, pe, pd: (i, 0, 0), memory_space=pltpu.SMEM),
            pl.BlockSpec(_rows4_shape(tm), lambda i, pe, pd: (i, 0, 0, 0)),
        ],
        out_specs=pl.BlockSpec(memory_space=pl.ANY),
        scratch_shapes=[
            pltpu.VMEM((MOE_BLOCK, ROW_TILES, LANES), F32),
            pltpu.SemaphoreType.DMA((2,)),
        ],
    )
    assert tm == MOE_BLOCK
    return pl.pallas_call(
        _dispatch_kernel,
        grid_spec=grid_spec,
        out_shape=jax.ShapeDtypeStruct((rows, ROW_TILES, LANES), F32),
        compiler_params=pltpu.CompilerParams(
            dimension_semantics=("arbitrary",), vmem_limit_bytes=VMEM_LIMIT),
        name="dispatch",
    )(pad_end, padded, dest_t, hn4)


def _moe_kernel(be_ref, nu_ref, xs_hbm, xs4_hbm, wg_ref, wu_ref, wd_ref, yb_hbm,
                xbuf, ybuf, wg_s, wu_s, wd_s, gsem, ssem):
    i = pl.program_id(0)
    last = pl.num_programs(0) - 1
    n_used = nu_ref[0]
    slot = lax.rem(i, 2)
    other = 1 - slot
    blk = MOE_BLOCK // SUBLANES
    row0 = i * blk
    half = MOE_BLOCK // 2

    def fetch(lead0, s, lo, hi):
        for r in range(lo, hi):
            pltpu.make_async_copy(xs_hbm.at[lead0 * SUBLANES + r],
                                  _row4(xbuf.at[s], r), gsem.at[s]).start(priority=r % 2)

    def store(lead0, s, lo, hi, sem_idx=None):
        for r in range(lo, hi):
            pltpu.make_async_copy(_row4(ybuf.at[s], r),
                                  yb_hbm.at[lead0 + r // SUBLANES, r % SUBLANES],
                                  ssem.at[s if sem_idx is None else sem_idx]).start(priority=r % 2)

    def wait_fetch(s):
        pltpu.make_async_copy(xs4_hbm.at[pl.ds(0, blk)], xbuf.at[s], gsem.at[s]).wait()

    def wait_store(s, sem_idx=None):
        pltpu.make_async_copy(ybuf.at[s], yb_hbm.at[pl.ds(0, blk)],
                              ssem.at[s if sem_idx is None else sem_idx]).wait()

    def compute(s, between):
        xe = _load_rows4(xbuf.at[s]).astype(BF16)
        between[0]()
        g = _mm(xe, wg_s[...])
        between[1]()
        u = _mm(xe, wu_s[...])
        between[2]()
        hmid = ((g * (1.0 / (1.0 + jnp.exp(-g)))) * u).astype(BF16)
        y = _mm(hmid, wd_s[...])
        between[3]()
        _store_rows4(ybuf.at[s], y)

    @pl.when((i == 0) & (n_used > 0))
    def _():
        fetch(0, 0, 0, MOE_BLOCK)

    @pl.when(i < n_used)
    def _():
        wait_fetch(slot)

    @pl.when((i >= 2) & (i <= n_used + 1))
    def _():
        wait_store(slot)

    @pl.when(i < n_used)
    def _():
        @pl.when((i == 0) | (be_ref[i] != be_ref[jnp.maximum(i - 1, 0)]))
        def _():
            wg_s[...] = wg_ref[0].astype(BF16)
            wu_s[...] = wu_ref[0].astype(BF16)
            wd_s[...] = wd_ref[0].astype(BF16)

        steady = (i >= 1) & (i + 1 < n_used)
        nothing = lambda: None

        @pl.when(steady)
        def _():
            compute(slot, (
                lambda: fetch(row0 + blk, other, 0, half),
                lambda: fetch(row0 + blk, other, half, MOE_BLOCK),
                lambda: store(row0 - blk, other, 0, half),
                lambda: store(row0 - blk, other, half, MOE_BLOCK)))

        @pl.when(jnp.logical_not(steady))
        def _():
            @pl.when(i + 1 < n_used)
            def _():
                fetch(row0 + blk, other, 0, MOE_BLOCK)

            @pl.when(i >= 1)
            def _():
                store(row0 - blk, other, 0, MOE_BLOCK)

            compute(slot, (nothing, nothing, nothing, nothing))

    @pl.when(i >= n_used)
    def _():
        @pl.when((i == n_used) & (i >= 1))
        def _():
            store(row0 - blk, other, 0, MOE_BLOCK)

        ybuf[slot] = jnp.zeros(ybuf.shape[1:], F32)
        store(row0, slot, 0, MOE_BLOCK, sem_idx=2)
        wait_store(slot, sem_idx=2)

        @pl.when((i == last) & (i == n_used) & (i >= 1))
        def _():
            wait_store(other)


def _moe(block_e, n_used, xs, w_gate, w_up, w_down):
    n_blocks = block_e.shape[0]
    grid_spec = pltpu.PrefetchScalarGridSpec(
        num_scalar_prefetch=2,
        grid=(n_blocks,),
        in_specs=[
            pl.BlockSpec(memory_space=pl.ANY),
            pl.BlockSpec(memory_space=pl.ANY),
            pl.BlockSpec((1, D_MODEL, D_EXPERT), lambda i, be, nu: (be[i], 0, 0)),
            pl.BlockSpec((1, D_MODEL, D_EXPERT), lambda i, be, nu: (be[i], 0, 0)),
            pl.BlockSpec((1, D_EXPERT, D_MODEL), lambda i, be, nu: (be[i], 0, 0)),
        ],
        out_specs=pl.BlockSpec(memory_space=pl.ANY),
        scratch_shapes=[
            pltpu.VMEM((2,) + _rows4_shape(MOE_BLOCK), F32),
            pltpu.VMEM((2,) + _rows4_shape(MOE_BLOCK), F32),
            pltpu.VMEM((D_MODEL, D_EXPERT), BF16),
            pltpu.VMEM((D_MODEL, D_EXPERT), BF16),
            pltpu.VMEM((D_EXPERT, D_MODEL), BF16),
            pltpu.SemaphoreType.DMA((2,)),
            pltpu.SemaphoreType.DMA((3,)),
        ],
    )
    return pl.pallas_call(
        _moe_kernel,
        grid_spec=grid_spec,
        out_shape=jax.ShapeDtypeStruct(_rows4_shape(xs.shape[0]), F32),
        compiler_params=pltpu.CompilerParams(
            dimension_semantics=("arbitrary",), vmem_limit_bytes=VMEM_LIMIT),
        name="moe",
    )(block_e, n_used, xs, xs.reshape(_rows4_shape(xs.shape[0])), w_gate, w_up, w_down)


def _combine_kernel(dest_ref, dest_next_ref, x2_ref, rt_ref, yb3_hbm, yb_hbm, o_ref, buf, sem):
    tm = x2_ref.shape[0]
    i = pl.program_id(0)
    slot = lax.rem(i, 2)

    def gather(d_ref, s):
        for k in range(2):
            for r in range(tm):
                pltpu.make_async_copy(yb3_hbm.at[d_ref[0, 0, k * tm + r]],
                                      _row4(buf.at[s, k], r), sem.at[s]).start(priority=r % 2)

    @pl.when(i == 0)
    def _():
        gather(dest_ref, 0)

    @pl.when(i + 1 < pl.num_programs(0))
    def _():
        gather(dest_next_ref, 1 - slot)

    for k in range(2):
        pltpu.make_async_copy(yb_hbm.at[pl.ds(0, tm // SUBLANES)], buf.at[slot, k],
                              sem.at[slot]).wait()
    rt = rt_ref[...]
    o_ref[...] = (x2_ref[...] + rt[:, 2:3] * _load_rows4(buf.at[slot, 0])
                  + rt[:, 3:4] * _load_rows4(buf.at[slot, 1]))


def _combine(dest_t, x2, rt, yb, tm):
    T = x2.shape[0]
    nt = T // tm
    return pl.pallas_call(
        _combine_kernel,
        grid=(nt,),
        in_specs=[
            pl.BlockSpec((1, 1, 2 * tm), lambda i: (i, 0, 0), memory_space=pltpu.SMEM),
            pl.BlockSpec((1, 1, 2 * tm), lambda i: (jnp.minimum(i + 1, nt - 1), 0, 0),
                         memory_space=pltpu.SMEM),
            pl.BlockSpec((tm, D_MODEL), lambda i: (i, 0)),
            pl.BlockSpec((tm, LANES), lambda i: (i, 0)),
            pl.BlockSpec(memory_space=pl.ANY),
            pl.BlockSpec(memory_space=pl.ANY),
        ],
        out_specs=pl.BlockSpec((tm, D_MODEL), lambda i: (i, 0)),
        out_shape=jax.ShapeDtypeStruct((T, D_MODEL), F32),
        scratch_shapes=[
            pltpu.VMEM((2, 2) + _rows4_shape(tm), F32),
            pltpu.SemaphoreType.DMA((2,)),
        ],
        compiler_params=pltpu.CompilerParams(
            dimension_semantics=("arbitrary",), vmem_limit_bytes=VMEM_LIMIT),
        name="combine",
    )(dest_t, dest_t, x2, rt, yb.reshape(-1, ROW_TILES, LANES), yb)


def _hier_moe(x2, hn4, rt, rtt, cnt, w_gate, w_up, w_down, tm):
    T = x2.shape[0]
    counts = cnt[0, ROUTER_COL0:ROUTER_COL0 + N_EXPERTS].astype(jnp.int32)
    padded = (counts + MOE_BLOCK - 1) // MOE_BLOCK * MOE_BLOCK
    pad_end = jnp.cumsum(padded)
    pad_start = pad_end - padded
    flat = lambda a: jnp.swapaxes(a, 0, 1).reshape(a.shape[1], T)
    eid = flat(rtt[:, 0:2, :]).astype(jnp.int32)
    rank = flat(rtt[:, 4:6, :]).astype(jnp.int32)
    experts = jnp.arange(N_EXPERTS, dtype=jnp.int32)[:, None, None]
    dest = rank + jnp.sum(jnp.where(eid[None] == experts, pad_start[:, None, None], 0), axis=0)
    n_blocks = (2 * T + N_EXPERTS * (MOE_BLOCK - 1)) // MOE_BLOCK + 1
    rows = n_blocks * MOE_BLOCK
    blk_row0 = jnp.arange(n_blocks, dtype=jnp.int32) * MOE_BLOCK
    block_e = jnp.minimum(
        jnp.sum((pad_end[None, :] <= blk_row0[:, None]).astype(jnp.int32), axis=1),
        N_EXPERTS - 1)
    n_used = (pad_end[-1] // MOE_BLOCK).astype(jnp.int32).reshape(1)
    nt = T // tm
    dest_t = dest.reshape(2, nt, tm).transpose(1, 0, 2).reshape(nt, 1, 2 * tm)
    xs = _dispatch(pad_end, padded, dest_t, hn4, rows, tm)
    yb = _moe(block_e, n_used, xs, w_gate, w_up, w_down)
    return _combine(dest_t, x2, rt, yb, tm)


def _rope_table(pos):
    half = ROPE_DIM // 2
    inv = ROPE_THETA ** (-jnp.arange(0, ROPE_DIM, 2, dtype=F32) / ROPE_DIM)
    ang = pos.astype(F32)[:, None] * inv[None, :]
    cos, sin = jnp.cos(ang), jnp.sin(ang)
    L = pos.shape[0]
    pad = jnp.zeros((L, HEAD_DIM - ROPE_DIM), F32)
    zero = jnp.zeros((L, half), F32)
    c64 = jnp.concatenate([cos, cos, pad + 1.0], axis=1)
    lo64 = jnp.concatenate([-sin, zero, pad], axis=1)
    hi64 = jnp.concatenate([zero, sin, pad], axis=1)
    two = lambda t: jnp.concatenate([t, t], axis=1)
    return jnp.concatenate([two(c64), two(lo64), two(hi64)], axis=1)


def _layer(x, pos_rope, kctx_prev, vctx_prev, h0r, h0i, mk, mv, wp, sp, ew, *,
           tm_in, tq, ssm_l, tm_mid, tm_comb, mask_context):
    B, S, _ = x.shape
    T = B * S
    q, k3, v3, u_tm = _in_proj(x, wp["gmix"], wp["win"], wp["gq"], wp["gk"], pos_rope, tm_in)
    kctx = jnp.concatenate([kctx_prev, k3], axis=1)
    vctx = jnp.concatenate([vctx_prev, v3], axis=1)
    att = _swa(wp["sink"], q, kctx, vctx, tq, mask_context)
    ssm_tm, hr, hi = _ssm(u_tm.reshape(S, B, SSM_WIDTH), h0r, h0i, sp, ssm_l)
    x2, hn4, rt, rtt, cnt = _mid(x, att, ssm_tm.reshape(S, B * SSM_WIDTH), mk, mv, wp, tm_mid)
    y = _hier_moe(x2.reshape(T, D_MODEL), hn4.reshape(_rows4_shape(T)),
                  rt.reshape(T, LANES), rtt, cnt, *ew, tm_comb)
    return y.reshape(B, S, D_MODEL), k3, v3, hr, hi


def kernel(x_prompt, x_sample, cache_attn_k, cache_attn_v, state_ssm_re, state_ssm_im, cache_mem_k, cache_mem_v, mem_prompt, norm_mix, w_in, q_norm, k_norm, attn_sink, ssm_lambda_re, ssm_lambda_im, ssm_log_dt, ssm_b_re, ssm_b_im, ssm_c_re, ssm_c_im, ssm_d, ssm_w_glu, ssm_b_glu, norm_attn_out, norm_ssm_out, w_out, norm_cross, norm_mem, w_cq, w_ck, w_cv, cq_norm, ck_norm, w_co, norm_ffn, w_router_group, b_router_group, w_router_expert, b_router_expert, w_e_gate, w_e_up, w_e_down):
    depth = norm_mix.shape[0]
    Bp, Lp, _ = x_prompt.shape
    Bs, Ls, _ = x_sample.shape
    yp, ys = x_prompt, x_sample
    rope_p = _rope_table(jnp.arange(Lp, dtype=jnp.int32))
    rope_s = _rope_table(PAST_LEN + jnp.arange(Ls, dtype=jnp.int32))
    outs = [[] for _ in range(10)]
    n_router = N_EXPERT_GROUPS + N_EXPERTS
    for l in range(depth):
        row = lambda a: a[l].astype(F32).reshape(1, -1)
        w_r = jnp.pad(jnp.concatenate([w_router_group[l], w_router_expert[l]], axis=1).astype(F32),
                      ((0, 0), (0, LANES - n_router)))
        w_r_hi = w_r.astype(BF16)
        w_r_lo = (w_r - w_r_hi.astype(F32)).astype(BF16)
        b_r = jnp.pad(jnp.concatenate([b_router_group[l], b_router_expert[l]]).astype(F32),
                      (0, LANES - n_router)).reshape(1, LANES)
        wp = {
            "gmix": row(norm_mix), "win": w_in[l].astype(BF16),
            "gq": jnp.tile(row(q_norm), (1, LANES // HEAD_DIM)),
            "gk": jnp.tile(row(k_norm), (1, LANES // HEAD_DIM)),
            "sink": attn_sink[l].astype(F32),
            "gao": row(norm_attn_out), "gso": row(norm_ssm_out),
            "wout": w_out[l].astype(BF16), "gx": row(norm_cross),
            "wcq": w_cq[l].astype(BF16), "gcq": row(cq_norm),
            "wco": w_co[l].astype(BF16), "gffn": row(norm_ffn),
            "wr": jnp.concatenate([w_r_hi, w_r_lo], axis=1), "br": b_r,
        }
        sp = _ssm_params(ssm_lambda_re[l], ssm_lambda_im[l], ssm_log_dt[l], ssm_b_re[l],
                         ssm_b_im[l], ssm_c_re[l], ssm_c_im[l], ssm_d[l], ssm_w_glu[l],
                         ssm_b_glu[l])
        ew = (w_e_gate[l].astype(F32), w_e_up[l].astype(F32), w_e_down[l].astype(F32))

        w_ckv = jnp.concatenate([w_ck[l], w_cv[l]], axis=1).astype(BF16)
        mkp, mvp = _memkv(mem_prompt.reshape(Bp * N_MEM, D_MODEL), row(norm_mem), w_ckv,
                          row(ck_norm), 512)
        mkp = mkp.reshape(Bp, N_MEM, CA_WIDTH)
        mvp = mvp.reshape(Bp, N_MEM, CA_WIDTH)

        zctx = jnp.zeros((Bp, WINDOW, KV_WIDTH), F32)
        zst = jnp.zeros((Bp, SSM_COLS), F32)
        yp, kp, vp, hpr, hpi = _layer(
            yp, rope_p, zctx, zctx, zst, zst, mkp, mvp, wp, sp, ew,
            tm_in=512, tq=256, ssm_l=64, tm_mid=512, tm_comb=256, mask_context=True)
        ys, kn, vn, hsr, hsi = _layer(
            ys, rope_s, cache_attn_k[l].reshape(Bs, WINDOW, KV_WIDTH).astype(F32),
            cache_attn_v[l].reshape(Bs, WINDOW, KV_WIDTH).astype(F32),
            state_ssm_re[l].astype(F32).reshape(Bs, SSM_COLS),
            state_ssm_im[l].astype(F32).reshape(Bs, SSM_COLS),
            cache_mem_k[l].astype(F32).reshape(Bs, N_MEM, CA_WIDTH),
            cache_mem_v[l].astype(F32).reshape(Bs, N_MEM, CA_WIDTH), wp, sp, ew,
            tm_in=Ls, tq=CHUNK, ssm_l=Ls, tm_mid=Ls, tm_comb=256, mask_context=False)

        sg = (N_SSM_GROUPS, SSM_STATE)
        kvs = (N_KV_HEADS, HEAD_DIM)
        vals = (kp[:, Lp - WINDOW:].reshape(Bp, WINDOW, *kvs),
                vp[:, Lp - WINDOW:].reshape(Bp, WINDOW, *kvs),
                hpr.reshape(Bp, *sg), hpi.reshape(Bp, *sg),
                mkp.reshape(Bp, N_MEM, CA_HEADS, CA_HEAD_DIM),
                mvp.reshape(Bp, N_MEM, CA_HEADS, CA_HEAD_DIM),
                kn.reshape(Bs, Ls, *kvs), vn.reshape(Bs, Ls, *kvs),
                hsr.reshape(Bs, *sg), hsi.reshape(Bs, *sg))
        for lst, val in zip(outs, vals):
            lst.append(val)
    return (yp, ys) + tuple(jnp.stack(lst) for lst in outs)
```

```python
import functools
import math

import jax
import jax.numpy as jnp
from jax import lax
from jax.experimental import pallas as pl
from jax.experimental.pallas import tpu as pltpu

F32 = jnp.float32
BF16 = jnp.bfloat16

D_MODEL = 1024
CHUNK = 64
N_Q_HEADS = 8
N_KV_HEADS = 2
GQA = N_Q_HEADS // N_KV_HEADS
HEAD_DIM = 64
WINDOW = 128
BAND = WINDOW + CHUNK
ROPE_DIM = HEAD_DIM // 4
ROPE_THETA = 500000.0
ATT_WIDTH = N_Q_HEADS * HEAD_DIM
KV_WIDTH = N_KV_HEADS * HEAD_DIM
SSM_GROUP = 16
SSM_WIDTH = D_MODEL // 2
N_SSM_GROUPS = SSM_WIDTH // SSM_GROUP
SSM_STATE = 64
SSM_COLS = N_SSM_GROUPS * SSM_STATE
IN_WIDTH = ATT_WIDTH + 2 * KV_WIDTH + SSM_WIDTH
N_MEM = 256
CA_HEADS = 4
CA_HEAD_DIM = 128
CA_WIDTH = CA_HEADS * CA_HEAD_DIM
N_EXPERT_GROUPS = 4
EXPERTS_PER_GROUP = 8
N_EXPERTS = N_EXPERT_GROUPS * EXPERTS_PER_GROUP
D_EXPERT = 512
MOE_BLOCK = 256
EPS = 1e-6
NEG = -1e30
PAST_LEN = 4096

LANES = 128
ROUTER_COL0 = N_EXPERT_GROUPS
VMEM_LIMIT = 48 * 1024 * 1024


def _rms(x, g):
    ms = jnp.mean(x * x, axis=-1, keepdims=True)
    return (x * lax.rsqrt(ms + EPS)) * g


def _mm(a, b):
    return jnp.dot(a, b, preferred_element_type=F32)


def _in_proj_kernel(x_ref, g_ref, w_ref, gq_ref, gk_ref, rope_ref,
                    q_ref, k_ref, v_ref, u_ref):
    tm = x_ref.shape[1]
    h = _rms(x_ref[0], g_ref[...])
    hin = _mm(h.astype(BF16), w_ref[...])
    rope = rope_ref[...]
    cos = rope[:, 0:LANES]
    sin_lo = rope[:, LANES:2 * LANES]
    sin_hi = rope[:, 2 * LANES:3 * LANES]
    lane = lax.broadcasted_iota(jnp.int32, (tm, LANES), 1)
    left = lane < HEAD_DIM

    def norm_rope(z, g):
        sq = z * z
        lsum = jnp.sum(jnp.where(left, sq, 0.0), axis=-1, keepdims=True)
        rsum = jnp.sum(jnp.where(left, 0.0, sq), axis=-1, keepdims=True)
        ms = jnp.where(left, lsum, rsum) * (1.0 / HEAD_DIM)
        zn = (z * lax.rsqrt(ms + EPS)) * g
        half = ROPE_DIM // 2
        return (zn * cos + pltpu.roll(zn, LANES - half, 1) * sin_lo
                + pltpu.roll(zn, half, 1) * sin_hi)

    for j in range(ATT_WIDTH // LANES):
        sl = slice(j * LANES, (j + 1) * LANES)
        q_ref[0, :, sl] = norm_rope(hin[:, sl], gq_ref[...])
    k_ref[0] = norm_rope(hin[:, ATT_WIDTH:ATT_WIDTH + KV_WIDTH], gk_ref[...])
    v_ref[0] = hin[:, ATT_WIDTH + KV_WIDTH:ATT_WIDTH + 2 * KV_WIDTH]
    u_ref[...] = hin[:, ATT_WIDTH + 2 * KV_WIDTH:]


def _in_proj(x, g, w_bf, gq, gk, rope, tm):
    B, S, _ = x.shape
    full = lambda b, i: (0, 0)
    tile = lambda w: pl.BlockSpec((1, tm, w), lambda b, i: (b, i, 0))
    return pl.pallas_call(
        _in_proj_kernel,
        grid=(B, S // tm),
        in_specs=[
            tile(D_MODEL),
            pl.BlockSpec((1, D_MODEL), full),
            pl.BlockSpec((D_MODEL, IN_WIDTH), full),
            pl.BlockSpec((1, LANES), full),
            pl.BlockSpec((1, LANES), full),
            pl.BlockSpec((tm, 3 * LANES), lambda b, i: (i, 0)),
        ],
        out_specs=[
            tile(ATT_WIDTH), tile(KV_WIDTH), tile(KV_WIDTH),
            pl.BlockSpec((tm, SSM_WIDTH), lambda b, i: (i, b)),
        ],
        out_shape=[
            jax.ShapeDtypeStruct((B, S, ATT_WIDTH), F32),
            jax.ShapeDtypeStruct((B, S, KV_WIDTH), F32),
            jax.ShapeDtypeStruct((B, S, KV_WIDTH), F32),
            jax.ShapeDtypeStruct((S, B * SSM_WIDTH), F32),
        ],
        compiler_params=pltpu.CompilerParams(
            dimension_semantics=("arbitrary", "arbitrary"),
            vmem_limit_bytes=VMEM_LIMIT),
        name="in_proj",
    )(x, g, w_bf, gq, gk, rope)


def _swa_kernel(sink_ref, q_ref, k_ref, v_ref, o_ref, *, mask_context):
    tq = q_ref.shape[1]
    i = pl.program_id(1)
    nch = tq // CHUNK
    lane = lax.broadcasted_iota(jnp.int32, (BAND, LANES), 1)
    lo_half = lane < HEAD_DIM
    slabs_per_kv = GQA * HEAD_DIM // LANES

    units = []
    scores = []
    vpads = {}
    valids = {}
    for c in range(nch):
        start = pl.multiple_of((i * nch + c) * CHUNK, CHUNK)
        kb = k_ref[0, pl.ds(start, BAND), :]
        vb = v_ref[0, pl.ds(start, BAND), :]
        kb_sw = pltpu.roll(kb, HEAD_DIM, 1)
        vb_sw = pltpu.roll(vb, HEAD_DIM, 1)
        if mask_context:
            kidx = start + lax.broadcasted_iota(jnp.int32, (1, BAND), 1)
            valids[c] = kidx >= WINDOW
        for kvh in range(N_KV_HEADS):
            k_own, k_oth = (kb, kb_sw) if kvh == 0 else (kb_sw, kb)
            v_own, v_oth = (vb, vb_sw) if kvh == 0 else (vb_sw, vb)
            kpad = (jnp.where(lo_half, k_own, 0.0).astype(BF16),
                    jnp.where(lo_half, 0.0, k_oth).astype(BF16))
            vpads[(c, kvh)] = (jnp.where(lo_half, v_own, 0.0).astype(BF16),
                               jnp.where(lo_half, 0.0, v_oth).astype(BF16))
            col0 = kvh * GQA * HEAD_DIM
            q2 = jnp.concatenate(
                [q_ref[0, c * CHUNK:(c + 1) * CHUNK, col0 + m * LANES:col0 + (m + 1) * LANES]
                 for m in range(slabs_per_kv)], axis=0).astype(BF16)
            for side in range(2):
                s = lax.dot_general(q2, kpad[side], (((1,), (1,)), ((), ())),
                                    preferred_element_type=F32) * (HEAD_DIM ** -0.5)
                if mask_context:
                    s = jnp.where(valids[c], s, NEG)
                units.append((c, kvh, side))
                scores.append(s)

    s_all = jnp.concatenate(scores, axis=0)
    sk = jnp.concatenate(
        [jnp.full((CHUNK, 1), sink_ref[kvh * GQA + 2 * m + side], F32)
         for (_, kvh, side) in units for m in range(slabs_per_kv)], axis=0)
    mx = jnp.maximum(jnp.max(s_all, axis=-1, keepdims=True), sk)
    p_all = jnp.exp(s_all - mx)
    den = jnp.sum(p_all, axis=-1, keepdims=True) + jnp.exp(sk - mx)
    p_all = (p_all / den).astype(BF16)
    rows_u = slabs_per_kv * CHUNK
    probs = [p_all[n * rows_u:(n + 1) * rows_u] for n in range(len(units))]

    for n in range(0, len(units), 2):
        c, kvh, _ = units[n]
        vp = vpads[(c, kvh)]
        o = _mm(probs[n], vp[0]) + _mm(probs[n + 1], vp[1])
        col0 = kvh * GQA * HEAD_DIM
        for m in range(slabs_per_kv):
            o_ref[0, c * CHUNK:(c + 1) * CHUNK, col0 + m * LANES:col0 + (m + 1) * LANES] = (
                o[m * CHUNK:(m + 1) * CHUNK])


def _swa(sink, q, kctx, vctx, tq, mask_context):
    B, Sq, _ = q.shape
    Sk = kctx.shape[1]
    return pl.pallas_call(
        functools.partial(_swa_kernel, mask_context=mask_context),
        grid=(B, Sq // tq),
        in_specs=[
            pl.BlockSpec(memory_space=pltpu.SMEM),
            pl.BlockSpec((1, tq, ATT_WIDTH), lambda b, i: (b, i, 0)),
            pl.BlockSpec((1, Sk, KV_WIDTH), lambda b, i: (b, 0, 0)),
            pl.BlockSpec((1, Sk, KV_WIDTH), lambda b, i: (b, 0, 0)),
        ],
        out_specs=pl.BlockSpec((1, tq, ATT_WIDTH), lambda b, i: (b, i, 0)),
        out_shape=jax.ShapeDtypeStruct((B, Sq, ATT_WIDTH), F32),
        compiler_params=pltpu.CompilerParams(
            dimension_semantics=("arbitrary", "arbitrary"),
            vmem_limit_bytes=VMEM_LIMIT),
        name="swa",
    )(sink, q, kctx, vctx)


def _ssm_kernel(u_ref, h0r_ref, h0i_ref, lam_ref, bre_ref, bim_ref, cre_ref, cim_ref,
                d_ref, wglu_ref, bglu_ref,
                y_ref, hr_out, hi_out, sr, si, hr_s, hi_s):
    L, B, _ = u_ref.shape
    rows = L * B
    half_w = SSM_WIDTH // 2
    half_c = SSM_COLS // 2

    @pl.when(pl.program_id(0) == 0)
    def _():
        hr_s[...] = h0r_ref[...]
        hi_s[...] = h0i_ref[...]

    u = u_ref[...].reshape(rows, SSM_WIDTH)
    ub = u.astype(BF16)
    for hf in range(2):
        uh = ub[:, hf * half_w:(hf + 1) * half_w]
        sr[:, hf * half_c:(hf + 1) * half_c] = _mm(uh, bre_ref[hf])
        si[:, hf * half_c:(hf + 1) * half_c] = _mm(uh, bim_ref[hf])

    cw = 4 * LANES
    for cc in range(SSM_COLS // cw):
        cols = slice(cc * cw, (cc + 1) * cw)
        lr = jnp.broadcast_to(lam_ref[0:1, cols], (B, cw))
        li = jnp.broadcast_to(lam_ref[1:2, cols], (B, cw))

        def body(t, carry):
            hr, hi = carry
            at_t = pl.ds(pl.multiple_of(t * B, B), B)
            nr = lr * hr - li * hi + sr[at_t, cols]
            ni = lr * hi + li * hr + si[at_t, cols]
            sr[at_t, cols] = nr
            si[at_t, cols] = ni
            return nr, ni

        hr, hi = lax.fori_loop(0, L, body, (hr_s[:, cols], hi_s[:, cols]), unroll=2)
        hr_s[:, cols] = hr
        hi_s[:, cols] = hi

    ys = []
    for hf in range(2):
        cs = slice(hf * half_c, (hf + 1) * half_c)
        ys.append(_mm(sr[:, cs].astype(BF16), cre_ref[hf])
                  + _mm(si[:, cs].astype(BF16), cim_ref[hf]))
    y = jnp.concatenate(ys, axis=1) + d_ref[...] * u
    g = 0.5 * y * (1.0 + jnp.tanh(math.sqrt(2.0 / math.pi) * (y + 0.044715 * (y * y * y))))
    gb = g.astype(BF16)
    z = jnp.concatenate(
        [_mm(gb[:, hf * half_w:(hf + 1) * half_w], wglu_ref[hf]) for hf in range(2)],
        axis=1) + bglu_ref[...]
    out = g * (1.0 / (1.0 + jnp.exp(-z)))
    y_ref[...] = out.reshape(L, B, SSM_WIDTH)
    hr_out[...] = hr_s[...]
    hi_out[...] = hi_s[...]


def _ssm(u, h0r, h0i, sp, L):
    S, B, _ = u.shape
    c2 = lambda i: (0, 0)
    c3 = lambda i: (0, 0, 0)
    return pl.pallas_call(
        _ssm_kernel,
        grid=(S // L,),
        in_specs=[
            pl.BlockSpec((L, B, SSM_WIDTH), lambda i: (i, 0, 0)),
            pl.BlockSpec((B, SSM_COLS), c2),
            pl.BlockSpec((B, SSM_COLS), c2),
            pl.BlockSpec((2, SSM_COLS), c2),
            pl.BlockSpec((2, SSM_WIDTH // 2, SSM_COLS // 2), c3),
            pl.BlockSpec((2, SSM_WIDTH // 2, SSM_COLS // 2), c3),
            pl.BlockSpec((2, SSM_COLS // 2, SSM_WIDTH // 2), c3),
            pl.BlockSpec((2, SSM_COLS // 2, SSM_WIDTH // 2), c3),
            pl.BlockSpec((1, SSM_WIDTH), c2),
            pl.BlockSpec((2, SSM_WIDTH // 2, SSM_WIDTH // 2), c3),
            pl.BlockSpec((1, SSM_WIDTH), c2),
        ],
        out_specs=[
            pl.BlockSpec((L, B, SSM_WIDTH), lambda i: (i, 0, 0)),
            pl.BlockSpec((B, SSM_COLS), c2),
            pl.BlockSpec((B, SSM_COLS), c2),
        ],
        out_shape=[
            jax.ShapeDtypeStruct((S, B, SSM_WIDTH), F32),
            jax.ShapeDtypeStruct((B, SSM_COLS), F32),
            jax.ShapeDtypeStruct((B, SSM_COLS), F32),
        ],
        scratch_shapes=[
            pltpu.VMEM((L * B, SSM_COLS), F32),
            pltpu.VMEM((L * B, SSM_COLS), F32),
            pltpu.VMEM((B, SSM_COLS), F32),
            pltpu.VMEM((B, SSM_COLS), F32),
        ],
        compiler_params=pltpu.CompilerParams(
            dimension_semantics=("arbitrary",), vmem_limit_bytes=VMEM_LIMIT),
        name="ssm",
    )(u, h0r, h0i, sp["lam"], sp["bre"], sp["bim"], sp["cre"], sp["cim"],
      sp["d"], sp["wglu"], sp["bglu"])


def _block_diag(blocks):
    G, r, c = blocks.shape
    eye = jnp.eye(G, dtype=blocks.dtype)
    return jnp.einsum("grc,gh->grhc", blocks, eye).reshape(G * r, G * c)


def _ssm_params(lam_re, lam_im, log_dt, b_re, b_im, c_re, c_im, d, w_glu, b_glu):
    lam = lax.complex(lam_re.astype(F32), lam_im.astype(F32))
    dt = jnp.exp(log_dt.astype(F32))[:, None]
    lam_bar = jnp.exp(lam * dt)
    bmat = lax.complex(b_re.astype(F32), b_im.astype(F32))
    b_bar = ((lam_bar - 1.0) / lam)[..., None] * bmat
    lam2 = jnp.stack([lam_bar.real.reshape(-1), lam_bar.imag.reshape(-1)])
    bt = jnp.swapaxes(b_bar, 1, 2)
    hw, hc = SSM_WIDTH // 2, SSM_COLS // 2
    split_b = lambda m: jnp.stack([m[:hw, :hc], m[hw:, hc:]]).astype(BF16)
    split_c = lambda m: jnp.stack([m[:hc, :hw], m[hc:, hw:]]).astype(BF16)
    ct_re = jnp.swapaxes(c_re.astype(F32), 1, 2)
    ct_im = jnp.swapaxes(c_im.astype(F32), 1, 2)
    wg = _block_diag(w_glu.astype(F32))
    return {
        "lam": lam2,
        "bre": split_b(_block_diag(bt.real)),
        "bim": split_b(_block_diag(bt.imag)),
        "cre": split_c(_block_diag(ct_re)),
        "cim": split_c(_block_diag(-ct_im)),
        "d": d.astype(F32).reshape(1, SSM_WIDTH),
        "wglu": jnp.stack([wg[:hw, :hw], wg[hw:, hw:]]).astype(BF16),
        "bglu": b_glu.astype(F32).reshape(1, SSM_WIDTH),
    }


def _memkv_kernel(m_ref, g_ref, w_ref, gk_ref, k_ref, v_ref):
    m = _rms(m_ref[...], g_ref[...])
    kv = _mm(m.astype(BF16), w_ref[...])
    for h in range(CA_HEADS):
        sl = slice(h * CA_HEAD_DIM, (h + 1) * CA_HEAD_DIM)
        k_ref[:, sl] = _rms(kv[:, sl], gk_ref[...])
    v_ref[...] = kv[:, CA_WIDTH:]


def _memkv(mem2d, g, w_bf, gk, tm):
    T = mem2d.shape[0]
    full = lambda i: (0, 0)
    return pl.pallas_call(
        _memkv_kernel,
        grid=(T // tm,),
        in_specs=[
            pl.BlockSpec((tm, D_MODEL), lambda i: (i, 0)),
            pl.BlockSpec((1, D_MODEL), full),
            pl.BlockSpec((D_MODEL, 2 * CA_WIDTH), full),
            pl.BlockSpec((1, CA_HEAD_DIM), full),
        ],
        out_specs=[
            pl.BlockSpec((tm, CA_WIDTH), lambda i: (i, 0)),
            pl.BlockSpec((tm, CA_WIDTH), lambda i: (i, 0)),
        ],
        out_shape=[
            jax.ShapeDtypeStruct((T, CA_WIDTH), F32),
            jax.ShapeDtypeStruct((T, CA_WIDTH), F32),
        ],
        compiler_params=pltpu.CompilerParams(
            dimension_semantics=("arbitrary",), vmem_limit_bytes=VMEM_LIMIT),
        name="memkv",
    )(mem2d, g, w_bf, gk)


def _mid_kernel(x_ref, att_ref, ssm_ref, mk_ref, mv_ref,
                gao_ref, gso_ref, wout_ref, gx_ref, wcq_ref, gcq_ref, wco_ref,
                gffn_ref, wr_ref, br_ref,
                x2_ref, hn_ref, rt_ref, rtt_ref, cnt_ref, base_s):
    tm = x_ref.shape[1]

    @pl.when((pl.program_id(0) == 0) & (pl.program_id(1) == 0))
    def _():
        base_s[...] = jnp.zeros_like(base_s)

    a = _rms(att_ref[0], gao_ref[...]).astype(BF16)
    s = _rms(ssm_ref[...], gso_ref[...]).astype(BF16)
    x1 = (x_ref[0] + _mm(a, wout_ref[0:ATT_WIDTH, :])
          + _mm(s, wout_ref[ATT_WIDTH:, :]))

    qx = _mm(_rms(x1, gx_ref[...]).astype(BF16), wcq_ref[...])
    heads = []
    for h in range(CA_HEADS):
        sl = slice(h * CA_HEAD_DIM, (h + 1) * CA_HEAD_DIM)
        qh = _rms(qx[:, sl], gcq_ref[...]).astype(BF16)
        kh = mk_ref[0, :, sl].astype(BF16)
        vh = mv_ref[0, :, sl].astype(BF16)
        sc = lax.dot_general(qh, kh, (((1,), (1,)), ((), ())),
                             preferred_element_type=F32) * (CA_HEAD_DIM ** -0.5)
        p = jnp.exp(sc - jnp.max(sc, axis=-1, keepdims=True))
        p = p / jnp.sum(p, axis=-1, keepdims=True)
        heads.append(_mm(p.astype(BF16), vh))
    o = jnp.concatenate(heads, axis=1).astype(BF16)
    x2 = x1 + _mm(o, wco_ref[...])
    x2_ref[0] = x2

    hn = _rms(x2, gffn_ref[...])
    hn_ref[0] = hn

    h_hi = hn.astype(BF16)
    h_lo = (hn - h_hi.astype(F32)).astype(BF16)
    r1 = _mm(h_hi, wr_ref[...])
    lg = (r1[:, :LANES] + r1[:, LANES:] + _mm(h_lo, wr_ref[:, 0:LANES])
          + br_ref[...])

    col = lax.broadcasted_iota(jnp.int32, (tm, LANES), 1)
    big = jnp.int32(4 * LANES)
    gmask = col < N_EXPERT_GROUPS
    lgg = jnp.where(gmask, lg, NEG)
    mg = jnp.max(lgg, axis=-1, keepdims=True)
    grp = jnp.min(jnp.where(gmask & (lgg == mg), col, big), axis=-1, keepdims=True)
    pg_top = 1.0 / jnp.sum(jnp.where(gmask, jnp.exp(lgg - mg), 0.0), axis=-1, keepdims=True)

    ecol = col - ROUTER_COL0
    emask = ((ecol >= 0) & (ecol < N_EXPERTS)
             & (lax.shift_right_arithmetic(ecol, 3) == grp))
    le = jnp.where(emask, lg, NEG)
    m1 = jnp.max(le, axis=-1, keepdims=True)
    i1 = jnp.min(jnp.where(emask & (le == m1), col, big), axis=-1, keepdims=True)
    rest = emask & (col != i1)
    le2 = jnp.where(rest, lg, NEG)
    m2 = jnp.max(le2, axis=-1, keepdims=True)
    i2 = jnp.min(jnp.where(rest & (le2 == m2), col, big), axis=-1, keepdims=True)
    den = jnp.sum(jnp.where(emask, jnp.exp(le - m1), 0.0), axis=-1, keepdims=True)
    p1 = 1.0 / den
    p2 = jnp.exp(m2 - m1) / den
    gate1 = pg_top * p1 / (p1 + p2)
    gate2 = pg_top * p2 / (p1 + p2)

    sel1 = col == i1
    sel2 = col == i2
    oh = jnp.where(sel1 | sel2, 1.0, 0.0)
    r_i = lax.broadcasted_iota(jnp.int32, (tm, tm), 0)
    c_i = lax.broadcasted_iota(jnp.int32, (tm, tm), 1)
    tri = jnp.where(r_i > c_i, 1.0, 0.0).astype(BF16)
    tot = base_s[...] + _mm(tri, oh.astype(BF16))
    rank1 = jnp.sum(jnp.where(sel1, tot, 0.0), axis=-1, keepdims=True)
    rank2 = jnp.sum(jnp.where(sel2, tot, 0.0), axis=-1, keepdims=True)
    base_s[...] = base_s[...] + jnp.sum(oh, axis=0, keepdims=True)
    cnt_ref[...] = base_s[...]

    e1 = (i1 - ROUTER_COL0).astype(F32)
    e2 = (i2 - ROUTER_COL0).astype(F32)
    rt = jnp.zeros((tm, LANES), F32)
    for k, val in enumerate((e1, e2, gate1, gate2, rank1, rank2)):
        rt = jnp.where(col == k, val, rt)
    rt_ref[0] = rt
    rtt_ref[0] = rt.T[0:8, :]


def _mid(x, att, ssm_tm, mk, mv, wp, tm):
    B, S, _ = x.shape
    c2 = lambda b, i: (0, 0)
    tile = lambda w: pl.BlockSpec((1, tm, w), lambda b, i: (b, i, 0))
    return pl.pallas_call(
        _mid_kernel,
        grid=(B, S // tm),
        in_specs=[
            tile(D_MODEL), tile(ATT_WIDTH),
            pl.BlockSpec((tm, SSM_WIDTH), lambda b, i: (i, b)),
            pl.BlockSpec((1, N_MEM, CA_WIDTH), lambda b, i: (b, 0, 0)),
            pl.BlockSpec((1, N_MEM, CA_WIDTH), lambda b, i: (b, 0, 0)),
            pl.BlockSpec((1, ATT_WIDTH), c2),
            pl.BlockSpec((1, SSM_WIDTH), c2),
            pl.BlockSpec((ATT_WIDTH + SSM_WIDTH, D_MODEL), c2),
            pl.BlockSpec((1, D_MODEL), c2),
            pl.BlockSpec((D_MODEL, CA_WIDTH), c2),
            pl.BlockSpec((1, CA_HEAD_DIM), c2),
            pl.BlockSpec((CA_WIDTH, D_MODEL), c2),
            pl.BlockSpec((1, D_MODEL), c2),
            pl.BlockSpec((D_MODEL, 2 * LANES), c2),
            pl.BlockSpec((1, LANES), c2),
        ],
        out_specs=[
            tile(D_MODEL), tile(D_MODEL), tile(LANES),
            pl.BlockSpec((1, 8, tm), lambda b, i: (b, 0, i)),
            pl.BlockSpec((1, LANES), c2),
        ],
        out_shape=[
            jax.ShapeDtypeStruct((B, S, D_MODEL), F32),
            jax.ShapeDtypeStruct((B, S, D_MODEL), F32),
            jax.ShapeDtypeStruct((B, S, LANES), F32),
            jax.ShapeDtypeStruct((B, 8, S), F32),
            jax.ShapeDtypeStruct((1, LANES), F32),
        ],
        scratch_shapes=[pltpu.VMEM((1, LANES), F32)],
        compiler_params=pltpu.CompilerParams(
            dimension_semantics=("arbitrary", "arbitrary"),
            vmem_limit_bytes=VMEM_LIMIT),
        name="mid",
    )(x, att, ssm_tm, mk, mv, wp["gao"], wp["gso"], wp["wout"], wp["gx"], wp["wcq"],
      wp["gcq"], wp["wco"], wp["gffn"], wp["wr"], wp["br"])


def _dispatch_kernel(pend_ref, padded_ref, dest_ref, hn_ref, xs_hbm, stage, zbuf, sem):
    tm = hn_ref.shape[0]
    i = pl.program_id(0)
    slot = lax.rem(i, 2)

    def wait_rows(s):
        for _ in range(2):
            pltpu.make_async_copy(stage.at[s], xs_hbm.at[pl.ds(0, tm)], sem.at[s]).wait()

    @pl.when(i == 0)
    def _():
        zbuf[...] = jnp.zeros_like(zbuf)
        for e in range(N_EXPERTS):
            @pl.when(padded_ref[e] > 0)
            def _():
                row0 = pl.multiple_of(pend_ref[e] - MOE_BLOCK, MOE_BLOCK)
                fill = pltpu.make_async_copy(
                    zbuf, xs_hbm.at[pl.ds(row0, MOE_BLOCK)], sem.at[2])
                fill.start()
                fill.wait()

        def fill_tail(b, carry):
            fill = pltpu.make_async_copy(
                zbuf, xs_hbm.at[pl.ds(pl.multiple_of(b * MOE_BLOCK, MOE_BLOCK), MOE_BLOCK)],
                sem.at[2])
            fill.start()
            fill.wait()
            return carry

        lax.fori_loop(pend_ref[N_EXPERTS - 1] // MOE_BLOCK, xs_hbm.shape[0] // MOE_BLOCK,
                      fill_tail, 0)

    @pl.when(i >= 2)
    def _():
        wait_rows(slot)

    stage[slot] = hn_ref[...]
    for k in range(2):
        for r in range(tm):
            pltpu.make_async_copy(stage.at[slot, pl.ds(r, 1), :],
                                  xs_hbm.at[pl.ds(dest_ref[0, 0, k * tm + r], 1), :],
                                  sem.at[slot]).start(priority=r % 2)

    @pl.when(i == pl.num_programs(0) - 1)
    def _():
        wait_rows(slot)

        @pl.when(i >= 1)
        def _():
            wait_rows(1 - slot)


def _dispatch(pad_end, padded, dest_t, hn, rows, tm):
    nt = hn.shape[0] // tm
    grid_spec = pltpu.PrefetchScalarGridSpec(
        num_scalar_prefetch=2,
        grid=(nt,),
        in_specs=[
            pl.BlockSpec((1, 1, 2 * tm), lambda i, pe, pd: (i, 0, 0), memory_space=pltpu.SMEM),
            pl.BlockSpec((tm, D_MODEL), lambda i, pe, pd: (i, 0)),
        ],
        out_specs=pl.BlockSpec(memory_space=pl.ANY),
        scratch_shapes=[
            pltpu.VMEM((2, tm, D_MODEL), F32),
            pltpu.VMEM((MOE_BLOCK, D_MODEL), F32),
            pltpu.SemaphoreType.DMA((3,)),
        ],
    )
    return pl.pallas_call(
        _dispatch_kernel,
        grid_spec=grid_spec,
        out_shape=jax.ShapeDtypeStruct((rows, D_MODEL), F32),
        compiler_params=pltpu.CompilerParams(
            dimension_semantics=("arbitrary",), vmem_limit_bytes=VMEM_LIMIT),
        name="dispatch",
    )(pad_end, padded, dest_t, hn)


def _moe_kernel(be_ref, nu_ref, xs_ref, wg_ref, wu_ref, wd_ref, yb_ref, wg_s, wu_s, wd_s):
    i = pl.program_id(0)

    @pl.when(i < nu_ref[0])
    def _():
        @pl.when((i == 0) | (be_ref[i] != be_ref[jnp.maximum(i - 1, 0)]))
        def _():
            wg_s[...] = wg_ref[0].astype(BF16)
            wu_s[...] = wu_ref[0].astype(BF16)
            wd_s[...] = wd_ref[0].astype(BF16)

        xe = xs_ref[...].astype(BF16)
        g = _mm(xe, wg_s[...])
        u = _mm(xe, wu_s[...])
        hmid = ((g * (1.0 / (1.0 + jnp.exp(-g)))) * u).astype(BF16)
        yb_ref[...] = _mm(hmid, wd_s[...])

    @pl.when(i >= nu_ref[0])
    def _():
        yb_ref[...] = jnp.zeros_like(yb_ref)


def _moe(block_e, n_used, xs, w_gate, w_up, w_down):
    n_blocks = block_e.shape[0]
    in_blk = lambda i, be, nu: (jnp.maximum(jnp.minimum(i, nu[0] - 1), 0), 0)
    grid_spec = pltpu.PrefetchScalarGridSpec(
        num_scalar_prefetch=2,
        grid=(n_blocks,),
        in_specs=[
            pl.BlockSpec((MOE_BLOCK, D_MODEL), in_blk),
            pl.BlockSpec((1, D_MODEL, D_EXPERT), lambda i, be, nu: (be[i], 0, 0)),
            pl.BlockSpec((1, D_MODEL, D_EXPERT), lambda i, be, nu: (be[i], 0, 0)),
            pl.BlockSpec((1, D_EXPERT, D_MODEL), lambda i, be, nu: (be[i], 0, 0)),
        ],
        out_specs=pl.BlockSpec((MOE_BLOCK, D_MODEL), lambda i, be, nu: (i, 0)),
        scratch_shapes=[
            pltpu.VMEM((D_MODEL, D_EXPERT), BF16),
            pltpu.VMEM((D_MODEL, D_EXPERT), BF16),
            pltpu.VMEM((D_EXPERT, D_MODEL), BF16),
        ],
    )
    return pl.pallas_call(
        _moe_kernel,
        grid_spec=grid_spec,
        out_shape=jax.ShapeDtypeStruct(xs.shape, F32),
        compiler_params=pltpu.CompilerParams(
            dimension_semantics=("arbitrary",), vmem_limit_bytes=VMEM_LIMIT),
        name="moe",
    )(block_e, n_used, xs, w_gate, w_up, w_down)


def _combine_kernel(dest_ref, dest_next_ref, x2_ref, rt_ref, yb_hbm, o_ref, buf, sem):
    tm = x2_ref.shape[0]
    i = pl.program_id(0)
    slot = lax.rem(i, 2)

    def gather(d_ref, s):
        for k in range(2):
            for r in range(tm):
                pltpu.make_async_copy(yb_hbm.at[pl.ds(d_ref[0, 0, k * tm + r], 1), :],
                                      buf.at[s, k, pl.ds(r, 1), :],
                                      sem.at[s]).start(priority=r % 2)

    @pl.when(i == 0)
    def _():
        gather(dest_ref, 0)

    @pl.when(i + 1 < pl.num_programs(0))
    def _():
        gather(dest_next_ref, 1 - slot)

    for k in range(2):
        pltpu.make_async_copy(yb_hbm.at[pl.ds(0, tm), :], buf.at[slot, k], sem.at[slot]).wait()
    rt = rt_ref[...]
    o_ref[...] = x2_ref[...] + rt[:, 2:3] * buf[slot, 0] + rt[:, 3:4] * buf[slot, 1]


def _combine(dest_t, x2, rt, yb, tm):
    T = x2.shape[0]
    nt = T // tm
    return pl.pallas_call(
        _combine_kernel,
        grid=(nt,),
        in_specs=[
            pl.BlockSpec((1, 1, 2 * tm), lambda i: (i, 0, 0), memory_space=pltpu.SMEM),
            pl.BlockSpec((1, 1, 2 * tm), lambda i: (jnp.minimum(i + 1, nt - 1), 0, 0),
                         memory_space=pltpu.SMEM),
            pl.BlockSpec((tm, D_MODEL), lambda i: (i, 0)),
            pl.BlockSpec((tm, LANES), lambda i: (i, 0)),
            pl.BlockSpec(memory_space=pl.ANY),
        ],
        out_specs=pl.BlockSpec((tm, D_MODEL), lambda i: (i, 0)),
        out_shape=jax.ShapeDtypeStruct((T, D_MODEL), F32),
        scratch_shapes=[
            pltpu.VMEM((2, 2, tm, D_MODEL), F32),
            pltpu.SemaphoreType.DMA((2,)),
        ],
        compiler_params=pltpu.CompilerParams(
            dimension_semantics=("arbitrary",), vmem_limit_bytes=VMEM_LIMIT),
        name="combine",
    )(dest_t, dest_t, x2, rt, yb)


def _hier_moe(x2, hn, rt, rtt, cnt, w_gate, w_up, w_down, tm):
    T = x2.shape[0]
    counts = cnt[0, ROUTER_COL0:ROUTER_COL0 + N_EXPERTS].astype(jnp.int32)
    padded = (counts + MOE_BLOCK - 1) // MOE_BLOCK * MOE_BLOCK
    pad_end = jnp.cumsum(padded)
    pad_start = pad_end - padded
    flat = lambda a: jnp.swapaxes(a, 0, 1).reshape(a.shape[1], T)
    eid = flat(rtt[:, 0:2, :]).astype(jnp.int32)
    rank = flat(rtt[:, 4:6, :]).astype(jnp.int32)
    experts = jnp.arange(N_EXPERTS, dtype=jnp.int32)[:, None, None]
    dest = rank + jnp.sum(jnp.where(eid[None] == experts, pad_start[:, None, None], 0), axis=0)
    n_blocks = (2 * T + N_EXPERTS * (MOE_BLOCK - 1)) // MOE_BLOCK + 1
    rows = n_blocks * MOE_BLOCK
    blk_row0 = jnp.arange(n_blocks, dtype=jnp.int32) * MOE_BLOCK
    block_e = jnp.minimum(
        jnp.sum((pad_end[None, :] <= blk_row0[:, None]).astype(jnp.int32), axis=1),
        N_EXPERTS - 1)
    n_used = (pad_end[-1] // MOE_BLOCK).astype(jnp.int32).reshape(1)
    nt = T // tm
    dest_t = dest.reshape(2, nt, tm).transpose(1, 0, 2).reshape(nt, 1, 2 * tm)
    xs = _dispatch(pad_end, padded, dest_t, hn, rows, tm)
    yb = _moe(block_e, n_used, xs, w_gate, w_up, w_down)
    return _combine(dest_t, x2, rt, yb, tm)


def _rope_table(pos):
    half = ROPE_DIM // 2
    inv = ROPE_THETA ** (-jnp.arange(0, ROPE_DIM, 2, dtype=F32) / ROPE_DIM)
    ang = pos.astype(F32)[:, None] * inv[None, :]
    cos, sin = jnp.cos(ang), jnp.sin(ang)
    L = pos.shape[0]
    pad = jnp.zeros((L, HEAD_DIM - ROPE_DIM), F32)
    zero = jnp.zeros((L, half), F32)
    c64 = jnp.concatenate([cos, cos, pad + 1.0], axis=1)
    lo64 = jnp.concatenate([-sin, zero, pad], axis=1)
    hi64 = jnp.concatenate([zero, sin, pad], axis=1)
    two = lambda t: jnp.concatenate([t, t], axis=1)
    return jnp.concatenate([two(c64), two(lo64), two(hi64)], axis=1)


def _layer(x, pos_rope, kctx_prev, vctx_prev, h0r, h0i, mk, mv, wp, sp, ew, *,
           tm_in, tq, ssm_l, tm_mid, tm_comb, mask_context):
    B, S, _ = x.shape
    T = B * S
    q, k3, v3, u_tm = _in_proj(x, wp["gmix"], wp["win"], wp["gq"], wp["gk"], pos_rope, tm_in)
    kctx = jnp.concatenate([kctx_prev, k3], axis=1)
    vctx = jnp.concatenate([vctx_prev, v3], axis=1)
    att = _swa(wp["sink"], q, kctx, vctx, tq, mask_context)
    ssm_tm, hr, hi = _ssm(u_tm.reshape(S, B, SSM_WIDTH), h0r, h0i, sp, ssm_l)
    x2, hn, rt, rtt, cnt = _mid(x, att, ssm_tm.reshape(S, B * SSM_WIDTH), mk, mv, wp, tm_mid)
    y = _hier_moe(x2.reshape(T, D_MODEL), hn.reshape(T, D_MODEL),
                  rt.reshape(T, LANES), rtt, cnt, *ew, tm_comb)
    return y.reshape(B, S, D_MODEL), k3, v3, hr, hi


def kernel(x_prompt, x_sample, cache_attn_k, cache_attn_v, state_ssm_re, state_ssm_im, cache_mem_k, cache_mem_v, mem_prompt, norm_mix, w_in, q_norm, k_norm, attn_sink, ssm_lambda_re, ssm_lambda_im, ssm_log_dt, ssm_b_re, ssm_b_im, ssm_c_re, ssm_c_im, ssm_d, ssm_w_glu, ssm_b_glu, norm_attn_out, norm_ssm_out, w_out, norm_cross, norm_mem, w_cq, w_ck, w_cv, cq_norm, ck_norm, w_co, norm_ffn, w_router_group, b_router_group, w_router_expert, b_router_expert, w_e_gate, w_e_up, w_e_down):
    depth = norm_mix.shape[0]
    Bp, Lp, _ = x_prompt.shape
    Bs, Ls, _ = x_sample.shape
    yp, ys = x_prompt, x_sample
    rope_p = _rope_table(jnp.arange(Lp, dtype=jnp.int32))
    rope_s = _rope_table(PAST_LEN + jnp.arange(Ls, dtype=jnp.int32))
    outs = [[] for _ in range(10)]
    n_router = N_EXPERT_GROUPS + N_EXPERTS
    for l in range(depth):
        row = lambda a: a[l].astype(F32).reshape(1, -1)
        w_r = jnp.pad(jnp.concatenate([w_router_group[l], w_router_expert[l]], axis=1).astype(F32),
                      ((0, 0), (0, LANES - n_router)))
        w_r_hi = w_r.astype(BF16)
        w_r_lo = (w_r - w_r_hi.astype(F32)).astype(BF16)
        b_r = jnp.pad(jnp.concatenate([b_router_group[l], b_router_expert[l]]).astype(F32),
                      (0, LANES - n_router)).reshape(1, LANES)
        wp = {
            "gmix": row(norm_mix), "win": w_in[l].astype(BF16),
            "gq": jnp.tile(row(q_norm), (1, LANES // HEAD_DIM)),
            "gk": jnp.tile(row(k_norm), (1, LANES // HEAD_DIM)),
            "sink": attn_sink[l].astype(F32),
            "gao": row(norm_attn_out), "gso": row(norm_ssm_out),
            "wout": w_out[l].astype(BF16), "gx": row(norm_cross),
            "wcq": w_cq[l].astype(BF16), "gcq": row(cq_norm),
            "wco": w_co[l].astype(BF16), "gffn": row(norm_ffn),
            "wr": jnp.concatenate([w_r_hi, w_r_lo], axis=1), "br": b_r,
        }
        sp = _ssm_params(ssm_lambda_re[l], ssm_lambda_im[l], ssm_log_dt[l], ssm_b_re[l],
                         ssm_b_im[l], ssm_c_re[l], ssm_c_im[l], ssm_d[l], ssm_w_glu[l],
                         ssm_b_glu[l])
        ew = (w_e_gate[l].astype(F32), w_e_up[l].astype(F32), w_e_down[l].astype(F32))

        w_ckv = jnp.concatenate([w_ck[l], w_cv[l]], axis=1).astype(BF16)
        mkp, mvp = _memkv(mem_prompt.reshape(Bp * N_MEM, D_MODEL), row(norm_mem), w_ckv,
                          row(ck_norm), 512)
        mkp = mkp.reshape(Bp, N_MEM, CA_WIDTH)
        mvp = mvp.reshape(Bp, N_MEM, CA_WIDTH)

        zctx = jnp.zeros((Bp, WINDOW, KV_WIDTH), F32)
        zst = jnp.zeros((Bp, SSM_COLS), F32)
        yp, kp, vp, hpr, hpi = _layer(
            yp, rope_p, zctx, zctx, zst, zst, mkp, mvp, wp, sp, ew,
            tm_in=512, tq=256, ssm_l=64, tm_mid=512, tm_comb=256, mask_context=True)
        ys, kn, vn, hsr, hsi = _layer(
            ys, rope_s, cache_attn_k[l].reshape(Bs, WINDOW, KV_WIDTH).astype(F32),
            cache_attn_v[l].reshape(Bs, WINDOW, KV_WIDTH).astype(F32),
            state_ssm_re[l].astype(F32).reshape(Bs, SSM_COLS),
            state_ssm_im[l].astype(F32).reshape(Bs, SSM_COLS),
            cache_mem_k[l].astype(F32).reshape(Bs, N_MEM, CA_WIDTH),
            cache_mem_v[l].astype(F32).reshape(Bs, N_MEM, CA_WIDTH), wp, sp, ew,
            tm_in=Ls, tq=CHUNK, ssm_l=Ls, tm_mid=Ls, tm_comb=256, mask_context=False)

        sg = (N_SSM_GROUPS, SSM_STATE)
        kvs = (N_KV_HEADS, HEAD_DIM)
        vals = (kp[:, Lp - WINDOW:].reshape(Bp, WINDOW, *kvs),
                vp[:, Lp - WINDOW:].reshape(Bp, WINDOW, *kvs),
                hpr.reshape(Bp, *sg), hpi.reshape(Bp, *sg),
                mkp.reshape(Bp, N_MEM, CA_HEADS, CA_HEAD_DIM),
                mvp.reshape(Bp, N_MEM, CA_HEADS, CA_HEAD_DIM),
                kn.reshape(Bs, Ls, *kvs), vn.reshape(Bs, Ls, *kvs),
                hsr.reshape(Bs, *sg), hsi.reshape(Bs, *sg))
        for lst, val in zip(outs, vals):
            lst.append(val)
    return (yp, ys) + tuple(jnp.stack(lst) for lst in outs)
```

```python
import functools
import math

import jax
import jax.numpy as jnp
from jax import lax
from jax.experimental import pallas as pl
from jax.experimental.pallas import tpu as pltpu

F32 = jnp.float32
BF16 = jnp.bfloat16

D_MODEL = 1024
CHUNK = 64
N_Q_HEADS = 8
N_KV_HEADS = 2
GQA = N_Q_HEADS // N_KV_HEADS
HEAD_DIM = 64
WINDOW = 128
BAND = WINDOW + CHUNK
ROPE_DIM = HEAD_DIM // 4
ROPE_THETA = 500000.0
ATT_WIDTH = N_Q_HEADS * HEAD_DIM
KV_WIDTH = N_KV_HEADS * HEAD_DIM
SSM_GROUP = 16
SSM_WIDTH = D_MODEL // 2
N_SSM_GROUPS = SSM_WIDTH // SSM_GROUP
SSM_STATE = 64
SSM_COLS = N_SSM_GROUPS * SSM_STATE
IN_WIDTH = ATT_WIDTH + 2 * KV_WIDTH + SSM_WIDTH
N_MEM = 256
CA_HEADS = 4
CA_HEAD_DIM = 128
CA_WIDTH = CA_HEADS * CA_HEAD_DIM
N_EXPERT_GROUPS = 4
EXPERTS_PER_GROUP = 8
N_EXPERTS = N_EXPERT_GROUPS * EXPERTS_PER_GROUP
D_EXPERT = 512
MOE_BLOCK = 256
EPS = 1e-6
NEG = -1e30
PAST_LEN = 4096

LANES = 128
ROUTER_COL0 = N_EXPERT_GROUPS
VMEM_LIMIT = 48 * 1024 * 1024


def _rms(x, g):
    ms = jnp.mean(x * x, axis=-1, keepdims=True)
    return (x * lax.rsqrt(ms + EPS)) * g


def _mm(a, b):
    return jnp.dot(a, b, preferred_element_type=F32)


def _in_proj_kernel(x_ref, g_ref, w_ref, gq_ref, gk_ref, rope_ref,
                    q_ref, k_ref, v_ref, u_ref):
    tm = x_ref.shape[1]
    h = _rms(x_ref[0], g_ref[...])
    hin = _mm(h.astype(BF16), w_ref[...])
    rope = rope_ref[...]
    cos = rope[:, 0:LANES]
    sin_lo = rope[:, LANES:2 * LANES]
    sin_hi = rope[:, 2 * LANES:3 * LANES]
    lane = lax.broadcasted_iota(jnp.int32, (tm, LANES), 1)
    left = lane < HEAD_DIM

    def norm_rope(z, g):
        sq = z * z
        lsum = jnp.sum(jnp.where(left, sq, 0.0), axis=-1, keepdims=True)
        rsum = jnp.sum(jnp.where(left, 0.0, sq), axis=-1, keepdims=True)
        ms = jnp.where(left, lsum, rsum) * (1.0 / HEAD_DIM)
        zn = (z * lax.rsqrt(ms + EPS)) * g
        half = ROPE_DIM // 2
        return (zn * cos + pltpu.roll(zn, LANES - half, 1) * sin_lo
                + pltpu.roll(zn, half, 1) * sin_hi)

    for j in range(ATT_WIDTH // LANES):
        sl = slice(j * LANES, (j + 1) * LANES)
        q_ref[0, :, sl] = norm_rope(hin[:, sl], gq_ref[...])
    k_ref[0] = norm_rope(hin[:, ATT_WIDTH:ATT_WIDTH + KV_WIDTH], gk_ref[...])
    v_ref[0] = hin[:, ATT_WIDTH + KV_WIDTH:ATT_WIDTH + 2 * KV_WIDTH]
    u_ref[...] = hin[:, ATT_WIDTH + 2 * KV_WIDTH:]


def _in_proj(x, g, w_bf, gq, gk, rope, tm):
    B, S, _ = x.shape
    full = lambda b, i: (0, 0)
    tile = lambda w: pl.BlockSpec((1, tm, w), lambda b, i: (b, i, 0))
    return pl.pallas_call(
        _in_proj_kernel,
        grid=(B, S // tm),
        in_specs=[
            tile(D_MODEL),
            pl.BlockSpec((1, D_MODEL), full),
            pl.BlockSpec((D_MODEL, IN_WIDTH), full),
            pl.BlockSpec((1, LANES), full),
            pl.BlockSpec((1, LANES), full),
            pl.BlockSpec((tm, 3 * LANES), lambda b, i: (i, 0)),
        ],
        out_specs=[
            tile(ATT_WIDTH), tile(KV_WIDTH), tile(KV_WIDTH),
            pl.BlockSpec((tm, SSM_WIDTH), lambda b, i: (i, b)),
        ],
        out_shape=[
            jax.ShapeDtypeStruct((B, S, ATT_WIDTH), F32),
            jax.ShapeDtypeStruct((B, S, KV_WIDTH), F32),
            jax.ShapeDtypeStruct((B, S, KV_WIDTH), F32),
            jax.ShapeDtypeStruct((S, B * SSM_WIDTH), F32),
        ],
        compiler_params=pltpu.CompilerParams(
            dimension_semantics=("arbitrary", "arbitrary"),
            vmem_limit_bytes=VMEM_LIMIT),
        name="in_proj",
    )(x, g, w_bf, gq, gk, rope)


def _swa_kernel(sink_ref, q_ref, k_ref, v_ref, o_ref, *, mask_context):
    tq = q_ref.shape[1]
    i = pl.program_id(1)
    nch = tq // CHUNK
    lane = lax.broadcasted_iota(jnp.int32, (BAND, LANES), 1)
    lo_half = lane < HEAD_DIM
    slabs_per_kv = GQA * HEAD_DIM // LANES

    units = []
    scores = []
    vpads = {}
    valids = {}
    for c in range(nch):
        start = pl.multiple_of((i * nch + c) * CHUNK, CHUNK)
        kb = k_ref[0, pl.ds(start, BAND), :]
        vb = v_ref[0, pl.ds(start, BAND), :]
        kb_sw = pltpu.roll(kb, HEAD_DIM, 1)
        vb_sw = pltpu.roll(vb, HEAD_DIM, 1)
        if mask_context:
            kidx = start + lax.broadcasted_iota(jnp.int32, (1, BAND), 1)
            valids[c] = kidx >= WINDOW
        for kvh in range(N_KV_HEADS):
            k_own, k_oth = (kb, kb_sw) if kvh == 0 else (kb_sw, kb)
            v_own, v_oth = (vb, vb_sw) if kvh == 0 else (vb_sw, vb)
            kpad = (jnp.where(lo_half, k_own, 0.0).astype(BF16),
                    jnp.where(lo_half, 0.0, k_oth).astype(BF16))
            vpads[(c, kvh)] = (jnp.where(lo_half, v_own, 0.0).astype(BF16),
                               jnp.where(lo_half, 0.0, v_oth).astype(BF16))
            col0 = kvh * GQA * HEAD_DIM
            q2 = jnp.concatenate(
                [q_ref[0, c * CHUNK:(c + 1) * CHUNK, col0 + m * LANES:col0 + (m + 1) * LANES]
                 for m in range(slabs_per_kv)], axis=0).astype(BF16)
            for side in range(2):
                s = lax.dot_general(q2, kpad[side], (((1,), (1,)), ((), ())),
                                    preferred_element_type=F32) * (HEAD_DIM ** -0.5)
                if mask_context:
                    s = jnp.where(valids[c], s, NEG)
                units.append((c, kvh, side))
                scores.append(s)

    s_all = jnp.concatenate(scores, axis=0)
    sk = jnp.concatenate(
        [jnp.full((CHUNK, 1), sink_ref[kvh * GQA + 2 * m + side], F32)
         for (_, kvh, side) in units for m in range(slabs_per_kv)], axis=0)
    mx = jnp.maximum(jnp.max(s_all, axis=-1, keepdims=True), sk)
    p_all = jnp.exp(s_all - mx)
    den = jnp.sum(p_all, axis=-1, keepdims=True) + jnp.exp(sk - mx)
    p_all = (p_all / den).astype(BF16)
    rows_u = slabs_per_kv * CHUNK
    probs = [p_all[n * rows_u:(n + 1) * rows_u] for n in range(len(units))]

    for n in range(0, len(units), 2):
        c, kvh, _ = units[n]
        vp = vpads[(c, kvh)]
        o = _mm(probs[n], vp[0]) + _mm(probs[n + 1], vp[1])
        col0 = kvh * GQA * HEAD_DIM
        for m in range(slabs_per_kv):
            o_ref[0, c * CHUNK:(c + 1) * CHUNK, col0 + m * LANES:col0 + (m + 1) * LANES] = (
                o[m * CHUNK:(m + 1) * CHUNK])


def _swa(sink, q, kctx, vctx, tq, mask_context):
    B, Sq, _ = q.shape
    Sk = kctx.shape[1]
    return pl.pallas_call(
        functools.partial(_swa_kernel, mask_context=mask_context),
        grid=(B, Sq // tq),
        in_specs=[
            pl.BlockSpec(memory_space=pltpu.SMEM),
            pl.BlockSpec((1, tq, ATT_WIDTH), lambda b, i: (b, i, 0)),
            pl.BlockSpec((1, Sk, KV_WIDTH), lambda b, i: (b, 0, 0)),
            pl.BlockSpec((1, Sk, KV_WIDTH), lambda b, i: (b, 0, 0)),
        ],
        out_specs=pl.BlockSpec((1, tq, ATT_WIDTH), lambda b, i: (b, i, 0)),
        out_shape=jax.ShapeDtypeStruct((B, Sq, ATT_WIDTH), F32),
        compiler_params=pltpu.CompilerParams(
            dimension_semantics=("arbitrary", "arbitrary"),
            vmem_limit_bytes=VMEM_LIMIT),
        name="swa",
    )(sink, q, kctx, vctx)


def _ssm_kernel(u_ref, h0r_ref, h0i_ref, lam_ref, bre_ref, bim_ref, cre_ref, cim_ref,
                d_ref, wglu_ref, bglu_ref,
                y_ref, hr_out, hi_out, sr, si, hr_s, hi_s):
    L, B, _ = u_ref.shape
    rows = L * B
    half_w = SSM_WIDTH // 2
    half_c = SSM_COLS // 2

    @pl.when(pl.program_id(0) == 0)
    def _():
        hr_s[...] = h0r_ref[...]
        hi_s[...] = h0i_ref[...]

    u = u_ref[...].reshape(rows, SSM_WIDTH)
    ub = u.astype(BF16)
    for hf in range(2):
        uh = ub[:, hf * half_w:(hf + 1) * half_w]
        sr[:, hf * half_c:(hf + 1) * half_c] = _mm(uh, bre_ref[hf])
        si[:, hf * half_c:(hf + 1) * half_c] = _mm(uh, bim_ref[hf])

    cw = 4 * LANES
    for cc in range(SSM_COLS // cw):
        cols = slice(cc * cw, (cc + 1) * cw)
        lr = jnp.broadcast_to(lam_ref[0:1, cols], (B, cw))
        li = jnp.broadcast_to(lam_ref[1:2, cols], (B, cw))

        def body(t, carry):
            hr, hi = carry
            at_t = pl.ds(pl.multiple_of(t * B, B), B)
            nr = lr * hr - li * hi + sr[at_t, cols]
            ni = lr * hi + li * hr + si[at_t, cols]
            sr[at_t, cols] = nr
            si[at_t, cols] = ni
            return nr, ni

        hr, hi = lax.fori_loop(0, L, body, (hr_s[:, cols], hi_s[:, cols]), unroll=2)
        hr_s[:, cols] = hr
        hi_s[:, cols] = hi

    ys = []
    for hf in range(2):
        cs = slice(hf * half_c, (hf + 1) * half_c)
        ys.append(_mm(sr[:, cs].astype(BF16), cre_ref[hf])
                  + _mm(si[:, cs].astype(BF16), cim_ref[hf]))
    y = jnp.concatenate(ys, axis=1) + d_ref[...] * u
    g = 0.5 * y * (1.0 + jnp.tanh(math.sqrt(2.0 / math.pi) * (y + 0.044715 * (y * y * y))))
    gb = g.astype(BF16)
    z = jnp.concatenate(
        [_mm(gb[:, hf * half_w:(hf + 1) * half_w], wglu_ref[hf]) for hf in range(2)],
        axis=1) + bglu_ref[...]
    out = g * (1.0 / (1.0 + jnp.exp(-z)))
    y_ref[...] = out.reshape(L, B, SSM_WIDTH)
    hr_out[...] = hr_s[...]
    hi_out[...] = hi_s[...]


def _ssm(u, h0r, h0i, sp, L):
    S, B, _ = u.shape
    c2 = lambda i: (0, 0)
    c3 = lambda i: (0, 0, 0)
    return pl.pallas_call(
        _ssm_kernel,
        grid=(S // L,),
        in_specs=[
            pl.BlockSpec((L, B, SSM_WIDTH), lambda i: (i, 0, 0)),
            pl.BlockSpec((B, SSM_COLS), c2),
            pl.BlockSpec((B, SSM_COLS), c2),
            pl.BlockSpec((2, SSM_COLS), c2),
            pl.BlockSpec((2, SSM_WIDTH // 2, SSM_COLS // 2), c3),
            pl.BlockSpec((2, SSM_WIDTH // 2, SSM_COLS // 2), c3),
            pl.BlockSpec((2, SSM_COLS // 2, SSM_WIDTH // 2), c3),
            pl.BlockSpec((2, SSM_COLS // 2, SSM_WIDTH // 2), c3),
            pl.BlockSpec((1, SSM_WIDTH), c2),
            pl.BlockSpec((2, SSM_WIDTH // 2, SSM_WIDTH // 2), c3),
            pl.BlockSpec((1, SSM_WIDTH), c2),
        ],
        out_specs=[
            pl.BlockSpec((L, B, SSM_WIDTH), lambda i: (i, 0, 0)),
            pl.BlockSpec((B, SSM_COLS), c2),
            pl.BlockSpec((B, SSM_COLS), c2),
        ],
        out_shape=[
            jax.ShapeDtypeStruct((S, B, SSM_WIDTH), F32),
            jax.ShapeDtypeStruct((B, SSM_COLS), F32),
            jax.ShapeDtypeStruct((B, SSM_COLS), F32),
        ],
        scratch_shapes=[
            pltpu.VMEM((L * B, SSM_COLS), F32),
            pltpu.VMEM((L * B, SSM_COLS), F32),
            pltpu.VMEM((B, SSM_COLS), F32),
            pltpu.VMEM((B, SSM_COLS), F32),
        ],
        compiler_params=pltpu.CompilerParams(
            dimension_semantics=("arbitrary",), vmem_limit_bytes=VMEM_LIMIT),
        name="ssm",
    )(u, h0r, h0i, sp["lam"], sp["bre"], sp["bim"], sp["cre"], sp["cim"],
      sp["d"], sp["wglu"], sp["bglu"])


def _block_diag(blocks):
    G, r, c = blocks.shape
    eye = jnp.eye(G, dtype=blocks.dtype)
    return jnp.einsum("grc,gh->grhc", blocks, eye).reshape(G * r, G * c)


def _ssm_params(lam_re, lam_im, log_dt, b_re, b_im, c_re, c_im, d, w_glu, b_glu):
    lam = lax.complex(lam_re.astype(F32), lam_im.astype(F32))
    dt = jnp.exp(log_dt.astype(F32))[:, None]
    lam_bar = jnp.exp(lam * dt)
    bmat = lax.complex(b_re.astype(F32), b_im.astype(F32))
    b_bar = ((lam_bar - 1.0) / lam)[..., None] * bmat
    lam2 = jnp.stack([lam_bar.real.reshape(-1), lam_bar.imag.reshape(-1)])
    bt = jnp.swapaxes(b_bar, 1, 2)
    hw, hc = SSM_WIDTH // 2, SSM_COLS // 2
    split_b = lambda m: jnp.stack([m[:hw, :hc], m[hw:, hc:]]).astype(BF16)
    split_c = lambda m: jnp.stack([m[:hc, :hw], m[hc:, hw:]]).astype(BF16)
    ct_re = jnp.swapaxes(c_re.astype(F32), 1, 2)
    ct_im = jnp.swapaxes(c_im.astype(F32), 1, 2)
    wg = _block_diag(w_glu.astype(F32))
    return {
        "lam": lam2,
        "bre": split_b(_block_diag(bt.real)),
        "bim": split_b(_block_diag(bt.imag)),
        "cre": split_c(_block_diag(ct_re)),
        "cim": split_c(_block_diag(-ct_im)),
        "d": d.astype(F32).reshape(1, SSM_WIDTH),
        "wglu": jnp.stack([wg[:hw, :hw], wg[hw:, hw:]]).astype(BF16),
        "bglu": b_glu.astype(F32).reshape(1, SSM_WIDTH),
    }


def _memkv_kernel(m_ref, g_ref, w_ref, gk_ref, k_ref, v_ref):
    m = _rms(m_ref[...], g_ref[...])
    kv = _mm(m.astype(BF16), w_ref[...])
    for h in range(CA_HEADS):
        sl = slice(h * CA_HEAD_DIM, (h + 1) * CA_HEAD_DIM)
        k_ref[:, sl] = _rms(kv[:, sl], gk_ref[...])
    v_ref[...] = kv[:, CA_WIDTH:]


def _memkv(mem2d, g, w_bf, gk, tm):
    T = mem2d.shape[0]
    full = lambda i: (0, 0)
    return pl.pallas_call(
        _memkv_kernel,
        grid=(T // tm,),
        in_specs=[
            pl.BlockSpec((tm, D_MODEL), lambda i: (i, 0)),
            pl.BlockSpec((1, D_MODEL), full),
            pl.BlockSpec((D_MODEL, 2 * CA_WIDTH), full),
            pl.BlockSpec((1, CA_HEAD_DIM), full),
        ],
        out_specs=[
            pl.BlockSpec((tm, CA_WIDTH), lambda i: (i, 0)),
            pl.BlockSpec((tm, CA_WIDTH), lambda i: (i, 0)),
        ],
        out_shape=[
            jax.ShapeDtypeStruct((T, CA_WIDTH), F32),
            jax.ShapeDtypeStruct((T, CA_WIDTH), F32),
        ],
        compiler_params=pltpu.CompilerParams(
            dimension_semantics=("arbitrary",), vmem_limit_bytes=VMEM_LIMIT),
        name="memkv",
    )(mem2d, g, w_bf, gk)


def _mid_kernel(x_ref, att_ref, ssm_ref, mk_ref, mv_ref,
                gao_ref, gso_ref, wout_ref, gx_ref, wcq_ref, gcq_ref, wco_ref,
                gffn_ref, wr_ref, br_ref,
                x2_ref, hn_ref, rt_ref, rtt_ref, cnt_ref, base_s):
    tm = x_ref.shape[1]

    @pl.when((pl.program_id(0) == 0) & (pl.program_id(1) == 0))
    def _():
        base_s[...] = jnp.zeros_like(base_s)

    a = _rms(att_ref[0], gao_ref[...]).astype(BF16)
    s = _rms(ssm_ref[...], gso_ref[...]).astype(BF16)
    x1 = (x_ref[0] + _mm(a, wout_ref[0:ATT_WIDTH, :])
          + _mm(s, wout_ref[ATT_WIDTH:, :]))

    qx = _mm(_rms(x1, gx_ref[...]).astype(BF16), wcq_ref[...])
    heads = []
    for h in range(CA_HEADS):
        sl = slice(h * CA_HEAD_DIM, (h + 1) * CA_HEAD_DIM)
        qh = _rms(qx[:, sl], gcq_ref[...]).astype(BF16)
        kh = mk_ref[0, :, sl].astype(BF16)
        vh = mv_ref[0, :, sl].astype(BF16)
        sc = lax.dot_general(qh, kh, (((1,), (1,)), ((), ())),
                             preferred_element_type=F32) * (CA_HEAD_DIM ** -0.5)
        p = jnp.exp(sc - jnp.max(sc, axis=-1, keepdims=True))
        p = p / jnp.sum(p, axis=-1, keepdims=True)
        heads.append(_mm(p.astype(BF16), vh))
    o = jnp.concatenate(heads, axis=1).astype(BF16)
    x2 = x1 + _mm(o, wco_ref[...])
    x2_ref[0] = x2

    hn = _rms(x2, gffn_ref[...])
    hn_ref[0] = hn

    h_hi = hn.astype(BF16)
    h_lo = (hn - h_hi.astype(F32)).astype(BF16)
    r1 = _mm(h_hi, wr_ref[...])
    lg = (r1[:, :LANES] + r1[:, LANES:] + _mm(h_lo, wr_ref[:, 0:LANES])
          + br_ref[...])

    col = lax.broadcasted_iota(jnp.int32, (tm, LANES), 1)
    big = jnp.int32(4 * LANES)
    gmask = col < N_EXPERT_GROUPS
    lgg = jnp.where(gmask, lg, NEG)
    mg = jnp.max(lgg, axis=-1, keepdims=True)
    grp = jnp.min(jnp.where(gmask & (lgg == mg), col, big), axis=-1, keepdims=True)
    pg_top = 1.0 / jnp.sum(jnp.where(gmask, jnp.exp(lgg - mg), 0.0), axis=-1, keepdims=True)

    ecol = col - ROUTER_COL0
    emask = ((ecol >= 0) & (ecol < N_EXPERTS)
             & (lax.shift_right_arithmetic(ecol, 3) == grp))
    le = jnp.where(emask, lg, NEG)
    m1 = jnp.max(le, axis=-1, keepdims=True)
    i1 = jnp.min(jnp.where(emask & (le == m1), col, big), axis=-1, keepdims=True)
    rest = emask & (col != i1)
    le2 = jnp.where(rest, lg, NEG)
    m2 = jnp.max(le2, axis=-1, keepdims=True)
    i2 = jnp.min(jnp.where(rest & (le2 == m2), col, big), axis=-1, keepdims=True)
    den = jnp.sum(jnp.where(emask, jnp.exp(le - m1), 0.0), axis=-1, keepdims=True)
    p1 = 1.0 / den
    p2 = jnp.exp(m2 - m1) / den
    gate1 = pg_top * p1 / (p1 + p2)
    gate2 = pg_top * p2 / (p1 + p2)

    sel1 = col == i1
    sel2 = col == i2
    oh = jnp.where(sel1 | sel2, 1.0, 0.0)
    r_i = lax.broadcasted_iota(jnp.int32, (tm, tm), 0)
    c_i = lax.broadcasted_iota(jnp.int32, (tm, tm), 1)
    tri = jnp.where(r_i > c_i, 1.0, 0.0).astype(BF16)
    tot = base_s[...] + _mm(tri, oh.astype(BF16))
    rank1 = jnp.sum(jnp.where(sel1, tot, 0.0), axis=-1, keepdims=True)
    rank2 = jnp.sum(jnp.where(sel2, tot, 0.0), axis=-1, keepdims=True)
    base_s[...] = base_s[...] + jnp.sum(oh, axis=0, keepdims=True)
    cnt_ref[...] = base_s[...]

    e1 = (i1 - ROUTER_COL0).astype(F32)
    e2 = (i2 - ROUTER_COL0).astype(F32)
    rt = jnp.zeros((tm, LANES), F32)
    for k, val in enumerate((e1, e2, gate1, gate2, rank1, rank2)):
        rt = jnp.where(col == k, val, rt)
    rt_ref[0] = rt
    rtt_ref[0] = rt.T[0:8, :]


def _mid(x, att, ssm_tm, mk, mv, wp, tm):
    B, S, _ = x.shape
    c2 = lambda b, i: (0, 0)
    tile = lambda w: pl.BlockSpec((1, tm, w), lambda b, i: (b, i, 0))
    return pl.pallas_call(
        _mid_kernel,
        grid=(B, S // tm),
        in_specs=[
            tile(D_MODEL), tile(ATT_WIDTH),
            pl.BlockSpec((tm, SSM_WIDTH), lambda b, i: (i, b)),
            pl.BlockSpec((1, N_MEM, CA_WIDTH), lambda b, i: (b, 0, 0)),
            pl.BlockSpec((1, N_MEM, CA_WIDTH), lambda b, i: (b, 0, 0)),
            pl.BlockSpec((1, ATT_WIDTH), c2),
            pl.BlockSpec((1, SSM_WIDTH), c2),
            pl.BlockSpec((ATT_WIDTH + SSM_WIDTH, D_MODEL), c2),
            pl.BlockSpec((1, D_MODEL), c2),
            pl.BlockSpec((D_MODEL, CA_WIDTH), c2),
            pl.BlockSpec((1, CA_HEAD_DIM), c2),
            pl.BlockSpec((CA_WIDTH, D_MODEL), c2),
            pl.BlockSpec((1, D_MODEL), c2),
            pl.BlockSpec((D_MODEL, 2 * LANES), c2),
            pl.BlockSpec((1, LANES), c2),
        ],
        out_specs=[
            tile(D_MODEL), tile(D_MODEL), tile(LANES),
            pl.BlockSpec((1, 8, tm), lambda b, i: (b, 0, i)),
            pl.BlockSpec((1, LANES), c2),
        ],
        out_shape=[
            jax.ShapeDtypeStruct((B, S, D_MODEL), F32),
            jax.ShapeDtypeStruct((B, S, D_MODEL), F32),
            jax.ShapeDtypeStruct((B, S, LANES), F32),
            jax.ShapeDtypeStruct((B, 8, S), F32),
            jax.ShapeDtypeStruct((1, LANES), F32),
        ],
        scratch_shapes=[pltpu.VMEM((1, LANES), F32)],
        compiler_params=pltpu.CompilerParams(
            dimension_semantics=("arbitrary", "arbitrary"),
            vmem_limit_bytes=VMEM_LIMIT),
        name="mid",
    )(x, att, ssm_tm, mk, mv, wp["gao"], wp["gso"], wp["wout"], wp["gx"], wp["wcq"],
      wp["gcq"], wp["wco"], wp["gffn"], wp["wr"], wp["br"])


def _dispatch_kernel(pend_ref, padded_ref, dest_ref, hn_ref, xs_hbm, stage, zbuf, sem):
    tm = hn_ref.shape[0]
    i = pl.program_id(0)
    slot = lax.rem(i, 2)
    blk = zbuf.shape[0]

    def wait_rows(s):
        for _ in range(2):
            pltpu.make_async_copy(stage.at[s], xs_hbm.at[pl.ds(0, tm)], sem.at[s]).wait()

    @pl.when(i == 0)
    def _():
        zbuf[...] = jnp.zeros_like(zbuf)
        for e in range(N_EXPERTS):
            @pl.when(padded_ref[e] > 0)
            def _():
                row0 = pl.multiple_of(pend_ref[e] - blk, blk)
                fill = pltpu.make_async_copy(zbuf, xs_hbm.at[pl.ds(row0, blk)], sem.at[2])
                fill.start()
                fill.wait()

        def fill_tail(b, carry):
            fill = pltpu.make_async_copy(
                zbuf, xs_hbm.at[pl.ds(pl.multiple_of(b * blk, blk), blk)], sem.at[2])
            fill.start()
            fill.wait()
            return carry

        lax.fori_loop(pend_ref[N_EXPERTS - 1] // blk, xs_hbm.shape[0] // blk, fill_tail, 0)

    @pl.when(i >= 2)
    def _():
        wait_rows(slot)

    stage[slot] = hn_ref[...]
    for k in range(2):
        for r in range(tm):
            pltpu.make_async_copy(stage.at[slot, pl.ds(r, 1), :],
                                  xs_hbm.at[pl.ds(dest_ref[0, 0, k * tm + r], 1), :],
                                  sem.at[slot]).start(priority=r % 2)

    @pl.when(i == pl.num_programs(0) - 1)
    def _():
        wait_rows(slot)

        @pl.when(i >= 1)
        def _():
            wait_rows(1 - slot)


def _dispatch(pad_end, padded, dest_t, hn, rows, tm, blk):
    nt = hn.shape[0] // tm
    grid_spec = pltpu.PrefetchScalarGridSpec(
        num_scalar_prefetch=2,
        grid=(nt,),
        in_specs=[
            pl.BlockSpec((1, 1, 2 * tm), lambda i, pe, pd: (i, 0, 0), memory_space=pltpu.SMEM),
            pl.BlockSpec((tm, D_MODEL), lambda i, pe, pd: (i, 0)),
        ],
        out_specs=pl.BlockSpec(memory_space=pl.ANY),
        scratch_shapes=[
            pltpu.VMEM((2, tm, D_MODEL), F32),
            pltpu.VMEM((blk, D_MODEL), F32),
            pltpu.SemaphoreType.DMA((3,)),
        ],
    )
    return pl.pallas_call(
        _dispatch_kernel,
        grid_spec=grid_spec,
        out_shape=jax.ShapeDtypeStruct((rows, D_MODEL), F32),
        compiler_params=pltpu.CompilerParams(
            dimension_semantics=("arbitrary",), vmem_limit_bytes=VMEM_LIMIT),
        name="dispatch",
    )(pad_end, padded, dest_t, hn)


def _moe_kernel(be_ref, nu_ref, xs_ref, wg_ref, wu_ref, wd_ref, yb_ref, wg_s, wu_s, wd_s):
    i = pl.program_id(0)

    @pl.when(i < nu_ref[0])
    def _():
        @pl.when((i == 0) | (be_ref[i] != be_ref[jnp.maximum(i - 1, 0)]))
        def _():
            wg_s[...] = wg_ref[0].astype(BF16)
            wu_s[...] = wu_ref[0].astype(BF16)
            wd_s[...] = wd_ref[0].astype(BF16)

        xe = xs_ref[...].astype(BF16)
        g = _mm(xe, wg_s[...])
        u = _mm(xe, wu_s[...])
        hmid = ((g * (1.0 / (1.0 + jnp.exp(-g)))) * u).astype(BF16)
        yb_ref[...] = _mm(hmid, wd_s[...])

    @pl.when(i >= nu_ref[0])
    def _():
        yb_ref[...] = jnp.zeros_like(yb_ref)


def _moe(block_e, n_used, xs, w_gate, w_up, w_down, blk):
    n_blocks = block_e.shape[0]
    in_blk = lambda i, be, nu: (jnp.maximum(jnp.minimum(i, nu[0] - 1), 0), 0)
    grid_spec = pltpu.PrefetchScalarGridSpec(
        num_scalar_prefetch=2,
        grid=(n_blocks,),
        in_specs=[
            pl.BlockSpec((blk, D_MODEL), in_blk),
            pl.BlockSpec((1, D_MODEL, D_EXPERT), lambda i, be, nu: (be[i], 0, 0)),
            pl.BlockSpec((1, D_MODEL, D_EXPERT), lambda i, be, nu: (be[i], 0, 0)),
            pl.BlockSpec((1, D_EXPERT, D_MODEL), lambda i, be, nu: (be[i], 0, 0)),
        ],
        out_specs=pl.BlockSpec((blk, D_MODEL), lambda i, be, nu: (i, 0)),
        scratch_shapes=[
            pltpu.VMEM((D_MODEL, D_EXPERT), BF16),
            pltpu.VMEM((D_MODEL, D_EXPERT), BF16),
            pltpu.VMEM((D_EXPERT, D_MODEL), BF16),
        ],
    )
    return pl.pallas_call(
        _moe_kernel,
        grid_spec=grid_spec,
        out_shape=jax.ShapeDtypeStruct(xs.shape, F32),
        compiler_params=pltpu.CompilerParams(
            dimension_semantics=("arbitrary",), vmem_limit_bytes=VMEM_LIMIT),
        name="moe",
    )(block_e, n_used, xs, w_gate, w_up, w_down)


def _combine_kernel(dest_ref, dest_next_ref, x2_ref, rt_ref, yb_hbm, o_ref, buf, sem):
    tm = x2_ref.shape[0]
    i = pl.program_id(0)
    slot = lax.rem(i, 2)

    def gather(d_ref, s):
        for k in range(2):
            for r in range(tm):
                pltpu.make_async_copy(yb_hbm.at[pl.ds(d_ref[0, 0, k * tm + r], 1), :],
                                      buf.at[s, k, pl.ds(r, 1), :],
                                      sem.at[s]).start(priority=r % 2)

    @pl.when(i == 0)
    def _():
        gather(dest_ref, 0)

    @pl.when(i + 1 < pl.num_programs(0))
    def _():
        gather(dest_next_ref, 1 - slot)

    for k in range(2):
        pltpu.make_async_copy(yb_hbm.at[pl.ds(0, tm), :], buf.at[slot, k], sem.at[slot]).wait()
    rt = rt_ref[...]
    o_ref[...] = x2_ref[...] + rt[:, 2:3] * buf[slot, 0] + rt[:, 3:4] * buf[slot, 1]


def _combine(dest_t, x2, rt, yb, tm):
    T = x2.shape[0]
    nt = T // tm
    return pl.pallas_call(
        _combine_kernel,
        grid=(nt,),
        in_specs=[
            pl.BlockSpec((1, 1, 2 * tm), lambda i: (i, 0, 0), memory_space=pltpu.SMEM),
            pl.BlockSpec((1, 1, 2 * tm), lambda i: (jnp.minimum(i + 1, nt - 1), 0, 0),
                         memory_space=pltpu.SMEM),
            pl.BlockSpec((tm, D_MODEL), lambda i: (i, 0)),
            pl.BlockSpec((tm, LANES), lambda i: (i, 0)),
            pl.BlockSpec(memory_space=pl.ANY),
        ],
        out_specs=pl.BlockSpec((tm, D_MODEL), lambda i: (i, 0)),
        out_shape=jax.ShapeDtypeStruct((T, D_MODEL), F32),
        scratch_shapes=[
            pltpu.VMEM((2, 2, tm, D_MODEL), F32),
            pltpu.SemaphoreType.DMA((2,)),
        ],
        compiler_params=pltpu.CompilerParams(
            dimension_semantics=("arbitrary",), vmem_limit_bytes=VMEM_LIMIT),
        name="combine",
    )(dest_t, dest_t, x2, rt, yb)


def _hier_moe(x2, hn, rt, rtt, cnt, w_gate, w_up, w_down, tm, blk):
    T = x2.shape[0]
    counts = cnt[0, ROUTER_COL0:ROUTER_COL0 + N_EXPERTS].astype(jnp.int32)
    padded = (counts + blk - 1) // blk * blk
    pad_end = jnp.cumsum(padded)
    pad_start = pad_end - padded
    flat = lambda a: jnp.swapaxes(a, 0, 1).reshape(a.shape[1], T)
    eid = flat(rtt[:, 0:2, :]).astype(jnp.int32)
    rank = flat(rtt[:, 4:6, :]).astype(jnp.int32)
    experts = jnp.arange(N_EXPERTS, dtype=jnp.int32)[:, None, None]
    dest = rank + jnp.sum(jnp.where(eid[None] == experts, pad_start[:, None, None], 0), axis=0)
    n_blocks = (2 * T + N_EXPERTS * (blk - 1)) // blk + 1
    rows = n_blocks * blk
    blk_row0 = jnp.arange(n_blocks, dtype=jnp.int32) * blk
    block_e = jnp.minimum(
        jnp.sum((pad_end[None, :] <= blk_row0[:, None]).astype(jnp.int32), axis=1),
        N_EXPERTS - 1)
    n_used = (pad_end[-1] // blk).astype(jnp.int32).reshape(1)
    nt = T // tm
    dest_t = dest.reshape(2, nt, tm).transpose(1, 0, 2).reshape(nt, 1, 2 * tm)
    xs = _dispatch(pad_end, padded, dest_t, hn, rows, tm, blk)
    yb = _moe(block_e, n_used, xs, w_gate, w_up, w_down, blk)
    return _combine(dest_t, x2, rt, yb, tm)


def _rope_table(pos):
    half = ROPE_DIM // 2
    inv = ROPE_THETA ** (-jnp.arange(0, ROPE_DIM, 2, dtype=F32) / ROPE_DIM)
    ang = pos.astype(F32)[:, None] * inv[None, :]
    cos, sin = jnp.cos(ang), jnp.sin(ang)
    L = pos.shape[0]
    pad = jnp.zeros((L, HEAD_DIM - ROPE_DIM), F32)
    zero = jnp.zeros((L, half), F32)
    c64 = jnp.concatenate([cos, cos, pad + 1.0], axis=1)
    lo64 = jnp.concatenate([-sin, zero, pad], axis=1)
    hi64 = jnp.concatenate([zero, sin, pad], axis=1)
    two = lambda t: jnp.concatenate([t, t], axis=1)
    return jnp.concatenate([two(c64), two(lo64), two(hi64)], axis=1)


def _layer(x, pos_rope, kctx_prev, vctx_prev, h0r, h0i, mk, mv, wp, sp, ew, *,
           tm_in, tq, ssm_l, tm_mid, tm_comb, moe_blk, mask_context):
    B, S, _ = x.shape
    T = B * S
    q, k3, v3, u_tm = _in_proj(x, wp["gmix"], wp["win"], wp["gq"], wp["gk"], pos_rope, tm_in)
    kctx = jnp.concatenate([kctx_prev, k3], axis=1)
    vctx = jnp.concatenate([vctx_prev, v3], axis=1)
    att = _swa(wp["sink"], q, kctx, vctx, tq, mask_context)
    ssm_tm, hr, hi = _ssm(u_tm.reshape(S, B, SSM_WIDTH), h0r, h0i, sp, ssm_l)
    x2, hn, rt, rtt, cnt = _mid(x, att, ssm_tm.reshape(S, B * SSM_WIDTH), mk, mv, wp, tm_mid)
    y = _hier_moe(x2.reshape(T, D_MODEL), hn.reshape(T, D_MODEL),
                  rt.reshape(T, LANES), rtt, cnt, *ew, tm_comb, moe_blk)
    return y.reshape(B, S, D_MODEL), k3, v3, hr, hi


def kernel(x_prompt, x_sample, cache_attn_k, cache_attn_v, state_ssm_re, state_ssm_im, cache_mem_k, cache_mem_v, mem_prompt, norm_mix, w_in, q_norm, k_norm, attn_sink, ssm_lambda_re, ssm_lambda_im, ssm_log_dt, ssm_b_re, ssm_b_im, ssm_c_re, ssm_c_im, ssm_d, ssm_w_glu, ssm_b_glu, norm_attn_out, norm_ssm_out, w_out, norm_cross, norm_mem, w_cq, w_ck, w_cv, cq_norm, ck_norm, w_co, norm_ffn, w_router_group, b_router_group, w_router_expert, b_router_expert, w_e_gate, w_e_up, w_e_down):
    depth = norm_mix.shape[0]
    Bp, Lp, _ = x_prompt.shape
    Bs, Ls, _ = x_sample.shape
    yp, ys = x_prompt, x_sample
    rope_p = _rope_table(jnp.arange(Lp, dtype=jnp.int32))
    rope_s = _rope_table(PAST_LEN + jnp.arange(Ls, dtype=jnp.int32))
    outs = [[] for _ in range(10)]
    n_router = N_EXPERT_GROUPS + N_EXPERTS
    for l in range(depth):
        row = lambda a: a[l].astype(F32).reshape(1, -1)
        w_r = jnp.pad(jnp.concatenate([w_router_group[l], w_router_expert[l]], axis=1).astype(F32),
                      ((0, 0), (0, LANES - n_router)))
        w_r_hi = w_r.astype(BF16)
        w_r_lo = (w_r - w_r_hi.astype(F32)).astype(BF16)
        b_r = jnp.pad(jnp.concatenate([b_router_group[l], b_router_expert[l]]).astype(F32),
                      (0, LANES - n_router)).reshape(1, LANES)
        wp = {
            "gmix": row(norm_mix), "win": w_in[l].astype(BF16),
            "gq": jnp.tile(row(q_norm), (1, LANES // HEAD_DIM)),
            "gk": jnp.tile(row(k_norm), (1, LANES // HEAD_DIM)),
            "sink": attn_sink[l].astype(F32),
            "gao": row(norm_attn_out), "gso": row(norm_ssm_out),
            "wout": w_out[l].astype(BF16), "gx": row(norm_cross),
            "wcq": w_cq[l].astype(BF16), "gcq": row(cq_norm),
            "wco": w_co[l].astype(BF16), "gffn": row(norm_ffn),
            "wr": jnp.concatenate([w_r_hi, w_r_lo], axis=1), "br": b_r,
        }
        sp = _ssm_params(ssm_lambda_re[l], ssm_lambda_im[l], ssm_log_dt[l], ssm_b_re[l],
                         ssm_b_im[l], ssm_c_re[l], ssm_c_im[l], ssm_d[l], ssm_w_glu[l],
                         ssm_b_glu[l])
        ew = (w_e_gate[l].astype(F32), w_e_up[l].astype(F32), w_e_down[l].astype(F32))

        w_ckv = jnp.concatenate([w_ck[l], w_cv[l]], axis=1).astype(BF16)
        mkp, mvp = _memkv(mem_prompt.reshape(Bp * N_MEM, D_MODEL), row(norm_mem), w_ckv,
                          row(ck_norm), 512)
        mkp = mkp.reshape(Bp, N_MEM, CA_WIDTH)
        mvp = mvp.reshape(Bp, N_MEM, CA_WIDTH)

        zctx = jnp.zeros((Bp, WINDOW, KV_WIDTH), F32)
        zst = jnp.zeros((Bp, SSM_COLS), F32)
        yp, kp, vp, hpr, hpi = _layer(
            yp, rope_p, zctx, zctx, zst, zst, mkp, mvp, wp, sp, ew,
            tm_in=512, tq=256, ssm_l=64, tm_mid=512, tm_comb=256, moe_blk=2 * MOE_BLOCK,
            mask_context=True)
        ys, kn, vn, hsr, hsi = _layer(
            ys, rope_s, cache_attn_k[l].reshape(Bs, WINDOW, KV_WIDTH).astype(F32),
            cache_attn_v[l].reshape(Bs, WINDOW, KV_WIDTH).astype(F32),
            state_ssm_re[l].astype(F32).reshape(Bs, SSM_COLS),
            state_ssm_im[l].astype(F32).reshape(Bs, SSM_COLS),
            cache_mem_k[l].astype(F32).reshape(Bs, N_MEM, CA_WIDTH),
            cache_mem_v[l].astype(F32).reshape(Bs, N_MEM, CA_WIDTH), wp, sp, ew,
            tm_in=Ls, tq=CHUNK, ssm_l=Ls, tm_mid=Ls, tm_comb=256, moe_blk=MOE_BLOCK,
            mask_context=False)

        sg = (N_SSM_GROUPS, SSM_STATE)
        kvs = (N_KV_HEADS, HEAD_DIM)
        vals = (kp[:, Lp - WINDOW:].reshape(Bp, WINDOW, *kvs),
                vp[:, Lp - WINDOW:].reshape(Bp, WINDOW, *kvs),
                hpr.reshape(Bp, *sg), hpi.reshape(Bp, *sg),
                mkp.reshape(Bp, N_MEM, CA_HEADS, CA_HEAD_DIM),
                mvp.reshape(Bp, N_MEM, CA_HEADS, CA_HEAD_DIM),
                kn.reshape(Bs, Ls, *kvs), vn.reshape(Bs, Ls, *kvs),
                hsr.reshape(Bs, *sg), hsi.reshape(Bs, *sg))
        for lst, val in zip(outs, vals):
            lst.append(val)
    return (yp, ys) + tuple(jnp.stack(lst) for lst in outs)
```

```python
import functools
import math

import jax
import jax.numpy as jnp
from jax import lax
from jax.experimental import pallas as pl
from jax.experimental.pallas import tpu as pltpu

F32 = jnp.float32
BF16 = jnp.bfloat16

D_MODEL = 1024
CHUNK = 64
N_Q_HEADS = 8
N_KV_HEADS = 2
GQA = N_Q_HEADS // N_KV_HEADS
HEAD_DIM = 64
WINDOW = 128
BAND = WINDOW + CHUNK
ROPE_DIM = HEAD_DIM // 4
ROPE_THETA = 500000.0
ATT_WIDTH = N_Q_HEADS * HEAD_DIM
KV_WIDTH = N_KV_HEADS * HEAD_DIM
SSM_GROUP = 16
SSM_WIDTH = D_MODEL // 2
N_SSM_GROUPS = SSM_WIDTH // SSM_GROUP
SSM_STATE = 64
SSM_COLS = N_SSM_GROUPS * SSM_STATE
IN_WIDTH = ATT_WIDTH + 2 * KV_WIDTH + SSM_WIDTH
N_MEM = 256
CA_HEADS = 4
CA_HEAD_DIM = 128
CA_WIDTH = CA_HEADS * CA_HEAD_DIM
N_EXPERT_GROUPS = 4
EXPERTS_PER_GROUP = 8
N_EXPERTS = N_EXPERT_GROUPS * EXPERTS_PER_GROUP
D_EXPERT = 512
MOE_BLOCK = 256
EPS = 1e-6
NEG = -1e30
PAST_LEN = 4096

LANES = 128
ROUTER_COL0 = N_EXPERT_GROUPS
VMEM_LIMIT = 48 * 1024 * 1024


def _rms(x, g):
    ms = jnp.mean(x * x, axis=-1, keepdims=True)
    return (x * lax.rsqrt(ms + EPS)) * g


def _mm(a, b):
    return jnp.dot(a, b, preferred_element_type=F32)


def _in_proj_kernel(x_ref, g_ref, w_ref, gq_ref, gk_ref, rope_ref,
                    q_ref, k_ref, v_ref, u_ref):
    tm = x_ref.shape[1]
    h = _rms(x_ref[0], g_ref[...])
    hin = _mm(h.astype(BF16), w_ref[...])
    rope = rope_ref[...]
    cos = rope[:, 0:LANES]
    sin_lo = rope[:, LANES:2 * LANES]
    sin_hi = rope[:, 2 * LANES:3 * LANES]
    lane = lax.broadcasted_iota(jnp.int32, (tm, LANES), 1)
    left = lane < HEAD_DIM

    def norm_rope(z, g):
        sq = z * z
        lsum = jnp.sum(jnp.where(left, sq, 0.0), axis=-1, keepdims=True)
        rsum = jnp.sum(jnp.where(left, 0.0, sq), axis=-1, keepdims=True)
        ms = jnp.where(left, lsum, rsum) * (1.0 / HEAD_DIM)
        zn = (z * lax.rsqrt(ms + EPS)) * g
        half = ROPE_DIM // 2
        return (zn * cos + pltpu.roll(zn, LANES - half, 1) * sin_lo
                + pltpu.roll(zn, half, 1) * sin_hi)

    for j in range(ATT_WIDTH // LANES):
        sl = slice(j * LANES, (j + 1) * LANES)
        q_ref[0, :, sl] = norm_rope(hin[:, sl], gq_ref[...])
    k_ref[0] = norm_rope(hin[:, ATT_WIDTH:ATT_WIDTH + KV_WIDTH], gk_ref[...])
    v_ref[0] = hin[:, ATT_WIDTH + KV_WIDTH:ATT_WIDTH + 2 * KV_WIDTH]
    u_ref[...] = hin[:, ATT_WIDTH + 2 * KV_WIDTH:]


def _in_proj(x, g, w_bf, gq, gk, rope, tm):
    B, S, _ = x.shape
    full = lambda b, i: (0, 0)
    tile = lambda w: pl.BlockSpec((1, tm, w), lambda b, i: (b, i, 0))
    return pl.pallas_call(
        _in_proj_kernel,
        grid=(B, S // tm),
        in_specs=[
            tile(D_MODEL),
            pl.BlockSpec((1, D_MODEL), full),
            pl.BlockSpec((D_MODEL, IN_WIDTH), full),
            pl.BlockSpec((1, LANES), full),
            pl.BlockSpec((1, LANES), full),
            pl.BlockSpec((tm, 3 * LANES), lambda b, i: (i, 0)),
        ],
        out_specs=[
            tile(ATT_WIDTH), tile(KV_WIDTH), tile(KV_WIDTH),
            pl.BlockSpec((tm, SSM_WIDTH), lambda b, i: (i, b)),
        ],
        out_shape=[
            jax.ShapeDtypeStruct((B, S, ATT_WIDTH), F32),
            jax.ShapeDtypeStruct((B, S, KV_WIDTH), F32),
            jax.ShapeDtypeStruct((B, S, KV_WIDTH), F32),
            jax.ShapeDtypeStruct((S, B * SSM_WIDTH), F32),
        ],
        compiler_params=pltpu.CompilerParams(
            dimension_semantics=("arbitrary", "arbitrary"),
            vmem_limit_bytes=VMEM_LIMIT),
        name="in_proj",
    )(x, g, w_bf, gq, gk, rope)


def _swa_kernel(sink_ref, q_ref, k_ref, v_ref, o_ref, *, mask_context):
    tq = q_ref.shape[1]
    i = pl.program_id(1)
    nch = tq // CHUNK
    lane = lax.broadcasted_iota(jnp.int32, (BAND, LANES), 1)
    lo_half = lane < HEAD_DIM
    slabs_per_kv = GQA * HEAD_DIM // LANES

    units = []
    scores = []
    vpads = {}
    valids = {}
    for c in range(nch):
        start = pl.multiple_of((i * nch + c) * CHUNK, CHUNK)
        kb = k_ref[0, pl.ds(start, BAND), :]
        vb = v_ref[0, pl.ds(start, BAND), :]
        kb_sw = pltpu.roll(kb, HEAD_DIM, 1)
        vb_sw = pltpu.roll(vb, HEAD_DIM, 1)
        if mask_context:
            kidx = start + lax.broadcasted_iota(jnp.int32, (1, BAND), 1)
            valids[c] = kidx >= WINDOW
        for kvh in range(N_KV_HEADS):
            k_own, k_oth = (kb, kb_sw) if kvh == 0 else (kb_sw, kb)
            v_own, v_oth = (vb, vb_sw) if kvh == 0 else (vb_sw, vb)
            kpad = (jnp.where(lo_half, k_own, 0.0).astype(BF16),
                    jnp.where(lo_half, 0.0, k_oth).astype(BF16))
            vpads[(c, kvh)] = (jnp.where(lo_half, v_own, 0.0).astype(BF16),
                               jnp.where(lo_half, 0.0, v_oth).astype(BF16))
            col0 = kvh * GQA * HEAD_DIM
            q2 = jnp.concatenate(
                [q_ref[0, c * CHUNK:(c + 1) * CHUNK, col0 + m * LANES:col0 + (m + 1) * LANES]
                 for m in range(slabs_per_kv)], axis=0).astype(BF16)
            for side in range(2):
                s = lax.dot_general(q2, kpad[side], (((1,), (1,)), ((), ())),
                                    preferred_element_type=F32) * (HEAD_DIM ** -0.5)
                if mask_context:
                    s = jnp.where(valids[c], s, NEG)
                units.append((c, kvh, side))
                scores.append(s)

    s_all = jnp.concatenate(scores, axis=0)
    sk = jnp.concatenate(
        [jnp.full((CHUNK, 1), sink_ref[kvh * GQA + 2 * m + side], F32)
         for (_, kvh, side) in units for m in range(slabs_per_kv)], axis=0)
    mx = jnp.maximum(jnp.max(s_all, axis=-1, keepdims=True), sk)
    p_all = jnp.exp(s_all - mx)
    den = jnp.sum(p_all, axis=-1, keepdims=True) + jnp.exp(sk - mx)
    p_all = (p_all / den).astype(BF16)
    rows_u = slabs_per_kv * CHUNK
    probs = [p_all[n * rows_u:(n + 1) * rows_u] for n in range(len(units))]

    for n in range(0, len(units), 2):
        c, kvh, _ = units[n]
        vp = vpads[(c, kvh)]
        o = _mm(probs[n], vp[0]) + _mm(probs[n + 1], vp[1])
        col0 = kvh * GQA * HEAD_DIM
        for m in range(slabs_per_kv):
            o_ref[0, c * CHUNK:(c + 1) * CHUNK, col0 + m * LANES:col0 + (m + 1) * LANES] = (
                o[m * CHUNK:(m + 1) * CHUNK])


def _swa(sink, q, kctx, vctx, tq, mask_context):
    B, Sq, _ = q.shape
    Sk = kctx.shape[1]
    return pl.pallas_call(
        functools.partial(_swa_kernel, mask_context=mask_context),
        grid=(B, Sq // tq),
        in_specs=[
            pl.BlockSpec(memory_space=pltpu.SMEM),
            pl.BlockSpec((1, tq, ATT_WIDTH), lambda b, i: (b, i, 0)),
            pl.BlockSpec((1, Sk, KV_WIDTH), lambda b, i: (b, 0, 0)),
            pl.BlockSpec((1, Sk, KV_WIDTH), lambda b, i: (b, 0, 0)),
        ],
        out_specs=pl.BlockSpec((1, tq, ATT_WIDTH), lambda b, i: (b, i, 0)),
        out_shape=jax.ShapeDtypeStruct((B, Sq, ATT_WIDTH), F32),
        compiler_params=pltpu.CompilerParams(
            dimension_semantics=("arbitrary", "arbitrary"),
            vmem_limit_bytes=VMEM_LIMIT),
        name="swa",
    )(sink, q, kctx, vctx)


def _ssm_kernel(u_ref, h0r_ref, h0i_ref, lam_ref, bre_ref, bim_ref, cre_ref, cim_ref,
                d_ref, wglu_ref, bglu_ref,
                y_ref, hr_out, hi_out, sr, si, hr_s, hi_s):
    L, B, _ = u_ref.shape
    rows = L * B
    half_w = SSM_WIDTH // 2
    half_c = SSM_COLS // 2

    @pl.when(pl.program_id(0) == 0)
    def _():
        hr_s[...] = h0r_ref[...]
        hi_s[...] = h0i_ref[...]

    u = u_ref[...].reshape(rows, SSM_WIDTH)
    ub = u.astype(BF16)
    for hf in range(2):
        uh = ub[:, hf * half_w:(hf + 1) * half_w]
        sr[:, hf * half_c:(hf + 1) * half_c] = _mm(uh, bre_ref[hf])
        si[:, hf * half_c:(hf + 1) * half_c] = _mm(uh, bim_ref[hf])

    cw = 4 * LANES
    for cc in range(SSM_COLS // cw):
        cols = slice(cc * cw, (cc + 1) * cw)
        lr = jnp.broadcast_to(lam_ref[0:1, cols], (B, cw))
        li = jnp.broadcast_to(lam_ref[1:2, cols], (B, cw))

        def body(t, carry):
            hr, hi = carry
            at_t = pl.ds(pl.multiple_of(t * B, B), B)
            nr = lr * hr - li * hi + sr[at_t, cols]
            ni = lr * hi + li * hr + si[at_t, cols]
            sr[at_t, cols] = nr
            si[at_t, cols] = ni
            return nr, ni

        hr, hi = lax.fori_loop(0, L, body, (hr_s[:, cols], hi_s[:, cols]), unroll=2)
        hr_s[:, cols] = hr
        hi_s[:, cols] = hi

    ys = []
    for hf in range(2):
        cs = slice(hf * half_c, (hf + 1) * half_c)
        ys.append(_mm(sr[:, cs].astype(BF16), cre_ref[hf])
                  + _mm(si[:, cs].astype(BF16), cim_ref[hf]))
    y = jnp.concatenate(ys, axis=1) + d_ref[...] * u
    g = 0.5 * y * (1.0 + jnp.tanh(math.sqrt(2.0 / math.pi) * (y + 0.044715 * (y * y * y))))
    gb = g.astype(BF16)
    z = jnp.concatenate(
        [_mm(gb[:, hf * half_w:(hf + 1) * half_w], wglu_ref[hf]) for hf in range(2)],
        axis=1) + bglu_ref[...]
    out = g * (1.0 / (1.0 + jnp.exp(-z)))
    y_ref[...] = out.reshape(L, B, SSM_WIDTH)
    hr_out[...] = hr_s[...]
    hi_out[...] = hi_s[...]


def _ssm(u, h0r, h0i, sp, L):
    S, B, _ = u.shape
    c2 = lambda i: (0, 0)
    c3 = lambda i: (0, 0, 0)
    return pl.pallas_call(
        _ssm_kernel,
        grid=(S // L,),
        in_specs=[
            pl.BlockSpec((L, B, SSM_WIDTH), lambda i: (i, 0, 0)),
            pl.BlockSpec((B, SSM_COLS), c2),
            pl.BlockSpec((B, SSM_COLS), c2),
            pl.BlockSpec((2, SSM_COLS), c2),
            pl.BlockSpec((2, SSM_WIDTH // 2, SSM_COLS // 2), c3),
            pl.BlockSpec((2, SSM_WIDTH // 2, SSM_COLS // 2), c3),
            pl.BlockSpec((2, SSM_COLS // 2, SSM_WIDTH // 2), c3),
            pl.BlockSpec((2, SSM_COLS // 2, SSM_WIDTH // 2), c3),
            pl.BlockSpec((1, SSM_WIDTH), c2),
            pl.BlockSpec((2, SSM_WIDTH // 2, SSM_WIDTH // 2), c3),
            pl.BlockSpec((1, SSM_WIDTH), c2),
        ],
        out_specs=[
            pl.BlockSpec((L, B, SSM_WIDTH), lambda i: (i, 0, 0)),
            pl.BlockSpec((B, SSM_COLS), c2),
            pl.BlockSpec((B, SSM_COLS), c2),
        ],
        out_shape=[
            jax.ShapeDtypeStruct((S, B, SSM_WIDTH), F32),
            jax.ShapeDtypeStruct((B, SSM_COLS), F32),
            jax.ShapeDtypeStruct((B, SSM_COLS), F32),
        ],
        scratch_shapes=[
            pltpu.VMEM((L * B, SSM_COLS), F32),
            pltpu.VMEM((L * B, SSM_COLS), F32),
            pltpu.VMEM((B, SSM_COLS), F32),
            pltpu.VMEM((B, SSM_COLS), F32),
        ],
        compiler_params=pltpu.CompilerParams(
            dimension_semantics=("arbitrary",), vmem_limit_bytes=VMEM_LIMIT),
        name="ssm",
    )(u, h0r, h0i, sp["lam"], sp["bre"], sp["bim"], sp["cre"], sp["cim"],
      sp["d"], sp["wglu"], sp["bglu"])


def _block_diag(blocks):
    G, r, c = blocks.shape
    eye = jnp.eye(G, dtype=blocks.dtype)
    return jnp.einsum("grc,gh->grhc", blocks, eye).reshape(G * r, G * c)


def _ssm_params(lam_re, lam_im, log_dt, b_re, b_im, c_re, c_im, d, w_glu, b_glu):
    lam = lax.complex(lam_re.astype(F32), lam_im.astype(F32))
    dt = jnp.exp(log_dt.astype(F32))[:, None]
    lam_bar = jnp.exp(lam * dt)
    bmat = lax.complex(b_re.astype(F32), b_im.astype(F32))
    b_bar = ((lam_bar - 1.0) / lam)[..., None] * bmat
    lam2 = jnp.stack([lam_bar.real.reshape(-1), lam_bar.imag.reshape(-1)])
    bt = jnp.swapaxes(b_bar, 1, 2)
    hw, hc = SSM_WIDTH // 2, SSM_COLS // 2
    split_b = lambda m: jnp.stack([m[:hw, :hc], m[hw:, hc:]]).astype(BF16)
    split_c = lambda m: jnp.stack([m[:hc, :hw], m[hc:, hw:]]).astype(BF16)
    ct_re = jnp.swapaxes(c_re.astype(F32), 1, 2)
    ct_im = jnp.swapaxes(c_im.astype(F32), 1, 2)
    wg = _block_diag(w_glu.astype(F32))
    return {
        "lam": lam2,
        "bre": split_b(_block_diag(bt.real)),
        "bim": split_b(_block_diag(bt.imag)),
        "cre": split_c(_block_diag(ct_re)),
        "cim": split_c(_block_diag(-ct_im)),
        "d": d.astype(F32).reshape(1, SSM_WIDTH),
        "wglu": jnp.stack([wg[:hw, :hw], wg[hw:, hw:]]).astype(BF16),
        "bglu": b_glu.astype(F32).reshape(1, SSM_WIDTH),
    }


def _memkv_kernel(m_ref, g_ref, w_ref, gk_ref, k_ref, v_ref):
    m = _rms(m_ref[...], g_ref[...])
    kv = _mm(m.astype(BF16), w_ref[...])
    for h in range(CA_HEADS):
        sl = slice(h * CA_HEAD_DIM, (h + 1) * CA_HEAD_DIM)
        k_ref[:, sl] = _rms(kv[:, sl], gk_ref[...])
    v_ref[...] = kv[:, CA_WIDTH:]


def _memkv(mem2d, g, w_bf, gk, tm):
    T = mem2d.shape[0]
    full = lambda i: (0, 0)
    return pl.pallas_call(
        _memkv_kernel,
        grid=(T // tm,),
        in_specs=[
            pl.BlockSpec((tm, D_MODEL), lambda i: (i, 0)),
            pl.BlockSpec((1, D_MODEL), full),
            pl.BlockSpec((D_MODEL, 2 * CA_WIDTH), full),
            pl.BlockSpec((1, CA_HEAD_DIM), full),
        ],
        out_specs=[
            pl.BlockSpec((tm, CA_WIDTH), lambda i: (i, 0)),
            pl.BlockSpec((tm, CA_WIDTH), lambda i: (i, 0)),
        ],
        out_shape=[
            jax.ShapeDtypeStruct((T, CA_WIDTH), F32),
            jax.ShapeDtypeStruct((T, CA_WIDTH), F32),
        ],
        compiler_params=pltpu.CompilerParams(
            dimension_semantics=("arbitrary",), vmem_limit_bytes=VMEM_LIMIT),
        name="memkv",
    )(mem2d, g, w_bf, gk)


def _mid_kernel(x_ref, att_ref, ssm_ref, mk_ref, mv_ref,
                gao_ref, gso_ref, wout_ref, gx_ref, wcq_ref, gcq_ref, wco_ref,
                gffn_ref, wr_ref, br_ref, cnt0_ref,
                x2_ref, hn_ref, rt_ref, rtt_ref, cnt_ref, base_s):
    tm = x_ref.shape[1]

    @pl.when((pl.program_id(0) == 0) & (pl.program_id(1) == 0))
    def _():
        base_s[...] = cnt0_ref[...]

    a = _rms(att_ref[0], gao_ref[...]).astype(BF16)
    s = _rms(ssm_ref[...], gso_ref[...]).astype(BF16)
    x1 = (x_ref[0] + _mm(a, wout_ref[0:ATT_WIDTH, :])
          + _mm(s, wout_ref[ATT_WIDTH:, :]))

    qx = _mm(_rms(x1, gx_ref[...]).astype(BF16), wcq_ref[...])
    heads = []
    for h in range(CA_HEADS):
        sl = slice(h * CA_HEAD_DIM, (h + 1) * CA_HEAD_DIM)
        qh = _rms(qx[:, sl], gcq_ref[...]).astype(BF16)
        kh = mk_ref[0, :, sl].astype(BF16)
        vh = mv_ref[0, :, sl].astype(BF16)
        sc = lax.dot_general(qh, kh, (((1,), (1,)), ((), ())),
                             preferred_element_type=F32) * (CA_HEAD_DIM ** -0.5)
        p = jnp.exp(sc - jnp.max(sc, axis=-1, keepdims=True))
        p = p / jnp.sum(p, axis=-1, keepdims=True)
        heads.append(_mm(p.astype(BF16), vh))
    o = jnp.concatenate(heads, axis=1).astype(BF16)
    x2 = x1 + _mm(o, wco_ref[...])
    x2_ref[0] = x2

    hn = _rms(x2, gffn_ref[...])
    hn_ref[0] = hn

    h_hi = hn.astype(BF16)
    h_lo = (hn - h_hi.astype(F32)).astype(BF16)
    r1 = _mm(h_hi, wr_ref[...])
    lg = (r1[:, :LANES] + r1[:, LANES:] + _mm(h_lo, wr_ref[:, 0:LANES])
          + br_ref[...])

    col = lax.broadcasted_iota(jnp.int32, (tm, LANES), 1)
    big = jnp.int32(4 * LANES)
    gmask = col < N_EXPERT_GROUPS
    lgg = jnp.where(gmask, lg, NEG)
    mg = jnp.max(lgg, axis=-1, keepdims=True)
    grp = jnp.min(jnp.where(gmask & (lgg == mg), col, big), axis=-1, keepdims=True)
    pg_top = 1.0 / jnp.sum(jnp.where(gmask, jnp.exp(lgg - mg), 0.0), axis=-1, keepdims=True)

    ecol = col - ROUTER_COL0
    emask = ((ecol >= 0) & (ecol < N_EXPERTS)
             & (lax.shift_right_arithmetic(ecol, 3) == grp))
    le = jnp.where(emask, lg, NEG)
    m1 = jnp.max(le, axis=-1, keepdims=True)
    i1 = jnp.min(jnp.where(emask & (le == m1), col, big), axis=-1, keepdims=True)
    rest = emask & (col != i1)
    le2 = jnp.where(rest, lg, NEG)
    m2 = jnp.max(le2, axis=-1, keepdims=True)
    i2 = jnp.min(jnp.where(rest & (le2 == m2), col, big), axis=-1, keepdims=True)
    den = jnp.sum(jnp.where(emask, jnp.exp(le - m1), 0.0), axis=-1, keepdims=True)
    p1 = 1.0 / den
    p2 = jnp.exp(m2 - m1) / den
    gate1 = pg_top * p1 / (p1 + p2)
    gate2 = pg_top * p2 / (p1 + p2)

    sel1 = col == i1
    sel2 = col == i2
    oh = jnp.where(sel1 | sel2, 1.0, 0.0)
    r_i = lax.broadcasted_iota(jnp.int32, (tm, tm), 0)
    c_i = lax.broadcasted_iota(jnp.int32, (tm, tm), 1)
    tri = jnp.where(r_i > c_i, 1.0, 0.0).astype(BF16)
    tot = base_s[...] + _mm(tri, oh.astype(BF16))
    rank1 = jnp.sum(jnp.where(sel1, tot, 0.0), axis=-1, keepdims=True)
    rank2 = jnp.sum(jnp.where(sel2, tot, 0.0), axis=-1, keepdims=True)
    base_s[...] = base_s[...] + jnp.sum(oh, axis=0, keepdims=True)
    cnt_ref[...] = base_s[...]

    e1 = (i1 - ROUTER_COL0).astype(F32)
    e2 = (i2 - ROUTER_COL0).astype(F32)
    rt = jnp.zeros((tm, LANES), F32)
    for k, val in enumerate((e1, e2, gate1, gate2, rank1, rank2)):
        rt = jnp.where(col == k, val, rt)
    rt_ref[0] = rt
    rtt_ref[0] = rt.T[0:8, :]


def _mid(x, att, ssm_tm, mk, mv, wp, cnt0, tm):
    B, S, _ = x.shape
    c2 = lambda b, i: (0, 0)
    tile = lambda w: pl.BlockSpec((1, tm, w), lambda b, i: (b, i, 0))
    return pl.pallas_call(
        _mid_kernel,
        grid=(B, S // tm),
        in_specs=[
            tile(D_MODEL), tile(ATT_WIDTH),
            pl.BlockSpec((tm, SSM_WIDTH), lambda b, i: (i, b)),
            pl.BlockSpec((1, N_MEM, CA_WIDTH), lambda b, i: (b, 0, 0)),
            pl.BlockSpec((1, N_MEM, CA_WIDTH), lambda b, i: (b, 0, 0)),
            pl.BlockSpec((1, ATT_WIDTH), c2),
            pl.BlockSpec((1, SSM_WIDTH), c2),
            pl.BlockSpec((ATT_WIDTH + SSM_WIDTH, D_MODEL), c2),
            pl.BlockSpec((1, D_MODEL), c2),
            pl.BlockSpec((D_MODEL, CA_WIDTH), c2),
            pl.BlockSpec((1, CA_HEAD_DIM), c2),
            pl.BlockSpec((CA_WIDTH, D_MODEL), c2),
            pl.BlockSpec((1, D_MODEL), c2),
            pl.BlockSpec((D_MODEL, 2 * LANES), c2),
            pl.BlockSpec((1, LANES), c2),
            pl.BlockSpec((1, LANES), c2),
        ],
        out_specs=[
            tile(D_MODEL), tile(D_MODEL), tile(LANES),
            pl.BlockSpec((1, 8, tm), lambda b, i: (b, 0, i)),
            pl.BlockSpec((1, LANES), c2),
        ],
        out_shape=[
            jax.ShapeDtypeStruct((B, S, D_MODEL), F32),
            jax.ShapeDtypeStruct((B, S, D_MODEL), F32),
            jax.ShapeDtypeStruct((B, S, LANES), F32),
            jax.ShapeDtypeStruct((B, 8, S), F32),
            jax.ShapeDtypeStruct((1, LANES), F32),
        ],
        scratch_shapes=[pltpu.VMEM((1, LANES), F32)],
        compiler_params=pltpu.CompilerParams(
            dimension_semantics=("arbitrary", "arbitrary"),
            vmem_limit_bytes=VMEM_LIMIT),
        name="mid",
    )(x, att, ssm_tm, mk, mv, wp["gao"], wp["gso"], wp["wout"], wp["gx"], wp["wcq"],
      wp["gcq"], wp["wco"], wp["gffn"], wp["wr"], wp["br"], cnt0)


def _select_part(i, tile_starts, refs):
    x = refs[0][...]
    for start, ref in zip(tile_starts[1:], refs[1:]):
        x = jnp.where(i >= start, ref[...], x)
    return x


def _part_spec(shape, tile_start, n_tiles):
    def index(i, *_):
        return (jnp.clip(i - tile_start, 0, n_tiles - 1),) + (0,) * (len(shape) - 1)
    return pl.BlockSpec(shape, index)


def _dispatch_kernel(pend_ref, padded_ref, dest_ref, *rest, tile_starts):
    n_parts = len(tile_starts)
    hn_refs = rest[:n_parts]
    xs_hbm, stage, zbuf, sem = rest[n_parts:]
    tm = hn_refs[0].shape[0]
    i = pl.program_id(0)
    slot = lax.rem(i, 2)
    blk = zbuf.shape[0]

    def wait_rows(s):
        for _ in range(2):
            pltpu.make_async_copy(stage.at[s], xs_hbm.at[pl.ds(0, tm)], sem.at[s]).wait()

    @pl.when(i == 0)
    def _():
        zbuf[...] = jnp.zeros_like(zbuf)
        for e in range(N_EXPERTS):
            @pl.when(padded_ref[e] > 0)
            def _():
                row0 = pl.multiple_of(pend_ref[e] - blk, blk)
                fill = pltpu.make_async_copy(zbuf, xs_hbm.at[pl.ds(row0, blk)], sem.at[2])
                fill.start()
                fill.wait()

        def fill_tail(b, carry):
            fill = pltpu.make_async_copy(
                zbuf, xs_hbm.at[pl.ds(pl.multiple_of(b * blk, blk), blk)], sem.at[2])
            fill.start()
            fill.wait()
            return carry

        lax.fori_loop(pend_ref[N_EXPERTS - 1] // blk, xs_hbm.shape[0] // blk, fill_tail, 0)

    @pl.when(i >= 2)
    def _():
        wait_rows(slot)

    stage[slot] = _select_part(i, tile_starts, hn_refs)
    for k in range(2):
        for r in range(tm):
            pltpu.make_async_copy(stage.at[slot, pl.ds(r, 1), :],
                                  xs_hbm.at[pl.ds(dest_ref[0, 0, k * tm + r], 1), :],
                                  sem.at[slot]).start(priority=r % 2)

    @pl.when(i == pl.num_programs(0) - 1)
    def _():
        wait_rows(slot)

        @pl.when(i >= 1)
        def _():
            wait_rows(1 - slot)


def _tile_layout(arrays, tm):
    counts = [a.shape[0] // tm for a in arrays]
    starts = [sum(counts[:p]) for p in range(len(counts))]
    return counts, starts


def _dispatch(pad_end, padded, dest_t, hns, rows, tm, blk):
    counts, starts = _tile_layout(hns, tm)
    grid_spec = pltpu.PrefetchScalarGridSpec(
        num_scalar_prefetch=2,
        grid=(sum(counts),),
        in_specs=[pl.BlockSpec((1, 1, 2 * tm), lambda i, pe, pd: (i, 0, 0),
                               memory_space=pltpu.SMEM)]
        + [_part_spec((tm, D_MODEL), s, n) for s, n in zip(starts, counts)],
        out_specs=pl.BlockSpec(memory_space=pl.ANY),
        scratch_shapes=[
            pltpu.VMEM((2, tm, D_MODEL), F32),
            pltpu.VMEM((blk, D_MODEL), F32),
            pltpu.SemaphoreType.DMA((3,)),
        ],
    )
    return pl.pallas_call(
        functools.partial(_dispatch_kernel, tile_starts=tuple(starts)),
        grid_spec=grid_spec,
        out_shape=jax.ShapeDtypeStruct((rows, D_MODEL), F32),
        compiler_params=pltpu.CompilerParams(
            dimension_semantics=("arbitrary",), vmem_limit_bytes=VMEM_LIMIT),
        name="dispatch",
    )(pad_end, padded, dest_t, *hns)


def _moe_kernel(be_ref, nu_ref, xs_ref, wg_ref, wu_ref, wd_ref, yb_ref, wg_s, wu_s, wd_s):
    i = pl.program_id(0)

    @pl.when(i < nu_ref[0])
    def _():
        @pl.when((i == 0) | (be_ref[i] != be_ref[jnp.maximum(i - 1, 0)]))
        def _():
            wg_s[...] = wg_ref[0].astype(BF16)
            wu_s[...] = wu_ref[0].astype(BF16)
            wd_s[...] = wd_ref[0].astype(BF16)

        xe = xs_ref[...].astype(BF16)
        g = _mm(xe, wg_s[...])
        u = _mm(xe, wu_s[...])
        hmid = ((g * (1.0 / (1.0 + jnp.exp(-g)))) * u).astype(BF16)
        yb_ref[...] = _mm(hmid, wd_s[...])

    @pl.when(i >= nu_ref[0])
    def _():
        yb_ref[...] = jnp.zeros_like(yb_ref)


def _moe(block_e, n_used, xs, w_gate, w_up, w_down, blk):
    n_blocks = block_e.shape[0]
    in_blk = lambda i, be, nu: (jnp.maximum(jnp.minimum(i, nu[0] - 1), 0), 0)
    grid_spec = pltpu.PrefetchScalarGridSpec(
        num_scalar_prefetch=2,
        grid=(n_blocks,),
        in_specs=[
            pl.BlockSpec((blk, D_MODEL), in_blk),
            pl.BlockSpec((1, D_MODEL, D_EXPERT), lambda i, be, nu: (be[i], 0, 0)),
            pl.BlockSpec((1, D_MODEL, D_EXPERT), lambda i, be, nu: (be[i], 0, 0)),
            pl.BlockSpec((1, D_EXPERT, D_MODEL), lambda i, be, nu: (be[i], 0, 0)),
        ],
        out_specs=pl.BlockSpec((blk, D_MODEL), lambda i, be, nu: (i, 0)),
        scratch_shapes=[
            pltpu.VMEM((D_MODEL, D_EXPERT), BF16),
            pltpu.VMEM((D_MODEL, D_EXPERT), BF16),
            pltpu.VMEM((D_EXPERT, D_MODEL), BF16),
        ],
    )
    return pl.pallas_call(
        _moe_kernel,
        grid_spec=grid_spec,
        out_shape=jax.ShapeDtypeStruct(xs.shape, F32),
        compiler_params=pltpu.CompilerParams(
            dimension_semantics=("arbitrary",), vmem_limit_bytes=VMEM_LIMIT),
        name="moe",
    )(block_e, n_used, xs, w_gate, w_up, w_down)


def _combine_kernel(dest_ref, dest_next_ref, *rest, tile_starts):
    n_parts = len(tile_starts)
    x2_refs, rt_refs = rest[:n_parts], rest[n_parts:2 * n_parts]
    yb_hbm = rest[2 * n_parts]
    o_refs = rest[2 * n_parts + 1:3 * n_parts + 1]
    buf, sem = rest[3 * n_parts + 1:]
    tm = x2_refs[0].shape[0]
    i = pl.program_id(0)
    slot = lax.rem(i, 2)

    def gather(d_ref, s):
        for k in range(2):
            for r in range(tm):
                pltpu.make_async_copy(yb_hbm.at[pl.ds(d_ref[0, 0, k * tm + r], 1), :],
                                      buf.at[s, k, pl.ds(r, 1), :],
                                      sem.at[s]).start(priority=r % 2)

    @pl.when(i == 0)
    def _():
        gather(dest_ref, 0)

    @pl.when(i + 1 < pl.num_programs(0))
    def _():
        gather(dest_next_ref, 1 - slot)

    for k in range(2):
        pltpu.make_async_copy(yb_hbm.at[pl.ds(0, tm), :], buf.at[slot, k], sem.at[slot]).wait()
    rt = _select_part(i, tile_starts, rt_refs)
    out = (_select_part(i, tile_starts, x2_refs) + rt[:, 2:3] * buf[slot, 0]
           + rt[:, 3:4] * buf[slot, 1])
    ends = tile_starts[1:] + (pl.num_programs(0),)
    for start, end, o_ref in zip(tile_starts, ends, o_refs):
        @pl.when((i >= start) & (i < end))
        def _():
            o_ref[...] = out


def _combine(dest_t, x2s, rts, yb, tm):
    counts, starts = _tile_layout(x2s, tm)
    nt = sum(counts)
    spec = lambda w: [_part_spec((tm, w), s, n) for s, n in zip(starts, counts)]
    return pl.pallas_call(
        functools.partial(_combine_kernel, tile_starts=tuple(starts)),
        grid=(nt,),
        in_specs=[
            pl.BlockSpec((1, 1, 2 * tm), lambda i: (i, 0, 0), memory_space=pltpu.SMEM),
            pl.BlockSpec((1, 1, 2 * tm), lambda i: (jnp.minimum(i + 1, nt - 1), 0, 0),
                         memory_space=pltpu.SMEM),
        ] + spec(D_MODEL) + spec(LANES) + [pl.BlockSpec(memory_space=pl.ANY)],
        out_specs=spec(D_MODEL),
        out_shape=[jax.ShapeDtypeStruct(x2.shape, F32) for x2 in x2s],
        scratch_shapes=[
            pltpu.VMEM((2, 2, tm, D_MODEL), F32),
            pltpu.SemaphoreType.DMA((2,)),
        ],
        compiler_params=pltpu.CompilerParams(
            dimension_semantics=("arbitrary",), vmem_limit_bytes=VMEM_LIMIT),
        name="combine",
    )(dest_t, dest_t, *x2s, *rts, yb)


def _hier_moe(parts, cnt, w_gate, w_up, w_down, tm, blk):
    counts = cnt[0, ROUTER_COL0:ROUTER_COL0 + N_EXPERTS].astype(jnp.int32)
    padded = (counts + blk - 1) // blk * blk
    pad_end = jnp.cumsum(padded)
    pad_start = pad_end - padded
    t_all = sum(p[0].shape[0] for p in parts)
    n_blocks = (2 * t_all + N_EXPERTS * (blk - 1)) // blk + 1
    rows = n_blocks * blk
    blk_row0 = jnp.arange(n_blocks, dtype=jnp.int32) * blk
    block_e = jnp.minimum(
        jnp.sum((pad_end[None, :] <= blk_row0[:, None]).astype(jnp.int32), axis=1),
        N_EXPERTS - 1)
    n_used = (pad_end[-1] // blk).astype(jnp.int32).reshape(1)
    experts = jnp.arange(N_EXPERTS, dtype=jnp.int32)[:, None, None]

    dests = []
    for x2, _, _, rtt in parts:
        T = x2.shape[0]
        flat = lambda a: jnp.swapaxes(a, 0, 1).reshape(a.shape[1], T)
        eid = flat(rtt[:, 0:2, :]).astype(jnp.int32)
        rank = flat(rtt[:, 4:6, :]).astype(jnp.int32)
        dest = rank + jnp.sum(
            jnp.where(eid[None] == experts, pad_start[:, None, None], 0), axis=0)
        nt = T // tm
        dests.append(dest.reshape(2, nt, tm).transpose(1, 0, 2).reshape(nt, 1, 2 * tm))
    dest_t = jnp.concatenate(dests, axis=0)
    xs = _dispatch(pad_end, padded, dest_t, [p[1] for p in parts], rows, tm, blk)
    yb = _moe(block_e, n_used, xs, w_gate, w_up, w_down, blk)
    return _combine(dest_t, [p[0] for p in parts], [p[2] for p in parts], yb, tm)


def _rope_table(pos):
    half = ROPE_DIM // 2
    inv = ROPE_THETA ** (-jnp.arange(0, ROPE_DIM, 2, dtype=F32) / ROPE_DIM)
    ang = pos.astype(F32)[:, None] * inv[None, :]
    cos, sin = jnp.cos(ang), jnp.sin(ang)
    L = pos.shape[0]
    pad = jnp.zeros((L, HEAD_DIM - ROPE_DIM), F32)
    zero = jnp.zeros((L, half), F32)
    c64 = jnp.concatenate([cos, cos, pad + 1.0], axis=1)
    lo64 = jnp.concatenate([-sin, zero, pad], axis=1)
    hi64 = jnp.concatenate([zero, sin, pad], axis=1)
    two = lambda t: jnp.concatenate([t, t], axis=1)
    return jnp.concatenate([two(c64), two(lo64), two(hi64)], axis=1)


def _mixers(x, pos_rope, kctx_prev, vctx_prev, h0r, h0i, mk, mv, wp, sp, cnt0, *,
            tm_in, tq, ssm_l, tm_mid, mask_context):
    B, S, _ = x.shape
    T = B * S
    q, k3, v3, u_tm = _in_proj(x, wp["gmix"], wp["win"], wp["gq"], wp["gk"], pos_rope, tm_in)
    kctx = jnp.concatenate([kctx_prev, k3], axis=1)
    vctx = jnp.concatenate([vctx_prev, v3], axis=1)
    att = _swa(wp["sink"], q, kctx, vctx, tq, mask_context)
    ssm_tm, hr, hi = _ssm(u_tm.reshape(S, B, SSM_WIDTH), h0r, h0i, sp, ssm_l)
    x2, hn, rt, rtt, cnt = _mid(x, att, ssm_tm.reshape(S, B * SSM_WIDTH), mk, mv, wp, cnt0,
                                tm_mid)
    part = (x2.reshape(T, D_MODEL), hn.reshape(T, D_MODEL), rt.reshape(T, LANES), rtt)
    return part, cnt, k3, v3, hr, hi


def kernel(x_prompt, x_sample, cache_attn_k, cache_attn_v, state_ssm_re, state_ssm_im, cache_mem_k, cache_mem_v, mem_prompt, norm_mix, w_in, q_norm, k_norm, attn_sink, ssm_lambda_re, ssm_lambda_im, ssm_log_dt, ssm_b_re, ssm_b_im, ssm_c_re, ssm_c_im, ssm_d, ssm_w_glu, ssm_b_glu, norm_attn_out, norm_ssm_out, w_out, norm_cross, norm_mem, w_cq, w_ck, w_cv, cq_norm, ck_norm, w_co, norm_ffn, w_router_group, b_router_group, w_router_expert, b_router_expert, w_e_gate, w_e_up, w_e_down):
    depth = norm_mix.shape[0]
    Bp, Lp, _ = x_prompt.shape
    Bs, Ls, _ = x_sample.shape
    yp, ys = x_prompt, x_sample
    rope_p = _rope_table(jnp.arange(Lp, dtype=jnp.int32))
    rope_s = _rope_table(PAST_LEN + jnp.arange(Ls, dtype=jnp.int32))
    outs = [[] for _ in range(10)]
    n_router = N_EXPERT_GROUPS + N_EXPERTS
    for l in range(depth):
        row = lambda a: a[l].astype(F32).reshape(1, -1)
        w_r = jnp.pad(jnp.concatenate([w_router_group[l], w_router_expert[l]], axis=1).astype(F32),
                      ((0, 0), (0, LANES - n_router)))
        w_r_hi = w_r.astype(BF16)
        w_r_lo = (w_r - w_r_hi.astype(F32)).astype(BF16)
        b_r = jnp.pad(jnp.concatenate([b_router_group[l], b_router_expert[l]]).astype(F32),
                      (0, LANES - n_router)).reshape(1, LANES)
        wp = {
            "gmix": row(norm_mix), "win": w_in[l].astype(BF16),
            "gq": jnp.tile(row(q_norm), (1, LANES // HEAD_DIM)),
            "gk": jnp.tile(row(k_norm), (1, LANES // HEAD_DIM)),
            "sink": attn_sink[l].astype(F32),
            "gao": row(norm_attn_out), "gso": row(norm_ssm_out),
            "wout": w_out[l].astype(BF16), "gx": row(norm_cross),
            "wcq": w_cq[l].astype(BF16), "gcq": row(cq_norm),
            "wco": w_co[l].astype(BF16), "gffn": row(norm_ffn),
            "wr": jnp.concatenate([w_r_hi, w_r_lo], axis=1), "br": b_r,
        }
        sp = _ssm_params(ssm_lambda_re[l], ssm_lambda_im[l], ssm_log_dt[l], ssm_b_re[l],
                         ssm_b_im[l], ssm_c_re[l], ssm_c_im[l], ssm_d[l], ssm_w_glu[l],
                         ssm_b_glu[l])
        ew = (w_e_gate[l].astype(F32), w_e_up[l].astype(F32), w_e_down[l].astype(F32))

        w_ckv = jnp.concatenate([w_ck[l], w_cv[l]], axis=1).astype(BF16)
        mkp, mvp = _memkv(mem_prompt.reshape(Bp * N_MEM, D_MODEL), row(norm_mem), w_ckv,
                          row(ck_norm), 512)
        mkp = mkp.reshape(Bp, N_MEM, CA_WIDTH)
        mvp = mvp.reshape(Bp, N_MEM, CA_WIDTH)

        zctx = jnp.zeros((Bp, WINDOW, KV_WIDTH), F32)
        zst = jnp.zeros((Bp, SSM_COLS), F32)
        part_p, cnt_p, kp, vp, hpr, hpi = _mixers(
            yp, rope_p, zctx, zctx, zst, zst, mkp, mvp, wp, sp, jnp.zeros((1, LANES), F32),
            tm_in=512, tq=256, ssm_l=64, tm_mid=512, mask_context=True)
        part_s, cnt_s, kn, vn, hsr, hsi = _mixers(
            ys, rope_s, cache_attn_k[l].reshape(Bs, WINDOW, KV_WIDTH).astype(F32),
            cache_attn_v[l].reshape(Bs, WINDOW, KV_WIDTH).astype(F32),
            state_ssm_re[l].astype(F32).reshape(Bs, SSM_COLS),
            state_ssm_im[l].astype(F32).reshape(Bs, SSM_COLS),
            cache_mem_k[l].astype(F32).reshape(Bs, N_MEM, CA_WIDTH),
            cache_mem_v[l].astype(F32).reshape(Bs, N_MEM, CA_WIDTH), wp, sp, cnt_p,
            tm_in=Ls, tq=CHUNK, ssm_l=Ls, tm_mid=Ls, mask_context=False)
        yp, ys = _hier_moe([part_p, part_s], cnt_s, *ew, 256, 2 * MOE_BLOCK)
        yp = yp.reshape(Bp, Lp, D_MODEL)
        ys = ys.reshape(Bs, Ls, D_MODEL)

        sg = (N_SSM_GROUPS, SSM_STATE)
        kvs = (N_KV_HEADS, HEAD_DIM)
        vals = (kp[:, Lp - WINDOW:].reshape(Bp, WINDOW, *kvs),
                vp[:, Lp - WINDOW:].reshape(Bp, WINDOW, *kvs),
                hpr.reshape(Bp, *sg), hpi.reshape(Bp, *sg),
                mkp.reshape(Bp, N_MEM, CA_HEADS, CA_HEAD_DIM),
                mvp.reshape(Bp, N_MEM, CA_HEADS, CA_HEAD_DIM),
                kn.reshape(Bs, Ls, *kvs), vn.reshape(Bs, Ls, *kvs),
                hsr.reshape(Bs, *sg), hsi.reshape(Bs, *sg))
        for lst, val in zip(outs, vals):
            lst.append(val)
    return (yp, ys) + tuple(jnp.stack(lst) for lst in outs)
```

```python
import functools
import math

import jax
import jax.numpy as jnp
from jax import lax
from jax.experimental import pallas as pl
from jax.experimental.pallas import tpu as pltpu

F32 = jnp.float32
BF16 = jnp.bfloat16

D_MODEL = 1024
CHUNK = 64
N_Q_HEADS = 8
N_KV_HEADS = 2
GQA = N_Q_HEADS // N_KV_HEADS
HEAD_DIM = 64
WINDOW = 128
BAND = WINDOW + CHUNK
ROPE_DIM = HEAD_DIM // 4
ROPE_THETA = 500000.0
ATT_WIDTH = N_Q_HEADS * HEAD_DIM
KV_WIDTH = N_KV_HEADS * HEAD_DIM
SSM_GROUP = 16
SSM_WIDTH = D_MODEL // 2
N_SSM_GROUPS = SSM_WIDTH // SSM_GROUP
SSM_STATE = 64
SSM_COLS = N_SSM_GROUPS * SSM_STATE
IN_WIDTH = ATT_WIDTH + 2 * KV_WIDTH + SSM_WIDTH
N_MEM = 256
CA_HEADS = 4
CA_HEAD_DIM = 128
CA_WIDTH = CA_HEADS * CA_HEAD_DIM
N_EXPERT_GROUPS = 4
EXPERTS_PER_GROUP = 8
N_EXPERTS = N_EXPERT_GROUPS * EXPERTS_PER_GROUP
D_EXPERT = 512
MOE_BLOCK = 256
EPS = 1e-6
NEG = -1e30
PAST_LEN = 4096

LANES = 128
ROUTER_COL0 = N_EXPERT_GROUPS
VMEM_LIMIT = 48 * 1024 * 1024


def _rms(x, g):
    ms = jnp.mean(x * x, axis=-1, keepdims=True)
    return (x * lax.rsqrt(ms + EPS)) * g


def _mm(a, b):
    return jnp.dot(a, b, preferred_element_type=F32)


def _in_proj_kernel(x_ref, g_ref, w_ref, gq_ref, gk_ref, rope_ref,
                    q_ref, k_ref, v_ref, u_ref):
    tm = x_ref.shape[1]
    h = _rms(x_ref[0], g_ref[...])
    hin = _mm(h.astype(BF16), w_ref[...])
    rope = rope_ref[...]
    cos = rope[:, 0:LANES]
    sin_lo = rope[:, LANES:2 * LANES]
    sin_hi = rope[:, 2 * LANES:3 * LANES]
    lane = lax.broadcasted_iota(jnp.int32, (tm, LANES), 1)
    left = lane < HEAD_DIM

    def norm_rope(z, g):
        sq = z * z
        lsum = jnp.sum(jnp.where(left, sq, 0.0), axis=-1, keepdims=True)
        rsum = jnp.sum(jnp.where(left, 0.0, sq), axis=-1, keepdims=True)
        ms = jnp.where(left, lsum, rsum) * (1.0 / HEAD_DIM)
        zn = (z * lax.rsqrt(ms + EPS)) * g
        half = ROPE_DIM // 2
        return (zn * cos + pltpu.roll(zn, LANES - half, 1) * sin_lo
                + pltpu.roll(zn, half, 1) * sin_hi)

    for j in range(ATT_WIDTH // LANES):
        sl = slice(j * LANES, (j + 1) * LANES)
        q_ref[0, :, sl] = norm_rope(hin[:, sl], gq_ref[...])
    k_ref[0] = norm_rope(hin[:, ATT_WIDTH:ATT_WIDTH + KV_WIDTH], gk_ref[...])
    v_ref[0] = hin[:, ATT_WIDTH + KV_WIDTH:ATT_WIDTH + 2 * KV_WIDTH]
    u_ref[...] = hin[:, ATT_WIDTH + 2 * KV_WIDTH:]


def _in_proj(x, g, w_bf, gq, gk, rope, tm):
    B, S, _ = x.shape
    full = lambda b, i: (0, 0)
    tile = lambda w: pl.BlockSpec((1, tm, w), lambda b, i: (b, i, 0))
    return pl.pallas_call(
        _in_proj_kernel,
        grid=(B, S // tm),
        in_specs=[
            tile(D_MODEL),
            pl.BlockSpec((1, D_MODEL), full),
            pl.BlockSpec((D_MODEL, IN_WIDTH), full),
            pl.BlockSpec((1, LANES), full),
            pl.BlockSpec((1, LANES), full),
            pl.BlockSpec((tm, 3 * LANES), lambda b, i: (i, 0)),
        ],
        out_specs=[
            tile(ATT_WIDTH), tile(KV_WIDTH), tile(KV_WIDTH),
            pl.BlockSpec((tm, SSM_WIDTH), lambda b, i: (i, b)),
        ],
        out_shape=[
            jax.ShapeDtypeStruct((B, S, ATT_WIDTH), F32),
            jax.ShapeDtypeStruct((B, S, KV_WIDTH), F32),
            jax.ShapeDtypeStruct((B, S, KV_WIDTH), F32),
            jax.ShapeDtypeStruct((S, B * SSM_WIDTH), F32),
        ],
        compiler_params=pltpu.CompilerParams(
            dimension_semantics=("arbitrary", "arbitrary"),
            vmem_limit_bytes=VMEM_LIMIT),
        name="in_proj",
    )(x, g, w_bf, gq, gk, rope)


def _swa_kernel(sink_ref, q_ref, k_ref, v_ref, o_ref, *, mask_context):
    tq = q_ref.shape[1]
    i = pl.program_id(1)
    nch = tq // CHUNK
    lane = lax.broadcasted_iota(jnp.int32, (BAND, LANES), 1)
    lo_half = lane < HEAD_DIM
    slabs_per_kv = GQA * HEAD_DIM // LANES

    units = []
    scores = []
    vpads = {}
    valids = {}
    for c in range(nch):
        chunk = i * nch + c
        if mask_context:
            first = jnp.maximum(chunk - WINDOW // CHUNK, 0)
            start = pl.multiple_of(first * CHUNK, CHUNK)
            kidx = start + lax.broadcasted_iota(jnp.int32, (1, BAND), 1)
            valids[c] = kidx < (chunk + 1) * CHUNK
        else:
            start = pl.multiple_of(chunk * CHUNK, CHUNK)
        kb = k_ref[0, pl.ds(start, BAND), :]
        vb = v_ref[0, pl.ds(start, BAND), :]
        kb_sw = pltpu.roll(kb, HEAD_DIM, 1)
        vb_sw = pltpu.roll(vb, HEAD_DIM, 1)
        for kvh in range(N_KV_HEADS):
            k_own, k_oth = (kb, kb_sw) if kvh == 0 else (kb_sw, kb)
            v_own, v_oth = (vb, vb_sw) if kvh == 0 else (vb_sw, vb)
            kpad = (jnp.where(lo_half, k_own, 0.0).astype(BF16),
                    jnp.where(lo_half, 0.0, k_oth).astype(BF16))
            vpads[(c, kvh)] = (jnp.where(lo_half, v_own, 0.0).astype(BF16),
                               jnp.where(lo_half, 0.0, v_oth).astype(BF16))
            col0 = kvh * GQA * HEAD_DIM
            q2 = jnp.concatenate(
                [q_ref[0, c * CHUNK:(c + 1) * CHUNK, col0 + m * LANES:col0 + (m + 1) * LANES]
                 for m in range(slabs_per_kv)], axis=0).astype(BF16)
            for side in range(2):
                s = lax.dot_general(q2, kpad[side], (((1,), (1,)), ((), ())),
                                    preferred_element_type=F32) * (HEAD_DIM ** -0.5)
                if mask_context:
                    s = jnp.where(valids[c], s, NEG)
                units.append((c, kvh, side))
                scores.append(s)

    s_all = jnp.concatenate(scores, axis=0)
    sk = jnp.concatenate(
        [jnp.full((CHUNK, 1), sink_ref[kvh * GQA + 2 * m + side], F32)
         for (_, kvh, side) in units for m in range(slabs_per_kv)], axis=0)
    mx = jnp.maximum(jnp.max(s_all, axis=-1, keepdims=True), sk)
    p_all = jnp.exp(s_all - mx)
    den = jnp.sum(p_all, axis=-1, keepdims=True) + jnp.exp(sk - mx)
    p_all = (p_all / den).astype(BF16)
    rows_u = slabs_per_kv * CHUNK
    probs = [p_all[n * rows_u:(n + 1) * rows_u] for n in range(len(units))]

    for n in range(0, len(units), 2):
        c, kvh, _ = units[n]
        vp = vpads[(c, kvh)]
        o = _mm(probs[n], vp[0]) + _mm(probs[n + 1], vp[1])
        col0 = kvh * GQA * HEAD_DIM
        for m in range(slabs_per_kv):
            o_ref[0, c * CHUNK:(c + 1) * CHUNK, col0 + m * LANES:col0 + (m + 1) * LANES] = (
                o[m * CHUNK:(m + 1) * CHUNK])


def _swa(sink, q, kctx, vctx, tq, mask_context):
    B, Sq, _ = q.shape
    Sk = kctx.shape[1]
    return pl.pallas_call(
        functools.partial(_swa_kernel, mask_context=mask_context),
        grid=(B, Sq // tq),
        in_specs=[
            pl.BlockSpec(memory_space=pltpu.SMEM),
            pl.BlockSpec((1, tq, ATT_WIDTH), lambda b, i: (b, i, 0)),
            pl.BlockSpec((1, Sk, KV_WIDTH), lambda b, i: (b, 0, 0)),
            pl.BlockSpec((1, Sk, KV_WIDTH), lambda b, i: (b, 0, 0)),
        ],
        out_specs=pl.BlockSpec((1, tq, ATT_WIDTH), lambda b, i: (b, i, 0)),
        out_shape=jax.ShapeDtypeStruct((B, Sq, ATT_WIDTH), F32),
        compiler_params=pltpu.CompilerParams(
            dimension_semantics=("arbitrary", "arbitrary"),
            vmem_limit_bytes=VMEM_LIMIT),
        name="swa",
    )(sink, q, kctx, vctx)


def _ssm_kernel(u_ref, h0r_ref, h0i_ref, lam_ref, bre_ref, bim_ref, cre_ref, cim_ref,
                d_ref, wglu_ref, bglu_ref,
                y_ref, hr_out, hi_out, sr, si, hr_s, hi_s):
    L, B, _ = u_ref.shape
    rows = L * B
    half_w = SSM_WIDTH // 2
    half_c = SSM_COLS // 2

    @pl.when(pl.program_id(0) == 0)
    def _():
        hr_s[...] = h0r_ref[...]
        hi_s[...] = h0i_ref[...]

    u = u_ref[...].reshape(rows, SSM_WIDTH)
    ub = u.astype(BF16)
    for hf in range(2):
        uh = ub[:, hf * half_w:(hf + 1) * half_w]
        sr[:, hf * half_c:(hf + 1) * half_c] = _mm(uh, bre_ref[hf])
        si[:, hf * half_c:(hf + 1) * half_c] = _mm(uh, bim_ref[hf])

    cw = 4 * LANES
    for cc in range(SSM_COLS // cw):
        cols = slice(cc * cw, (cc + 1) * cw)
        lr = jnp.broadcast_to(lam_ref[0:1, cols], (B, cw))
        li = jnp.broadcast_to(lam_ref[1:2, cols], (B, cw))

        def body(t, carry):
            hr, hi = carry
            at_t = pl.ds(pl.multiple_of(t * B, B), B)
            nr = lr * hr - li * hi + sr[at_t, cols]
            ni = lr * hi + li * hr + si[at_t, cols]
            sr[at_t, cols] = nr
            si[at_t, cols] = ni
            return nr, ni

        hr, hi = lax.fori_loop(0, L, body, (hr_s[:, cols], hi_s[:, cols]), unroll=2)
        hr_s[:, cols] = hr
        hi_s[:, cols] = hi

    ys = []
    for hf in range(2):
        cs = slice(hf * half_c, (hf + 1) * half_c)
        ys.append(_mm(sr[:, cs].astype(BF16), cre_ref[hf])
                  + _mm(si[:, cs].astype(BF16), cim_ref[hf]))
    y = jnp.concatenate(ys, axis=1) + d_ref[...] * u
    g = 0.5 * y * (1.0 + jnp.tanh(math.sqrt(2.0 / math.pi) * (y + 0.044715 * (y * y * y))))
    gb = g.astype(BF16)
    z = jnp.concatenate(
        [_mm(gb[:, hf * half_w:(hf + 1) * half_w], wglu_ref[hf]) for hf in range(2)],
        axis=1) + bglu_ref[...]
    out = g * (1.0 / (1.0 + jnp.exp(-z)))
    y_ref[...] = out.reshape(L, B, SSM_WIDTH)
    hr_out[...] = hr_s[...]
    hi_out[...] = hi_s[...]


def _ssm(u, h0r, h0i, sp, L):
    S, B, _ = u.shape
    c2 = lambda i: (0, 0)
    c3 = lambda i: (0, 0, 0)
    return pl.pallas_call(
        _ssm_kernel,
        grid=(S // L,),
        in_specs=[
            pl.BlockSpec((L, B, SSM_WIDTH), lambda i: (i, 0, 0)),
            pl.BlockSpec((B, SSM_COLS), c2),
            pl.BlockSpec((B, SSM_COLS), c2),
            pl.BlockSpec((2, SSM_COLS), c2),
            pl.BlockSpec((2, SSM_WIDTH // 2, SSM_COLS // 2), c3),
            pl.BlockSpec((2, SSM_WIDTH // 2, SSM_COLS // 2), c3),
            pl.BlockSpec((2, SSM_COLS // 2, SSM_WIDTH // 2), c3),
            pl.BlockSpec((2, SSM_COLS // 2, SSM_WIDTH // 2), c3),
            pl.BlockSpec((1, SSM_WIDTH), c2),
            pl.BlockSpec((2, SSM_WIDTH // 2, SSM_WIDTH // 2), c3),
            pl.BlockSpec((1, SSM_WIDTH), c2),
        ],
        out_specs=[
            pl.BlockSpec((L, B, SSM_WIDTH), lambda i: (i, 0, 0)),
            pl.BlockSpec((B, SSM_COLS), c2),
            pl.BlockSpec((B, SSM_COLS), c2),
        ],
        out_shape=[
            jax.ShapeDtypeStruct((S, B, SSM_WIDTH), F32),
            jax.ShapeDtypeStruct((B, SSM_COLS), F32),
            jax.ShapeDtypeStruct((B, SSM_COLS), F32),
        ],
        scratch_shapes=[
            pltpu.VMEM((L * B, SSM_COLS), F32),
            pltpu.VMEM((L * B, SSM_COLS), F32),
            pltpu.VMEM((B, SSM_COLS), F32),
            pltpu.VMEM((B, SSM_COLS), F32),
        ],
        compiler_params=pltpu.CompilerParams(
            dimension_semantics=("arbitrary",), vmem_limit_bytes=VMEM_LIMIT),
        name="ssm",
    )(u, h0r, h0i, sp["lam"], sp["bre"], sp["bim"], sp["cre"], sp["cim"],
      sp["d"], sp["wglu"], sp["bglu"])


def _block_diag(blocks):
    G, r, c = blocks.shape
    eye = jnp.eye(G, dtype=blocks.dtype)
    return jnp.einsum("grc,gh->grhc", blocks, eye).reshape(G * r, G * c)


def _ssm_params(lam_re, lam_im, log_dt, b_re, b_im, c_re, c_im, d, w_glu, b_glu):
    lam = lax.complex(lam_re.astype(F32), lam_im.astype(F32))
    dt = jnp.exp(log_dt.astype(F32))[:, None]
    lam_bar = jnp.exp(lam * dt)
    bmat = lax.complex(b_re.astype(F32), b_im.astype(F32))
    b_bar = ((lam_bar - 1.0) / lam)[..., None] * bmat
    lam2 = jnp.stack([lam_bar.real.reshape(-1), lam_bar.imag.reshape(-1)])
    bt = jnp.swapaxes(b_bar, 1, 2)
    hw, hc = SSM_WIDTH // 2, SSM_COLS // 2
    split_b = lambda m: jnp.stack([m[:hw, :hc], m[hw:, hc:]]).astype(BF16)
    split_c = lambda m: jnp.stack([m[:hc, :hw], m[hc:, hw:]]).astype(BF16)
    ct_re = jnp.swapaxes(c_re.astype(F32), 1, 2)
    ct_im = jnp.swapaxes(c_im.astype(F32), 1, 2)
    wg = _block_diag(w_glu.astype(F32))
    return {
        "lam": lam2,
        "bre": split_b(_block_diag(bt.real)),
        "bim": split_b(_block_diag(bt.imag)),
        "cre": split_c(_block_diag(ct_re)),
        "cim": split_c(_block_diag(-ct_im)),
        "d": d.astype(F32).reshape(1, SSM_WIDTH),
        "wglu": jnp.stack([wg[:hw, :hw], wg[hw:, hw:]]).astype(BF16),
        "bglu": b_glu.astype(F32).reshape(1, SSM_WIDTH),
    }


def _memkv_kernel(m_ref, g_ref, w_ref, gk_ref, k_ref, v_ref):
    m = _rms(m_ref[...], g_ref[...])
    kv = _mm(m.astype(BF16), w_ref[...])
    for h in range(CA_HEADS):
        sl = slice(h * CA_HEAD_DIM, (h + 1) * CA_HEAD_DIM)
        k_ref[:, sl] = _rms(kv[:, sl], gk_ref[...])
    v_ref[...] = kv[:, CA_WIDTH:]


def _memkv(mem2d, g, w_bf, gk, tm):
    T = mem2d.shape[0]
    full = lambda i: (0, 0)
    return pl.pallas_call(
        _memkv_kernel,
        grid=(T // tm,),
        in_specs=[
            pl.BlockSpec((tm, D_MODEL), lambda i: (i, 0)),
            pl.BlockSpec((1, D_MODEL), full),
            pl.BlockSpec((D_MODEL, 2 * CA_WIDTH), full),
            pl.BlockSpec((1, CA_HEAD_DIM), full),
        ],
        out_specs=[
            pl.BlockSpec((tm, CA_WIDTH), lambda i: (i, 0)),
            pl.BlockSpec((tm, CA_WIDTH), lambda i: (i, 0)),
        ],
        out_shape=[
            jax.ShapeDtypeStruct((T, CA_WIDTH), F32),
            jax.ShapeDtypeStruct((T, CA_WIDTH), F32),
        ],
        compiler_params=pltpu.CompilerParams(
            dimension_semantics=("arbitrary",), vmem_limit_bytes=VMEM_LIMIT),
        name="memkv",
    )(mem2d, g, w_bf, gk)


def _mid_kernel(x_ref, att_ref, ssm_ref, mk_ref, mv_ref,
                gao_ref, gso_ref, wout_ref, gx_ref, wcq_ref, gcq_ref, wco_ref,
                gffn_ref, wr_ref, br_ref, cnt0_ref,
                x2_ref, hn_ref, rt_ref, rtt_ref, cnt_ref, base_s):
    nb, ts, _ = x_ref.shape
    tm = nb * ts

    @pl.when((pl.program_id(0) == 0) & (pl.program_id(1) == 0))
    def _():
        base_s[...] = cnt0_ref[...]

    ssm = jnp.concatenate(
        [ssm_ref[:, b * SSM_WIDTH:(b + 1) * SSM_WIDTH] for b in range(nb)], axis=0)
    a = _rms(att_ref[...].reshape(tm, ATT_WIDTH), gao_ref[...]).astype(BF16)
    s = _rms(ssm, gso_ref[...]).astype(BF16)
    x1 = (x_ref[...].reshape(tm, D_MODEL) + _mm(a, wout_ref[0:ATT_WIDTH, :])
          + _mm(s, wout_ref[ATT_WIDTH:, :]))

    qx = _mm(_rms(x1, gx_ref[...]).astype(BF16), wcq_ref[...])
    heads = []
    for h in range(CA_HEADS):
        sl = slice(h * CA_HEAD_DIM, (h + 1) * CA_HEAD_DIM)
        qh = _rms(qx[:, sl], gcq_ref[...]).astype(BF16)
        per_batch = []
        for b in range(nb):
            kh = mk_ref[b, :, sl].astype(BF16)
            vh = mv_ref[b, :, sl].astype(BF16)
            sc = lax.dot_general(qh[b * ts:(b + 1) * ts], kh, (((1,), (1,)), ((), ())),
                                 preferred_element_type=F32) * (CA_HEAD_DIM ** -0.5)
            p = jnp.exp(sc - jnp.max(sc, axis=-1, keepdims=True))
            p = p / jnp.sum(p, axis=-1, keepdims=True)
            per_batch.append(_mm(p.astype(BF16), vh))
        heads.append(jnp.concatenate(per_batch, axis=0))
    o = jnp.concatenate(heads, axis=1).astype(BF16)
    x2 = x1 + _mm(o, wco_ref[...])
    x2_ref[...] = x2.reshape(nb, ts, D_MODEL)

    hn = _rms(x2, gffn_ref[...])
    hn_ref[...] = hn.reshape(nb, ts, D_MODEL)

    h_hi = hn.astype(BF16)
    h_lo = (hn - h_hi.astype(F32)).astype(BF16)
    r1 = _mm(h_hi, wr_ref[...])
    lg = (r1[:, :LANES] + r1[:, LANES:] + _mm(h_lo, wr_ref[:, 0:LANES])
          + br_ref[...])

    col = lax.broadcasted_iota(jnp.int32, (tm, LANES), 1)
    big = jnp.int32(4 * LANES)
    gmask = col < N_EXPERT_GROUPS
    lgg = jnp.where(gmask, lg, NEG)
    mg = jnp.max(lgg, axis=-1, keepdims=True)
    grp = jnp.min(jnp.where(gmask & (lgg == mg), col, big), axis=-1, keepdims=True)
    pg_top = 1.0 / jnp.sum(jnp.where(gmask, jnp.exp(lgg - mg), 0.0), axis=-1, keepdims=True)

    ecol = col - ROUTER_COL0
    emask = ((ecol >= 0) & (ecol < N_EXPERTS)
             & (lax.shift_right_arithmetic(ecol, 3) == grp))
    le = jnp.where(emask, lg, NEG)
    m1 = jnp.max(le, axis=-1, keepdims=True)
    i1 = jnp.min(jnp.where(emask & (le == m1), col, big), axis=-1, keepdims=True)
    rest = emask & (col != i1)
    le2 = jnp.where(rest, lg, NEG)
    m2 = jnp.max(le2, axis=-1, keepdims=True)
    i2 = jnp.min(jnp.where(rest & (le2 == m2), col, big), axis=-1, keepdims=True)
    den = jnp.sum(jnp.where(emask, jnp.exp(le - m1), 0.0), axis=-1, keepdims=True)
    p1 = 1.0 / den
    p2 = jnp.exp(m2 - m1) / den
    gate1 = pg_top * p1 / (p1 + p2)
    gate2 = pg_top * p2 / (p1 + p2)

    sel1 = col == i1
    sel2 = col == i2
    oh = jnp.where(sel1 | sel2, 1.0, 0.0)
    r_i = lax.broadcasted_iota(jnp.int32, (tm, tm), 0)
    c_i = lax.broadcasted_iota(jnp.int32, (tm, tm), 1)
    tri = jnp.where(r_i > c_i, 1.0, 0.0).astype(BF16)
    tot = base_s[...] + _mm(tri, oh.astype(BF16))
    rank1 = jnp.sum(jnp.where(sel1, tot, 0.0), axis=-1, keepdims=True)
    rank2 = jnp.sum(jnp.where(sel2, tot, 0.0), axis=-1, keepdims=True)
    base_s[...] = base_s[...] + jnp.sum(oh, axis=0, keepdims=True)
    cnt_ref[...] = base_s[...]

    e1 = (i1 - ROUTER_COL0).astype(F32)
    e2 = (i2 - ROUTER_COL0).astype(F32)
    rt = jnp.zeros((tm, LANES), F32)
    for k, val in enumerate((e1, e2, gate1, gate2, rank1, rank2)):
        rt = jnp.where(col == k, val, rt)
    rt_ref[...] = rt.reshape(nb, ts, LANES)
    for b in range(nb):
        rtt_ref[b] = rt[b * ts:(b + 1) * ts].T[0:8, :]


def _mid(x, att, ssm_tm, mk, mv, wp, cnt0, nb, ts):
    B, S, _ = x.shape
    c2 = lambda b, i: (0, 0)
    tile = lambda w: pl.BlockSpec((nb, ts, w), lambda b, i: (b, i, 0))
    return pl.pallas_call(
        _mid_kernel,
        grid=(B // nb, S // ts),
        in_specs=[
            tile(D_MODEL), tile(ATT_WIDTH),
            pl.BlockSpec((ts, nb * SSM_WIDTH), lambda b, i: (i, b)),
            pl.BlockSpec((nb, N_MEM, CA_WIDTH), lambda b, i: (b, 0, 0)),
            pl.BlockSpec((nb, N_MEM, CA_WIDTH), lambda b, i: (b, 0, 0)),
            pl.BlockSpec((1, ATT_WIDTH), c2),
            pl.BlockSpec((1, SSM_WIDTH), c2),
            pl.BlockSpec((ATT_WIDTH + SSM_WIDTH, D_MODEL), c2),
            pl.BlockSpec((1, D_MODEL), c2),
            pl.BlockSpec((D_MODEL, CA_WIDTH), c2),
            pl.BlockSpec((1, CA_HEAD_DIM), c2),
            pl.BlockSpec((CA_WIDTH, D_MODEL), c2),
            pl.BlockSpec((1, D_MODEL), c2),
            pl.BlockSpec((D_MODEL, 2 * LANES), c2),
            pl.BlockSpec((1, LANES), c2),
            pl.BlockSpec((1, LANES), c2),
        ],
        out_specs=[
            tile(D_MODEL), tile(D_MODEL), tile(LANES),
            pl.BlockSpec((nb, 8, ts), lambda b, i: (b, 0, i)),
            pl.BlockSpec((1, LANES), c2),
        ],
        out_shape=[
            jax.ShapeDtypeStruct((B, S, D_MODEL), F32),
            jax.ShapeDtypeStruct((B, S, D_MODEL), F32),
            jax.ShapeDtypeStruct((B, S, LANES), F32),
            jax.ShapeDtypeStruct((B, 8, S), F32),
            jax.ShapeDtypeStruct((1, LANES), F32),
        ],
        scratch_shapes=[pltpu.VMEM((1, LANES), F32)],
        compiler_params=pltpu.CompilerParams(
            dimension_semantics=("arbitrary", "arbitrary"),
            vmem_limit_bytes=VMEM_LIMIT),
        name="mid",
    )(x, att, ssm_tm, mk, mv, wp["gao"], wp["gso"], wp["wout"], wp["gx"], wp["wcq"],
      wp["gcq"], wp["wco"], wp["gffn"], wp["wr"], wp["br"], cnt0)


def _select_part(i, tile_starts, refs):
    x = refs[0][...]
    for start, ref in zip(tile_starts[1:], refs[1:]):
        x = jnp.where(i >= start, ref[...], x)
    return x


def _part_spec(shape, tile_start, n_tiles):
    def index(i, *_):
        return (jnp.clip(i - tile_start, 0, n_tiles - 1),) + (0,) * (len(shape) - 1)
    return pl.BlockSpec(shape, index)


def _dispatch_kernel(pend_ref, padded_ref, dest_ref, *rest, tile_starts):
    n_parts = len(tile_starts)
    hn_refs = rest[:n_parts]
    xs_hbm, stage, zbuf, sem = rest[n_parts:]
    tm = hn_refs[0].shape[0]
    i = pl.program_id(0)
    slot = lax.rem(i, 2)
    blk = zbuf.shape[0]

    def wait_rows(s):
        for _ in range(2):
            pltpu.make_async_copy(stage.at[s], xs_hbm.at[pl.ds(0, tm)], sem.at[s]).wait()

    @pl.when(i == 0)
    def _():
        zbuf[...] = jnp.zeros_like(zbuf)
        for e in range(N_EXPERTS):
            @pl.when(padded_ref[e] > 0)
            def _():
                row0 = pl.multiple_of(pend_ref[e] - blk, blk)
                fill = pltpu.make_async_copy(zbuf, xs_hbm.at[pl.ds(row0, blk)], sem.at[2])
                fill.start()
                fill.wait()

        def fill_tail(b, carry):
            fill = pltpu.make_async_copy(
                zbuf, xs_hbm.at[pl.ds(pl.multiple_of(b * blk, blk), blk)], sem.at[2])
            fill.start()
            fill.wait()
            return carry

        lax.fori_loop(pend_ref[N_EXPERTS - 1] // blk, xs_hbm.shape[0] // blk, fill_tail, 0)

    @pl.when(i >= 2)
    def _():
        wait_rows(slot)

    stage[slot] = _select_part(i, tile_starts, hn_refs)
    for k in range(2):
        for r in range(tm):
            pltpu.make_async_copy(stage.at[slot, pl.ds(r, 1), :],
                                  xs_hbm.at[pl.ds(dest_ref[0, 0, k * tm + r], 1), :],
                                  sem.at[slot]).start(priority=r % 2)

    @pl.when(i == pl.num_programs(0) - 1)
    def _():
        wait_rows(slot)

        @pl.when(i >= 1)
        def _():
            wait_rows(1 - slot)


def _tile_layout(arrays, tm):
    counts = [a.shape[0] // tm for a in arrays]
    starts = [sum(counts[:p]) for p in range(len(counts))]
    return counts, starts


def _dispatch(pad_end, padded, dest_t, hns, rows, tm, blk):
    counts, starts = _tile_layout(hns, tm)
    grid_spec = pltpu.PrefetchScalarGridSpec(
        num_scalar_prefetch=2,
        grid=(sum(counts),),
        in_specs=[pl.BlockSpec((1, 1, 2 * tm), lambda i, pe, pd: (i, 0, 0),
                               memory_space=pltpu.SMEM)]
        + [_part_spec((tm, D_MODEL), s, n) for s, n in zip(starts, counts)],
        out_specs=pl.BlockSpec(memory_space=pl.ANY),
        scratch_shapes=[
            pltpu.VMEM((2, tm, D_MODEL), F32),
            pltpu.VMEM((blk, D_MODEL), F32),
            pltpu.SemaphoreType.DMA((3,)),
        ],
    )
    return pl.pallas_call(
        functools.partial(_dispatch_kernel, tile_starts=tuple(starts)),
        grid_spec=grid_spec,
        out_shape=jax.ShapeDtypeStruct((rows, D_MODEL), F32),
        compiler_params=pltpu.CompilerParams(
            dimension_semantics=("arbitrary",), vmem_limit_bytes=VMEM_LIMIT),
        name="dispatch",
    )(pad_end, padded, dest_t, *hns)


def _moe_kernel(be_ref, nu_ref, xs_ref, wg_ref, wu_ref, wd_ref, yb_ref, wg_s, wu_s, wd_s):
    i = pl.program_id(0)

    @pl.when(i < nu_ref[0])
    def _():
        @pl.when((i == 0) | (be_ref[i] != be_ref[jnp.maximum(i - 1, 0)]))
        def _():
            wg_s[...] = wg_ref[0].astype(BF16)
            wu_s[...] = wu_ref[0].astype(BF16)
            wd_s[...] = wd_ref[0].astype(BF16)

        xe = xs_ref[...].astype(BF16)
        g = _mm(xe, wg_s[...])
        u = _mm(xe, wu_s[...])
        hmid = ((g * (1.0 / (1.0 + jnp.exp(-g)))) * u).astype(BF16)
        yb_ref[...] = _mm(hmid, wd_s[...])

    @pl.when(i >= nu_ref[0])
    def _():
        yb_ref[...] = jnp.zeros_like(yb_ref)


def _moe(block_e, n_used, xs, w_gate, w_up, w_down, blk):
    n_blocks = block_e.shape[0]
    in_blk = lambda i, be, nu: (jnp.maximum(jnp.minimum(i, nu[0] - 1), 0), 0)
    grid_spec = pltpu.PrefetchScalarGridSpec(
        num_scalar_prefetch=2,
        grid=(n_blocks,),
        in_specs=[
            pl.BlockSpec((blk, D_MODEL), in_blk),
            pl.BlockSpec((1, D_MODEL, D_EXPERT), lambda i, be, nu: (be[i], 0, 0)),
            pl.BlockSpec((1, D_MODEL, D_EXPERT), lambda i, be, nu: (be[i], 0, 0)),
            pl.BlockSpec((1, D_EXPERT, D_MODEL), lambda i, be, nu: (be[i], 0, 0)),
        ],
        out_specs=pl.BlockSpec((blk, D_MODEL), lambda i, be, nu: (i, 0)),
        scratch_shapes=[
            pltpu.VMEM((D_MODEL, D_EXPERT), BF16),
            pltpu.VMEM((D_MODEL, D_EXPERT), BF16),
            pltpu.VMEM((D_EXPERT, D_MODEL), BF16),
        ],
    )
    return pl.pallas_call(
        _moe_kernel,
        grid_spec=grid_spec,
        out_shape=jax.ShapeDtypeStruct(xs.shape, F32),
        compiler_params=pltpu.CompilerParams(
            dimension_semantics=("arbitrary",), vmem_limit_bytes=VMEM_LIMIT),
        name="moe",
    )(block_e, n_used, xs, w_gate, w_up, w_down)


def _combine_kernel(dest_ref, dest_next_ref, *rest, tile_starts):
    n_parts = len(tile_starts)
    x2_refs, rt_refs = rest[:n_parts], rest[n_parts:2 * n_parts]
    yb_hbm = rest[2 * n_parts]
    o_refs = rest[2 * n_parts + 1:3 * n_parts + 1]
    buf, sem = rest[3 * n_parts + 1:]
    tm = x2_refs[0].shape[0]
    i = pl.program_id(0)
    slot = lax.rem(i, 2)

    def gather(d_ref, s):
        for k in range(2):
            for r in range(tm):
                pltpu.make_async_copy(yb_hbm.at[pl.ds(d_ref[0, 0, k * tm + r], 1), :],
                                      buf.at[s, k, pl.ds(r, 1), :],
                                      sem.at[s]).start(priority=r % 2)

    @pl.when(i == 0)
    def _():
        gather(dest_ref, 0)

    @pl.when(i + 1 < pl.num_programs(0))
    def _():
        gather(dest_next_ref, 1 - slot)

    for k in range(2):
        pltpu.make_async_copy(yb_hbm.at[pl.ds(0, tm), :], buf.at[slot, k], sem.at[slot]).wait()
    rt = _select_part(i, tile_starts, rt_refs)
    out = (_select_part(i, tile_starts, x2_refs) + rt[:, 2:3] * buf[slot, 0]
           + rt[:, 3:4] * buf[slot, 1])
    ends = tile_starts[1:] + (pl.num_programs(0),)
    for start, end, o_ref in zip(tile_starts, ends, o_refs):
        @pl.when((i >= start) & (i < end))
        def _():
            o_ref[...] = out


def _combine(dest_t, x2s, rts, yb, tm):
    counts, starts = _tile_layout(x2s, tm)
    nt = sum(counts)
    spec = lambda w: [_part_spec((tm, w), s, n) for s, n in zip(starts, counts)]
    return pl.pallas_call(
        functools.partial(_combine_kernel, tile_starts=tuple(starts)),
        grid=(nt,),
        in_specs=[
            pl.BlockSpec((1, 1, 2 * tm), lambda i: (i, 0, 0), memory_space=pltpu.SMEM),
            pl.BlockSpec((1, 1, 2 * tm), lambda i: (jnp.minimum(i + 1, nt - 1), 0, 0),
                         memory_space=pltpu.SMEM),
        ] + spec(D_MODEL) + spec(LANES) + [pl.BlockSpec(memory_space=pl.ANY)],
        out_specs=spec(D_MODEL),
        out_shape=[jax.ShapeDtypeStruct(x2.shape, F32) for x2 in x2s],
        scratch_shapes=[
            pltpu.VMEM((2, 2, tm, D_MODEL), F32),
            pltpu.SemaphoreType.DMA((2,)),
        ],
        compiler_params=pltpu.CompilerParams(
            dimension_semantics=("arbitrary",), vmem_limit_bytes=VMEM_LIMIT),
        name="combine",
    )(dest_t, dest_t, *x2s, *rts, yb)


def _hier_moe(parts, cnt, w_gate, w_up, w_down, tm, blk):
    counts = cnt[0, ROUTER_COL0:ROUTER_COL0 + N_EXPERTS].astype(jnp.int32)
    padded = (counts + blk - 1) // blk * blk
    pad_end = jnp.cumsum(padded)
    pad_start = pad_end - padded
    t_all = sum(p[0].shape[0] for p in parts)
    n_blocks = (2 * t_all + N_EXPERTS * (blk - 1)) // blk + 1
    rows = n_blocks * blk
    blk_row0 = jnp.arange(n_blocks, dtype=jnp.int32) * blk
    block_e = jnp.minimum(
        jnp.sum((pad_end[None, :] <= blk_row0[:, None]).astype(jnp.int32), axis=1),
        N_EXPERTS - 1)
    n_used = (pad_end[-1] // blk).astype(jnp.int32).reshape(1)
    experts = jnp.arange(N_EXPERTS, dtype=jnp.int32)[:, None, None]

    dests = []
    for x2, _, _, rtt in parts:
        T = x2.shape[0]
        flat = lambda a: jnp.swapaxes(a, 0, 1).reshape(a.shape[1], T)
        eid = flat(rtt[:, 0:2, :]).astype(jnp.int32)
        rank = flat(rtt[:, 4:6, :]).astype(jnp.int32)
        dest = rank + jnp.sum(
            jnp.where(eid[None] == experts, pad_start[:, None, None], 0), axis=0)
        nt = T // tm
        dests.append(dest.reshape(2, nt, tm).transpose(1, 0, 2).reshape(nt, 1, 2 * tm))
    dest_t = jnp.concatenate(dests, axis=0)
    xs = _dispatch(pad_end, padded, dest_t, [p[1] for p in parts], rows, tm, blk)
    yb = _moe(block_e, n_used, xs, w_gate, w_up, w_down, blk)
    return _combine(dest_t, [p[0] for p in parts], [p[2] for p in parts], yb, tm)


def _rope_table(pos):
    half = ROPE_DIM // 2
    inv = ROPE_THETA ** (-jnp.arange(0, ROPE_DIM, 2, dtype=F32) / ROPE_DIM)
    ang = pos.astype(F32)[:, None] * inv[None, :]
    cos, sin = jnp.cos(ang), jnp.sin(ang)
    L = pos.shape[0]
    pad = jnp.zeros((L, HEAD_DIM - ROPE_DIM), F32)
    zero = jnp.zeros((L, half), F32)
    c64 = jnp.concatenate([cos, cos, pad + 1.0], axis=1)
    lo64 = jnp.concatenate([-sin, zero, pad], axis=1)
    hi64 = jnp.concatenate([zero, sin, pad], axis=1)
    two = lambda t: jnp.concatenate([t, t], axis=1)
    return jnp.concatenate([two(c64), two(lo64), two(hi64)], axis=1)


def _mixers(x, pos_rope, kctx_prev, vctx_prev, h0r, h0i, mk, mv, wp, sp, cnt0, *,
            tm_in, tq, ssm_l, tm_mid):
    B, S, _ = x.shape
    T = B * S
    q, k3, v3, u_tm = _in_proj(x, wp["gmix"], wp["win"], wp["gq"], wp["gk"], pos_rope, tm_in)
    if kctx_prev is None:
        kctx, vctx = k3, v3
    else:
        kctx = jnp.concatenate([kctx_prev, k3], axis=1)
        vctx = jnp.concatenate([vctx_prev, v3], axis=1)
    att = _swa(wp["sink"], q, kctx, vctx, tq, mask_context=kctx_prev is None)
    ssm_tm, hr, hi = _ssm(u_tm.reshape(S, B, SSM_WIDTH), h0r, h0i, sp, ssm_l)
    x2, hn, rt, rtt, cnt = _mid(x, att, ssm_tm.reshape(S, B * SSM_WIDTH), mk, mv, wp, cnt0,
                                *tm_mid)
    part = (x2.reshape(T, D_MODEL), hn.reshape(T, D_MODEL), rt.reshape(T, LANES), rtt)
    return part, cnt, k3, v3, hr, hi


def kernel(x_prompt, x_sample, cache_attn_k, cache_attn_v, state_ssm_re, state_ssm_im, cache_mem_k, cache_mem_v, mem_prompt, norm_mix, w_in, q_norm, k_norm, attn_sink, ssm_lambda_re, ssm_lambda_im, ssm_log_dt, ssm_b_re, ssm_b_im, ssm_c_re, ssm_c_im, ssm_d, ssm_w_glu, ssm_b_glu, norm_attn_out, norm_ssm_out, w_out, norm_cross, norm_mem, w_cq, w_ck, w_cv, cq_norm, ck_norm, w_co, norm_ffn, w_router_group, b_router_group, w_router_expert, b_router_expert, w_e_gate, w_e_up, w_e_down):
    depth = norm_mix.shape[0]
    Bp, Lp, _ = x_prompt.shape
    Bs, Ls, _ = x_sample.shape
    yp, ys = x_prompt, x_sample
    rope_p = _rope_table(jnp.arange(Lp, dtype=jnp.int32))
    rope_s = _rope_table(PAST_LEN + jnp.arange(Ls, dtype=jnp.int32))
    outs = [[] for _ in range(10)]
    n_router = N_EXPERT_GROUPS + N_EXPERTS
    for l in range(depth):
        row = lambda a: a[l].astype(F32).reshape(1, -1)
        w_r = jnp.pad(jnp.concatenate([w_router_group[l], w_router_expert[l]], axis=1).astype(F32),
                      ((0, 0), (0, LANES - n_router)))
        w_r_hi = w_r.astype(BF16)
        w_r_lo = (w_r - w_r_hi.astype(F32)).astype(BF16)
        b_r = jnp.pad(jnp.concatenate([b_router_group[l], b_router_expert[l]]).astype(F32),
                      (0, LANES - n_router)).reshape(1, LANES)
        wp = {
            "gmix": row(norm_mix), "win": w_in[l].astype(BF16),
            "gq": jnp.tile(row(q_norm), (1, LANES // HEAD_DIM)),
            "gk": jnp.tile(row(k_norm), (1, LANES // HEAD_DIM)),
            "sink": attn_sink[l].astype(F32),
            "gao": row(norm_attn_out), "gso": row(norm_ssm_out),
            "wout": w_out[l].astype(BF16), "gx": row(norm_cross),
            "wcq": w_cq[l].astype(BF16), "gcq": row(cq_norm),
            "wco": w_co[l].astype(BF16), "gffn": row(norm_ffn),
            "wr": jnp.concatenate([w_r_hi, w_r_lo], axis=1), "br": b_r,
        }
        sp = _ssm_params(ssm_lambda_re[l], ssm_lambda_im[l], ssm_log_dt[l], ssm_b_re[l],
                         ssm_b_im[l], ssm_c_re[l], ssm_c_im[l], ssm_d[l], ssm_w_glu[l],
                         ssm_b_glu[l])
        ew = (w_e_gate[l].astype(F32), w_e_up[l].astype(F32), w_e_down[l].astype(F32))

        w_ckv = jnp.concatenate([w_ck[l], w_cv[l]], axis=1).astype(BF16)
        mkp, mvp = _memkv(mem_prompt.reshape(Bp * N_MEM, D_MODEL), row(norm_mem), w_ckv,
                          row(ck_norm), 512)
        mkp = mkp.reshape(Bp, N_MEM, CA_WIDTH)
        mvp = mvp.reshape(Bp, N_MEM, CA_WIDTH)

        zst = jnp.zeros((Bp, SSM_COLS), F32)
        part_p, cnt_p, kp, vp, hpr, hpi = _mixers(
            yp, rope_p, None, None, zst, zst, mkp, mvp, wp, sp, jnp.zeros((1, LANES), F32),
            tm_in=512, tq=256, ssm_l=64, tm_mid=(1, 512))
        part_s, cnt_s, kn, vn, hsr, hsi = _mixers(
            ys, rope_s, cache_attn_k[l].reshape(Bs, WINDOW, KV_WIDTH).astype(F32),
            cache_attn_v[l].reshape(Bs, WINDOW, KV_WIDTH).astype(F32),
            state_ssm_re[l].astype(F32).reshape(Bs, SSM_COLS),
            state_ssm_im[l].astype(F32).reshape(Bs, SSM_COLS),
            cache_mem_k[l].astype(F32).reshape(Bs, N_MEM, CA_WIDTH),
            cache_mem_v[l].astype(F32).reshape(Bs, N_MEM, CA_WIDTH), wp, sp, cnt_p,
            tm_in=Ls, tq=CHUNK, ssm_l=Ls, tm_mid=(8, Ls))
        yp, ys = _hier_moe([part_p, part_s], cnt_s, *ew, 256, 2 * MOE_BLOCK)
        yp = yp.reshape(Bp, Lp, D_MODEL)
        ys = ys.reshape(Bs, Ls, D_MODEL)

        sg = (N_SSM_GROUPS, SSM_STATE)
        kvs = (N_KV_HEADS, HEAD_DIM)
        vals = (kp[:, Lp - WINDOW:].reshape(Bp, WINDOW, *kvs),
                vp[:, Lp - WINDOW:].reshape(Bp, WINDOW, *kvs),
                hpr.reshape(Bp, *sg), hpi.reshape(Bp, *sg),
                mkp.reshape(Bp, N_MEM, CA_HEADS, CA_HEAD_DIM),
                mvp.reshape(Bp, N_MEM, CA_HEADS, CA_HEAD_DIM),
                kn.reshape(Bs, Ls, *kvs), vn.reshape(Bs, Ls, *kvs),
                hsr.reshape(Bs, *sg), hsi.reshape(Bs, *sg))
        for lst, val in zip(outs, vals):
            lst.append(val)
    return (yp, ys) + tuple(jnp.stack(lst) for lst in outs)
```

```python
import functools
import math

import jax
import jax.numpy as jnp
from jax import lax
from jax.experimental import pallas as pl
from jax.experimental.pallas import tpu as pltpu

F32 = jnp.float32
BF16 = jnp.bfloat16

D_MODEL = 1024
CHUNK = 64
N_Q_HEADS = 8
N_KV_HEADS = 2
GQA = N_Q_HEADS // N_KV_HEADS
HEAD_DIM = 64
WINDOW = 128
BAND = WINDOW + CHUNK
ROPE_DIM = HEAD_DIM // 4
ROPE_THETA = 500000.0
ATT_WIDTH = N_Q_HEADS * HEAD_DIM
KV_WIDTH = N_KV_HEADS * HEAD_DIM
SSM_GROUP = 16
SSM_WIDTH = D_MODEL // 2
N_SSM_GROUPS = SSM_WIDTH // SSM_GROUP
SSM_STATE = 64
SSM_COLS = N_SSM_GROUPS * SSM_STATE
IN_WIDTH = ATT_WIDTH + 2 * KV_WIDTH + SSM_WIDTH
N_MEM = 256
CA_HEADS = 4
CA_HEAD_DIM = 128
CA_WIDTH = CA_HEADS * CA_HEAD_DIM
N_EXPERT_GROUPS = 4
EXPERTS_PER_GROUP = 8
N_EXPERTS = N_EXPERT_GROUPS * EXPERTS_PER_GROUP
D_EXPERT = 512
MOE_BLOCK = 256
EPS = 1e-6
NEG = -1e30
PAST_LEN = 4096

LANES = 128
ROUTER_COL0 = N_EXPERT_GROUPS
VMEM_LIMIT = 48 * 1024 * 1024


def _rms(x, g):
    ms = jnp.mean(x * x, axis=-1, keepdims=True)
    return (x * lax.rsqrt(ms + EPS)) * g


def _mm(a, b):
    return jnp.dot(a, b, preferred_element_type=F32)


def _in_proj_kernel(x_ref, g_ref, w_ref, gq_ref, gk_ref, rope_ref,
                    q_ref, k_ref, v_ref, u_ref):
    tm = x_ref.shape[1]
    h = _rms(x_ref[0], g_ref[...])
    hin = _mm(h.astype(BF16), w_ref[...])
    rope = rope_ref[...]
    cos = rope[:, 0:LANES]
    sin_lo = rope[:, LANES:2 * LANES]
    sin_hi = rope[:, 2 * LANES:3 * LANES]
    lane = lax.broadcasted_iota(jnp.int32, (tm, LANES), 1)
    left = lane < HEAD_DIM

    def norm_rope(z, g):
        sq = z * z
        lsum = jnp.sum(jnp.where(left, sq, 0.0), axis=-1, keepdims=True)
        rsum = jnp.sum(jnp.where(left, 0.0, sq), axis=-1, keepdims=True)
        ms = jnp.where(left, lsum, rsum) * (1.0 / HEAD_DIM)
        zn = (z * lax.rsqrt(ms + EPS)) * g
        half = ROPE_DIM // 2
        return (zn * cos + pltpu.roll(zn, LANES - half, 1) * sin_lo
                + pltpu.roll(zn, half, 1) * sin_hi)

    for j in range(ATT_WIDTH // LANES):
        sl = slice(j * LANES, (j + 1) * LANES)
        q_ref[0, :, sl] = norm_rope(hin[:, sl], gq_ref[...])
    k_ref[0] = norm_rope(hin[:, ATT_WIDTH:ATT_WIDTH + KV_WIDTH], gk_ref[...])
    v_ref[0] = hin[:, ATT_WIDTH + KV_WIDTH:ATT_WIDTH + 2 * KV_WIDTH]
    u_ref[...] = hin[:, ATT_WIDTH + 2 * KV_WIDTH:]


def _in_proj(x, g, w_bf, gq, gk, rope, tm):
    B, S, _ = x.shape
    full = lambda b, i: (0, 0)
    tile = lambda w: pl.BlockSpec((1, tm, w), lambda b, i: (b, i, 0))
    return pl.pallas_call(
        _in_proj_kernel,
        grid=(B, S // tm),
        in_specs=[
            tile(D_MODEL),
            pl.BlockSpec((1, D_MODEL), full),
            pl.BlockSpec((D_MODEL, IN_WIDTH), full),
            pl.BlockSpec((1, LANES), full),
            pl.BlockSpec((1, LANES), full),
            pl.BlockSpec((tm, 3 * LANES), lambda b, i: (i, 0)),
        ],
        out_specs=[
            tile(ATT_WIDTH), tile(KV_WIDTH), tile(KV_WIDTH),
            pl.BlockSpec((tm, SSM_WIDTH), lambda b, i: (i, b)),
        ],
        out_shape=[
            jax.ShapeDtypeStruct((B, S, ATT_WIDTH), F32),
            jax.ShapeDtypeStruct((B, S, KV_WIDTH), F32),
            jax.ShapeDtypeStruct((B, S, KV_WIDTH), F32),
            jax.ShapeDtypeStruct((S, B * SSM_WIDTH), F32),
        ],
        compiler_params=pltpu.CompilerParams(
            dimension_semantics=("arbitrary", "arbitrary"),
            vmem_limit_bytes=VMEM_LIMIT),
        name="in_proj",
    )(x, g, w_bf, gq, gk, rope)


def _swa_kernel(sink_ref, q_ref, k_ref, v_ref, o_ref, *, mask_context):
    tq = q_ref.shape[1]
    i = pl.program_id(1)
    nch = tq // CHUNK
    lane = lax.broadcasted_iota(jnp.int32, (BAND, LANES), 1)
    lo_half = lane < HEAD_DIM
    slabs_per_kv = GQA * HEAD_DIM // LANES

    units = []
    scores = []
    vpads = {}
    valids = {}
    for c in range(nch):
        chunk = i * nch + c
        if mask_context:
            first = jnp.maximum(chunk - WINDOW // CHUNK, 0)
            start = pl.multiple_of(first * CHUNK, CHUNK)
            kidx = start + lax.broadcasted_iota(jnp.int32, (1, BAND), 1)
            valids[c] = kidx < (chunk + 1) * CHUNK
        else:
            start = pl.multiple_of(chunk * CHUNK, CHUNK)
        kb = k_ref[0, pl.ds(start, BAND), :]
        vb = v_ref[0, pl.ds(start, BAND), :]
        kb_sw = pltpu.roll(kb, HEAD_DIM, 1)
        vb_sw = pltpu.roll(vb, HEAD_DIM, 1)
        for kvh in range(N_KV_HEADS):
            k_own, k_oth = (kb, kb_sw) if kvh == 0 else (kb_sw, kb)
            v_own, v_oth = (vb, vb_sw) if kvh == 0 else (vb_sw, vb)
            kpad = (jnp.where(lo_half, k_own, 0.0).astype(BF16),
                    jnp.where(lo_half, 0.0, k_oth).astype(BF16))
            vpads[(c, kvh)] = (jnp.where(lo_half, v_own, 0.0).astype(BF16),
                               jnp.where(lo_half, 0.0, v_oth).astype(BF16))
            col0 = kvh * GQA * HEAD_DIM
            q2 = jnp.concatenate(
                [q_ref[0, c * CHUNK:(c + 1) * CHUNK, col0 + m * LANES:col0 + (m + 1) * LANES]
                 for m in range(slabs_per_kv)], axis=0).astype(BF16)
            for side in range(2):
                s = lax.dot_general(q2, kpad[side], (((1,), (1,)), ((), ())),
                                    preferred_element_type=F32) * (HEAD_DIM ** -0.5)
                if mask_context:
                    s = jnp.where(valids[c], s, NEG)
                units.append((c, kvh, side))
                scores.append(s)

    s_all = jnp.concatenate(scores, axis=0)
    sk = jnp.concatenate(
        [jnp.full((CHUNK, 1), sink_ref[kvh * GQA + 2 * m + side], F32)
         for (_, kvh, side) in units for m in range(slabs_per_kv)], axis=0)
    mx = jnp.maximum(jnp.max(s_all, axis=-1, keepdims=True), sk)
    p_all = jnp.exp(s_all - mx)
    den = jnp.sum(p_all, axis=-1, keepdims=True) + jnp.exp(sk - mx)
    p_all = (p_all / den).astype(BF16)
    rows_u = slabs_per_kv * CHUNK
    probs = [p_all[n * rows_u:(n + 1) * rows_u] for n in range(len(units))]

    for n in range(0, len(units), 2):
        c, kvh, _ = units[n]
        vp = vpads[(c, kvh)]
        o = _mm(probs[n], vp[0]) + _mm(probs[n + 1], vp[1])
        col0 = kvh * GQA * HEAD_DIM
        for m in range(slabs_per_kv):
            o_ref[0, c * CHUNK:(c + 1) * CHUNK, col0 + m * LANES:col0 + (m + 1) * LANES] = (
                o[m * CHUNK:(m + 1) * CHUNK])


def _swa(sink, q, kctx, vctx, tq, mask_context):
    B, Sq, _ = q.shape
    Sk = kctx.shape[1]
    return pl.pallas_call(
        functools.partial(_swa_kernel, mask_context=mask_context),
        grid=(B, Sq // tq),
        in_specs=[
            pl.BlockSpec(memory_space=pltpu.SMEM),
            pl.BlockSpec((1, tq, ATT_WIDTH), lambda b, i: (b, i, 0)),
            pl.BlockSpec((1, Sk, KV_WIDTH), lambda b, i: (b, 0, 0)),
            pl.BlockSpec((1, Sk, KV_WIDTH), lambda b, i: (b, 0, 0)),
        ],
        out_specs=pl.BlockSpec((1, tq, ATT_WIDTH), lambda b, i: (b, i, 0)),
        out_shape=jax.ShapeDtypeStruct((B, Sq, ATT_WIDTH), F32),
        compiler_params=pltpu.CompilerParams(
            dimension_semantics=("arbitrary", "arbitrary"),
            vmem_limit_bytes=VMEM_LIMIT),
        name="swa",
    )(sink, q, kctx, vctx)


def _ssm_kernel(u_ref, h0r_ref, h0i_ref, lam_ref, bre_ref, bim_ref, cre_ref, cim_ref,
                d_ref, wglu_ref, bglu_ref,
                y_ref, hr_out, hi_out, sr, si, hr_s, hi_s):
    L, B, _ = u_ref.shape
    rows = L * B
    half_w = SSM_WIDTH // 2
    half_c = SSM_COLS // 2

    @pl.when(pl.program_id(0) == 0)
    def _():
        hr_s[...] = h0r_ref[...]
        hi_s[...] = h0i_ref[...]

    u = u_ref[...].reshape(rows, SSM_WIDTH)
    ub = u.astype(BF16)
    for hf in range(2):
        uh = ub[:, hf * half_w:(hf + 1) * half_w]
        sr[:, hf * half_c:(hf + 1) * half_c] = _mm(uh, bre_ref[hf])
        si[:, hf * half_c:(hf + 1) * half_c] = _mm(uh, bim_ref[hf])

    cw = 4 * LANES
    for cc in range(SSM_COLS // cw):
        cols = slice(cc * cw, (cc + 1) * cw)
        lr = jnp.broadcast_to(lam_ref[0:1, cols], (B, cw))
        li = jnp.broadcast_to(lam_ref[1:2, cols], (B, cw))

        def body(t, carry):
            hr, hi = carry
            at_t = pl.ds(pl.multiple_of(t * B, B), B)
            nr = lr * hr - li * hi + sr[at_t, cols]
            ni = lr * hi + li * hr + si[at_t, cols]
            sr[at_t, cols] = nr
            si[at_t, cols] = ni
            return nr, ni

        hr, hi = lax.fori_loop(0, L, body, (hr_s[:, cols], hi_s[:, cols]), unroll=2)
        hr_s[:, cols] = hr
        hi_s[:, cols] = hi

    ys = []
    for hf in range(2):
        cs = slice(hf * half_c, (hf + 1) * half_c)
        ys.append(_mm(sr[:, cs].astype(BF16), cre_ref[hf])
                  + _mm(si[:, cs].astype(BF16), cim_ref[hf]))
    y = jnp.concatenate(ys, axis=1) + d_ref[...] * u
    g = 0.5 * y * (1.0 + jnp.tanh(math.sqrt(2.0 / math.pi) * (y + 0.044715 * (y * y * y))))
    gb = g.astype(BF16)
    z = jnp.concatenate(
        [_mm(gb[:, hf * half_w:(hf + 1) * half_w], wglu_ref[hf]) for hf in range(2)],
        axis=1) + bglu_ref[...]
    out = g * (1.0 / (1.0 + jnp.exp(-z)))
    y_ref[...] = out.reshape(L, B, SSM_WIDTH)
    hr_out[...] = hr_s[...]
    hi_out[...] = hi_s[...]


def _ssm(u, h0r, h0i, sp, L):
    S, B, _ = u.shape
    c2 = lambda i: (0, 0)
    c3 = lambda i: (0, 0, 0)
    return pl.pallas_call(
        _ssm_kernel,
        grid=(S // L,),
        in_specs=[
            pl.BlockSpec((L, B, SSM_WIDTH), lambda i: (i, 0, 0)),
            pl.BlockSpec((B, SSM_COLS), c2),
            pl.BlockSpec((B, SSM_COLS), c2),
            pl.BlockSpec((2, SSM_COLS), c2),
            pl.BlockSpec((2, SSM_WIDTH // 2, SSM_COLS // 2), c3),
            pl.BlockSpec((2, SSM_WIDTH // 2, SSM_COLS // 2), c3),
            pl.BlockSpec((2, SSM_COLS // 2, SSM_WIDTH // 2), c3),
            pl.BlockSpec((2, SSM_COLS // 2, SSM_WIDTH // 2), c3),
            pl.BlockSpec((1, SSM_WIDTH), c2),
            pl.BlockSpec((2, SSM_WIDTH // 2, SSM_WIDTH // 2), c3),
            pl.BlockSpec((1, SSM_WIDTH), c2),
        ],
        out_specs=[
            pl.BlockSpec((L, B, SSM_WIDTH), lambda i: (i, 0, 0)),
            pl.BlockSpec((B, SSM_COLS), c2),
            pl.BlockSpec((B, SSM_COLS), c2),
        ],
        out_shape=[
            jax.ShapeDtypeStruct((S, B, SSM_WIDTH), F32),
            jax.ShapeDtypeStruct((B, SSM_COLS), F32),
            jax.ShapeDtypeStruct((B, SSM_COLS), F32),
        ],
        scratch_shapes=[
            pltpu.VMEM((L * B, SSM_COLS), F32),
            pltpu.VMEM((L * B, SSM_COLS), F32),
            pltpu.VMEM((B, SSM_COLS), F32),
            pltpu.VMEM((B, SSM_COLS), F32),
        ],
        compiler_params=pltpu.CompilerParams(
            dimension_semantics=("arbitrary",), vmem_limit_bytes=VMEM_LIMIT),
        name="ssm",
    )(u, h0r, h0i, sp["lam"], sp["bre"], sp["bim"], sp["cre"], sp["cim"],
      sp["d"], sp["wglu"], sp["bglu"])


def _block_diag(blocks):
    G, r, c = blocks.shape
    eye = jnp.eye(G, dtype=blocks.dtype)
    return jnp.einsum("grc,gh->grhc", blocks, eye).reshape(G * r, G * c)


def _ssm_params(lam_re, lam_im, log_dt, b_re, b_im, c_re, c_im, d, w_glu, b_glu):
    lam = lax.complex(lam_re.astype(F32), lam_im.astype(F32))
    dt = jnp.exp(log_dt.astype(F32))[:, None]
    lam_bar = jnp.exp(lam * dt)
    bmat = lax.complex(b_re.astype(F32), b_im.astype(F32))
    b_bar = ((lam_bar - 1.0) / lam)[..., None] * bmat
    lam2 = jnp.stack([lam_bar.real.reshape(-1), lam_bar.imag.reshape(-1)])
    bt = jnp.swapaxes(b_bar, 1, 2)
    hw, hc = SSM_WIDTH // 2, SSM_COLS // 2
    split_b = lambda m: jnp.stack([m[:hw, :hc], m[hw:, hc:]]).astype(BF16)
    split_c = lambda m: jnp.stack([m[:hc, :hw], m[hc:, hw:]]).astype(BF16)
    ct_re = jnp.swapaxes(c_re.astype(F32), 1, 2)
    ct_im = jnp.swapaxes(c_im.astype(F32), 1, 2)
    wg = _block_diag(w_glu.astype(F32))
    return {
        "lam": lam2,
        "bre": split_b(_block_diag(bt.real)),
        "bim": split_b(_block_diag(bt.imag)),
        "cre": split_c(_block_diag(ct_re)),
        "cim": split_c(_block_diag(-ct_im)),
        "d": d.astype(F32).reshape(1, SSM_WIDTH),
        "wglu": jnp.stack([wg[:hw, :hw], wg[hw:, hw:]]).astype(BF16),
        "bglu": b_glu.astype(F32).reshape(1, SSM_WIDTH),
    }


def _memkv_kernel(m_ref, g_ref, w_ref, gk_ref, k_ref, v_ref):
    m = _rms(m_ref[...], g_ref[...])
    kv = _mm(m.astype(BF16), w_ref[...])
    for h in range(CA_HEADS):
        sl = slice(h * CA_HEAD_DIM, (h + 1) * CA_HEAD_DIM)
        k_ref[:, sl] = _rms(kv[:, sl], gk_ref[...])
    v_ref[...] = kv[:, CA_WIDTH:]


def _memkv(mem2d, g, w_bf, gk, tm):
    T = mem2d.shape[0]
    full = lambda i: (0, 0)
    return pl.pallas_call(
        _memkv_kernel,
        grid=(T // tm,),
        in_specs=[
            pl.BlockSpec((tm, D_MODEL), lambda i: (i, 0)),
            pl.BlockSpec((1, D_MODEL), full),
            pl.BlockSpec((D_MODEL, 2 * CA_WIDTH), full),
            pl.BlockSpec((1, CA_HEAD_DIM), full),
        ],
        out_specs=[
            pl.BlockSpec((tm, CA_WIDTH), lambda i: (i, 0)),
            pl.BlockSpec((tm, CA_WIDTH), lambda i: (i, 0)),
        ],
        out_shape=[
            jax.ShapeDtypeStruct((T, CA_WIDTH), F32),
            jax.ShapeDtypeStruct((T, CA_WIDTH), F32),
        ],
        compiler_params=pltpu.CompilerParams(
            dimension_semantics=("arbitrary",), vmem_limit_bytes=VMEM_LIMIT),
        name="memkv",
    )(mem2d, g, w_bf, gk)


def _mid_kernel(x_ref, att_ref, ssm_ref, mk_ref, mv_ref,
                gao_ref, gso_ref, wout_ref, gx_ref, wcq_ref, gcq_ref, wco_ref,
                gffn_ref, wr_ref, br_ref, cnt0_ref,
                x2_ref, hn_ref, rt_ref, rtt_ref, cnt_ref, base_s):
    nb, ts, _ = x_ref.shape
    tm = nb * ts

    @pl.when((pl.program_id(0) == 0) & (pl.program_id(1) == 0))
    def _():
        base_s[...] = cnt0_ref[...]

    ssm = jnp.concatenate(
        [ssm_ref[:, b * SSM_WIDTH:(b + 1) * SSM_WIDTH] for b in range(nb)], axis=0)
    a = _rms(att_ref[...].reshape(tm, ATT_WIDTH), gao_ref[...]).astype(BF16)
    s = _rms(ssm, gso_ref[...]).astype(BF16)
    x1 = (x_ref[...].reshape(tm, D_MODEL) + _mm(a, wout_ref[0:ATT_WIDTH, :])
          + _mm(s, wout_ref[ATT_WIDTH:, :]))

    qx = _mm(_rms(x1, gx_ref[...]).astype(BF16), wcq_ref[...])
    heads = []
    for h in range(CA_HEADS):
        sl = slice(h * CA_HEAD_DIM, (h + 1) * CA_HEAD_DIM)
        qh = _rms(qx[:, sl], gcq_ref[...]).astype(BF16)
        per_batch = []
        for b in range(nb):
            kh = mk_ref[b, :, sl].astype(BF16)
            vh = mv_ref[b, :, sl].astype(BF16)
            sc = lax.dot_general(qh[b * ts:(b + 1) * ts], kh, (((1,), (1,)), ((), ())),
                                 preferred_element_type=F32) * (CA_HEAD_DIM ** -0.5)
            p = jnp.exp(sc - jnp.max(sc, axis=-1, keepdims=True))
            p = p / jnp.sum(p, axis=-1, keepdims=True)
            per_batch.append(_mm(p.astype(BF16), vh))
        heads.append(jnp.concatenate(per_batch, axis=0))
    o = jnp.concatenate(heads, axis=1).astype(BF16)
    x2 = x1 + _mm(o, wco_ref[...])
    x2_ref[...] = x2.reshape(nb, ts, D_MODEL)

    hn = _rms(x2, gffn_ref[...])
    hn_ref[...] = hn.reshape(nb, ts, D_MODEL)

    h_hi = hn.astype(BF16)
    h_lo = (hn - h_hi.astype(F32)).astype(BF16)
    r1 = _mm(h_hi, wr_ref[...])
    lg = (r1[:, :LANES] + r1[:, LANES:] + _mm(h_lo, wr_ref[:, 0:LANES])
          + br_ref[...])

    col = lax.broadcasted_iota(jnp.int32, (tm, LANES), 1)
    big = jnp.int32(4 * LANES)
    gmask = col < N_EXPERT_GROUPS
    lgg = jnp.where(gmask, lg, NEG)
    mg = jnp.max(lgg, axis=-1, keepdims=True)
    grp = jnp.min(jnp.where(gmask & (lgg == mg), col, big), axis=-1, keepdims=True)
    pg_top = 1.0 / jnp.sum(jnp.where(gmask, jnp.exp(lgg - mg), 0.0), axis=-1, keepdims=True)

    ecol = col - ROUTER_COL0
    emask = ((ecol >= 0) & (ecol < N_EXPERTS)
             & (lax.shift_right_arithmetic(ecol, 3) == grp))
    le = jnp.where(emask, lg, NEG)
    m1 = jnp.max(le, axis=-1, keepdims=True)
    i1 = jnp.min(jnp.where(emask & (le == m1), col, big), axis=-1, keepdims=True)
    rest = emask & (col != i1)
    le2 = jnp.where(rest, lg, NEG)
    m2 = jnp.max(le2, axis=-1, keepdims=True)
    i2 = jnp.min(jnp.where(rest & (le2 == m2), col, big), axis=-1, keepdims=True)
    den = jnp.sum(jnp.where(emask, jnp.exp(le - m1), 0.0), axis=-1, keepdims=True)
    p1 = 1.0 / den
    p2 = jnp.exp(m2 - m1) / den
    gate1 = pg_top * p1 / (p1 + p2)
    gate2 = pg_top * p2 / (p1 + p2)

    sel1 = col == i1
    sel2 = col == i2
    oh = jnp.where(sel1 | sel2, 1.0, 0.0)
    r_i = lax.broadcasted_iota(jnp.int32, (tm, tm), 0)
    c_i = lax.broadcasted_iota(jnp.int32, (tm, tm), 1)
    tri = jnp.where(r_i > c_i, 1.0, 0.0).astype(BF16)
    tot = base_s[...] + _mm(tri, oh.astype(BF16))
    rank1 = jnp.sum(jnp.where(sel1, tot, 0.0), axis=-1, keepdims=True)
    rank2 = jnp.sum(jnp.where(sel2, tot, 0.0), axis=-1, keepdims=True)
    base_s[...] = base_s[...] + jnp.sum(oh, axis=0, keepdims=True)
    cnt_ref[...] = base_s[...]

    e1 = (i1 - ROUTER_COL0).astype(F32)
    e2 = (i2 - ROUTER_COL0).astype(F32)
    rt = jnp.zeros((tm, LANES), F32)
    for k, val in enumerate((e1, e2, gate1, gate2, rank1, rank2)):
        rt = jnp.where(col == k, val, rt)
    rt_ref[...] = rt.reshape(nb, ts, LANES)
    for b in range(nb):
        rtt_ref[b] = rt[b * ts:(b + 1) * ts].T[0:8, :]


def _mid(x, att, ssm_tm, mk, mv, wp, cnt0, nb, ts):
    B, S, _ = x.shape
    c2 = lambda b, i: (0, 0)
    tile = lambda w: pl.BlockSpec((nb, ts, w), lambda b, i: (b, i, 0))
    return pl.pallas_call(
        _mid_kernel,
        grid=(B // nb, S // ts),
        in_specs=[
            tile(D_MODEL), tile(ATT_WIDTH),
            pl.BlockSpec((ts, nb * SSM_WIDTH), lambda b, i: (i, b)),
            pl.BlockSpec((nb, N_MEM, CA_WIDTH), lambda b, i: (b, 0, 0)),
            pl.BlockSpec((nb, N_MEM, CA_WIDTH), lambda b, i: (b, 0, 0)),
            pl.BlockSpec((1, ATT_WIDTH), c2),
            pl.BlockSpec((1, SSM_WIDTH), c2),
            pl.BlockSpec((ATT_WIDTH + SSM_WIDTH, D_MODEL), c2),
            pl.BlockSpec((1, D_MODEL), c2),
            pl.BlockSpec((D_MODEL, CA_WIDTH), c2),
            pl.BlockSpec((1, CA_HEAD_DIM), c2),
            pl.BlockSpec((CA_WIDTH, D_MODEL), c2),
            pl.BlockSpec((1, D_MODEL), c2),
            pl.BlockSpec((D_MODEL, 2 * LANES), c2),
            pl.BlockSpec((1, LANES), c2),
            pl.BlockSpec((1, LANES), c2),
        ],
        out_specs=[
            tile(D_MODEL), tile(D_MODEL), tile(LANES),
            pl.BlockSpec((nb, 8, ts), lambda b, i: (b, 0, i)),
            pl.BlockSpec((1, LANES), c2),
        ],
        out_shape=[
            jax.ShapeDtypeStruct((B, S, D_MODEL), F32),
            jax.ShapeDtypeStruct((B, S, D_MODEL), F32),
            jax.ShapeDtypeStruct((B, S, LANES), F32),
            jax.ShapeDtypeStruct((B, 8, S), F32),
            jax.ShapeDtypeStruct((1, LANES), F32),
        ],
        scratch_shapes=[pltpu.VMEM((1, LANES), F32)],
        compiler_params=pltpu.CompilerParams(
            dimension_semantics=("arbitrary", "arbitrary"),
            vmem_limit_bytes=VMEM_LIMIT),
        name="mid",
    )(x, att, ssm_tm, mk, mv, wp["gao"], wp["gso"], wp["wout"], wp["gx"], wp["wcq"],
      wp["gcq"], wp["wco"], wp["gffn"], wp["wr"], wp["br"], cnt0)


def _select_part(i, tile_starts, refs):
    x = refs[0][...]
    for start, ref in zip(tile_starts[1:], refs[1:]):
        x = jnp.where(i >= start, ref[...], x)
    return x


def _part_spec(shape, tile_start, n_tiles):
    def index(i, *_):
        return (jnp.clip(i - tile_start, 0, n_tiles - 1),) + (0,) * (len(shape) - 1)
    return pl.BlockSpec(shape, index)


def _dispatch_kernel(pend_ref, padded_ref, dest_ref, *rest, tile_starts):
    n_parts = len(tile_starts)
    hn_refs = rest[:n_parts]
    xs_hbm, stage, zbuf, sem = rest[n_parts:]
    tm = hn_refs[0].shape[0]
    i = pl.program_id(0)
    slot = lax.rem(i, 2)
    blk = zbuf.shape[0]

    def wait_rows(s):
        for _ in range(2):
            pltpu.make_async_copy(stage.at[s], xs_hbm.at[pl.ds(0, tm)], sem.at[s]).wait()

    @pl.when(i == 0)
    def _():
        zbuf[...] = jnp.zeros_like(zbuf)
        for e in range(N_EXPERTS):
            @pl.when(padded_ref[e] > 0)
            def _():
                row0 = pl.multiple_of(pend_ref[e] - blk, blk)
                fill = pltpu.make_async_copy(zbuf, xs_hbm.at[pl.ds(row0, blk)], sem.at[2])
                fill.start()
                fill.wait()

        def fill_tail(b, carry):
            fill = pltpu.make_async_copy(
                zbuf, xs_hbm.at[pl.ds(pl.multiple_of(b * blk, blk), blk)], sem.at[2])
            fill.start()
            fill.wait()
            return carry

        lax.fori_loop(pend_ref[N_EXPERTS - 1] // blk, xs_hbm.shape[0] // blk, fill_tail, 0)

    @pl.when(i >= 2)
    def _():
        wait_rows(slot)

    tile = _select_part(i, tile_starts, hn_refs)
    for s in range(2):
        @pl.when(slot == s)
        def _():
            stage[s] = tile
            for k in range(2):
                for r in range(tm):
                    pltpu.make_async_copy(stage.at[s, pl.ds(r, 1), :],
                                          xs_hbm.at[pl.ds(dest_ref[0, 0, k * tm + r], 1), :],
                                          sem.at[s]).start(priority=r % 2)

    @pl.when(i == pl.num_programs(0) - 1)
    def _():
        wait_rows(slot)

        @pl.when(i >= 1)
        def _():
            wait_rows(1 - slot)


def _tile_layout(arrays, tm):
    counts = [a.shape[0] // tm for a in arrays]
    starts = [sum(counts[:p]) for p in range(len(counts))]
    return counts, starts


def _dispatch(pad_end, padded, dest_t, hns, rows, tm, blk):
    counts, starts = _tile_layout(hns, tm)
    grid_spec = pltpu.PrefetchScalarGridSpec(
        num_scalar_prefetch=2,
        grid=(sum(counts),),
        in_specs=[pl.BlockSpec((1, 1, 2 * tm), lambda i, pe, pd: (i, 0, 0),
                               memory_space=pltpu.SMEM)]
        + [_part_spec((tm, D_MODEL), s, n) for s, n in zip(starts, counts)],
        out_specs=pl.BlockSpec(memory_space=pl.ANY),
        scratch_shapes=[
            pltpu.VMEM((2, tm, D_MODEL), F32),
            pltpu.VMEM((blk, D_MODEL), F32),
            pltpu.SemaphoreType.DMA((3,)),
        ],
    )
    return pl.pallas_call(
        functools.partial(_dispatch_kernel, tile_starts=tuple(starts)),
        grid_spec=grid_spec,
        out_shape=jax.ShapeDtypeStruct((rows, D_MODEL), F32),
        compiler_params=pltpu.CompilerParams(
            dimension_semantics=("arbitrary",), vmem_limit_bytes=VMEM_LIMIT),
        name="dispatch",
    )(pad_end, padded, dest_t, *hns)


def _moe_kernel(be_ref, nu_ref, xs_ref, wg_ref, wu_ref, wd_ref, yb_ref, wg_s, wu_s, wd_s):
    i = pl.program_id(0)

    @pl.when(i < nu_ref[0])
    def _():
        @pl.when((i == 0) | (be_ref[i] != be_ref[jnp.maximum(i - 1, 0)]))
        def _():
            wg_s[...] = wg_ref[0].astype(BF16)
            wu_s[...] = wu_ref[0].astype(BF16)
            wd_s[...] = wd_ref[0].astype(BF16)

        xe = xs_ref[...].astype(BF16)
        g = _mm(xe, wg_s[...])
        u = _mm(xe, wu_s[...])
        hmid = ((g * (1.0 / (1.0 + jnp.exp(-g)))) * u).astype(BF16)
        yb_ref[...] = _mm(hmid, wd_s[...])

    @pl.when(i >= nu_ref[0])
    def _():
        yb_ref[...] = jnp.zeros_like(yb_ref)


def _moe(block_e, n_used, xs, w_gate, w_up, w_down, blk):
    n_blocks = block_e.shape[0]
    in_blk = lambda i, be, nu: (jnp.maximum(jnp.minimum(i, nu[0] - 1), 0), 0)
    grid_spec = pltpu.PrefetchScalarGridSpec(
        num_scalar_prefetch=2,
        grid=(n_blocks,),
        in_specs=[
            pl.BlockSpec((blk, D_MODEL), in_blk),
            pl.BlockSpec((1, D_MODEL, D_EXPERT), lambda i, be, nu: (be[i], 0, 0)),
            pl.BlockSpec((1, D_MODEL, D_EXPERT), lambda i, be, nu: (be[i], 0, 0)),
            pl.BlockSpec((1, D_EXPERT, D_MODEL), lambda i, be, nu: (be[i], 0, 0)),
        ],
        out_specs=pl.BlockSpec((blk, D_MODEL), lambda i, be, nu: (i, 0)),
        scratch_shapes=[
            pltpu.VMEM((D_MODEL, D_EXPERT), BF16),
            pltpu.VMEM((D_MODEL, D_EXPERT), BF16),
            pltpu.VMEM((D_EXPERT, D_MODEL), BF16),
        ],
    )
    return pl.pallas_call(
        _moe_kernel,
        grid_spec=grid_spec,
        out_shape=jax.ShapeDtypeStruct(xs.shape, F32),
        compiler_params=pltpu.CompilerParams(
            dimension_semantics=("arbitrary",), vmem_limit_bytes=VMEM_LIMIT),
        name="moe",
    )(block_e, n_used, xs, w_gate, w_up, w_down)


def _combine_kernel(dest_ref, dest_next_ref, *rest, tile_starts):
    n_parts = len(tile_starts)
    x2_refs, rt_refs = rest[:n_parts], rest[n_parts:2 * n_parts]
    yb_hbm = rest[2 * n_parts]
    o_refs = rest[2 * n_parts + 1:3 * n_parts + 1]
    buf, sem = rest[3 * n_parts + 1:]
    tm = x2_refs[0].shape[0]
    i = pl.program_id(0)
    slot = lax.rem(i, 2)

    def gather(d_ref, s):
        for k in range(2):
            for r in range(tm):
                pltpu.make_async_copy(yb_hbm.at[pl.ds(d_ref[0, 0, k * tm + r], 1), :],
                                      buf.at[s, k, pl.ds(r, 1), :],
                                      sem.at[s]).start(priority=r % 2)

    @pl.when(i == 0)
    def _():
        gather(dest_ref, 0)

    for s in range(2):
        @pl.when((i + 1 < pl.num_programs(0)) & (slot == 1 - s))
        def _():
            gather(dest_next_ref, s)

    for k in range(2):
        pltpu.make_async_copy(yb_hbm.at[pl.ds(0, tm), :], buf.at[slot, k], sem.at[slot]).wait()
    rt = _select_part(i, tile_starts, rt_refs)
    out = (_select_part(i, tile_starts, x2_refs) + rt[:, 2:3] * buf[slot, 0]
           + rt[:, 3:4] * buf[slot, 1])
    ends = tile_starts[1:] + (pl.num_programs(0),)
    for start, end, o_ref in zip(tile_starts, ends, o_refs):
        @pl.when((i >= start) & (i < end))
        def _():
            o_ref[...] = out


def _combine(dest_t, x2s, rts, yb, tm):
    counts, starts = _tile_layout(x2s, tm)
    nt = sum(counts)
    spec = lambda w: [_part_spec((tm, w), s, n) for s, n in zip(starts, counts)]
    return pl.pallas_call(
        functools.partial(_combine_kernel, tile_starts=tuple(starts)),
        grid=(nt,),
        in_specs=[
            pl.BlockSpec((1, 1, 2 * tm), lambda i: (i, 0, 0), memory_space=pltpu.SMEM),
            pl.BlockSpec((1, 1, 2 * tm), lambda i: (jnp.minimum(i + 1, nt - 1), 0, 0),
                         memory_space=pltpu.SMEM),
        ] + spec(D_MODEL) + spec(LANES) + [pl.BlockSpec(memory_space=pl.ANY)],
        out_specs=spec(D_MODEL),
        out_shape=[jax.ShapeDtypeStruct(x2.shape, F32) for x2 in x2s],
        scratch_shapes=[
            pltpu.VMEM((2, 2, tm, D_MODEL), F32),
            pltpu.SemaphoreType.DMA((2,)),
        ],
        compiler_params=pltpu.CompilerParams(
            dimension_semantics=("arbitrary",), vmem_limit_bytes=VMEM_LIMIT),
        name="combine",
    )(dest_t, dest_t, *x2s, *rts, yb)


def _hier_moe(parts, cnt, w_gate, w_up, w_down, tm, blk):
    counts = cnt[0, ROUTER_COL0:ROUTER_COL0 + N_EXPERTS].astype(jnp.int32)
    padded = (counts + blk - 1) // blk * blk
    pad_end = jnp.cumsum(padded)
    pad_start = pad_end - padded
    t_all = sum(p[0].shape[0] for p in parts)
    n_blocks = (2 * t_all + N_EXPERTS * (blk - 1)) // blk + 1
    rows = n_blocks * blk
    blk_row0 = jnp.arange(n_blocks, dtype=jnp.int32) * blk
    block_e = jnp.minimum(
        jnp.sum((pad_end[None, :] <= blk_row0[:, None]).astype(jnp.int32), axis=1),
        N_EXPERTS - 1)
    n_used = (pad_end[-1] // blk).astype(jnp.int32).reshape(1)
    experts = jnp.arange(N_EXPERTS, dtype=jnp.int32)[:, None, None]

    dests = []
    for x2, _, _, rtt in parts:
        T = x2.shape[0]
        flat = lambda a: jnp.swapaxes(a, 0, 1).reshape(a.shape[1], T)
        eid = flat(rtt[:, 0:2, :]).astype(jnp.int32)
        rank = flat(rtt[:, 4:6, :]).astype(jnp.int32)
        dest = rank + jnp.sum(
            jnp.where(eid[None] == experts, pad_start[:, None, None], 0), axis=0)
        nt = T // tm
        dests.append(dest.reshape(2, nt, tm).transpose(1, 0, 2).reshape(nt, 1, 2 * tm))
    dest_t = jnp.concatenate(dests, axis=0)
    xs = _dispatch(pad_end, padded, dest_t, [p[1] for p in parts], rows, tm, blk)
    yb = _moe(block_e, n_used, xs, w_gate, w_up, w_down, blk)
    return _combine(dest_t, [p[0] for p in parts], [p[2] for p in parts], yb, tm)


def _rope_table(pos):
    half = ROPE_DIM // 2
    inv = ROPE_THETA ** (-jnp.arange(0, ROPE_DIM, 2, dtype=F32) / ROPE_DIM)
    ang = pos.astype(F32)[:, None] * inv[None, :]
    cos, sin = jnp.cos(ang), jnp.sin(ang)
    L = pos.shape[0]
    pad = jnp.zeros((L, HEAD_DIM - ROPE_DIM), F32)
    zero = jnp.zeros((L, half), F32)
    c64 = jnp.concatenate([cos, cos, pad + 1.0], axis=1)
    lo64 = jnp.concatenate([-sin, zero, pad], axis=1)
    hi64 = jnp.concatenate([zero, sin, pad], axis=1)
    two = lambda t: jnp.concatenate([t, t], axis=1)
    return jnp.concatenate([two(c64), two(lo64), two(hi64)], axis=1)


def _mixers(x, pos_rope, kctx_prev, vctx_prev, h0r, h0i, mk, mv, wp, sp, cnt0, *,
            tm_in, tq, ssm_l, tm_mid):
    B, S, _ = x.shape
    T = B * S
    q, k3, v3, u_tm = _in_proj(x, wp["gmix"], wp["win"], wp["gq"], wp["gk"], pos_rope, tm_in)
    if kctx_prev is None:
        kctx, vctx = k3, v3
    else:
        kctx = jnp.concatenate([kctx_prev, k3], axis=1)
        vctx = jnp.concatenate([vctx_prev, v3], axis=1)
    att = _swa(wp["sink"], q, kctx, vctx, tq, mask_context=kctx_prev is None)
    ssm_tm, hr, hi = _ssm(u_tm.reshape(S, B, SSM_WIDTH), h0r, h0i, sp, ssm_l)
    x2, hn, rt, rtt, cnt = _mid(x, att, ssm_tm.reshape(S, B * SSM_WIDTH), mk, mv, wp, cnt0,
                                *tm_mid)
    part = (x2.reshape(T, D_MODEL), hn.reshape(T, D_MODEL), rt.reshape(T, LANES), rtt)
    return part, cnt, k3, v3, hr, hi


def kernel(x_prompt, x_sample, cache_attn_k, cache_attn_v, state_ssm_re, state_ssm_im, cache_mem_k, cache_mem_v, mem_prompt, norm_mix, w_in, q_norm, k_norm, attn_sink, ssm_lambda_re, ssm_lambda_im, ssm_log_dt, ssm_b_re, ssm_b_im, ssm_c_re, ssm_c_im, ssm_d, ssm_w_glu, ssm_b_glu, norm_attn_out, norm_ssm_out, w_out, norm_cross, norm_mem, w_cq, w_ck, w_cv, cq_norm, ck_norm, w_co, norm_ffn, w_router_group, b_router_group, w_router_expert, b_router_expert, w_e_gate, w_e_up, w_e_down):
    depth = norm_mix.shape[0]
    Bp, Lp, _ = x_prompt.shape
    Bs, Ls, _ = x_sample.shape
    yp, ys = x_prompt, x_sample
    rope_p = _rope_table(jnp.arange(Lp, dtype=jnp.int32))
    rope_s = _rope_table(PAST_LEN + jnp.arange(Ls, dtype=jnp.int32))
    outs = [[] for _ in range(10)]
    n_router = N_EXPERT_GROUPS + N_EXPERTS
    for l in range(depth):
        row = lambda a: a[l].astype(F32).reshape(1, -1)
        w_r = jnp.pad(jnp.concatenate([w_router_group[l], w_router_expert[l]], axis=1).astype(F32),
                      ((0, 0), (0, LANES - n_router)))
        w_r_hi = w_r.astype(BF16)
        w_r_lo = (w_r - w_r_hi.astype(F32)).astype(BF16)
        b_r = jnp.pad(jnp.concatenate([b_router_group[l], b_router_expert[l]]).astype(F32),
                      (0, LANES - n_router)).reshape(1, LANES)
        wp = {
            "gmix": row(norm_mix), "win": w_in[l].astype(BF16),
            "gq": jnp.tile(row(q_norm), (1, LANES // HEAD_DIM)),
            "gk": jnp.tile(row(k_norm), (1, LANES // HEAD_DIM)),
            "sink": attn_sink[l].astype(F32),
            "gao": row(norm_attn_out), "gso": row(norm_ssm_out),
            "wout": w_out[l].astype(BF16), "gx": row(norm_cross),
            "wcq": w_cq[l].astype(BF16), "gcq": row(cq_norm),
            "wco": w_co[l].astype(BF16), "gffn": row(norm_ffn),
            "wr": jnp.concatenate([w_r_hi, w_r_lo], axis=1), "br": b_r,
        }
        sp = _ssm_params(ssm_lambda_re[l], ssm_lambda_im[l], ssm_log_dt[l], ssm_b_re[l],
                         ssm_b_im[l], ssm_c_re[l], ssm_c_im[l], ssm_d[l], ssm_w_glu[l],
                         ssm_b_glu[l])
        ew = (w_e_gate[l].astype(F32), w_e_up[l].astype(F32), w_e_down[l].astype(F32))

        w_ckv = jnp.concatenate([w_ck[l], w_cv[l]], axis=1).astype(BF16)
        mkp, mvp = _memkv(mem_prompt.reshape(Bp * N_MEM, D_MODEL), row(norm_mem), w_ckv,
                          row(ck_norm), 512)
        mkp = mkp.reshape(Bp, N_MEM, CA_WIDTH)
        mvp = mvp.reshape(Bp, N_MEM, CA_WIDTH)

        zst = jnp.zeros((Bp, SSM_COLS), F32)
        part_p, cnt_p, kp, vp, hpr, hpi = _mixers(
            yp, rope_p, None, None, zst, zst, mkp, mvp, wp, sp, jnp.zeros((1, LANES), F32),
            tm_in=512, tq=256, ssm_l=64, tm_mid=(1, 512))
        part_s, cnt_s, kn, vn, hsr, hsi = _mixers(
            ys, rope_s, cache_attn_k[l].reshape(Bs, WINDOW, KV_WIDTH).astype(F32),
            cache_attn_v[l].reshape(Bs, WINDOW, KV_WIDTH).astype(F32),
            state_ssm_re[l].astype(F32).reshape(Bs, SSM_COLS),
            state_ssm_im[l].astype(F32).reshape(Bs, SSM_COLS),
            cache_mem_k[l].astype(F32).reshape(Bs, N_MEM, CA_WIDTH),
            cache_mem_v[l].astype(F32).reshape(Bs, N_MEM, CA_WIDTH), wp, sp, cnt_p,
            tm_in=Ls, tq=CHUNK, ssm_l=Ls, tm_mid=(8, Ls))
        yp, ys = _hier_moe([part_p, part_s], cnt_s, *ew, 256, 2 * MOE_BLOCK)
        yp = yp.reshape(Bp, Lp, D_MODEL)
        ys = ys.reshape(Bs, Ls, D_MODEL)

        sg = (N_SSM_GROUPS, SSM_STATE)
        kvs = (N_KV_HEADS, HEAD_DIM)
        vals = (kp[:, Lp - WINDOW:].reshape(Bp, WINDOW, *kvs),
                vp[:, Lp - WINDOW:].reshape(Bp, WINDOW, *kvs),
                hpr.reshape(Bp, *sg), hpi.reshape(Bp, *sg),
                mkp.reshape(Bp, N_MEM, CA_HEADS, CA_HEAD_DIM),
                mvp.reshape(Bp, N_MEM, CA_HEADS, CA_HEAD_DIM),
                kn.reshape(Bs, Ls, *kvs), vn.reshape(Bs, Ls, *kvs),
                hsr.reshape(Bs, *sg), hsi.reshape(Bs, *sg))
        for lst, val in zip(outs, vals):
            lst.append(val)
    return (yp, ys) + tuple(jnp.stack(lst) for lst in outs)
```

```python
import functools
import math

import jax
import jax.numpy as jnp
from jax import lax
from jax.experimental import pallas as pl
from jax.experimental.pallas import tpu as pltpu

F32 = jnp.float32
BF16 = jnp.bfloat16

D_MODEL = 1024
CHUNK = 64
N_Q_HEADS = 8
N_KV_HEADS = 2
GQA = N_Q_HEADS // N_KV_HEADS
HEAD_DIM = 64
WINDOW = 128
BAND = WINDOW + CHUNK
ROPE_DIM = HEAD_DIM // 4
ROPE_THETA = 500000.0
ATT_WIDTH = N_Q_HEADS * HEAD_DIM
KV_WIDTH = N_KV_HEADS * HEAD_DIM
SSM_GROUP = 16
SSM_WIDTH = D_MODEL // 2
N_SSM_GROUPS = SSM_WIDTH // SSM_GROUP
SSM_STATE = 64
SSM_COLS = N_SSM_GROUPS * SSM_STATE
IN_WIDTH = ATT_WIDTH + 2 * KV_WIDTH + SSM_WIDTH
N_MEM = 256
CA_HEADS = 4
CA_HEAD_DIM = 128
CA_WIDTH = CA_HEADS * CA_HEAD_DIM
N_EXPERT_GROUPS = 4
EXPERTS_PER_GROUP = 8
N_EXPERTS = N_EXPERT_GROUPS * EXPERTS_PER_GROUP
D_EXPERT = 512
MOE_BLOCK = 256
EPS = 1e-6
NEG = -1e30
PAST_LEN = 4096

LANES = 128
ROUTER_COL0 = N_EXPERT_GROUPS
VMEM_LIMIT = 48 * 1024 * 1024


def _rms(x, g):
    ms = jnp.mean(x * x, axis=-1, keepdims=True)
    return (x * lax.rsqrt(ms + EPS)) * g


def _mm(a, b):
    return jnp.dot(a, b, preferred_element_type=F32)


def _in_proj_kernel(x_ref, g_ref, w_ref, gq_ref, gk_ref, rope_ref,
                    q_ref, k_ref, v_ref, u_ref):
    tm = x_ref.shape[1]
    h = _rms(x_ref[0], g_ref[...])
    hin = _mm(h.astype(BF16), w_ref[...])
    rope = rope_ref[...]
    cos = rope[:, 0:LANES]
    sin_lo = rope[:, LANES:2 * LANES]
    sin_hi = rope[:, 2 * LANES:3 * LANES]
    lane = lax.broadcasted_iota(jnp.int32, (tm, LANES), 1)
    left = lane < HEAD_DIM

    def norm_rope(z, g):
        sq = z * z
        lsum = jnp.sum(jnp.where(left, sq, 0.0), axis=-1, keepdims=True)
        rsum = jnp.sum(jnp.where(left, 0.0, sq), axis=-1, keepdims=True)
        ms = jnp.where(left, lsum, rsum) * (1.0 / HEAD_DIM)
        zn = (z * lax.rsqrt(ms + EPS)) * g
        half = ROPE_DIM // 2
        return (zn * cos + pltpu.roll(zn, LANES - half, 1) * sin_lo
                + pltpu.roll(zn, half, 1) * sin_hi)

    for j in range(ATT_WIDTH // LANES):
        sl = slice(j * LANES, (j + 1) * LANES)
        q_ref[0, :, sl] = norm_rope(hin[:, sl], gq_ref[...])
    k_ref[0] = norm_rope(hin[:, ATT_WIDTH:ATT_WIDTH + KV_WIDTH], gk_ref[...])
    v_ref[0] = hin[:, ATT_WIDTH + KV_WIDTH:ATT_WIDTH + 2 * KV_WIDTH]
    u_ref[...] = hin[:, ATT_WIDTH + 2 * KV_WIDTH:]


def _in_proj(x, g, w_bf, gq, gk, rope, tm):
    B, S, _ = x.shape
    full = lambda b, i: (0, 0)
    tile = lambda w: pl.BlockSpec((1, tm, w), lambda b, i: (b, i, 0))
    return pl.pallas_call(
        _in_proj_kernel,
        grid=(B, S // tm),
        in_specs=[
            tile(D_MODEL),
            pl.BlockSpec((1, D_MODEL), full),
            pl.BlockSpec((D_MODEL, IN_WIDTH), full),
            pl.BlockSpec((1, LANES), full),
            pl.BlockSpec((1, LANES), full),
            pl.BlockSpec((tm, 3 * LANES), lambda b, i: (i, 0)),
        ],
        out_specs=[
            tile(ATT_WIDTH), tile(KV_WIDTH), tile(KV_WIDTH),
            pl.BlockSpec((tm, SSM_WIDTH), lambda b, i: (i, b)),
        ],
        out_shape=[
            jax.ShapeDtypeStruct((B, S, ATT_WIDTH), F32),
            jax.ShapeDtypeStruct((B, S, KV_WIDTH), F32),
            jax.ShapeDtypeStruct((B, S, KV_WIDTH), F32),
            jax.ShapeDtypeStruct((S, B * SSM_WIDTH), F32),
        ],
        compiler_params=pltpu.CompilerParams(
            dimension_semantics=("arbitrary", "arbitrary"),
            vmem_limit_bytes=VMEM_LIMIT),
        name="in_proj",
    )(x, g, w_bf, gq, gk, rope)


def _swa_kernel(sink_ref, q_ref, k_ref, v_ref, o_ref, *, mask_context):
    tq = q_ref.shape[1]
    i = pl.program_id(1)
    nch = tq // CHUNK
    lane = lax.broadcasted_iota(jnp.int32, (BAND, LANES), 1)
    lo_half = lane < HEAD_DIM
    vrow_lo = lax.broadcasted_iota(jnp.int32, (LANES, BAND), 0) < HEAD_DIM
    q_lo = lax.broadcasted_iota(jnp.int32, (1, LANES), 1) < CHUNK
    slabs_per_kv = GQA * HEAD_DIM // LANES

    units = []
    scores = []
    vpads = {}
    for c in range(nch):
        chunk = i * nch + c
        if mask_context:
            first = jnp.maximum(chunk - WINDOW // CHUNK, 0)
            start = pl.multiple_of(first * CHUNK, CHUNK)
            kidx = start + lax.broadcasted_iota(jnp.int32, (BAND, LANES), 0)
            valid = kidx < (chunk + 1) * CHUNK
        else:
            start = pl.multiple_of(chunk * CHUNK, CHUNK)
        kb = k_ref[0, pl.ds(start, BAND), :]
        kb_sw = pltpu.roll(kb, HEAD_DIM, 1)
        vt = v_ref[0, pl.ds(start, BAND), :].T
        vt_sw = jnp.concatenate([vt[HEAD_DIM:], vt[:HEAD_DIM]], axis=0)
        for kvh in range(N_KV_HEADS):
            k_own, k_oth = (kb, kb_sw) if kvh == 0 else (kb_sw, kb)
            v_own, v_oth = (vt, vt_sw) if kvh == 0 else (vt_sw, vt)
            kpad = (jnp.where(lo_half, k_own, 0.0).astype(BF16),
                    jnp.where(lo_half, 0.0, k_oth).astype(BF16))
            vpads[(c, kvh)] = (jnp.where(vrow_lo, v_own, 0.0).astype(BF16),
                               jnp.where(vrow_lo, 0.0, v_oth).astype(BF16))
            col0 = kvh * GQA * HEAD_DIM
            q2 = jnp.concatenate(
                [q_ref[0, c * CHUNK:(c + 1) * CHUNK, col0 + m * LANES:col0 + (m + 1) * LANES]
                 for m in range(slabs_per_kv)], axis=0).astype(BF16)
            for side in range(2):
                s = lax.dot_general(kpad[side], q2, (((1,), (1,)), ((), ())),
                                    preferred_element_type=F32) * (HEAD_DIM ** -0.5)
                if mask_context:
                    s = jnp.where(valid, s, NEG)
                units.append((c, kvh, side))
                scores.append(s)

    sinks = [jnp.where(q_lo, sink_ref[kvh * GQA + side], sink_ref[kvh * GQA + 2 + side])
             for (_, kvh, side) in units]
    maxes = [jnp.maximum(jnp.max(s, axis=0, keepdims=True), sk)
             for s, sk in zip(scores, sinks)]
    exps = [jnp.exp(s - mx) for s, mx in zip(scores, maxes)]
    dens = [jnp.sum(p, axis=0, keepdims=True) + jnp.exp(sk - mx)
            for p, sk, mx in zip(exps, sinks, maxes)]
    probs = [(p * (1.0 / den)).astype(BF16) for p, den in zip(exps, dens)]

    for n in range(0, len(units), 2):
        c, kvh, _ = units[n]
        vp = vpads[(c, kvh)]
        o = (_mm(vp[0], probs[n]) + _mm(vp[1], probs[n + 1])).T
        col0 = kvh * GQA * HEAD_DIM
        for m in range(slabs_per_kv):
            o_ref[0, c * CHUNK:(c + 1) * CHUNK, col0 + m * LANES:col0 + (m + 1) * LANES] = (
                o[m * CHUNK:(m + 1) * CHUNK])


def _swa(sink, q, kctx, vctx, tq, mask_context):
    B, Sq, _ = q.shape
    Sk = kctx.shape[1]
    return pl.pallas_call(
        functools.partial(_swa_kernel, mask_context=mask_context),
        grid=(B, Sq // tq),
        in_specs=[
            pl.BlockSpec(memory_space=pltpu.SMEM),
            pl.BlockSpec((1, tq, ATT_WIDTH), lambda b, i: (b, i, 0)),
            pl.BlockSpec((1, Sk, KV_WIDTH), lambda b, i: (b, 0, 0)),
            pl.BlockSpec((1, Sk, KV_WIDTH), lambda b, i: (b, 0, 0)),
        ],
        out_specs=pl.BlockSpec((1, tq, ATT_WIDTH), lambda b, i: (b, i, 0)),
        out_shape=jax.ShapeDtypeStruct((B, Sq, ATT_WIDTH), F32),
        compiler_params=pltpu.CompilerParams(
            dimension_semantics=("arbitrary", "arbitrary"),
            vmem_limit_bytes=VMEM_LIMIT),
        name="swa",
    )(sink, q, kctx, vctx)


def _ssm_kernel(u_ref, h0r_ref, h0i_ref, lam_ref, bre_ref, bim_ref, cre_ref, cim_ref,
                d_ref, wglu_ref, bglu_ref,
                y_ref, hr_out, hi_out, sr, si, hr_s, hi_s):
    L, B, _ = u_ref.shape
    rows = L * B
    half_w = SSM_WIDTH // 2
    half_c = SSM_COLS // 2

    @pl.when(pl.program_id(0) == 0)
    def _():
        hr_s[...] = h0r_ref[...]
        hi_s[...] = h0i_ref[...]

    u = u_ref[...].reshape(rows, SSM_WIDTH)
    ub = u.astype(BF16)
    for hf in range(2):
        uh = ub[:, hf * half_w:(hf + 1) * half_w]
        sr[:, hf * half_c:(hf + 1) * half_c] = _mm(uh, bre_ref[hf])
        si[:, hf * half_c:(hf + 1) * half_c] = _mm(uh, bim_ref[hf])

    cw = 4 * LANES
    for cc in range(SSM_COLS // cw):
        cols = slice(cc * cw, (cc + 1) * cw)
        lr = jnp.broadcast_to(lam_ref[0:1, cols], (B, cw))
        li = jnp.broadcast_to(lam_ref[1:2, cols], (B, cw))

        def body(t, carry):
            hr, hi = carry
            at_t = pl.ds(pl.multiple_of(t * B, B), B)
            nr = lr * hr - li * hi + sr[at_t, cols]
            ni = lr * hi + li * hr + si[at_t, cols]
            sr[at_t, cols] = nr
            si[at_t, cols] = ni
            return nr, ni

        hr, hi = lax.fori_loop(0, L, body, (hr_s[:, cols], hi_s[:, cols]), unroll=2)
        hr_s[:, cols] = hr
        hi_s[:, cols] = hi

    ys = []
    for hf in range(2):
        cs = slice(hf * half_c, (hf + 1) * half_c)
        ys.append(_mm(sr[:, cs].astype(BF16), cre_ref[hf])
                  + _mm(si[:, cs].astype(BF16), cim_ref[hf]))
    y = jnp.concatenate(ys, axis=1) + d_ref[...] * u
    g = 0.5 * y * (1.0 + jnp.tanh(math.sqrt(2.0 / math.pi) * (y + 0.044715 * (y * y * y))))
    gb = g.astype(BF16)
    z = jnp.concatenate(
        [_mm(gb[:, hf * half_w:(hf + 1) * half_w], wglu_ref[hf]) for hf in range(2)],
        axis=1) + bglu_ref[...]
    out = g * (1.0 / (1.0 + jnp.exp(-z)))
    y_ref[...] = out.reshape(L, B, SSM_WIDTH)
    hr_out[...] = hr_s[...]
    hi_out[...] = hi_s[...]


def _ssm(u, h0r, h0i, sp, L):
    S, B, _ = u.shape
    c2 = lambda i: (0, 0)
    c3 = lambda i: (0, 0, 0)
    return pl.pallas_call(
        _ssm_kernel,
        grid=(S // L,),
        in_specs=[
            pl.BlockSpec((L, B, SSM_WIDTH), lambda i: (i, 0, 0)),
            pl.BlockSpec((B, SSM_COLS), c2),
            pl.BlockSpec((B, SSM_COLS), c2),
            pl.BlockSpec((2, SSM_COLS), c2),
            pl.BlockSpec((2, SSM_WIDTH // 2, SSM_COLS // 2), c3),
            pl.BlockSpec((2, SSM_WIDTH // 2, SSM_COLS // 2), c3),
            pl.BlockSpec((2, SSM_COLS // 2, SSM_WIDTH // 2), c3),
            pl.BlockSpec((2, SSM_COLS // 2, SSM_WIDTH // 2), c3),
            pl.BlockSpec((1, SSM_WIDTH), c2),
            pl.BlockSpec((2, SSM_WIDTH // 2, SSM_WIDTH // 2), c3),
            pl.BlockSpec((1, SSM_WIDTH), c2),
        ],
        out_specs=[
            pl.BlockSpec((L, B, SSM_WIDTH), lambda i: (i, 0, 0)),
            pl.BlockSpec((B, SSM_COLS), c2),
            pl.BlockSpec((B, SSM_COLS), c2),
        ],
        out_shape=[
            jax.ShapeDtypeStruct((S, B, SSM_WIDTH), F32),
            jax.ShapeDtypeStruct((B, SSM_COLS), F32),
            jax.ShapeDtypeStruct((B, SSM_COLS), F32),
        ],
        scratch_shapes=[
            pltpu.VMEM((L * B, SSM_COLS), F32),
            pltpu.VMEM((L * B, SSM_COLS), F32),
            pltpu.VMEM((B, SSM_COLS), F32),
            pltpu.VMEM((B, SSM_COLS), F32),
        ],
        compiler_params=pltpu.CompilerParams(
            dimension_semantics=("arbitrary",), vmem_limit_bytes=VMEM_LIMIT),
        name="ssm",
    )(u, h0r, h0i, sp["lam"], sp["bre"], sp["bim"], sp["cre"], sp["cim"],
      sp["d"], sp["wglu"], sp["bglu"])


def _block_diag(blocks):
    G, r, c = blocks.shape
    eye = jnp.eye(G, dtype=blocks.dtype)
    return jnp.einsum("grc,gh->grhc", blocks, eye).reshape(G * r, G * c)


def _ssm_params(lam_re, lam_im, log_dt, b_re, b_im, c_re, c_im, d, w_glu, b_glu):
    lam = lax.complex(lam_re.astype(F32), lam_im.astype(F32))
    dt = jnp.exp(log_dt.astype(F32))[:, None]
    lam_bar = jnp.exp(lam * dt)
    bmat = lax.complex(b_re.astype(F32), b_im.astype(F32))
    b_bar = ((lam_bar - 1.0) / lam)[..., None] * bmat
    lam2 = jnp.stack([lam_bar.real.reshape(-1), lam_bar.imag.reshape(-1)])
    bt = jnp.swapaxes(b_bar, 1, 2)
    hw, hc = SSM_WIDTH // 2, SSM_COLS // 2
    split_b = lambda m: jnp.stack([m[:hw, :hc], m[hw:, hc:]]).astype(BF16)
    split_c = lambda m: jnp.stack([m[:hc, :hw], m[hc:, hw:]]).astype(BF16)
    ct_re = jnp.swapaxes(c_re.astype(F32), 1, 2)
    ct_im = jnp.swapaxes(c_im.astype(F32), 1, 2)
    wg = _block_diag(w_glu.astype(F32))
    return {
        "lam": lam2,
        "bre": split_b(_block_diag(bt.real)),
        "bim": split_b(_block_diag(bt.imag)),
        "cre": split_c(_block_diag(ct_re)),
        "cim": split_c(_block_diag(-ct_im)),
        "d": d.astype(F32).reshape(1, SSM_WIDTH),
        "wglu": jnp.stack([wg[:hw, :hw], wg[hw:, hw:]]).astype(BF16),
        "bglu": b_glu.astype(F32).reshape(1, SSM_WIDTH),
    }


def _memkv_kernel(m_ref, g_ref, w_ref, gk_ref, k_ref, v_ref):
    m = _rms(m_ref[...], g_ref[...])
    kv = _mm(m.astype(BF16), w_ref[...])
    for h in range(CA_HEADS):
        sl = slice(h * CA_HEAD_DIM, (h + 1) * CA_HEAD_DIM)
        k_ref[:, sl] = _rms(kv[:, sl], gk_ref[...])
    v_ref[...] = kv[:, CA_WIDTH:]


def _memkv(mem2d, g, w_bf, gk, tm):
    T = mem2d.shape[0]
    full = lambda i: (0, 0)
    return pl.pallas_call(
        _memkv_kernel,
        grid=(T // tm,),
        in_specs=[
            pl.BlockSpec((tm, D_MODEL), lambda i: (i, 0)),
            pl.BlockSpec((1, D_MODEL), full),
            pl.BlockSpec((D_MODEL, 2 * CA_WIDTH), full),
            pl.BlockSpec((1, CA_HEAD_DIM), full),
        ],
        out_specs=[
            pl.BlockSpec((tm, CA_WIDTH), lambda i: (i, 0)),
            pl.BlockSpec((tm, CA_WIDTH), lambda i: (i, 0)),
        ],
        out_shape=[
            jax.ShapeDtypeStruct((T, CA_WIDTH), F32),
            jax.ShapeDtypeStruct((T, CA_WIDTH), F32),
        ],
        compiler_params=pltpu.CompilerParams(
            dimension_semantics=("arbitrary",), vmem_limit_bytes=VMEM_LIMIT),
        name="memkv",
    )(mem2d, g, w_bf, gk)


def _mid_kernel(x_ref, att_ref, ssm_ref, mk_ref, mv_ref,
                gao_ref, gso_ref, wout_ref, gx_ref, wcq_ref, gcq_ref, wco_ref,
                gffn_ref, wr_ref, br_ref, cnt0_ref,
                x2_ref, hn_ref, rt_ref, rtt_ref, cnt_ref, base_s):
    nb, ts, _ = x_ref.shape
    tm = nb * ts

    @pl.when((pl.program_id(0) == 0) & (pl.program_id(1) == 0))
    def _():
        base_s[...] = cnt0_ref[...]

    ssm = jnp.concatenate(
        [ssm_ref[:, b * SSM_WIDTH:(b + 1) * SSM_WIDTH] for b in range(nb)], axis=0)
    a = _rms(att_ref[...].reshape(tm, ATT_WIDTH), gao_ref[...]).astype(BF16)
    s = _rms(ssm, gso_ref[...]).astype(BF16)
    x1 = (x_ref[...].reshape(tm, D_MODEL) + _mm(a, wout_ref[0:ATT_WIDTH, :])
          + _mm(s, wout_ref[ATT_WIDTH:, :]))

    qx = _mm(_rms(x1, gx_ref[...]).astype(BF16), wcq_ref[...])
    heads = []
    for h in range(CA_HEADS):
        sl = slice(h * CA_HEAD_DIM, (h + 1) * CA_HEAD_DIM)
        qh = _rms(qx[:, sl], gcq_ref[...]).astype(BF16)
        per_batch = []
        for b in range(nb):
            kh = mk_ref[b, :, sl].astype(BF16)
            vh = mv_ref[b, :, sl].astype(BF16)
            sc = lax.dot_general(qh[b * ts:(b + 1) * ts], kh, (((1,), (1,)), ((), ())),
                                 preferred_element_type=F32) * (CA_HEAD_DIM ** -0.5)
            p = jnp.exp(sc - jnp.max(sc, axis=-1, keepdims=True))
            p = p / jnp.sum(p, axis=-1, keepdims=True)
            per_batch.append(_mm(p.astype(BF16), vh))
        heads.append(jnp.concatenate(per_batch, axis=0))
    o = jnp.concatenate(heads, axis=1).astype(BF16)
    x2 = x1 + _mm(o, wco_ref[...])
    x2_ref[...] = x2.reshape(nb, ts, D_MODEL)

    hn = _rms(x2, gffn_ref[...])
    hn_ref[...] = hn.reshape(nb, ts, D_MODEL)

    h_hi = hn.astype(BF16)
    h_lo = (hn - h_hi.astype(F32)).astype(BF16)
    r1 = _mm(h_hi, wr_ref[...])
    lg = (r1[:, :LANES] + r1[:, LANES:] + _mm(h_lo, wr_ref[:, 0:LANES])
          + br_ref[...])

    col = lax.broadcasted_iota(jnp.int32, (tm, LANES), 1)
    big = jnp.int32(4 * LANES)
    gmask = col < N_EXPERT_GROUPS
    lgg = jnp.where(gmask, lg, NEG)
    mg = jnp.max(lgg, axis=-1, keepdims=True)
    grp = jnp.min(jnp.where(gmask & (lgg == mg), col, big), axis=-1, keepdims=True)
    pg_top = 1.0 / jnp.sum(jnp.where(gmask, jnp.exp(lgg - mg), 0.0), axis=-1, keepdims=True)

    ecol = col - ROUTER_COL0
    emask = ((ecol >= 0) & (ecol < N_EXPERTS)
             & (lax.shift_right_arithmetic(ecol, 3) == grp))
    le = jnp.where(emask, lg, NEG)
    m1 = jnp.max(le, axis=-1, keepdims=True)
    i1 = jnp.min(jnp.where(emask & (le == m1), col, big), axis=-1, keepdims=True)
    rest = emask & (col != i1)
    le2 = jnp.where(rest, lg, NEG)
    m2 = jnp.max(le2, axis=-1, keepdims=True)
    i2 = jnp.min(jnp.where(rest & (le2 == m2), col, big), axis=-1, keepdims=True)
    den = jnp.sum(jnp.where(emask, jnp.exp(le - m1), 0.0), axis=-1, keepdims=True)
    p1 = 1.0 / den
    p2 = jnp.exp(m2 - m1) / den
    gate1 = pg_top * p1 / (p1 + p2)
    gate2 = pg_top * p2 / (p1 + p2)

    sel1 = col == i1
    sel2 = col == i2
    oh = jnp.where(sel1 | sel2, 1.0, 0.0)
    r_i = lax.broadcasted_iota(jnp.int32, (tm, tm), 0)
    c_i = lax.broadcasted_iota(jnp.int32, (tm, tm), 1)
    tri = jnp.where(r_i > c_i, 1.0, 0.0).astype(BF16)
    tot = base_s[...] + _mm(tri, oh.astype(BF16))
    rank1 = jnp.sum(jnp.where(sel1, tot, 0.0), axis=-1, keepdims=True)
    rank2 = jnp.sum(jnp.where(sel2, tot, 0.0), axis=-1, keepdims=True)
    base_s[...] = base_s[...] + jnp.sum(oh, axis=0, keepdims=True)
    cnt_ref[...] = base_s[...]

    e1 = (i1 - ROUTER_COL0).astype(F32)
    e2 = (i2 - ROUTER_COL0).astype(F32)
    rt = jnp.zeros((tm, LANES), F32)
    for k, val in enumerate((e1, e2, gate1, gate2, rank1, rank2)):
        rt = jnp.where(col == k, val, rt)
    rt_ref[...] = rt.reshape(nb, ts, LANES)
    for b in range(nb):
        rtt_ref[b] = rt[b * ts:(b + 1) * ts].T[0:8, :]


def _mid(x, att, ssm_tm, mk, mv, wp, cnt0, nb, ts):
    B, S, _ = x.shape
    c2 = lambda b, i: (0, 0)
    tile = lambda w: pl.BlockSpec((nb, ts, w), lambda b, i: (b, i, 0))
    return pl.pallas_call(
        _mid_kernel,
        grid=(B // nb, S // ts),
        in_specs=[
            tile(D_MODEL), tile(ATT_WIDTH),
            pl.BlockSpec((ts, nb * SSM_WIDTH), lambda b, i: (i, b)),
            pl.BlockSpec((nb, N_MEM, CA_WIDTH), lambda b, i: (b, 0, 0)),
            pl.BlockSpec((nb, N_MEM, CA_WIDTH), lambda b, i: (b, 0, 0)),
            pl.BlockSpec((1, ATT_WIDTH), c2),
            pl.BlockSpec((1, SSM_WIDTH), c2),
            pl.BlockSpec((ATT_WIDTH + SSM_WIDTH, D_MODEL), c2),
            pl.BlockSpec((1, D_MODEL), c2),
            pl.BlockSpec((D_MODEL, CA_WIDTH), c2),
            pl.BlockSpec((1, CA_HEAD_DIM), c2),
            pl.BlockSpec((CA_WIDTH, D_MODEL), c2),
            pl.BlockSpec((1, D_MODEL), c2),
            pl.BlockSpec((D_MODEL, 2 * LANES), c2),
            pl.BlockSpec((1, LANES), c2),
            pl.BlockSpec((1, LANES), c2),
        ],
        out_specs=[
            tile(D_MODEL), tile(D_MODEL), tile(LANES),
            pl.BlockSpec((nb, 8, ts), lambda b, i: (b, 0, i)),
            pl.BlockSpec((1, LANES), c2),
        ],
        out_shape=[
            jax.ShapeDtypeStruct((B, S, D_MODEL), F32),
            jax.ShapeDtypeStruct((B, S, D_MODEL), F32),
            jax.ShapeDtypeStruct((B, S, LANES), F32),
            jax.ShapeDtypeStruct((B, 8, S), F32),
            jax.ShapeDtypeStruct((1, LANES), F32),
        ],
        scratch_shapes=[pltpu.VMEM((1, LANES), F32)],
        compiler_params=pltpu.CompilerParams(
            dimension_semantics=("arbitrary", "arbitrary"),
            vmem_limit_bytes=VMEM_LIMIT),
        name="mid",
    )(x, att, ssm_tm, mk, mv, wp["gao"], wp["gso"], wp["wout"], wp["gx"], wp["wcq"],
      wp["gcq"], wp["wco"], wp["gffn"], wp["wr"], wp["br"], cnt0)


def _select_part(i, tile_starts, refs):
    x = refs[0][...]
    for start, ref in zip(tile_starts[1:], refs[1:]):
        x = jnp.where(i >= start, ref[...], x)
    return x


def _part_spec(shape, tile_start, n_tiles):
    def index(i, *_):
        return (jnp.clip(i - tile_start, 0, n_tiles - 1),) + (0,) * (len(shape) - 1)
    return pl.BlockSpec(shape, index)


def _dispatch_kernel(pend_ref, padded_ref, dest_ref, *rest, tile_starts):
    n_parts = len(tile_starts)
    hn_refs = rest[:n_parts]
    xs_hbm, stage, zbuf, sem = rest[n_parts:]
    tm = hn_refs[0].shape[0]
    i = pl.program_id(0)
    slot = lax.rem(i, 2)
    blk = zbuf.shape[0]

    def wait_rows(s):
        for _ in range(2):
            pltpu.make_async_copy(stage.at[s], xs_hbm.at[pl.ds(0, tm)], sem.at[s]).wait()

    @pl.when(i == 0)
    def _():
        zbuf[...] = jnp.zeros_like(zbuf)
        for e in range(N_EXPERTS):
            @pl.when(padded_ref[e] > 0)
            def _():
                row0 = pl.multiple_of(pend_ref[e] - blk, blk)
                fill = pltpu.make_async_copy(zbuf, xs_hbm.at[pl.ds(row0, blk)], sem.at[2])
                fill.start()
                fill.wait()

        def fill_tail(b, carry):
            fill = pltpu.make_async_copy(
                zbuf, xs_hbm.at[pl.ds(pl.multiple_of(b * blk, blk), blk)], sem.at[2])
            fill.start()
            fill.wait()
            return carry

        lax.fori_loop(pend_ref[N_EXPERTS - 1] // blk, xs_hbm.shape[0] // blk, fill_tail, 0)

    @pl.when(i >= 2)
    def _():
        wait_rows(slot)

    tile = _select_part(i, tile_starts, hn_refs)
    for s in range(2):
        @pl.when(slot == s)
        def _():
            stage[s] = tile
            for k in range(2):
                for r in range(tm):
                    pltpu.make_async_copy(stage.at[s, pl.ds(r, 1), :],
                                          xs_hbm.at[pl.ds(dest_ref[0, 0, k * tm + r], 1), :],
                                          sem.at[s]).start(priority=r % 2)

    @pl.when(i == pl.num_programs(0) - 1)
    def _():
        wait_rows(slot)

        @pl.when(i >= 1)
        def _():
            wait_rows(1 - slot)


def _tile_layout(arrays, tm):
    counts = [a.shape[0] // tm for a in arrays]
    starts = [sum(counts[:p]) for p in range(len(counts))]
    return counts, starts


def _dispatch(pad_end, padded, dest_t, hns, rows, tm, blk):
    counts, starts = _tile_layout(hns, tm)
    grid_spec = pltpu.PrefetchScalarGridSpec(
        num_scalar_prefetch=2,
        grid=(sum(counts),),
        in_specs=[pl.BlockSpec((1, 1, 2 * tm), lambda i, pe, pd: (i, 0, 0),
                               memory_space=pltpu.SMEM)]
        + [_part_spec((tm, D_MODEL), s, n) for s, n in zip(starts, counts)],
        out_specs=pl.BlockSpec(memory_space=pl.ANY),
        scratch_shapes=[
            pltpu.VMEM((2, tm, D_MODEL), F32),
            pltpu.VMEM((blk, D_MODEL), F32),
            pltpu.SemaphoreType.DMA((3,)),
        ],
    )
    return pl.pallas_call(
        functools.partial(_dispatch_kernel, tile_starts=tuple(starts)),
        grid_spec=grid_spec,
        out_shape=jax.ShapeDtypeStruct((rows, D_MODEL), F32),
        compiler_params=pltpu.CompilerParams(
            dimension_semantics=("arbitrary",), vmem_limit_bytes=VMEM_LIMIT),
        name="dispatch",
    )(pad_end, padded, dest_t, *hns)


def _moe_kernel(be_ref, nu_ref, xs_ref, wg_ref, wu_ref, wd_ref, yb_ref, wg_s, wu_s, wd_s):
    i = pl.program_id(0)

    @pl.when(i < nu_ref[0])
    def _():
        @pl.when((i == 0) | (be_ref[i] != be_ref[jnp.maximum(i - 1, 0)]))
        def _():
            wg_s[...] = wg_ref[0].astype(BF16)
            wu_s[...] = wu_ref[0].astype(BF16)
            wd_s[...] = wd_ref[0].astype(BF16)

        xe = xs_ref[...].astype(BF16)
        g = _mm(xe, wg_s[...])
        u = _mm(xe, wu_s[...])
        hmid = ((g * (1.0 / (1.0 + jnp.exp(-g)))) * u).astype(BF16)
        yb_ref[...] = _mm(hmid, wd_s[...])

    @pl.when(i >= nu_ref[0])
    def _():
        yb_ref[...] = jnp.zeros_like(yb_ref)


def _moe(block_e, n_used, xs, w_gate, w_up, w_down, blk):
    n_blocks = block_e.shape[0]
    in_blk = lambda i, be, nu: (jnp.maximum(jnp.minimum(i, nu[0] - 1), 0), 0)
    grid_spec = pltpu.PrefetchScalarGridSpec(
        num_scalar_prefetch=2,
        grid=(n_blocks,),
        in_specs=[
            pl.BlockSpec((blk, D_MODEL), in_blk),
            pl.BlockSpec((1, D_MODEL, D_EXPERT), lambda i, be, nu: (be[i], 0, 0)),
            pl.BlockSpec((1, D_MODEL, D_EXPERT), lambda i, be, nu: (be[i], 0, 0)),
            pl.BlockSpec((1, D_EXPERT, D_MODEL), lambda i, be, nu: (be[i], 0, 0)),
        ],
        out_specs=pl.BlockSpec((blk, D_MODEL), lambda i, be, nu: (i, 0)),
        scratch_shapes=[
            pltpu.VMEM((D_MODEL, D_EXPERT), BF16),
            pltpu.VMEM((D_MODEL, D_EXPERT), BF16),
            pltpu.VMEM((D_EXPERT, D_MODEL), BF16),
        ],
    )
    return pl.pallas_call(
        _moe_kernel,
        grid_spec=grid_spec,
        out_shape=jax.ShapeDtypeStruct(xs.shape, F32),
        compiler_params=pltpu.CompilerParams(
            dimension_semantics=("arbitrary",), vmem_limit_bytes=VMEM_LIMIT),
        name="moe",
    )(block_e, n_used, xs, w_gate, w_up, w_down)


def _combine_kernel(dest_ref, dest_next_ref, *rest, tile_starts):
    n_parts = len(tile_starts)
    x2_refs, rt_refs = rest[:n_parts], rest[n_parts:2 * n_parts]
    yb_hbm = rest[2 * n_parts]
    o_refs = rest[2 * n_parts + 1:3 * n_parts + 1]
    buf, sem = rest[3 * n_parts + 1:]
    tm = x2_refs[0].shape[0]
    i = pl.program_id(0)
    slot = lax.rem(i, 2)

    def gather(d_ref, s):
        for k in range(2):
            for r in range(tm):
                pltpu.make_async_copy(yb_hbm.at[pl.ds(d_ref[0, 0, k * tm + r], 1), :],
                                      buf.at[s, k, pl.ds(r, 1), :],
                                      sem.at[s]).start(priority=r % 2)

    @pl.when(i == 0)
    def _():
        gather(dest_ref, 0)

    for s in range(2):
        @pl.when((i + 1 < pl.num_programs(0)) & (slot == 1 - s))
        def _():
            gather(dest_next_ref, s)

    for k in range(2):
        pltpu.make_async_copy(yb_hbm.at[pl.ds(0, tm), :], buf.at[slot, k], sem.at[slot]).wait()
    rt = _select_part(i, tile_starts, rt_refs)
    out = (_select_part(i, tile_starts, x2_refs) + rt[:, 2:3] * buf[slot, 0]
           + rt[:, 3:4] * buf[slot, 1])
    ends = tile_starts[1:] + (pl.num_programs(0),)
    for start, end, o_ref in zip(tile_starts, ends, o_refs):
        @pl.when((i >= start) & (i < end))
        def _():
            o_ref[...] = out


def _combine(dest_t, x2s, rts, yb, tm):
    counts, starts = _tile_layout(x2s, tm)
    nt = sum(counts)
    spec = lambda w: [_part_spec((tm, w), s, n) for s, n in zip(starts, counts)]
    return pl.pallas_call(
        functools.partial(_combine_kernel, tile_starts=tuple(starts)),
        grid=(nt,),
        in_specs=[
            pl.BlockSpec((1, 1, 2 * tm), lambda i: (i, 0, 0), memory_space=pltpu.SMEM),
            pl.BlockSpec((1, 1, 2 * tm), lambda i: (jnp.minimum(i + 1, nt - 1), 0, 0),
                         memory_space=pltpu.SMEM),
        ] + spec(D_MODEL) + spec(LANES) + [pl.BlockSpec(memory_space=pl.ANY)],
        out_specs=spec(D_MODEL),
        out_shape=[jax.ShapeDtypeStruct(x2.shape, F32) for x2 in x2s],
        scratch_shapes=[
            pltpu.VMEM((2, 2, tm, D_MODEL), F32),
            pltpu.SemaphoreType.DMA((2,)),
        ],
        compiler_params=pltpu.CompilerParams(
            dimension_semantics=("arbitrary",), vmem_limit_bytes=VMEM_LIMIT),
        name="combine",
    )(dest_t, dest_t, *x2s, *rts, yb)


def _hier_moe(parts, cnt, w_gate, w_up, w_down, tm, blk):
    counts = cnt[0, ROUTER_COL0:ROUTER_COL0 + N_EXPERTS].astype(jnp.int32)
    padded = (counts + blk - 1) // blk * blk
    pad_end = jnp.cumsum(padded)
    pad_start = pad_end - padded
    t_all = sum(p[0].shape[0] for p in parts)
    n_blocks = (2 * t_all + N_EXPERTS * (blk - 1)) // blk + 1
    rows = n_blocks * blk
    blk_row0 = jnp.arange(n_blocks, dtype=jnp.int32) * blk
    block_e = jnp.minimum(
        jnp.sum((pad_end[None, :] <= blk_row0[:, None]).astype(jnp.int32), axis=1),
        N_EXPERTS - 1)
    n_used = (pad_end[-1] // blk).astype(jnp.int32).reshape(1)
    experts = jnp.arange(N_EXPERTS, dtype=jnp.int32)[:, None, None]

    dests = []
    for x2, _, _, rtt in parts:
        T = x2.shape[0]
        flat = lambda a: jnp.swapaxes(a, 0, 1).reshape(a.shape[1], T)
        eid = flat(rtt[:, 0:2, :]).astype(jnp.int32)
        rank = flat(rtt[:, 4:6, :]).astype(jnp.int32)
        dest = rank + jnp.sum(
            jnp.where(eid[None] == experts, pad_start[:, None, None], 0), axis=0)
        nt = T // tm
        dests.append(dest.reshape(2, nt, tm).transpose(1, 0, 2).reshape(nt, 1, 2 * tm))
    dest_t = jnp.concatenate(dests, axis=0)
    xs = _dispatch(pad_end, padded, dest_t, [p[1] for p in parts], rows, tm, blk)
    yb = _moe(block_e, n_used, xs, w_gate, w_up, w_down, blk)
    return _combine(dest_t, [p[0] for p in parts], [p[2] for p in parts], yb, tm)


def _rope_table(pos):
    half = ROPE_DIM // 2
    inv = ROPE_THETA ** (-jnp.arange(0, ROPE_DIM, 2, dtype=F32) / ROPE_DIM)
    ang = pos.astype(F32)[:, None] * inv[None, :]
    cos, sin = jnp.cos(ang), jnp.sin(ang)
    L = pos.shape[0]
    pad = jnp.zeros((L, HEAD_DIM - ROPE_DIM), F32)
    zero = jnp.zeros((L, half), F32)
    c64 = jnp.concatenate([cos, cos, pad + 1.0], axis=1)
    lo64 = jnp.concatenate([-sin, zero, pad], axis=1)
    hi64 = jnp.concatenate([zero, sin, pad], axis=1)
    two = lambda t: jnp.concatenate([t, t], axis=1)
    return jnp.concatenate([two(c64), two(lo64), two(hi64)], axis=1)


def _mixers(x, pos_rope, kctx_prev, vctx_prev, h0r, h0i, mk, mv, wp, sp, cnt0, *,
            tm_in, tq, ssm_l, tm_mid):
    B, S, _ = x.shape
    T = B * S
    q, k3, v3, u_tm = _in_proj(x, wp["gmix"], wp["win"], wp["gq"], wp["gk"], pos_rope, tm_in)
    if kctx_prev is None:
        kctx, vctx = k3, v3
    else:
        kctx = jnp.concatenate([kctx_prev, k3], axis=1)
        vctx = jnp.concatenate([vctx_prev, v3], axis=1)
    att = _swa(wp["sink"], q, kctx, vctx, tq, mask_context=kctx_prev is None)
    ssm_tm, hr, hi = _ssm(u_tm.reshape(S, B, SSM_WIDTH), h0r, h0i, sp, ssm_l)
    x2, hn, rt, rtt, cnt = _mid(x, att, ssm_tm.reshape(S, B * SSM_WIDTH), mk, mv, wp, cnt0,
                                *tm_mid)
    part = (x2.reshape(T, D_MODEL), hn.reshape(T, D_MODEL), rt.reshape(T, LANES), rtt)
    return part, cnt, k3, v3, hr, hi


def kernel(x_prompt, x_sample, cache_attn_k, cache_attn_v, state_ssm_re, state_ssm_im, cache_mem_k, cache_mem_v, mem_prompt, norm_mix, w_in, q_norm, k_norm, attn_sink, ssm_lambda_re, ssm_lambda_im, ssm_log_dt, ssm_b_re, ssm_b_im, ssm_c_re, ssm_c_im, ssm_d, ssm_w_glu, ssm_b_glu, norm_attn_out, norm_ssm_out, w_out, norm_cross, norm_mem, w_cq, w_ck, w_cv, cq_norm, ck_norm, w_co, norm_ffn, w_router_group, b_router_group, w_router_expert, b_router_expert, w_e_gate, w_e_up, w_e_down):
    depth = norm_mix.shape[0]
    Bp, Lp, _ = x_prompt.shape
    Bs, Ls, _ = x_sample.shape
    yp, ys = x_prompt, x_sample
    rope_p = _rope_table(jnp.arange(Lp, dtype=jnp.int32))
    rope_s = _rope_table(PAST_LEN + jnp.arange(Ls, dtype=jnp.int32))
    outs = [[] for _ in range(10)]
    n_router = N_EXPERT_GROUPS + N_EXPERTS
    for l in range(depth):
        row = lambda a: a[l].astype(F32).reshape(1, -1)
        w_r = jnp.pad(jnp.concatenate([w_router_group[l], w_router_expert[l]], axis=1).astype(F32),
                      ((0, 0), (0, LANES - n_router)))
        w_r_hi = w_r.astype(BF16)
        w_r_lo = (w_r - w_r_hi.astype(F32)).astype(BF16)
        b_r = jnp.pad(jnp.concatenate([b_router_group[l], b_router_expert[l]]).astype(F32),
                      (0, LANES - n_router)).reshape(1, LANES)
        wp = {
            "gmix": row(norm_mix), "win": w_in[l].astype(BF16),
            "gq": jnp.tile(row(q_norm), (1, LANES // HEAD_DIM)),
            "gk": jnp.tile(row(k_norm), (1, LANES // HEAD_DIM)),
            "sink": attn_sink[l].astype(F32),
            "gao": row(norm_attn_out), "gso": row(norm_ssm_out),
            "wout": w_out[l].astype(BF16), "gx": row(norm_cross),
            "wcq": w_cq[l].astype(BF16), "gcq": row(cq_norm),
            "wco": w_co[l].astype(BF16), "gffn": row(norm_ffn),
            "wr": jnp.concatenate([w_r_hi, w_r_lo], axis=1), "br": b_r,
        }
        sp = _ssm_params(ssm_lambda_re[l], ssm_lambda_im[l], ssm_log_dt[l], ssm_b_re[l],
                         ssm_b_im[l], ssm_c_re[l], ssm_c_im[l], ssm_d[l], ssm_w_glu[l],
                         ssm_b_glu[l])
        ew = (w_e_gate[l].astype(F32), w_e_up[l].astype(F32), w_e_down[l].astype(F32))

        w_ckv = jnp.concatenate([w_ck[l], w_cv[l]], axis=1).astype(BF16)
        mkp, mvp = _memkv(mem_prompt.reshape(Bp * N_MEM, D_MODEL), row(norm_mem), w_ckv,
                          row(ck_norm), 512)
        mkp = mkp.reshape(Bp, N_MEM, CA_WIDTH)
        mvp = mvp.reshape(Bp, N_MEM, CA_WIDTH)

        zst = jnp.zeros((Bp, SSM_COLS), F32)
        part_p, cnt_p, kp, vp, hpr, hpi = _mixers(
            yp, rope_p, None, None, zst, zst, mkp, mvp, wp, sp, jnp.zeros((1, LANES), F32),
            tm_in=512, tq=256, ssm_l=64, tm_mid=(1, 512))
        part_s, cnt_s, kn, vn, hsr, hsi = _mixers(
            ys, rope_s, cache_attn_k[l].reshape(Bs, WINDOW, KV_WIDTH).astype(F32),
            cache_attn_v[l].reshape(Bs, WINDOW, KV_WIDTH).astype(F32),
            state_ssm_re[l].astype(F32).reshape(Bs, SSM_COLS),
            state_ssm_im[l].astype(F32).reshape(Bs, SSM_COLS),
            cache_mem_k[l].astype(F32).reshape(Bs, N_MEM, CA_WIDTH),
            cache_mem_v[l].astype(F32).reshape(Bs, N_MEM, CA_WIDTH), wp, sp, cnt_p,
            tm_in=Ls, tq=CHUNK, ssm_l=Ls, tm_mid=(8, Ls))
        yp, ys = _hier_moe([part_p, part_s], cnt_s, *ew, 256, 2 * MOE_BLOCK)
        yp = yp.reshape(Bp, Lp, D_MODEL)
        ys = ys.reshape(Bs, Ls, D_MODEL)

        sg = (N_SSM_GROUPS, SSM_STATE)
        kvs = (N_KV_HEADS, HEAD_DIM)
        vals = (kp[:, Lp - WINDOW:].reshape(Bp, WINDOW, *kvs),
                vp[:, Lp - WINDOW:].reshape(Bp, WINDOW, *kvs),
                hpr.reshape(Bp, *sg), hpi.reshape(Bp, *sg),
                mkp.reshape(Bp, N_MEM, CA_HEADS, CA_HEAD_DIM),
                mvp.reshape(Bp, N_MEM, CA_HEADS, CA_HEAD_DIM),
                kn.reshape(Bs, Ls, *kvs), vn.reshape(Bs, Ls, *kvs),
                hsr.reshape(Bs, *sg), hsi.reshape(Bs, *sg))
        for lst, val in zip(outs, vals):
            lst.append(val)
    return (yp, ys) + tuple(jnp.stack(lst) for lst in outs)
```

```python
import functools
import math

import jax
import jax.numpy as jnp
from jax import lax
from jax.experimental import pallas as pl
from jax.experimental.pallas import tpu as pltpu

F32 = jnp.float32
BF16 = jnp.bfloat16

D_MODEL = 1024
CHUNK = 64
N_Q_HEADS = 8
N_KV_HEADS = 2
GQA = N_Q_HEADS // N_KV_HEADS
HEAD_DIM = 64
WINDOW = 128
BAND = WINDOW + CHUNK
ROPE_DIM = HEAD_DIM // 4
ROPE_THETA = 500000.0
ATT_WIDTH = N_Q_HEADS * HEAD_DIM
KV_WIDTH = N_KV_HEADS * HEAD_DIM
SSM_GROUP = 16
SSM_WIDTH = D_MODEL // 2
N_SSM_GROUPS = SSM_WIDTH // SSM_GROUP
SSM_STATE = 64
SSM_COLS = N_SSM_GROUPS * SSM_STATE
IN_WIDTH = ATT_WIDTH + 2 * KV_WIDTH + SSM_WIDTH
N_MEM = 256
CA_HEADS = 4
CA_HEAD_DIM = 128
CA_WIDTH = CA_HEADS * CA_HEAD_DIM
N_EXPERT_GROUPS = 4
EXPERTS_PER_GROUP = 8
N_EXPERTS = N_EXPERT_GROUPS * EXPERTS_PER_GROUP
D_EXPERT = 512
MOE_BLOCK = 256
EPS = 1e-6
NEG = -1e30
PAST_LEN = 4096

LANES = 128
ROUTER_COL0 = N_EXPERT_GROUPS
VMEM_LIMIT = 48 * 1024 * 1024


def _rms(x, g):
    ms = jnp.mean(x * x, axis=-1, keepdims=True)
    return (x * lax.rsqrt(ms + EPS)) * g


def _mm(a, b):
    return jnp.dot(a, b, preferred_element_type=F32)


_HI_HALF = 0xFFFF0000


def _pack_bf16_pairs(x):
    half = x.shape[1] // 2
    bits = lambda v: lax.bitcast_convert_type(v.astype(BF16).astype(F32), jnp.uint32)
    return (lax.shift_right_logical(bits(x[:, :half]), jnp.uint32(16))
            | (bits(x[:, half:]) & jnp.uint32(_HI_HALF)))


def _unpack_bf16_pairs(w):
    lo = lax.bitcast_convert_type(lax.shift_left(w, jnp.uint32(16)), F32)
    hi = lax.bitcast_convert_type(w & jnp.uint32(_HI_HALF), F32)
    return jnp.concatenate([lo.astype(BF16), hi.astype(BF16)], axis=1)


def _in_proj_kernel(x_ref, g_ref, w_ref, gq_ref, gk_ref, rope_ref,
                    q_ref, k_ref, v_ref, u_ref):
    tm = x_ref.shape[1]
    h = _rms(x_ref[0], g_ref[...])
    hin = _mm(h.astype(BF16), w_ref[...])
    rope = rope_ref[...]
    cos = rope[:, 0:LANES]
    sin_lo = rope[:, LANES:2 * LANES]
    sin_hi = rope[:, 2 * LANES:3 * LANES]
    lane = lax.broadcasted_iota(jnp.int32, (tm, LANES), 1)
    left = lane < HEAD_DIM

    def norm_rope(z, g):
        sq = z * z
        lsum = jnp.sum(jnp.where(left, sq, 0.0), axis=-1, keepdims=True)
        rsum = jnp.sum(jnp.where(left, 0.0, sq), axis=-1, keepdims=True)
        ms = jnp.where(left, lsum, rsum) * (1.0 / HEAD_DIM)
        zn = (z * lax.rsqrt(ms + EPS)) * g
        half = ROPE_DIM // 2
        return (zn * cos + pltpu.roll(zn, LANES - half, 1) * sin_lo
                + pltpu.roll(zn, half, 1) * sin_hi)

    for j in range(ATT_WIDTH // LANES):
        sl = slice(j * LANES, (j + 1) * LANES)
        q_ref[0, :, sl] = norm_rope(hin[:, sl], gq_ref[...])
    k_ref[0] = norm_rope(hin[:, ATT_WIDTH:ATT_WIDTH + KV_WIDTH], gk_ref[...])
    v_ref[0] = hin[:, ATT_WIDTH + KV_WIDTH:ATT_WIDTH + 2 * KV_WIDTH]
    u_ref[...] = hin[:, ATT_WIDTH + 2 * KV_WIDTH:]


def _in_proj(x, g, w_bf, gq, gk, rope, tm):
    B, S, _ = x.shape
    full = lambda b, i: (0, 0)
    tile = lambda w: pl.BlockSpec((1, tm, w), lambda b, i: (b, i, 0))
    return pl.pallas_call(
        _in_proj_kernel,
        grid=(B, S // tm),
        in_specs=[
            tile(D_MODEL),
            pl.BlockSpec((1, D_MODEL), full),
            pl.BlockSpec((D_MODEL, IN_WIDTH), full),
            pl.BlockSpec((1, LANES), full),
            pl.BlockSpec((1, LANES), full),
            pl.BlockSpec((tm, 3 * LANES), lambda b, i: (i, 0)),
        ],
        out_specs=[
            tile(ATT_WIDTH), tile(KV_WIDTH), tile(KV_WIDTH),
            pl.BlockSpec((tm, SSM_WIDTH), lambda b, i: (i, b)),
        ],
        out_shape=[
            jax.ShapeDtypeStruct((B, S, ATT_WIDTH), F32),
            jax.ShapeDtypeStruct((B, S, KV_WIDTH), F32),
            jax.ShapeDtypeStruct((B, S, KV_WIDTH), F32),
            jax.ShapeDtypeStruct((S, B * SSM_WIDTH), F32),
        ],
        compiler_params=pltpu.CompilerParams(
            dimension_semantics=("arbitrary", "arbitrary"),
            vmem_limit_bytes=VMEM_LIMIT),
        name="in_proj",
    )(x, g, w_bf, gq, gk, rope)


def _swa_kernel(sink_ref, q_ref, k_ref, v_ref, o_ref, *, mask_context):
    tq = q_ref.shape[1]
    i = pl.program_id(1)
    nch = tq // CHUNK
    lane = lax.broadcasted_iota(jnp.int32, (BAND, LANES), 1)
    lo_half = lane < HEAD_DIM
    vrow_lo = lax.broadcasted_iota(jnp.int32, (LANES, BAND), 0) < HEAD_DIM
    q_lo = lax.broadcasted_iota(jnp.int32, (1, LANES), 1) < CHUNK
    slabs_per_kv = GQA * HEAD_DIM // LANES

    units = []
    scores = []
    vpads = {}
    for c in range(nch):
        chunk = i * nch + c
        if mask_context:
            first = jnp.maximum(chunk - WINDOW // CHUNK, 0)
            start = pl.multiple_of(first * CHUNK, CHUNK)
            kidx = start + lax.broadcasted_iota(jnp.int32, (BAND, LANES), 0)
            valid = kidx < (chunk + 1) * CHUNK
        else:
            start = pl.multiple_of(chunk * CHUNK, CHUNK)
        kb = k_ref[0, pl.ds(start, BAND), :]
        kb_sw = pltpu.roll(kb, HEAD_DIM, 1)
        vt = v_ref[0, pl.ds(start, BAND), :].T
        vt_sw = jnp.concatenate([vt[HEAD_DIM:], vt[:HEAD_DIM]], axis=0)
        for kvh in range(N_KV_HEADS):
            k_own, k_oth = (kb, kb_sw) if kvh == 0 else (kb_sw, kb)
            v_own, v_oth = (vt, vt_sw) if kvh == 0 else (vt_sw, vt)
            kpad = (jnp.where(lo_half, k_own, 0.0).astype(BF16),
                    jnp.where(lo_half, 0.0, k_oth).astype(BF16))
            vpads[(c, kvh)] = (jnp.where(vrow_lo, v_own, 0.0).astype(BF16),
                               jnp.where(vrow_lo, 0.0, v_oth).astype(BF16))
            col0 = kvh * GQA * HEAD_DIM
            q2 = jnp.concatenate(
                [q_ref[0, c * CHUNK:(c + 1) * CHUNK, col0 + m * LANES:col0 + (m + 1) * LANES]
                 for m in range(slabs_per_kv)], axis=0).astype(BF16)
            for side in range(2):
                s = lax.dot_general(kpad[side], q2, (((1,), (1,)), ((), ())),
                                    preferred_element_type=F32) * (HEAD_DIM ** -0.5)
                if mask_context:
                    s = jnp.where(valid, s, NEG)
                units.append((c, kvh, side))
                scores.append(s)

    sinks = [jnp.where(q_lo, sink_ref[kvh * GQA + side], sink_ref[kvh * GQA + 2 + side])
             for (_, kvh, side) in units]
    maxes = [jnp.maximum(jnp.max(s, axis=0, keepdims=True), sk)
             for s, sk in zip(scores, sinks)]
    exps = [jnp.exp(s - mx) for s, mx in zip(scores, maxes)]
    dens = [jnp.sum(p, axis=0, keepdims=True) + jnp.exp(sk - mx)
            for p, sk, mx in zip(exps, sinks, maxes)]
    probs = [(p * (1.0 / den)).astype(BF16) for p, den in zip(exps, dens)]

    for n in range(0, len(units), 2):
        c, kvh, _ = units[n]
        vp = vpads[(c, kvh)]
        o = (_mm(vp[0], probs[n]) + _mm(vp[1], probs[n + 1])).T
        col0 = kvh * GQA * HEAD_DIM
        for m in range(slabs_per_kv):
            o_ref[0, c * CHUNK:(c + 1) * CHUNK, col0 + m * LANES:col0 + (m + 1) * LANES] = (
                o[m * CHUNK:(m + 1) * CHUNK])


def _swa(sink, q, kctx, vctx, tq, mask_context):
    B, Sq, _ = q.shape
    Sk = kctx.shape[1]
    return pl.pallas_call(
        functools.partial(_swa_kernel, mask_context=mask_context),
        grid=(B, Sq // tq),
        in_specs=[
            pl.BlockSpec(memory_space=pltpu.SMEM),
            pl.BlockSpec((1, tq, ATT_WIDTH), lambda b, i: (b, i, 0)),
            pl.BlockSpec((1, Sk, KV_WIDTH), lambda b, i: (b, 0, 0)),
            pl.BlockSpec((1, Sk, KV_WIDTH), lambda b, i: (b, 0, 0)),
        ],
        out_specs=pl.BlockSpec((1, tq, ATT_WIDTH), lambda b, i: (b, i, 0)),
        out_shape=jax.ShapeDtypeStruct((B, Sq, ATT_WIDTH), F32),
        compiler_params=pltpu.CompilerParams(
            dimension_semantics=("arbitrary", "arbitrary"),
            vmem_limit_bytes=VMEM_LIMIT),
        name="swa",
    )(sink, q, kctx, vctx)


def _ssm_kernel(u_ref, h0r_ref, h0i_ref, lam_ref, bre_ref, bim_ref, cre_ref, cim_ref,
                d_ref, wglu_ref, bglu_ref,
                y_ref, hr_out, hi_out, sr, si, hr_s, hi_s):
    L, B, _ = u_ref.shape
    rows = L * B
    half_w = SSM_WIDTH // 2
    half_c = SSM_COLS // 2

    @pl.when(pl.program_id(0) == 0)
    def _():
        hr_s[...] = h0r_ref[...]
        hi_s[...] = h0i_ref[...]

    u = u_ref[...].reshape(rows, SSM_WIDTH)
    ub = u.astype(BF16)
    for hf in range(2):
        uh = ub[:, hf * half_w:(hf + 1) * half_w]
        sr[:, hf * half_c:(hf + 1) * half_c] = _mm(uh, bre_ref[hf])
        si[:, hf * half_c:(hf + 1) * half_c] = _mm(uh, bim_ref[hf])

    cw = 4 * LANES
    for cc in range(SSM_COLS // cw):
        cols = slice(cc * cw, (cc + 1) * cw)
        lr = jnp.broadcast_to(lam_ref[0:1, cols], (B, cw))
        li = jnp.broadcast_to(lam_ref[1:2, cols], (B, cw))

        def body(t, carry):
            hr, hi = carry
            at_t = pl.ds(pl.multiple_of(t * B, B), B)
            nr = lr * hr - li * hi + sr[at_t, cols]
            ni = lr * hi + li * hr + si[at_t, cols]
            sr[at_t, cols] = nr
            si[at_t, cols] = ni
            return nr, ni

        hr, hi = lax.fori_loop(0, L, body, (hr_s[:, cols], hi_s[:, cols]), unroll=2)
        hr_s[:, cols] = hr
        hi_s[:, cols] = hi

    ys = []
    for hf in range(2):
        cs = slice(hf * half_c, (hf + 1) * half_c)
        ys.append(_mm(sr[:, cs].astype(BF16), cre_ref[hf])
                  + _mm(si[:, cs].astype(BF16), cim_ref[hf]))
    y = jnp.concatenate(ys, axis=1) + d_ref[...] * u
    g = 0.5 * y * (1.0 + jnp.tanh(math.sqrt(2.0 / math.pi) * (y + 0.044715 * (y * y * y))))
    gb = g.astype(BF16)
    z = jnp.concatenate(
        [_mm(gb[:, hf * half_w:(hf + 1) * half_w], wglu_ref[hf]) for hf in range(2)],
        axis=1) + bglu_ref[...]
    out = g * (1.0 / (1.0 + jnp.exp(-z)))
    y_ref[...] = out.reshape(L, B, SSM_WIDTH)
    hr_out[...] = hr_s[...]
    hi_out[...] = hi_s[...]


def _ssm(u, h0r, h0i, sp, L):
    S, B, _ = u.shape
    c2 = lambda i: (0, 0)
    c3 = lambda i: (0, 0, 0)
    return pl.pallas_call(
        _ssm_kernel,
        grid=(S // L,),
        in_specs=[
            pl.BlockSpec((L, B, SSM_WIDTH), lambda i: (i, 0, 0)),
            pl.BlockSpec((B, SSM_COLS), c2),
            pl.BlockSpec((B, SSM_COLS), c2),
            pl.BlockSpec((2, SSM_COLS), c2),
            pl.BlockSpec((2, SSM_WIDTH // 2, SSM_COLS // 2), c3),
            pl.BlockSpec((2, SSM_WIDTH // 2, SSM_COLS // 2), c3),
            pl.BlockSpec((2, SSM_COLS // 2, SSM_WIDTH // 2), c3),
            pl.BlockSpec((2, SSM_COLS // 2, SSM_WIDTH // 2), c3),
            pl.BlockSpec((1, SSM_WIDTH), c2),
            pl.BlockSpec((2, SSM_WIDTH // 2, SSM_WIDTH // 2), c3),
            pl.BlockSpec((1, SSM_WIDTH), c2),
        ],
        out_specs=[
            pl.BlockSpec((L, B, SSM_WIDTH), lambda i: (i, 0, 0)),
            pl.BlockSpec((B, SSM_COLS), c2),
            pl.BlockSpec((B, SSM_COLS), c2),
        ],
        out_shape=[
            jax.ShapeDtypeStruct((S, B, SSM_WIDTH), F32),
            jax.ShapeDtypeStruct((B, SSM_COLS), F32),
            jax.ShapeDtypeStruct((B, SSM_COLS), F32),
        ],
        scratch_shapes=[
            pltpu.VMEM((L * B, SSM_COLS), F32),
            pltpu.VMEM((L * B, SSM_COLS), F32),
            pltpu.VMEM((B, SSM_COLS), F32),
            pltpu.VMEM((B, SSM_COLS), F32),
        ],
        compiler_params=pltpu.CompilerParams(
            dimension_semantics=("arbitrary",), vmem_limit_bytes=VMEM_LIMIT),
        name="ssm",
    )(u, h0r, h0i, sp["lam"], sp["bre"], sp["bim"], sp["cre"], sp["cim"],
      sp["d"], sp["wglu"], sp["bglu"])


def _block_diag(blocks):
    G, r, c = blocks.shape
    eye = jnp.eye(G, dtype=blocks.dtype)
    return jnp.einsum("grc,gh->grhc", blocks, eye).reshape(G * r, G * c)


def _ssm_params(lam_re, lam_im, log_dt, b_re, b_im, c_re, c_im, d, w_glu, b_glu):
    lam = lax.complex(lam_re.astype(F32), lam_im.astype(F32))
    dt = jnp.exp(log_dt.astype(F32))[:, None]
    lam_bar = jnp.exp(lam * dt)
    bmat = lax.complex(b_re.astype(F32), b_im.astype(F32))
    b_bar = ((lam_bar - 1.0) / lam)[..., None] * bmat
    lam2 = jnp.stack([lam_bar.real.reshape(-1), lam_bar.imag.reshape(-1)])
    bt = jnp.swapaxes(b_bar, 1, 2)
    hw, hc = SSM_WIDTH // 2, SSM_COLS // 2
    split_b = lambda m: jnp.stack([m[:hw, :hc], m[hw:, hc:]]).astype(BF16)
    split_c = lambda m: jnp.stack([m[:hc, :hw], m[hc:, hw:]]).astype(BF16)
    ct_re = jnp.swapaxes(c_re.astype(F32), 1, 2)
    ct_im = jnp.swapaxes(c_im.astype(F32), 1, 2)
    wg = _block_diag(w_glu.astype(F32))
    return {
        "lam": lam2,
        "bre": split_b(_block_diag(bt.real)),
        "bim": split_b(_block_diag(bt.imag)),
        "cre": split_c(_block_diag(ct_re)),
        "cim": split_c(_block_diag(-ct_im)),
        "d": d.astype(F32).reshape(1, SSM_WIDTH),
        "wglu": jnp.stack([wg[:hw, :hw], wg[hw:, hw:]]).astype(BF16),
        "bglu": b_glu.astype(F32).reshape(1, SSM_WIDTH),
    }


def _memkv_kernel(m_ref, g_ref, w_ref, gk_ref, k_ref, v_ref):
    m = _rms(m_ref[...], g_ref[...])
    kv = _mm(m.astype(BF16), w_ref[...])
    for h in range(CA_HEADS):
        sl = slice(h * CA_HEAD_DIM, (h + 1) * CA_HEAD_DIM)
        k_ref[:, sl] = _rms(kv[:, sl], gk_ref[...])
    v_ref[...] = kv[:, CA_WIDTH:]


def _memkv(mem2d, g, w_bf, gk, tm):
    T = mem2d.shape[0]
    full = lambda i: (0, 0)
    return pl.pallas_call(
        _memkv_kernel,
        grid=(T // tm,),
        in_specs=[
            pl.BlockSpec((tm, D_MODEL), lambda i: (i, 0)),
            pl.BlockSpec((1, D_MODEL), full),
            pl.BlockSpec((D_MODEL, 2 * CA_WIDTH), full),
            pl.BlockSpec((1, CA_HEAD_DIM), full),
        ],
        out_specs=[
            pl.BlockSpec((tm, CA_WIDTH), lambda i: (i, 0)),
            pl.BlockSpec((tm, CA_WIDTH), lambda i: (i, 0)),
        ],
        out_shape=[
            jax.ShapeDtypeStruct((T, CA_WIDTH), F32),
            jax.ShapeDtypeStruct((T, CA_WIDTH), F32),
        ],
        compiler_params=pltpu.CompilerParams(
            dimension_semantics=("arbitrary",), vmem_limit_bytes=VMEM_LIMIT),
        name="memkv",
    )(mem2d, g, w_bf, gk)


def _mid_kernel(x_ref, att_ref, ssm_ref, mk_ref, mv_ref,
                gao_ref, gso_ref, wout_ref, gx_ref, wcq_ref, gcq_ref, wco_ref,
                gffn_ref, wr_ref, br_ref, cnt0_ref,
                x2_ref, hn_ref, rt_ref, rtt_ref, cnt_ref, base_s):
    nb, ts, _ = x_ref.shape
    tm = nb * ts

    @pl.when((pl.program_id(0) == 0) & (pl.program_id(1) == 0))
    def _():
        base_s[...] = cnt0_ref[...]

    ssm = jnp.concatenate(
        [ssm_ref[:, b * SSM_WIDTH:(b + 1) * SSM_WIDTH] for b in range(nb)], axis=0)
    a = _rms(att_ref[...].reshape(tm, ATT_WIDTH), gao_ref[...]).astype(BF16)
    s = _rms(ssm, gso_ref[...]).astype(BF16)
    x1 = (x_ref[...].reshape(tm, D_MODEL) + _mm(a, wout_ref[0:ATT_WIDTH, :])
          + _mm(s, wout_ref[ATT_WIDTH:, :]))

    qx = _mm(_rms(x1, gx_ref[...]).astype(BF16), wcq_ref[...])
    heads = []
    for h in range(CA_HEADS):
        sl = slice(h * CA_HEAD_DIM, (h + 1) * CA_HEAD_DIM)
        qh = _rms(qx[:, sl], gcq_ref[...]).astype(BF16)
        per_batch = []
        for b in range(nb):
            kh = mk_ref[b, :, sl].astype(BF16)
            vh = mv_ref[b, :, sl].astype(BF16)
            sc = lax.dot_general(qh[b * ts:(b + 1) * ts], kh, (((1,), (1,)), ((), ())),
                                 preferred_element_type=F32) * (CA_HEAD_DIM ** -0.5)
            p = jnp.exp(sc - jnp.max(sc, axis=-1, keepdims=True))
            p = p / jnp.sum(p, axis=-1, keepdims=True)
            per_batch.append(_mm(p.astype(BF16), vh))
        heads.append(jnp.concatenate(per_batch, axis=0))
    o = jnp.concatenate(heads, axis=1).astype(BF16)
    x2 = x1 + _mm(o, wco_ref[...])
    x2_ref[...] = x2.reshape(nb, ts, D_MODEL)

    hn = _rms(x2, gffn_ref[...])
    hn_ref[...] = hn.reshape(nb, ts, D_MODEL)

    h_hi = hn.astype(BF16)
    h_lo = (hn - h_hi.astype(F32)).astype(BF16)
    r1 = _mm(h_hi, wr_ref[...])
    lg = (r1[:, :LANES] + r1[:, LANES:] + _mm(h_lo, wr_ref[:, 0:LANES])
          + br_ref[...])

    col = lax.broadcasted_iota(jnp.int32, (tm, LANES), 1)
    big = jnp.int32(4 * LANES)
    gmask = col < N_EXPERT_GROUPS
    lgg = jnp.where(gmask, lg, NEG)
    mg = jnp.max(lgg, axis=-1, keepdims=True)
    grp = jnp.min(jnp.where(gmask & (lgg == mg), col, big), axis=-1, keepdims=True)
    pg_top = 1.0 / jnp.sum(jnp.where(gmask, jnp.exp(lgg - mg), 0.0), axis=-1, keepdims=True)

    ecol = col - ROUTER_COL0
    emask = ((ecol >= 0) & (ecol < N_EXPERTS)
             & (lax.shift_right_arithmetic(ecol, 3) == grp))
    le = jnp.where(emask, lg, NEG)
    m1 = jnp.max(le, axis=-1, keepdims=True)
    i1 = jnp.min(jnp.where(emask & (le == m1), col, big), axis=-1, keepdims=True)
    rest = emask & (col != i1)
    le2 = jnp.where(rest, lg, NEG)
    m2 = jnp.max(le2, axis=-1, keepdims=True)
    i2 = jnp.min(jnp.where(rest & (le2 == m2), col, big), axis=-1, keepdims=True)
    den = jnp.sum(jnp.where(emask, jnp.exp(le - m1), 0.0), axis=-1, keepdims=True)
    p1 = 1.0 / den
    p2 = jnp.exp(m2 - m1) / den
    gate1 = pg_top * p1 / (p1 + p2)
    gate2 = pg_top * p2 / (p1 + p2)

    sel1 = col == i1
    sel2 = col == i2
    oh = jnp.where(sel1 | sel2, 1.0, 0.0)
    r_i = lax.broadcasted_iota(jnp.int32, (tm, tm), 0)
    c_i = lax.broadcasted_iota(jnp.int32, (tm, tm), 1)
    tri = jnp.where(r_i > c_i, 1.0, 0.0).astype(BF16)
    tot = base_s[...] + _mm(tri, oh.astype(BF16))
    rank1 = jnp.sum(jnp.where(sel1, tot, 0.0), axis=-1, keepdims=True)
    rank2 = jnp.sum(jnp.where(sel2, tot, 0.0), axis=-1, keepdims=True)
    base_s[...] = base_s[...] + jnp.sum(oh, axis=0, keepdims=True)
    cnt_ref[...] = base_s[...]

    e1 = (i1 - ROUTER_COL0).astype(F32)
    e2 = (i2 - ROUTER_COL0).astype(F32)
    rt = jnp.zeros((tm, LANES), F32)
    for k, val in enumerate((e1, e2, gate1, gate2, rank1, rank2)):
        rt = jnp.where(col == k, val, rt)
    rt_ref[...] = rt.reshape(nb, ts, LANES)
    for b in range(nb):
        rtt_ref[b] = rt[b * ts:(b + 1) * ts].T[0:8, :]


def _mid(x, att, ssm_tm, mk, mv, wp, cnt0, nb, ts):
    B, S, _ = x.shape
    c2 = lambda b, i: (0, 0)
    tile = lambda w: pl.BlockSpec((nb, ts, w), lambda b, i: (b, i, 0))
    return pl.pallas_call(
        _mid_kernel,
        grid=(B // nb, S // ts),
        in_specs=[
            tile(D_MODEL), tile(ATT_WIDTH),
            pl.BlockSpec((ts, nb * SSM_WIDTH), lambda b, i: (i, b)),
            pl.BlockSpec((nb, N_MEM, CA_WIDTH), lambda b, i: (b, 0, 0)),
            pl.BlockSpec((nb, N_MEM, CA_WIDTH), lambda b, i: (b, 0, 0)),
            pl.BlockSpec((1, ATT_WIDTH), c2),
            pl.BlockSpec((1, SSM_WIDTH), c2),
            pl.BlockSpec((ATT_WIDTH + SSM_WIDTH, D_MODEL), c2),
            pl.BlockSpec((1, D_MODEL), c2),
            pl.BlockSpec((D_MODEL, CA_WIDTH), c2),
            pl.BlockSpec((1, CA_HEAD_DIM), c2),
            pl.BlockSpec((CA_WIDTH, D_MODEL), c2),
            pl.BlockSpec((1, D_MODEL), c2),
            pl.BlockSpec((D_MODEL, 2 * LANES), c2),
            pl.BlockSpec((1, LANES), c2),
            pl.BlockSpec((1, LANES), c2),
        ],
        out_specs=[
            tile(D_MODEL), tile(D_MODEL), tile(LANES),
            pl.BlockSpec((nb, 8, ts), lambda b, i: (b, 0, i)),
            pl.BlockSpec((1, LANES), c2),
        ],
        out_shape=[
            jax.ShapeDtypeStruct((B, S, D_MODEL), F32),
            jax.ShapeDtypeStruct((B, S, D_MODEL), F32),
            jax.ShapeDtypeStruct((B, S, LANES), F32),
            jax.ShapeDtypeStruct((B, 8, S), F32),
            jax.ShapeDtypeStruct((1, LANES), F32),
        ],
        scratch_shapes=[pltpu.VMEM((1, LANES), F32)],
        compiler_params=pltpu.CompilerParams(
            dimension_semantics=("arbitrary", "arbitrary"),
            vmem_limit_bytes=VMEM_LIMIT),
        name="mid",
    )(x, att, ssm_tm, mk, mv, wp["gao"], wp["gso"], wp["wout"], wp["gx"], wp["wcq"],
      wp["gcq"], wp["wco"], wp["gffn"], wp["wr"], wp["br"], cnt0)


def _select_part(i, tile_starts, refs):
    x = refs[0][...]
    for start, ref in zip(tile_starts[1:], refs[1:]):
        x = jnp.where(i >= start, ref[...], x)
    return x


def _part_spec(shape, tile_start, n_tiles):
    def index(i, *_):
        return (jnp.clip(i - tile_start, 0, n_tiles - 1),) + (0,) * (len(shape) - 1)
    return pl.BlockSpec(shape, index)


def _dispatch_kernel(pend_ref, padded_ref, dest_ref, *rest, tile_starts):
    n_parts = len(tile_starts)
    hn_refs = rest[:n_parts]
    xs_hbm, stage, zbuf, sem = rest[n_parts:]
    tm = hn_refs[0].shape[0]
    i = pl.program_id(0)
    slot = lax.rem(i, 2)
    blk = zbuf.shape[0]

    def wait_rows(s):
        for _ in range(2):
            pltpu.make_async_copy(stage.at[s], xs_hbm.at[pl.ds(0, tm)], sem.at[s]).wait()

    @pl.when(i == 0)
    def _():
        zbuf[...] = jnp.zeros_like(zbuf)
        for e in range(N_EXPERTS):
            @pl.when(padded_ref[e] > 0)
            def _():
                row0 = pl.multiple_of(pend_ref[e] - blk, blk)
                fill = pltpu.make_async_copy(zbuf, xs_hbm.at[pl.ds(row0, blk)], sem.at[2])
                fill.start()
                fill.wait()

        def fill_tail(b, carry):
            fill = pltpu.make_async_copy(
                zbuf, xs_hbm.at[pl.ds(pl.multiple_of(b * blk, blk), blk)], sem.at[2])
            fill.start()
            fill.wait()
            return carry

        lax.fori_loop(pend_ref[N_EXPERTS - 1] // blk, xs_hbm.shape[0] // blk, fill_tail, 0)

    @pl.when(i >= 2)
    def _():
        wait_rows(slot)

    tile = _pack_bf16_pairs(_select_part(i, tile_starts, hn_refs))
    for s in range(2):
        @pl.when(slot == s)
        def _():
            stage[s] = tile
            for k in range(2):
                for r in range(tm):
                    pltpu.make_async_copy(stage.at[s, pl.ds(r, 1), :],
                                          xs_hbm.at[pl.ds(dest_ref[0, 0, k * tm + r], 1), :],
                                          sem.at[s]).start(priority=r % 2)

    @pl.when(i == pl.num_programs(0) - 1)
    def _():
        wait_rows(slot)

        @pl.when(i >= 1)
        def _():
            wait_rows(1 - slot)


def _tile_layout(arrays, tm):
    counts = [a.shape[0] // tm for a in arrays]
    starts = [sum(counts[:p]) for p in range(len(counts))]
    return counts, starts


def _dispatch(pad_end, padded, dest_t, hns, rows, tm, blk):
    counts, starts = _tile_layout(hns, tm)
    grid_spec = pltpu.PrefetchScalarGridSpec(
        num_scalar_prefetch=2,
        grid=(sum(counts),),
        in_specs=[pl.BlockSpec((1, 1, 2 * tm), lambda i, pe, pd: (i, 0, 0),
                               memory_space=pltpu.SMEM)]
        + [_part_spec((tm, D_MODEL), s, n) for s, n in zip(starts, counts)],
        out_specs=pl.BlockSpec(memory_space=pl.ANY),
        scratch_shapes=[
            pltpu.VMEM((2, tm, D_MODEL // 2), jnp.uint32),
            pltpu.VMEM((blk, D_MODEL // 2), jnp.uint32),
            pltpu.SemaphoreType.DMA((3,)),
        ],
    )
    return pl.pallas_call(
        functools.partial(_dispatch_kernel, tile_starts=tuple(starts)),
        grid_spec=grid_spec,
        out_shape=jax.ShapeDtypeStruct((rows, D_MODEL // 2), jnp.uint32),
        compiler_params=pltpu.CompilerParams(
            dimension_semantics=("arbitrary",), vmem_limit_bytes=VMEM_LIMIT),
        name="dispatch",
    )(pad_end, padded, dest_t, *hns)


def _moe_kernel(be_ref, nu_ref, xs_ref, wg_ref, wu_ref, wd_ref, yb_ref, wg_s, wu_s, wd_s):
    i = pl.program_id(0)

    @pl.when(i < nu_ref[0])
    def _():
        @pl.when((i == 0) | (be_ref[i] != be_ref[jnp.maximum(i - 1, 0)]))
        def _():
            wg_s[...] = wg_ref[0].astype(BF16)
            wu_s[...] = wu_ref[0].astype(BF16)
            wd_s[...] = wd_ref[0].astype(BF16)

        xe = _unpack_bf16_pairs(xs_ref[...])
        g = _mm(xe, wg_s[...])
        u = _mm(xe, wu_s[...])
        hmid = ((g * (1.0 / (1.0 + jnp.exp(-g)))) * u).astype(BF16)
        yb_ref[...] = _mm(hmid, wd_s[...])

    @pl.when(i >= nu_ref[0])
    def _():
        yb_ref[...] = jnp.zeros_like(yb_ref)


def _moe(block_e, n_used, xs, w_gate, w_up, w_down, blk):
    n_blocks = block_e.shape[0]
    in_blk = lambda i, be, nu: (jnp.maximum(jnp.minimum(i, nu[0] - 1), 0), 0)
    grid_spec = pltpu.PrefetchScalarGridSpec(
        num_scalar_prefetch=2,
        grid=(n_blocks,),
        in_specs=[
            pl.BlockSpec((blk, D_MODEL // 2), in_blk),
            pl.BlockSpec((1, D_MODEL, D_EXPERT), lambda i, be, nu: (be[i], 0, 0)),
            pl.BlockSpec((1, D_MODEL, D_EXPERT), lambda i, be, nu: (be[i], 0, 0)),
            pl.BlockSpec((1, D_EXPERT, D_MODEL), lambda i, be, nu: (be[i], 0, 0)),
        ],
        out_specs=pl.BlockSpec((blk, D_MODEL), lambda i, be, nu: (i, 0)),
        scratch_shapes=[
            pltpu.VMEM((D_MODEL, D_EXPERT), BF16),
            pltpu.VMEM((D_MODEL, D_EXPERT), BF16),
            pltpu.VMEM((D_EXPERT, D_MODEL), BF16),
        ],
    )
    return pl.pallas_call(
        _moe_kernel,
        grid_spec=grid_spec,
        out_shape=jax.ShapeDtypeStruct((xs.shape[0], D_MODEL), F32),
        compiler_params=pltpu.CompilerParams(
            dimension_semantics=("arbitrary",), vmem_limit_bytes=VMEM_LIMIT),
        name="moe",
    )(block_e, n_used, xs, w_gate, w_up, w_down)


def _combine_kernel(dest_ref, dest_next_ref, *rest, tile_starts):
    n_parts = len(tile_starts)
    x2_refs, rt_refs = rest[:n_parts], rest[n_parts:2 * n_parts]
    yb_hbm = rest[2 * n_parts]
    o_refs = rest[2 * n_parts + 1:3 * n_parts + 1]
    buf, sem = rest[3 * n_parts + 1:]
    tm = x2_refs[0].shape[0]
    i = pl.program_id(0)
    slot = lax.rem(i, 2)

    def gather(d_ref, s):
        for k in range(2):
            for r in range(tm):
                pltpu.make_async_copy(yb_hbm.at[pl.ds(d_ref[0, 0, k * tm + r], 1), :],
                                      buf.at[s, k, pl.ds(r, 1), :],
                                      sem.at[s]).start(priority=r % 2)

    @pl.when(i == 0)
    def _():
        gather(dest_ref, 0)

    for s in range(2):
        @pl.when((i + 1 < pl.num_programs(0)) & (slot == 1 - s))
        def _():
            gather(dest_next_ref, s)

    for k in range(2):
        pltpu.make_async_copy(yb_hbm.at[pl.ds(0, tm), :], buf.at[slot, k], sem.at[slot]).wait()
    rt = _select_part(i, tile_starts, rt_refs)
    out = (_select_part(i, tile_starts, x2_refs) + rt[:, 2:3] * buf[slot, 0]
           + rt[:, 3:4] * buf[slot, 1])
    ends = tile_starts[1:] + (pl.num_programs(0),)
    for start, end, o_ref in zip(tile_starts, ends, o_refs):
        @pl.when((i >= start) & (i < end))
        def _():
            o_ref[...] = out


def _combine(dest_t, x2s, rts, yb, tm):
    counts, starts = _tile_layout(x2s, tm)
    nt = sum(counts)
    spec = lambda w: [_part_spec((tm, w), s, n) for s, n in zip(starts, counts)]
    return pl.pallas_call(
        functools.partial(_combine_kernel, tile_starts=tuple(starts)),
        grid=(nt,),
        in_specs=[
            pl.BlockSpec((1, 1, 2 * tm), lambda i: (i, 0, 0), memory_space=pltpu.SMEM),
            pl.BlockSpec((1, 1, 2 * tm), lambda i: (jnp.minimum(i + 1, nt - 1), 0, 0),
                         memory_space=pltpu.SMEM),
        ] + spec(D_MODEL) + spec(LANES) + [pl.BlockSpec(memory_space=pl.ANY)],
        out_specs=spec(D_MODEL),
        out_shape=[jax.ShapeDtypeStruct(x2.shape, F32) for x2 in x2s],
        scratch_shapes=[
            pltpu.VMEM((2, 2, tm, D_MODEL), F32),
            pltpu.SemaphoreType.DMA((2,)),
        ],
        compiler_params=pltpu.CompilerParams(
            dimension_semantics=("arbitrary",), vmem_limit_bytes=VMEM_LIMIT),
        name="combine",
    )(dest_t, dest_t, *x2s, *rts, yb)


def _hier_moe(parts, cnt, w_gate, w_up, w_down, tm, blk):
    counts = cnt[0, ROUTER_COL0:ROUTER_COL0 + N_EXPERTS].astype(jnp.int32)
    padded = (counts + blk - 1) // blk * blk
    pad_end = jnp.cumsum(padded)
    pad_start = pad_end - padded
    t_all = sum(p[0].shape[0] for p in parts)
    n_blocks = (2 * t_all + N_EXPERTS * (blk - 1)) // blk + 1
    rows = n_blocks * blk
    blk_row0 = jnp.arange(n_blocks, dtype=jnp.int32) * blk
    block_e = jnp.minimum(
        jnp.sum((pad_end[None, :] <= blk_row0[:, None]).astype(jnp.int32), axis=1),
        N_EXPERTS - 1)
    n_used = (pad_end[-1] // blk).astype(jnp.int32).reshape(1)
    experts = jnp.arange(N_EXPERTS, dtype=jnp.int32)[:, None, None]

    dests = []
    for x2, _, _, rtt in parts:
        T = x2.shape[0]
        flat = lambda a: jnp.swapaxes(a, 0, 1).reshape(a.shape[1], T)
        eid = flat(rtt[:, 0:2, :]).astype(jnp.int32)
        rank = flat(rtt[:, 4:6, :]).astype(jnp.int32)
        dest = rank + jnp.sum(
            jnp.where(eid[None] == experts, pad_start[:, None, None], 0), axis=0)
        nt = T // tm
        dests.append(dest.reshape(2, nt, tm).transpose(1, 0, 2).reshape(nt, 1, 2 * tm))
    dest_t = jnp.concatenate(dests, axis=0)
    xs = _dispatch(pad_end, padded, dest_t, [p[1] for p in parts], rows, tm, blk)
    yb = _moe(block_e, n_used, xs, w_gate, w_up, w_down, blk)
    return _combine(dest_t, [p[0] for p in parts], [p[2] for p in parts], yb, tm)


def _rope_table(pos):
    half = ROPE_DIM // 2
    inv = ROPE_THETA ** (-jnp.arange(0, ROPE_DIM, 2, dtype=F32) / ROPE_DIM)
    ang = pos.astype(F32)[:, None] * inv[None, :]
    cos, sin = jnp.cos(ang), jnp.sin(ang)
    L = pos.shape[0]
    pad = jnp.zeros((L, HEAD_DIM - ROPE_DIM), F32)
    zero = jnp.zeros((L, half), F32)
    c64 = jnp.concatenate([cos, cos, pad + 1.0], axis=1)
    lo64 = jnp.concatenate([-sin, zero, pad], axis=1)
    hi64 = jnp.concatenate([zero, sin, pad], axis=1)
    two = lambda t: jnp.concatenate([t, t], axis=1)
    return jnp.concatenate([two(c64), two(lo64), two(hi64)], axis=1)


def _mixers(x, pos_rope, kctx_prev, vctx_prev, h0r, h0i, mk, mv, wp, sp, cnt0, *,
            tm_in, tq, ssm_l, tm_mid):
    B, S, _ = x.shape
    T = B * S
    q, k3, v3, u_tm = _in_proj(x, wp["gmix"], wp["win"], wp["gq"], wp["gk"], pos_rope, tm_in)
    if kctx_prev is None:
        kctx, vctx = k3, v3
    else:
        kctx = jnp.concatenate([kctx_prev, k3], axis=1)
        vctx = jnp.concatenate([vctx_prev, v3], axis=1)
    att = _swa(wp["sink"], q, kctx, vctx, tq, mask_context=kctx_prev is None)
    ssm_tm, hr, hi = _ssm(u_tm.reshape(S, B, SSM_WIDTH), h0r, h0i, sp, ssm_l)
    x2, hn, rt, rtt, cnt = _mid(x, att, ssm_tm.reshape(S, B * SSM_WIDTH), mk, mv, wp, cnt0,
                                *tm_mid)
    part = (x2.reshape(T, D_MODEL), hn.reshape(T, D_MODEL), rt.reshape(T, LANES), rtt)
    return part, cnt, k3, v3, hr, hi


def kernel(x_prompt, x_sample, cache_attn_k, cache_attn_v, state_ssm_re, state_ssm_im, cache_mem_k, cache_mem_v, mem_prompt, norm_mix, w_in, q_norm, k_norm, attn_sink, ssm_lambda_re, ssm_lambda_im, ssm_log_dt, ssm_b_re, ssm_b_im, ssm_c_re, ssm_c_im, ssm_d, ssm_w_glu, ssm_b_glu, norm_attn_out, norm_ssm_out, w_out, norm_cross, norm_mem, w_cq, w_ck, w_cv, cq_norm, ck_norm, w_co, norm_ffn, w_router_group, b_router_group, w_router_expert, b_router_expert, w_e_gate, w_e_up, w_e_down):
    depth = norm_mix.shape[0]
    Bp, Lp, _ = x_prompt.shape
    Bs, Ls, _ = x_sample.shape
    yp, ys = x_prompt, x_sample
    rope_p = _rope_table(jnp.arange(Lp, dtype=jnp.int32))
    rope_s = _rope_table(PAST_LEN + jnp.arange(Ls, dtype=jnp.int32))
    outs = [[] for _ in range(10)]
    n_router = N_EXPERT_GROUPS + N_EXPERTS
    for l in range(depth):
        row = lambda a: a[l].astype(F32).reshape(1, -1)
        w_r = jnp.pad(jnp.concatenate([w_router_group[l], w_router_expert[l]], axis=1).astype(F32),
                      ((0, 0), (0, LANES - n_router)))
        w_r_hi = w_r.astype(BF16)
        w_r_lo = (w_r - w_r_hi.astype(F32)).astype(BF16)
        b_r = jnp.pad(jnp.concatenate([b_router_group[l], b_router_expert[l]]).astype(F32),
                      (0, LANES - n_router)).reshape(1, LANES)
        wp = {
            "gmix": row(norm_mix), "win": w_in[l].astype(BF16),
            "gq": jnp.tile(row(q_norm), (1, LANES // HEAD_DIM)),
            "gk": jnp.tile(row(k_norm), (1, LANES // HEAD_DIM)),
            "sink": attn_sink[l].astype(F32),
            "gao": row(norm_attn_out), "gso": row(norm_ssm_out),
            "wout": w_out[l].astype(BF16), "gx": row(norm_cross),
            "wcq": w_cq[l].astype(BF16), "gcq": row(cq_norm),
            "wco": w_co[l].astype(BF16), "gffn": row(norm_ffn),
            "wr": jnp.concatenate([w_r_hi, w_r_lo], axis=1), "br": b_r,
        }
        sp = _ssm_params(ssm_lambda_re[l], ssm_lambda_im[l], ssm_log_dt[l], ssm_b_re[l],
                         ssm_b_im[l], ssm_c_re[l], ssm_c_im[l], ssm_d[l], ssm_w_glu[l],
                         ssm_b_glu[l])
        ew = (w_e_gate[l].astype(F32), w_e_up[l].astype(F32), w_e_down[l].astype(F32))

        w_ckv = jnp.concatenate([w_ck[l], w_cv[l]], axis=1).astype(BF16)
        mkp, mvp = _memkv(mem_prompt.reshape(Bp * N_MEM, D_MODEL), row(norm_mem), w_ckv,
                          row(ck_norm), 512)
        mkp = mkp.reshape(Bp, N_MEM, CA_WIDTH)
        mvp = mvp.reshape(Bp, N_MEM, CA_WIDTH)

        zst = jnp.zeros((Bp, SSM_COLS), F32)
        part_p, cnt_p, kp, vp, hpr, hpi = _mixers(
            yp, rope_p, None, None, zst, zst, mkp, mvp, wp, sp, jnp.zeros((1, LANES), F32),
            tm_in=512, tq=256, ssm_l=64, tm_mid=(1, 512))
        part_s, cnt_s, kn, vn, hsr, hsi = _mixers(
            ys, rope_s, cache_attn_k[l].reshape(Bs, WINDOW, KV_WIDTH).astype(F32),
            cache_attn_v[l].reshape(Bs, WINDOW, KV_WIDTH).astype(F32),
            state_ssm_re[l].astype(F32).reshape(Bs, SSM_COLS),
            state_ssm_im[l].astype(F32).reshape(Bs, SSM_COLS),
            cache_mem_k[l].astype(F32).reshape(Bs, N_MEM, CA_WIDTH),
            cache_mem_v[l].astype(F32).reshape(Bs, N_MEM, CA_WIDTH), wp, sp, cnt_p,
            tm_in=Ls, tq=CHUNK, ssm_l=Ls, tm_mid=(8, Ls))
        yp, ys = _hier_moe([part_p, part_s], cnt_s, *ew, 256, 2 * MOE_BLOCK)
        yp = yp.reshape(Bp, Lp, D_MODEL)
        ys = ys.reshape(Bs, Ls, D_MODEL)

        sg = (N_SSM_GROUPS, SSM_STATE)
        kvs = (N_KV_HEADS, HEAD_DIM)
        vals = (kp[:, Lp - WINDOW:].reshape(Bp, WINDOW, *kvs),
                vp[:, Lp - WINDOW:].reshape(Bp, WINDOW, *kvs),
                hpr.reshape(Bp, *sg), hpi.reshape(Bp, *sg),
                mkp.reshape(Bp, N_MEM, CA_HEADS, CA_HEAD_DIM),
                mvp.reshape(Bp, N_MEM, CA_HEADS, CA_HEAD_DIM),
                kn.reshape(Bs, Ls, *kvs), vn.reshape(Bs, Ls, *kvs),
                hsr.reshape(Bs, *sg), hsi.reshape(Bs, *sg))
        for lst, val in zip(outs, vals):
            lst.append(val)
    return (yp, ys) + tuple(jnp.stack(lst) for lst in outs)
```

```python
import functools
import math

import jax
import jax.numpy as jnp
from jax import lax
from jax.experimental import pallas as pl
from jax.experimental.pallas import tpu as pltpu

F32 = jnp.float32
BF16 = jnp.bfloat16

D_MODEL = 1024
CHUNK = 64
N_Q_HEADS = 8
N_KV_HEADS = 2
GQA = N_Q_HEADS // N_KV_HEADS
HEAD_DIM = 64
WINDOW = 128
BAND = WINDOW + CHUNK
ROPE_DIM = HEAD_DIM // 4
ROPE_THETA = 500000.0
ATT_WIDTH = N_Q_HEADS * HEAD_DIM
KV_WIDTH = N_KV_HEADS * HEAD_DIM
SSM_GROUP = 16
SSM_WIDTH = D_MODEL // 2
N_SSM_GROUPS = SSM_WIDTH // SSM_GROUP
SSM_STATE = 64
SSM_COLS = N_SSM_GROUPS * SSM_STATE
IN_WIDTH = ATT_WIDTH + 2 * KV_WIDTH + SSM_WIDTH
N_MEM = 256
CA_HEADS = 4
CA_HEAD_DIM = 128
CA_WIDTH = CA_HEADS * CA_HEAD_DIM
N_EXPERT_GROUPS = 4
EXPERTS_PER_GROUP = 8
N_EXPERTS = N_EXPERT_GROUPS * EXPERTS_PER_GROUP
D_EXPERT = 512
MOE_BLOCK = 256
EPS = 1e-6
NEG = -1e30
PAST_LEN = 4096

LANES = 128
ROUTER_COL0 = N_EXPERT_GROUPS
VMEM_LIMIT = 48 * 1024 * 1024


def _rms(x, g):
    ms = jnp.mean(x * x, axis=-1, keepdims=True)
    return (x * lax.rsqrt(ms + EPS)) * g


def _mm(a, b):
    return jnp.dot(a, b, preferred_element_type=F32)


_HI_HALF = 0xFFFF0000


def _pack_bf16_pairs(x):
    half = x.shape[1] // 2
    bits = lambda v: lax.bitcast_convert_type(v.astype(BF16).astype(F32), jnp.uint32)
    return (lax.shift_right_logical(bits(x[:, :half]), jnp.uint32(16))
            | (bits(x[:, half:]) & jnp.uint32(_HI_HALF)))


def _unpack_bf16_pairs(w):
    lo = lax.bitcast_convert_type(lax.shift_left(w, jnp.uint32(16)), F32)
    hi = lax.bitcast_convert_type(w & jnp.uint32(_HI_HALF), F32)
    return jnp.concatenate([lo.astype(BF16), hi.astype(BF16)], axis=1)


def _in_proj_kernel(x_ref, g_ref, w_ref, gq_ref, gk_ref, rope_ref,
                    q_ref, k_ref, v_ref, u_ref):
    tm = x_ref.shape[1]
    h = _rms(x_ref[0], g_ref[...])
    hin = _mm(h.astype(BF16), w_ref[...])
    rope = rope_ref[...]
    cos = rope[:, 0:LANES]
    sin_lo = rope[:, LANES:2 * LANES]
    sin_hi = rope[:, 2 * LANES:3 * LANES]
    lane = lax.broadcasted_iota(jnp.int32, (tm, LANES), 1)
    left = lane < HEAD_DIM

    def norm_rope(z, g):
        sq = z * z
        lsum = jnp.sum(jnp.where(left, sq, 0.0), axis=-1, keepdims=True)
        rsum = jnp.sum(jnp.where(left, 0.0, sq), axis=-1, keepdims=True)
        ms = jnp.where(left, lsum, rsum) * (1.0 / HEAD_DIM)
        zn = (z * lax.rsqrt(ms + EPS)) * g
        half = ROPE_DIM // 2
        return (zn * cos + pltpu.roll(zn, LANES - half, 1) * sin_lo
                + pltpu.roll(zn, half, 1) * sin_hi)

    for j in range(ATT_WIDTH // LANES):
        sl = slice(j * LANES, (j + 1) * LANES)
        q_ref[0, :, sl] = norm_rope(hin[:, sl], gq_ref[...])
    k_ref[0] = norm_rope(hin[:, ATT_WIDTH:ATT_WIDTH + KV_WIDTH], gk_ref[...])
    v_ref[0] = hin[:, ATT_WIDTH + KV_WIDTH:ATT_WIDTH + 2 * KV_WIDTH]
    u_ref[...] = hin[:, ATT_WIDTH + 2 * KV_WIDTH:]


def _in_proj(x, g, w_bf, gq, gk, rope, tm):
    B, S, _ = x.shape
    full = lambda b, i: (0, 0)
    tile = lambda w: pl.BlockSpec((1, tm, w), lambda b, i: (b, i, 0))
    return pl.pallas_call(
        _in_proj_kernel,
        grid=(B, S // tm),
        in_specs=[
            tile(D_MODEL),
            pl.BlockSpec((1, D_MODEL), full),
            pl.BlockSpec((D_MODEL, IN_WIDTH), full),
            pl.BlockSpec((1, LANES), full),
            pl.BlockSpec((1, LANES), full),
            pl.BlockSpec((tm, 3 * LANES), lambda b, i: (i, 0)),
        ],
        out_specs=[
            tile(ATT_WIDTH), tile(KV_WIDTH), tile(KV_WIDTH),
            pl.BlockSpec((tm, SSM_WIDTH), lambda b, i: (i, b)),
        ],
        out_shape=[
            jax.ShapeDtypeStruct((B, S, ATT_WIDTH), F32),
            jax.ShapeDtypeStruct((B, S, KV_WIDTH), F32),
            jax.ShapeDtypeStruct((B, S, KV_WIDTH), F32),
            jax.ShapeDtypeStruct((S, B * SSM_WIDTH), F32),
        ],
        compiler_params=pltpu.CompilerParams(
            dimension_semantics=("arbitrary", "arbitrary"),
            vmem_limit_bytes=VMEM_LIMIT),
        name="in_proj",
    )(x, g, w_bf, gq, gk, rope)


def _swa_kernel(sink_ref, q_ref, k_ref, v_ref, o_ref, *, mask_context):
    tq = q_ref.shape[1]
    i = pl.program_id(1)
    nch = tq // CHUNK
    lane = lax.broadcasted_iota(jnp.int32, (BAND, LANES), 1)
    lo_half = lane < HEAD_DIM
    vrow_lo = lax.broadcasted_iota(jnp.int32, (LANES, BAND), 0) < HEAD_DIM
    q_lo = lax.broadcasted_iota(jnp.int32, (1, LANES), 1) < CHUNK
    slabs_per_kv = GQA * HEAD_DIM // LANES

    units = []
    scores = []
    vpads = {}
    for c in range(nch):
        chunk = i * nch + c
        if mask_context:
            first = jnp.maximum(chunk - WINDOW // CHUNK, 0)
            start = pl.multiple_of(first * CHUNK, CHUNK)
            kidx = start + lax.broadcasted_iota(jnp.int32, (BAND, LANES), 0)
            valid = kidx < (chunk + 1) * CHUNK
        else:
            start = pl.multiple_of(chunk * CHUNK, CHUNK)
        kb = k_ref[0, pl.ds(start, BAND), :]
        kb_sw = pltpu.roll(kb, HEAD_DIM, 1)
        vt = v_ref[0, pl.ds(start, BAND), :].T
        vt_sw = jnp.concatenate([vt[HEAD_DIM:], vt[:HEAD_DIM]], axis=0)
        for kvh in range(N_KV_HEADS):
            k_own, k_oth = (kb, kb_sw) if kvh == 0 else (kb_sw, kb)
            v_own, v_oth = (vt, vt_sw) if kvh == 0 else (vt_sw, vt)
            kpad = (jnp.where(lo_half, k_own, 0.0).astype(BF16),
                    jnp.where(lo_half, 0.0, k_oth).astype(BF16))
            vpads[(c, kvh)] = (jnp.where(vrow_lo, v_own, 0.0).astype(BF16),
                               jnp.where(vrow_lo, 0.0, v_oth).astype(BF16))
            col0 = kvh * GQA * HEAD_DIM
            q2 = jnp.concatenate(
                [q_ref[0, c * CHUNK:(c + 1) * CHUNK, col0 + m * LANES:col0 + (m + 1) * LANES]
                 for m in range(slabs_per_kv)], axis=0).astype(BF16)
            for side in range(2):
                s = lax.dot_general(kpad[side], q2, (((1,), (1,)), ((), ())),
                                    preferred_element_type=F32) * (HEAD_DIM ** -0.5)
                if mask_context:
                    s = jnp.where(valid, s, NEG)
                units.append((c, kvh, side))
                scores.append(s)

    sinks = [jnp.where(q_lo, sink_ref[kvh * GQA + side], sink_ref[kvh * GQA + 2 + side])
             for (_, kvh, side) in units]
    maxes = [jnp.maximum(jnp.max(s, axis=0, keepdims=True), sk)
             for s, sk in zip(scores, sinks)]
    exps = [jnp.exp(s - mx) for s, mx in zip(scores, maxes)]
    dens = [jnp.sum(p, axis=0, keepdims=True) + jnp.exp(sk - mx)
            for p, sk, mx in zip(exps, sinks, maxes)]
    probs = [(p * (1.0 / den)).astype(BF16) for p, den in zip(exps, dens)]

    for n in range(0, len(units), 2):
        c, kvh, _ = units[n]
        vp = vpads[(c, kvh)]
        o = (_mm(vp[0], probs[n]) + _mm(vp[1], probs[n + 1])).T
        col0 = kvh * GQA * HEAD_DIM
        for m in range(slabs_per_kv):
            o_ref[0, c * CHUNK:(c + 1) * CHUNK, col0 + m * LANES:col0 + (m + 1) * LANES] = (
                o[m * CHUNK:(m + 1) * CHUNK])


def _swa(sink, q, kctx, vctx, tq, mask_context):
    B, Sq, _ = q.shape
    Sk = kctx.shape[1]
    return pl.pallas_call(
        functools.partial(_swa_kernel, mask_context=mask_context),
        grid=(B, Sq // tq),
        in_specs=[
            pl.BlockSpec(memory_space=pltpu.SMEM),
            pl.BlockSpec((1, tq, ATT_WIDTH), lambda b, i: (b, i, 0)),
            pl.BlockSpec((1, Sk, KV_WIDTH), lambda b, i: (b, 0, 0)),
            pl.BlockSpec((1, Sk, KV_WIDTH), lambda b, i: (b, 0, 0)),
        ],
        out_specs=pl.BlockSpec((1, tq, ATT_WIDTH), lambda b, i: (b, i, 0)),
        out_shape=jax.ShapeDtypeStruct((B, Sq, ATT_WIDTH), F32),
        compiler_params=pltpu.CompilerParams(
            dimension_semantics=("arbitrary", "arbitrary"),
            vmem_limit_bytes=VMEM_LIMIT),
        name="swa",
    )(sink, q, kctx, vctx)


def _ssm_kernel(u_ref, h0r_ref, h0i_ref, lam_ref, bre_ref, bim_ref, cre_ref, cim_ref,
                d_ref, wglu_ref, bglu_ref,
                y_ref, hr_out, hi_out, sr, si, hr_s, hi_s):
    L, B, _ = u_ref.shape
    rows = L * B
    half_w = SSM_WIDTH // 2
    half_c = SSM_COLS // 2

    @pl.when(pl.program_id(0) == 0)
    def _():
        hr_s[...] = h0r_ref[...]
        hi_s[...] = h0i_ref[...]

    u = u_ref[...].reshape(rows, SSM_WIDTH)
    ub = u.astype(BF16)
    for hf in range(2):
        uh = ub[:, hf * half_w:(hf + 1) * half_w]
        sr[:, hf * half_c:(hf + 1) * half_c] = _mm(uh, bre_ref[hf])
        si[:, hf * half_c:(hf + 1) * half_c] = _mm(uh, bim_ref[hf])

    cw = 4 * LANES
    for cc in range(SSM_COLS // cw):
        cols = slice(cc * cw, (cc + 1) * cw)
        lr = jnp.broadcast_to(lam_ref[0:1, cols], (B, cw))
        li = jnp.broadcast_to(lam_ref[1:2, cols], (B, cw))

        def body(t, carry):
            hr, hi = carry
            at_t = pl.ds(pl.multiple_of(t * B, B), B)
            nr = lr * hr - li * hi + sr[at_t, cols]
            ni = lr * hi + li * hr + si[at_t, cols]
            sr[at_t, cols] = nr
            si[at_t, cols] = ni
            return nr, ni

        hr, hi = lax.fori_loop(0, L, body, (hr_s[:, cols], hi_s[:, cols]), unroll=2)
        hr_s[:, cols] = hr
        hi_s[:, cols] = hi

    ys = []
    for hf in range(2):
        cs = slice(hf * half_c, (hf + 1) * half_c)
        ys.append(_mm(sr[:, cs].astype(BF16), cre_ref[hf])
                  + _mm(si[:, cs].astype(BF16), cim_ref[hf]))
    y = jnp.concatenate(ys, axis=1) + d_ref[...] * u
    g = 0.5 * y * (1.0 + jnp.tanh(math.sqrt(2.0 / math.pi) * (y + 0.044715 * (y * y * y))))
    gb = g.astype(BF16)
    z = jnp.concatenate(
        [_mm(gb[:, hf * half_w:(hf + 1) * half_w], wglu_ref[hf]) for hf in range(2)],
        axis=1) + bglu_ref[...]
    out = g * (1.0 / (1.0 + jnp.exp(-z)))
    y_ref[...] = out.reshape(L, B, SSM_WIDTH)
    hr_out[...] = hr_s[...]
    hi_out[...] = hi_s[...]


def _ssm(u, h0r, h0i, sp, L):
    S, B, _ = u.shape
    c2 = lambda i: (0, 0)
    c3 = lambda i: (0, 0, 0)
    return pl.pallas_call(
        _ssm_kernel,
        grid=(S // L,),
        in_specs=[
            pl.BlockSpec((L, B, SSM_WIDTH), lambda i: (i, 0, 0)),
            pl.BlockSpec((B, SSM_COLS), c2),
            pl.BlockSpec((B, SSM_COLS), c2),
            pl.BlockSpec((2, SSM_COLS), c2),
            pl.BlockSpec((2, SSM_WIDTH // 2, SSM_COLS // 2), c3),
            pl.BlockSpec((2, SSM_WIDTH // 2, SSM_COLS // 2), c3),
            pl.BlockSpec((2, SSM_COLS // 2, SSM_WIDTH // 2), c3),
            pl.BlockSpec((2, SSM_COLS // 2, SSM_WIDTH // 2), c3),
            pl.BlockSpec((1, SSM_WIDTH), c2),
            pl.BlockSpec((2, SSM_WIDTH // 2, SSM_WIDTH // 2), c3),
            pl.BlockSpec((1, SSM_WIDTH), c2),
        ],
        out_specs=[
            pl.BlockSpec((L, B, SSM_WIDTH), lambda i: (i, 0, 0)),
            pl.BlockSpec((B, SSM_COLS), c2),
            pl.BlockSpec((B, SSM_COLS), c2),
        ],
        out_shape=[
            jax.ShapeDtypeStruct((S, B, SSM_WIDTH), F32),
            jax.ShapeDtypeStruct((B, SSM_COLS), F32),
            jax.ShapeDtypeStruct((B, SSM_COLS), F32),
        ],
        scratch_shapes=[
            pltpu.VMEM((L * B, SSM_COLS), F32),
            pltpu.VMEM((L * B, SSM_COLS), F32),
            pltpu.VMEM((B, SSM_COLS), F32),
            pltpu.VMEM((B, SSM_COLS), F32),
        ],
        compiler_params=pltpu.CompilerParams(
            dimension_semantics=("arbitrary",), vmem_limit_bytes=VMEM_LIMIT),
        name="ssm",
    )(u, h0r, h0i, sp["lam"], sp["bre"], sp["bim"], sp["cre"], sp["cim"],
      sp["d"], sp["wglu"], sp["bglu"])


def _block_diag(blocks):
    G, r, c = blocks.shape
    eye = jnp.eye(G, dtype=blocks.dtype)
    return jnp.einsum("grc,gh->grhc", blocks, eye).reshape(G * r, G * c)


def _ssm_params(lam_re, lam_im, log_dt, b_re, b_im, c_re, c_im, d, w_glu, b_glu):
    lam = lax.complex(lam_re.astype(F32), lam_im.astype(F32))
    dt = jnp.exp(log_dt.astype(F32))[:, None]
    lam_bar = jnp.exp(lam * dt)
    bmat = lax.complex(b_re.astype(F32), b_im.astype(F32))
    b_bar = ((lam_bar - 1.0) / lam)[..., None] * bmat
    lam2 = jnp.stack([lam_bar.real.reshape(-1), lam_bar.imag.reshape(-1)])
    bt = jnp.swapaxes(b_bar, 1, 2)
    hw, hc = SSM_WIDTH // 2, SSM_COLS // 2
    split_b = lambda m: jnp.stack([m[:hw, :hc], m[hw:, hc:]]).astype(BF16)
    split_c = lambda m: jnp.stack([m[:hc, :hw], m[hc:, hw:]]).astype(BF16)
    ct_re = jnp.swapaxes(c_re.astype(F32), 1, 2)
    ct_im = jnp.swapaxes(c_im.astype(F32), 1, 2)
    wg = _block_diag(w_glu.astype(F32))
    return {
        "lam": lam2,
        "bre": split_b(_block_diag(bt.real)),
        "bim": split_b(_block_diag(bt.imag)),
        "cre": split_c(_block_diag(ct_re)),
        "cim": split_c(_block_diag(-ct_im)),
        "d": d.astype(F32).reshape(1, SSM_WIDTH),
        "wglu": jnp.stack([wg[:hw, :hw], wg[hw:, hw:]]).astype(BF16),
        "bglu": b_glu.astype(F32).reshape(1, SSM_WIDTH),
    }


def _memkv_kernel(m_ref, g_ref, w_ref, gk_ref, k_ref, v_ref):
    m = _rms(m_ref[...], g_ref[...])
    kv = _mm(m.astype(BF16), w_ref[...])
    for h in range(CA_HEADS):
        sl = slice(h * CA_HEAD_DIM, (h + 1) * CA_HEAD_DIM)
        k_ref[:, sl] = _rms(kv[:, sl], gk_ref[...])
    v_ref[...] = kv[:, CA_WIDTH:]


def _memkv(mem2d, g, w_bf, gk, tm):
    T = mem2d.shape[0]
    full = lambda i: (0, 0)
    return pl.pallas_call(
        _memkv_kernel,
        grid=(T // tm,),
        in_specs=[
            pl.BlockSpec((tm, D_MODEL), lambda i: (i, 0)),
            pl.BlockSpec((1, D_MODEL), full),
            pl.BlockSpec((D_MODEL, 2 * CA_WIDTH), full),
            pl.BlockSpec((1, CA_HEAD_DIM), full),
        ],
        out_specs=[
            pl.BlockSpec((tm, CA_WIDTH), lambda i: (i, 0)),
            pl.BlockSpec((tm, CA_WIDTH), lambda i: (i, 0)),
        ],
        out_shape=[
            jax.ShapeDtypeStruct((T, CA_WIDTH), F32),
            jax.ShapeDtypeStruct((T, CA_WIDTH), F32),
        ],
        compiler_params=pltpu.CompilerParams(
            dimension_semantics=("arbitrary",), vmem_limit_bytes=VMEM_LIMIT),
        name="memkv",
    )(mem2d, g, w_bf, gk)


def _mid_kernel(x_ref, att_ref, ssm_ref, mk_ref, mv_ref,
                gao_ref, gso_ref, wout_ref, gx_ref, wcq_ref, gcq_ref, wco_ref,
                gffn_ref, wr_ref, br_ref, cnt0_ref,
                x2_ref, hn_ref, rt_ref, rtt_ref, cnt_ref, base_s):
    nb, ts, _ = x_ref.shape
    tm = nb * ts

    @pl.when((pl.program_id(0) == 0) & (pl.program_id(1) == 0))
    def _():
        base_s[...] = cnt0_ref[...]

    ssm = jnp.concatenate(
        [ssm_ref[:, b * SSM_WIDTH:(b + 1) * SSM_WIDTH] for b in range(nb)], axis=0)
    a = _rms(att_ref[...].reshape(tm, ATT_WIDTH), gao_ref[...]).astype(BF16)
    s = _rms(ssm, gso_ref[...]).astype(BF16)
    x1 = (x_ref[...].reshape(tm, D_MODEL) + _mm(a, wout_ref[0:ATT_WIDTH, :])
          + _mm(s, wout_ref[ATT_WIDTH:, :]))

    qx = _mm(_rms(x1, gx_ref[...]).astype(BF16), wcq_ref[...])
    heads = []
    for h in range(CA_HEADS):
        sl = slice(h * CA_HEAD_DIM, (h + 1) * CA_HEAD_DIM)
        qh = _rms(qx[:, sl], gcq_ref[...]).astype(BF16)
        per_batch = []
        for b in range(nb):
            kh = mk_ref[b, :, sl].astype(BF16)
            vh = mv_ref[b, :, sl].astype(BF16)
            sc = lax.dot_general(qh[b * ts:(b + 1) * ts], kh, (((1,), (1,)), ((), ())),
                                 preferred_element_type=F32) * (CA_HEAD_DIM ** -0.5)
            p = jnp.exp(sc - jnp.max(sc, axis=-1, keepdims=True))
            p = p / jnp.sum(p, axis=-1, keepdims=True)
            per_batch.append(_mm(p.astype(BF16), vh))
        heads.append(jnp.concatenate(per_batch, axis=0))
    o = jnp.concatenate(heads, axis=1).astype(BF16)
    x2 = x1 + _mm(o, wco_ref[...])
    x2_ref[...] = x2.reshape(nb, ts, D_MODEL)

    hn = _rms(x2, gffn_ref[...])
    hn_ref[...] = hn.reshape(nb, ts, D_MODEL)

    h_hi = hn.astype(BF16)
    h_lo = (hn - h_hi.astype(F32)).astype(BF16)
    r1 = _mm(h_hi, wr_ref[...])
    lg = (r1[:, :LANES] + r1[:, LANES:] + _mm(h_lo, wr_ref[:, 0:LANES])
          + br_ref[...])

    col = lax.broadcasted_iota(jnp.int32, (tm, LANES), 1)
    big = jnp.int32(4 * LANES)
    gmask = col < N_EXPERT_GROUPS
    lgg = jnp.where(gmask, lg, NEG)
    mg = jnp.max(lgg, axis=-1, keepdims=True)
    grp = jnp.min(jnp.where(gmask & (lgg == mg), col, big), axis=-1, keepdims=True)
    pg_top = 1.0 / jnp.sum(jnp.where(gmask, jnp.exp(lgg - mg), 0.0), axis=-1, keepdims=True)

    ecol = col - ROUTER_COL0
    emask = ((ecol >= 0) & (ecol < N_EXPERTS)
             & (lax.shift_right_arithmetic(ecol, 3) == grp))
    le = jnp.where(emask, lg, NEG)
    m1 = jnp.max(le, axis=-1, keepdims=True)
    i1 = jnp.min(jnp.where(emask & (le == m1), col, big), axis=-1, keepdims=True)
    rest = emask & (col != i1)
    le2 = jnp.where(rest, lg, NEG)
    m2 = jnp.max(le2, axis=-1, keepdims=True)
    i2 = jnp.min(jnp.where(rest & (le2 == m2), col, big), axis=-1, keepdims=True)
    den = jnp.sum(jnp.where(emask, jnp.exp(le - m1), 0.0), axis=-1, keepdims=True)
    p1 = 1.0 / den
    p2 = jnp.exp(m2 - m1) / den
    gate1 = pg_top * p1 / (p1 + p2)
    gate2 = pg_top * p2 / (p1 + p2)

    sel1 = col == i1
    sel2 = col == i2
    oh = jnp.where(sel1 | sel2, 1.0, 0.0)
    r_i = lax.broadcasted_iota(jnp.int32, (tm, tm), 0)
    c_i = lax.broadcasted_iota(jnp.int32, (tm, tm), 1)
    tri = jnp.where(r_i > c_i, 1.0, 0.0).astype(BF16)
    tot = base_s[...] + _mm(tri, oh.astype(BF16))
    rank1 = jnp.sum(jnp.where(sel1, tot, 0.0), axis=-1, keepdims=True)
    rank2 = jnp.sum(jnp.where(sel2, tot, 0.0), axis=-1, keepdims=True)
    base_s[...] = base_s[...] + jnp.sum(oh, axis=0, keepdims=True)
    cnt_ref[...] = base_s[...]

    e1 = (i1 - ROUTER_COL0).astype(F32)
    e2 = (i2 - ROUTER_COL0).astype(F32)
    rt = jnp.zeros((tm, LANES), F32)
    for k, val in enumerate((e1, e2, gate1, gate2, rank1, rank2)):
        rt = jnp.where(col == k, val, rt)
    rt_ref[...] = rt.reshape(nb, ts, LANES)
    for b in range(nb):
        rtt_ref[b] = rt[b * ts:(b + 1) * ts].T[0:8, :]


def _mid(x, att, ssm_tm, mk, mv, wp, cnt0, nb, ts):
    B, S, _ = x.shape
    c2 = lambda b, i: (0, 0)
    tile = lambda w: pl.BlockSpec((nb, ts, w), lambda b, i: (b, i, 0))
    return pl.pallas_call(
        _mid_kernel,
        grid=(B // nb, S // ts),
        in_specs=[
            tile(D_MODEL), tile(ATT_WIDTH),
            pl.BlockSpec((ts, nb * SSM_WIDTH), lambda b, i: (i, b)),
            pl.BlockSpec((nb, N_MEM, CA_WIDTH), lambda b, i: (b, 0, 0)),
            pl.BlockSpec((nb, N_MEM, CA_WIDTH), lambda b, i: (b, 0, 0)),
            pl.BlockSpec((1, ATT_WIDTH), c2),
            pl.BlockSpec((1, SSM_WIDTH), c2),
            pl.BlockSpec((ATT_WIDTH + SSM_WIDTH, D_MODEL), c2),
            pl.BlockSpec((1, D_MODEL), c2),
            pl.BlockSpec((D_MODEL, CA_WIDTH), c2),
            pl.BlockSpec((1, CA_HEAD_DIM), c2),
            pl.BlockSpec((CA_WIDTH, D_MODEL), c2),
            pl.BlockSpec((1, D_MODEL), c2),
            pl.BlockSpec((D_MODEL, 2 * LANES), c2),
            pl.BlockSpec((1, LANES), c2),
            pl.BlockSpec((1, LANES), c2),
        ],
        out_specs=[
            tile(D_MODEL), tile(D_MODEL), tile(LANES),
            pl.BlockSpec((nb, 8, ts), lambda b, i: (b, 0, i)),
            pl.BlockSpec((1, LANES), c2),
        ],
        out_shape=[
            jax.ShapeDtypeStruct((B, S, D_MODEL), F32),
            jax.ShapeDtypeStruct((B, S, D_MODEL), F32),
            jax.ShapeDtypeStruct((B, S, LANES), F32),
            jax.ShapeDtypeStruct((B, 8, S), F32),
            jax.ShapeDtypeStruct((1, LANES), F32),
        ],
        scratch_shapes=[pltpu.VMEM((1, LANES), F32)],
        compiler_params=pltpu.CompilerParams(
            dimension_semantics=("arbitrary", "arbitrary"),
            vmem_limit_bytes=VMEM_LIMIT),
        name="mid",
    )(x, att, ssm_tm, mk, mv, wp["gao"], wp["gso"], wp["wout"], wp["gx"], wp["wcq"],
      wp["gcq"], wp["wco"], wp["gffn"], wp["wr"], wp["br"], cnt0)


def _select_part(i, tile_starts, refs):
    x = refs[0][...]
    for start, ref in zip(tile_starts[1:], refs[1:]):
        x = jnp.where(i >= start, ref[...], x)
    return x


def _part_spec(shape, tile_start, n_tiles):
    def index(i, *_):
        return (jnp.clip(i - tile_start, 0, n_tiles - 1),) + (0,) * (len(shape) - 1)
    return pl.BlockSpec(shape, index)


def _dispatch_kernel(pend_ref, padded_ref, dest_ref, *rest, tile_starts):
    n_parts = len(tile_starts)
    hn_refs = rest[:n_parts]
    xs_hbm, stage, zbuf, sem = rest[n_parts:]
    tm = hn_refs[0].shape[0]
    i = pl.program_id(0)
    slot = lax.rem(i, 2)
    blk = zbuf.shape[0]

    def wait_rows(s):
        for _ in range(2):
            pltpu.make_async_copy(stage.at[s], xs_hbm.at[pl.ds(0, tm)], sem.at[s]).wait()

    @pl.when(i == 0)
    def _():
        zbuf[...] = jnp.zeros_like(zbuf)
        for e in range(N_EXPERTS):
            @pl.when(padded_ref[e] > 0)
            def _():
                row0 = pl.multiple_of(pend_ref[e] - blk, blk)
                fill = pltpu.make_async_copy(zbuf, xs_hbm.at[pl.ds(row0, blk)], sem.at[2])
                fill.start()
                fill.wait()

        def fill_tail(b, carry):
            fill = pltpu.make_async_copy(
                zbuf, xs_hbm.at[pl.ds(pl.multiple_of(b * blk, blk), blk)], sem.at[2])
            fill.start()
            fill.wait()
            return carry

        lax.fori_loop(pend_ref[N_EXPERTS - 1] // blk, xs_hbm.shape[0] // blk, fill_tail, 0)

    @pl.when(i >= 2)
    def _():
        wait_rows(slot)

    tile = _pack_bf16_pairs(_select_part(i, tile_starts, hn_refs))
    for s in range(2):
        @pl.when(slot == s)
        def _():
            stage[s] = tile
            for k in range(2):
                for r in range(tm):
                    pltpu.make_async_copy(stage.at[s, pl.ds(r, 1), :],
                                          xs_hbm.at[pl.ds(dest_ref[0, 0, k * tm + r], 1), :],
                                          sem.at[s]).start(priority=r % 2)

    @pl.when(i == pl.num_programs(0) - 1)
    def _():
        wait_rows(slot)

        @pl.when(i >= 1)
        def _():
            wait_rows(1 - slot)


def _tile_layout(arrays, tm):
    counts = [a.shape[0] // tm for a in arrays]
    starts = [sum(counts[:p]) for p in range(len(counts))]
    return counts, starts


def _dispatch(pad_end, padded, dest_t, hns, rows, tm, blk):
    counts, starts = _tile_layout(hns, tm)
    grid_spec = pltpu.PrefetchScalarGridSpec(
        num_scalar_prefetch=2,
        grid=(sum(counts),),
        in_specs=[pl.BlockSpec((1, 1, 2 * tm), lambda i, pe, pd: (i, 0, 0),
                               memory_space=pltpu.SMEM)]
        + [_part_spec((tm, D_MODEL), s, n) for s, n in zip(starts, counts)],
        out_specs=pl.BlockSpec(memory_space=pl.ANY),
        scratch_shapes=[
            pltpu.VMEM((2, tm, D_MODEL // 2), jnp.uint32),
            pltpu.VMEM((blk, D_MODEL // 2), jnp.uint32),
            pltpu.SemaphoreType.DMA((3,)),
        ],
    )
    return pl.pallas_call(
        functools.partial(_dispatch_kernel, tile_starts=tuple(starts)),
        grid_spec=grid_spec,
        out_shape=jax.ShapeDtypeStruct((rows, D_MODEL // 2), jnp.uint32),
        compiler_params=pltpu.CompilerParams(
            dimension_semantics=("arbitrary",), vmem_limit_bytes=VMEM_LIMIT),
        name="dispatch",
    )(pad_end, padded, dest_t, *hns)


def _moe_kernel(be_ref, nu_ref, nxt_ref, xs_ref, wg_hbm, wu_hbm, wd_hbm, yb_ref,
                wg_f, wu_f, wd_f, wg_s, wu_s, wd_s, run_s, sem):
    i = pl.program_id(0)

    def fetch(e, slot):
        return [pltpu.make_async_copy(src.at[e], dst.at[slot], sem.at[slot])
                for src, dst in ((wg_hbm, wg_f), (wu_hbm, wu_f), (wd_hbm, wd_f))]

    @pl.when(i < nu_ref[0])
    def _():
        e = be_ref[i]

        @pl.when(i == 0)
        def _():
            run_s[0] = 0
            for c in fetch(e, 0):
                c.start()

        @pl.when((i == 0) | (e != be_ref[jnp.maximum(i - 1, 0)]))
        def _():
            slot = lax.rem(run_s[0], 2)
            run_s[0] = run_s[0] + 1
            for c in fetch(e, slot):
                c.wait()
            wg_s[...] = wg_f[slot].astype(BF16)
            wu_s[...] = wu_f[slot].astype(BF16)
            wd_s[...] = wd_f[slot].astype(BF16)

            @pl.when(nxt_ref[e] != e)
            def _():
                for c in fetch(nxt_ref[e], 1 - slot):
                    c.start()

        xe = _unpack_bf16_pairs(xs_ref[...])
        g = _mm(xe, wg_s[...])
        u = _mm(xe, wu_s[...])
        hmid = ((g * (1.0 / (1.0 + jnp.exp(-g)))) * u).astype(BF16)
        yb_ref[...] = _mm(hmid, wd_s[...])

    @pl.when(i >= nu_ref[0])
    def _():
        yb_ref[...] = jnp.zeros_like(yb_ref)


def _moe(block_e, n_used, next_e, xs, w_gate, w_up, w_down, blk):
    n_blocks = block_e.shape[0]
    in_blk = lambda i, be, nu, nx: (jnp.maximum(jnp.minimum(i, nu[0] - 1), 0), 0)
    grid_spec = pltpu.PrefetchScalarGridSpec(
        num_scalar_prefetch=3,
        grid=(n_blocks,),
        in_specs=[
            pl.BlockSpec((blk, D_MODEL // 2), in_blk),
            pl.BlockSpec(memory_space=pl.ANY),
            pl.BlockSpec(memory_space=pl.ANY),
            pl.BlockSpec(memory_space=pl.ANY),
        ],
        out_specs=pl.BlockSpec((blk, D_MODEL), lambda i, be, nu, nx: (i, 0)),
        scratch_shapes=[
            pltpu.VMEM((2, D_MODEL, D_EXPERT), F32),
            pltpu.VMEM((2, D_MODEL, D_EXPERT), F32),
            pltpu.VMEM((2, D_EXPERT, D_MODEL), F32),
            pltpu.VMEM((D_MODEL, D_EXPERT), BF16),
            pltpu.VMEM((D_MODEL, D_EXPERT), BF16),
            pltpu.VMEM((D_EXPERT, D_MODEL), BF16),
            pltpu.SMEM((1,), jnp.int32),
            pltpu.SemaphoreType.DMA((2,)),
        ],
    )
    return pl.pallas_call(
        _moe_kernel,
        grid_spec=grid_spec,
        out_shape=jax.ShapeDtypeStruct((xs.shape[0], D_MODEL), F32),
        compiler_params=pltpu.CompilerParams(
            dimension_semantics=("arbitrary",), vmem_limit_bytes=VMEM_LIMIT),
        name="moe",
    )(block_e, n_used, next_e, xs, w_gate, w_up, w_down)


def _combine_kernel(dest_ref, dest_next_ref, *rest, tile_starts):
    n_parts = len(tile_starts)
    x2_refs, rt_refs = rest[:n_parts], rest[n_parts:2 * n_parts]
    yb_hbm = rest[2 * n_parts]
    o_refs = rest[2 * n_parts + 1:3 * n_parts + 1]
    buf, sem = rest[3 * n_parts + 1:]
    tm = x2_refs[0].shape[0]
    i = pl.program_id(0)
    slot = lax.rem(i, 2)

    def gather(d_ref, s):
        for k in range(2):
            for r in range(tm):
                pltpu.make_async_copy(yb_hbm.at[pl.ds(d_ref[0, 0, k * tm + r], 1), :],
                                      buf.at[s, k, pl.ds(r, 1), :],
                                      sem.at[s]).start(priority=r % 2)

    @pl.when(i == 0)
    def _():
        gather(dest_ref, 0)

    for s in range(2):
        @pl.when((i + 1 < pl.num_programs(0)) & (slot == 1 - s))
        def _():
            gather(dest_next_ref, s)

    for k in range(2):
        pltpu.make_async_copy(yb_hbm.at[pl.ds(0, tm), :], buf.at[slot, k], sem.at[slot]).wait()
    rt = _select_part(i, tile_starts, rt_refs)
    out = (_select_part(i, tile_starts, x2_refs) + rt[:, 2:3] * buf[slot, 0]
           + rt[:, 3:4] * buf[slot, 1])
    ends = tile_starts[1:] + (pl.num_programs(0),)
    for start, end, o_ref in zip(tile_starts, ends, o_refs):
        @pl.when((i >= start) & (i < end))
        def _():
            o_ref[...] = out


def _combine(dest_t, x2s, rts, yb, tm):
    counts, starts = _tile_layout(x2s, tm)
    nt = sum(counts)
    spec = lambda w: [_part_spec((tm, w), s, n) for s, n in zip(starts, counts)]
    return pl.pallas_call(
        functools.partial(_combine_kernel, tile_starts=tuple(starts)),
        grid=(nt,),
        in_specs=[
            pl.BlockSpec((1, 1, 2 * tm), lambda i: (i, 0, 0), memory_space=pltpu.SMEM),
            pl.BlockSpec((1, 1, 2 * tm), lambda i: (jnp.minimum(i + 1, nt - 1), 0, 0),
                         memory_space=pltpu.SMEM),
        ] + spec(D_MODEL) + spec(LANES) + [pl.BlockSpec(memory_space=pl.ANY)],
        out_specs=spec(D_MODEL),
        out_shape=[jax.ShapeDtypeStruct(x2.shape, F32) for x2 in x2s],
        scratch_shapes=[
            pltpu.VMEM((2, 2, tm, D_MODEL), F32),
            pltpu.SemaphoreType.DMA((2,)),
        ],
        compiler_params=pltpu.CompilerParams(
            dimension_semantics=("arbitrary",), vmem_limit_bytes=VMEM_LIMIT),
        name="combine",
    )(dest_t, dest_t, *x2s, *rts, yb)


def _hier_moe(parts, cnt, w_gate, w_up, w_down, tm, blk):
    counts = cnt[0, ROUTER_COL0:ROUTER_COL0 + N_EXPERTS].astype(jnp.int32)
    padded = (counts + blk - 1) // blk * blk
    pad_end = jnp.cumsum(padded)
    pad_start = pad_end - padded
    t_all = sum(p[0].shape[0] for p in parts)
    n_blocks = (2 * t_all + N_EXPERTS * (blk - 1)) // blk + 1
    rows = n_blocks * blk
    blk_row0 = jnp.arange(n_blocks, dtype=jnp.int32) * blk
    block_e = jnp.minimum(
        jnp.sum((pad_end[None, :] <= blk_row0[:, None]).astype(jnp.int32), axis=1),
        N_EXPERTS - 1)
    n_used = (pad_end[-1] // blk).astype(jnp.int32).reshape(1)
    ids = jnp.arange(N_EXPERTS, dtype=jnp.int32)
    later = (ids[None, :] > ids[:, None]) & (padded[None, :] > 0)
    next_e = jnp.where(jnp.any(later, axis=1),
                       jnp.min(jnp.where(later, ids[None, :], N_EXPERTS), axis=1), ids)
    experts = jnp.arange(N_EXPERTS, dtype=jnp.int32)[:, None, None]

    dests = []
    for x2, _, _, rtt in parts:
        T = x2.shape[0]
        flat = lambda a: jnp.swapaxes(a, 0, 1).reshape(a.shape[1], T)
        eid = flat(rtt[:, 0:2, :]).astype(jnp.int32)
        rank = flat(rtt[:, 4:6, :]).astype(jnp.int32)
        dest = rank + jnp.sum(
            jnp.where(eid[None] == experts, pad_start[:, None, None], 0), axis=0)
        nt = T // tm
        dests.append(dest.reshape(2, nt, tm).transpose(1, 0, 2).reshape(nt, 1, 2 * tm))
    dest_t = jnp.concatenate(dests, axis=0)
    xs = _dispatch(pad_end, padded, dest_t, [p[1] for p in parts], rows, tm, blk)
    yb = _moe(block_e, n_used, next_e.astype(jnp.int32), xs, w_gate, w_up, w_down, blk)
    return _combine(dest_t, [p[0] for p in parts], [p[2] for p in parts], yb, tm)


def _rope_table(pos):
    half = ROPE_DIM // 2
    inv = ROPE_THETA ** (-jnp.arange(0, ROPE_DIM, 2, dtype=F32) / ROPE_DIM)
    ang = pos.astype(F32)[:, None] * inv[None, :]
    cos, sin = jnp.cos(ang), jnp.sin(ang)
    L = pos.shape[0]
    pad = jnp.zeros((L, HEAD_DIM - ROPE_DIM), F32)
    zero = jnp.zeros((L, half), F32)
    c64 = jnp.concatenate([cos, cos, pad + 1.0], axis=1)
    lo64 = jnp.concatenate([-sin, zero, pad], axis=1)
    hi64 = jnp.concatenate([zero, sin, pad], axis=1)
    two = lambda t: jnp.concatenate([t, t], axis=1)
    return jnp.concatenate([two(c64), two(lo64), two(hi64)], axis=1)


def _mixers(x, pos_rope, kctx_prev, vctx_prev, h0r, h0i, mk, mv, wp, sp, cnt0, *,
            tm_in, tq, ssm_l, tm_mid):
    B, S, _ = x.shape
    T = B * S
    q, k3, v3, u_tm = _in_proj(x, wp["gmix"], wp["win"], wp["gq"], wp["gk"], pos_rope, tm_in)
    if kctx_prev is None:
        kctx, vctx = k3, v3
    else:
        kctx = jnp.concatenate([kctx_prev, k3], axis=1)
        vctx = jnp.concatenate([vctx_prev, v3], axis=1)
    att = _swa(wp["sink"], q, kctx, vctx, tq, mask_context=kctx_prev is None)
    ssm_tm, hr, hi = _ssm(u_tm.reshape(S, B, SSM_WIDTH), h0r, h0i, sp, ssm_l)
    x2, hn, rt, rtt, cnt = _mid(x, att, ssm_tm.reshape(S, B * SSM_WIDTH), mk, mv, wp, cnt0,
                                *tm_mid)
    part = (x2.reshape(T, D_MODEL), hn.reshape(T, D_MODEL), rt.reshape(T, LANES), rtt)
    return part, cnt, k3, v3, hr, hi


def kernel(x_prompt, x_sample, cache_attn_k, cache_attn_v, state_ssm_re, state_ssm_im, cache_mem_k, cache_mem_v, mem_prompt, norm_mix, w_in, q_norm, k_norm, attn_sink, ssm_lambda_re, ssm_lambda_im, ssm_log_dt, ssm_b_re, ssm_b_im, ssm_c_re, ssm_c_im, ssm_d, ssm_w_glu, ssm_b_glu, norm_attn_out, norm_ssm_out, w_out, norm_cross, norm_mem, w_cq, w_ck, w_cv, cq_norm, ck_norm, w_co, norm_ffn, w_router_group, b_router_group, w_router_expert, b_router_expert, w_e_gate, w_e_up, w_e_down):
    depth = norm_mix.shape[0]
    Bp, Lp, _ = x_prompt.shape
    Bs, Ls, _ = x_sample.shape
    yp, ys = x_prompt, x_sample
    rope_p = _rope_table(jnp.arange(Lp, dtype=jnp.int32))
    rope_s = _rope_table(PAST_LEN + jnp.arange(Ls, dtype=jnp.int32))
    outs = [[] for _ in range(10)]
    n_router = N_EXPERT_GROUPS + N_EXPERTS
    for l in range(depth):
        row = lambda a: a[l].astype(F32).reshape(1, -1)
        w_r = jnp.pad(jnp.concatenate([w_router_group[l], w_router_expert[l]], axis=1).astype(F32),
                      ((0, 0), (0, LANES - n_router)))
        w_r_hi = w_r.astype(BF16)
        w_r_lo = (w_r - w_r_hi.astype(F32)).astype(BF16)
        b_r = jnp.pad(jnp.concatenate([b_router_group[l], b_router_expert[l]]).astype(F32),
                      (0, LANES - n_router)).reshape(1, LANES)
        wp = {
            "gmix": row(norm_mix), "win": w_in[l].astype(BF16),
            "gq": jnp.tile(row(q_norm), (1, LANES // HEAD_DIM)),
            "gk": jnp.tile(row(k_norm), (1, LANES // HEAD_DIM)),
            "sink": attn_sink[l].astype(F32),
            "gao": row(norm_attn_out), "gso": row(norm_ssm_out),
            "wout": w_out[l].astype(BF16), "gx": row(norm_cross),
            "wcq": w_cq[l].astype(BF16), "gcq": row(cq_norm),
            "wco": w_co[l].astype(BF16), "gffn": row(norm_ffn),
            "wr": jnp.concatenate([w_r_hi, w_r_lo], axis=1), "br": b_r,
        }
        sp = _ssm_params(ssm_lambda_re[l], ssm_lambda_im[l], ssm_log_dt[l], ssm_b_re[l],
                         ssm_b_im[l], ssm_c_re[l], ssm_c_im[l], ssm_d[l], ssm_w_glu[l],
                         ssm_b_glu[l])
        ew = (w_e_gate[l].astype(F32), w_e_up[l].astype(F32), w_e_down[l].astype(F32))

        w_ckv = jnp.concatenate([w_ck[l], w_cv[l]], axis=1).astype(BF16)
        mkp, mvp = _memkv(mem_prompt.reshape(Bp * N_MEM, D_MODEL), row(norm_mem), w_ckv,
                          row(ck_norm), 512)
        mkp = mkp.reshape(Bp, N_MEM, CA_WIDTH)
        mvp = mvp.reshape(Bp, N_MEM, CA_WIDTH)

        zst = jnp.zeros((Bp, SSM_COLS), F32)
        part_p, cnt_p, kp, vp, hpr, hpi = _mixers(
            yp, rope_p, None, None, zst, zst, mkp, mvp, wp, sp, jnp.zeros((1, LANES), F32),
            tm_in=512, tq=256, ssm_l=64, tm_mid=(1, 512))
        part_s, cnt_s, kn, vn, hsr, hsi = _mixers(
            ys, rope_s, cache_attn_k[l].reshape(Bs, WINDOW, KV_WIDTH).astype(F32),
            cache_attn_v[l].reshape(Bs, WINDOW, KV_WIDTH).astype(F32),
            state_ssm_re[l].astype(F32).reshape(Bs, SSM_COLS),
            state_ssm_im[l].astype(F32).reshape(Bs, SSM_COLS),
            cache_mem_k[l].astype(F32).reshape(Bs, N_MEM, CA_WIDTH),
            cache_mem_v[l].astype(F32).reshape(Bs, N_MEM, CA_WIDTH), wp, sp, cnt_p,
            tm_in=Ls, tq=CHUNK, ssm_l=Ls, tm_mid=(8, Ls))
        yp, ys = _hier_moe([part_p, part_s], cnt_s, *ew, 256, 2 * MOE_BLOCK)
        yp = yp.reshape(Bp, Lp, D_MODEL)
        ys = ys.reshape(Bs, Ls, D_MODEL)

        sg = (N_SSM_GROUPS, SSM_STATE)
        kvs = (N_KV_HEADS, HEAD_DIM)
        vals = (kp[:, Lp - WINDOW:].reshape(Bp, WINDOW, *kvs),
                vp[:, Lp - WINDOW:].reshape(Bp, WINDOW, *kvs),
                hpr.reshape(Bp, *sg), hpi.reshape(Bp, *sg),
                mkp.reshape(Bp, N_MEM, CA_HEADS, CA_HEAD_DIM),
                mvp.reshape(Bp, N_MEM, CA_HEADS, CA_HEAD_DIM),
                kn.reshape(Bs, Ls, *kvs), vn.reshape(Bs, Ls, *kvs),
                hsr.reshape(Bs, *sg), hsi.reshape(Bs, *sg))
        for lst, val in zip(outs, vals):
            lst.append(val)
    return (yp, ys) + tuple(jnp.stack(lst) for lst in outs)
```

```python
import functools
import math

import jax
import jax.numpy as jnp
from jax import lax
from jax.experimental import pallas as pl
from jax.experimental.pallas import tpu as pltpu

F32 = jnp.float32
BF16 = jnp.bfloat16

D_MODEL = 1024
CHUNK = 64
N_Q_HEADS = 8
N_KV_HEADS = 2
GQA = N_Q_HEADS // N_KV_HEADS
HEAD_DIM = 64
WINDOW = 128
BAND = WINDOW + CHUNK
ROPE_DIM = HEAD_DIM // 4
ROPE_THETA = 500000.0
ATT_WIDTH = N_Q_HEADS * HEAD_DIM
KV_WIDTH = N_KV_HEADS * HEAD_DIM
SSM_GROUP = 16
SSM_WIDTH = D_MODEL // 2
N_SSM_GROUPS = SSM_WIDTH // SSM_GROUP
SSM_STATE = 64
SSM_COLS = N_SSM_GROUPS * SSM_STATE
IN_WIDTH = ATT_WIDTH + 2 * KV_WIDTH + SSM_WIDTH
N_MEM = 256
CA_HEADS = 4
CA_HEAD_DIM = 128
CA_WIDTH = CA_HEADS * CA_HEAD_DIM
N_EXPERT_GROUPS = 4
EXPERTS_PER_GROUP = 8
N_EXPERTS = N_EXPERT_GROUPS * EXPERTS_PER_GROUP
D_EXPERT = 512
MOE_BLOCK = 256
EPS = 1e-6
NEG = -1e30
PAST_LEN = 4096

LANES = 128
ROUTER_COL0 = N_EXPERT_GROUPS
VMEM_LIMIT = 48 * 1024 * 1024


def _rms(x, g):
    ms = jnp.mean(x * x, axis=-1, keepdims=True)
    return (x * lax.rsqrt(ms + EPS)) * g


def _mm(a, b):
    return jnp.dot(a, b, preferred_element_type=F32)


_HI_HALF = 0xFFFF0000


def _pack_bf16_pairs(x):
    half = x.shape[1] // 2
    bits = lambda v: lax.bitcast_convert_type(v.astype(BF16).astype(F32), jnp.uint32)
    return (lax.shift_right_logical(bits(x[:, :half]), jnp.uint32(16))
            | (bits(x[:, half:]) & jnp.uint32(_HI_HALF)))


def _unpack_bf16_pairs(w):
    lo = lax.bitcast_convert_type(lax.shift_left(w, jnp.uint32(16)), F32)
    hi = lax.bitcast_convert_type(w & jnp.uint32(_HI_HALF), F32)
    return jnp.concatenate([lo.astype(BF16), hi.astype(BF16)], axis=1)


def _in_proj_kernel(x_ref, g_ref, w_ref, gq_ref, gk_ref, rope_ref,
                    q_ref, k_ref, v_ref, u_ref):
    nb, ts, _ = x_ref.shape
    tm = nb * ts
    h = _rms(x_ref[...].reshape(tm, D_MODEL), g_ref[...])
    hin = _mm(h.astype(BF16), w_ref[...])
    rope = jnp.concatenate([rope_ref[...]] * nb, axis=0)
    cos = rope[:, 0:LANES]
    sin_lo = rope[:, LANES:2 * LANES]
    sin_hi = rope[:, 2 * LANES:3 * LANES]
    lane = lax.broadcasted_iota(jnp.int32, (tm, LANES), 1)
    left = lane < HEAD_DIM

    def norm_rope(z, g):
        sq = z * z
        lsum = jnp.sum(jnp.where(left, sq, 0.0), axis=-1, keepdims=True)
        rsum = jnp.sum(jnp.where(left, 0.0, sq), axis=-1, keepdims=True)
        ms = jnp.where(left, lsum, rsum) * (1.0 / HEAD_DIM)
        zn = (z * lax.rsqrt(ms + EPS)) * g
        half = ROPE_DIM // 2
        return (zn * cos + pltpu.roll(zn, LANES - half, 1) * sin_lo
                + pltpu.roll(zn, half, 1) * sin_hi)

    for j in range(ATT_WIDTH // LANES):
        sl = slice(j * LANES, (j + 1) * LANES)
        q_ref[:, :, sl] = norm_rope(hin[:, sl], gq_ref[...]).reshape(nb, ts, LANES)
    k_ref[...] = norm_rope(hin[:, ATT_WIDTH:ATT_WIDTH + KV_WIDTH],
                           gk_ref[...]).reshape(nb, ts, KV_WIDTH)
    v_ref[...] = hin[:, ATT_WIDTH + KV_WIDTH:ATT_WIDTH + 2 * KV_WIDTH].reshape(nb, ts, KV_WIDTH)
    for b in range(nb):
        u_ref[:, b * SSM_WIDTH:(b + 1) * SSM_WIDTH] = (
            hin[b * ts:(b + 1) * ts, ATT_WIDTH + 2 * KV_WIDTH:])


def _in_proj(x, g, w_bf, gq, gk, rope, nb, ts):
    B, S, _ = x.shape
    full = lambda b, i: (0, 0)
    tile = lambda w: pl.BlockSpec((nb, ts, w), lambda b, i: (b, i, 0))
    return pl.pallas_call(
        _in_proj_kernel,
        grid=(B // nb, S // ts),
        in_specs=[
            tile(D_MODEL),
            pl.BlockSpec((1, D_MODEL), full),
            pl.BlockSpec((D_MODEL, IN_WIDTH), full),
            pl.BlockSpec((1, LANES), full),
            pl.BlockSpec((1, LANES), full),
            pl.BlockSpec((ts, 3 * LANES), lambda b, i: (i, 0)),
        ],
        out_specs=[
            tile(ATT_WIDTH), tile(KV_WIDTH), tile(KV_WIDTH),
            pl.BlockSpec((ts, nb * SSM_WIDTH), lambda b, i: (i, b)),
        ],
        out_shape=[
            jax.ShapeDtypeStruct((B, S, ATT_WIDTH), F32),
            jax.ShapeDtypeStruct((B, S, KV_WIDTH), F32),
            jax.ShapeDtypeStruct((B, S, KV_WIDTH), F32),
            jax.ShapeDtypeStruct((S, B * SSM_WIDTH), F32),
        ],
        compiler_params=pltpu.CompilerParams(
            dimension_semantics=("arbitrary", "arbitrary"),
            vmem_limit_bytes=VMEM_LIMIT),
        name="in_proj",
    )(x, g, w_bf, gq, gk, rope)


def _swa_kernel(sink_ref, q_ref, k_ref, v_ref, o_ref, *, mask_context):
    tq = q_ref.shape[1]
    i = pl.program_id(1)
    nch = tq // CHUNK
    lane = lax.broadcasted_iota(jnp.int32, (BAND, LANES), 1)
    lo_half = lane < HEAD_DIM
    vrow_lo = lax.broadcasted_iota(jnp.int32, (LANES, BAND), 0) < HEAD_DIM
    q_lo = lax.broadcasted_iota(jnp.int32, (1, LANES), 1) < CHUNK
    slabs_per_kv = GQA * HEAD_DIM // LANES

    units = []
    scores = []
    vpads = {}
    for c in range(nch):
        chunk = i * nch + c
        if mask_context:
            first = jnp.maximum(chunk - WINDOW // CHUNK, 0)
            start = pl.multiple_of(first * CHUNK, CHUNK)
            kidx = start + lax.broadcasted_iota(jnp.int32, (BAND, LANES), 0)
            valid = kidx < (chunk + 1) * CHUNK
        else:
            start = pl.multiple_of(chunk * CHUNK, CHUNK)
        kb = k_ref[0, pl.ds(start, BAND), :]
        kb_sw = pltpu.roll(kb, HEAD_DIM, 1)
        vt = v_ref[0, pl.ds(start, BAND), :].T
        vt_sw = jnp.concatenate([vt[HEAD_DIM:], vt[:HEAD_DIM]], axis=0)
        for kvh in range(N_KV_HEADS):
            k_own, k_oth = (kb, kb_sw) if kvh == 0 else (kb_sw, kb)
            v_own, v_oth = (vt, vt_sw) if kvh == 0 else (vt_sw, vt)
            kpad = (jnp.where(lo_half, k_own, 0.0).astype(BF16),
                    jnp.where(lo_half, 0.0, k_oth).astype(BF16))
            vpads[(c, kvh)] = (jnp.where(vrow_lo, v_own, 0.0).astype(BF16),
                               jnp.where(vrow_lo, 0.0, v_oth).astype(BF16))
            col0 = kvh * GQA * HEAD_DIM
            q2 = jnp.concatenate(
                [q_ref[0, c * CHUNK:(c + 1) * CHUNK, col0 + m * LANES:col0 + (m + 1) * LANES]
                 for m in range(slabs_per_kv)], axis=0).astype(BF16)
            for side in range(2):
                s = lax.dot_general(kpad[side], q2, (((1,), (1,)), ((), ())),
                                    preferred_element_type=F32) * (HEAD_DIM ** -0.5)
                if mask_context:
                    s = jnp.where(valid, s, NEG)
                units.append((c, kvh, side))
                scores.append(s)

    sinks = [jnp.where(q_lo, sink_ref[kvh * GQA + side], sink_ref[kvh * GQA + 2 + side])
             for (_, kvh, side) in units]
    maxes = [jnp.maximum(jnp.max(s, axis=0, keepdims=True), sk)
             for s, sk in zip(scores, sinks)]
    exps = [jnp.exp(s - mx) for s, mx in zip(scores, maxes)]
    dens = [jnp.sum(p, axis=0, keepdims=True) + jnp.exp(sk - mx)
            for p, sk, mx in zip(exps, sinks, maxes)]
    probs = [(p * (1.0 / den)).astype(BF16) for p, den in zip(exps, dens)]

    for n in range(0, len(units), 2):
        c, kvh, _ = units[n]
        vp = vpads[(c, kvh)]
        o = (_mm(vp[0], probs[n]) + _mm(vp[1], probs[n + 1])).T
        col0 = kvh * GQA * HEAD_DIM
        for m in range(slabs_per_kv):
            o_ref[0, c * CHUNK:(c + 1) * CHUNK, col0 + m * LANES:col0 + (m + 1) * LANES] = (
                o[m * CHUNK:(m + 1) * CHUNK])


def _swa(sink, q, kctx, vctx, tq, mask_context):
    B, Sq, _ = q.shape
    Sk = kctx.shape[1]
    return pl.pallas_call(
        functools.partial(_swa_kernel, mask_context=mask_context),
        grid=(B, Sq // tq),
        in_specs=[
            pl.BlockSpec(memory_space=pltpu.SMEM),
            pl.BlockSpec((1, tq, ATT_WIDTH), lambda b, i: (b, i, 0)),
            pl.BlockSpec((1, Sk, KV_WIDTH), lambda b, i: (b, 0, 0)),
            pl.BlockSpec((1, Sk, KV_WIDTH), lambda b, i: (b, 0, 0)),
        ],
        out_specs=pl.BlockSpec((1, tq, ATT_WIDTH), lambda b, i: (b, i, 0)),
        out_shape=jax.ShapeDtypeStruct((B, Sq, ATT_WIDTH), F32),
        compiler_params=pltpu.CompilerParams(
            dimension_semantics=("arbitrary", "arbitrary"),
            vmem_limit_bytes=VMEM_LIMIT),
        name="swa",
    )(sink, q, kctx, vctx)


def _ssm_kernel(u_ref, h0r_ref, h0i_ref, lam_ref, bre_ref, bim_ref, cre_ref, cim_ref,
                d_ref, wglu_ref, bglu_ref,
                y_ref, hr_out, hi_out, sr, si, hr_s, hi_s):
    L, B, _ = u_ref.shape
    rows = L * B
    half_w = SSM_WIDTH // 2
    half_c = SSM_COLS // 2

    @pl.when(pl.program_id(0) == 0)
    def _():
        hr_s[...] = h0r_ref[...]
        hi_s[...] = h0i_ref[...]

    u = u_ref[...].reshape(rows, SSM_WIDTH)
    ub = u.astype(BF16)
    for hf in range(2):
        uh = ub[:, hf * half_w:(hf + 1) * half_w]
        sr[:, hf * half_c:(hf + 1) * half_c] = _mm(uh, bre_ref[hf])
        si[:, hf * half_c:(hf + 1) * half_c] = _mm(uh, bim_ref[hf])

    cw = 4 * LANES
    for cc in range(SSM_COLS // cw):
        cols = slice(cc * cw, (cc + 1) * cw)
        lr = jnp.broadcast_to(lam_ref[0:1, cols], (B, cw))
        li = jnp.broadcast_to(lam_ref[1:2, cols], (B, cw))

        def body(t, carry):
            hr, hi = carry
            at_t = pl.ds(pl.multiple_of(t * B, B), B)
            nr = lr * hr - li * hi + sr[at_t, cols]
            ni = lr * hi + li * hr + si[at_t, cols]
            sr[at_t, cols] = nr
            si[at_t, cols] = ni
            return nr, ni

        hr, hi = lax.fori_loop(0, L, body, (hr_s[:, cols], hi_s[:, cols]), unroll=2)
        hr_s[:, cols] = hr
        hi_s[:, cols] = hi

    ys = []
    for hf in range(2):
        cs = slice(hf * half_c, (hf + 1) * half_c)
        ys.append(_mm(sr[:, cs].astype(BF16), cre_ref[hf])
                  + _mm(si[:, cs].astype(BF16), cim_ref[hf]))
    y = jnp.concatenate(ys, axis=1) + d_ref[...] * u
    g = 0.5 * y * (1.0 + jnp.tanh(math.sqrt(2.0 / math.pi) * (y + 0.044715 * (y * y * y))))
    gb = g.astype(BF16)
    z = jnp.concatenate(
        [_mm(gb[:, hf * half_w:(hf + 1) * half_w], wglu_ref[hf]) for hf in range(2)],
        axis=1) + bglu_ref[...]
    out = g * (1.0 / (1.0 + jnp.exp(-z)))
    y_ref[...] = out.reshape(L, B, SSM_WIDTH)
    hr_out[...] = hr_s[...]
    hi_out[...] = hi_s[...]


def _ssm(u, h0r, h0i, sp, L):
    S, B, _ = u.shape
    c2 = lambda i: (0, 0)
    c3 = lambda i: (0, 0, 0)
    return pl.pallas_call(
        _ssm_kernel,
        grid=(S // L,),
        in_specs=[
            pl.BlockSpec((L, B, SSM_WIDTH), lambda i: (i, 0, 0)),
            pl.BlockSpec((B, SSM_COLS), c2),
            pl.BlockSpec((B, SSM_COLS), c2),
            pl.BlockSpec((2, SSM_COLS), c2),
            pl.BlockSpec((2, SSM_WIDTH // 2, SSM_COLS // 2), c3),
            pl.BlockSpec((2, SSM_WIDTH // 2, SSM_COLS // 2), c3),
            pl.BlockSpec((2, SSM_COLS // 2, SSM_WIDTH // 2), c3),
            pl.BlockSpec((2, SSM_COLS // 2, SSM_WIDTH // 2), c3),
            pl.BlockSpec((1, SSM_WIDTH), c2),
            pl.BlockSpec((2, SSM_WIDTH // 2, SSM_WIDTH // 2), c3),
            pl.BlockSpec((1, SSM_WIDTH), c2),
        ],
        out_specs=[
            pl.BlockSpec((L, B, SSM_WIDTH), lambda i: (i, 0, 0)),
            pl.BlockSpec((B, SSM_COLS), c2),
            pl.BlockSpec((B, SSM_COLS), c2),
        ],
        out_shape=[
            jax.ShapeDtypeStruct((S, B, SSM_WIDTH), F32),
            jax.ShapeDtypeStruct((B, SSM_COLS), F32),
            jax.ShapeDtypeStruct((B, SSM_COLS), F32),
        ],
        scratch_shapes=[
            pltpu.VMEM((L * B, SSM_COLS), F32),
            pltpu.VMEM((L * B, SSM_COLS), F32),
            pltpu.VMEM((B, SSM_COLS), F32),
            pltpu.VMEM((B, SSM_COLS), F32),
        ],
        compiler_params=pltpu.CompilerParams(
            dimension_semantics=("arbitrary",), vmem_limit_bytes=VMEM_LIMIT),
        name="ssm",
    )(u, h0r, h0i, sp["lam"], sp["bre"], sp["bim"], sp["cre"], sp["cim"],
      sp["d"], sp["wglu"], sp["bglu"])


def _block_diag(blocks):
    G, r, c = blocks.shape
    eye = jnp.eye(G, dtype=blocks.dtype)
    return jnp.einsum("grc,gh->grhc", blocks, eye).reshape(G * r, G * c)


def _ssm_params(lam_re, lam_im, log_dt, b_re, b_im, c_re, c_im, d, w_glu, b_glu):
    lam = lax.complex(lam_re.astype(F32), lam_im.astype(F32))
    dt = jnp.exp(log_dt.astype(F32))[:, None]
    lam_bar = jnp.exp(lam * dt)
    bmat = lax.complex(b_re.astype(F32), b_im.astype(F32))
    b_bar = ((lam_bar - 1.0) / lam)[..., None] * bmat
    lam2 = jnp.stack([lam_bar.real.reshape(-1), lam_bar.imag.reshape(-1)])
    bt = jnp.swapaxes(b_bar, 1, 2)
    hw, hc = SSM_WIDTH // 2, SSM_COLS // 2
    split_b = lambda m: jnp.stack([m[:hw, :hc], m[hw:, hc:]]).astype(BF16)
    split_c = lambda m: jnp.stack([m[:hc, :hw], m[hc:, hw:]]).astype(BF16)
    ct_re = jnp.swapaxes(c_re.astype(F32), 1, 2)
    ct_im = jnp.swapaxes(c_im.astype(F32), 1, 2)
    wg = _block_diag(w_glu.astype(F32))
    return {
        "lam": lam2,
        "bre": split_b(_block_diag(bt.real)),
        "bim": split_b(_block_diag(bt.imag)),
        "cre": split_c(_block_diag(ct_re)),
        "cim": split_c(_block_diag(-ct_im)),
        "d": d.astype(F32).reshape(1, SSM_WIDTH),
        "wglu": jnp.stack([wg[:hw, :hw], wg[hw:, hw:]]).astype(BF16),
        "bglu": b_glu.astype(F32).reshape(1, SSM_WIDTH),
    }


def _memkv_kernel(m_ref, g_ref, w_ref, gk_ref, k_ref, v_ref):
    m = _rms(m_ref[...], g_ref[...])
    kv = _mm(m.astype(BF16), w_ref[...])
    for h in range(CA_HEADS):
        sl = slice(h * CA_HEAD_DIM, (h + 1) * CA_HEAD_DIM)
        k_ref[:, sl] = _rms(kv[:, sl], gk_ref[...])
    v_ref[...] = kv[:, CA_WIDTH:]


def _memkv(mem2d, g, w_bf, gk, tm):
    T = mem2d.shape[0]
    full = lambda i: (0, 0)
    return pl.pallas_call(
        _memkv_kernel,
        grid=(T // tm,),
        in_specs=[
            pl.BlockSpec((tm, D_MODEL), lambda i: (i, 0)),
            pl.BlockSpec((1, D_MODEL), full),
            pl.BlockSpec((D_MODEL, 2 * CA_WIDTH), full),
            pl.BlockSpec((1, CA_HEAD_DIM), full),
        ],
        out_specs=[
            pl.BlockSpec((tm, CA_WIDTH), lambda i: (i, 0)),
            pl.BlockSpec((tm, CA_WIDTH), lambda i: (i, 0)),
        ],
        out_shape=[
            jax.ShapeDtypeStruct((T, CA_WIDTH), F32),
            jax.ShapeDtypeStruct((T, CA_WIDTH), F32),
        ],
        compiler_params=pltpu.CompilerParams(
            dimension_semantics=("arbitrary",), vmem_limit_bytes=VMEM_LIMIT),
        name="memkv",
    )(mem2d, g, w_bf, gk)


def _mid_kernel(x_ref, att_ref, ssm_ref, mk_ref, mv_ref,
                gao_ref, gso_ref, wout_ref, gx_ref, wcq_ref, gcq_ref, wco_ref,
                gffn_ref, wr_ref, br_ref, cnt0_ref, tri_ref,
                x2_ref, hn_ref, rt_ref, rtt_ref, cnt_ref, base_s):
    nb, ts, _ = x_ref.shape
    tm = nb * ts

    @pl.when((pl.program_id(0) == 0) & (pl.program_id(1) == 0))
    def _():
        base_s[...] = cnt0_ref[...]

    ssm = jnp.concatenate(
        [ssm_ref[:, b * SSM_WIDTH:(b + 1) * SSM_WIDTH] for b in range(nb)], axis=0)
    a = _rms(att_ref[...].reshape(tm, ATT_WIDTH), gao_ref[...]).astype(BF16)
    s = _rms(ssm, gso_ref[...]).astype(BF16)
    x1 = (x_ref[...].reshape(tm, D_MODEL) + _mm(a, wout_ref[0:ATT_WIDTH, :])
          + _mm(s, wout_ref[ATT_WIDTH:, :]))

    qx = _mm(_rms(x1, gx_ref[...]).astype(BF16), wcq_ref[...])
    heads = []
    for h in range(CA_HEADS):
        sl = slice(h * CA_HEAD_DIM, (h + 1) * CA_HEAD_DIM)
        qh = _rms(qx[:, sl], gcq_ref[...]).astype(BF16)
        per_batch = []
        for b in range(nb):
            kh = mk_ref[b, :, sl].astype(BF16)
            qb = qh[b * ts:(b + 1) * ts]
            if ts <= LANES:
                vt = mv_ref[b, :, sl].T.astype(BF16)
                sc = lax.dot_general(kh, qb, (((1,), (1,)), ((), ())),
                                     preferred_element_type=F32) * (CA_HEAD_DIM ** -0.5)
                p = jnp.exp(sc - jnp.max(sc, axis=0, keepdims=True))
                p = p * (1.0 / jnp.sum(p, axis=0, keepdims=True))
                per_batch.append(_mm(vt, p.astype(BF16)).T)
            else:
                vh = mv_ref[b, :, sl].astype(BF16)
                sc = lax.dot_general(qb, kh, (((1,), (1,)), ((), ())),
                                     preferred_element_type=F32) * (CA_HEAD_DIM ** -0.5)
                p = jnp.exp(sc - jnp.max(sc, axis=-1, keepdims=True))
                p = p / jnp.sum(p, axis=-1, keepdims=True)
                per_batch.append(_mm(p.astype(BF16), vh))
        heads.append(jnp.concatenate(per_batch, axis=0))
    o = jnp.concatenate(heads, axis=1).astype(BF16)
    x2 = x1 + _mm(o, wco_ref[...])
    x2_ref[...] = x2.reshape(nb, ts, D_MODEL)

    hn = _rms(x2, gffn_ref[...])
    hn_ref[...] = hn.reshape(nb, ts, D_MODEL)

    h_hi = hn.astype(BF16)
    h_lo = (hn - h_hi.astype(F32)).astype(BF16)
    r1 = _mm(h_hi, wr_ref[...])
    lg = (r1[:, :LANES] + r1[:, LANES:] + _mm(h_lo, wr_ref[:, 0:LANES])
          + br_ref[...])

    col = lax.broadcasted_iota(jnp.int32, (tm, LANES), 1)
    big = jnp.int32(4 * LANES)
    gmask = col < N_EXPERT_GROUPS
    lgg = jnp.where(gmask, lg, NEG)
    mg = jnp.max(lgg, axis=-1, keepdims=True)
    grp = jnp.min(jnp.where(gmask & (lgg == mg), col, big), axis=-1, keepdims=True)
    pg_top = 1.0 / jnp.sum(jnp.where(gmask, jnp.exp(lgg - mg), 0.0), axis=-1, keepdims=True)

    ecol = col - ROUTER_COL0
    emask = ((ecol >= 0) & (ecol < N_EXPERTS)
             & (lax.shift_right_arithmetic(ecol, 3) == grp))
    le = jnp.where(emask, lg, NEG)
    m1 = jnp.max(le, axis=-1, keepdims=True)
    i1 = jnp.min(jnp.where(emask & (le == m1), col, big), axis=-1, keepdims=True)
    rest = emask & (col != i1)
    le2 = jnp.where(rest, lg, NEG)
    m2 = jnp.max(le2, axis=-1, keepdims=True)
    i2 = jnp.min(jnp.where(rest & (le2 == m2), col, big), axis=-1, keepdims=True)
    den = jnp.sum(jnp.where(emask, jnp.exp(le - m1), 0.0), axis=-1, keepdims=True)
    p1 = 1.0 / den
    p2 = jnp.exp(m2 - m1) / den
    gate1 = pg_top * p1 / (p1 + p2)
    gate2 = pg_top * p2 / (p1 + p2)

    sel1 = col == i1
    sel2 = col == i2
    oh = jnp.where(sel1 | sel2, 1.0, 0.0)
    tot = base_s[...] + _mm(tri_ref[...], oh.astype(BF16))
    rank1 = jnp.sum(jnp.where(sel1, tot, 0.0), axis=-1, keepdims=True)
    rank2 = jnp.sum(jnp.where(sel2, tot, 0.0), axis=-1, keepdims=True)
    base_s[...] = base_s[...] + jnp.sum(oh, axis=0, keepdims=True)
    cnt_ref[...] = base_s[...]

    e1 = (i1 - ROUTER_COL0).astype(F32)
    e2 = (i2 - ROUTER_COL0).astype(F32)
    rt = jnp.zeros((tm, LANES), F32)
    for k, val in enumerate((e1, e2, gate1, gate2, rank1, rank2)):
        rt = jnp.where(col == k, val, rt)
    rt_ref[...] = rt.reshape(nb, ts, LANES)
    for b in range(nb):
        rtt_ref[b] = rt[b * ts:(b + 1) * ts].T[0:8, :]


def _mid(x, att, ssm_tm, mk, mv, wp, cnt0, nb, ts):
    B, S, _ = x.shape
    c2 = lambda b, i: (0, 0)
    tile = lambda w: pl.BlockSpec((nb, ts, w), lambda b, i: (b, i, 0))
    return pl.pallas_call(
        _mid_kernel,
        grid=(B // nb, S // ts),
        in_specs=[
            tile(D_MODEL), tile(ATT_WIDTH),
            pl.BlockSpec((ts, nb * SSM_WIDTH), lambda b, i: (i, b)),
            pl.BlockSpec((nb, N_MEM, CA_WIDTH), lambda b, i: (b, 0, 0)),
            pl.BlockSpec((nb, N_MEM, CA_WIDTH), lambda b, i: (b, 0, 0)),
            pl.BlockSpec((1, ATT_WIDTH), c2),
            pl.BlockSpec((1, SSM_WIDTH), c2),
            pl.BlockSpec((ATT_WIDTH + SSM_WIDTH, D_MODEL), c2),
            pl.BlockSpec((1, D_MODEL), c2),
            pl.BlockSpec((D_MODEL, CA_WIDTH), c2),
            pl.BlockSpec((1, CA_HEAD_DIM), c2),
            pl.BlockSpec((CA_WIDTH, D_MODEL), c2),
            pl.BlockSpec((1, D_MODEL), c2),
            pl.BlockSpec((D_MODEL, 2 * LANES), c2),
            pl.BlockSpec((1, LANES), c2),
            pl.BlockSpec((1, LANES), c2),
            pl.BlockSpec((nb * ts, nb * ts), c2),
        ],
        out_specs=[
            tile(D_MODEL), tile(D_MODEL), tile(LANES),
            pl.BlockSpec((nb, 8, ts), lambda b, i: (b, 0, i)),
            pl.BlockSpec((1, LANES), c2),
        ],
        out_shape=[
            jax.ShapeDtypeStruct((B, S, D_MODEL), F32),
            jax.ShapeDtypeStruct((B, S, D_MODEL), F32),
            jax.ShapeDtypeStruct((B, S, LANES), F32),
            jax.ShapeDtypeStruct((B, 8, S), F32),
            jax.ShapeDtypeStruct((1, LANES), F32),
        ],
        scratch_shapes=[pltpu.VMEM((1, LANES), F32)],
        compiler_params=pltpu.CompilerParams(
            dimension_semantics=("arbitrary", "arbitrary"),
            vmem_limit_bytes=VMEM_LIMIT),
        name="mid",
    )(x, att, ssm_tm, mk, mv, wp["gao"], wp["gso"], wp["wout"], wp["gx"], wp["wcq"],
      wp["gcq"], wp["wco"], wp["gffn"], wp["wr"], wp["br"], cnt0,
      jnp.tri(nb * ts, k=-1, dtype=BF16))


def _select_part(i, tile_starts, refs):
    x = refs[0][...]
    for start, ref in zip(tile_starts[1:], refs[1:]):
        x = jnp.where(i >= start, ref[...], x)
    return x


def _part_spec(shape, tile_start, n_tiles):
    def index(i, *_):
        return (jnp.clip(i - tile_start, 0, n_tiles - 1),) + (0,) * (len(shape) - 1)
    return pl.BlockSpec(shape, index)


def _dispatch_kernel(pend_ref, padded_ref, dest_ref, *rest, tile_starts):
    n_parts = len(tile_starts)
    hn_refs = rest[:n_parts]
    xs_hbm, stage, zbuf, sem = rest[n_parts:]
    tm = hn_refs[0].shape[0]
    i = pl.program_id(0)
    slot = lax.rem(i, 2)
    blk = zbuf.shape[0]

    def wait_rows(s):
        for _ in range(2):
            pltpu.make_async_copy(stage.at[s], xs_hbm.at[pl.ds(0, tm)], sem.at[s]).wait()

    @pl.when(i == 0)
    def _():
        zbuf[...] = jnp.zeros_like(zbuf)
        for e in range(N_EXPERTS):
            @pl.when(padded_ref[e] > 0)
            def _():
                row0 = pl.multiple_of(pend_ref[e] - blk, blk)
                fill = pltpu.make_async_copy(zbuf, xs_hbm.at[pl.ds(row0, blk)], sem.at[2])
                fill.start()
                fill.wait()

        def fill_tail(b, carry):
            fill = pltpu.make_async_copy(
                zbuf, xs_hbm.at[pl.ds(pl.multiple_of(b * blk, blk), blk)], sem.at[2])
            fill.start()
            fill.wait()
            return carry

        lax.fori_loop(pend_ref[N_EXPERTS - 1] // blk, xs_hbm.shape[0] // blk, fill_tail, 0)

    @pl.when(i >= 2)
    def _():
        wait_rows(slot)

    tile = _pack_bf16_pairs(_select_part(i, tile_starts, hn_refs))
    for s in range(2):
        @pl.when(slot == s)
        def _():
            stage[s] = tile
            for k in range(2):
                for r in range(tm):
                    pltpu.make_async_copy(stage.at[s, pl.ds(r, 1), :],
                                          xs_hbm.at[pl.ds(dest_ref[0, 0, k * tm + r], 1), :],
                                          sem.at[s]).start(priority=r % 2)

    @pl.when(i == pl.num_programs(0) - 1)
    def _():
        wait_rows(slot)

        @pl.when(i >= 1)
        def _():
            wait_rows(1 - slot)


def _tile_layout(arrays, tm):
    counts = [a.shape[0] // tm for a in arrays]
    starts = [sum(counts[:p]) for p in range(len(counts))]
    return counts, starts


def _dispatch(pad_end, padded, dest_t, hns, rows, tm, blk):
    counts, starts = _tile_layout(hns, tm)
    grid_spec = pltpu.PrefetchScalarGridSpec(
        num_scalar_prefetch=2,
        grid=(sum(counts),),
        in_specs=[pl.BlockSpec((1, 1, 2 * tm), lambda i, pe, pd: (i, 0, 0),
                               memory_space=pltpu.SMEM)]
        + [_part_spec((tm, D_MODEL), s, n) for s, n in zip(starts, counts)],
        out_specs=pl.BlockSpec(memory_space=pl.ANY),
        scratch_shapes=[
            pltpu.VMEM((2, tm, D_MODEL // 2), jnp.uint32),
            pltpu.VMEM((blk, D_MODEL // 2), jnp.uint32),
            pltpu.SemaphoreType.DMA((3,)),
        ],
    )
    return pl.pallas_call(
        functools.partial(_dispatch_kernel, tile_starts=tuple(starts)),
        grid_spec=grid_spec,
        out_shape=jax.ShapeDtypeStruct((rows, D_MODEL // 2), jnp.uint32),
        compiler_params=pltpu.CompilerParams(
            dimension_semantics=("arbitrary",), vmem_limit_bytes=VMEM_LIMIT),
        name="dispatch",
    )(pad_end, padded, dest_t, *hns)


def _moe_kernel(be_ref, nu_ref, nxt_ref, xs_ref, wg_hbm, wu_hbm, wd_hbm, yb_ref,
                wg_f, wu_f, wd_f, wg_s, wu_s, wd_s, run_s, sem):
    i = pl.program_id(0)

    def fetch(e, slot):
        return [pltpu.make_async_copy(src.at[e], dst.at[slot], sem.at[slot])
                for src, dst in ((wg_hbm, wg_f), (wu_hbm, wu_f), (wd_hbm, wd_f))]

    @pl.when(i < nu_ref[0])
    def _():
        e = be_ref[i]

        @pl.when(i == 0)
        def _():
            run_s[0] = 0
            for c in fetch(e, 0):
                c.start()

        @pl.when((i == 0) | (e != be_ref[jnp.maximum(i - 1, 0)]))
        def _():
            slot = lax.rem(run_s[0], 2)
            run_s[0] = run_s[0] + 1
            for c in fetch(e, slot):
                c.wait()
            wg_s[...] = wg_f[slot].astype(BF16)
            wu_s[...] = wu_f[slot].astype(BF16)
            wd_s[...] = wd_f[slot].astype(BF16)

            @pl.when(nxt_ref[e] != e)
            def _():
                for c in fetch(nxt_ref[e], 1 - slot):
                    c.start()

        xe = _unpack_bf16_pairs(xs_ref[...])
        g = _mm(xe, wg_s[...])
        u = _mm(xe, wu_s[...])
        hmid = ((g * (1.0 / (1.0 + jnp.exp(-g)))) * u).astype(BF16)
        yb_ref[...] = _mm(hmid, wd_s[...])

    @pl.when(i >= nu_ref[0])
    def _():
        yb_ref[...] = jnp.zeros_like(yb_ref)


def _moe(block_e, n_used, next_e, xs, w_gate, w_up, w_down, blk):
    n_blocks = block_e.shape[0]
    in_blk = lambda i, be, nu, nx: (jnp.maximum(jnp.minimum(i, nu[0] - 1), 0), 0)
    grid_spec = pltpu.PrefetchScalarGridSpec(
        num_scalar_prefetch=3,
        grid=(n_blocks,),
        in_specs=[
            pl.BlockSpec((blk, D_MODEL // 2), in_blk),
            pl.BlockSpec(memory_space=pl.ANY),
            pl.BlockSpec(memory_space=pl.ANY),
            pl.BlockSpec(memory_space=pl.ANY),
        ],
        out_specs=pl.BlockSpec((blk, D_MODEL), lambda i, be, nu, nx: (i, 0)),
        scratch_shapes=[
            pltpu.VMEM((2, D_MODEL, D_EXPERT), F32),
            pltpu.VMEM((2, D_MODEL, D_EXPERT), F32),
            pltpu.VMEM((2, D_EXPERT, D_MODEL), F32),
            pltpu.VMEM((D_MODEL, D_EXPERT), BF16),
            pltpu.VMEM((D_MODEL, D_EXPERT), BF16),
            pltpu.VMEM((D_EXPERT, D_MODEL), BF16),
            pltpu.SMEM((1,), jnp.int32),
            pltpu.SemaphoreType.DMA((2,)),
        ],
    )
    return pl.pallas_call(
        _moe_kernel,
        grid_spec=grid_spec,
        out_shape=jax.ShapeDtypeStruct((xs.shape[0], D_MODEL), F32),
        compiler_params=pltpu.CompilerParams(
            dimension_semantics=("arbitrary",), vmem_limit_bytes=VMEM_LIMIT),
        name="moe",
    )(block_e, n_used, next_e, xs, w_gate, w_up, w_down)


def _combine_kernel(dest_ref, dest_next_ref, *rest, tile_starts):
    n_parts = len(tile_starts)
    x2_refs, rt_refs = rest[:n_parts], rest[n_parts:2 * n_parts]
    yb_hbm = rest[2 * n_parts]
    o_refs = rest[2 * n_parts + 1:3 * n_parts + 1]
    buf, sem = rest[3 * n_parts + 1:]
    tm = x2_refs[0].shape[0]
    i = pl.program_id(0)
    slot = lax.rem(i, 2)

    def gather(d_ref, s):
        for k in range(2):
            for r in range(tm):
                pltpu.make_async_copy(yb_hbm.at[pl.ds(d_ref[0, 0, k * tm + r], 1), :],
                                      buf.at[s, k, pl.ds(r, 1), :],
                                      sem.at[s]).start(priority=r % 2)

    @pl.when(i == 0)
    def _():
        gather(dest_ref, 0)

    for s in range(2):
        @pl.when((i + 1 < pl.num_programs(0)) & (slot == 1 - s))
        def _():
            gather(dest_next_ref, s)

    for k in range(2):
        pltpu.make_async_copy(yb_hbm.at[pl.ds(0, tm), :], buf.at[slot, k], sem.at[slot]).wait()
    rt = _select_part(i, tile_starts, rt_refs)
    out = (_select_part(i, tile_starts, x2_refs) + rt[:, 2:3] * buf[slot, 0]
           + rt[:, 3:4] * buf[slot, 1])
    ends = tile_starts[1:] + (pl.num_programs(0),)
    for start, end, o_ref in zip(tile_starts, ends, o_refs):
        @pl.when((i >= start) & (i < end))
        def _():
            o_ref[...] = out


def _combine(dest_t, x2s, rts, yb, tm):
    counts, starts = _tile_layout(x2s, tm)
    nt = sum(counts)
    spec = lambda w: [_part_spec((tm, w), s, n) for s, n in zip(starts, counts)]
    return pl.pallas_call(
        functools.partial(_combine_kernel, tile_starts=tuple(starts)),
        grid=(nt,),
        in_specs=[
            pl.BlockSpec((1, 1, 2 * tm), lambda i: (i, 0, 0), memory_space=pltpu.SMEM),
            pl.BlockSpec((1, 1, 2 * tm), lambda i: (jnp.minimum(i + 1, nt - 1), 0, 0),
                         memory_space=pltpu.SMEM),
        ] + spec(D_MODEL) + spec(LANES) + [pl.BlockSpec(memory_space=pl.ANY)],
        out_specs=spec(D_MODEL),
        out_shape=[jax.ShapeDtypeStruct(x2.shape, F32) for x2 in x2s],
        scratch_shapes=[
            pltpu.VMEM((2, 2, tm, D_MODEL), F32),
            pltpu.SemaphoreType.DMA((2,)),
        ],
        compiler_params=pltpu.CompilerParams(
            dimension_semantics=("arbitrary",), vmem_limit_bytes=VMEM_LIMIT),
        name="combine",
    )(dest_t, dest_t, *x2s, *rts, yb)


def _hier_moe(parts, cnt, w_gate, w_up, w_down, tm, blk):
    counts = cnt[0, ROUTER_COL0:ROUTER_COL0 + N_EXPERTS].astype(jnp.int32)
    padded = (counts + blk - 1) // blk * blk
    pad_end = jnp.cumsum(padded)
    pad_start = pad_end - padded
    t_all = sum(p[0].shape[0] for p in parts)
    n_blocks = (2 * t_all + N_EXPERTS * (blk - 1)) // blk + 1
    rows = n_blocks * blk
    blk_row0 = jnp.arange(n_blocks, dtype=jnp.int32) * blk
    block_e = jnp.minimum(
        jnp.sum((pad_end[None, :] <= blk_row0[:, None]).astype(jnp.int32), axis=1),
        N_EXPERTS - 1)
    n_used = (pad_end[-1] // blk).astype(jnp.int32).reshape(1)
    ids = jnp.arange(N_EXPERTS, dtype=jnp.int32)
    later = (ids[None, :] > ids[:, None]) & (padded[None, :] > 0)
    next_e = jnp.where(jnp.any(later, axis=1),
                       jnp.min(jnp.where(later, ids[None, :], N_EXPERTS), axis=1), ids)
    experts = jnp.arange(N_EXPERTS, dtype=jnp.int32)[:, None, None]

    dests = []
    for x2, _, _, rtt in parts:
        T = x2.shape[0]
        flat = lambda a: jnp.swapaxes(a, 0, 1).reshape(a.shape[1], T)
        eid = flat(rtt[:, 0:2, :]).astype(jnp.int32)
        rank = flat(rtt[:, 4:6, :]).astype(jnp.int32)
        dest = rank + jnp.sum(
            jnp.where(eid[None] == experts, pad_start[:, None, None], 0), axis=0)
        nt = T // tm
        dests.append(dest.reshape(2, nt, tm).transpose(1, 0, 2).reshape(nt, 1, 2 * tm))
    dest_t = jnp.concatenate(dests, axis=0)
    xs = _dispatch(pad_end, padded, dest_t, [p[1] for p in parts], rows, tm, blk)
    yb = _moe(block_e, n_used, next_e.astype(jnp.int32), xs, w_gate, w_up, w_down, blk)
    return _combine(dest_t, [p[0] for p in parts], [p[2] for p in parts], yb, tm)


def _rope_table(pos):
    half = ROPE_DIM // 2
    inv = ROPE_THETA ** (-jnp.arange(0, ROPE_DIM, 2, dtype=F32) / ROPE_DIM)
    ang = pos.astype(F32)[:, None] * inv[None, :]
    cos, sin = jnp.cos(ang), jnp.sin(ang)
    L = pos.shape[0]
    pad = jnp.zeros((L, HEAD_DIM - ROPE_DIM), F32)
    zero = jnp.zeros((L, half), F32)
    c64 = jnp.concatenate([cos, cos, pad + 1.0], axis=1)
    lo64 = jnp.concatenate([-sin, zero, pad], axis=1)
    hi64 = jnp.concatenate([zero, sin, pad], axis=1)
    two = lambda t: jnp.concatenate([t, t], axis=1)
    return jnp.concatenate([two(c64), two(lo64), two(hi64)], axis=1)


def _mixers(x, pos_rope, kctx_prev, vctx_prev, h0r, h0i, mk, mv, wp, sp, cnt0, *,
            tm_in, tq, ssm_l, tm_mid):
    B, S, _ = x.shape
    T = B * S
    q, k3, v3, u_tm = _in_proj(x, wp["gmix"], wp["win"], wp["gq"], wp["gk"], pos_rope, *tm_in)
    if kctx_prev is None:
        kctx, vctx = k3, v3
    else:
        kctx = jnp.concatenate([kctx_prev, k3], axis=1)
        vctx = jnp.concatenate([vctx_prev, v3], axis=1)
    att = _swa(wp["sink"], q, kctx, vctx, tq, mask_context=kctx_prev is None)
    ssm_tm, hr, hi = _ssm(u_tm.reshape(S, B, SSM_WIDTH), h0r, h0i, sp, ssm_l)
    x2, hn, rt, rtt, cnt = _mid(x, att, ssm_tm.reshape(S, B * SSM_WIDTH), mk, mv, wp, cnt0,
                                *tm_mid)
    part = (x2.reshape(T, D_MODEL), hn.reshape(T, D_MODEL), rt.reshape(T, LANES), rtt)
    return part, cnt, k3, v3, hr, hi


def kernel(x_prompt, x_sample, cache_attn_k, cache_attn_v, state_ssm_re, state_ssm_im, cache_mem_k, cache_mem_v, mem_prompt, norm_mix, w_in, q_norm, k_norm, attn_sink, ssm_lambda_re, ssm_lambda_im, ssm_log_dt, ssm_b_re, ssm_b_im, ssm_c_re, ssm_c_im, ssm_d, ssm_w_glu, ssm_b_glu, norm_attn_out, norm_ssm_out, w_out, norm_cross, norm_mem, w_cq, w_ck, w_cv, cq_norm, ck_norm, w_co, norm_ffn, w_router_group, b_router_group, w_router_expert, b_router_expert, w_e_gate, w_e_up, w_e_down):
    depth = norm_mix.shape[0]
    Bp, Lp, _ = x_prompt.shape
    Bs, Ls, _ = x_sample.shape
    yp, ys = x_prompt, x_sample
    rope_p = _rope_table(jnp.arange(Lp, dtype=jnp.int32))
    rope_s = _rope_table(PAST_LEN + jnp.arange(Ls, dtype=jnp.int32))
    outs = [[] for _ in range(10)]
    n_router = N_EXPERT_GROUPS + N_EXPERTS
    for l in range(depth):
        row = lambda a: a[l].astype(F32).reshape(1, -1)
        w_r = jnp.pad(jnp.concatenate([w_router_group[l], w_router_expert[l]], axis=1).astype(F32),
                      ((0, 0), (0, LANES - n_router)))
        w_r_hi = w_r.astype(BF16)
        w_r_lo = (w_r - w_r_hi.astype(F32)).astype(BF16)
        b_r = jnp.pad(jnp.concatenate([b_router_group[l], b_router_expert[l]]).astype(F32),
                      (0, LANES - n_router)).reshape(1, LANES)
        wp = {
            "gmix": row(norm_mix), "win": w_in[l].astype(BF16),
            "gq": jnp.tile(row(q_norm), (1, LANES // HEAD_DIM)),
            "gk": jnp.tile(row(k_norm), (1, LANES // HEAD_DIM)),
            "sink": attn_sink[l].astype(F32),
            "gao": row(norm_attn_out), "gso": row(norm_ssm_out),
            "wout": w_out[l].astype(BF16), "gx": row(norm_cross),
            "wcq": w_cq[l].astype(BF16), "gcq": row(cq_norm),
            "wco": w_co[l].astype(BF16), "gffn": row(norm_ffn),
            "wr": jnp.concatenate([w_r_hi, w_r_lo], axis=1), "br": b_r,
        }
        sp = _ssm_params(ssm_lambda_re[l], ssm_lambda_im[l], ssm_log_dt[l], ssm_b_re[l],
                         ssm_b_im[l], ssm_c_re[l], ssm_c_im[l], ssm_d[l], ssm_w_glu[l],
                         ssm_b_glu[l])
        ew = (w_e_gate[l].astype(F32), w_e_up[l].astype(F32), w_e_down[l].astype(F32))

        w_ckv = jnp.concatenate([w_ck[l], w_cv[l]], axis=1).astype(BF16)
        mkp, mvp = _memkv(mem_prompt.reshape(Bp * N_MEM, D_MODEL), row(norm_mem), w_ckv,
                          row(ck_norm), 512)
        mkp = mkp.reshape(Bp, N_MEM, CA_WIDTH)
        mvp = mvp.reshape(Bp, N_MEM, CA_WIDTH)

        zst = jnp.zeros((Bp, SSM_COLS), F32)
        part_p, cnt_p, kp, vp, hpr, hpi = _mixers(
            yp, rope_p, None, None, zst, zst, mkp, mvp, wp, sp, jnp.zeros((1, LANES), F32),
            tm_in=(1, 512), tq=256, ssm_l=64, tm_mid=(1, 512))
        part_s, cnt_s, kn, vn, hsr, hsi = _mixers(
            ys, rope_s, cache_attn_k[l].reshape(Bs, WINDOW, KV_WIDTH).astype(F32),
            cache_attn_v[l].reshape(Bs, WINDOW, KV_WIDTH).astype(F32),
            state_ssm_re[l].astype(F32).reshape(Bs, SSM_COLS),
            state_ssm_im[l].astype(F32).reshape(Bs, SSM_COLS),
            cache_mem_k[l].astype(F32).reshape(Bs, N_MEM, CA_WIDTH),
            cache_mem_v[l].astype(F32).reshape(Bs, N_MEM, CA_WIDTH), wp, sp, cnt_p,
            tm_in=(8, Ls), tq=CHUNK, ssm_l=Ls, tm_mid=(8, Ls))
        yp, ys = _hier_moe([part_p, part_s], cnt_s, *ew, 256, 2 * MOE_BLOCK)
        yp = yp.reshape(Bp, Lp, D_MODEL)
        ys = ys.reshape(Bs, Ls, D_MODEL)

        sg = (N_SSM_GROUPS, SSM_STATE)
        kvs = (N_KV_HEADS, HEAD_DIM)
        vals = (kp[:, Lp - WINDOW:].reshape(Bp, WINDOW, *kvs),
                vp[:, Lp - WINDOW:].reshape(Bp, WINDOW, *kvs),
                hpr.reshape(Bp, *sg), hpi.reshape(Bp, *sg),
                mkp.reshape(Bp, N_MEM, CA_HEADS, CA_HEAD_DIM),
                mvp.reshape(Bp, N_MEM, CA_HEADS, CA_HEAD_DIM),
                kn.reshape(Bs, Ls, *kvs), vn.reshape(Bs, Ls, *kvs),
                hsr.reshape(Bs, *sg), hsi.reshape(Bs, *sg))
        for lst, val in zip(outs, vals):
            lst.append(val)
    return (yp, ys) + tuple(jnp.stack(lst) for lst in outs)
```

```python
import functools
import math

import jax
import jax.numpy as jnp
from jax import lax
from jax.experimental import pallas as pl
from jax.experimental.pallas import tpu as pltpu

F32 = jnp.float32
BF16 = jnp.bfloat16

D_MODEL = 1024
CHUNK = 64
N_Q_HEADS = 8
N_KV_HEADS = 2
GQA = N_Q_HEADS // N_KV_HEADS
HEAD_DIM = 64
WINDOW = 128
BAND = WINDOW + CHUNK
ROPE_DIM = HEAD_DIM // 4
ROPE_THETA = 500000.0
ATT_WIDTH = N_Q_HEADS * HEAD_DIM
KV_WIDTH = N_KV_HEADS * HEAD_DIM
SSM_GROUP = 16
SSM_WIDTH = D_MODEL // 2
N_SSM_GROUPS = SSM_WIDTH // SSM_GROUP
SSM_STATE = 64
SSM_COLS = N_SSM_GROUPS * SSM_STATE
IN_WIDTH = ATT_WIDTH + 2 * KV_WIDTH + SSM_WIDTH
N_MEM = 256
CA_HEADS = 4
CA_HEAD_DIM = 128
CA_WIDTH = CA_HEADS * CA_HEAD_DIM
N_EXPERT_GROUPS = 4
EXPERTS_PER_GROUP = 8
N_EXPERTS = N_EXPERT_GROUPS * EXPERTS_PER_GROUP
D_EXPERT = 512
MOE_BLOCK = 256
EPS = 1e-6
NEG = -1e30
PAST_LEN = 4096

LANES = 128
ROUTER_COL0 = N_EXPERT_GROUPS
VMEM_LIMIT = 48 * 1024 * 1024


def _rms(x, g):
    ms = jnp.mean(x * x, axis=-1, keepdims=True)
    return (x * lax.rsqrt(ms + EPS)) * g


def _mm(a, b):
    return jnp.dot(a, b, preferred_element_type=F32)


_HI_HALF = 0xFFFF0000


def _pack_bf16_pairs(x):
    half = x.shape[1] // 2
    bits = lambda v: lax.bitcast_convert_type(v.astype(BF16).astype(F32), jnp.uint32)
    return (lax.shift_right_logical(bits(x[:, :half]), jnp.uint32(16))
            | (bits(x[:, half:]) & jnp.uint32(_HI_HALF)))


def _unpack_bf16_pairs(w):
    lo = lax.bitcast_convert_type(lax.shift_left(w, jnp.uint32(16)), F32)
    hi = lax.bitcast_convert_type(w & jnp.uint32(_HI_HALF), F32)
    return jnp.concatenate([lo.astype(BF16), hi.astype(BF16)], axis=1)


def _in_proj_kernel(x_ref, g_ref, w_ref, gq_ref, gk_ref, rope_ref,
                    q_ref, k_ref, v_ref, u_ref):
    nb, ts, _ = x_ref.shape
    tm = nb * ts
    h = _rms(x_ref[...].reshape(tm, D_MODEL), g_ref[...])
    hin = _mm(h.astype(BF16), w_ref[...])
    rope = jnp.concatenate([rope_ref[...]] * nb, axis=0)
    cos = rope[:, 0:LANES]
    sin_lo = rope[:, LANES:2 * LANES]
    sin_hi = rope[:, 2 * LANES:3 * LANES]
    lane = lax.broadcasted_iota(jnp.int32, (tm, LANES), 1)
    left = lane < HEAD_DIM

    def norm_rope(z, g):
        sq = z * z
        lsum = jnp.sum(jnp.where(left, sq, 0.0), axis=-1, keepdims=True)
        rsum = jnp.sum(jnp.where(left, 0.0, sq), axis=-1, keepdims=True)
        ms = jnp.where(left, lsum, rsum) * (1.0 / HEAD_DIM)
        zn = (z * lax.rsqrt(ms + EPS)) * g
        half = ROPE_DIM // 2
        return (zn * cos + pltpu.roll(zn, LANES - half, 1) * sin_lo
                + pltpu.roll(zn, half, 1) * sin_hi)

    for j in range(ATT_WIDTH // LANES):
        sl = slice(j * LANES, (j + 1) * LANES)
        q_ref[:, :, sl] = norm_rope(hin[:, sl], gq_ref[...]).reshape(nb, ts, LANES)
    k_ref[...] = norm_rope(hin[:, ATT_WIDTH:ATT_WIDTH + KV_WIDTH],
                           gk_ref[...]).reshape(nb, ts, KV_WIDTH)
    v_ref[...] = hin[:, ATT_WIDTH + KV_WIDTH:ATT_WIDTH + 2 * KV_WIDTH].reshape(nb, ts, KV_WIDTH)
    for b in range(nb):
        u_ref[:, b * SSM_WIDTH:(b + 1) * SSM_WIDTH] = (
            hin[b * ts:(b + 1) * ts, ATT_WIDTH + 2 * KV_WIDTH:])


def _in_proj(x, g, w_bf, gq, gk, rope, nb, ts):
    B, S, _ = x.shape
    full = lambda b, i: (0, 0)
    tile = lambda w: pl.BlockSpec((nb, ts, w), lambda b, i: (b, i, 0))
    return pl.pallas_call(
        _in_proj_kernel,
        grid=(B // nb, S // ts),
        in_specs=[
            tile(D_MODEL),
            pl.BlockSpec((1, D_MODEL), full),
            pl.BlockSpec((D_MODEL, IN_WIDTH), full),
            pl.BlockSpec((1, LANES), full),
            pl.BlockSpec((1, LANES), full),
            pl.BlockSpec((ts, 3 * LANES), lambda b, i: (i, 0)),
        ],
        out_specs=[
            tile(ATT_WIDTH), tile(KV_WIDTH), tile(KV_WIDTH),
            pl.BlockSpec((ts, nb * SSM_WIDTH), lambda b, i: (i, b)),
        ],
        out_shape=[
            jax.ShapeDtypeStruct((B, S, ATT_WIDTH), F32),
            jax.ShapeDtypeStruct((B, S, KV_WIDTH), F32),
            jax.ShapeDtypeStruct((B, S, KV_WIDTH), F32),
            jax.ShapeDtypeStruct((S, B * SSM_WIDTH), F32),
        ],
        compiler_params=pltpu.CompilerParams(
            dimension_semantics=("arbitrary", "arbitrary"),
            vmem_limit_bytes=VMEM_LIMIT),
        name="in_proj",
    )(x, g, w_bf, gq, gk, rope)


def _swa_kernel(sink_ref, q_ref, k_ref, v_ref, o_ref, *, mask_context):
    tq = q_ref.shape[1]
    i = pl.program_id(1)
    nch = tq // CHUNK
    lane = lax.broadcasted_iota(jnp.int32, (BAND, LANES), 1)
    lo_half = lane < HEAD_DIM
    vrow_lo = lax.broadcasted_iota(jnp.int32, (LANES, BAND), 0) < HEAD_DIM
    q_lo = lax.broadcasted_iota(jnp.int32, (1, LANES), 1) < CHUNK
    slabs_per_kv = GQA * HEAD_DIM // LANES

    units = []
    scores = []
    vpads = {}
    for c in range(nch):
        chunk = i * nch + c
        if mask_context:
            first = jnp.maximum(chunk - WINDOW // CHUNK, 0)
            start = pl.multiple_of(first * CHUNK, CHUNK)
            kidx = start + lax.broadcasted_iota(jnp.int32, (BAND, LANES), 0)
            valid = kidx < (chunk + 1) * CHUNK
        else:
            start = pl.multiple_of(chunk * CHUNK, CHUNK)
        kb = k_ref[0, pl.ds(start, BAND), :]
        kb_sw = pltpu.roll(kb, HEAD_DIM, 1)
        vt = v_ref[0, pl.ds(start, BAND), :].T
        vt_sw = jnp.concatenate([vt[HEAD_DIM:], vt[:HEAD_DIM]], axis=0)
        for kvh in range(N_KV_HEADS):
            k_own, k_oth = (kb, kb_sw) if kvh == 0 else (kb_sw, kb)
            v_own, v_oth = (vt, vt_sw) if kvh == 0 else (vt_sw, vt)
            kpad = (jnp.where(lo_half, k_own, 0.0).astype(BF16),
                    jnp.where(lo_half, 0.0, k_oth).astype(BF16))
            vpads[(c, kvh)] = (jnp.where(vrow_lo, v_own, 0.0).astype(BF16),
                               jnp.where(vrow_lo, 0.0, v_oth).astype(BF16))
            col0 = kvh * GQA * HEAD_DIM
            q2 = jnp.concatenate(
                [q_ref[0, c * CHUNK:(c + 1) * CHUNK, col0 + m * LANES:col0 + (m + 1) * LANES]
                 for m in range(slabs_per_kv)], axis=0).astype(BF16)
            for side in range(2):
                s = lax.dot_general(kpad[side], q2, (((1,), (1,)), ((), ())),
                                    preferred_element_type=F32) * (HEAD_DIM ** -0.5)
                if mask_context:
                    s = jnp.where(valid, s, NEG)
                units.append((c, kvh, side))
                scores.append(s)

    sinks = [jnp.where(q_lo, sink_ref[kvh * GQA + side], sink_ref[kvh * GQA + 2 + side])
             for (_, kvh, side) in units]
    maxes = [jnp.maximum(jnp.max(s, axis=0, keepdims=True), sk)
             for s, sk in zip(scores, sinks)]
    exps = [jnp.exp(s - mx) for s, mx in zip(scores, maxes)]
    dens = [jnp.sum(p, axis=0, keepdims=True) + jnp.exp(sk - mx)
            for p, sk, mx in zip(exps, sinks, maxes)]
    probs = [(p * (1.0 / den)).astype(BF16) for p, den in zip(exps, dens)]

    for n in range(0, len(units), 2):
        c, kvh, _ = units[n]
        vp = vpads[(c, kvh)]
        o = (_mm(vp[0], probs[n]) + _mm(vp[1], probs[n + 1])).T
        col0 = kvh * GQA * HEAD_DIM
        for m in range(slabs_per_kv):
            o_ref[0, c * CHUNK:(c + 1) * CHUNK, col0 + m * LANES:col0 + (m + 1) * LANES] = (
                o[m * CHUNK:(m + 1) * CHUNK])


def _swa(sink, q, kctx, vctx, tq, mask_context):
    B, Sq, _ = q.shape
    Sk = kctx.shape[1]
    return pl.pallas_call(
        functools.partial(_swa_kernel, mask_context=mask_context),
        grid=(B, Sq // tq),
        in_specs=[
            pl.BlockSpec(memory_space=pltpu.SMEM),
            pl.BlockSpec((1, tq, ATT_WIDTH), lambda b, i: (b, i, 0)),
            pl.BlockSpec((1, Sk, KV_WIDTH), lambda b, i: (b, 0, 0)),
            pl.BlockSpec((1, Sk, KV_WIDTH), lambda b, i: (b, 0, 0)),
        ],
        out_specs=pl.BlockSpec((1, tq, ATT_WIDTH), lambda b, i: (b, i, 0)),
        out_shape=jax.ShapeDtypeStruct((B, Sq, ATT_WIDTH), F32),
        compiler_params=pltpu.CompilerParams(
            dimension_semantics=("arbitrary", "arbitrary"),
            vmem_limit_bytes=VMEM_LIMIT),
        name="swa",
    )(sink, q, kctx, vctx)


def _ssm_kernel(u_ref, h0r_ref, h0i_ref, lam_ref, bre_ref, bim_ref, cre_ref, cim_ref,
                d_ref, wglu_ref, bglu_ref,
                y_ref, hr_out, hi_out, sr0, si0, sr1, si1, hr_s, hi_s):
    L, B, _ = u_ref.shape
    rows = L * B
    half_w = SSM_WIDTH // 2
    half_c = SSM_COLS // 2
    halves = ((sr0, si0), (sr1, si1))

    @pl.when(pl.program_id(0) == 0)
    def _():
        hr_s[...] = h0r_ref[...]
        hi_s[...] = h0i_ref[...]

    u = u_ref[...].reshape(rows, SSM_WIDTH)
    ub = u.astype(BF16)

    def project_in(hf):
        sr, si = halves[hf]
        uh = ub[:, hf * half_w:(hf + 1) * half_w]
        sr[...] = _mm(uh, bre_ref[hf])
        si[...] = _mm(uh, bim_ref[hf])

    def recur(hf):
        sr, si = halves[hf]
        cw = 4 * LANES
        for cc in range(half_c // cw):
            cols = slice(cc * cw, (cc + 1) * cw)
            gcols = slice(hf * half_c + cc * cw, hf * half_c + (cc + 1) * cw)
            lr = jnp.broadcast_to(lam_ref[0:1, gcols], (B, cw))
            li = jnp.broadcast_to(lam_ref[1:2, gcols], (B, cw))
            hr, hi = hr_s[:, gcols], hi_s[:, gcols]
            for t in range(L):
                at_t = slice(t * B, (t + 1) * B)
                hr, hi = (lr * hr - li * hi + sr[at_t, cols],
                          lr * hi + li * hr + si[at_t, cols])
                sr[at_t, cols] = hr
                si[at_t, cols] = hi
            hr_s[:, gcols] = hr
            hi_s[:, gcols] = hi

    def project_out(hf):
        sr, si = halves[hf]
        return (_mm(sr[...].astype(BF16), cre_ref[hf]) + _mm(si[...].astype(BF16), cim_ref[hf]))

    project_in(0)
    project_in(1)
    recur(0)
    y0 = project_out(0)
    recur(1)
    y1 = project_out(1)
    y = jnp.concatenate([y0, y1], axis=1) + d_ref[...] * u
    g = 0.5 * y * (1.0 + jnp.tanh(math.sqrt(2.0 / math.pi) * (y + 0.044715 * (y * y * y))))
    gb = g.astype(BF16)
    z = jnp.concatenate(
        [_mm(gb[:, hf * half_w:(hf + 1) * half_w], wglu_ref[hf]) for hf in range(2)],
        axis=1) + bglu_ref[...]
    out = g * (1.0 / (1.0 + jnp.exp(-z)))
    y_ref[...] = out.reshape(L, B, SSM_WIDTH)
    hr_out[...] = hr_s[...]
    hi_out[...] = hi_s[...]


def _ssm(u, h0r, h0i, sp, L):
    S, B, _ = u.shape
    c2 = lambda i: (0, 0)
    c3 = lambda i: (0, 0, 0)
    return pl.pallas_call(
        _ssm_kernel,
        grid=(S // L,),
        in_specs=[
            pl.BlockSpec((L, B, SSM_WIDTH), lambda i: (i, 0, 0)),
            pl.BlockSpec((B, SSM_COLS), c2),
            pl.BlockSpec((B, SSM_COLS), c2),
            pl.BlockSpec((2, SSM_COLS), c2),
            pl.BlockSpec((2, SSM_WIDTH // 2, SSM_COLS // 2), c3),
            pl.BlockSpec((2, SSM_WIDTH // 2, SSM_COLS // 2), c3),
            pl.BlockSpec((2, SSM_COLS // 2, SSM_WIDTH // 2), c3),
            pl.BlockSpec((2, SSM_COLS // 2, SSM_WIDTH // 2), c3),
            pl.BlockSpec((1, SSM_WIDTH), c2),
            pl.BlockSpec((2, SSM_WIDTH // 2, SSM_WIDTH // 2), c3),
            pl.BlockSpec((1, SSM_WIDTH), c2),
        ],
        out_specs=[
            pl.BlockSpec((L, B, SSM_WIDTH), lambda i: (i, 0, 0)),
            pl.BlockSpec((B, SSM_COLS), c2),
            pl.BlockSpec((B, SSM_COLS), c2),
        ],
        out_shape=[
            jax.ShapeDtypeStruct((S, B, SSM_WIDTH), F32),
            jax.ShapeDtypeStruct((B, SSM_COLS), F32),
            jax.ShapeDtypeStruct((B, SSM_COLS), F32),
        ],
        scratch_shapes=[
            pltpu.VMEM((L * B, SSM_COLS // 2), F32),
            pltpu.VMEM((L * B, SSM_COLS // 2), F32),
            pltpu.VMEM((L * B, SSM_COLS // 2), F32),
            pltpu.VMEM((L * B, SSM_COLS // 2), F32),
            pltpu.VMEM((B, SSM_COLS), F32),
            pltpu.VMEM((B, SSM_COLS), F32),
        ],
        compiler_params=pltpu.CompilerParams(
            dimension_semantics=("arbitrary",), vmem_limit_bytes=VMEM_LIMIT),
        name="ssm",
    )(u, h0r, h0i, sp["lam"], sp["bre"], sp["bim"], sp["cre"], sp["cim"],
      sp["d"], sp["wglu"], sp["bglu"])


def _block_diag(blocks):
    G, r, c = blocks.shape
    eye = jnp.eye(G, dtype=blocks.dtype)
    return jnp.einsum("grc,gh->grhc", blocks, eye).reshape(G * r, G * c)


def _ssm_params(lam_re, lam_im, log_dt, b_re, b_im, c_re, c_im, d, w_glu, b_glu):
    lam = lax.complex(lam_re.astype(F32), lam_im.astype(F32))
    dt = jnp.exp(log_dt.astype(F32))[:, None]
    lam_bar = jnp.exp(lam * dt)
    bmat = lax.complex(b_re.astype(F32), b_im.astype(F32))
    b_bar = ((lam_bar - 1.0) / lam)[..., None] * bmat
    lam2 = jnp.stack([lam_bar.real.reshape(-1), lam_bar.imag.reshape(-1)])
    bt = jnp.swapaxes(b_bar, 1, 2)
    hw, hc = SSM_WIDTH // 2, SSM_COLS // 2
    split_b = lambda m: jnp.stack([m[:hw, :hc], m[hw:, hc:]]).astype(BF16)
    split_c = lambda m: jnp.stack([m[:hc, :hw], m[hc:, hw:]]).astype(BF16)
    ct_re = jnp.swapaxes(c_re.astype(F32), 1, 2)
    ct_im = jnp.swapaxes(c_im.astype(F32), 1, 2)
    wg = _block_diag(w_glu.astype(F32))
    return {
        "lam": lam2,
        "bre": split_b(_block_diag(bt.real)),
        "bim": split_b(_block_diag(bt.imag)),
        "cre": split_c(_block_diag(ct_re)),
        "cim": split_c(_block_diag(-ct_im)),
        "d": d.astype(F32).reshape(1, SSM_WIDTH),
        "wglu": jnp.stack([wg[:hw, :hw], wg[hw:, hw:]]).astype(BF16),
        "bglu": b_glu.astype(F32).reshape(1, SSM_WIDTH),
    }


def _memkv_kernel(m_ref, g_ref, w_ref, gk_ref, k_ref, v_ref):
    m = _rms(m_ref[...], g_ref[...])
    kv = _mm(m.astype(BF16), w_ref[...])
    for h in range(CA_HEADS):
        sl = slice(h * CA_HEAD_DIM, (h + 1) * CA_HEAD_DIM)
        k_ref[:, sl] = _rms(kv[:, sl], gk_ref[...])
    v_ref[...] = kv[:, CA_WIDTH:]


def _memkv(mem2d, g, w_bf, gk, tm):
    T = mem2d.shape[0]
    full = lambda i: (0, 0)
    return pl.pallas_call(
        _memkv_kernel,
        grid=(T // tm,),
        in_specs=[
            pl.BlockSpec((tm, D_MODEL), lambda i: (i, 0)),
            pl.BlockSpec((1, D_MODEL), full),
            pl.BlockSpec((D_MODEL, 2 * CA_WIDTH), full),
            pl.BlockSpec((1, CA_HEAD_DIM), full),
        ],
        out_specs=[
            pl.BlockSpec((tm, CA_WIDTH), lambda i: (i, 0)),
            pl.BlockSpec((tm, CA_WIDTH), lambda i: (i, 0)),
        ],
        out_shape=[
            jax.ShapeDtypeStruct((T, CA_WIDTH), F32),
            jax.ShapeDtypeStruct((T, CA_WIDTH), F32),
        ],
        compiler_params=pltpu.CompilerParams(
            dimension_semantics=("arbitrary",), vmem_limit_bytes=VMEM_LIMIT),
        name="memkv",
    )(mem2d, g, w_bf, gk)


def _mid_kernel(x_ref, att_ref, ssm_ref, mk_ref, mv_ref,
                gao_ref, gso_ref, wout_ref, gx_ref, wcq_ref, gcq_ref, wco_ref,
                gffn_ref, wr_ref, br_ref, cnt0_ref, tri_ref,
                x2_ref, hn_ref, rt_ref, rtt_ref, cnt_ref, base_s):
    nb, ts, _ = x_ref.shape
    tm = nb * ts

    @pl.when((pl.program_id(0) == 0) & (pl.program_id(1) == 0))
    def _():
        base_s[...] = cnt0_ref[...]

    ssm = jnp.concatenate(
        [ssm_ref[:, b * SSM_WIDTH:(b + 1) * SSM_WIDTH] for b in range(nb)], axis=0)
    a = _rms(att_ref[...].reshape(tm, ATT_WIDTH), gao_ref[...]).astype(BF16)
    s = _rms(ssm, gso_ref[...]).astype(BF16)
    x1 = (x_ref[...].reshape(tm, D_MODEL) + _mm(a, wout_ref[0:ATT_WIDTH, :])
          + _mm(s, wout_ref[ATT_WIDTH:, :]))

    qx = _mm(_rms(x1, gx_ref[...]).astype(BF16), wcq_ref[...])
    heads = []
    for h in range(CA_HEADS):
        sl = slice(h * CA_HEAD_DIM, (h + 1) * CA_HEAD_DIM)
        qh = _rms(qx[:, sl], gcq_ref[...]).astype(BF16)
        per_batch = []
        for b in range(nb):
            kh = mk_ref[b, :, sl].astype(BF16)
            qb = qh[b * ts:(b + 1) * ts]
            if ts <= LANES:
                vt = mv_ref[b, :, sl].T.astype(BF16)
                sc = lax.dot_general(kh, qb, (((1,), (1,)), ((), ())),
                                     preferred_element_type=F32) * (CA_HEAD_DIM ** -0.5)
                p = jnp.exp(sc - jnp.max(sc, axis=0, keepdims=True))
                p = p * (1.0 / jnp.sum(p, axis=0, keepdims=True))
                per_batch.append(_mm(vt, p.astype(BF16)).T)
            else:
                vh = mv_ref[b, :, sl].astype(BF16)
                sc = lax.dot_general(qb, kh, (((1,), (1,)), ((), ())),
                                     preferred_element_type=F32) * (CA_HEAD_DIM ** -0.5)
                p = jnp.exp(sc - jnp.max(sc, axis=-1, keepdims=True))
                p = p / jnp.sum(p, axis=-1, keepdims=True)
                per_batch.append(_mm(p.astype(BF16), vh))
        heads.append(jnp.concatenate(per_batch, axis=0))
    o = jnp.concatenate(heads, axis=1).astype(BF16)
    x2 = x1 + _mm(o, wco_ref[...])
    x2_ref[...] = x2.reshape(nb, ts, D_MODEL)

    hn = _rms(x2, gffn_ref[...])
    hn_ref[...] = hn.reshape(nb, ts, D_MODEL)

    h_hi = hn.astype(BF16)
    h_lo = (hn - h_hi.astype(F32)).astype(BF16)
    r1 = _mm(h_hi, wr_ref[...])
    lg = (r1[:, :LANES] + r1[:, LANES:] + _mm(h_lo, wr_ref[:, 0:LANES])
          + br_ref[...])

    col = lax.broadcasted_iota(jnp.int32, (tm, LANES), 1)
    big = jnp.int32(4 * LANES)
    gmask = col < N_EXPERT_GROUPS
    lgg = jnp.where(gmask, lg, NEG)
    mg = jnp.max(lgg, axis=-1, keepdims=True)
    grp = jnp.min(jnp.where(gmask & (lgg == mg), col, big), axis=-1, keepdims=True)
    pg_top = 1.0 / jnp.sum(jnp.where(gmask, jnp.exp(lgg - mg), 0.0), axis=-1, keepdims=True)

    ecol = col - ROUTER_COL0
    emask = ((ecol >= 0) & (ecol < N_EXPERTS)
             & (lax.shift_right_arithmetic(ecol, 3) == grp))
    le = jnp.where(emask, lg, NEG)
    m1 = jnp.max(le, axis=-1, keepdims=True)
    i1 = jnp.min(jnp.where(emask & (le == m1), col, big), axis=-1, keepdims=True)
    rest = emask & (col != i1)
    le2 = jnp.where(rest, lg, NEG)
    m2 = jnp.max(le2, axis=-1, keepdims=True)
    i2 = jnp.min(jnp.where(rest & (le2 == m2), col, big), axis=-1, keepdims=True)
    den = jnp.sum(jnp.where(emask, jnp.exp(le - m1), 0.0), axis=-1, keepdims=True)
    p1 = 1.0 / den
    p2 = jnp.exp(m2 - m1) / den
    gate1 = pg_top * p1 / (p1 + p2)
    gate2 = pg_top * p2 / (p1 + p2)

    sel1 = col == i1
    sel2 = col == i2
    oh = jnp.where(sel1 | sel2, 1.0, 0.0)
    tot = base_s[...] + _mm(tri_ref[...], oh.astype(BF16))
    rank1 = jnp.sum(jnp.where(sel1, tot, 0.0), axis=-1, keepdims=True)
    rank2 = jnp.sum(jnp.where(sel2, tot, 0.0), axis=-1, keepdims=True)
    base_s[...] = base_s[...] + jnp.sum(oh, axis=0, keepdims=True)
    cnt_ref[...] = base_s[...]

    e1 = (i1 - ROUTER_COL0).astype(F32)
    e2 = (i2 - ROUTER_COL0).astype(F32)
    rt = jnp.zeros((tm, LANES), F32)
    for k, val in enumerate((e1, e2, gate1, gate2, rank1, rank2)):
        rt = jnp.where(col == k, val, rt)
    rt_ref[...] = rt.reshape(nb, ts, LANES)
    for b in range(nb):
        rtt_ref[b] = rt[b * ts:(b + 1) * ts].T[0:8, :]


def _mid(x, att, ssm_tm, mk, mv, wp, cnt0, nb, ts):
    B, S, _ = x.shape
    c2 = lambda b, i: (0, 0)
    tile = lambda w: pl.BlockSpec((nb, ts, w), lambda b, i: (b, i, 0))
    return pl.pallas_call(
        _mid_kernel,
        grid=(B // nb, S // ts),
        in_specs=[
            tile(D_MODEL), tile(ATT_WIDTH),
            pl.BlockSpec((ts, nb * SSM_WIDTH), lambda b, i: (i, b)),
            pl.BlockSpec((nb, N_MEM, CA_WIDTH), lambda b, i: (b, 0, 0)),
            pl.BlockSpec((nb, N_MEM, CA_WIDTH), lambda b, i: (b, 0, 0)),
            pl.BlockSpec((1, ATT_WIDTH), c2),
            pl.BlockSpec((1, SSM_WIDTH), c2),
            pl.BlockSpec((ATT_WIDTH + SSM_WIDTH, D_MODEL), c2),
            pl.BlockSpec((1, D_MODEL), c2),
            pl.BlockSpec((D_MODEL, CA_WIDTH), c2),
            pl.BlockSpec((1, CA_HEAD_DIM), c2),
            pl.BlockSpec((CA_WIDTH, D_MODEL), c2),
            pl.BlockSpec((1, D_MODEL), c2),
            pl.BlockSpec((D_MODEL, 2 * LANES), c2),
            pl.BlockSpec((1, LANES), c2),
            pl.BlockSpec((1, LANES), c2),
            pl.BlockSpec((nb * ts, nb * ts), c2),
        ],
        out_specs=[
            tile(D_MODEL), tile(D_MODEL), tile(LANES),
            pl.BlockSpec((nb, 8, ts), lambda b, i: (b, 0, i)),
            pl.BlockSpec((1, LANES), c2),
        ],
        out_shape=[
            jax.ShapeDtypeStruct((B, S, D_MODEL), F32),
            jax.ShapeDtypeStruct((B, S, D_MODEL), F32),
            jax.ShapeDtypeStruct((B, S, LANES), F32),
            jax.ShapeDtypeStruct((B, 8, S), F32),
            jax.ShapeDtypeStruct((1, LANES), F32),
        ],
        scratch_shapes=[pltpu.VMEM((1, LANES), F32)],
        compiler_params=pltpu.CompilerParams(
            dimension_semantics=("arbitrary", "arbitrary"),
            vmem_limit_bytes=VMEM_LIMIT),
        name="mid",
    )(x, att, ssm_tm, mk, mv, wp["gao"], wp["gso"], wp["wout"], wp["gx"], wp["wcq"],
      wp["gcq"], wp["wco"], wp["gffn"], wp["wr"], wp["br"], cnt0,
      jnp.tri(nb * ts, k=-1, dtype=BF16))


def _select_part(i, tile_starts, refs):
    x = refs[0][...]
    for start, ref in zip(tile_starts[1:], refs[1:]):
        x = jnp.where(i >= start, ref[...], x)
    return x


def _part_spec(shape, tile_start, n_tiles):
    def index(i, *_):
        return (jnp.clip(i - tile_start, 0, n_tiles - 1),) + (0,) * (len(shape) - 1)
    return pl.BlockSpec(shape, index)


def _dispatch_kernel(pend_ref, padded_ref, dest_ref, *rest, tile_starts):
    n_parts = len(tile_starts)
    hn_refs = rest[:n_parts]
    xs_hbm, stage, zbuf, sem = rest[n_parts:]
    tm = hn_refs[0].shape[0]
    i = pl.program_id(0)
    slot = lax.rem(i, 2)
    blk = zbuf.shape[0]

    def wait_rows(s):
        for _ in range(2):
            pltpu.make_async_copy(stage.at[s], xs_hbm.at[pl.ds(0, tm)], sem.at[s]).wait()

    @pl.when(i == 0)
    def _():
        zbuf[...] = jnp.zeros_like(zbuf)
        for e in range(N_EXPERTS):
            @pl.when(padded_ref[e] > 0)
            def _():
                row0 = pl.multiple_of(pend_ref[e] - blk, blk)
                fill = pltpu.make_async_copy(zbuf, xs_hbm.at[pl.ds(row0, blk)], sem.at[2])
                fill.start()
                fill.wait()

        def fill_tail(b, carry):
            fill = pltpu.make_async_copy(
                zbuf, xs_hbm.at[pl.ds(pl.multiple_of(b * blk, blk), blk)], sem.at[2])
            fill.start()
            fill.wait()
            return carry

        lax.fori_loop(pend_ref[N_EXPERTS - 1] // blk, xs_hbm.shape[0] // blk, fill_tail, 0)

    @pl.when(i >= 2)
    def _():
        wait_rows(slot)

    tile = _pack_bf16_pairs(_select_part(i, tile_starts, hn_refs))
    for s in range(2):
        @pl.when(slot == s)
        def _():
            stage[s] = tile
            for k in range(2):
                for r in range(tm):
                    pltpu.make_async_copy(stage.at[s, pl.ds(r, 1), :],
                                          xs_hbm.at[pl.ds(dest_ref[0, 0, k * tm + r], 1), :],
                                          sem.at[s]).start(priority=r % 2)

    @pl.when(i == pl.num_programs(0) - 1)
    def _():
        wait_rows(slot)

        @pl.when(i >= 1)
        def _():
            wait_rows(1 - slot)


def _tile_layout(arrays, tm):
    counts = [a.shape[0] // tm for a in arrays]
    starts = [sum(counts[:p]) for p in range(len(counts))]
    return counts, starts


def _dispatch(pad_end, padded, dest_t, hns, rows, tm, blk):
    counts, starts = _tile_layout(hns, tm)
    grid_spec = pltpu.PrefetchScalarGridSpec(
        num_scalar_prefetch=2,
        grid=(sum(counts),),
        in_specs=[pl.BlockSpec((1, 1, 2 * tm), lambda i, pe, pd: (i, 0, 0),
                               memory_space=pltpu.SMEM)]
        + [_part_spec((tm, D_MODEL), s, n) for s, n in zip(starts, counts)],
        out_specs=pl.BlockSpec(memory_space=pl.ANY),
        scratch_shapes=[
            pltpu.VMEM((2, tm, D_MODEL // 2), jnp.uint32),
            pltpu.VMEM((blk, D_MODEL // 2), jnp.uint32),
            pltpu.SemaphoreType.DMA((3,)),
        ],
    )
    return pl.pallas_call(
        functools.partial(_dispatch_kernel, tile_starts=tuple(starts)),
        grid_spec=grid_spec,
        out_shape=jax.ShapeDtypeStruct((rows, D_MODEL // 2), jnp.uint32),
        compiler_params=pltpu.CompilerParams(
            dimension_semantics=("arbitrary",), vmem_limit_bytes=VMEM_LIMIT),
        name="dispatch",
    )(pad_end, padded, dest_t, *hns)


def _moe_kernel(be_ref, nu_ref, nxt_ref, xs_ref, wg_hbm, wu_hbm, wd_hbm, yb_ref,
                wg_f, wu_f, wd_f, wg_s, wu_s, wd_s, run_s, sem):
    i = pl.program_id(0)

    def fetch(e, slot):
        return [pltpu.make_async_copy(src.at[e], dst.at[slot], sem.at[slot])
                for src, dst in ((wg_hbm, wg_f), (wu_hbm, wu_f), (wd_hbm, wd_f))]

    @pl.when(i < nu_ref[0])
    def _():
        e = be_ref[i]

        @pl.when(i == 0)
        def _():
            run_s[0] = 0
            for c in fetch(e, 0):
                c.start()

        @pl.when((i == 0) | (e != be_ref[jnp.maximum(i - 1, 0)]))
        def _():
            slot = lax.rem(run_s[0], 2)
            run_s[0] = run_s[0] + 1
            for c in fetch(e, slot):
                c.wait()
            wg_s[...] = wg_f[slot].astype(BF16)
            wu_s[...] = wu_f[slot].astype(BF16)
            wd_s[...] = wd_f[slot].astype(BF16)

            @pl.when(nxt_ref[e] != e)
            def _():
                for c in fetch(nxt_ref[e], 1 - slot):
                    c.start()

        xe = _unpack_bf16_pairs(xs_ref[...])
        g = _mm(xe, wg_s[...])
        u = _mm(xe, wu_s[...])
        hmid = ((g * (1.0 / (1.0 + jnp.exp(-g)))) * u).astype(BF16)
        yb_ref[...] = _mm(hmid, wd_s[...])

    @pl.when(i >= nu_ref[0])
    def _():
        yb_ref[...] = jnp.zeros_like(yb_ref)


def _moe(block_e, n_used, next_e, xs, w_gate, w_up, w_down, blk):
    n_blocks = block_e.shape[0]
    in_blk = lambda i, be, nu, nx: (jnp.maximum(jnp.minimum(i, nu[0] - 1), 0), 0)
    grid_spec = pltpu.PrefetchScalarGridSpec(
        num_scalar_prefetch=3,
        grid=(n_blocks,),
        in_specs=[
            pl.BlockSpec((blk, D_MODEL // 2), in_blk),
            pl.BlockSpec(memory_space=pl.ANY),
            pl.BlockSpec(memory_space=pl.ANY),
            pl.BlockSpec(memory_space=pl.ANY),
        ],
        out_specs=pl.BlockSpec((blk, D_MODEL), lambda i, be, nu, nx: (i, 0)),
        scratch_shapes=[
            pltpu.VMEM((2, D_MODEL, D_EXPERT), F32),
            pltpu.VMEM((2, D_MODEL, D_EXPERT), F32),
            pltpu.VMEM((2, D_EXPERT, D_MODEL), F32),
            pltpu.VMEM((D_MODEL, D_EXPERT), BF16),
            pltpu.VMEM((D_MODEL, D_EXPERT), BF16),
            pltpu.VMEM((D_EXPERT, D_MODEL), BF16),
            pltpu.SMEM((1,), jnp.int32),
            pltpu.SemaphoreType.DMA((2,)),
        ],
    )
    return pl.pallas_call(
        _moe_kernel,
        grid_spec=grid_spec,
        out_shape=jax.ShapeDtypeStruct((xs.shape[0], D_MODEL), F32),
        compiler_params=pltpu.CompilerParams(
            dimension_semantics=("arbitrary",), vmem_limit_bytes=VMEM_LIMIT),
        name="moe",
    )(block_e, n_used, next_e, xs, w_gate, w_up, w_down)


def _combine_kernel(dest_ref, dest_next_ref, *rest, tile_starts):
    n_parts = len(tile_starts)
    x2_refs, rt_refs = rest[:n_parts], rest[n_parts:2 * n_parts]
    yb_hbm = rest[2 * n_parts]
    o_refs = rest[2 * n_parts + 1:3 * n_parts + 1]
    buf, sem = rest[3 * n_parts + 1:]
    tm = x2_refs[0].shape[0]
    i = pl.program_id(0)
    slot = lax.rem(i, 2)

    def gather(d_ref, s):
        for k in range(2):
            for r in range(tm):
                pltpu.make_async_copy(yb_hbm.at[pl.ds(d_ref[0, 0, k * tm + r], 1), :],
                                      buf.at[s, k, pl.ds(r, 1), :],
                                      sem.at[s]).start(priority=r % 2)

    @pl.when(i == 0)
    def _():
        gather(dest_ref, 0)

    for s in range(2):
        @pl.when((i + 1 < pl.num_programs(0)) & (slot == 1 - s))
        def _():
            gather(dest_next_ref, s)

    for k in range(2):
        pltpu.make_async_copy(yb_hbm.at[pl.ds(0, tm), :], buf.at[slot, k], sem.at[slot]).wait()
    rt = _select_part(i, tile_starts, rt_refs)
    out = (_select_part(i, tile_starts, x2_refs) + rt[:, 2:3] * buf[slot, 0]
           + rt[:, 3:4] * buf[slot, 1])
    ends = tile_starts[1:] + (pl.num_programs(0),)
    for start, end, o_ref in zip(tile_starts, ends, o_refs):
        @pl.when((i >= start) & (i < end))
        def _():
            o_ref[...] = out


def _combine(dest_t, x2s, rts, yb, tm):
    counts, starts = _tile_layout(x2s, tm)
    nt = sum(counts)
    spec = lambda w: [_part_spec((tm, w), s, n) for s, n in zip(starts, counts)]
    return pl.pallas_call(
        functools.partial(_combine_kernel, tile_starts=tuple(starts)),
        grid=(nt,),
        in_specs=[
            pl.BlockSpec((1, 1, 2 * tm), lambda i: (i, 0, 0), memory_space=pltpu.SMEM),
            pl.BlockSpec((1, 1, 2 * tm), lambda i: (jnp.minimum(i + 1, nt - 1), 0, 0),
                         memory_space=pltpu.SMEM),
        ] + spec(D_MODEL) + spec(LANES) + [pl.BlockSpec(memory_space=pl.ANY)],
        out_specs=spec(D_MODEL),
        out_shape=[jax.ShapeDtypeStruct(x2.shape, F32) for x2 in x2s],
        scratch_shapes=[
            pltpu.VMEM((2, 2, tm, D_MODEL), F32),
            pltpu.SemaphoreType.DMA((2,)),
        ],
        compiler_params=pltpu.CompilerParams(
            dimension_semantics=("arbitrary",), vmem_limit_bytes=VMEM_LIMIT),
        name="combine",
    )(dest_t, dest_t, *x2s, *rts, yb)


def _hier_moe(parts, cnt, w_gate, w_up, w_down, tm, blk):
    counts = cnt[0, ROUTER_COL0:ROUTER_COL0 + N_EXPERTS].astype(jnp.int32)
    padded = (counts + blk - 1) // blk * blk
    pad_end = jnp.cumsum(padded)
    pad_start = pad_end - padded
    t_all = sum(p[0].shape[0] for p in parts)
    n_blocks = (2 * t_all + N_EXPERTS * (blk - 1)) // blk + 1
    rows = n_blocks * blk
    blk_row0 = jnp.arange(n_blocks, dtype=jnp.int32) * blk
    block_e = jnp.minimum(
        jnp.sum((pad_end[None, :] <= blk_row0[:, None]).astype(jnp.int32), axis=1),
        N_EXPERTS - 1)
    n_used = (pad_end[-1] // blk).astype(jnp.int32).reshape(1)
    ids = jnp.arange(N_EXPERTS, dtype=jnp.int32)
    later = (ids[None, :] > ids[:, None]) & (padded[None, :] > 0)
    next_e = jnp.where(jnp.any(later, axis=1),
                       jnp.min(jnp.where(later, ids[None, :], N_EXPERTS), axis=1), ids)
    experts = jnp.arange(N_EXPERTS, dtype=jnp.int32)[:, None, None]

    dests = []
    for x2, _, _, rtt in parts:
        T = x2.shape[0]
        flat = lambda a: jnp.swapaxes(a, 0, 1).reshape(a.shape[1], T)
        eid = flat(rtt[:, 0:2, :]).astype(jnp.int32)
        rank = flat(rtt[:, 4:6, :]).astype(jnp.int32)
        dest = rank + jnp.sum(
            jnp.where(eid[None] == experts, pad_start[:, None, None], 0), axis=0)
        nt = T // tm
        dests.append(dest.reshape(2, nt, tm).transpose(1, 0, 2).reshape(nt, 1, 2 * tm))
    dest_t = jnp.concatenate(dests, axis=0)
    xs = _dispatch(pad_end, padded, dest_t, [p[1] for p in parts], rows, tm, blk)
    yb = _moe(block_e, n_used, next_e.astype(jnp.int32), xs, w_gate, w_up, w_down, blk)
    return _combine(dest_t, [p[0] for p in parts], [p[2] for p in parts], yb, tm)


def _rope_table(pos):
    half = ROPE_DIM // 2
    inv = ROPE_THETA ** (-jnp.arange(0, ROPE_DIM, 2, dtype=F32) / ROPE_DIM)
    ang = pos.astype(F32)[:, None] * inv[None, :]
    cos, sin = jnp.cos(ang), jnp.sin(ang)
    L = pos.shape[0]
    pad = jnp.zeros((L, HEAD_DIM - ROPE_DIM), F32)
    zero = jnp.zeros((L, half), F32)
    c64 = jnp.concatenate([cos, cos, pad + 1.0], axis=1)
    lo64 = jnp.concatenate([-sin, zero, pad], axis=1)
    hi64 = jnp.concatenate([zero, sin, pad], axis=1)
    two = lambda t: jnp.concatenate([t, t], axis=1)
    return jnp.concatenate([two(c64), two(lo64), two(hi64)], axis=1)


def _mixers(x, pos_rope, kctx_prev, vctx_prev, h0r, h0i, mk, mv, wp, sp, cnt0, *,
            tm_in, tq, ssm_l, tm_mid):
    B, S, _ = x.shape
    T = B * S
    q, k3, v3, u_tm = _in_proj(x, wp["gmix"], wp["win"], wp["gq"], wp["gk"], pos_rope, *tm_in)
    if kctx_prev is None:
        kctx, vctx = k3, v3
    else:
        kctx = jnp.concatenate([kctx_prev, k3], axis=1)
        vctx = jnp.concatenate([vctx_prev, v3], axis=1)
    att = _swa(wp["sink"], q, kctx, vctx, tq, mask_context=kctx_prev is None)
    ssm_tm, hr, hi = _ssm(u_tm.reshape(S, B, SSM_WIDTH), h0r, h0i, sp, ssm_l)
    x2, hn, rt, rtt, cnt = _mid(x, att, ssm_tm.reshape(S, B * SSM_WIDTH), mk, mv, wp, cnt0,
                                *tm_mid)
    part = (x2.reshape(T, D_MODEL), hn.reshape(T, D_MODEL), rt.reshape(T, LANES), rtt)
    return part, cnt, k3, v3, hr, hi


def kernel(x_prompt, x_sample, cache_attn_k, cache_attn_v, state_ssm_re, state_ssm_im, cache_mem_k, cache_mem_v, mem_prompt, norm_mix, w_in, q_norm, k_norm, attn_sink, ssm_lambda_re, ssm_lambda_im, ssm_log_dt, ssm_b_re, ssm_b_im, ssm_c_re, ssm_c_im, ssm_d, ssm_w_glu, ssm_b_glu, norm_attn_out, norm_ssm_out, w_out, norm_cross, norm_mem, w_cq, w_ck, w_cv, cq_norm, ck_norm, w_co, norm_ffn, w_router_group, b_router_group, w_router_expert, b_router_expert, w_e_gate, w_e_up, w_e_down):
    depth = norm_mix.shape[0]
    Bp, Lp, _ = x_prompt.shape
    Bs, Ls, _ = x_sample.shape
    yp, ys = x_prompt, x_sample
    rope_p = _rope_table(jnp.arange(Lp, dtype=jnp.int32))
    rope_s = _rope_table(PAST_LEN + jnp.arange(Ls, dtype=jnp.int32))
    outs = [[] for _ in range(10)]
    n_router = N_EXPERT_GROUPS + N_EXPERTS
    for l in range(depth):
        row = lambda a: a[l].astype(F32).reshape(1, -1)
        w_r = jnp.pad(jnp.concatenate([w_router_group[l], w_router_expert[l]], axis=1).astype(F32),
                      ((0, 0), (0, LANES - n_router)))
        w_r_hi = w_r.astype(BF16)
        w_r_lo = (w_r - w_r_hi.astype(F32)).astype(BF16)
        b_r = jnp.pad(jnp.concatenate([b_router_group[l], b_router_expert[l]]).astype(F32),
                      (0, LANES - n_router)).reshape(1, LANES)
        wp = {
            "gmix": row(norm_mix), "win": w_in[l].astype(BF16),
            "gq": jnp.tile(row(q_norm), (1, LANES // HEAD_DIM)),
            "gk": jnp.tile(row(k_norm), (1, LANES // HEAD_DIM)),
            "sink": attn_sink[l].astype(F32),
            "gao": row(norm_attn_out), "gso": row(norm_ssm_out),
            "wout": w_out[l].astype(BF16), "gx": row(norm_cross),
            "wcq": w_cq[l].astype(BF16), "gcq": row(cq_norm),
            "wco": w_co[l].astype(BF16), "gffn": row(norm_ffn),
            "wr": jnp.concatenate([w_r_hi, w_r_lo], axis=1), "br": b_r,
        }
        sp = _ssm_params(ssm_lambda_re[l], ssm_lambda_im[l], ssm_log_dt[l], ssm_b_re[l],
                         ssm_b_im[l], ssm_c_re[l], ssm_c_im[l], ssm_d[l], ssm_w_glu[l],
                         ssm_b_glu[l])
        ew = (w_e_gate[l].astype(F32), w_e_up[l].astype(F32), w_e_down[l].astype(F32))

        w_ckv = jnp.concatenate([w_ck[l], w_cv[l]], axis=1).astype(BF16)
        mkp, mvp = _memkv(mem_prompt.reshape(Bp * N_MEM, D_MODEL), row(norm_mem), w_ckv,
                          row(ck_norm), 512)
        mkp = mkp.reshape(Bp, N_MEM, CA_WIDTH)
        mvp = mvp.reshape(Bp, N_MEM, CA_WIDTH)

        zst = jnp.zeros((Bp, SSM_COLS), F32)
        part_p, cnt_p, kp, vp, hpr, hpi = _mixers(
            yp, rope_p, None, None, zst, zst, mkp, mvp, wp, sp, jnp.zeros((1, LANES), F32),
            tm_in=(1, 512), tq=256, ssm_l=64, tm_mid=(1, 512))
        part_s, cnt_s, kn, vn, hsr, hsi = _mixers(
            ys, rope_s, cache_attn_k[l].reshape(Bs, WINDOW, KV_WIDTH).astype(F32),
            cache_attn_v[l].reshape(Bs, WINDOW, KV_WIDTH).astype(F32),
            state_ssm_re[l].astype(F32).reshape(Bs, SSM_COLS),
            state_ssm_im[l].astype(F32).reshape(Bs, SSM_COLS),
            cache_mem_k[l].astype(F32).reshape(Bs, N_MEM, CA_WIDTH),
            cache_mem_v[l].astype(F32).reshape(Bs, N_MEM, CA_WIDTH), wp, sp, cnt_p,
            tm_in=(8, Ls), tq=CHUNK, ssm_l=Ls, tm_mid=(8, Ls))
        yp, ys = _hier_moe([part_p, part_s], cnt_s, *ew, 256, 2 * MOE_BLOCK)
        yp = yp.reshape(Bp, Lp, D_MODEL)
        ys = ys.reshape(Bs, Ls, D_MODEL)

        sg = (N_SSM_GROUPS, SSM_STATE)
        kvs = (N_KV_HEADS, HEAD_DIM)
        vals = (kp[:, Lp - WINDOW:].reshape(Bp, WINDOW, *kvs),
                vp[:, Lp - WINDOW:].reshape(Bp, WINDOW, *kvs),
                hpr.reshape(Bp, *sg), hpi.reshape(Bp, *sg),
                mkp.reshape(Bp, N_MEM, CA_HEADS, CA_HEAD_DIM),
                mvp.reshape(Bp, N_MEM, CA_HEADS, CA_HEAD_DIM),
                kn.reshape(Bs, Ls, *kvs), vn.reshape(Bs, Ls, *kvs),
                hsr.reshape(Bs, *sg), hsi.reshape(Bs, *sg))
        for lst, val in zip(outs, vals):
            lst.append(val)
    return (yp, ys) + tuple(jnp.stack(lst) for lst in outs)
```

```python
import functools
import math

import jax
import jax.numpy as jnp
from jax import lax
from jax.experimental import pallas as pl
from jax.experimental.pallas import tpu as pltpu

F32 = jnp.float32
BF16 = jnp.bfloat16

D_MODEL = 1024
CHUNK = 64
N_Q_HEADS = 8
N_KV_HEADS = 2
GQA = N_Q_HEADS // N_KV_HEADS
HEAD_DIM = 64
WINDOW = 128
BAND = WINDOW + CHUNK
ROPE_DIM = HEAD_DIM // 4
ROPE_THETA = 500000.0
ATT_WIDTH = N_Q_HEADS * HEAD_DIM
KV_WIDTH = N_KV_HEADS * HEAD_DIM
SSM_GROUP = 16
SSM_WIDTH = D_MODEL // 2
N_SSM_GROUPS = SSM_WIDTH // SSM_GROUP
SSM_STATE = 64
SSM_COLS = N_SSM_GROUPS * SSM_STATE
IN_WIDTH = ATT_WIDTH + 2 * KV_WIDTH + SSM_WIDTH
N_MEM = 256
CA_HEADS = 4
CA_HEAD_DIM = 128
CA_WIDTH = CA_HEADS * CA_HEAD_DIM
N_EXPERT_GROUPS = 4
EXPERTS_PER_GROUP = 8
N_EXPERTS = N_EXPERT_GROUPS * EXPERTS_PER_GROUP
D_EXPERT = 512
MOE_BLOCK = 256
EPS = 1e-6
NEG = -1e30
PAST_LEN = 4096

LANES = 128
ROUTER_COL0 = N_EXPERT_GROUPS
VMEM_LIMIT = 48 * 1024 * 1024


def _rms(x, g):
    ms = jnp.mean(x * x, axis=-1, keepdims=True)
    return (x * lax.rsqrt(ms + EPS)) * g


def _mm(a, b):
    return jnp.dot(a, b, preferred_element_type=F32)


_HI_HALF = 0xFFFF0000


def _pack_bf16_pairs(x):
    half = x.shape[1] // 2
    bits = lambda v: lax.bitcast_convert_type(v.astype(BF16).astype(F32), jnp.uint32)
    return (lax.shift_right_logical(bits(x[:, :half]), jnp.uint32(16))
            | (bits(x[:, half:]) & jnp.uint32(_HI_HALF)))


def _unpack_bf16_pairs(w):
    lo = lax.bitcast_convert_type(lax.shift_left(w, jnp.uint32(16)), F32)
    hi = lax.bitcast_convert_type(w & jnp.uint32(_HI_HALF), F32)
    return jnp.concatenate([lo.astype(BF16), hi.astype(BF16)], axis=1)


def _in_proj_kernel(x_ref, g_ref, w_ref, gq_ref, gk_ref, rope_ref,
                    q_ref, k_ref, v_ref, u_ref):
    nb, ts, _ = x_ref.shape
    tm = nb * ts
    h = _rms(x_ref[...].reshape(tm, D_MODEL), g_ref[...])
    hin = _mm(h.astype(BF16), w_ref[...])
    rope = jnp.concatenate([rope_ref[...]] * nb, axis=0)
    cos = rope[:, 0:LANES]
    sin_lo = rope[:, LANES:2 * LANES]
    sin_hi = rope[:, 2 * LANES:3 * LANES]
    lane = lax.broadcasted_iota(jnp.int32, (tm, LANES), 1)
    left = lane < HEAD_DIM

    def norm_rope(z, g):
        sq = z * z
        lsum = jnp.sum(jnp.where(left, sq, 0.0), axis=-1, keepdims=True)
        rsum = jnp.sum(jnp.where(left, 0.0, sq), axis=-1, keepdims=True)
        ms = jnp.where(left, lsum, rsum) * (1.0 / HEAD_DIM)
        zn = (z * lax.rsqrt(ms + EPS)) * g
        half = ROPE_DIM // 2
        return (zn * cos + pltpu.roll(zn, LANES - half, 1) * sin_lo
                + pltpu.roll(zn, half, 1) * sin_hi)

    for j in range(ATT_WIDTH // LANES):
        sl = slice(j * LANES, (j + 1) * LANES)
        q_ref[:, :, sl] = norm_rope(hin[:, sl], gq_ref[...]).reshape(nb, ts, LANES)
    k_ref[...] = norm_rope(hin[:, ATT_WIDTH:ATT_WIDTH + KV_WIDTH],
                           gk_ref[...]).reshape(nb, ts, KV_WIDTH)
    v_ref[...] = hin[:, ATT_WIDTH + KV_WIDTH:ATT_WIDTH + 2 * KV_WIDTH].reshape(nb, ts, KV_WIDTH)
    for b in range(nb):
        u_ref[:, b * SSM_WIDTH:(b + 1) * SSM_WIDTH] = (
            hin[b * ts:(b + 1) * ts, ATT_WIDTH + 2 * KV_WIDTH:])


def _in_proj(x, g, w_bf, gq, gk, rope, nb, ts):
    B, S, _ = x.shape
    full = lambda b, i: (0, 0)
    tile = lambda w: pl.BlockSpec((nb, ts, w), lambda b, i: (b, i, 0))
    return pl.pallas_call(
        _in_proj_kernel,
        grid=(B // nb, S // ts),
        in_specs=[
            tile(D_MODEL),
            pl.BlockSpec((1, D_MODEL), full),
            pl.BlockSpec((D_MODEL, IN_WIDTH), full),
            pl.BlockSpec((1, LANES), full),
            pl.BlockSpec((1, LANES), full),
            pl.BlockSpec((ts, 3 * LANES), lambda b, i: (i, 0)),
        ],
        out_specs=[
            tile(ATT_WIDTH), tile(KV_WIDTH), tile(KV_WIDTH),
            pl.BlockSpec((ts, nb * SSM_WIDTH), lambda b, i: (i, b)),
        ],
        out_shape=[
            jax.ShapeDtypeStruct((B, S, ATT_WIDTH), F32),
            jax.ShapeDtypeStruct((B, S, KV_WIDTH), F32),
            jax.ShapeDtypeStruct((B, S, KV_WIDTH), F32),
            jax.ShapeDtypeStruct((S, B * SSM_WIDTH), F32),
        ],
        compiler_params=pltpu.CompilerParams(
            dimension_semantics=("arbitrary", "arbitrary"),
            vmem_limit_bytes=VMEM_LIMIT),
        name="in_proj",
    )(x, g, w_bf, gq, gk, rope)


def _swa_kernel(sink_ref, q_ref, k_ref, v_ref, o_ref, *, mask_context):
    tq = q_ref.shape[1]
    i = pl.program_id(1)
    nch = tq // CHUNK
    lane = lax.broadcasted_iota(jnp.int32, (BAND, LANES), 1)
    lo_half = lane < HEAD_DIM
    vrow_lo = lax.broadcasted_iota(jnp.int32, (LANES, BAND), 0) < HEAD_DIM
    q_lo = lax.broadcasted_iota(jnp.int32, (1, LANES), 1) < CHUNK
    slabs_per_kv = GQA * HEAD_DIM // LANES

    units = []
    scores = []
    vpads = {}
    for c in range(nch):
        chunk = i * nch + c
        if mask_context:
            first = jnp.maximum(chunk - WINDOW // CHUNK, 0)
            start = pl.multiple_of(first * CHUNK, CHUNK)
            kidx = start + lax.broadcasted_iota(jnp.int32, (BAND, LANES), 0)
            valid = kidx < (chunk + 1) * CHUNK
        else:
            start = pl.multiple_of(chunk * CHUNK, CHUNK)
        kb = k_ref[0, pl.ds(start, BAND), :]
        kb_sw = pltpu.roll(kb, HEAD_DIM, 1)
        vt = v_ref[0, pl.ds(start, BAND), :].T
        vt_sw = jnp.concatenate([vt[HEAD_DIM:], vt[:HEAD_DIM]], axis=0)
        for kvh in range(N_KV_HEADS):
            k_own, k_oth = (kb, kb_sw) if kvh == 0 else (kb_sw, kb)
            v_own, v_oth = (vt, vt_sw) if kvh == 0 else (vt_sw, vt)
            kpad = (jnp.where(lo_half, k_own, 0.0).astype(BF16),
                    jnp.where(lo_half, 0.0, k_oth).astype(BF16))
            vpads[(c, kvh)] = (jnp.where(vrow_lo, v_own, 0.0).astype(BF16),
                               jnp.where(vrow_lo, 0.0, v_oth).astype(BF16))
            col0 = kvh * GQA * HEAD_DIM
            q2 = jnp.concatenate(
                [q_ref[0, c * CHUNK:(c + 1) * CHUNK, col0 + m * LANES:col0 + (m + 1) * LANES]
                 for m in range(slabs_per_kv)], axis=0).astype(BF16)
            for side in range(2):
                s = lax.dot_general(kpad[side], q2, (((1,), (1,)), ((), ())),
                                    preferred_element_type=F32) * (HEAD_DIM ** -0.5)
                if mask_context:
                    s = jnp.where(valid, s, NEG)
                units.append((c, kvh, side))
                scores.append(s)

    sinks = [jnp.where(q_lo, sink_ref[kvh * GQA + side], sink_ref[kvh * GQA + 2 + side])
             for (_, kvh, side) in units]
    maxes = [jnp.maximum(jnp.max(s, axis=0, keepdims=True), sk)
             for s, sk in zip(scores, sinks)]
    exps = [jnp.exp(s - mx) for s, mx in zip(scores, maxes)]
    dens = [jnp.sum(p, axis=0, keepdims=True) + jnp.exp(sk - mx)
            for p, sk, mx in zip(exps, sinks, maxes)]
    probs = [(p * (1.0 / den)).astype(BF16) for p, den in zip(exps, dens)]

    for n in range(0, len(units), 2):
        c, kvh, _ = units[n]
        vp = vpads[(c, kvh)]
        o = (_mm(vp[0], probs[n]) + _mm(vp[1], probs[n + 1])).T
        col0 = kvh * GQA * HEAD_DIM
        for m in range(slabs_per_kv):
            o_ref[0, c * CHUNK:(c + 1) * CHUNK, col0 + m * LANES:col0 + (m + 1) * LANES] = (
                o[m * CHUNK:(m + 1) * CHUNK])


def _swa(sink, q, kctx, vctx, tq, mask_context):
    B, Sq, _ = q.shape
    Sk = kctx.shape[1]
    return pl.pallas_call(
        functools.partial(_swa_kernel, mask_context=mask_context),
        grid=(B, Sq // tq),
        in_specs=[
            pl.BlockSpec(memory_space=pltpu.SMEM),
            pl.BlockSpec((1, tq, ATT_WIDTH), lambda b, i: (b, i, 0)),
            pl.BlockSpec((1, Sk, KV_WIDTH), lambda b, i: (b, 0, 0)),
            pl.BlockSpec((1, Sk, KV_WIDTH), lambda b, i: (b, 0, 0)),
        ],
        out_specs=pl.BlockSpec((1, tq, ATT_WIDTH), lambda b, i: (b, i, 0)),
        out_shape=jax.ShapeDtypeStruct((B, Sq, ATT_WIDTH), F32),
        compiler_params=pltpu.CompilerParams(
            dimension_semantics=("arbitrary", "arbitrary"),
            vmem_limit_bytes=VMEM_LIMIT),
        name="swa",
    )(sink, q, kctx, vctx)


def _ssm_kernel(u_ref, h0r_ref, h0i_ref, lam_ref, bre_ref, bim_ref, cre_ref, cim_ref,
                d_ref, wglu_ref, bglu_ref,
                y_ref, hr_out, hi_out, sr0, si0, sr1, si1, hr_s, hi_s):
    L, B, _ = u_ref.shape
    rows = L * B
    half_w = SSM_WIDTH // 2
    half_c = SSM_COLS // 2
    halves = ((sr0, si0), (sr1, si1))

    @pl.when(pl.program_id(0) == 0)
    def _():
        hr_s[...] = h0r_ref[...]
        hi_s[...] = h0i_ref[...]

    u = u_ref[...].reshape(rows, SSM_WIDTH)
    ub = u.astype(BF16)

    def project_in(hf):
        sr, si = halves[hf]
        uh = ub[:, hf * half_w:(hf + 1) * half_w]
        sr[...] = _mm(uh, bre_ref[hf])
        si[...] = _mm(uh, bim_ref[hf])

    def recur(hf):
        sr, si = halves[hf]
        cw = 4 * LANES
        for cc in range(half_c // cw):
            cols = slice(cc * cw, (cc + 1) * cw)
            gcols = slice(hf * half_c + cc * cw, hf * half_c + (cc + 1) * cw)
            lr = jnp.broadcast_to(lam_ref[0:1, gcols], (B, cw))
            li = jnp.broadcast_to(lam_ref[1:2, gcols], (B, cw))
            hr, hi = hr_s[:, gcols], hi_s[:, gcols]
            for t in range(L):
                at_t = slice(t * B, (t + 1) * B)
                hr, hi = (lr * hr - li * hi + sr[at_t, cols],
                          lr * hi + li * hr + si[at_t, cols])
                sr[at_t, cols] = hr
                si[at_t, cols] = hi
            hr_s[:, gcols] = hr
            hi_s[:, gcols] = hi

    def project_out(hf):
        sr, si = halves[hf]
        return (_mm(sr[...].astype(BF16), cre_ref[hf]) + _mm(si[...].astype(BF16), cim_ref[hf]))

    project_in(0)
    project_in(1)
    recur(0)
    y0 = project_out(0)
    recur(1)
    y1 = project_out(1)
    y = jnp.concatenate([y0, y1], axis=1) + d_ref[...] * u
    g = 0.5 * y * (1.0 + jnp.tanh(math.sqrt(2.0 / math.pi) * (y + 0.044715 * (y * y * y))))
    gb = g.astype(BF16)
    z = jnp.concatenate(
        [_mm(gb[:, hf * half_w:(hf + 1) * half_w], wglu_ref[hf]) for hf in range(2)],
        axis=1) + bglu_ref[...]
    out = g * (1.0 / (1.0 + jnp.exp(-z)))
    y_ref[...] = out.reshape(L, B, SSM_WIDTH)
    hr_out[...] = hr_s[...]
    hi_out[...] = hi_s[...]


def _ssm(u, h0r, h0i, sp, L):
    S, B, _ = u.shape
    c2 = lambda i: (0, 0)
    c3 = lambda i: (0, 0, 0)
    return pl.pallas_call(
        _ssm_kernel,
        grid=(S // L,),
        in_specs=[
            pl.BlockSpec((L, B, SSM_WIDTH), lambda i: (i, 0, 0)),
            pl.BlockSpec((B, SSM_COLS), c2),
            pl.BlockSpec((B, SSM_COLS), c2),
            pl.BlockSpec((2, SSM_COLS), c2),
            pl.BlockSpec((2, SSM_WIDTH // 2, SSM_COLS // 2), c3),
            pl.BlockSpec((2, SSM_WIDTH // 2, SSM_COLS // 2), c3),
            pl.BlockSpec((2, SSM_COLS // 2, SSM_WIDTH // 2), c3),
            pl.BlockSpec((2, SSM_COLS // 2, SSM_WIDTH // 2), c3),
            pl.BlockSpec((1, SSM_WIDTH), c2),
            pl.BlockSpec((2, SSM_WIDTH // 2, SSM_WIDTH // 2), c3),
            pl.BlockSpec((1, SSM_WIDTH), c2),
        ],
        out_specs=[
            pl.BlockSpec((L, B, SSM_WIDTH), lambda i: (i, 0, 0)),
            pl.BlockSpec((B, SSM_COLS), c2),
            pl.BlockSpec((B, SSM_COLS), c2),
        ],
        out_shape=[
            jax.ShapeDtypeStruct((S, B, SSM_WIDTH), F32),
            jax.ShapeDtypeStruct((B, SSM_COLS), F32),
            jax.ShapeDtypeStruct((B, SSM_COLS), F32),
        ],
        scratch_shapes=[
            pltpu.VMEM((L * B, SSM_COLS // 2), F32),
            pltpu.VMEM((L * B, SSM_COLS // 2), F32),
            pltpu.VMEM((L * B, SSM_COLS // 2), F32),
            pltpu.VMEM((L * B, SSM_COLS // 2), F32),
            pltpu.VMEM((B, SSM_COLS), F32),
            pltpu.VMEM((B, SSM_COLS), F32),
        ],
        compiler_params=pltpu.CompilerParams(
            dimension_semantics=("arbitrary",), vmem_limit_bytes=VMEM_LIMIT),
        name="ssm",
    )(u, h0r, h0i, sp["lam"], sp["bre"], sp["bim"], sp["cre"], sp["cim"],
      sp["d"], sp["wglu"], sp["bglu"])


def _block_diag(blocks):
    G, r, c = blocks.shape
    col = jnp.arange(G * c, dtype=jnp.int32)
    spread = (col[None, :] % c == jnp.arange(c, dtype=jnp.int32)[:, None]).astype(F32)
    same_group = (jnp.arange(G * r, dtype=jnp.int32)[:, None] // r) == (col[None, :] // c)
    tiled = jnp.dot(blocks.reshape(G * r, c), spread, precision=lax.Precision.HIGHEST)
    return jnp.where(same_group, tiled, 0.0)


def _ssm_params(lam_re, lam_im, log_dt, b_re, b_im, c_re, c_im, d, w_glu, b_glu):
    lam = lax.complex(lam_re.astype(F32), lam_im.astype(F32))
    dt = jnp.exp(log_dt.astype(F32))[:, None]
    lam_bar = jnp.exp(lam * dt)
    bmat = lax.complex(b_re.astype(F32), b_im.astype(F32))
    b_bar = ((lam_bar - 1.0) / lam)[..., None] * bmat
    lam2 = jnp.stack([lam_bar.real.reshape(-1), lam_bar.imag.reshape(-1)])
    bt = jnp.swapaxes(b_bar, 1, 2)
    hw, hc = SSM_WIDTH // 2, SSM_COLS // 2
    split_b = lambda m: jnp.stack([m[:hw, :hc], m[hw:, hc:]]).astype(BF16)
    split_c = lambda m: jnp.stack([m[:hc, :hw], m[hc:, hw:]]).astype(BF16)
    ct_re = jnp.swapaxes(c_re.astype(F32), 1, 2)
    ct_im = jnp.swapaxes(c_im.astype(F32), 1, 2)
    wg = _block_diag(w_glu.astype(F32))
    return {
        "lam": lam2,
        "bre": split_b(_block_diag(bt.real)),
        "bim": split_b(_block_diag(bt.imag)),
        "cre": split_c(_block_diag(ct_re)),
        "cim": split_c(_block_diag(-ct_im)),
        "d": d.astype(F32).reshape(1, SSM_WIDTH),
        "wglu": jnp.stack([wg[:hw, :hw], wg[hw:, hw:]]).astype(BF16),
        "bglu": b_glu.astype(F32).reshape(1, SSM_WIDTH),
    }


def _memkv_kernel(m_ref, g_ref, w_ref, gk_ref, k_ref, v_ref):
    m = _rms(m_ref[...], g_ref[...])
    kv = _mm(m.astype(BF16), w_ref[...])
    for h in range(CA_HEADS):
        sl = slice(h * CA_HEAD_DIM, (h + 1) * CA_HEAD_DIM)
        k_ref[:, sl] = _rms(kv[:, sl], gk_ref[...])
    v_ref[...] = kv[:, CA_WIDTH:]


def _memkv(mem2d, g, w_bf, gk, tm):
    T = mem2d.shape[0]
    full = lambda i: (0, 0)
    return pl.pallas_call(
        _memkv_kernel,
        grid=(T // tm,),
        in_specs=[
            pl.BlockSpec((tm, D_MODEL), lambda i: (i, 0)),
            pl.BlockSpec((1, D_MODEL), full),
            pl.BlockSpec((D_MODEL, 2 * CA_WIDTH), full),
            pl.BlockSpec((1, CA_HEAD_DIM), full),
        ],
        out_specs=[
            pl.BlockSpec((tm, CA_WIDTH), lambda i: (i, 0)),
            pl.BlockSpec((tm, CA_WIDTH), lambda i: (i, 0)),
        ],
        out_shape=[
            jax.ShapeDtypeStruct((T, CA_WIDTH), F32),
            jax.ShapeDtypeStruct((T, CA_WIDTH), F32),
        ],
        compiler_params=pltpu.CompilerParams(
            dimension_semantics=("arbitrary",), vmem_limit_bytes=VMEM_LIMIT),
        name="memkv",
    )(mem2d, g, w_bf, gk)


def _mid_kernel(x_ref, att_ref, ssm_ref, mk_ref, mv_ref,
                gao_ref, gso_ref, wout_ref, gx_ref, wcq_ref, gcq_ref, wco_ref,
                gffn_ref, wr_ref, br_ref, cnt0_ref, tri_ref,
                x2_ref, hn_ref, rt_ref, rtt_ref, cnt_ref, base_s):
    nb, ts, _ = x_ref.shape
    tm = nb * ts

    @pl.when((pl.program_id(0) == 0) & (pl.program_id(1) == 0))
    def _():
        base_s[...] = cnt0_ref[...]

    ssm = jnp.concatenate(
        [ssm_ref[:, b * SSM_WIDTH:(b + 1) * SSM_WIDTH] for b in range(nb)], axis=0)
    a = _rms(att_ref[...].reshape(tm, ATT_WIDTH), gao_ref[...]).astype(BF16)
    s = _rms(ssm, gso_ref[...]).astype(BF16)
    x1 = (x_ref[...].reshape(tm, D_MODEL) + _mm(a, wout_ref[0:ATT_WIDTH, :])
          + _mm(s, wout_ref[ATT_WIDTH:, :]))

    qx = _mm(_rms(x1, gx_ref[...]).astype(BF16), wcq_ref[...])
    heads = []
    for h in range(CA_HEADS):
        sl = slice(h * CA_HEAD_DIM, (h + 1) * CA_HEAD_DIM)
        qh = _rms(qx[:, sl], gcq_ref[...]).astype(BF16)
        per_batch = []
        for b in range(nb):
            kh = mk_ref[b, :, sl].astype(BF16)
            qb = qh[b * ts:(b + 1) * ts]
            if ts <= LANES:
                vt = mv_ref[b, :, sl].T.astype(BF16)
                sc = lax.dot_general(kh, qb, (((1,), (1,)), ((), ())),
                                     preferred_element_type=F32) * (CA_HEAD_DIM ** -0.5)
                p = jnp.exp(sc - jnp.max(sc, axis=0, keepdims=True))
                p = p * (1.0 / jnp.sum(p, axis=0, keepdims=True))
                per_batch.append(_mm(vt, p.astype(BF16)).T)
            else:
                vh = mv_ref[b, :, sl].astype(BF16)
                sc = lax.dot_general(qb, kh, (((1,), (1,)), ((), ())),
                                     preferred_element_type=F32) * (CA_HEAD_DIM ** -0.5)
                p = jnp.exp(sc - jnp.max(sc, axis=-1, keepdims=True))
                p = p / jnp.sum(p, axis=-1, keepdims=True)
                per_batch.append(_mm(p.astype(BF16), vh))
        heads.append(jnp.concatenate(per_batch, axis=0))
    o = jnp.concatenate(heads, axis=1).astype(BF16)
    x2 = x1 + _mm(o, wco_ref[...])
    x2_ref[...] = x2.reshape(nb, ts, D_MODEL)

    hn = _rms(x2, gffn_ref[...])
    hn_ref[...] = hn.reshape(nb, ts, D_MODEL)

    h_hi = hn.astype(BF16)
    h_lo = (hn - h_hi.astype(F32)).astype(BF16)
    r1 = _mm(h_hi, wr_ref[...])
    lg = (r1[:, :LANES] + r1[:, LANES:] + _mm(h_lo, wr_ref[:, 0:LANES])
          + br_ref[...])

    col = lax.broadcasted_iota(jnp.int32, (tm, LANES), 1)
    big = jnp.int32(4 * LANES)
    gmask = col < N_EXPERT_GROUPS
    lgg = jnp.where(gmask, lg, NEG)
    mg = jnp.max(lgg, axis=-1, keepdims=True)
    grp = jnp.min(jnp.where(gmask & (lgg == mg), col, big), axis=-1, keepdims=True)
    pg_top = 1.0 / jnp.sum(jnp.where(gmask, jnp.exp(lgg - mg), 0.0), axis=-1, keepdims=True)

    ecol = col - ROUTER_COL0
    emask = ((ecol >= 0) & (ecol < N_EXPERTS)
             & (lax.shift_right_arithmetic(ecol, 3) == grp))
    le = jnp.where(emask, lg, NEG)
    m1 = jnp.max(le, axis=-1, keepdims=True)
    i1 = jnp.min(jnp.where(emask & (le == m1), col, big), axis=-1, keepdims=True)
    rest = emask & (col != i1)
    le2 = jnp.where(rest, lg, NEG)
    m2 = jnp.max(le2, axis=-1, keepdims=True)
    i2 = jnp.min(jnp.where(rest & (le2 == m2), col, big), axis=-1, keepdims=True)
    den = jnp.sum(jnp.where(emask, jnp.exp(le - m1), 0.0), axis=-1, keepdims=True)
    p1 = 1.0 / den
    p2 = jnp.exp(m2 - m1) / den
    gate1 = pg_top * p1 / (p1 + p2)
    gate2 = pg_top * p2 / (p1 + p2)

    sel1 = col == i1
    sel2 = col == i2
    oh = jnp.where(sel1 | sel2, 1.0, 0.0)
    tot = base_s[...] + _mm(tri_ref[...], oh.astype(BF16))
    rank1 = jnp.sum(jnp.where(sel1, tot, 0.0), axis=-1, keepdims=True)
    rank2 = jnp.sum(jnp.where(sel2, tot, 0.0), axis=-1, keepdims=True)
    base_s[...] = base_s[...] + jnp.sum(oh, axis=0, keepdims=True)
    cnt_ref[...] = base_s[...]

    e1 = (i1 - ROUTER_COL0).astype(F32)
    e2 = (i2 - ROUTER_COL0).astype(F32)
    rt = jnp.zeros((tm, LANES), F32)
    for k, val in enumerate((e1, e2, gate1, gate2, rank1, rank2)):
        rt = jnp.where(col == k, val, rt)
    rt_ref[...] = rt.reshape(nb, ts, LANES)
    for b in range(nb):
        rtt_ref[b] = rt[b * ts:(b + 1) * ts].T[0:8, :]


def _mid(x, att, ssm_tm, mk, mv, wp, cnt0, nb, ts):
    B, S, _ = x.shape
    c2 = lambda b, i: (0, 0)
    tile = lambda w: pl.BlockSpec((nb, ts, w), lambda b, i: (b, i, 0))
    return pl.pallas_call(
        _mid_kernel,
        grid=(B // nb, S // ts),
        in_specs=[
            tile(D_MODEL), tile(ATT_WIDTH),
            pl.BlockSpec((ts, nb * SSM_WIDTH), lambda b, i: (i, b)),
            pl.BlockSpec((nb, N_MEM, CA_WIDTH), lambda b, i: (b, 0, 0)),
            pl.BlockSpec((nb, N_MEM, CA_WIDTH), lambda b, i: (b, 0, 0)),
            pl.BlockSpec((1, ATT_WIDTH), c2),
            pl.BlockSpec((1, SSM_WIDTH), c2),
            pl.BlockSpec((ATT_WIDTH + SSM_WIDTH, D_MODEL), c2),
            pl.BlockSpec((1, D_MODEL), c2),
            pl.BlockSpec((D_MODEL, CA_WIDTH), c2),
            pl.BlockSpec((1, CA_HEAD_DIM), c2),
            pl.BlockSpec((CA_WIDTH, D_MODEL), c2),
            pl.BlockSpec((1, D_MODEL), c2),
            pl.BlockSpec((D_MODEL, 2 * LANES), c2),
            pl.BlockSpec((1, LANES), c2),
            pl.BlockSpec((1, LANES), c2),
            pl.BlockSpec((nb * ts, nb * ts), c2),
        ],
        out_specs=[
            tile(D_MODEL), tile(D_MODEL), tile(LANES),
            pl.BlockSpec((nb, 8, ts), lambda b, i: (b, 0, i)),
            pl.BlockSpec((1, LANES), c2),
        ],
        out_shape=[
            jax.ShapeDtypeStruct((B, S, D_MODEL), F32),
            jax.ShapeDtypeStruct((B, S, D_MODEL), F32),
            jax.ShapeDtypeStruct((B, S, LANES), F32),
            jax.ShapeDtypeStruct((B, 8, S), F32),
            jax.ShapeDtypeStruct((1, LANES), F32),
        ],
        scratch_shapes=[pltpu.VMEM((1, LANES), F32)],
        compiler_params=pltpu.CompilerParams(
            dimension_semantics=("arbitrary", "arbitrary"),
            vmem_limit_bytes=VMEM_LIMIT),
        name="mid",
    )(x, att, ssm_tm, mk, mv, wp["gao"], wp["gso"], wp["wout"], wp["gx"], wp["wcq"],
      wp["gcq"], wp["wco"], wp["gffn"], wp["wr"], wp["br"], cnt0,
      jnp.tri(nb * ts, k=-1, dtype=BF16))


def _select_part(i, tile_starts, refs):
    x = refs[0][...]
    for start, ref in zip(tile_starts[1:], refs[1:]):
        x = jnp.where(i >= start, ref[...], x)
    return x


def _part_spec(shape, tile_start, n_tiles):
    def index(i, *_):
        return (jnp.clip(i - tile_start, 0, n_tiles - 1),) + (0,) * (len(shape) - 1)
    return pl.BlockSpec(shape, index)


def _dispatch_kernel(pend_ref, padded_ref, dest_ref, *rest, tile_starts):
    n_parts = len(tile_starts)
    hn_refs = rest[:n_parts]
    xs_hbm, stage, zbuf, sem = rest[n_parts:]
    tm = hn_refs[0].shape[0]
    i = pl.program_id(0)
    slot = lax.rem(i, 2)
    blk = zbuf.shape[0]

    def wait_rows(s):
        for _ in range(2):
            pltpu.make_async_copy(stage.at[s], xs_hbm.at[pl.ds(0, tm)], sem.at[s]).wait()

    @pl.when(i == 0)
    def _():
        zbuf[...] = jnp.zeros_like(zbuf)
        for e in range(N_EXPERTS):
            @pl.when(padded_ref[e] > 0)
            def _():
                row0 = pl.multiple_of(pend_ref[e] - blk, blk)
                fill = pltpu.make_async_copy(zbuf, xs_hbm.at[pl.ds(row0, blk)], sem.at[2])
                fill.start()
                fill.wait()

        def fill_tail(b, carry):
            fill = pltpu.make_async_copy(
                zbuf, xs_hbm.at[pl.ds(pl.multiple_of(b * blk, blk), blk)], sem.at[2])
            fill.start()
            fill.wait()
            return carry

        lax.fori_loop(pend_ref[N_EXPERTS - 1] // blk, xs_hbm.shape[0] // blk, fill_tail, 0)

    @pl.when(i >= 2)
    def _():
        wait_rows(slot)

    tile = _pack_bf16_pairs(_select_part(i, tile_starts, hn_refs))
    for s in range(2):
        @pl.when(slot == s)
        def _():
            stage[s] = tile
            for k in range(2):
                for r in range(tm):
                    pltpu.make_async_copy(stage.at[s, pl.ds(r, 1), :],
                                          xs_hbm.at[pl.ds(dest_ref[0, 0, k * tm + r], 1), :],
                                          sem.at[s]).start(priority=r % 2)

    @pl.when(i == pl.num_programs(0) - 1)
    def _():
        wait_rows(slot)

        @pl.when(i >= 1)
        def _():
            wait_rows(1 - slot)


def _tile_layout(arrays, tm):
    counts = [a.shape[0] // tm for a in arrays]
    starts = [sum(counts[:p]) for p in range(len(counts))]
    return counts, starts


def _dispatch(pad_end, padded, dest_t, hns, rows, tm, blk):
    counts, starts = _tile_layout(hns, tm)
    grid_spec = pltpu.PrefetchScalarGridSpec(
        num_scalar_prefetch=2,
        grid=(sum(counts),),
        in_specs=[pl.BlockSpec((1, 1, 2 * tm), lambda i, pe, pd: (i, 0, 0),
                               memory_space=pltpu.SMEM)]
        + [_part_spec((tm, D_MODEL), s, n) for s, n in zip(starts, counts)],
        out_specs=pl.BlockSpec(memory_space=pl.ANY),
        scratch_shapes=[
            pltpu.VMEM((2, tm, D_MODEL // 2), jnp.uint32),
            pltpu.VMEM((blk, D_MODEL // 2), jnp.uint32),
            pltpu.SemaphoreType.DMA((3,)),
        ],
    )
    return pl.pallas_call(
        functools.partial(_dispatch_kernel, tile_starts=tuple(starts)),
        grid_spec=grid_spec,
        out_shape=jax.ShapeDtypeStruct((rows, D_MODEL // 2), jnp.uint32),
        compiler_params=pltpu.CompilerParams(
            dimension_semantics=("arbitrary",), vmem_limit_bytes=VMEM_LIMIT),
        name="dispatch",
    )(pad_end, padded, dest_t, *hns)


def _moe_kernel(be_ref, nu_ref, nxt_ref, xs_ref, wg_hbm, wu_hbm, wd_hbm, yb_ref,
                wg_f, wu_f, wd_f, wg_s, wu_s, wd_s, run_s, sem):
    i = pl.program_id(0)

    def fetch(e, slot):
        return [pltpu.make_async_copy(src.at[e], dst.at[slot], sem.at[slot])
                for src, dst in ((wg_hbm, wg_f), (wu_hbm, wu_f), (wd_hbm, wd_f))]

    @pl.when(i < nu_ref[0])
    def _():
        e = be_ref[i]

        @pl.when(i == 0)
        def _():
            run_s[0] = 0
            for c in fetch(e, 0):
                c.start()

        @pl.when((i == 0) | (e != be_ref[jnp.maximum(i - 1, 0)]))
        def _():
            slot = lax.rem(run_s[0], 2)
            run_s[0] = run_s[0] + 1
            for c in fetch(e, slot):
                c.wait()
            wg_s[...] = wg_f[slot].astype(BF16)
            wu_s[...] = wu_f[slot].astype(BF16)
            wd_s[...] = wd_f[slot].astype(BF16)

            @pl.when(nxt_ref[e] != e)
            def _():
                for c in fetch(nxt_ref[e], 1 - slot):
                    c.start()

        xe = _unpack_bf16_pairs(xs_ref[...])
        g = _mm(xe, wg_s[...])
        u = _mm(xe, wu_s[...])
        hmid = ((g * (1.0 / (1.0 + jnp.exp(-g)))) * u).astype(BF16)
        yb_ref[...] = _mm(hmid, wd_s[...])

    @pl.when(i >= nu_ref[0])
    def _():
        yb_ref[...] = jnp.zeros_like(yb_ref)


def _moe(block_e, n_used, next_e, xs, w_gate, w_up, w_down, blk):
    n_blocks = block_e.shape[0]
    in_blk = lambda i, be, nu, nx: (jnp.maximum(jnp.minimum(i, nu[0] - 1), 0), 0)
    grid_spec = pltpu.PrefetchScalarGridSpec(
        num_scalar_prefetch=3,
        grid=(n_blocks,),
        in_specs=[
            pl.BlockSpec((blk, D_MODEL // 2), in_blk),
            pl.BlockSpec(memory_space=pl.ANY),
            pl.BlockSpec(memory_space=pl.ANY),
            pl.BlockSpec(memory_space=pl.ANY),
        ],
        out_specs=pl.BlockSpec((blk, D_MODEL), lambda i, be, nu, nx: (i, 0)),
        scratch_shapes=[
            pltpu.VMEM((2, D_MODEL, D_EXPERT), F32),
            pltpu.VMEM((2, D_MODEL, D_EXPERT), F32),
            pltpu.VMEM((2, D_EXPERT, D_MODEL), F32),
            pltpu.VMEM((D_MODEL, D_EXPERT), BF16),
            pltpu.VMEM((D_MODEL, D_EXPERT), BF16),
            pltpu.VMEM((D_EXPERT, D_MODEL), BF16),
            pltpu.SMEM((1,), jnp.int32),
            pltpu.SemaphoreType.DMA((2,)),
        ],
    )
    return pl.pallas_call(
        _moe_kernel,
        grid_spec=grid_spec,
        out_shape=jax.ShapeDtypeStruct((xs.shape[0], D_MODEL), F32),
        compiler_params=pltpu.CompilerParams(
            dimension_semantics=("arbitrary",), vmem_limit_bytes=VMEM_LIMIT),
        name="moe",
    )(block_e, n_used, next_e, xs, w_gate, w_up, w_down)


def _combine_kernel(dest_ref, dest_next_ref, *rest, tile_starts):
    n_parts = len(tile_starts)
    x2_refs, rt_refs = rest[:n_parts], rest[n_parts:2 * n_parts]
    yb_hbm = rest[2 * n_parts]
    o_refs = rest[2 * n_parts + 1:3 * n_parts + 1]
    buf, sem = rest[3 * n_parts + 1:]
    tm = x2_refs[0].shape[0]
    i = pl.program_id(0)
    slot = lax.rem(i, 2)

    def gather(d_ref, s):
        for k in range(2):
            for r in range(tm):
                pltpu.make_async_copy(yb_hbm.at[pl.ds(d_ref[0, 0, k * tm + r], 1), :],
                                      buf.at[s, k, pl.ds(r, 1), :],
                                      sem.at[s]).start(priority=r % 2)

    @pl.when(i == 0)
    def _():
        gather(dest_ref, 0)

    for s in range(2):
        @pl.when((i + 1 < pl.num_programs(0)) & (slot == 1 - s))
        def _():
            gather(dest_next_ref, s)

    for k in range(2):
        pltpu.make_async_copy(yb_hbm.at[pl.ds(0, tm), :], buf.at[slot, k], sem.at[slot]).wait()
    rt = _select_part(i, tile_starts, rt_refs)
    out = (_select_part(i, tile_starts, x2_refs) + rt[:, 2:3] * buf[slot, 0]
           + rt[:, 3:4] * buf[slot, 1])
    ends = tile_starts[1:] + (pl.num_programs(0),)
    for start, end, o_ref in zip(tile_starts, ends, o_refs):
        @pl.when((i >= start) & (i < end))
        def _():
            o_ref[...] = out


def _combine(dest_t, x2s, rts, yb, tm):
    counts, starts = _tile_layout(x2s, tm)
    nt = sum(counts)
    spec = lambda w: [_part_spec((tm, w), s, n) for s, n in zip(starts, counts)]
    return pl.pallas_call(
        functools.partial(_combine_kernel, tile_starts=tuple(starts)),
        grid=(nt,),
        in_specs=[
            pl.BlockSpec((1, 1, 2 * tm), lambda i: (i, 0, 0), memory_space=pltpu.SMEM),
            pl.BlockSpec((1, 1, 2 * tm), lambda i: (jnp.minimum(i + 1, nt - 1), 0, 0),
                         memory_space=pltpu.SMEM),
        ] + spec(D_MODEL) + spec(LANES) + [pl.BlockSpec(memory_space=pl.ANY)],
        out_specs=spec(D_MODEL),
        out_shape=[jax.ShapeDtypeStruct(x2.shape, F32) for x2 in x2s],
        scratch_shapes=[
            pltpu.VMEM((2, 2, tm, D_MODEL), F32),
            pltpu.SemaphoreType.DMA((2,)),
        ],
        compiler_params=pltpu.CompilerParams(
            dimension_semantics=("arbitrary",), vmem_limit_bytes=VMEM_LIMIT),
        name="combine",
    )(dest_t, dest_t, *x2s, *rts, yb)


def _hier_moe(parts, cnt, w_gate, w_up, w_down, tm, blk):
    counts = cnt[0, ROUTER_COL0:ROUTER_COL0 + N_EXPERTS].astype(jnp.int32)
    padded = (counts + blk - 1) // blk * blk
    pad_end = jnp.cumsum(padded)
    pad_start = pad_end - padded
    t_all = sum(p[0].shape[0] for p in parts)
    n_blocks = (2 * t_all + N_EXPERTS * (blk - 1)) // blk + 1
    rows = n_blocks * blk
    blk_row0 = jnp.arange(n_blocks, dtype=jnp.int32) * blk
    block_e = jnp.minimum(
        jnp.sum((pad_end[None, :] <= blk_row0[:, None]).astype(jnp.int32), axis=1),
        N_EXPERTS - 1)
    n_used = (pad_end[-1] // blk).astype(jnp.int32).reshape(1)
    ids = jnp.arange(N_EXPERTS, dtype=jnp.int32)
    later = (ids[None, :] > ids[:, None]) & (padded[None, :] > 0)
    next_e = jnp.where(jnp.any(later, axis=1),
                       jnp.min(jnp.where(later, ids[None, :], N_EXPERTS), axis=1), ids)
    experts = jnp.arange(N_EXPERTS, dtype=jnp.int32)[:, None, None]

    dests = []
    for x2, _, _, rtt in parts:
        T = x2.shape[0]
        flat = lambda a: jnp.swapaxes(a, 0, 1).reshape(a.shape[1], T)
        eid = flat(rtt[:, 0:2, :]).astype(jnp.int32)
        rank = flat(rtt[:, 4:6, :]).astype(jnp.int32)
        dest = rank + jnp.sum(
            jnp.where(eid[None] == experts, pad_start[:, None, None], 0), axis=0)
        nt = T // tm
        dests.append(dest.reshape(2, nt, tm).transpose(1, 0, 2).reshape(nt, 1, 2 * tm))
    dest_t = jnp.concatenate(dests, axis=0)
    xs = _dispatch(pad_end, padded, dest_t, [p[1] for p in parts], rows, tm, blk)
    yb = _moe(block_e, n_used, next_e.astype(jnp.int32), xs, w_gate, w_up, w_down, blk)
    return _combine(dest_t, [p[0] for p in parts], [p[2] for p in parts], yb, tm)


def _rope_table(pos):
    half = ROPE_DIM // 2
    d = jnp.arange(LANES, dtype=jnp.int32) % HEAD_DIM
    inv = ROPE_THETA ** (-(2 * (d % half)).astype(F32) / ROPE_DIM)
    ang = pos.astype(F32)[:, None] * inv[None, :]
    cos, sin = jnp.cos(ang), jnp.sin(ang)
    rotary = (d < ROPE_DIM)[None, :]
    first = (d < half)[None, :]
    return jnp.concatenate([jnp.where(rotary, cos, 1.0),
                            jnp.where(first, -sin, 0.0),
                            jnp.where(rotary & ~first, sin, 0.0)], axis=1)


def _mixers(x, pos_rope, kctx_prev, vctx_prev, h0r, h0i, mk, mv, wp, sp, cnt0, *,
            tm_in, tq, ssm_l, tm_mid):
    B, S, _ = x.shape
    T = B * S
    q, k3, v3, u_tm = _in_proj(x, wp["gmix"], wp["win"], wp["gq"], wp["gk"], pos_rope, *tm_in)
    if kctx_prev is None:
        kctx, vctx = k3, v3
    else:
        kctx = jnp.concatenate([kctx_prev, k3], axis=1)
        vctx = jnp.concatenate([vctx_prev, v3], axis=1)
    att = _swa(wp["sink"], q, kctx, vctx, tq, mask_context=kctx_prev is None)
    ssm_tm, hr, hi = _ssm(u_tm.reshape(S, B, SSM_WIDTH), h0r, h0i, sp, ssm_l)
    x2, hn, rt, rtt, cnt = _mid(x, att, ssm_tm.reshape(S, B * SSM_WIDTH), mk, mv, wp, cnt0,
                                *tm_mid)
    part = (x2.reshape(T, D_MODEL), hn.reshape(T, D_MODEL), rt.reshape(T, LANES), rtt)
    return part, cnt, k3, v3, hr, hi


def kernel(x_prompt, x_sample, cache_attn_k, cache_attn_v, state_ssm_re, state_ssm_im, cache_mem_k, cache_mem_v, mem_prompt, norm_mix, w_in, q_norm, k_norm, attn_sink, ssm_lambda_re, ssm_lambda_im, ssm_log_dt, ssm_b_re, ssm_b_im, ssm_c_re, ssm_c_im, ssm_d, ssm_w_glu, ssm_b_glu, norm_attn_out, norm_ssm_out, w_out, norm_cross, norm_mem, w_cq, w_ck, w_cv, cq_norm, ck_norm, w_co, norm_ffn, w_router_group, b_router_group, w_router_expert, b_router_expert, w_e_gate, w_e_up, w_e_down):
    depth = norm_mix.shape[0]
    Bp, Lp, _ = x_prompt.shape
    Bs, Ls, _ = x_sample.shape
    yp, ys = x_prompt, x_sample
    rope_p = _rope_table(jnp.arange(Lp, dtype=jnp.int32))
    rope_s = _rope_table(PAST_LEN + jnp.arange(Ls, dtype=jnp.int32))
    outs = [[] for _ in range(10)]
    n_router = N_EXPERT_GROUPS + N_EXPERTS
    for l in range(depth):
        row = lambda a: a[l].astype(F32).reshape(1, -1)
        w_r = jnp.pad(jnp.concatenate([w_router_group[l], w_router_expert[l]], axis=1).astype(F32),
                      ((0, 0), (0, LANES - n_router)))
        w_r_hi = w_r.astype(BF16)
        w_r_lo = (w_r - w_r_hi.astype(F32)).astype(BF16)
        b_r = jnp.pad(jnp.concatenate([b_router_group[l], b_router_expert[l]]).astype(F32),
                      (0, LANES - n_router)).reshape(1, LANES)
        wp = {
            "gmix": row(norm_mix), "win": w_in[l].astype(BF16),
            "gq": jnp.tile(row(q_norm), (1, LANES // HEAD_DIM)),
            "gk": jnp.tile(row(k_norm), (1, LANES // HEAD_DIM)),
            "sink": attn_sink[l].astype(F32),
            "gao": row(norm_attn_out), "gso": row(norm_ssm_out),
            "wout": w_out[l].astype(BF16), "gx": row(norm_cross),
            "wcq": w_cq[l].astype(BF16), "gcq": row(cq_norm),
            "wco": w_co[l].astype(BF16), "gffn": row(norm_ffn),
            "wr": jnp.concatenate([w_r_hi, w_r_lo], axis=1), "br": b_r,
        }
        sp = _ssm_params(ssm_lambda_re[l], ssm_lambda_im[l], ssm_log_dt[l], ssm_b_re[l],
                         ssm_b_im[l], ssm_c_re[l], ssm_c_im[l], ssm_d[l], ssm_w_glu[l],
                         ssm_b_glu[l])
        ew = (w_e_gate[l].astype(F32), w_e_up[l].astype(F32), w_e_down[l].astype(F32))

        w_ckv = jnp.concatenate([w_ck[l], w_cv[l]], axis=1).astype(BF16)
        mkp, mvp = _memkv(mem_prompt.reshape(Bp * N_MEM, D_MODEL), row(norm_mem), w_ckv,
                          row(ck_norm), 512)
        mkp = mkp.reshape(Bp, N_MEM, CA_WIDTH)
        mvp = mvp.reshape(Bp, N_MEM, CA_WIDTH)

        zst = jnp.zeros((Bp, SSM_COLS), F32)
        part_p, cnt_p, kp, vp, hpr, hpi = _mixers(
            yp, rope_p, None, None, zst, zst, mkp, mvp, wp, sp, jnp.zeros((1, LANES), F32),
            tm_in=(1, 512), tq=256, ssm_l=64, tm_mid=(1, 512))
        part_s, cnt_s, kn, vn, hsr, hsi = _mixers(
            ys, rope_s, cache_attn_k[l].reshape(Bs, WINDOW, KV_WIDTH).astype(F32),
            cache_attn_v[l].reshape(Bs, WINDOW, KV_WIDTH).astype(F32),
            state_ssm_re[l].astype(F32).reshape(Bs, SSM_COLS),
            state_ssm_im[l].astype(F32).reshape(Bs, SSM_COLS),
            cache_mem_k[l].astype(F32).reshape(Bs, N_MEM, CA_WIDTH),
            cache_mem_v[l].astype(F32).reshape(Bs, N_MEM, CA_WIDTH), wp, sp, cnt_p,
            tm_in=(8, Ls), tq=CHUNK, ssm_l=Ls, tm_mid=(8, Ls))
        yp, ys = _hier_moe([part_p, part_s], cnt_s, *ew, 256, 2 * MOE_BLOCK)
        yp = yp.reshape(Bp, Lp, D_MODEL)
        ys = ys.reshape(Bs, Ls, D_MODEL)

        sg = (N_SSM_GROUPS, SSM_STATE)
        kvs = (N_KV_HEADS, HEAD_DIM)
        vals = (kp[:, Lp - WINDOW:].reshape(Bp, WINDOW, *kvs),
                vp[:, Lp - WINDOW:].reshape(Bp, WINDOW, *kvs),
                hpr.reshape(Bp, *sg), hpi.reshape(Bp, *sg),
                mkp.reshape(Bp, N_MEM, CA_HEADS, CA_HEAD_DIM),
                mvp.reshape(Bp, N_MEM, CA_HEADS, CA_HEAD_DIM),
                kn.reshape(Bs, Ls, *kvs), vn.reshape(Bs, Ls, *kvs),
                hsr.reshape(Bs, *sg), hsi.reshape(Bs, *sg))
        for lst, val in zip(outs, vals):
            lst.append(val)
    return (yp, ys) + tuple(jnp.stack(lst) for lst in outs)
```

```python
import functools
import math

import jax
import jax.numpy as jnp
from jax import lax
from jax.experimental import pallas as pl
from jax.experimental.pallas import tpu as pltpu

F32 = jnp.float32
BF16 = jnp.bfloat16

D_MODEL = 1024
CHUNK = 64
N_Q_HEADS = 8
N_KV_HEADS = 2
GQA = N_Q_HEADS // N_KV_HEADS
HEAD_DIM = 64
WINDOW = 128
BAND = WINDOW + CHUNK
ROPE_DIM = HEAD_DIM // 4
ROPE_THETA = 500000.0
ATT_WIDTH = N_Q_HEADS * HEAD_DIM
KV_WIDTH = N_KV_HEADS * HEAD_DIM
SSM_GROUP = 16
SSM_WIDTH = D_MODEL // 2
N_SSM_GROUPS = SSM_WIDTH // SSM_GROUP
SSM_STATE = 64
SSM_COLS = N_SSM_GROUPS * SSM_STATE
IN_WIDTH = ATT_WIDTH + 2 * KV_WIDTH + SSM_WIDTH
N_MEM = 256
CA_HEADS = 4
CA_HEAD_DIM = 128
CA_WIDTH = CA_HEADS * CA_HEAD_DIM
N_EXPERT_GROUPS = 4
EXPERTS_PER_GROUP = 8
N_EXPERTS = N_EXPERT_GROUPS * EXPERTS_PER_GROUP
D_EXPERT = 512
MOE_BLOCK = 256
EPS = 1e-6
NEG = -1e30
PAST_LEN = 4096

LANES = 128
ROUTER_COL0 = N_EXPERT_GROUPS
VMEM_LIMIT = 48 * 1024 * 1024


def _rms(x, g):
    ms = jnp.mean(x * x, axis=-1, keepdims=True)
    return (x * lax.rsqrt(ms + EPS)) * g


def _mm(a, b):
    return jnp.dot(a, b, preferred_element_type=F32)


_HI_HALF = 0xFFFF0000


def _pack_bf16_pairs(x):
    half = x.shape[1] // 2
    bits = lambda v: lax.bitcast_convert_type(v.astype(BF16).astype(F32), jnp.uint32)
    return (lax.shift_right_logical(bits(x[:, :half]), jnp.uint32(16))
            | (bits(x[:, half:]) & jnp.uint32(_HI_HALF)))


def _unpack_bf16_pairs(w):
    lo = lax.bitcast_convert_type(lax.shift_left(w, jnp.uint32(16)), F32)
    hi = lax.bitcast_convert_type(w & jnp.uint32(_HI_HALF), F32)
    return jnp.concatenate([lo.astype(BF16), hi.astype(BF16)], axis=1)


def _in_proj_kernel(x_ref, g_ref, w_ref, gq_ref, gk_ref, rope_ref,
                    q_ref, k_ref, v_ref, u_ref):
    nb, ts, _ = x_ref.shape
    tm = nb * ts
    h = _rms(x_ref[...].reshape(tm, D_MODEL), g_ref[...])
    hin = _mm(h.astype(BF16), w_ref[...])
    rope = jnp.concatenate([rope_ref[...]] * nb, axis=0)
    cos = rope[:, 0:LANES]
    sin_lo = rope[:, LANES:2 * LANES]
    sin_hi = rope[:, 2 * LANES:3 * LANES]
    lane = lax.broadcasted_iota(jnp.int32, (tm, LANES), 1)
    left = lane < HEAD_DIM

    def norm_rope(z, g):
        sq = z * z
        lsum = jnp.sum(jnp.where(left, sq, 0.0), axis=-1, keepdims=True)
        rsum = jnp.sum(jnp.where(left, 0.0, sq), axis=-1, keepdims=True)
        ms = jnp.where(left, lsum, rsum) * (1.0 / HEAD_DIM)
        zn = (z * lax.rsqrt(ms + EPS)) * g
        half = ROPE_DIM // 2
        return (zn * cos + pltpu.roll(zn, LANES - half, 1) * sin_lo
                + pltpu.roll(zn, half, 1) * sin_hi)

    for j in range(ATT_WIDTH // LANES):
        sl = slice(j * LANES, (j + 1) * LANES)
        q_ref[:, :, sl] = norm_rope(hin[:, sl], gq_ref[...]).reshape(nb, ts, LANES)
    k_ref[...] = norm_rope(hin[:, ATT_WIDTH:ATT_WIDTH + KV_WIDTH],
                           gk_ref[...]).reshape(nb, ts, KV_WIDTH)
    v_ref[...] = hin[:, ATT_WIDTH + KV_WIDTH:ATT_WIDTH + 2 * KV_WIDTH].reshape(nb, ts, KV_WIDTH)
    for b in range(nb):
        u_ref[:, b * SSM_WIDTH:(b + 1) * SSM_WIDTH] = (
            hin[b * ts:(b + 1) * ts, ATT_WIDTH + 2 * KV_WIDTH:])


def _in_proj(x, g, w_bf, gq, gk, rope, nb, ts):
    B, S, _ = x.shape
    full = lambda b, i: (0, 0)
    tile = lambda w: pl.BlockSpec((nb, ts, w), lambda b, i: (b, i, 0))
    return pl.pallas_call(
        _in_proj_kernel,
        grid=(B // nb, S // ts),
        in_specs=[
            tile(D_MODEL),
            pl.BlockSpec((1, D_MODEL), full),
            pl.BlockSpec((D_MODEL, IN_WIDTH), full),
            pl.BlockSpec((1, LANES), full),
            pl.BlockSpec((1, LANES), full),
            pl.BlockSpec((ts, 3 * LANES), lambda b, i: (i, 0)),
        ],
        out_specs=[
            tile(ATT_WIDTH), tile(KV_WIDTH), tile(KV_WIDTH),
            pl.BlockSpec((ts, nb * SSM_WIDTH), lambda b, i: (i, b)),
        ],
        out_shape=[
            jax.ShapeDtypeStruct((B, S, ATT_WIDTH), F32),
            jax.ShapeDtypeStruct((B, S, KV_WIDTH), F32),
            jax.ShapeDtypeStruct((B, S, KV_WIDTH), F32),
            jax.ShapeDtypeStruct((S, B * SSM_WIDTH), F32),
        ],
        compiler_params=pltpu.CompilerParams(
            dimension_semantics=("arbitrary", "arbitrary"),
            vmem_limit_bytes=VMEM_LIMIT),
        name="in_proj",
    )(x, g, w_bf, gq, gk, rope)


def _swa_kernel(sink_ref, q_ref, k_ref, v_ref, o_ref, *, mask_context):
    tq = q_ref.shape[1]
    i = pl.program_id(1)
    nch = tq // CHUNK
    lane = lax.broadcasted_iota(jnp.int32, (BAND, LANES), 1)
    lo_half = lane < HEAD_DIM
    vrow_lo = lax.broadcasted_iota(jnp.int32, (LANES, BAND), 0) < HEAD_DIM
    q_lo = lax.broadcasted_iota(jnp.int32, (1, LANES), 1) < CHUNK
    slabs_per_kv = GQA * HEAD_DIM // LANES

    units = []
    scores = []
    vpads = {}
    for c in range(nch):
        chunk = i * nch + c
        if mask_context:
            first = jnp.maximum(chunk - WINDOW // CHUNK, 0)
            start = pl.multiple_of(first * CHUNK, CHUNK)
            kidx = start + lax.broadcasted_iota(jnp.int32, (BAND, LANES), 0)
            valid = kidx < (chunk + 1) * CHUNK
        else:
            start = pl.multiple_of(chunk * CHUNK, CHUNK)
        kb = k_ref[0, pl.ds(start, BAND), :]
        kb_sw = pltpu.roll(kb, HEAD_DIM, 1)
        vt = v_ref[0, pl.ds(start, BAND), :].T
        vt_sw = jnp.concatenate([vt[HEAD_DIM:], vt[:HEAD_DIM]], axis=0)
        for kvh in range(N_KV_HEADS):
            k_own, k_oth = (kb, kb_sw) if kvh == 0 else (kb_sw, kb)
            v_own, v_oth = (vt, vt_sw) if kvh == 0 else (vt_sw, vt)
            kpad = (jnp.where(lo_half, k_own, 0.0).astype(BF16),
                    jnp.where(lo_half, 0.0, k_oth).astype(BF16))
            vpads[(c, kvh)] = (jnp.where(vrow_lo, v_own, 0.0).astype(BF16),
                               jnp.where(vrow_lo, 0.0, v_oth).astype(BF16))
            col0 = kvh * GQA * HEAD_DIM
            q2 = jnp.concatenate(
                [q_ref[0, c * CHUNK:(c + 1) * CHUNK, col0 + m * LANES:col0 + (m + 1) * LANES]
                 for m in range(slabs_per_kv)], axis=0).astype(BF16)
            for side in range(2):
                s = lax.dot_general(kpad[side], q2, (((1,), (1,)), ((), ())),
                                    preferred_element_type=F32) * (HEAD_DIM ** -0.5)
                if mask_context:
                    s = jnp.where(valid, s, NEG)
                units.append((c, kvh, side))
                scores.append(s)

    sinks = [jnp.where(q_lo, sink_ref[kvh * GQA + side], sink_ref[kvh * GQA + 2 + side])
             for (_, kvh, side) in units]
    maxes = [jnp.maximum(jnp.max(s, axis=0, keepdims=True), sk)
             for s, sk in zip(scores, sinks)]
    exps = [jnp.exp(s - mx) for s, mx in zip(scores, maxes)]
    dens = [jnp.sum(p, axis=0, keepdims=True) + jnp.exp(sk - mx)
            for p, sk, mx in zip(exps, sinks, maxes)]
    probs = [(p * (1.0 / den)).astype(BF16) for p, den in zip(exps, dens)]

    for n in range(0, len(units), 2):
        c, kvh, _ = units[n]
        vp = vpads[(c, kvh)]
        o = (_mm(vp[0], probs[n]) + _mm(vp[1], probs[n + 1])).T
        col0 = kvh * GQA * HEAD_DIM
        for m in range(slabs_per_kv):
            o_ref[0, c * CHUNK:(c + 1) * CHUNK, col0 + m * LANES:col0 + (m + 1) * LANES] = (
                o[m * CHUNK:(m + 1) * CHUNK])


def _swa(sink, q, kctx, vctx, tq, mask_context):
    B, Sq, _ = q.shape
    Sk = kctx.shape[1]
    return pl.pallas_call(
        functools.partial(_swa_kernel, mask_context=mask_context),
        grid=(B, Sq // tq),
        in_specs=[
            pl.BlockSpec(memory_space=pltpu.SMEM),
            pl.BlockSpec((1, tq, ATT_WIDTH), lambda b, i: (b, i, 0)),
            pl.BlockSpec((1, Sk, KV_WIDTH), lambda b, i: (b, 0, 0)),
            pl.BlockSpec((1, Sk, KV_WIDTH), lambda b, i: (b, 0, 0)),
        ],
        out_specs=pl.BlockSpec((1, tq, ATT_WIDTH), lambda b, i: (b, i, 0)),
        out_shape=jax.ShapeDtypeStruct((B, Sq, ATT_WIDTH), F32),
        compiler_params=pltpu.CompilerParams(
            dimension_semantics=("arbitrary", "arbitrary"),
            vmem_limit_bytes=VMEM_LIMIT),
        name="swa",
    )(sink, q, kctx, vctx)


def _ssm_kernel(u_ref, h0r_ref, h0i_ref, lam_ref, bre_ref, bim_ref, cre_ref, cim_ref,
                d_ref, wglu_ref, bglu_ref,
                y_ref, hr_out, hi_out, sr0, si0, sr1, si1, hr_s, hi_s):
    L, B, _ = u_ref.shape
    rows = L * B
    half_w = SSM_WIDTH // 2
    half_c = SSM_COLS // 2
    halves = ((sr0, si0), (sr1, si1))

    @pl.when(pl.program_id(0) == 0)
    def _():
        hr_s[...] = h0r_ref[...]
        hi_s[...] = h0i_ref[...]

    u = u_ref[...].reshape(rows, SSM_WIDTH)
    ub = u.astype(BF16)

    def project_in(hf):
        sr, si = halves[hf]
        uh = ub[:, hf * half_w:(hf + 1) * half_w]
        sr[...] = _mm(uh, bre_ref[hf])
        si[...] = _mm(uh, bim_ref[hf])

    def recur(hf):
        sr, si = halves[hf]
        cw = 4 * LANES
        for cc in range(half_c // cw):
            cols = slice(cc * cw, (cc + 1) * cw)
            gcols = slice(hf * half_c + cc * cw, hf * half_c + (cc + 1) * cw)
            lr = jnp.broadcast_to(lam_ref[0:1, gcols], (B, cw))
            li = jnp.broadcast_to(lam_ref[1:2, gcols], (B, cw))
            hr, hi = hr_s[:, gcols], hi_s[:, gcols]
            for t in range(L):
                at_t = slice(t * B, (t + 1) * B)
                hr, hi = (lr * hr - li * hi + sr[at_t, cols],
                          lr * hi + li * hr + si[at_t, cols])
                sr[at_t, cols] = hr
                si[at_t, cols] = hi
            hr_s[:, gcols] = hr
            hi_s[:, gcols] = hi

    def project_out(hf):
        sr, si = halves[hf]
        return (_mm(sr[...].astype(BF16), cre_ref[hf]) + _mm(si[...].astype(BF16), cim_ref[hf]))

    project_in(0)
    project_in(1)
    recur(0)
    y0 = project_out(0)
    recur(1)
    y1 = project_out(1)
    y = jnp.concatenate([y0, y1], axis=1) + d_ref[...] * u
    g = 0.5 * y * (1.0 + jnp.tanh(math.sqrt(2.0 / math.pi) * (y + 0.044715 * (y * y * y))))
    gb = g.astype(BF16)
    z = jnp.concatenate(
        [_mm(gb[:, hf * half_w:(hf + 1) * half_w], wglu_ref[hf]) for hf in range(2)],
        axis=1) + bglu_ref[...]
    out = g * (1.0 / (1.0 + jnp.exp(-z)))
    y_ref[...] = out.reshape(L, B, SSM_WIDTH)
    hr_out[...] = hr_s[...]
    hi_out[...] = hi_s[...]


def _ssm(u, h0r, h0i, sp, L):
    S, B, _ = u.shape
    c2 = lambda i: (0, 0)
    c3 = lambda i: (0, 0, 0)
    return pl.pallas_call(
        _ssm_kernel,
        grid=(S // L,),
        in_specs=[
            pl.BlockSpec((L, B, SSM_WIDTH), lambda i: (i, 0, 0)),
            pl.BlockSpec((B, SSM_COLS), c2),
            pl.BlockSpec((B, SSM_COLS), c2),
            pl.BlockSpec((2, SSM_COLS), c2),
            pl.BlockSpec((2, SSM_WIDTH // 2, SSM_COLS // 2), c3),
            pl.BlockSpec((2, SSM_WIDTH // 2, SSM_COLS // 2), c3),
            pl.BlockSpec((2, SSM_COLS // 2, SSM_WIDTH // 2), c3),
            pl.BlockSpec((2, SSM_COLS // 2, SSM_WIDTH // 2), c3),
            pl.BlockSpec((1, SSM_WIDTH), c2),
            pl.BlockSpec((2, SSM_WIDTH // 2, SSM_WIDTH // 2), c3),
            pl.BlockSpec((1, SSM_WIDTH), c2),
        ],
        out_specs=[
            pl.BlockSpec((L, B, SSM_WIDTH), lambda i: (i, 0, 0)),
            pl.BlockSpec((B, SSM_COLS), c2),
            pl.BlockSpec((B, SSM_COLS), c2),
        ],
        out_shape=[
            jax.ShapeDtypeStruct((S, B, SSM_WIDTH), F32),
            jax.ShapeDtypeStruct((B, SSM_COLS), F32),
            jax.ShapeDtypeStruct((B, SSM_COLS), F32),
        ],
        scratch_shapes=[
            pltpu.VMEM((L * B, SSM_COLS // 2), F32),
            pltpu.VMEM((L * B, SSM_COLS // 2), F32),
            pltpu.VMEM((L * B, SSM_COLS // 2), F32),
            pltpu.VMEM((L * B, SSM_COLS // 2), F32),
            pltpu.VMEM((B, SSM_COLS), F32),
            pltpu.VMEM((B, SSM_COLS), F32),
        ],
        compiler_params=pltpu.CompilerParams(
            dimension_semantics=("arbitrary",), vmem_limit_bytes=VMEM_LIMIT),
        name="ssm",
    )(u, h0r, h0i, sp["lam"], sp["bre"], sp["bim"], sp["cre"], sp["cim"],
      sp["d"], sp["wglu"], sp["bglu"])


def _block_diag(blocks):
    G, r, c = blocks.shape
    col = jnp.arange(G * c, dtype=jnp.int32)
    spread = (col[None, :] % c == jnp.arange(c, dtype=jnp.int32)[:, None]).astype(F32)
    same_group = (jnp.arange(G * r, dtype=jnp.int32)[:, None] // r) == (col[None, :] // c)
    tiled = jnp.dot(blocks.reshape(G * r, c), spread, precision=lax.Precision.HIGHEST)
    return jnp.where(same_group, tiled, 0.0)


def _ssm_params(lam_re, lam_im, log_dt, b_re, b_im, c_re, c_im, d, w_glu, b_glu):
    lam = lax.complex(lam_re.astype(F32), lam_im.astype(F32))
    dt = jnp.exp(log_dt.astype(F32))[:, None]
    lam_bar = jnp.exp(lam * dt)
    bmat = lax.complex(b_re.astype(F32), b_im.astype(F32))
    b_bar = ((lam_bar - 1.0) / lam)[..., None] * bmat
    lam2 = jnp.stack([lam_bar.real.reshape(-1), lam_bar.imag.reshape(-1)])
    hw, hc = SSM_WIDTH // 2, SSM_COLS // 2
    split_b = lambda m: jnp.stack([m[:hw, :hc], m[hw:, hc:]]).astype(BF16)
    split_c = lambda m: jnp.stack([m[:hc, :hw], m[hc:, hw:]]).astype(BF16)
    wg = _block_diag(w_glu.astype(F32))
    return {
        "lam": lam2,
        "bre": split_b(_block_diag(b_bar.real).T),
        "bim": split_b(_block_diag(b_bar.imag).T),
        "cre": split_c(_block_diag(c_re.astype(F32)).T),
        "cim": split_c(_block_diag(-c_im.astype(F32)).T),
        "d": d.astype(F32).reshape(1, SSM_WIDTH),
        "wglu": jnp.stack([wg[:hw, :hw], wg[hw:, hw:]]).astype(BF16),
        "bglu": b_glu.astype(F32).reshape(1, SSM_WIDTH),
    }


def _memkv_kernel(m_ref, g_ref, w_ref, gk_ref, k_ref, v_ref):
    m = _rms(m_ref[...], g_ref[...])
    kv = _mm(m.astype(BF16), w_ref[...])
    for h in range(CA_HEADS):
        sl = slice(h * CA_HEAD_DIM, (h + 1) * CA_HEAD_DIM)
        k_ref[:, sl] = _rms(kv[:, sl], gk_ref[...])
    v_ref[...] = kv[:, CA_WIDTH:]


def _memkv(mem2d, g, w_bf, gk, tm):
    T = mem2d.shape[0]
    full = lambda i: (0, 0)
    return pl.pallas_call(
        _memkv_kernel,
        grid=(T // tm,),
        in_specs=[
            pl.BlockSpec((tm, D_MODEL), lambda i: (i, 0)),
            pl.BlockSpec((1, D_MODEL), full),
            pl.BlockSpec((D_MODEL, 2 * CA_WIDTH), full),
            pl.BlockSpec((1, CA_HEAD_DIM), full),
        ],
        out_specs=[
            pl.BlockSpec((tm, CA_WIDTH), lambda i: (i, 0)),
            pl.BlockSpec((tm, CA_WIDTH), lambda i: (i, 0)),
        ],
        out_shape=[
            jax.ShapeDtypeStruct((T, CA_WIDTH), F32),
            jax.ShapeDtypeStruct((T, CA_WIDTH), F32),
        ],
        compiler_params=pltpu.CompilerParams(
            dimension_semantics=("arbitrary",), vmem_limit_bytes=VMEM_LIMIT),
        name="memkv",
    )(mem2d, g, w_bf, gk)


def _mid_kernel(x_ref, att_ref, ssm_ref, mk_ref, mv_ref,
                gao_ref, gso_ref, wout_ref, gx_ref, wcq_ref, gcq_ref, wco_ref,
                gffn_ref, wr_ref, br_ref, cnt0_ref, tri_ref,
                x2_ref, hn_ref, rt_ref, rtt_ref, cnt_ref, base_s):
    nb, ts, _ = x_ref.shape
    tm = nb * ts

    @pl.when((pl.program_id(0) == 0) & (pl.program_id(1) == 0))
    def _():
        base_s[...] = cnt0_ref[...]

    ssm = jnp.concatenate(
        [ssm_ref[:, b * SSM_WIDTH:(b + 1) * SSM_WIDTH] for b in range(nb)], axis=0)
    a = _rms(att_ref[...].reshape(tm, ATT_WIDTH), gao_ref[...]).astype(BF16)
    s = _rms(ssm, gso_ref[...]).astype(BF16)
    x1 = (x_ref[...].reshape(tm, D_MODEL) + _mm(a, wout_ref[0:ATT_WIDTH, :])
          + _mm(s, wout_ref[ATT_WIDTH:, :]))

    qx = _mm(_rms(x1, gx_ref[...]).astype(BF16), wcq_ref[...])
    heads = []
    for h in range(CA_HEADS):
        sl = slice(h * CA_HEAD_DIM, (h + 1) * CA_HEAD_DIM)
        qh = _rms(qx[:, sl], gcq_ref[...]).astype(BF16)
        per_batch = []
        for b in range(nb):
            kh = mk_ref[b, :, sl].astype(BF16)
            qb = qh[b * ts:(b + 1) * ts]
            if ts <= LANES:
                vt = mv_ref[b, :, sl].T.astype(BF16)
                sc = lax.dot_general(kh, qb, (((1,), (1,)), ((), ())),
                                     preferred_element_type=F32) * (CA_HEAD_DIM ** -0.5)
                p = jnp.exp(sc - jnp.max(sc, axis=0, keepdims=True))
                p = p * (1.0 / jnp.sum(p, axis=0, keepdims=True))
                per_batch.append(_mm(vt, p.astype(BF16)).T)
            else:
                vh = mv_ref[b, :, sl].astype(BF16)
                sc = lax.dot_general(qb, kh, (((1,), (1,)), ((), ())),
                                     preferred_element_type=F32) * (CA_HEAD_DIM ** -0.5)
                p = jnp.exp(sc - jnp.max(sc, axis=-1, keepdims=True))
                p = p / jnp.sum(p, axis=-1, keepdims=True)
                per_batch.append(_mm(p.astype(BF16), vh))
        heads.append(jnp.concatenate(per_batch, axis=0))
    o = jnp.concatenate(heads, axis=1).astype(BF16)
    x2 = x1 + _mm(o, wco_ref[...])
    x2_ref[...] = x2.reshape(nb, ts, D_MODEL)

    hn = _rms(x2, gffn_ref[...])
    hn_ref[...] = hn.reshape(nb, ts, D_MODEL)

    h_hi = hn.astype(BF16)
    h_lo = (hn - h_hi.astype(F32)).astype(BF16)
    r1 = _mm(h_hi, wr_ref[...])
    lg = (r1[:, :LANES] + r1[:, LANES:] + _mm(h_lo, wr_ref[:, 0:LANES])
          + br_ref[...])

    col = lax.broadcasted_iota(jnp.int32, (tm, LANES), 1)
    big = jnp.int32(4 * LANES)
    gmask = col < N_EXPERT_GROUPS
    lgg = jnp.where(gmask, lg, NEG)
    mg = jnp.max(lgg, axis=-1, keepdims=True)
    grp = jnp.min(jnp.where(gmask & (lgg == mg), col, big), axis=-1, keepdims=True)
    pg_top = 1.0 / jnp.sum(jnp.where(gmask, jnp.exp(lgg - mg), 0.0), axis=-1, keepdims=True)

    ecol = col - ROUTER_COL0
    emask = ((ecol >= 0) & (ecol < N_EXPERTS)
             & (lax.shift_right_arithmetic(ecol, 3) == grp))
    le = jnp.where(emask, lg, NEG)
    m1 = jnp.max(le, axis=-1, keepdims=True)
    i1 = jnp.min(jnp.where(emask & (le == m1), col, big), axis=-1, keepdims=True)
    rest = emask & (col != i1)
    le2 = jnp.where(rest, lg, NEG)
    m2 = jnp.max(le2, axis=-1, keepdims=True)
    i2 = jnp.min(jnp.where(rest & (le2 == m2), col, big), axis=-1, keepdims=True)
    den = jnp.sum(jnp.where(emask, jnp.exp(le - m1), 0.0), axis=-1, keepdims=True)
    p1 = 1.0 / den
    p2 = jnp.exp(m2 - m1) / den
    gate1 = pg_top * p1 / (p1 + p2)
    gate2 = pg_top * p2 / (p1 + p2)

    sel1 = col == i1
    sel2 = col == i2
    oh = jnp.where(sel1 | sel2, 1.0, 0.0)
    tot = base_s[...] + _mm(tri_ref[...], oh.astype(BF16))
    rank1 = jnp.sum(jnp.where(sel1, tot, 0.0), axis=-1, keepdims=True)
    rank2 = jnp.sum(jnp.where(sel2, tot, 0.0), axis=-1, keepdims=True)
    base_s[...] = base_s[...] + jnp.sum(oh, axis=0, keepdims=True)
    cnt_ref[...] = base_s[...]

    e1 = (i1 - ROUTER_COL0).astype(F32)
    e2 = (i2 - ROUTER_COL0).astype(F32)
    rt = jnp.zeros((tm, LANES), F32)
    for k, val in enumerate((e1, e2, gate1, gate2, rank1, rank2)):
        rt = jnp.where(col == k, val, rt)
    rt_ref[...] = rt.reshape(nb, ts, LANES)
    for b in range(nb):
        rtt_ref[b] = rt[b * ts:(b + 1) * ts].T[0:8, :]


def _mid(x, att, ssm_tm, mk, mv, wp, cnt0, nb, ts):
    B, S, _ = x.shape
    c2 = lambda b, i: (0, 0)
    tile = lambda w: pl.BlockSpec((nb, ts, w), lambda b, i: (b, i, 0))
    return pl.pallas_call(
        _mid_kernel,
        grid=(B // nb, S // ts),
        in_specs=[
            tile(D_MODEL), tile(ATT_WIDTH),
            pl.BlockSpec((ts, nb * SSM_WIDTH), lambda b, i: (i, b)),
            pl.BlockSpec((nb, N_MEM, CA_WIDTH), lambda b, i: (b, 0, 0)),
            pl.BlockSpec((nb, N_MEM, CA_WIDTH), lambda b, i: (b, 0, 0)),
            pl.BlockSpec((1, ATT_WIDTH), c2),
            pl.BlockSpec((1, SSM_WIDTH), c2),
            pl.BlockSpec((ATT_WIDTH + SSM_WIDTH, D_MODEL), c2),
            pl.BlockSpec((1, D_MODEL), c2),
            pl.BlockSpec((D_MODEL, CA_WIDTH), c2),
            pl.BlockSpec((1, CA_HEAD_DIM), c2),
            pl.BlockSpec((CA_WIDTH, D_MODEL), c2),
            pl.BlockSpec((1, D_MODEL), c2),
            pl.BlockSpec((D_MODEL, 2 * LANES), c2),
            pl.BlockSpec((1, LANES), c2),
            pl.BlockSpec((1, LANES), c2),
            pl.BlockSpec((nb * ts, nb * ts), c2),
        ],
        out_specs=[
            tile(D_MODEL), tile(D_MODEL), tile(LANES),
            pl.BlockSpec((nb, 8, ts), lambda b, i: (b, 0, i)),
            pl.BlockSpec((1, LANES), c2),
        ],
        out_shape=[
            jax.ShapeDtypeStruct((B, S, D_MODEL), F32),
            jax.ShapeDtypeStruct((B, S, D_MODEL), F32),
            jax.ShapeDtypeStruct((B, S, LANES), F32),
            jax.ShapeDtypeStruct((B, 8, S), F32),
            jax.ShapeDtypeStruct((1, LANES), F32),
        ],
        scratch_shapes=[pltpu.VMEM((1, LANES), F32)],
        compiler_params=pltpu.CompilerParams(
            dimension_semantics=("arbitrary", "arbitrary"),
            vmem_limit_bytes=VMEM_LIMIT),
        name="mid",
    )(x, att, ssm_tm, mk, mv, wp["gao"], wp["gso"], wp["wout"], wp["gx"], wp["wcq"],
      wp["gcq"], wp["wco"], wp["gffn"], wp["wr"], wp["br"], cnt0,
      jnp.tri(nb * ts, k=-1, dtype=BF16))


def _select_part(i, tile_starts, refs):
    x = refs[0][...]
    for start, ref in zip(tile_starts[1:], refs[1:]):
        x = jnp.where(i >= start, ref[...], x)
    return x


def _part_spec(shape, tile_start, n_tiles):
    def index(i, *_):
        return (jnp.clip(i - tile_start, 0, n_tiles - 1),) + (0,) * (len(shape) - 1)
    return pl.BlockSpec(shape, index)


def _dispatch_kernel(pend_ref, padded_ref, dest_ref, *rest, tile_starts):
    n_parts = len(tile_starts)
    hn_refs = rest[:n_parts]
    xs_hbm, stage, zbuf, sem = rest[n_parts:]
    tm = hn_refs[0].shape[0]
    i = pl.program_id(0)
    slot = lax.rem(i, 2)
    blk = zbuf.shape[0]

    def wait_rows(s):
        for _ in range(2):
            pltpu.make_async_copy(stage.at[s], xs_hbm.at[pl.ds(0, tm)], sem.at[s]).wait()

    @pl.when(i == 0)
    def _():
        zbuf[...] = jnp.zeros_like(zbuf)
        for e in range(N_EXPERTS):
            @pl.when(padded_ref[e] > 0)
            def _():
                row0 = pl.multiple_of(pend_ref[e] - blk, blk)
                fill = pltpu.make_async_copy(zbuf, xs_hbm.at[pl.ds(row0, blk)], sem.at[2])
                fill.start()
                fill.wait()

        def fill_tail(b, carry):
            fill = pltpu.make_async_copy(
                zbuf, xs_hbm.at[pl.ds(pl.multiple_of(b * blk, blk), blk)], sem.at[2])
            fill.start()
            fill.wait()
            return carry

        lax.fori_loop(pend_ref[N_EXPERTS - 1] // blk, xs_hbm.shape[0] // blk, fill_tail, 0)

    @pl.when(i >= 2)
    def _():
        wait_rows(slot)

    tile = _pack_bf16_pairs(_select_part(i, tile_starts, hn_refs))
    for s in range(2):
        @pl.when(slot == s)
        def _():
            stage[s] = tile
            for k in range(2):
                for r in range(tm):
                    pltpu.make_async_copy(stage.at[s, pl.ds(r, 1), :],
                                          xs_hbm.at[pl.ds(dest_ref[0, 0, k * tm + r], 1), :],
                                          sem.at[s]).start(priority=r % 2)

    @pl.when(i == pl.num_programs(0) - 1)
    def _():
        wait_rows(slot)

        @pl.when(i >= 1)
        def _():
            wait_rows(1 - slot)


def _tile_layout(arrays, tm):
    counts = [a.shape[0] // tm for a in arrays]
    starts = [sum(counts[:p]) for p in range(len(counts))]
    return counts, starts


def _dispatch(pad_end, padded, dest_t, hns, rows, tm, blk):
    counts, starts = _tile_layout(hns, tm)
    grid_spec = pltpu.PrefetchScalarGridSpec(
        num_scalar_prefetch=2,
        grid=(sum(counts),),
        in_specs=[pl.BlockSpec((1, 1, 2 * tm), lambda i, pe, pd: (i, 0, 0),
                               memory_space=pltpu.SMEM)]
        + [_part_spec((tm, D_MODEL), s, n) for s, n in zip(starts, counts)],
        out_specs=pl.BlockSpec(memory_space=pl.ANY),
        scratch_shapes=[
            pltpu.VMEM((2, tm, D_MODEL // 2), jnp.uint32),
            pltpu.VMEM((blk, D_MODEL // 2), jnp.uint32),
            pltpu.SemaphoreType.DMA((3,)),
        ],
    )
    return pl.pallas_call(
        functools.partial(_dispatch_kernel, tile_starts=tuple(starts)),
        grid_spec=grid_spec,
        out_shape=jax.ShapeDtypeStruct((rows, D_MODEL // 2), jnp.uint32),
        compiler_params=pltpu.CompilerParams(
            dimension_semantics=("arbitrary",), vmem_limit_bytes=VMEM_LIMIT),
        name="dispatch",
    )(pad_end, padded, dest_t, *hns)


def _moe_kernel(be_ref, nu_ref, nxt_ref, xs_ref, wg_hbm, wu_hbm, wd_hbm, yb_ref,
                wg_f, wu_f, wd_f, wg_s, wu_s, wd_s, run_s, sem):
    i = pl.program_id(0)

    def fetch(e, slot):
        return [pltpu.make_async_copy(src.at[e], dst.at[slot], sem.at[slot])
                for src, dst in ((wg_hbm, wg_f), (wu_hbm, wu_f), (wd_hbm, wd_f))]

    @pl.when(i < nu_ref[0])
    def _():
        e = be_ref[i]

        @pl.when(i == 0)
        def _():
            run_s[0] = 0
            for c in fetch(e, 0):
                c.start()

        @pl.when((i == 0) | (e != be_ref[jnp.maximum(i - 1, 0)]))
        def _():
            slot = lax.rem(run_s[0], 2)
            run_s[0] = run_s[0] + 1
            for c in fetch(e, slot):
                c.wait()
            wg_s[...] = wg_f[slot].astype(BF16)
            wu_s[...] = wu_f[slot].astype(BF16)
            wd_s[...] = wd_f[slot].astype(BF16)

            @pl.when(nxt_ref[e] != e)
            def _():
                for c in fetch(nxt_ref[e], 1 - slot):
                    c.start()

        xe = _unpack_bf16_pairs(xs_ref[...])
        g = _mm(xe, wg_s[...])
        u = _mm(xe, wu_s[...])
        hmid = ((g * (1.0 / (1.0 + jnp.exp(-g)))) * u).astype(BF16)
        yb_ref[...] = _mm(hmid, wd_s[...])

    @pl.when(i >= nu_ref[0])
    def _():
        yb_ref[...] = jnp.zeros_like(yb_ref)


def _moe(block_e, n_used, next_e, xs, w_gate, w_up, w_down, blk):
    n_blocks = block_e.shape[0]
    in_blk = lambda i, be, nu, nx: (jnp.maximum(jnp.minimum(i, nu[0] - 1), 0), 0)
    grid_spec = pltpu.PrefetchScalarGridSpec(
        num_scalar_prefetch=3,
        grid=(n_blocks,),
        in_specs=[
            pl.BlockSpec((blk, D_MODEL // 2), in_blk),
            pl.BlockSpec(memory_space=pl.ANY),
            pl.BlockSpec(memory_space=pl.ANY),
            pl.BlockSpec(memory_space=pl.ANY),
        ],
        out_specs=pl.BlockSpec((blk, D_MODEL), lambda i, be, nu, nx: (i, 0)),
        scratch_shapes=[
            pltpu.VMEM((2, D_MODEL, D_EXPERT), F32),
            pltpu.VMEM((2, D_MODEL, D_EXPERT), F32),
            pltpu.VMEM((2, D_EXPERT, D_MODEL), F32),
            pltpu.VMEM((D_MODEL, D_EXPERT), BF16),
            pltpu.VMEM((D_MODEL, D_EXPERT), BF16),
            pltpu.VMEM((D_EXPERT, D_MODEL), BF16),
            pltpu.SMEM((1,), jnp.int32),
            pltpu.SemaphoreType.DMA((2,)),
        ],
    )
    return pl.pallas_call(
        _moe_kernel,
        grid_spec=grid_spec,
        out_shape=jax.ShapeDtypeStruct((xs.shape[0], D_MODEL), F32),
        compiler_params=pltpu.CompilerParams(
            dimension_semantics=("arbitrary",), vmem_limit_bytes=VMEM_LIMIT),
        name="moe",
    )(block_e, n_used, next_e, xs, w_gate, w_up, w_down)


def _combine_kernel(dest_ref, dest_next_ref, *rest, tile_starts):
    n_parts = len(tile_starts)
    x2_refs, rt_refs = rest[:n_parts], rest[n_parts:2 * n_parts]
    yb_hbm = rest[2 * n_parts]
    o_refs = rest[2 * n_parts + 1:3 * n_parts + 1]
    buf, sem = rest[3 * n_parts + 1:]
    tm = x2_refs[0].shape[0]
    i = pl.program_id(0)
    slot = lax.rem(i, 2)

    def gather(d_ref, s):
        for k in range(2):
            for r in range(tm):
                pltpu.make_async_copy(yb_hbm.at[pl.ds(d_ref[0, 0, k * tm + r], 1), :],
                                      buf.at[s, k, pl.ds(r, 1), :],
                                      sem.at[s]).start(priority=r % 2)

    @pl.when(i == 0)
    def _():
        gather(dest_ref, 0)

    for s in range(2):
        @pl.when((i + 1 < pl.num_programs(0)) & (slot == 1 - s))
        def _():
            gather(dest_next_ref, s)

    for k in range(2):
        pltpu.make_async_copy(yb_hbm.at[pl.ds(0, tm), :], buf.at[slot, k], sem.at[slot]).wait()
    rt = _select_part(i, tile_starts, rt_refs)
    out = (_select_part(i, tile_starts, x2_refs) + rt[:, 2:3] * buf[slot, 0]
           + rt[:, 3:4] * buf[slot, 1])
    ends = tile_starts[1:] + (pl.num_programs(0),)
    for start, end, o_ref in zip(tile_starts, ends, o_refs):
        @pl.when((i >= start) & (i < end))
        def _():
            o_ref[...] = out


def _combine(dest_t, x2s, rts, yb, tm):
    counts, starts = _tile_layout(x2s, tm)
    nt = sum(counts)
    spec = lambda w: [_part_spec((tm, w), s, n) for s, n in zip(starts, counts)]
    return pl.pallas_call(
        functools.partial(_combine_kernel, tile_starts=tuple(starts)),
        grid=(nt,),
        in_specs=[
            pl.BlockSpec((1, 1, 2 * tm), lambda i: (i, 0, 0), memory_space=pltpu.SMEM),
            pl.BlockSpec((1, 1, 2 * tm), lambda i: (jnp.minimum(i + 1, nt - 1), 0, 0),
                         memory_space=pltpu.SMEM),
        ] + spec(D_MODEL) + spec(LANES) + [pl.BlockSpec(memory_space=pl.ANY)],
        out_specs=spec(D_MODEL),
        out_shape=[jax.ShapeDtypeStruct(x2.shape, F32) for x2 in x2s],
        scratch_shapes=[
            pltpu.VMEM((2, 2, tm, D_MODEL), F32),
            pltpu.SemaphoreType.DMA((2,)),
        ],
        compiler_params=pltpu.CompilerParams(
            dimension_semantics=("arbitrary",), vmem_limit_bytes=VMEM_LIMIT),
        name="combine",
    )(dest_t, dest_t, *x2s, *rts, yb)


def _hier_moe(parts, cnt, w_gate, w_up, w_down, tm, blk):
    counts = cnt[0, ROUTER_COL0:ROUTER_COL0 + N_EXPERTS].astype(jnp.int32)
    padded = (counts + blk - 1) // blk * blk
    pad_end = jnp.cumsum(padded)
    pad_start = pad_end - padded
    t_all = sum(p[0].shape[0] for p in parts)
    n_blocks = (2 * t_all + N_EXPERTS * (blk - 1)) // blk + 1
    rows = n_blocks * blk
    blk_row0 = jnp.arange(n_blocks, dtype=jnp.int32) * blk
    block_e = jnp.minimum(
        jnp.sum((pad_end[None, :] <= blk_row0[:, None]).astype(jnp.int32), axis=1),
        N_EXPERTS - 1)
    n_used = (pad_end[-1] // blk).astype(jnp.int32).reshape(1)
    ids = jnp.arange(N_EXPERTS, dtype=jnp.int32)
    later = (ids[None, :] > ids[:, None]) & (padded[None, :] > 0)
    next_e = jnp.where(jnp.any(later, axis=1),
                       jnp.min(jnp.where(later, ids[None, :], N_EXPERTS), axis=1), ids)
    experts = jnp.arange(N_EXPERTS, dtype=jnp.int32)[:, None, None]

    dests = []
    for x2, _, _, rtt in parts:
        T = x2.shape[0]
        flat = lambda a: jnp.swapaxes(a, 0, 1).reshape(a.shape[1], T)
        eid = flat(rtt[:, 0:2, :]).astype(jnp.int32)
        rank = flat(rtt[:, 4:6, :]).astype(jnp.int32)
        dest = rank + jnp.sum(
            jnp.where(eid[None] == experts, pad_start[:, None, None], 0), axis=0)
        nt = T // tm
        dests.append(dest.reshape(2, nt, tm).transpose(1, 0, 2).reshape(nt, 1, 2 * tm))
    dest_t = jnp.concatenate(dests, axis=0)
    xs = _dispatch(pad_end, padded, dest_t, [p[1] for p in parts], rows, tm, blk)
    yb = _moe(block_e, n_used, next_e.astype(jnp.int32), xs, w_gate, w_up, w_down, blk)
    return _combine(dest_t, [p[0] for p in parts], [p[2] for p in parts], yb, tm)


def _rope_table(pos):
    half = ROPE_DIM // 2
    d = jnp.arange(LANES, dtype=jnp.int32) % HEAD_DIM
    inv = ROPE_THETA ** (-(2 * (d % half)).astype(F32) / ROPE_DIM)
    ang = pos.astype(F32)[:, None] * inv[None, :]
    cos, sin = jnp.cos(ang), jnp.sin(ang)
    rotary = (d < ROPE_DIM)[None, :]
    first = (d < half)[None, :]
    return jnp.concatenate([jnp.where(rotary, cos, 1.0),
                            jnp.where(first, -sin, 0.0),
                            jnp.where(rotary & ~first, sin, 0.0)], axis=1)


def _mixers(x, pos_rope, kctx_prev, vctx_prev, h0r, h0i, mk, mv, wp, sp, cnt0, *,
            tm_in, tq, ssm_l, tm_mid):
    B, S, _ = x.shape
    T = B * S
    q, k3, v3, u_tm = _in_proj(x, wp["gmix"], wp["win"], wp["gq"], wp["gk"], pos_rope, *tm_in)
    if kctx_prev is None:
        kctx, vctx = k3, v3
    else:
        kctx = jnp.concatenate([kctx_prev, k3], axis=1)
        vctx = jnp.concatenate([vctx_prev, v3], axis=1)
    att = _swa(wp["sink"], q, kctx, vctx, tq, mask_context=kctx_prev is None)
    ssm_tm, hr, hi = _ssm(u_tm.reshape(S, B, SSM_WIDTH), h0r, h0i, sp, ssm_l)
    x2, hn, rt, rtt, cnt = _mid(x, att, ssm_tm.reshape(S, B * SSM_WIDTH), mk, mv, wp, cnt0,
                                *tm_mid)
    part = (x2.reshape(T, D_MODEL), hn.reshape(T, D_MODEL), rt.reshape(T, LANES), rtt)
    return part, cnt, k3, v3, hr, hi


def kernel(x_prompt, x_sample, cache_attn_k, cache_attn_v, state_ssm_re, state_ssm_im, cache_mem_k, cache_mem_v, mem_prompt, norm_mix, w_in, q_norm, k_norm, attn_sink, ssm_lambda_re, ssm_lambda_im, ssm_log_dt, ssm_b_re, ssm_b_im, ssm_c_re, ssm_c_im, ssm_d, ssm_w_glu, ssm_b_glu, norm_attn_out, norm_ssm_out, w_out, norm_cross, norm_mem, w_cq, w_ck, w_cv, cq_norm, ck_norm, w_co, norm_ffn, w_router_group, b_router_group, w_router_expert, b_router_expert, w_e_gate, w_e_up, w_e_down):
    depth = norm_mix.shape[0]
    Bp, Lp, _ = x_prompt.shape
    Bs, Ls, _ = x_sample.shape
    yp, ys = x_prompt, x_sample
    rope_p = _rope_table(jnp.arange(Lp, dtype=jnp.int32))
    rope_s = _rope_table(PAST_LEN + jnp.arange(Ls, dtype=jnp.int32))
    outs = [[] for _ in range(10)]
    n_router = N_EXPERT_GROUPS + N_EXPERTS
    for l in range(depth):
        row = lambda a: a[l].astype(F32).reshape(1, -1)
        w_r = jnp.pad(jnp.concatenate([w_router_group[l], w_router_expert[l]], axis=1).astype(F32),
                      ((0, 0), (0, LANES - n_router)))
        w_r_hi = w_r.astype(BF16)
        w_r_lo = (w_r - w_r_hi.astype(F32)).astype(BF16)
        b_r = jnp.pad(jnp.concatenate([b_router_group[l], b_router_expert[l]]).astype(F32),
                      (0, LANES - n_router)).reshape(1, LANES)
        wp = {
            "gmix": row(norm_mix), "win": w_in[l].astype(BF16),
            "gq": jnp.tile(row(q_norm), (1, LANES // HEAD_DIM)),
            "gk": jnp.tile(row(k_norm), (1, LANES // HEAD_DIM)),
            "sink": attn_sink[l].astype(F32),
            "gao": row(norm_attn_out), "gso": row(norm_ssm_out),
            "wout": w_out[l].astype(BF16), "gx": row(norm_cross),
            "wcq": w_cq[l].astype(BF16), "gcq": row(cq_norm),
            "wco": w_co[l].astype(BF16), "gffn": row(norm_ffn),
            "wr": jnp.concatenate([w_r_hi, w_r_lo], axis=1), "br": b_r,
        }
        sp = _ssm_params(ssm_lambda_re[l], ssm_lambda_im[l], ssm_log_dt[l], ssm_b_re[l],
                         ssm_b_im[l], ssm_c_re[l], ssm_c_im[l], ssm_d[l], ssm_w_glu[l],
                         ssm_b_glu[l])
        ew = (w_e_gate[l].astype(F32), w_e_up[l].astype(F32), w_e_down[l].astype(F32))

        w_ckv = jnp.concatenate([w_ck[l], w_cv[l]], axis=1).astype(BF16)
        mkp, mvp = _memkv(mem_prompt.reshape(Bp * N_MEM, D_MODEL), row(norm_mem), w_ckv,
                          row(ck_norm), 512)
        mkp = mkp.reshape(Bp, N_MEM, CA_WIDTH)
        mvp = mvp.reshape(Bp, N_MEM, CA_WIDTH)

        zst = jnp.zeros((Bp, SSM_COLS), F32)
        part_p, cnt_p, kp, vp, hpr, hpi = _mixers(
            yp, rope_p, None, None, zst, zst, mkp, mvp, wp, sp, jnp.zeros((1, LANES), F32),
            tm_in=(1, 512), tq=256, ssm_l=64, tm_mid=(1, 512))
        part_s, cnt_s, kn, vn, hsr, hsi = _mixers(
            ys, rope_s, cache_attn_k[l].reshape(Bs, WINDOW, KV_WIDTH).astype(F32),
            cache_attn_v[l].reshape(Bs, WINDOW, KV_WIDTH).astype(F32),
            state_ssm_re[l].astype(F32).reshape(Bs, SSM_COLS),
            state_ssm_im[l].astype(F32).reshape(Bs, SSM_COLS),
            cache_mem_k[l].astype(F32).reshape(Bs, N_MEM, CA_WIDTH),
            cache_mem_v[l].astype(F32).reshape(Bs, N_MEM, CA_WIDTH), wp, sp, cnt_p,
            tm_in=(8, Ls), tq=CHUNK, ssm_l=Ls, tm_mid=(8, Ls))
        yp, ys = _hier_moe([part_p, part_s], cnt_s, *ew, 256, 2 * MOE_BLOCK)
        yp = yp.reshape(Bp, Lp, D_MODEL)
        ys = ys.reshape(Bs, Ls, D_MODEL)

        sg = (N_SSM_GROUPS, SSM_STATE)
        kvs = (N_KV_HEADS, HEAD_DIM)
        vals = (kp[:, Lp - WINDOW:].reshape(Bp, WINDOW, *kvs),
                vp[:, Lp - WINDOW:].reshape(Bp, WINDOW, *kvs),
                hpr.reshape(Bp, *sg), hpi.reshape(Bp, *sg),
                mkp.reshape(Bp, N_MEM, CA_HEADS, CA_HEAD_DIM),
                mvp.reshape(Bp, N_MEM, CA_HEADS, CA_HEAD_DIM),
                kn.reshape(Bs, Ls, *kvs), vn.reshape(Bs, Ls, *kvs),
                hsr.reshape(Bs, *sg), hsi.reshape(Bs, *sg))
        for lst, val in zip(outs, vals):
            lst.append(val)
    return (yp, ys) + tuple(jnp.stack(lst) for lst in outs)
```

```python
import functools
import math

import jax
import jax.numpy as jnp
from jax import lax
from jax.experimental import pallas as pl
from jax.experimental.pallas import tpu as pltpu

F32 = jnp.float32
BF16 = jnp.bfloat16

D_MODEL = 1024
CHUNK = 64
N_Q_HEADS = 8
N_KV_HEADS = 2
GQA = N_Q_HEADS // N_KV_HEADS
HEAD_DIM = 64
WINDOW = 128
BAND = WINDOW + CHUNK
ROPE_DIM = HEAD_DIM // 4
ROPE_THETA = 500000.0
ATT_WIDTH = N_Q_HEADS * HEAD_DIM
KV_WIDTH = N_KV_HEADS * HEAD_DIM
SSM_GROUP = 16
SSM_WIDTH = D_MODEL // 2
N_SSM_GROUPS = SSM_WIDTH // SSM_GROUP
SSM_STATE = 64
SSM_COLS = N_SSM_GROUPS * SSM_STATE
IN_WIDTH = ATT_WIDTH + 2 * KV_WIDTH + SSM_WIDTH
N_MEM = 256
CA_HEADS = 4
CA_HEAD_DIM = 128
CA_WIDTH = CA_HEADS * CA_HEAD_DIM
N_EXPERT_GROUPS = 4
EXPERTS_PER_GROUP = 8
N_EXPERTS = N_EXPERT_GROUPS * EXPERTS_PER_GROUP
D_EXPERT = 512
MOE_BLOCK = 256
EPS = 1e-6
NEG = -1e30
PAST_LEN = 4096

LANES = 128
ROUTER_COL0 = N_EXPERT_GROUPS
VMEM_LIMIT = 48 * 1024 * 1024


def _rms(x, g):
    ms = jnp.mean(x * x, axis=-1, keepdims=True)
    return (x * lax.rsqrt(ms + EPS)) * g


def _mm(a, b):
    return jnp.dot(a, b, preferred_element_type=F32)


_HI_HALF = 0xFFFF0000


def _pack_bf16_pairs(x):
    half = x.shape[1] // 2
    bits = lambda v: lax.bitcast_convert_type(v.astype(BF16).astype(F32), jnp.uint32)
    return (lax.shift_right_logical(bits(x[:, :half]), jnp.uint32(16))
            | (bits(x[:, half:]) & jnp.uint32(_HI_HALF)))


def _unpack_bf16_pairs(w):
    lo = lax.bitcast_convert_type(lax.shift_left(w, jnp.uint32(16)), F32)
    hi = lax.bitcast_convert_type(w & jnp.uint32(_HI_HALF), F32)
    return jnp.concatenate([lo.astype(BF16), hi.astype(BF16)], axis=1)


def _in_proj_kernel(x_ref, g_ref, w_ref, gq_ref, gk_ref, rope_ref,
                    q_ref, k_ref, v_ref, u_ref):
    nb, ts, _ = x_ref.shape
    tm = nb * ts
    h = _rms(x_ref[...].reshape(tm, D_MODEL), g_ref[...])
    hin = _mm(h.astype(BF16), w_ref[...])
    rope = jnp.concatenate([rope_ref[...]] * nb, axis=0)
    cos = rope[:, 0:LANES]
    sin_lo = rope[:, LANES:2 * LANES]
    sin_hi = rope[:, 2 * LANES:3 * LANES]
    lane = lax.broadcasted_iota(jnp.int32, (tm, LANES), 1)
    left = lane < HEAD_DIM

    def norm_rope(z, g):
        sq = z * z
        lsum = jnp.sum(jnp.where(left, sq, 0.0), axis=-1, keepdims=True)
        rsum = jnp.sum(jnp.where(left, 0.0, sq), axis=-1, keepdims=True)
        ms = jnp.where(left, lsum, rsum) * (1.0 / HEAD_DIM)
        zn = (z * lax.rsqrt(ms + EPS)) * g
        half = ROPE_DIM // 2
        return (zn * cos + pltpu.roll(zn, LANES - half, 1) * sin_lo
                + pltpu.roll(zn, half, 1) * sin_hi)

    for j in range(ATT_WIDTH // LANES):
        sl = slice(j * LANES, (j + 1) * LANES)
        q_ref[:, :, sl] = norm_rope(hin[:, sl], gq_ref[...]).reshape(nb, ts, LANES)
    k_ref[...] = norm_rope(hin[:, ATT_WIDTH:ATT_WIDTH + KV_WIDTH],
                           gk_ref[...]).reshape(nb, ts, KV_WIDTH)
    v_ref[...] = hin[:, ATT_WIDTH + KV_WIDTH:ATT_WIDTH + 2 * KV_WIDTH].reshape(nb, ts, KV_WIDTH)
    for b in range(nb):
        u_ref[:, b * SSM_WIDTH:(b + 1) * SSM_WIDTH] = (
            hin[b * ts:(b + 1) * ts, ATT_WIDTH + 2 * KV_WIDTH:])


def _in_proj(x, g, w_bf, gq, gk, rope, nb, ts):
    B, S, _ = x.shape
    full = lambda b, i: (0, 0)
    tile = lambda w: pl.BlockSpec((nb, ts, w), lambda b, i: (b, i, 0))
    return pl.pallas_call(
        _in_proj_kernel,
        grid=(B // nb, S // ts),
        in_specs=[
            tile(D_MODEL),
            pl.BlockSpec((1, D_MODEL), full),
            pl.BlockSpec((D_MODEL, IN_WIDTH), full),
            pl.BlockSpec((1, LANES), full),
            pl.BlockSpec((1, LANES), full),
            pl.BlockSpec((ts, 3 * LANES), lambda b, i: (i, 0)),
        ],
        out_specs=[
            tile(ATT_WIDTH), tile(KV_WIDTH), tile(KV_WIDTH),
            pl.BlockSpec((ts, nb * SSM_WIDTH), lambda b, i: (i, b)),
        ],
        out_shape=[
            jax.ShapeDtypeStruct((B, S, ATT_WIDTH), F32),
            jax.ShapeDtypeStruct((B, S, KV_WIDTH), F32),
            jax.ShapeDtypeStruct((B, S, KV_WIDTH), F32),
            jax.ShapeDtypeStruct((S, B * SSM_WIDTH), F32),
        ],
        compiler_params=pltpu.CompilerParams(
            dimension_semantics=("arbitrary", "arbitrary"),
            vmem_limit_bytes=VMEM_LIMIT),
        name="in_proj",
    )(x, g, w_bf, gq, gk, rope)


def _swa_kernel(sink_ref, q_ref, k_ref, v_ref, o_ref, *, mask_context):
    tq = q_ref.shape[1]
    i = pl.program_id(1)
    nch = tq // CHUNK
    lane = lax.broadcasted_iota(jnp.int32, (BAND, LANES), 1)
    lo_half = lane < HEAD_DIM
    vrow_lo = lax.broadcasted_iota(jnp.int32, (LANES, BAND), 0) < HEAD_DIM
    q_lo = lax.broadcasted_iota(jnp.int32, (1, LANES), 1) < CHUNK
    slabs_per_kv = GQA * HEAD_DIM // LANES

    units = []
    scores = []
    vpads = {}
    for c in range(nch):
        chunk = i * nch + c
        if mask_context:
            first = jnp.maximum(chunk - WINDOW // CHUNK, 0)
            start = pl.multiple_of(first * CHUNK, CHUNK)
            kidx = start + lax.broadcasted_iota(jnp.int32, (BAND, LANES), 0)
            valid = kidx < (chunk + 1) * CHUNK
        else:
            start = pl.multiple_of(chunk * CHUNK, CHUNK)
        kb = k_ref[0, pl.ds(start, BAND), :]
        kb_sw = pltpu.roll(kb, HEAD_DIM, 1)
        vt = v_ref[0, pl.ds(start, BAND), :].T
        vt_sw = jnp.concatenate([vt[HEAD_DIM:], vt[:HEAD_DIM]], axis=0)
        for kvh in range(N_KV_HEADS):
            k_own, k_oth = (kb, kb_sw) if kvh == 0 else (kb_sw, kb)
            v_own, v_oth = (vt, vt_sw) if kvh == 0 else (vt_sw, vt)
            kpad = (jnp.where(lo_half, k_own, 0.0).astype(BF16),
                    jnp.where(lo_half, 0.0, k_oth).astype(BF16))
            vpads[(c, kvh)] = (jnp.where(vrow_lo, v_own, 0.0).astype(BF16),
                               jnp.where(vrow_lo, 0.0, v_oth).astype(BF16))
            col0 = kvh * GQA * HEAD_DIM
            q2 = jnp.concatenate(
                [q_ref[0, c * CHUNK:(c + 1) * CHUNK, col0 + m * LANES:col0 + (m + 1) * LANES]
                 for m in range(slabs_per_kv)], axis=0).astype(BF16)
            for side in range(2):
                s = lax.dot_general(kpad[side], q2, (((1,), (1,)), ((), ())),
                                    preferred_element_type=F32) * (HEAD_DIM ** -0.5)
                if mask_context:
                    s = jnp.where(valid, s, NEG)
                units.append((c, kvh, side))
                scores.append(s)

    sinks = [jnp.where(q_lo, sink_ref[kvh * GQA + side], sink_ref[kvh * GQA + 2 + side])
             for (_, kvh, side) in units]
    maxes = [jnp.maximum(jnp.max(s, axis=0, keepdims=True), sk)
             for s, sk in zip(scores, sinks)]
    exps = [jnp.exp(s - mx) for s, mx in zip(scores, maxes)]
    dens = [jnp.sum(p, axis=0, keepdims=True) + jnp.exp(sk - mx)
            for p, sk, mx in zip(exps, sinks, maxes)]
    probs = [(p * (1.0 / den)).astype(BF16) for p, den in zip(exps, dens)]

    for n in range(0, len(units), 2):
        c, kvh, _ = units[n]
        vp = vpads[(c, kvh)]
        o = (_mm(vp[0], probs[n]) + _mm(vp[1], probs[n + 1])).T
        col0 = kvh * GQA * HEAD_DIM
        for m in range(slabs_per_kv):
            o_ref[0, c * CHUNK:(c + 1) * CHUNK, col0 + m * LANES:col0 + (m + 1) * LANES] = (
                o[m * CHUNK:(m + 1) * CHUNK])


def _swa(sink, q, kctx, vctx, tq, mask_context):
    B, Sq, _ = q.shape
    Sk = kctx.shape[1]
    return pl.pallas_call(
        functools.partial(_swa_kernel, mask_context=mask_context),
        grid=(B, Sq // tq),
        in_specs=[
            pl.BlockSpec(memory_space=pltpu.SMEM),
            pl.BlockSpec((1, tq, ATT_WIDTH), lambda b, i: (b, i, 0)),
            pl.BlockSpec((1, Sk, KV_WIDTH), lambda b, i: (b, 0, 0)),
            pl.BlockSpec((1, Sk, KV_WIDTH), lambda b, i: (b, 0, 0)),
        ],
        out_specs=pl.BlockSpec((1, tq, ATT_WIDTH), lambda b, i: (b, i, 0)),
        out_shape=jax.ShapeDtypeStruct((B, Sq, ATT_WIDTH), F32),
        compiler_params=pltpu.CompilerParams(
            dimension_semantics=("arbitrary", "arbitrary"),
            vmem_limit_bytes=VMEM_LIMIT),
        name="swa",
    )(sink, q, kctx, vctx)


def _ssm_kernel(u_ref, h0r_ref, h0i_ref, lam_ref, bre_ref, bim_ref, cre_ref, cim_ref,
                d_ref, wglu_ref, bglu_ref,
                y_ref, hr_out, hi_out, sr0, si0, sr1, si1, hr_s, hi_s):
    L, B, _ = u_ref.shape
    rows = L * B
    half_w = SSM_WIDTH // 2
    half_c = SSM_COLS // 2
    halves = ((sr0, si0), (sr1, si1))

    @pl.when(pl.program_id(0) == 0)
    def _():
        hr_s[...] = h0r_ref[...]
        hi_s[...] = h0i_ref[...]

    u = u_ref[...].reshape(rows, SSM_WIDTH)
    ub = u.astype(BF16)

    def project_in(hf):
        sr, si = halves[hf]
        uh = ub[:, hf * half_w:(hf + 1) * half_w]
        sr[...] = _mm(uh, bre_ref[hf])
        si[...] = _mm(uh, bim_ref[hf])

    def recur(hf):
        sr, si = halves[hf]
        cw = 4 * LANES
        for cc in range(half_c // cw):
            cols = slice(cc * cw, (cc + 1) * cw)
            gcols = slice(hf * half_c + cc * cw, hf * half_c + (cc + 1) * cw)
            lr = jnp.broadcast_to(lam_ref[0:1, gcols], (B, cw))
            li = jnp.broadcast_to(lam_ref[1:2, gcols], (B, cw))
            hr, hi = hr_s[:, gcols], hi_s[:, gcols]
            for t in range(L):
                at_t = slice(t * B, (t + 1) * B)
                hr, hi = (lr * hr - li * hi + sr[at_t, cols],
                          lr * hi + li * hr + si[at_t, cols])
                sr[at_t, cols] = hr
                si[at_t, cols] = hi
            hr_s[:, gcols] = hr
            hi_s[:, gcols] = hi

    def project_out(hf):
        sr, si = halves[hf]
        return (_mm(sr[...].astype(BF16), cre_ref[hf]) + _mm(si[...].astype(BF16), cim_ref[hf]))

    project_in(0)
    project_in(1)
    recur(0)
    y0 = project_out(0)
    recur(1)
    y1 = project_out(1)
    y = jnp.concatenate([y0, y1], axis=1) + d_ref[...] * u
    g = 0.5 * y * (1.0 + jnp.tanh(math.sqrt(2.0 / math.pi) * (y + 0.044715 * (y * y * y))))
    gb = g.astype(BF16)
    z = jnp.concatenate(
        [_mm(gb[:, hf * half_w:(hf + 1) * half_w], wglu_ref[hf]) for hf in range(2)],
        axis=1) + bglu_ref[...]
    out = g * (1.0 / (1.0 + jnp.exp(-z)))
    y_ref[...] = out.reshape(L, B, SSM_WIDTH)
    hr_out[...] = hr_s[...]
    hi_out[...] = hi_s[...]


def _ssm(u, h0r, h0i, sp, L):
    S, B, _ = u.shape
    c2 = lambda i: (0, 0)
    c3 = lambda i: (0, 0, 0)
    return pl.pallas_call(
        _ssm_kernel,
        grid=(S // L,),
        in_specs=[
            pl.BlockSpec((L, B, SSM_WIDTH), lambda i: (i, 0, 0)),
            pl.BlockSpec((B, SSM_COLS), c2),
            pl.BlockSpec((B, SSM_COLS), c2),
            pl.BlockSpec((2, SSM_COLS), c2),
            pl.BlockSpec((2, SSM_WIDTH // 2, SSM_COLS // 2), c3),
            pl.BlockSpec((2, SSM_WIDTH // 2, SSM_COLS // 2), c3),
            pl.BlockSpec((2, SSM_COLS // 2, SSM_WIDTH // 2), c3),
            pl.BlockSpec((2, SSM_COLS // 2, SSM_WIDTH // 2), c3),
            pl.BlockSpec((1, SSM_WIDTH), c2),
            pl.BlockSpec((2, SSM_WIDTH // 2, SSM_WIDTH // 2), c3),
            pl.BlockSpec((1, SSM_WIDTH), c2),
        ],
        out_specs=[
            pl.BlockSpec((L, B, SSM_WIDTH), lambda i: (i, 0, 0)),
            pl.BlockSpec((B, SSM_COLS), c2),
            pl.BlockSpec((B, SSM_COLS), c2),
        ],
        out_shape=[
            jax.ShapeDtypeStruct((S, B, SSM_WIDTH), F32),
            jax.ShapeDtypeStruct((B, SSM_COLS), F32),
            jax.ShapeDtypeStruct((B, SSM_COLS), F32),
        ],
        scratch_shapes=[
            pltpu.VMEM((L * B, SSM_COLS // 2), F32),
            pltpu.VMEM((L * B, SSM_COLS // 2), F32),
            pltpu.VMEM((L * B, SSM_COLS // 2), F32),
            pltpu.VMEM((L * B, SSM_COLS // 2), F32),
            pltpu.VMEM((B, SSM_COLS), F32),
            pltpu.VMEM((B, SSM_COLS), F32),
        ],
        compiler_params=pltpu.CompilerParams(
            dimension_semantics=("arbitrary",), vmem_limit_bytes=VMEM_LIMIT),
        name="ssm",
    )(u, h0r, h0i, sp["lam"], sp["bre"], sp["bim"], sp["cre"], sp["cim"],
      sp["d"], sp["wglu"], sp["bglu"])


def _block_diag(blocks):
    G, r, c = blocks.shape
    col = jnp.arange(G * c, dtype=jnp.int32)
    spread = (col[None, :] % c == jnp.arange(c, dtype=jnp.int32)[:, None]).astype(F32)
    same_group = (jnp.arange(G * r, dtype=jnp.int32)[:, None] // r) == (col[None, :] // c)
    tiled = jnp.dot(blocks.reshape(G * r, c), spread, precision=lax.Precision.HIGHEST)
    return jnp.where(same_group, tiled, 0.0)


def _ssm_params(lam_re, lam_im, log_dt, b_re, b_im, c_re, c_im, d, w_glu, b_glu):
    lam = lax.complex(lam_re.astype(F32), lam_im.astype(F32))
    dt = jnp.exp(log_dt.astype(F32))[:, None]
    lam_bar = jnp.exp(lam * dt)
    bmat = lax.complex(b_re.astype(F32), b_im.astype(F32))
    b_bar = ((lam_bar - 1.0) / lam)[..., None] * bmat
    lam2 = jnp.stack([lam_bar.real.reshape(-1), lam_bar.imag.reshape(-1)])
    hw, hc = SSM_WIDTH // 2, SSM_COLS // 2
    split_b = lambda m: jnp.stack([m[:hw, :hc], m[hw:, hc:]]).astype(BF16)
    split_c = lambda m: jnp.stack([m[:hc, :hw], m[hc:, hw:]]).astype(BF16)
    wg = _block_diag(w_glu.astype(F32))
    return {
        "lam": lam2,
        "bre": split_b(_block_diag(b_bar.real).T),
        "bim": split_b(_block_diag(b_bar.imag).T),
        "cre": split_c(_block_diag(c_re.astype(F32)).T),
        "cim": split_c(_block_diag(-c_im.astype(F32)).T),
        "d": d.astype(F32).reshape(1, SSM_WIDTH),
        "wglu": jnp.stack([wg[:hw, :hw], wg[hw:, hw:]]).astype(BF16),
        "bglu": b_glu.astype(F32).reshape(1, SSM_WIDTH),
    }


def _memkv_kernel(m_ref, g_ref, w_ref, gk_ref, k_ref, v_ref):
    m = _rms(m_ref[...], g_ref[...])
    kv = _mm(m.astype(BF16), w_ref[...])
    for h in range(CA_HEADS):
        sl = slice(h * CA_HEAD_DIM, (h + 1) * CA_HEAD_DIM)
        k_ref[:, sl] = _rms(kv[:, sl], gk_ref[...])
    v_ref[...] = kv[:, CA_WIDTH:]


def _memkv(mem2d, g, w_bf, gk, tm):
    T = mem2d.shape[0]
    full = lambda i: (0, 0)
    return pl.pallas_call(
        _memkv_kernel,
        grid=(T // tm,),
        in_specs=[
            pl.BlockSpec((tm, D_MODEL), lambda i: (i, 0)),
            pl.BlockSpec((1, D_MODEL), full),
            pl.BlockSpec((D_MODEL, 2 * CA_WIDTH), full),
            pl.BlockSpec((1, CA_HEAD_DIM), full),
        ],
        out_specs=[
            pl.BlockSpec((tm, CA_WIDTH), lambda i: (i, 0)),
            pl.BlockSpec((tm, CA_WIDTH), lambda i: (i, 0)),
        ],
        out_shape=[
            jax.ShapeDtypeStruct((T, CA_WIDTH), F32),
            jax.ShapeDtypeStruct((T, CA_WIDTH), F32),
        ],
        compiler_params=pltpu.CompilerParams(
            dimension_semantics=("arbitrary",), vmem_limit_bytes=VMEM_LIMIT),
        name="memkv",
    )(mem2d, g, w_bf, gk)


def _mid_kernel(x_ref, att_ref, ssm_ref, mk_ref, mv_ref,
                gao_ref, gso_ref, wout_ref, gx_ref, wcq_ref, gcq_ref, wco_ref,
                gffn_ref, wr_ref, br_ref, cnt0_ref, tri_ref,
                x2_ref, hn_ref, rt_ref, rtt_ref, cnt_ref, base_s):
    nb, ts, _ = x_ref.shape
    tm = nb * ts

    @pl.when((pl.program_id(0) == 0) & (pl.program_id(1) == 0))
    def _():
        base_s[...] = cnt0_ref[...]

    ssm = jnp.concatenate(
        [ssm_ref[:, b * SSM_WIDTH:(b + 1) * SSM_WIDTH] for b in range(nb)], axis=0)
    a = _rms(att_ref[...].reshape(tm, ATT_WIDTH), gao_ref[...]).astype(BF16)
    s = _rms(ssm, gso_ref[...]).astype(BF16)
    x1 = (x_ref[...].reshape(tm, D_MODEL) + _mm(a, wout_ref[0:ATT_WIDTH, :])
          + _mm(s, wout_ref[ATT_WIDTH:, :]))

    qx = _mm(_rms(x1, gx_ref[...]).astype(BF16), wcq_ref[...])
    heads = []
    for h in range(CA_HEADS):
        sl = slice(h * CA_HEAD_DIM, (h + 1) * CA_HEAD_DIM)
        qh = _rms(qx[:, sl], gcq_ref[...]).astype(BF16)
        per_batch = []
        for b in range(nb):
            kh = mk_ref[b, :, sl].astype(BF16)
            qb = qh[b * ts:(b + 1) * ts]
            if ts <= LANES:
                vt = mv_ref[b, :, sl].T.astype(BF16)
                sc = lax.dot_general(kh, qb, (((1,), (1,)), ((), ())),
                                     preferred_element_type=F32) * (CA_HEAD_DIM ** -0.5)
                p = jnp.exp(sc - jnp.max(sc, axis=0, keepdims=True))
                p = p * (1.0 / jnp.sum(p, axis=0, keepdims=True))
                per_batch.append(_mm(vt, p.astype(BF16)).T)
            else:
                vh = mv_ref[b, :, sl].astype(BF16)
                sc = lax.dot_general(qb, kh, (((1,), (1,)), ((), ())),
                                     preferred_element_type=F32) * (CA_HEAD_DIM ** -0.5)
                p = jnp.exp(sc - jnp.max(sc, axis=-1, keepdims=True))
                p = p / jnp.sum(p, axis=-1, keepdims=True)
                per_batch.append(_mm(p.astype(BF16), vh))
        heads.append(jnp.concatenate(per_batch, axis=0))
    o = jnp.concatenate(heads, axis=1).astype(BF16)
    x2 = x1 + _mm(o, wco_ref[...])
    x2_ref[...] = x2.reshape(nb, ts, D_MODEL)

    hn = _rms(x2, gffn_ref[...])
    hn_ref[...] = hn.reshape(nb, ts, D_MODEL)

    h_hi = hn.astype(BF16)
    h_lo = (hn - h_hi.astype(F32)).astype(BF16)
    r1 = _mm(h_hi, wr_ref[...])
    lg = (r1[:, :LANES] + r1[:, LANES:] + _mm(h_lo, wr_ref[:, 0:LANES])
          + br_ref[...])

    col = lax.broadcasted_iota(jnp.int32, (tm, LANES), 1)
    big = jnp.int32(4 * LANES)
    gmask = col < N_EXPERT_GROUPS
    lgg = jnp.where(gmask, lg, NEG)
    mg = jnp.max(lgg, axis=-1, keepdims=True)
    grp = jnp.min(jnp.where(gmask & (lgg == mg), col, big), axis=-1, keepdims=True)
    pg_top = 1.0 / jnp.sum(jnp.where(gmask, jnp.exp(lgg - mg), 0.0), axis=-1, keepdims=True)

    ecol = col - ROUTER_COL0
    emask = ((ecol >= 0) & (ecol < N_EXPERTS)
             & (lax.shift_right_arithmetic(ecol, 3) == grp))
    le = jnp.where(emask, lg, NEG)
    m1 = jnp.max(le, axis=-1, keepdims=True)
    i1 = jnp.min(jnp.where(emask & (le == m1), col, big), axis=-1, keepdims=True)
    rest = emask & (col != i1)
    le2 = jnp.where(rest, lg, NEG)
    m2 = jnp.max(le2, axis=-1, keepdims=True)
    i2 = jnp.min(jnp.where(rest & (le2 == m2), col, big), axis=-1, keepdims=True)
    den = jnp.sum(jnp.where(emask, jnp.exp(le - m1), 0.0), axis=-1, keepdims=True)
    p1 = 1.0 / den
    p2 = jnp.exp(m2 - m1) / den
    gate1 = pg_top * p1 / (p1 + p2)
    gate2 = pg_top * p2 / (p1 + p2)

    sel1 = col == i1
    sel2 = col == i2
    oh = jnp.where(sel1 | sel2, 1.0, 0.0)
    tot = base_s[...] + _mm(tri_ref[...], oh.astype(BF16))
    rank1 = jnp.sum(jnp.where(sel1, tot, 0.0), axis=-1, keepdims=True)
    rank2 = jnp.sum(jnp.where(sel2, tot, 0.0), axis=-1, keepdims=True)
    base_s[...] = base_s[...] + jnp.sum(oh, axis=0, keepdims=True)
    cnt_ref[...] = base_s[...]

    e1 = (i1 - ROUTER_COL0).astype(F32)
    e2 = (i2 - ROUTER_COL0).astype(F32)
    rt = jnp.zeros((tm, LANES), F32)
    for k, val in enumerate((e1, e2, gate1, gate2, rank1, rank2)):
        rt = jnp.where(col == k, val, rt)
    rt_ref[...] = rt.reshape(nb, ts, LANES)
    rtt_ref[...] = rt.T[0:8, :]


def _mid(x, att, ssm_tm, mk, mv, wp, cnt0, nb, ts):
    B, S, _ = x.shape
    assert nb == 1 or ts == S
    c2 = lambda b, i: (0, 0)
    tile = lambda w: pl.BlockSpec((nb, ts, w), lambda b, i: (b, i, 0))
    return pl.pallas_call(
        _mid_kernel,
        grid=(B // nb, S // ts),
        in_specs=[
            tile(D_MODEL), tile(ATT_WIDTH),
            pl.BlockSpec((ts, nb * SSM_WIDTH), lambda b, i: (i, b)),
            pl.BlockSpec((nb, N_MEM, CA_WIDTH), lambda b, i: (b, 0, 0)),
            pl.BlockSpec((nb, N_MEM, CA_WIDTH), lambda b, i: (b, 0, 0)),
            pl.BlockSpec((1, ATT_WIDTH), c2),
            pl.BlockSpec((1, SSM_WIDTH), c2),
            pl.BlockSpec((ATT_WIDTH + SSM_WIDTH, D_MODEL), c2),
            pl.BlockSpec((1, D_MODEL), c2),
            pl.BlockSpec((D_MODEL, CA_WIDTH), c2),
            pl.BlockSpec((1, CA_HEAD_DIM), c2),
            pl.BlockSpec((CA_WIDTH, D_MODEL), c2),
            pl.BlockSpec((1, D_MODEL), c2),
            pl.BlockSpec((D_MODEL, 2 * LANES), c2),
            pl.BlockSpec((1, LANES), c2),
            pl.BlockSpec((1, LANES), c2),
            pl.BlockSpec((nb * ts, nb * ts), c2),
        ],
        out_specs=[
            tile(D_MODEL), tile(D_MODEL), tile(LANES),
            pl.BlockSpec((8, nb * ts), lambda b, i: (0, b * (S // ts) + i)),
            pl.BlockSpec((1, LANES), c2),
        ],
        out_shape=[
            jax.ShapeDtypeStruct((B, S, D_MODEL), F32),
            jax.ShapeDtypeStruct((B, S, D_MODEL), F32),
            jax.ShapeDtypeStruct((B, S, LANES), F32),
            jax.ShapeDtypeStruct((8, B * S), F32),
            jax.ShapeDtypeStruct((1, LANES), F32),
        ],
        scratch_shapes=[pltpu.VMEM((1, LANES), F32)],
        compiler_params=pltpu.CompilerParams(
            dimension_semantics=("arbitrary", "arbitrary"),
            vmem_limit_bytes=VMEM_LIMIT),
        name="mid",
    )(x, att, ssm_tm, mk, mv, wp["gao"], wp["gso"], wp["wout"], wp["gx"], wp["wcq"],
      wp["gcq"], wp["wco"], wp["gffn"], wp["wr"], wp["br"], cnt0,
      jnp.tri(nb * ts, k=-1, dtype=BF16))


def _select_part(i, tile_starts, refs):
    x = refs[0][...]
    for start, ref in zip(tile_starts[1:], refs[1:]):
        x = jnp.where(i >= start, ref[...], x)
    return x


def _part_spec(shape, tile_start, n_tiles):
    def index(i, *_):
        return (jnp.clip(i - tile_start, 0, n_tiles - 1),) + (0,) * (len(shape) - 1)
    return pl.BlockSpec(shape, index)


def _dispatch_kernel(pend_ref, padded_ref, dest_ref, *rest, tile_starts):
    n_parts = len(tile_starts)
    hn_refs = rest[:n_parts]
    xs_hbm, stage, zbuf, sem = rest[n_parts:]
    tm = hn_refs[0].shape[0]
    i = pl.program_id(0)
    slot = lax.rem(i, 2)
    blk = zbuf.shape[0]

    def wait_rows(s):
        for _ in range(2):
            pltpu.make_async_copy(stage.at[s], xs_hbm.at[pl.ds(0, tm)], sem.at[s]).wait()

    @pl.when(i == 0)
    def _():
        zbuf[...] = jnp.zeros_like(zbuf)
        for e in range(N_EXPERTS):
            @pl.when(padded_ref[e] > 0)
            def _():
                row0 = pl.multiple_of(pend_ref[e] - blk, blk)
                fill = pltpu.make_async_copy(zbuf, xs_hbm.at[pl.ds(row0, blk)], sem.at[2])
                fill.start()
                fill.wait()

        def fill_tail(b, carry):
            fill = pltpu.make_async_copy(
                zbuf, xs_hbm.at[pl.ds(pl.multiple_of(b * blk, blk), blk)], sem.at[2])
            fill.start()
            fill.wait()
            return carry

        lax.fori_loop(pend_ref[N_EXPERTS - 1] // blk, xs_hbm.shape[0] // blk, fill_tail, 0)

    @pl.when(i >= 2)
    def _():
        wait_rows(slot)

    tile = _pack_bf16_pairs(_select_part(i, tile_starts, hn_refs))
    for s in range(2):
        @pl.when(slot == s)
        def _():
            stage[s] = tile
            for k in range(2):
                for r in range(tm):
                    pltpu.make_async_copy(stage.at[s, pl.ds(r, 1), :],
                                          xs_hbm.at[pl.ds(dest_ref[0, 0, k * tm + r], 1), :],
                                          sem.at[s]).start(priority=r % 2)

    @pl.when(i == pl.num_programs(0) - 1)
    def _():
        wait_rows(slot)

        @pl.when(i >= 1)
        def _():
            wait_rows(1 - slot)


def _tile_layout(arrays, tm):
    counts = [a.shape[0] // tm for a in arrays]
    starts = [sum(counts[:p]) for p in range(len(counts))]
    return counts, starts


def _dispatch(pad_end, padded, dest_t, hns, rows, tm, blk):
    counts, starts = _tile_layout(hns, tm)
    grid_spec = pltpu.PrefetchScalarGridSpec(
        num_scalar_prefetch=2,
        grid=(sum(counts),),
        in_specs=[pl.BlockSpec((1, 1, 2 * tm), lambda i, pe, pd: (i, 0, 0),
                               memory_space=pltpu.SMEM)]
        + [_part_spec((tm, D_MODEL), s, n) for s, n in zip(starts, counts)],
        out_specs=pl.BlockSpec(memory_space=pl.ANY),
        scratch_shapes=[
            pltpu.VMEM((2, tm, D_MODEL // 2), jnp.uint32),
            pltpu.VMEM((blk, D_MODEL // 2), jnp.uint32),
            pltpu.SemaphoreType.DMA((3,)),
        ],
    )
    return pl.pallas_call(
        functools.partial(_dispatch_kernel, tile_starts=tuple(starts)),
        grid_spec=grid_spec,
        out_shape=jax.ShapeDtypeStruct((rows, D_MODEL // 2), jnp.uint32),
        compiler_params=pltpu.CompilerParams(
            dimension_semantics=("arbitrary",), vmem_limit_bytes=VMEM_LIMIT),
        name="dispatch",
    )(pad_end, padded, dest_t, *hns)


def _moe_kernel(be_ref, nu_ref, nxt_ref, xs_ref, wg_hbm, wu_hbm, wd_hbm, yb_ref,
                wg_f, wu_f, wd_f, wg_s, wu_s, wd_s, run_s, sem):
    i = pl.program_id(0)

    def fetch(e, slot):
        return [pltpu.make_async_copy(src.at[e], dst.at[slot], sem.at[slot])
                for src, dst in ((wg_hbm, wg_f), (wu_hbm, wu_f), (wd_hbm, wd_f))]

    @pl.when(i < nu_ref[0])
    def _():
        e = be_ref[i]

        @pl.when(i == 0)
        def _():
            run_s[0] = 0
            for c in fetch(e, 0):
                c.start()

        @pl.when((i == 0) | (e != be_ref[jnp.maximum(i - 1, 0)]))
        def _():
            slot = lax.rem(run_s[0], 2)
            run_s[0] = run_s[0] + 1
            for c in fetch(e, slot):
                c.wait()
            wg_s[...] = wg_f[slot].astype(BF16)
            wu_s[...] = wu_f[slot].astype(BF16)
            wd_s[...] = wd_f[slot].astype(BF16)

            @pl.when(nxt_ref[e] != e)
            def _():
                for c in fetch(nxt_ref[e], 1 - slot):
                    c.start()

        xe = _unpack_bf16_pairs(xs_ref[...])
        g = _mm(xe, wg_s[...])
        u = _mm(xe, wu_s[...])
        hmid = ((g * (1.0 / (1.0 + jnp.exp(-g)))) * u).astype(BF16)
        yb_ref[...] = _mm(hmid, wd_s[...])

    @pl.when(i >= nu_ref[0])
    def _():
        yb_ref[...] = jnp.zeros_like(yb_ref)


def _moe(block_e, n_used, next_e, xs, w_gate, w_up, w_down, blk):
    n_blocks = block_e.shape[0]
    in_blk = lambda i, be, nu, nx: (jnp.maximum(jnp.minimum(i, nu[0] - 1), 0), 0)
    grid_spec = pltpu.PrefetchScalarGridSpec(
        num_scalar_prefetch=3,
        grid=(n_blocks,),
        in_specs=[
            pl.BlockSpec((blk, D_MODEL // 2), in_blk),
            pl.BlockSpec(memory_space=pl.ANY),
            pl.BlockSpec(memory_space=pl.ANY),
            pl.BlockSpec(memory_space=pl.ANY),
        ],
        out_specs=pl.BlockSpec((blk, D_MODEL), lambda i, be, nu, nx: (i, 0)),
        scratch_shapes=[
            pltpu.VMEM((2, D_MODEL, D_EXPERT), F32),
            pltpu.VMEM((2, D_MODEL, D_EXPERT), F32),
            pltpu.VMEM((2, D_EXPERT, D_MODEL), F32),
            pltpu.VMEM((D_MODEL, D_EXPERT), BF16),
            pltpu.VMEM((D_MODEL, D_EXPERT), BF16),
            pltpu.VMEM((D_EXPERT, D_MODEL), BF16),
            pltpu.SMEM((1,), jnp.int32),
            pltpu.SemaphoreType.DMA((2,)),
        ],
    )
    return pl.pallas_call(
        _moe_kernel,
        grid_spec=grid_spec,
        out_shape=jax.ShapeDtypeStruct((xs.shape[0], D_MODEL), F32),
        compiler_params=pltpu.CompilerParams(
            dimension_semantics=("arbitrary",), vmem_limit_bytes=VMEM_LIMIT),
        name="moe",
    )(block_e, n_used, next_e, xs, w_gate, w_up, w_down)


def _combine_kernel(dest_ref, dest_next_ref, *rest, tile_starts):
    n_parts = len(tile_starts)
    x2_refs, rt_refs = rest[:n_parts], rest[n_parts:2 * n_parts]
    yb_hbm = rest[2 * n_parts]
    o_refs = rest[2 * n_parts + 1:3 * n_parts + 1]
    buf, sem = rest[3 * n_parts + 1:]
    tm = x2_refs[0].shape[0]
    i = pl.program_id(0)
    slot = lax.rem(i, 2)

    def gather(d_ref, s):
        for k in range(2):
            for r in range(tm):
                pltpu.make_async_copy(yb_hbm.at[pl.ds(d_ref[0, 0, k * tm + r], 1), :],
                                      buf.at[s, k, pl.ds(r, 1), :],
                                      sem.at[s]).start(priority=r % 2)

    @pl.when(i == 0)
    def _():
        gather(dest_ref, 0)

    for s in range(2):
        @pl.when((i + 1 < pl.num_programs(0)) & (slot == 1 - s))
        def _():
            gather(dest_next_ref, s)

    for k in range(2):
        pltpu.make_async_copy(yb_hbm.at[pl.ds(0, tm), :], buf.at[slot, k], sem.at[slot]).wait()
    rt = _select_part(i, tile_starts, rt_refs)
    out = (_select_part(i, tile_starts, x2_refs) + rt[:, 2:3] * buf[slot, 0]
           + rt[:, 3:4] * buf[slot, 1])
    ends = tile_starts[1:] + (pl.num_programs(0),)
    for start, end, o_ref in zip(tile_starts, ends, o_refs):
        @pl.when((i >= start) & (i < end))
        def _():
            o_ref[...] = out


def _combine(dest_t, x2s, rts, yb, tm):
    counts, starts = _tile_layout(x2s, tm)
    nt = sum(counts)
    spec = lambda w: [_part_spec((tm, w), s, n) for s, n in zip(starts, counts)]
    return pl.pallas_call(
        functools.partial(_combine_kernel, tile_starts=tuple(starts)),
        grid=(nt,),
        in_specs=[
            pl.BlockSpec((1, 1, 2 * tm), lambda i: (i, 0, 0), memory_space=pltpu.SMEM),
            pl.BlockSpec((1, 1, 2 * tm), lambda i: (jnp.minimum(i + 1, nt - 1), 0, 0),
                         memory_space=pltpu.SMEM),
        ] + spec(D_MODEL) + spec(LANES) + [pl.BlockSpec(memory_space=pl.ANY)],
        out_specs=spec(D_MODEL),
        out_shape=[jax.ShapeDtypeStruct(x2.shape, F32) for x2 in x2s],
        scratch_shapes=[
            pltpu.VMEM((2, 2, tm, D_MODEL), F32),
            pltpu.SemaphoreType.DMA((2,)),
        ],
        compiler_params=pltpu.CompilerParams(
            dimension_semantics=("arbitrary",), vmem_limit_bytes=VMEM_LIMIT),
        name="combine",
    )(dest_t, dest_t, *x2s, *rts, yb)


def _hier_moe(parts, cnt, w_gate, w_up, w_down, tm, blk):
    counts = cnt[0, ROUTER_COL0:ROUTER_COL0 + N_EXPERTS].astype(jnp.int32)
    padded = (counts + blk - 1) // blk * blk
    pad_end = jnp.cumsum(padded)
    pad_start = pad_end - padded
    t_all = sum(p[0].shape[0] for p in parts)
    n_blocks = (2 * t_all + N_EXPERTS * (blk - 1)) // blk + 1
    rows = n_blocks * blk
    blk_row0 = jnp.arange(n_blocks, dtype=jnp.int32) * blk
    block_e = jnp.minimum(
        jnp.sum((pad_end[None, :] <= blk_row0[:, None]).astype(jnp.int32), axis=1),
        N_EXPERTS - 1)
    n_used = (pad_end[-1] // blk).astype(jnp.int32).reshape(1)
    ids = jnp.arange(N_EXPERTS, dtype=jnp.int32)
    later = (ids[None, :] > ids[:, None]) & (padded[None, :] > 0)
    next_e = jnp.where(jnp.any(later, axis=1),
                       jnp.min(jnp.where(later, ids[None, :], N_EXPERTS), axis=1), ids)
    experts = jnp.arange(N_EXPERTS, dtype=jnp.int32)[:, None, None]

    dests = []
    for x2, _, _, rtt in parts:
        nt = x2.shape[0] // tm
        eid = rtt[0:2].astype(jnp.int32)
        rank = rtt[4:6].astype(jnp.int32)
        dest = rank + jnp.sum(
            jnp.where(eid[None] == experts, pad_start[:, None, None], 0), axis=0)
        dests.append(jnp.concatenate([dest[0].reshape(nt, tm), dest[1].reshape(nt, tm)],
                                     axis=1).reshape(nt, 1, 2 * tm))
    dest_t = jnp.concatenate(dests, axis=0)
    xs = _dispatch(pad_end, padded, dest_t, [p[1] for p in parts], rows, tm, blk)
    yb = _moe(block_e, n_used, next_e.astype(jnp.int32), xs, w_gate, w_up, w_down, blk)
    return _combine(dest_t, [p[0] for p in parts], [p[2] for p in parts], yb, tm)


def _rope_table(pos):
    half = ROPE_DIM // 2
    d = jnp.arange(LANES, dtype=jnp.int32) % HEAD_DIM
    inv = ROPE_THETA ** (-(2 * (d % half)).astype(F32) / ROPE_DIM)
    ang = pos.astype(F32)[:, None] * inv[None, :]
    cos, sin = jnp.cos(ang), jnp.sin(ang)
    rotary = (d < ROPE_DIM)[None, :]
    first = (d < half)[None, :]
    return jnp.concatenate([jnp.where(rotary, cos, 1.0),
                            jnp.where(first, -sin, 0.0),
                            jnp.where(rotary & ~first, sin, 0.0)], axis=1)


def _mixers(x, pos_rope, kctx_prev, vctx_prev, h0r, h0i, mk, mv, wp, sp, cnt0, *,
            tm_in, tq, ssm_l, tm_mid):
    B, S, _ = x.shape
    T = B * S
    q, k3, v3, u_tm = _in_proj(x, wp["gmix"], wp["win"], wp["gq"], wp["gk"], pos_rope, *tm_in)
    if kctx_prev is None:
        kctx, vctx = k3, v3
    else:
        kctx = jnp.concatenate([kctx_prev, k3], axis=1)
        vctx = jnp.concatenate([vctx_prev, v3], axis=1)
    att = _swa(wp["sink"], q, kctx, vctx, tq, mask_context=kctx_prev is None)
    ssm_tm, hr, hi = _ssm(u_tm.reshape(S, B, SSM_WIDTH), h0r, h0i, sp, ssm_l)
    x2, hn, rt, rtt, cnt = _mid(x, att, ssm_tm.reshape(S, B * SSM_WIDTH), mk, mv, wp, cnt0,
                                *tm_mid)
    part = (x2.reshape(T, D_MODEL), hn.reshape(T, D_MODEL), rt.reshape(T, LANES), rtt)
    return part, cnt, k3, v3, hr, hi


def kernel(x_prompt, x_sample, cache_attn_k, cache_attn_v, state_ssm_re, state_ssm_im, cache_mem_k, cache_mem_v, mem_prompt, norm_mix, w_in, q_norm, k_norm, attn_sink, ssm_lambda_re, ssm_lambda_im, ssm_log_dt, ssm_b_re, ssm_b_im, ssm_c_re, ssm_c_im, ssm_d, ssm_w_glu, ssm_b_glu, norm_attn_out, norm_ssm_out, w_out, norm_cross, norm_mem, w_cq, w_ck, w_cv, cq_norm, ck_norm, w_co, norm_ffn, w_router_group, b_router_group, w_router_expert, b_router_expert, w_e_gate, w_e_up, w_e_down):
    depth = norm_mix.shape[0]
    Bp, Lp, _ = x_prompt.shape
    Bs, Ls, _ = x_sample.shape
    yp, ys = x_prompt, x_sample
    rope_p = _rope_table(jnp.arange(Lp, dtype=jnp.int32))
    rope_s = _rope_table(PAST_LEN + jnp.arange(Ls, dtype=jnp.int32))
    outs = [[] for _ in range(10)]
    n_router = N_EXPERT_GROUPS + N_EXPERTS
    for l in range(depth):
        row = lambda a: a[l].astype(F32).reshape(1, -1)
        w_r = jnp.pad(jnp.concatenate([w_router_group[l], w_router_expert[l]], axis=1).astype(F32),
                      ((0, 0), (0, LANES - n_router)))
        w_r_hi = w_r.astype(BF16)
        w_r_lo = (w_r - w_r_hi.astype(F32)).astype(BF16)
        b_r = jnp.pad(jnp.concatenate([b_router_group[l], b_router_expert[l]]).astype(F32),
                      (0, LANES - n_router)).reshape(1, LANES)
        wp = {
            "gmix": row(norm_mix), "win": w_in[l].astype(BF16),
            "gq": jnp.tile(row(q_norm), (1, LANES // HEAD_DIM)),
            "gk": jnp.tile(row(k_norm), (1, LANES // HEAD_DIM)),
            "sink": attn_sink[l].astype(F32),
            "gao": row(norm_attn_out), "gso": row(norm_ssm_out),
            "wout": w_out[l].astype(BF16), "gx": row(norm_cross),
            "wcq": w_cq[l].astype(BF16), "gcq": row(cq_norm),
            "wco": w_co[l].astype(BF16), "gffn": row(norm_ffn),
            "wr": jnp.concatenate([w_r_hi, w_r_lo], axis=1), "br": b_r,
        }
        sp = _ssm_params(ssm_lambda_re[l], ssm_lambda_im[l], ssm_log_dt[l], ssm_b_re[l],
                         ssm_b_im[l], ssm_c_re[l], ssm_c_im[l], ssm_d[l], ssm_w_glu[l],
                         ssm_b_glu[l])
        ew = (w_e_gate[l].astype(F32), w_e_up[l].astype(F32), w_e_down[l].astype(F32))

        w_ckv = jnp.concatenate([w_ck[l], w_cv[l]], axis=1).astype(BF16)
        mkp, mvp = _memkv(mem_prompt.reshape(Bp * N_MEM, D_MODEL), row(norm_mem), w_ckv,
                          row(ck_norm), 512)
        mkp = mkp.reshape(Bp, N_MEM, CA_WIDTH)
        mvp = mvp.reshape(Bp, N_MEM, CA_WIDTH)

        zst = jnp.zeros((Bp, SSM_COLS), F32)
        part_p, cnt_p, kp, vp, hpr, hpi = _mixers(
            yp, rope_p, None, None, zst, zst, mkp, mvp, wp, sp, jnp.zeros((1, LANES), F32),
            tm_in=(1, 512), tq=256, ssm_l=64, tm_mid=(1, 512))
        part_s, cnt_s, kn, vn, hsr, hsi = _mixers(
            ys, rope_s, cache_attn_k[l].reshape(Bs, WINDOW, KV_WIDTH).astype(F32),
            cache_attn_v[l].reshape(Bs, WINDOW, KV_WIDTH).astype(F32),
            state_ssm_re[l].astype(F32).reshape(Bs, SSM_COLS),
            state_ssm_im[l].astype(F32).reshape(Bs, SSM_COLS),
            cache_mem_k[l].astype(F32).reshape(Bs, N_MEM, CA_WIDTH),
            cache_mem_v[l].astype(F32).reshape(Bs, N_MEM, CA_WIDTH), wp, sp, cnt_p,
            tm_in=(8, Ls), tq=CHUNK, ssm_l=Ls, tm_mid=(8, Ls))
        yp, ys = _hier_moe([part_p, part_s], cnt_s, *ew, 256, 2 * MOE_BLOCK)
        yp = yp.reshape(Bp, Lp, D_MODEL)
        ys = ys.reshape(Bs, Ls, D_MODEL)

        sg = (N_SSM_GROUPS, SSM_STATE)
        kvs = (N_KV_HEADS, HEAD_DIM)
        vals = (kp[:, Lp - WINDOW:].reshape(Bp, WINDOW, *kvs),
                vp[:, Lp - WINDOW:].reshape(Bp, WINDOW, *kvs),
                hpr.reshape(Bp, *sg), hpi.reshape(Bp, *sg),
                mkp.reshape(Bp, N_MEM, CA_HEADS, CA_HEAD_DIM),
                mvp.reshape(Bp, N_MEM, CA_HEADS, CA_HEAD_DIM),
                kn.reshape(Bs, Ls, *kvs), vn.reshape(Bs, Ls, *kvs),
                hsr.reshape(Bs, *sg), hsi.reshape(Bs, *sg))
        for lst, val in zip(outs, vals):
            lst.append(val)
    return (yp, ys) + tuple(jnp.stack(lst) for lst in outs)
```

```python
import functools
import math

import jax
import jax.numpy as jnp
from jax import lax
from jax.experimental import pallas as pl
from jax.experimental.pallas import tpu as pltpu

F32 = jnp.float32
BF16 = jnp.bfloat16

D_MODEL = 1024
CHUNK = 64
N_Q_HEADS = 8
N_KV_HEADS = 2
GQA = N_Q_HEADS // N_KV_HEADS
HEAD_DIM = 64
WINDOW = 128
BAND = WINDOW + CHUNK
ROPE_DIM = HEAD_DIM // 4
ROPE_THETA = 500000.0
ATT_WIDTH = N_Q_HEADS * HEAD_DIM
KV_WIDTH = N_KV_HEADS * HEAD_DIM
SSM_GROUP = 16
SSM_WIDTH = D_MODEL // 2
N_SSM_GROUPS = SSM_WIDTH // SSM_GROUP
SSM_STATE = 64
SSM_COLS = N_SSM_GROUPS * SSM_STATE
IN_WIDTH = ATT_WIDTH + 2 * KV_WIDTH + SSM_WIDTH
N_MEM = 256
CA_HEADS = 4
CA_HEAD_DIM = 128
CA_WIDTH = CA_HEADS * CA_HEAD_DIM
N_EXPERT_GROUPS = 4
EXPERTS_PER_GROUP = 8
N_EXPERTS = N_EXPERT_GROUPS * EXPERTS_PER_GROUP
D_EXPERT = 512
MOE_BLOCK = 256
EPS = 1e-6
NEG = -1e30
PAST_LEN = 4096

LANES = 128
ROUTER_COL0 = N_EXPERT_GROUPS
VMEM_LIMIT = 48 * 1024 * 1024


def _rms(x, g):
    ms = jnp.mean(x * x, axis=-1, keepdims=True)
    return (x * lax.rsqrt(ms + EPS)) * g


def _mm(a, b):
    return jnp.dot(a, b, preferred_element_type=F32)


_HI_HALF = 0xFFFF0000


def _pack_bf16_pairs(x):
    half = x.shape[1] // 2
    bits = lambda v: lax.bitcast_convert_type(v.astype(BF16).astype(F32), jnp.uint32)
    return (lax.shift_right_logical(bits(x[:, :half]), jnp.uint32(16))
            | (bits(x[:, half:]) & jnp.uint32(_HI_HALF)))


def _unpack_bf16_pairs(w):
    lo = lax.bitcast_convert_type(lax.shift_left(w, jnp.uint32(16)), F32)
    hi = lax.bitcast_convert_type(w & jnp.uint32(_HI_HALF), F32)
    return jnp.concatenate([lo.astype(BF16), hi.astype(BF16)], axis=1)


def _in_proj_kernel(x_ref, g_ref, w_ref, gq_ref, gk_ref, rope_ref,
                    q_ref, k_ref, v_ref, u_ref):
    nb, ts, _ = x_ref.shape
    tm = nb * ts
    h = _rms(x_ref[...].reshape(tm, D_MODEL), g_ref[...])
    hin = _mm(h.astype(BF16), w_ref[...])
    rope = jnp.concatenate([rope_ref[...]] * nb, axis=0)
    cos = rope[:, 0:LANES]
    sin_lo = rope[:, LANES:2 * LANES]
    sin_hi = rope[:, 2 * LANES:3 * LANES]
    lane = lax.broadcasted_iota(jnp.int32, (tm, LANES), 1)
    left = lane < HEAD_DIM

    def norm_rope(z, g):
        sq = z * z
        lsum = jnp.sum(jnp.where(left, sq, 0.0), axis=-1, keepdims=True)
        rsum = jnp.sum(jnp.where(left, 0.0, sq), axis=-1, keepdims=True)
        ms = jnp.where(left, lsum, rsum) * (1.0 / HEAD_DIM)
        zn = (z * lax.rsqrt(ms + EPS)) * g
        half = ROPE_DIM // 2
        return (zn * cos + pltpu.roll(zn, LANES - half, 1) * sin_lo
                + pltpu.roll(zn, half, 1) * sin_hi)

    for j in range(ATT_WIDTH // LANES):
        sl = slice(j * LANES, (j + 1) * LANES)
        q_ref[:, :, sl] = norm_rope(hin[:, sl], gq_ref[...]).reshape(nb, ts, LANES)
    k_ref[...] = norm_rope(hin[:, ATT_WIDTH:ATT_WIDTH + KV_WIDTH],
                           gk_ref[...]).reshape(nb, ts, KV_WIDTH)
    v_ref[...] = hin[:, ATT_WIDTH + KV_WIDTH:ATT_WIDTH + 2 * KV_WIDTH].reshape(nb, ts, KV_WIDTH)
    for b in range(nb):
        u_ref[:, b * SSM_WIDTH:(b + 1) * SSM_WIDTH] = (
            hin[b * ts:(b + 1) * ts, ATT_WIDTH + 2 * KV_WIDTH:])


def _in_proj(x, g, w_bf, gq, gk, rope, nb, ts):
    B, S, _ = x.shape
    full = lambda b, i: (0, 0)
    tile = lambda w: pl.BlockSpec((nb, ts, w), lambda b, i: (b, i, 0))
    return pl.pallas_call(
        _in_proj_kernel,
        grid=(B // nb, S // ts),
        in_specs=[
            tile(D_MODEL),
            pl.BlockSpec((1, D_MODEL), full),
            pl.BlockSpec((D_MODEL, IN_WIDTH), full),
            pl.BlockSpec((1, LANES), full),
            pl.BlockSpec((1, LANES), full),
            pl.BlockSpec((ts, 3 * LANES), lambda b, i: (i, 0)),
        ],
        out_specs=[
            tile(ATT_WIDTH), tile(KV_WIDTH), tile(KV_WIDTH),
            pl.BlockSpec((ts, nb * SSM_WIDTH), lambda b, i: (i, b)),
        ],
        out_shape=[
            jax.ShapeDtypeStruct((B, S, ATT_WIDTH), F32),
            jax.ShapeDtypeStruct((B, S, KV_WIDTH), F32),
            jax.ShapeDtypeStruct((B, S, KV_WIDTH), F32),
            jax.ShapeDtypeStruct((S, B * SSM_WIDTH), F32),
        ],
        compiler_params=pltpu.CompilerParams(
            dimension_semantics=("arbitrary", "arbitrary"),
            vmem_limit_bytes=VMEM_LIMIT),
        name="in_proj",
    )(x, g, w_bf, gq, gk, rope)


def _swa_kernel(sink_ref, q_ref, k_ref, v_ref, o_ref, *, mask_context):
    tq = q_ref.shape[1]
    i = pl.program_id(1)
    nch = tq // CHUNK
    lane = lax.broadcasted_iota(jnp.int32, (BAND, LANES), 1)
    lo_half = lane < HEAD_DIM
    vrow_lo = lax.broadcasted_iota(jnp.int32, (LANES, BAND), 0) < HEAD_DIM
    q_lo = lax.broadcasted_iota(jnp.int32, (1, LANES), 1) < CHUNK
    slabs_per_kv = GQA * HEAD_DIM // LANES

    units = []
    scores = []
    vpads = {}
    for c in range(nch):
        chunk = i * nch + c
        if mask_context:
            first = jnp.maximum(chunk - WINDOW // CHUNK, 0)
            start = pl.multiple_of(first * CHUNK, CHUNK)
            kidx = start + lax.broadcasted_iota(jnp.int32, (BAND, LANES), 0)
            valid = kidx < (chunk + 1) * CHUNK
        else:
            start = pl.multiple_of(chunk * CHUNK, CHUNK)
        kb = k_ref[0, pl.ds(start, BAND), :]
        kb_sw = pltpu.roll(kb, HEAD_DIM, 1)
        vt = v_ref[0, pl.ds(start, BAND), :].T
        vt_sw = jnp.concatenate([vt[HEAD_DIM:], vt[:HEAD_DIM]], axis=0)
        for kvh in range(N_KV_HEADS):
            k_own, k_oth = (kb, kb_sw) if kvh == 0 else (kb_sw, kb)
            v_own, v_oth = (vt, vt_sw) if kvh == 0 else (vt_sw, vt)
            kpad = (jnp.where(lo_half, k_own, 0.0).astype(BF16),
                    jnp.where(lo_half, 0.0, k_oth).astype(BF16))
            vpads[(c, kvh)] = (jnp.where(vrow_lo, v_own, 0.0).astype(BF16),
                               jnp.where(vrow_lo, 0.0, v_oth).astype(BF16))
            col0 = kvh * GQA * HEAD_DIM
            q2 = jnp.concatenate(
                [q_ref[0, c * CHUNK:(c + 1) * CHUNK, col0 + m * LANES:col0 + (m + 1) * LANES]
                 for m in range(slabs_per_kv)], axis=0).astype(BF16)
            for side in range(2):
                s = lax.dot_general(kpad[side], q2, (((1,), (1,)), ((), ())),
                                    preferred_element_type=F32) * (HEAD_DIM ** -0.5)
                if mask_context:
                    s = jnp.where(valid, s, NEG)
                units.append((c, kvh, side))
                scores.append(s)

    sinks = [jnp.where(q_lo, sink_ref[kvh * GQA + side], sink_ref[kvh * GQA + 2 + side])
             for (_, kvh, side) in units]
    maxes = [jnp.maximum(jnp.max(s, axis=0, keepdims=True), sk)
             for s, sk in zip(scores, sinks)]
    exps = [jnp.exp(s - mx) for s, mx in zip(scores, maxes)]
    dens = [jnp.sum(p, axis=0, keepdims=True) + jnp.exp(sk - mx)
            for p, sk, mx in zip(exps, sinks, maxes)]
    probs = [(p * (1.0 / den)).astype(BF16) for p, den in zip(exps, dens)]

    for n in range(0, len(units), 2):
        c, kvh, _ = units[n]
        vp = vpads[(c, kvh)]
        o = (_mm(vp[0], probs[n]) + _mm(vp[1], probs[n + 1])).T
        col0 = kvh * GQA * HEAD_DIM
        for m in range(slabs_per_kv):
            o_ref[0, c * CHUNK:(c + 1) * CHUNK, col0 + m * LANES:col0 + (m + 1) * LANES] = (
                o[m * CHUNK:(m + 1) * CHUNK])


def _swa(sink, q, kctx, vctx, tq, mask_context):
    B, Sq, _ = q.shape
    Sk = kctx.shape[1]
    return pl.pallas_call(
        functools.partial(_swa_kernel, mask_context=mask_context),
        grid=(B, Sq // tq),
        in_specs=[
            pl.BlockSpec(memory_space=pltpu.SMEM),
            pl.BlockSpec((1, tq, ATT_WIDTH), lambda b, i: (b, i, 0)),
            pl.BlockSpec((1, Sk, KV_WIDTH), lambda b, i: (b, 0, 0)),
            pl.BlockSpec((1, Sk, KV_WIDTH), lambda b, i: (b, 0, 0)),
        ],
        out_specs=pl.BlockSpec((1, tq, ATT_WIDTH), lambda b, i: (b, i, 0)),
        out_shape=jax.ShapeDtypeStruct((B, Sq, ATT_WIDTH), F32),
        compiler_params=pltpu.CompilerParams(
            dimension_semantics=("arbitrary", "arbitrary"),
            vmem_limit_bytes=VMEM_LIMIT),
        name="swa",
    )(sink, q, kctx, vctx)


def _ssm_kernel(u_ref, h0r_ref, h0i_ref, lam_ref, bre_ref, bim_ref, cre_ref, cim_ref,
                d_ref, wglu_ref, bglu_ref,
                y_ref, hr_out, hi_out, sr0, si0, sr1, si1, hr_s, hi_s):
    L, B, _ = u_ref.shape
    rows = L * B
    half_w = SSM_WIDTH // 2
    half_c = SSM_COLS // 2
    halves = ((sr0, si0), (sr1, si1))

    @pl.when(pl.program_id(0) == 0)
    def _():
        hr_s[...] = h0r_ref[...]
        hi_s[...] = h0i_ref[...]

    u = u_ref[...].reshape(rows, SSM_WIDTH)
    ub = u.astype(BF16)

    def project_in(hf):
        sr, si = halves[hf]
        uh = ub[:, hf * half_w:(hf + 1) * half_w]
        sr[...] = _mm(uh, bre_ref[hf])
        si[...] = _mm(uh, bim_ref[hf])

    def recur(hf):
        sr, si = halves[hf]
        cw = 4 * LANES
        for cc in range(half_c // cw):
            cols = slice(cc * cw, (cc + 1) * cw)
            gcols = slice(hf * half_c + cc * cw, hf * half_c + (cc + 1) * cw)
            lr = jnp.broadcast_to(lam_ref[0:1, gcols], (B, cw))
            li = jnp.broadcast_to(lam_ref[1:2, gcols], (B, cw))
            hr, hi = hr_s[:, gcols], hi_s[:, gcols]
            for t in range(L):
                at_t = slice(t * B, (t + 1) * B)
                hr, hi = (lr * hr - li * hi + sr[at_t, cols],
                          lr * hi + li * hr + si[at_t, cols])
                sr[at_t, cols] = hr
                si[at_t, cols] = hi
            hr_s[:, gcols] = hr
            hi_s[:, gcols] = hi

    def project_out(hf):
        sr, si = halves[hf]
        return (_mm(sr[...].astype(BF16), cre_ref[hf]) + _mm(si[...].astype(BF16), cim_ref[hf]))

    project_in(0)
    project_in(1)
    recur(0)
    y0 = project_out(0)
    recur(1)
    y1 = project_out(1)
    y = jnp.concatenate([y0, y1], axis=1) + d_ref[...] * u
    g = 0.5 * y * (1.0 + jnp.tanh(math.sqrt(2.0 / math.pi) * (y + 0.044715 * (y * y * y))))
    gb = g.astype(BF16)
    z = jnp.concatenate(
        [_mm(gb[:, hf * half_w:(hf + 1) * half_w], wglu_ref[hf]) for hf in range(2)],
        axis=1) + bglu_ref[...]
    out = g * (1.0 / (1.0 + jnp.exp(-z)))
    y_ref[...] = out.reshape(L, B, SSM_WIDTH)
    hr_out[...] = hr_s[...]
    hi_out[...] = hi_s[...]


def _ssm(u, h0r, h0i, sp, L):
    S, B, _ = u.shape
    c2 = lambda i: (0, 0)
    c3 = lambda i: (0, 0, 0)
    return pl.pallas_call(
        _ssm_kernel,
        grid=(S // L,),
        in_specs=[
            pl.BlockSpec((L, B, SSM_WIDTH), lambda i: (i, 0, 0)),
            pl.BlockSpec((B, SSM_COLS), c2),
            pl.BlockSpec((B, SSM_COLS), c2),
            pl.BlockSpec((2, SSM_COLS), c2),
            pl.BlockSpec((2, SSM_WIDTH // 2, SSM_COLS // 2), c3),
            pl.BlockSpec((2, SSM_WIDTH // 2, SSM_COLS // 2), c3),
            pl.BlockSpec((2, SSM_COLS // 2, SSM_WIDTH // 2), c3),
            pl.BlockSpec((2, SSM_COLS // 2, SSM_WIDTH // 2), c3),
            pl.BlockSpec((1, SSM_WIDTH), c2),
            pl.BlockSpec((2, SSM_WIDTH // 2, SSM_WIDTH // 2), c3),
            pl.BlockSpec((1, SSM_WIDTH), c2),
        ],
        out_specs=[
            pl.BlockSpec((L, B, SSM_WIDTH), lambda i: (i, 0, 0)),
            pl.BlockSpec((B, SSM_COLS), c2),
            pl.BlockSpec((B, SSM_COLS), c2),
        ],
        out_shape=[
            jax.ShapeDtypeStruct((S, B, SSM_WIDTH), F32),
            jax.ShapeDtypeStruct((B, SSM_COLS), F32),
            jax.ShapeDtypeStruct((B, SSM_COLS), F32),
        ],
        scratch_shapes=[
            pltpu.VMEM((L * B, SSM_COLS // 2), F32),
            pltpu.VMEM((L * B, SSM_COLS // 2), F32),
            pltpu.VMEM((L * B, SSM_COLS // 2), F32),
            pltpu.VMEM((L * B, SSM_COLS // 2), F32),
            pltpu.VMEM((B, SSM_COLS), F32),
            pltpu.VMEM((B, SSM_COLS), F32),
        ],
        compiler_params=pltpu.CompilerParams(
            dimension_semantics=("arbitrary",), vmem_limit_bytes=VMEM_LIMIT),
        name="ssm",
    )(u, h0r, h0i, sp["lam"], sp["bre"], sp["bim"], sp["cre"], sp["cim"],
      sp["d"], sp["wglu"], sp["bglu"])


def _block_diag(blocks):
    G, r, c = blocks.shape
    col = jnp.arange(G * c, dtype=jnp.int32)
    spread = (col[None, :] % c == jnp.arange(c, dtype=jnp.int32)[:, None]).astype(F32)
    same_group = (jnp.arange(G * r, dtype=jnp.int32)[:, None] // r) == (col[None, :] // c)
    tiled = jnp.dot(blocks.reshape(G * r, c), spread, precision=lax.Precision.HIGHEST)
    return jnp.where(same_group, tiled, 0.0)


def _ssm_params(lam_re, lam_im, log_dt, b_re, b_im, c_re, c_im, d, w_glu, b_glu):
    lam = lax.complex(lam_re.astype(F32), lam_im.astype(F32))
    dt = jnp.exp(log_dt.astype(F32))[:, None]
    lam_bar = jnp.exp(lam * dt)
    bmat = lax.complex(b_re.astype(F32), b_im.astype(F32))
    b_bar = ((lam_bar - 1.0) / lam)[..., None] * bmat
    lam2 = jnp.stack([lam_bar.real.reshape(-1), lam_bar.imag.reshape(-1)])
    hw, hc = SSM_WIDTH // 2, SSM_COLS // 2
    split_b = lambda m: jnp.stack([m[:hw, :hc], m[hw:, hc:]]).astype(BF16)
    split_c = lambda m: jnp.stack([m[:hc, :hw], m[hc:, hw:]]).astype(BF16)
    wg = _block_diag(w_glu.astype(F32))
    return {
        "lam": lam2,
        "bre": split_b(_block_diag(b_bar.real).T),
        "bim": split_b(_block_diag(b_bar.imag).T),
        "cre": split_c(_block_diag(c_re.astype(F32)).T),
        "cim": split_c(_block_diag(-c_im.astype(F32)).T),
        "d": d.astype(F32).reshape(1, SSM_WIDTH),
        "wglu": jnp.stack([wg[:hw, :hw], wg[hw:, hw:]]).astype(BF16),
        "bglu": b_glu.astype(F32).reshape(1, SSM_WIDTH),
    }


def _memkv_kernel(m_ref, g_ref, w_ref, gk_ref, k_ref, v_ref):
    m = _rms(m_ref[...], g_ref[...])
    kv = _mm(m.astype(BF16), w_ref[...])
    for h in range(CA_HEADS):
        sl = slice(h * CA_HEAD_DIM, (h + 1) * CA_HEAD_DIM)
        k_ref[:, sl] = _rms(kv[:, sl], gk_ref[...])
    v_ref[...] = kv[:, CA_WIDTH:]


def _memkv(mem2d, g, w_bf, gk, tm):
    T = mem2d.shape[0]
    full = lambda i: (0, 0)
    return pl.pallas_call(
        _memkv_kernel,
        grid=(T // tm,),
        in_specs=[
            pl.BlockSpec((tm, D_MODEL), lambda i: (i, 0)),
            pl.BlockSpec((1, D_MODEL), full),
            pl.BlockSpec((D_MODEL, 2 * CA_WIDTH), full),
            pl.BlockSpec((1, CA_HEAD_DIM), full),
        ],
        out_specs=[
            pl.BlockSpec((tm, CA_WIDTH), lambda i: (i, 0)),
            pl.BlockSpec((tm, CA_WIDTH), lambda i: (i, 0)),
        ],
        out_shape=[
            jax.ShapeDtypeStruct((T, CA_WIDTH), F32),
            jax.ShapeDtypeStruct((T, CA_WIDTH), F32),
        ],
        compiler_params=pltpu.CompilerParams(
            dimension_semantics=("arbitrary",), vmem_limit_bytes=VMEM_LIMIT),
        name="memkv",
    )(mem2d, g, w_bf, gk)


def _mid_kernel(x_ref, att_ref, ssm_ref, mk_ref, mv_ref,
                gao_ref, gso_ref, wout_ref, gx_ref, wcq_ref, gcq_ref, wco_ref,
                gffn_ref, wr_ref, br_ref, cnt0_ref, tri_ref,
                x2_ref, hn_ref, rt_ref, rtt_ref, cnt_ref, base_s):
    nb, ts, _ = x_ref.shape
    tm = nb * ts

    @pl.when((pl.program_id(0) == 0) & (pl.program_id(1) == 0))
    def _():
        base_s[...] = cnt0_ref[...]

    ssm = jnp.concatenate(
        [ssm_ref[:, b * SSM_WIDTH:(b + 1) * SSM_WIDTH] for b in range(nb)], axis=0)
    a = _rms(att_ref[...].reshape(tm, ATT_WIDTH), gao_ref[...]).astype(BF16)
    s = _rms(ssm, gso_ref[...]).astype(BF16)
    x1 = (x_ref[...].reshape(tm, D_MODEL) + _mm(a, wout_ref[0:ATT_WIDTH, :])
          + _mm(s, wout_ref[ATT_WIDTH:, :]))

    qx = _mm(_rms(x1, gx_ref[...]).astype(BF16), wcq_ref[...])
    heads = []
    for h in range(CA_HEADS):
        sl = slice(h * CA_HEAD_DIM, (h + 1) * CA_HEAD_DIM)
        qh = _rms(qx[:, sl], gcq_ref[...]).astype(BF16)
        per_batch = []
        for b in range(nb):
            kh = mk_ref[b, :, sl].astype(BF16)
            qb = qh[b * ts:(b + 1) * ts]
            if ts <= LANES:
                vt = mv_ref[b, :, sl].T.astype(BF16)
                sc = lax.dot_general(kh, qb, (((1,), (1,)), ((), ())),
                                     preferred_element_type=F32) * (CA_HEAD_DIM ** -0.5)
                p = jnp.exp(sc - jnp.max(sc, axis=0, keepdims=True))
                p = p * (1.0 / jnp.sum(p, axis=0, keepdims=True))
                per_batch.append(_mm(vt, p.astype(BF16)).T)
            else:
                vh = mv_ref[b, :, sl].astype(BF16)
                sc = lax.dot_general(qb, kh, (((1,), (1,)), ((), ())),
                                     preferred_element_type=F32) * (CA_HEAD_DIM ** -0.5)
                p = jnp.exp(sc - jnp.max(sc, axis=-1, keepdims=True))
                p = p / jnp.sum(p, axis=-1, keepdims=True)
                per_batch.append(_mm(p.astype(BF16), vh))
        heads.append(jnp.concatenate(per_batch, axis=0))
    o = jnp.concatenate(heads, axis=1).astype(BF16)
    x2 = x1 + _mm(o, wco_ref[...])
    x2_ref[...] = x2.reshape(nb, ts, D_MODEL)

    hn = _rms(x2, gffn_ref[...])
    hn_ref[...] = hn.reshape(nb, ts, D_MODEL)

    h_hi = hn.astype(BF16)
    h_lo = (hn - h_hi.astype(F32)).astype(BF16)
    r1 = _mm(h_hi, wr_ref[...])
    lg = (r1[:, :LANES] + r1[:, LANES:] + _mm(h_lo, wr_ref[:, 0:LANES])
          + br_ref[...])

    col = lax.broadcasted_iota(jnp.int32, (tm, LANES), 1)
    big = jnp.int32(4 * LANES)
    gmask = col < N_EXPERT_GROUPS
    lgg = jnp.where(gmask, lg, NEG)
    mg = jnp.max(lgg, axis=-1, keepdims=True)
    grp = jnp.min(jnp.where(gmask & (lgg == mg), col, big), axis=-1, keepdims=True)
    pg_top = 1.0 / jnp.sum(jnp.where(gmask, jnp.exp(lgg - mg), 0.0), axis=-1, keepdims=True)

    ecol = col - ROUTER_COL0
    emask = ((ecol >= 0) & (ecol < N_EXPERTS)
             & (lax.shift_right_arithmetic(ecol, 3) == grp))
    le = jnp.where(emask, lg, NEG)
    m1 = jnp.max(le, axis=-1, keepdims=True)
    i1 = jnp.min(jnp.where(emask & (le == m1), col, big), axis=-1, keepdims=True)
    rest = emask & (col != i1)
    le2 = jnp.where(rest, lg, NEG)
    m2 = jnp.max(le2, axis=-1, keepdims=True)
    i2 = jnp.min(jnp.where(rest & (le2 == m2), col, big), axis=-1, keepdims=True)
    den = jnp.sum(jnp.where(emask, jnp.exp(le - m1), 0.0), axis=-1, keepdims=True)
    p1 = 1.0 / den
    p2 = jnp.exp(m2 - m1) / den
    gate1 = pg_top * p1 / (p1 + p2)
    gate2 = pg_top * p2 / (p1 + p2)

    sel1 = col == i1
    sel2 = col == i2
    oh = jnp.where(sel1 | sel2, 1.0, 0.0)
    tot = base_s[...] + _mm(tri_ref[...], oh.astype(BF16))
    rank1 = jnp.sum(jnp.where(sel1, tot, 0.0), axis=-1, keepdims=True)
    rank2 = jnp.sum(jnp.where(sel2, tot, 0.0), axis=-1, keepdims=True)
    base_s[...] = base_s[...] + jnp.sum(oh, axis=0, keepdims=True)
    cnt_ref[...] = base_s[...]

    e1 = (i1 - ROUTER_COL0).astype(F32)
    e2 = (i2 - ROUTER_COL0).astype(F32)
    rt = jnp.zeros((tm, LANES), F32)
    for k, val in enumerate((e1, e2, gate1, gate2, rank1, rank2)):
        rt = jnp.where(col == k, val, rt)
    rt_ref[...] = rt.reshape(nb, ts, LANES)
    rtt_ref[...] = rt.T[0:8, :]


def _mid(x, att, ssm_tm, mk, mv, wp, cnt0, nb, ts):
    B, S, _ = x.shape
    assert nb == 1 or ts == S
    c2 = lambda b, i: (0, 0)
    tile = lambda w: pl.BlockSpec((nb, ts, w), lambda b, i: (b, i, 0))
    return pl.pallas_call(
        _mid_kernel,
        grid=(B // nb, S // ts),
        in_specs=[
            tile(D_MODEL), tile(ATT_WIDTH),
            pl.BlockSpec((ts, nb * SSM_WIDTH), lambda b, i: (i, b)),
            pl.BlockSpec((nb, N_MEM, CA_WIDTH), lambda b, i: (b, 0, 0)),
            pl.BlockSpec((nb, N_MEM, CA_WIDTH), lambda b, i: (b, 0, 0)),
            pl.BlockSpec((1, ATT_WIDTH), c2),
            pl.BlockSpec((1, SSM_WIDTH), c2),
            pl.BlockSpec((ATT_WIDTH + SSM_WIDTH, D_MODEL), c2),
            pl.BlockSpec((1, D_MODEL), c2),
            pl.BlockSpec((D_MODEL, CA_WIDTH), c2),
            pl.BlockSpec((1, CA_HEAD_DIM), c2),
            pl.BlockSpec((CA_WIDTH, D_MODEL), c2),
            pl.BlockSpec((1, D_MODEL), c2),
            pl.BlockSpec((D_MODEL, 2 * LANES), c2),
            pl.BlockSpec((1, LANES), c2),
            pl.BlockSpec((1, LANES), c2),
            pl.BlockSpec((nb * ts, nb * ts), c2),
        ],
        out_specs=[
            tile(D_MODEL), tile(D_MODEL), tile(LANES),
            pl.BlockSpec((8, nb * ts), lambda b, i: (0, b * (S // ts) + i)),
            pl.BlockSpec((1, LANES), c2),
        ],
        out_shape=[
            jax.ShapeDtypeStruct((B, S, D_MODEL), F32),
            jax.ShapeDtypeStruct((B, S, D_MODEL), F32),
            jax.ShapeDtypeStruct((B, S, LANES), F32),
            jax.ShapeDtypeStruct((8, B * S), F32),
            jax.ShapeDtypeStruct((1, LANES), F32),
        ],
        scratch_shapes=[pltpu.VMEM((1, LANES), F32)],
        compiler_params=pltpu.CompilerParams(
            dimension_semantics=("arbitrary", "arbitrary"),
            vmem_limit_bytes=VMEM_LIMIT),
        name="mid",
    )(x, att, ssm_tm, mk, mv, wp["gao"], wp["gso"], wp["wout"], wp["gx"], wp["wcq"],
      wp["gcq"], wp["wco"], wp["gffn"], wp["wr"], wp["br"], cnt0,
      jnp.tri(nb * ts, k=-1, dtype=BF16))


def _select_part(i, tile_starts, refs):
    x = refs[0][...]
    for start, ref in zip(tile_starts[1:], refs[1:]):
        x = jnp.where(i >= start, ref[...], x)
    return x


def _part_spec(shape, tile_start, n_tiles):
    def index(i, *_):
        return (jnp.clip(i - tile_start, 0, n_tiles - 1),) + (0,) * (len(shape) - 1)
    return pl.BlockSpec(shape, index)


def _dispatch_kernel(pend_ref, padded_ref, dest_ref, *rest, tile_starts):
    n_parts = len(tile_starts)
    hn_refs = rest[:n_parts]
    xs_hbm, stage, zbuf, sem = rest[n_parts:]
    tm = hn_refs[0].shape[0]
    i = pl.program_id(0)
    slot = lax.rem(i, 2)
    blk = zbuf.shape[0]

    def wait_rows(s):
        for _ in range(2):
            pltpu.make_async_copy(stage.at[s], xs_hbm.at[pl.ds(0, tm)], sem.at[s]).wait()

    @pl.when(i == 0)
    def _():
        zbuf[...] = jnp.zeros_like(zbuf)
        for e in range(N_EXPERTS):
            @pl.when(padded_ref[e] > 0)
            def _():
                row0 = pl.multiple_of(pend_ref[e] - blk, blk)
                fill = pltpu.make_async_copy(zbuf, xs_hbm.at[pl.ds(row0, blk)], sem.at[2])
                fill.start()
                fill.wait()

        def fill_tail(b, carry):
            fill = pltpu.make_async_copy(
                zbuf, xs_hbm.at[pl.ds(pl.multiple_of(b * blk, blk), blk)], sem.at[2])
            fill.start()
            fill.wait()
            return carry

        lax.fori_loop(pend_ref[N_EXPERTS - 1] // blk, xs_hbm.shape[0] // blk, fill_tail, 0)

    @pl.when(i >= 2)
    def _():
        wait_rows(slot)

    tile = _pack_bf16_pairs(_select_part(i, tile_starts, hn_refs))
    for s in range(2):
        @pl.when(slot == s)
        def _():
            stage[s] = tile
            for k in range(2):
                for r in range(tm):
                    pltpu.make_async_copy(stage.at[s, pl.ds(r, 1), :],
                                          xs_hbm.at[pl.ds(dest_ref[0, 0, k * tm + r], 1), :],
                                          sem.at[s]).start(priority=r % 2)

    @pl.when(i == pl.num_programs(0) - 1)
    def _():
        wait_rows(slot)

        @pl.when(i >= 1)
        def _():
            wait_rows(1 - slot)


def _tile_layout(arrays, tm):
    counts = [a.shape[0] // tm for a in arrays]
    starts = [sum(counts[:p]) for p in range(len(counts))]
    return counts, starts


def _dispatch(pad_end, padded, dest_t, hns, rows, tm, blk):
    counts, starts = _tile_layout(hns, tm)
    grid_spec = pltpu.PrefetchScalarGridSpec(
        num_scalar_prefetch=2,
        grid=(sum(counts),),
        in_specs=[pl.BlockSpec((1, 1, 2 * tm), lambda i, pe, pd: (i, 0, 0),
                               memory_space=pltpu.SMEM)]
        + [_part_spec((tm, D_MODEL), s, n) for s, n in zip(starts, counts)],
        out_specs=pl.BlockSpec(memory_space=pl.ANY),
        scratch_shapes=[
            pltpu.VMEM((2, tm, D_MODEL // 2), jnp.uint32),
            pltpu.VMEM((blk, D_MODEL // 2), jnp.uint32),
            pltpu.SemaphoreType.DMA((3,)),
        ],
    )
    return pl.pallas_call(
        functools.partial(_dispatch_kernel, tile_starts=tuple(starts)),
        grid_spec=grid_spec,
        out_shape=jax.ShapeDtypeStruct((rows, D_MODEL // 2), jnp.uint32),
        compiler_params=pltpu.CompilerParams(
            dimension_semantics=("arbitrary",), vmem_limit_bytes=VMEM_LIMIT),
        name="dispatch",
    )(pad_end, padded, dest_t, *hns)


def _moe_kernel(be_ref, nu_ref, nxt_ref, xs_ref, wg_hbm, wu_hbm, wd_hbm, yb_ref,
                wg_f, wu_f, wd_f, wg_s, wu_s, wd_s, run_s, sem):
    i = pl.program_id(0)

    def fetch(e, slot):
        return [pltpu.make_async_copy(src.at[e], dst.at[slot], sem.at[slot])
                for src, dst in ((wg_hbm, wg_f), (wu_hbm, wu_f), (wd_hbm, wd_f))]

    @pl.when(i < nu_ref[0])
    def _():
        e = be_ref[i]

        @pl.when(i == 0)
        def _():
            run_s[0] = 0
            for c in fetch(e, 0):
                c.start()

        @pl.when((i == 0) | (e != be_ref[jnp.maximum(i - 1, 0)]))
        def _():
            slot = lax.rem(run_s[0], 2)
            run_s[0] = run_s[0] + 1
            for c in fetch(e, slot):
                c.wait()
            wg_s[...] = wg_f[slot].astype(BF16)
            wu_s[...] = wu_f[slot].astype(BF16)
            wd_s[...] = wd_f[slot].astype(BF16)

            @pl.when(nxt_ref[e] != e)
            def _():
                for c in fetch(nxt_ref[e], 1 - slot):
                    c.start()

        xe = _unpack_bf16_pairs(xs_ref[...])
        g = _mm(xe, wg_s[...])
        u = _mm(xe, wu_s[...])
        hmid = ((g * (1.0 / (1.0 + jnp.exp(-g)))) * u).astype(BF16)
        yb_ref[...] = _mm(hmid, wd_s[...])

    @pl.when(i >= nu_ref[0])
    def _():
        yb_ref[...] = jnp.zeros_like(yb_ref)


def _moe(block_e, n_used, next_e, xs, w_gate, w_up, w_down, blk):
    n_blocks = block_e.shape[0]
    in_blk = lambda i, be, nu, nx: (jnp.maximum(jnp.minimum(i, nu[0] - 1), 0), 0)
    grid_spec = pltpu.PrefetchScalarGridSpec(
        num_scalar_prefetch=3,
        grid=(n_blocks,),
        in_specs=[
            pl.BlockSpec((blk, D_MODEL // 2), in_blk),
            pl.BlockSpec(memory_space=pl.ANY),
            pl.BlockSpec(memory_space=pl.ANY),
            pl.BlockSpec(memory_space=pl.ANY),
        ],
        out_specs=pl.BlockSpec((blk, D_MODEL), lambda i, be, nu, nx: (i, 0)),
        scratch_shapes=[
            pltpu.VMEM((2, D_MODEL, D_EXPERT), F32),
            pltpu.VMEM((2, D_MODEL, D_EXPERT), F32),
            pltpu.VMEM((2, D_EXPERT, D_MODEL), F32),
            pltpu.VMEM((D_MODEL, D_EXPERT), BF16),
            pltpu.VMEM((D_MODEL, D_EXPERT), BF16),
            pltpu.VMEM((D_EXPERT, D_MODEL), BF16),
            pltpu.SMEM((1,), jnp.int32),
            pltpu.SemaphoreType.DMA((2,)),
        ],
    )
    return pl.pallas_call(
        _moe_kernel,
        grid_spec=grid_spec,
        out_shape=jax.ShapeDtypeStruct((xs.shape[0], D_MODEL), F32),
        compiler_params=pltpu.CompilerParams(
            dimension_semantics=("arbitrary",), vmem_limit_bytes=VMEM_LIMIT),
        name="moe",
    )(block_e, n_used, next_e, xs, w_gate, w_up, w_down)


def _combine_kernel(dest_ref, dest_next_ref, *rest, tile_starts):
    n_parts = len(tile_starts)
    x2_refs, rt_refs = rest[:n_parts], rest[n_parts:2 * n_parts]
    yb_hbm = rest[2 * n_parts]
    o_refs = rest[2 * n_parts + 1:3 * n_parts + 1]
    buf, sem = rest[3 * n_parts + 1:]
    tm = x2_refs[0].shape[0]
    i = pl.program_id(0)
    slot = lax.rem(i, 2)

    def gather(d_ref, s):
        for k in range(2):
            for r in range(tm):
                pltpu.make_async_copy(yb_hbm.at[pl.ds(d_ref[0, 0, k * tm + r], 1), :],
                                      buf.at[s, k, pl.ds(r, 1), :],
                                      sem.at[s]).start(priority=r % 2)

    @pl.when(i == 0)
    def _():
        gather(dest_ref, 0)

    for s in range(2):
        @pl.when((i + 1 < pl.num_programs(0)) & (slot == 1 - s))
        def _():
            gather(dest_next_ref, s)

    for k in range(2):
        pltpu.make_async_copy(yb_hbm.at[pl.ds(0, tm), :], buf.at[slot, k], sem.at[slot]).wait()
    rt = _select_part(i, tile_starts, rt_refs)
    out = (_select_part(i, tile_starts, x2_refs) + rt[:, 2:3] * buf[slot, 0]
           + rt[:, 3:4] * buf[slot, 1])
    ends = tile_starts[1:] + (pl.num_programs(0),)
    for start, end, o_ref in zip(tile_starts, ends, o_refs):
        @pl.when((i >= start) & (i < end))
        def _():
            o_ref[...] = out


def _combine(dest_t, x2s, rts, yb, tm):
    counts, starts = _tile_layout(x2s, tm)
    nt = sum(counts)
    spec = lambda w: [_part_spec((tm, w), s, n) for s, n in zip(starts, counts)]
    return pl.pallas_call(
        functools.partial(_combine_kernel, tile_starts=tuple(starts)),
        grid=(nt,),
        in_specs=[
            pl.BlockSpec((1, 1, 2 * tm), lambda i: (i, 0, 0), memory_space=pltpu.SMEM),
            pl.BlockSpec((1, 1, 2 * tm), lambda i: (jnp.minimum(i + 1, nt - 1), 0, 0),
                         memory_space=pltpu.SMEM),
        ] + spec(D_MODEL) + spec(LANES) + [pl.BlockSpec(memory_space=pl.ANY)],
        out_specs=spec(D_MODEL),
        out_shape=[jax.ShapeDtypeStruct(x2.shape, F32) for x2 in x2s],
        scratch_shapes=[
            pltpu.VMEM((2, 2, tm, D_MODEL), F32),
            pltpu.SemaphoreType.DMA((2,)),
        ],
        compiler_params=pltpu.CompilerParams(
            dimension_semantics=("arbitrary",), vmem_limit_bytes=VMEM_LIMIT),
        name="combine",
    )(dest_t, dest_t, *x2s, *rts, yb)


def _hier_moe(parts, cnt, w_gate, w_up, w_down, tm, blk):
    counts = cnt[0, ROUTER_COL0:ROUTER_COL0 + N_EXPERTS].astype(jnp.int32)
    padded = (counts + blk - 1) // blk * blk
    pad_end = jnp.cumsum(padded)
    pad_start = pad_end - padded
    t_all = sum(p[0].shape[0] for p in parts)
    n_blocks = (2 * t_all + N_EXPERTS * (blk - 1)) // blk + 1
    rows = n_blocks * blk
    blk_row0 = jnp.arange(n_blocks, dtype=jnp.int32) * blk
    block_e = jnp.minimum(
        jnp.sum((pad_end[None, :] <= blk_row0[:, None]).astype(jnp.int32), axis=1),
        N_EXPERTS - 1)
    n_used = (pad_end[-1] // blk).astype(jnp.int32).reshape(1)
    ids = jnp.arange(N_EXPERTS, dtype=jnp.int32)
    later = (ids[None, :] > ids[:, None]) & (padded[None, :] > 0)
    next_e = jnp.where(jnp.any(later, axis=1),
                       jnp.min(jnp.where(later, ids[None, :], N_EXPERTS), axis=1), ids)
    experts = jnp.arange(N_EXPERTS, dtype=jnp.int32)[:, None, None]

    dests = []
    for x2, _, _, rtt in parts:
        nt = x2.shape[0] // tm
        eid = rtt[0:2].astype(jnp.int32)
        rank = rtt[4:6].astype(jnp.int32)
        dest = rank + jnp.sum(
            jnp.where(eid[None] == experts, pad_start[:, None, None], 0), axis=0)
        dests.append(jnp.concatenate([dest[0].reshape(nt, tm), dest[1].reshape(nt, tm)],
                                     axis=1).reshape(nt, 1, 2 * tm))
    dest_t = jnp.concatenate(dests, axis=0)
    xs = _dispatch(pad_end, padded, dest_t, [p[1] for p in parts], rows, tm, blk)
    yb = _moe(block_e, n_used, next_e.astype(jnp.int32), xs, w_gate, w_up, w_down, blk)
    return _combine(dest_t, [p[0] for p in parts], [p[2] for p in parts], yb, tm)


def _rope_table(pos):
    half = ROPE_DIM // 2
    d = jnp.arange(LANES, dtype=jnp.int32) % HEAD_DIM
    inv = ROPE_THETA ** (-(2 * (d % half)).astype(F32) / ROPE_DIM)
    ang = pos.astype(F32)[:, None] * inv[None, :]
    cos, sin = jnp.cos(ang), jnp.sin(ang)
    rotary = (d < ROPE_DIM)[None, :]
    first = (d < half)[None, :]
    return jnp.concatenate([jnp.where(rotary, cos, 1.0),
                            jnp.where(first, -sin, 0.0),
                            jnp.where(rotary & ~first, sin, 0.0)], axis=1)


def _mixers(x, pos_rope, kctx_prev, vctx_prev, h0r, h0i, mk, mv, wp, sp, cnt0, *,
            tm_in, tq, ssm_l, tm_mid):
    B, S, _ = x.shape
    T = B * S
    q, k3, v3, u_tm = _in_proj(x, wp["gmix"], wp["win"], wp["gq"], wp["gk"], pos_rope, *tm_in)
    if kctx_prev is None:
        kctx, vctx = k3, v3
    else:
        kctx = jnp.concatenate([kctx_prev, k3], axis=1)
        vctx = jnp.concatenate([vctx_prev, v3], axis=1)
    att = _swa(wp["sink"], q, kctx, vctx, tq, mask_context=kctx_prev is None)
    ssm_tm, hr, hi = _ssm(u_tm.reshape(S, B, SSM_WIDTH), h0r, h0i, sp, ssm_l)
    x2, hn, rt, rtt, cnt = _mid(x, att, ssm_tm.reshape(S, B * SSM_WIDTH), mk, mv, wp, cnt0,
                                *tm_mid)
    part = (x2.reshape(T, D_MODEL), hn.reshape(T, D_MODEL), rt.reshape(T, LANES), rtt)
    return part, cnt, k3, v3, hr, hi


def kernel(x_prompt, x_sample, cache_attn_k, cache_attn_v, state_ssm_re, state_ssm_im, cache_mem_k, cache_mem_v, mem_prompt, norm_mix, w_in, q_norm, k_norm, attn_sink, ssm_lambda_re, ssm_lambda_im, ssm_log_dt, ssm_b_re, ssm_b_im, ssm_c_re, ssm_c_im, ssm_d, ssm_w_glu, ssm_b_glu, norm_attn_out, norm_ssm_out, w_out, norm_cross, norm_mem, w_cq, w_ck, w_cv, cq_norm, ck_norm, w_co, norm_ffn, w_router_group, b_router_group, w_router_expert, b_router_expert, w_e_gate, w_e_up, w_e_down):
    depth = norm_mix.shape[0]
    Bp, Lp, _ = x_prompt.shape
    Bs, Ls, _ = x_sample.shape
    yp, ys = x_prompt, x_sample
    rope_p = _rope_table(jnp.arange(Lp, dtype=jnp.int32))
    rope_s = _rope_table(PAST_LEN + jnp.arange(Ls, dtype=jnp.int32))
    outs = [[] for _ in range(10)]
    n_router = N_EXPERT_GROUPS + N_EXPERTS
    for l in range(depth):
        row = lambda a: a[l].astype(F32).reshape(1, -1)
        w_r = jnp.pad(jnp.concatenate([w_router_group[l], w_router_expert[l]], axis=1).astype(F32),
                      ((0, 0), (0, LANES - n_router)))
        w_r_hi = w_r.astype(BF16)
        w_r_lo = (w_r - w_r_hi.astype(F32)).astype(BF16)
        b_r = jnp.pad(jnp.concatenate([b_router_group[l], b_router_expert[l]]).astype(F32),
                      (0, LANES - n_router)).reshape(1, LANES)
        wp = {
            "gmix": row(norm_mix), "win": w_in[l].astype(BF16),
            "gq": jnp.tile(row(q_norm), (1, LANES // HEAD_DIM)),
            "gk": jnp.tile(row(k_norm), (1, LANES // HEAD_DIM)),
            "sink": attn_sink[l].astype(F32),
            "gao": row(norm_attn_out), "gso": row(norm_ssm_out),
            "wout": w_out[l].astype(BF16), "gx": row(norm_cross),
            "wcq": w_cq[l].astype(BF16), "gcq": row(cq_norm),
            "wco": w_co[l].astype(BF16), "gffn": row(norm_ffn),
            "wr": jnp.concatenate([w_r_hi, w_r_lo], axis=1), "br": b_r,
        }
        sp = _ssm_params(ssm_lambda_re[l], ssm_lambda_im[l], ssm_log_dt[l], ssm_b_re[l],
                         ssm_b_im[l], ssm_c_re[l], ssm_c_im[l], ssm_d[l], ssm_w_glu[l],
                         ssm_b_glu[l])
        ew = (w_e_gate[l].astype(F32), w_e_up[l].astype(F32), w_e_down[l].astype(F32))

        w_ckv = jnp.concatenate([w_ck[l], w_cv[l]], axis=1).astype(BF16)
        mkp, mvp = _memkv(mem_prompt.reshape(Bp * N_MEM, D_MODEL), row(norm_mem), w_ckv,
                          row(ck_norm), 512)
        mkp = mkp.reshape(Bp, N_MEM, CA_WIDTH)
        mvp = mvp.reshape(Bp, N_MEM, CA_WIDTH)

        zst = jnp.zeros((Bp, SSM_COLS), F32)
        part_p, cnt_p, kp, vp, hpr, hpi = _mixers(
            yp, rope_p, None, None, zst, zst, mkp, mvp, wp, sp, jnp.zeros((1, LANES), F32),
            tm_in=(1, 512), tq=1024, ssm_l=64, tm_mid=(1, 512))
        part_s, cnt_s, kn, vn, hsr, hsi = _mixers(
            ys, rope_s, cache_attn_k[l].reshape(Bs, WINDOW, KV_WIDTH).astype(F32),
            cache_attn_v[l].reshape(Bs, WINDOW, KV_WIDTH).astype(F32),
            state_ssm_re[l].astype(F32).reshape(Bs, SSM_COLS),
            state_ssm_im[l].astype(F32).reshape(Bs, SSM_COLS),
            cache_mem_k[l].astype(F32).reshape(Bs, N_MEM, CA_WIDTH),
            cache_mem_v[l].astype(F32).reshape(Bs, N_MEM, CA_WIDTH), wp, sp, cnt_p,
            tm_in=(8, Ls), tq=CHUNK, ssm_l=Ls, tm_mid=(8, Ls))
        yp, ys = _hier_moe([part_p, part_s], cnt_s, *ew, 256, 2 * MOE_BLOCK)
        yp = yp.reshape(Bp, Lp, D_MODEL)
        ys = ys.reshape(Bs, Ls, D_MODEL)

        sg = (N_SSM_GROUPS, SSM_STATE)
        kvs = (N_KV_HEADS, HEAD_DIM)
        vals = (kp[:, Lp - WINDOW:].reshape(Bp, WINDOW, *kvs),
                vp[:, Lp - WINDOW:].reshape(Bp, WINDOW, *kvs),
                hpr.reshape(Bp, *sg), hpi.reshape(Bp, *sg),
                mkp.reshape(Bp, N_MEM, CA_HEADS, CA_HEAD_DIM),
                mvp.reshape(Bp, N_MEM, CA_HEADS, CA_HEAD_DIM),
                kn.reshape(Bs, Ls, *kvs), vn.reshape(Bs, Ls, *kvs),
                hsr.reshape(Bs, *sg), hsi.reshape(Bs, *sg))
        for lst, val in zip(outs, vals):
            lst.append(val)
    return (yp, ys) + tuple(jnp.stack(lst) for lst in outs)
```

```python
import functools
import math

import jax
import jax.numpy as jnp
from jax import lax
from jax.experimental import pallas as pl
from jax.experimental.pallas import tpu as pltpu

F32 = jnp.float32
BF16 = jnp.bfloat16

D_MODEL = 1024
CHUNK = 64
N_Q_HEADS = 8
N_KV_HEADS = 2
GQA = N_Q_HEADS // N_KV_HEADS
HEAD_DIM = 64
WINDOW = 128
BAND = WINDOW + CHUNK
ROPE_DIM = HEAD_DIM // 4
ROPE_THETA = 500000.0
ATT_WIDTH = N_Q_HEADS * HEAD_DIM
KV_WIDTH = N_KV_HEADS * HEAD_DIM
SSM_GROUP = 16
SSM_WIDTH = D_MODEL // 2
N_SSM_GROUPS = SSM_WIDTH // SSM_GROUP
SSM_STATE = 64
SSM_COLS = N_SSM_GROUPS * SSM_STATE
IN_WIDTH = ATT_WIDTH + 2 * KV_WIDTH + SSM_WIDTH
N_MEM = 256
CA_HEADS = 4
CA_HEAD_DIM = 128
CA_WIDTH = CA_HEADS * CA_HEAD_DIM
N_EXPERT_GROUPS = 4
EXPERTS_PER_GROUP = 8
N_EXPERTS = N_EXPERT_GROUPS * EXPERTS_PER_GROUP
D_EXPERT = 512
MOE_BLOCK = 256
EPS = 1e-6
NEG = -1e30
PAST_LEN = 4096

LANES = 128
ROUTER_COL0 = N_EXPERT_GROUPS
VMEM_LIMIT = 48 * 1024 * 1024


def _rms(x, g):
    ms = jnp.mean(x * x, axis=-1, keepdims=True)
    return (x * lax.rsqrt(ms + EPS)) * g


def _mm(a, b):
    return jnp.dot(a, b, preferred_element_type=F32)


_HI_HALF = 0xFFFF0000


def _pack_bf16_pairs(x):
    half = x.shape[1] // 2
    bits = lambda v: lax.bitcast_convert_type(v.astype(BF16).astype(F32), jnp.uint32)
    return (lax.shift_right_logical(bits(x[:, :half]), jnp.uint32(16))
            | (bits(x[:, half:]) & jnp.uint32(_HI_HALF)))


def _unpack_bf16_pairs(w):
    lo = lax.bitcast_convert_type(lax.shift_left(w, jnp.uint32(16)), F32)
    hi = lax.bitcast_convert_type(w & jnp.uint32(_HI_HALF), F32)
    return jnp.concatenate([lo.astype(BF16), hi.astype(BF16)], axis=1)


def _in_proj_kernel(x_ref, g_ref, w_ref, gq_ref, gk_ref, rope_ref,
                    q_ref, k_ref, v_ref, u_ref):
    nb, ts, _ = x_ref.shape
    tm = nb * ts
    h = _rms(x_ref[...].reshape(tm, D_MODEL), g_ref[...])
    hin = _mm(h.astype(BF16), w_ref[...])
    rope = jnp.concatenate([rope_ref[...]] * nb, axis=0)
    cos = rope[:, 0:LANES]
    sin_lo = rope[:, LANES:2 * LANES]
    sin_hi = rope[:, 2 * LANES:3 * LANES]
    lane = lax.broadcasted_iota(jnp.int32, (tm, LANES), 1)
    left = lane < HEAD_DIM

    def norm_rope(z, g):
        sq = z * z
        lsum = jnp.sum(jnp.where(left, sq, 0.0), axis=-1, keepdims=True)
        rsum = jnp.sum(jnp.where(left, 0.0, sq), axis=-1, keepdims=True)
        ms = jnp.where(left, lsum, rsum) * (1.0 / HEAD_DIM)
        zn = (z * lax.rsqrt(ms + EPS)) * g
        half = ROPE_DIM // 2
        return (zn * cos + pltpu.roll(zn, LANES - half, 1) * sin_lo
                + pltpu.roll(zn, half, 1) * sin_hi)

    for j in range(ATT_WIDTH // LANES):
        sl = slice(j * LANES, (j + 1) * LANES)
        q_ref[:, :, sl] = norm_rope(hin[:, sl], gq_ref[...]).reshape(nb, ts, LANES)
    k_ref[...] = norm_rope(hin[:, ATT_WIDTH:ATT_WIDTH + KV_WIDTH],
                           gk_ref[...]).reshape(nb, ts, KV_WIDTH)
    v_ref[...] = hin[:, ATT_WIDTH + KV_WIDTH:ATT_WIDTH + 2 * KV_WIDTH].reshape(nb, ts, KV_WIDTH)
    for b in range(nb):
        u_ref[:, b * SSM_WIDTH:(b + 1) * SSM_WIDTH] = (
            hin[b * ts:(b + 1) * ts, ATT_WIDTH + 2 * KV_WIDTH:])


def _in_proj(x, g, w_bf, gq, gk, rope, nb, ts):
    B, S, _ = x.shape
    full = lambda b, i: (0, 0)
    tile = lambda w: pl.BlockSpec((nb, ts, w), lambda b, i: (b, i, 0))
    return pl.pallas_call(
        _in_proj_kernel,
        grid=(B // nb, S // ts),
        in_specs=[
            tile(D_MODEL),
            pl.BlockSpec((1, D_MODEL), full),
            pl.BlockSpec((D_MODEL, IN_WIDTH), full),
            pl.BlockSpec((1, LANES), full),
            pl.BlockSpec((1, LANES), full),
            pl.BlockSpec((ts, 3 * LANES), lambda b, i: (i, 0)),
        ],
        out_specs=[
            tile(ATT_WIDTH), tile(KV_WIDTH), tile(KV_WIDTH),
            pl.BlockSpec((ts, nb * SSM_WIDTH), lambda b, i: (i, b)),
        ],
        out_shape=[
            jax.ShapeDtypeStruct((B, S, ATT_WIDTH), F32),
            jax.ShapeDtypeStruct((B, S, KV_WIDTH), F32),
            jax.ShapeDtypeStruct((B, S, KV_WIDTH), F32),
            jax.ShapeDtypeStruct((S, B * SSM_WIDTH), F32),
        ],
        compiler_params=pltpu.CompilerParams(
            dimension_semantics=("arbitrary", "arbitrary"),
            vmem_limit_bytes=VMEM_LIMIT),
        name="in_proj",
    )(x, g, w_bf, gq, gk, rope)


def _swa_kernel(sink_ref, q_ref, k_ref, v_ref, o_ref, *, mask_context):
    tq = q_ref.shape[1]
    i = pl.program_id(1)
    nch = tq // CHUNK
    lane = lax.broadcasted_iota(jnp.int32, (BAND, LANES), 1)
    lo_half = lane < HEAD_DIM
    vrow_lo = lax.broadcasted_iota(jnp.int32, (LANES, BAND), 0) < HEAD_DIM
    q_lo = lax.broadcasted_iota(jnp.int32, (1, LANES), 1) < CHUNK
    slabs_per_kv = GQA * HEAD_DIM // LANES

    units = []
    scores = []
    vpads = {}
    for c in range(nch):
        chunk = i * nch + c
        if mask_context:
            first = jnp.maximum(chunk - WINDOW // CHUNK, 0)
            start = pl.multiple_of(first * CHUNK, CHUNK)
            kidx = start + lax.broadcasted_iota(jnp.int32, (BAND, LANES), 0)
            valid = kidx < (chunk + 1) * CHUNK
        else:
            start = pl.multiple_of(chunk * CHUNK, CHUNK)
        kb = k_ref[0, pl.ds(start, BAND), :]
        kb_sw = pltpu.roll(kb, HEAD_DIM, 1)
        vt = v_ref[0, pl.ds(start, BAND), :].T
        vt_sw = jnp.concatenate([vt[HEAD_DIM:], vt[:HEAD_DIM]], axis=0)
        for kvh in range(N_KV_HEADS):
            k_own, k_oth = (kb, kb_sw) if kvh == 0 else (kb_sw, kb)
            v_own, v_oth = (vt, vt_sw) if kvh == 0 else (vt_sw, vt)
            kpad = (jnp.where(lo_half, k_own, 0.0).astype(BF16),
                    jnp.where(lo_half, 0.0, k_oth).astype(BF16))
            vpads[(c, kvh)] = (jnp.where(vrow_lo, v_own, 0.0).astype(BF16),
                               jnp.where(vrow_lo, 0.0, v_oth).astype(BF16))
            col0 = kvh * GQA * HEAD_DIM
            q2 = jnp.concatenate(
                [q_ref[0, c * CHUNK:(c + 1) * CHUNK, col0 + m * LANES:col0 + (m + 1) * LANES]
                 for m in range(slabs_per_kv)], axis=0).astype(BF16)
            for side in range(2):
                s = lax.dot_general(kpad[side], q2, (((1,), (1,)), ((), ())),
                                    preferred_element_type=F32) * (HEAD_DIM ** -0.5)
                if mask_context:
                    s = jnp.where(valid, s, NEG)
                units.append((c, kvh, side))
                scores.append(s)

    sinks = [jnp.where(q_lo, sink_ref[kvh * GQA + side], sink_ref[kvh * GQA + 2 + side])
             for (_, kvh, side) in units]
    maxes = [jnp.maximum(jnp.max(s, axis=0, keepdims=True), sk)
             for s, sk in zip(scores, sinks)]
    exps = [jnp.exp(s - mx) for s, mx in zip(scores, maxes)]
    dens = [jnp.sum(p, axis=0, keepdims=True) + jnp.exp(sk - mx)
            for p, sk, mx in zip(exps, sinks, maxes)]
    probs = [(p * (1.0 / den)).astype(BF16) for p, den in zip(exps, dens)]

    for n in range(0, len(units), 2):
        c, kvh, _ = units[n]
        vp = vpads[(c, kvh)]
        o = (_mm(vp[0], probs[n]) + _mm(vp[1], probs[n + 1])).T
        col0 = kvh * GQA * HEAD_DIM
        for m in range(slabs_per_kv):
            o_ref[0, c * CHUNK:(c + 1) * CHUNK, col0 + m * LANES:col0 + (m + 1) * LANES] = (
                o[m * CHUNK:(m + 1) * CHUNK])


def _swa(sink, q, kctx, vctx, tq, mask_context):
    B, Sq, _ = q.shape
    Sk = kctx.shape[1]
    return pl.pallas_call(
        functools.partial(_swa_kernel, mask_context=mask_context),
        grid=(B, Sq // tq),
        in_specs=[
            pl.BlockSpec(memory_space=pltpu.SMEM),
            pl.BlockSpec((1, tq, ATT_WIDTH), lambda b, i: (b, i, 0)),
            pl.BlockSpec((1, Sk, KV_WIDTH), lambda b, i: (b, 0, 0)),
            pl.BlockSpec((1, Sk, KV_WIDTH), lambda b, i: (b, 0, 0)),
        ],
        out_specs=pl.BlockSpec((1, tq, ATT_WIDTH), lambda b, i: (b, i, 0)),
        out_shape=jax.ShapeDtypeStruct((B, Sq, ATT_WIDTH), F32),
        compiler_params=pltpu.CompilerParams(
            dimension_semantics=("arbitrary", "arbitrary"),
            vmem_limit_bytes=VMEM_LIMIT),
        name="swa",
    )(sink, q, kctx, vctx)


def _ssm_kernel(u_ref, h0r_ref, h0i_ref, lam_ref, bre_ref, bim_ref, cre_ref, cim_ref,
                d_ref, wglu_ref, bglu_ref,
                y_ref, hr_out, hi_out, sr0, si0, sr1, si1, hr_s, hi_s):
    L, B, _ = u_ref.shape
    rows = L * B
    half_w = SSM_WIDTH // 2
    half_c = SSM_COLS // 2
    halves = ((sr0, si0), (sr1, si1))

    @pl.when(pl.program_id(0) == 0)
    def _():
        hr_s[...] = h0r_ref[...]
        hi_s[...] = h0i_ref[...]

    u = u_ref[...].reshape(rows, SSM_WIDTH)
    ub = u.astype(BF16)

    def project_in(hf):
        sr, si = halves[hf]
        uh = ub[:, hf * half_w:(hf + 1) * half_w]
        sr[...] = _mm(uh, bre_ref[hf])
        si[...] = _mm(uh, bim_ref[hf])

    def recur(hf):
        sr, si = halves[hf]
        cw = 4 * LANES
        for cc in range(half_c // cw):
            cols = slice(cc * cw, (cc + 1) * cw)
            gcols = slice(hf * half_c + cc * cw, hf * half_c + (cc + 1) * cw)
            lr = jnp.broadcast_to(lam_ref[0:1, gcols], (B, cw))
            li = jnp.broadcast_to(lam_ref[1:2, gcols], (B, cw))
            hr, hi = hr_s[:, gcols], hi_s[:, gcols]
            for t in range(L):
                at_t = slice(t * B, (t + 1) * B)
                hr, hi = (lr * hr - li * hi + sr[at_t, cols],
                          lr * hi + li * hr + si[at_t, cols])
                sr[at_t, cols] = hr
                si[at_t, cols] = hi
            hr_s[:, gcols] = hr
            hi_s[:, gcols] = hi

    def project_out(hf):
        sr, si = halves[hf]
        return (_mm(sr[...].astype(BF16), cre_ref[hf]) + _mm(si[...].astype(BF16), cim_ref[hf]))

    project_in(0)
    project_in(1)
    recur(0)
    y0 = project_out(0)
    recur(1)
    y1 = project_out(1)
    y = jnp.concatenate([y0, y1], axis=1) + d_ref[...] * u
    g = 0.5 * y * (1.0 + jnp.tanh(math.sqrt(2.0 / math.pi) * (y + 0.044715 * (y * y * y))))
    gb = g.astype(BF16)
    z = jnp.concatenate(
        [_mm(gb[:, hf * half_w:(hf + 1) * half_w], wglu_ref[hf]) for hf in range(2)],
        axis=1) + bglu_ref[...]
    out = g * (1.0 / (1.0 + jnp.exp(-z)))
    y_ref[...] = out.reshape(L, B, SSM_WIDTH)
    hr_out[...] = hr_s[...]
    hi_out[...] = hi_s[...]


def _ssm(u, h0r, h0i, sp, L):
    S, B, _ = u.shape
    c2 = lambda i: (0, 0)
    c3 = lambda i: (0, 0, 0)
    return pl.pallas_call(
        _ssm_kernel,
        grid=(S // L,),
        in_specs=[
            pl.BlockSpec((L, B, SSM_WIDTH), lambda i: (i, 0, 0)),
            pl.BlockSpec((B, SSM_COLS), c2),
            pl.BlockSpec((B, SSM_COLS), c2),
            pl.BlockSpec((2, SSM_COLS), c2),
            pl.BlockSpec((2, SSM_WIDTH // 2, SSM_COLS // 2), c3),
            pl.BlockSpec((2, SSM_WIDTH // 2, SSM_COLS // 2), c3),
            pl.BlockSpec((2, SSM_COLS // 2, SSM_WIDTH // 2), c3),
            pl.BlockSpec((2, SSM_COLS // 2, SSM_WIDTH // 2), c3),
            pl.BlockSpec((1, SSM_WIDTH), c2),
            pl.BlockSpec((2, SSM_WIDTH // 2, SSM_WIDTH // 2), c3),
            pl.BlockSpec((1, SSM_WIDTH), c2),
        ],
        out_specs=[
            pl.BlockSpec((L, B, SSM_WIDTH), lambda i: (i, 0, 0)),
            pl.BlockSpec((B, SSM_COLS), c2),
            pl.BlockSpec((B, SSM_COLS), c2),
        ],
        out_shape=[
            jax.ShapeDtypeStruct((S, B, SSM_WIDTH), F32),
            jax.ShapeDtypeStruct((B, SSM_COLS), F32),
            jax.ShapeDtypeStruct((B, SSM_COLS), F32),
        ],
        scratch_shapes=[
            pltpu.VMEM((L * B, SSM_COLS // 2), F32),
            pltpu.VMEM((L * B, SSM_COLS // 2), F32),
            pltpu.VMEM((L * B, SSM_COLS // 2), F32),
            pltpu.VMEM((L * B, SSM_COLS // 2), F32),
            pltpu.VMEM((B, SSM_COLS), F32),
            pltpu.VMEM((B, SSM_COLS), F32),
        ],
        compiler_params=pltpu.CompilerParams(
            dimension_semantics=("arbitrary",), vmem_limit_bytes=VMEM_LIMIT),
        name="ssm",
    )(u, h0r, h0i, sp["lam"], sp["bre"], sp["bim"], sp["cre"], sp["cim"],
      sp["d"], sp["wglu"], sp["bglu"])


def _block_diag(blocks):
    G, r, c = blocks.shape
    col = jnp.arange(G * c, dtype=jnp.int32)
    spread = (col[None, :] % c == jnp.arange(c, dtype=jnp.int32)[:, None]).astype(F32)
    same_group = (jnp.arange(G * r, dtype=jnp.int32)[:, None] // r) == (col[None, :] // c)
    tiled = jnp.dot(blocks.reshape(G * r, c), spread, precision=lax.Precision.HIGHEST)
    return jnp.where(same_group, tiled, 0.0)


def _ssm_params(lam_re, lam_im, log_dt, b_re, b_im, c_re, c_im, d, w_glu, b_glu):
    lam = lax.complex(lam_re.astype(F32), lam_im.astype(F32))
    dt = jnp.exp(log_dt.astype(F32))[:, None]
    lam_bar = jnp.exp(lam * dt)
    bmat = lax.complex(b_re.astype(F32), b_im.astype(F32))
    b_bar = ((lam_bar - 1.0) / lam)[..., None] * bmat
    lam2 = jnp.stack([lam_bar.real.reshape(-1), lam_bar.imag.reshape(-1)])
    hw, hc = SSM_WIDTH // 2, SSM_COLS // 2
    split_b = lambda m: jnp.stack([m[:hw, :hc], m[hw:, hc:]]).astype(BF16)
    split_c = lambda m: jnp.stack([m[:hc, :hw], m[hc:, hw:]]).astype(BF16)
    wg = _block_diag(w_glu.astype(F32))
    return {
        "lam": lam2,
        "bre": split_b(_block_diag(b_bar.real).T),
        "bim": split_b(_block_diag(b_bar.imag).T),
        "cre": split_c(_block_diag(c_re.astype(F32)).T),
        "cim": split_c(_block_diag(-c_im.astype(F32)).T),
        "d": d.astype(F32).reshape(1, SSM_WIDTH),
        "wglu": jnp.stack([wg[:hw, :hw], wg[hw:, hw:]]).astype(BF16),
        "bglu": b_glu.astype(F32).reshape(1, SSM_WIDTH),
    }


def _memkv_kernel(m_ref, g_ref, w_ref, gk_ref, k_ref, v_ref):
    m = _rms(m_ref[...], g_ref[...])
    kv = _mm(m.astype(BF16), w_ref[...])
    for h in range(CA_HEADS):
        sl = slice(h * CA_HEAD_DIM, (h + 1) * CA_HEAD_DIM)
        k_ref[:, sl] = _rms(kv[:, sl], gk_ref[...])
    v_ref[...] = kv[:, CA_WIDTH:]


def _memkv(mem2d, g, w_bf, gk, tm):
    T = mem2d.shape[0]
    full = lambda i: (0, 0)
    return pl.pallas_call(
        _memkv_kernel,
        grid=(T // tm,),
        in_specs=[
            pl.BlockSpec((tm, D_MODEL), lambda i: (i, 0)),
            pl.BlockSpec((1, D_MODEL), full),
            pl.BlockSpec((D_MODEL, 2 * CA_WIDTH), full),
            pl.BlockSpec((1, CA_HEAD_DIM), full),
        ],
        out_specs=[
            pl.BlockSpec((tm, CA_WIDTH), lambda i: (i, 0)),
            pl.BlockSpec((tm, CA_WIDTH), lambda i: (i, 0)),
        ],
        out_shape=[
            jax.ShapeDtypeStruct((T, CA_WIDTH), F32),
            jax.ShapeDtypeStruct((T, CA_WIDTH), F32),
        ],
        compiler_params=pltpu.CompilerParams(
            dimension_semantics=("arbitrary",), vmem_limit_bytes=VMEM_LIMIT),
        name="memkv",
    )(mem2d, g, w_bf, gk)


def _mid_kernel(x_ref, att_ref, ssm_ref, mk_ref, mv_ref,
                gao_ref, gso_ref, wout_ref, gx_ref, wcq_ref, gcq_ref, wco_ref,
                gffn_ref, wr_ref, br_ref, cnt0_ref, tri_ref,
                x2_ref, hn_ref, rt_ref, rtt_ref, cnt_ref, base_s):
    nb, ts, _ = x_ref.shape
    tm = nb * ts

    @pl.when((pl.program_id(0) == 0) & (pl.program_id(1) == 0))
    def _():
        base_s[...] = cnt0_ref[...]

    ssm = jnp.concatenate(
        [ssm_ref[:, b * SSM_WIDTH:(b + 1) * SSM_WIDTH] for b in range(nb)], axis=0)
    a = _rms(att_ref[...].reshape(tm, ATT_WIDTH), gao_ref[...]).astype(BF16)
    s = _rms(ssm, gso_ref[...]).astype(BF16)
    x1 = (x_ref[...].reshape(tm, D_MODEL) + _mm(a, wout_ref[0:ATT_WIDTH, :])
          + _mm(s, wout_ref[ATT_WIDTH:, :]))

    qx = _mm(_rms(x1, gx_ref[...]).astype(BF16), wcq_ref[...])
    heads = []
    for h in range(CA_HEADS):
        sl = slice(h * CA_HEAD_DIM, (h + 1) * CA_HEAD_DIM)
        qh = _rms(qx[:, sl], gcq_ref[...]).astype(BF16)
        per_batch = []
        for b in range(nb):
            kh = mk_ref[b, :, sl].astype(BF16)
            qb = qh[b * ts:(b + 1) * ts]
            if ts <= LANES:
                vt = mv_ref[b, :, sl].T.astype(BF16)
                sc = lax.dot_general(kh, qb, (((1,), (1,)), ((), ())),
                                     preferred_element_type=F32) * (CA_HEAD_DIM ** -0.5)
                p = jnp.exp(sc - jnp.max(sc, axis=0, keepdims=True))
                p = p * (1.0 / jnp.sum(p, axis=0, keepdims=True))
                per_batch.append(_mm(vt, p.astype(BF16)).T)
            else:
                vh = mv_ref[b, :, sl].astype(BF16)
                sc = lax.dot_general(qb, kh, (((1,), (1,)), ((), ())),
                                     preferred_element_type=F32) * (CA_HEAD_DIM ** -0.5)
                p = jnp.exp(sc - jnp.max(sc, axis=-1, keepdims=True))
                p = p / jnp.sum(p, axis=-1, keepdims=True)
                per_batch.append(_mm(p.astype(BF16), vh))
        heads.append(jnp.concatenate(per_batch, axis=0))
    o = jnp.concatenate(heads, axis=1).astype(BF16)
    x2 = x1 + _mm(o, wco_ref[...])
    x2_ref[...] = x2.reshape(nb, ts, D_MODEL)

    hn = _rms(x2, gffn_ref[...])
    hn_ref[...] = hn.reshape(nb, ts, D_MODEL)

    h_hi = hn.astype(BF16)
    h_lo = (hn - h_hi.astype(F32)).astype(BF16)
    r1 = _mm(h_hi, wr_ref[...])
    lg = (r1[:, :LANES] + r1[:, LANES:] + _mm(h_lo, wr_ref[:, 0:LANES])
          + br_ref[...])

    col = lax.broadcasted_iota(jnp.int32, (tm, LANES), 1)
    big = jnp.int32(4 * LANES)
    gmask = col < N_EXPERT_GROUPS
    lgg = jnp.where(gmask, lg, NEG)
    mg = jnp.max(lgg, axis=-1, keepdims=True)
    grp = jnp.min(jnp.where(gmask & (lgg == mg), col, big), axis=-1, keepdims=True)
    pg_top = 1.0 / jnp.sum(jnp.where(gmask, jnp.exp(lgg - mg), 0.0), axis=-1, keepdims=True)

    ecol = col - ROUTER_COL0
    emask = ((ecol >= 0) & (ecol < N_EXPERTS)
             & (lax.shift_right_arithmetic(ecol, 3) == grp))
    le = jnp.where(emask, lg, NEG)
    m1 = jnp.max(le, axis=-1, keepdims=True)
    i1 = jnp.min(jnp.where(emask & (le == m1), col, big), axis=-1, keepdims=True)
    rest = emask & (col != i1)
    le2 = jnp.where(rest, lg, NEG)
    m2 = jnp.max(le2, axis=-1, keepdims=True)
    i2 = jnp.min(jnp.where(rest & (le2 == m2), col, big), axis=-1, keepdims=True)
    den = jnp.sum(jnp.where(emask, jnp.exp(le - m1), 0.0), axis=-1, keepdims=True)
    p1 = 1.0 / den
    p2 = jnp.exp(m2 - m1) / den
    gate1 = pg_top * p1 / (p1 + p2)
    gate2 = pg_top * p2 / (p1 + p2)

    sel1 = col == i1
    sel2 = col == i2
    oh = jnp.where(sel1 | sel2, 1.0, 0.0)
    tot = base_s[...] + _mm(tri_ref[...], oh.astype(BF16))
    rank1 = jnp.sum(jnp.where(sel1, tot, 0.0), axis=-1, keepdims=True)
    rank2 = jnp.sum(jnp.where(sel2, tot, 0.0), axis=-1, keepdims=True)
    base_s[...] = base_s[...] + jnp.sum(oh, axis=0, keepdims=True)
    cnt_ref[...] = base_s[...]

    e1 = (i1 - ROUTER_COL0).astype(F32)
    e2 = (i2 - ROUTER_COL0).astype(F32)
    rt = jnp.zeros((tm, LANES), F32)
    for k, val in enumerate((e1, e2, gate1, gate2, rank1, rank2)):
        rt = jnp.where(col == k, val, rt)
    rt_ref[...] = rt.reshape(nb, ts, LANES)
    rtt_ref[...] = rt.T[0:8, :]


def _mid(x, att, ssm_tm, mk, mv, wp, cnt0, nb, ts):
    B, S, _ = x.shape
    assert nb == 1 or ts == S
    c2 = lambda b, i: (0, 0)
    tile = lambda w: pl.BlockSpec((nb, ts, w), lambda b, i: (b, i, 0))
    return pl.pallas_call(
        _mid_kernel,
        grid=(B // nb, S // ts),
        in_specs=[
            tile(D_MODEL), tile(ATT_WIDTH),
            pl.BlockSpec((ts, nb * SSM_WIDTH), lambda b, i: (i, b)),
            pl.BlockSpec((nb, N_MEM, CA_WIDTH), lambda b, i: (b, 0, 0)),
            pl.BlockSpec((nb, N_MEM, CA_WIDTH), lambda b, i: (b, 0, 0)),
            pl.BlockSpec((1, ATT_WIDTH), c2),
            pl.BlockSpec((1, SSM_WIDTH), c2),
            pl.BlockSpec((ATT_WIDTH + SSM_WIDTH, D_MODEL), c2),
            pl.BlockSpec((1, D_MODEL), c2),
            pl.BlockSpec((D_MODEL, CA_WIDTH), c2),
            pl.BlockSpec((1, CA_HEAD_DIM), c2),
            pl.BlockSpec((CA_WIDTH, D_MODEL), c2),
            pl.BlockSpec((1, D_MODEL), c2),
            pl.BlockSpec((D_MODEL, 2 * LANES), c2),
            pl.BlockSpec((1, LANES), c2),
            pl.BlockSpec((1, LANES), c2),
            pl.BlockSpec((nb * ts, nb * ts), c2),
        ],
        out_specs=[
            tile(D_MODEL), tile(D_MODEL), tile(LANES),
            pl.BlockSpec((8, nb * ts), lambda b, i: (0, b * (S // ts) + i)),
            pl.BlockSpec((1, LANES), c2),
        ],
        out_shape=[
            jax.ShapeDtypeStruct((B, S, D_MODEL), F32),
            jax.ShapeDtypeStruct((B, S, D_MODEL), F32),
            jax.ShapeDtypeStruct((B, S, LANES), F32),
            jax.ShapeDtypeStruct((8, B * S), F32),
            jax.ShapeDtypeStruct((1, LANES), F32),
        ],
        scratch_shapes=[pltpu.VMEM((1, LANES), F32)],
        compiler_params=pltpu.CompilerParams(
            dimension_semantics=("arbitrary", "arbitrary"),
            vmem_limit_bytes=VMEM_LIMIT),
        name="mid",
    )(x, att, ssm_tm, mk, mv, wp["gao"], wp["gso"], wp["wout"], wp["gx"], wp["wcq"],
      wp["gcq"], wp["wco"], wp["gffn"], wp["wr"], wp["br"], cnt0,
      jnp.tri(nb * ts, k=-1, dtype=BF16))


def _select_part(i, tile_starts, refs):
    x = refs[0][...]
    for start, ref in zip(tile_starts[1:], refs[1:]):
        x = jnp.where(i >= start, ref[...], x)
    return x


def _part_spec(shape, tile_start, n_tiles):
    def index(i, *_):
        return (jnp.clip(i - tile_start, 0, n_tiles - 1),) + (0,) * (len(shape) - 1)
    return pl.BlockSpec(shape, index)


def _dispatch_kernel(pend_ref, padded_ref, dest_ref, *rest, tile_starts):
    n_parts = len(tile_starts)
    hn_refs = rest[:n_parts]
    xs_hbm, stage, zbuf, sem = rest[n_parts:]
    tm = hn_refs[0].shape[0]
    i = pl.program_id(0)
    slot = lax.rem(i, 2)
    blk = zbuf.shape[0]

    def wait_rows(s):
        for _ in range(2):
            pltpu.make_async_copy(stage.at[s], xs_hbm.at[pl.ds(0, tm)], sem.at[s]).wait()

    @pl.when(i == 0)
    def _():
        zbuf[...] = jnp.zeros_like(zbuf)

        def fill(row0):
            return pltpu.make_async_copy(
                zbuf, xs_hbm.at[pl.ds(pl.multiple_of(row0, blk), blk)], sem.at[2])

        tail = (pend_ref[N_EXPERTS - 1] // blk, xs_hbm.shape[0] // blk)
        for start in (True, False):
            for e in range(N_EXPERTS):
                @pl.when(padded_ref[e] > 0)
                def _():
                    c = fill(pend_ref[e] - blk)
                    c.start() if start else c.wait()

            def tail_block(b, carry):
                c = fill(b * blk)
                c.start() if start else c.wait()
                return carry

            lax.fori_loop(*tail, tail_block, 0)

    @pl.when(i >= 2)
    def _():
        wait_rows(slot)

    tile = _pack_bf16_pairs(_select_part(i, tile_starts, hn_refs))
    for s in range(2):
        @pl.when(slot == s)
        def _():
            stage[s] = tile
            for k in range(2):
                for r in range(tm):
                    pltpu.make_async_copy(stage.at[s, pl.ds(r, 1), :],
                                          xs_hbm.at[pl.ds(dest_ref[0, 0, k * tm + r], 1), :],
                                          sem.at[s]).start(priority=r % 2)

    @pl.when(i == pl.num_programs(0) - 1)
    def _():
        wait_rows(slot)

        @pl.when(i >= 1)
        def _():
            wait_rows(1 - slot)


def _tile_layout(arrays, tm):
    counts = [a.shape[0] // tm for a in arrays]
    starts = [sum(counts[:p]) for p in range(len(counts))]
    return counts, starts


def _dispatch(pad_end, padded, dest_t, hns, rows, tm, blk):
    counts, starts = _tile_layout(hns, tm)
    grid_spec = pltpu.PrefetchScalarGridSpec(
        num_scalar_prefetch=2,
        grid=(sum(counts),),
        in_specs=[pl.BlockSpec((1, 1, 2 * tm), lambda i, pe, pd: (i, 0, 0),
                               memory_space=pltpu.SMEM)]
        + [_part_spec((tm, D_MODEL), s, n) for s, n in zip(starts, counts)],
        out_specs=pl.BlockSpec(memory_space=pl.ANY),
        scratch_shapes=[
            pltpu.VMEM((2, tm, D_MODEL // 2), jnp.uint32),
            pltpu.VMEM((blk, D_MODEL // 2), jnp.uint32),
            pltpu.SemaphoreType.DMA((3,)),
        ],
    )
    return pl.pallas_call(
        functools.partial(_dispatch_kernel, tile_starts=tuple(starts)),
        grid_spec=grid_spec,
        out_shape=jax.ShapeDtypeStruct((rows, D_MODEL // 2), jnp.uint32),
        compiler_params=pltpu.CompilerParams(
            dimension_semantics=("arbitrary",), vmem_limit_bytes=VMEM_LIMIT),
        name="dispatch",
    )(pad_end, padded, dest_t, *hns)


def _moe_kernel(be_ref, nu_ref, nxt_ref, xs_ref, wg_hbm, wu_hbm, wd_hbm, yb_ref,
                wg_f, wu_f, wd_f, wg_s, wu_s, wd_s, run_s, sem):
    i = pl.program_id(0)

    def fetch(e, slot):
        return [pltpu.make_async_copy(src.at[e], dst.at[slot], sem.at[slot])
                for src, dst in ((wg_hbm, wg_f), (wu_hbm, wu_f), (wd_hbm, wd_f))]

    @pl.when(i < nu_ref[0])
    def _():
        e = be_ref[i]

        @pl.when(i == 0)
        def _():
            run_s[0] = 0
            for c in fetch(e, 0):
                c.start()

        @pl.when((i == 0) | (e != be_ref[jnp.maximum(i - 1, 0)]))
        def _():
            slot = lax.rem(run_s[0], 2)
            run_s[0] = run_s[0] + 1
            for c in fetch(e, slot):
                c.wait()
            wg_s[...] = wg_f[slot].astype(BF16)
            wu_s[...] = wu_f[slot].astype(BF16)
            wd_s[...] = wd_f[slot].astype(BF16)

            @pl.when(nxt_ref[e] != e)
            def _():
                for c in fetch(nxt_ref[e], 1 - slot):
                    c.start()

        xe = _unpack_bf16_pairs(xs_ref[...])
        g = _mm(xe, wg_s[...])
        u = _mm(xe, wu_s[...])
        hmid = ((g * (1.0 / (1.0 + jnp.exp(-g)))) * u).astype(BF16)
        yb_ref[...] = _mm(hmid, wd_s[...])

    @pl.when(i >= nu_ref[0])
    def _():
        yb_ref[...] = jnp.zeros_like(yb_ref)


def _moe(block_e, n_used, next_e, xs, w_gate, w_up, w_down, blk):
    n_blocks = block_e.shape[0]
    in_blk = lambda i, be, nu, nx: (jnp.maximum(jnp.minimum(i, nu[0] - 1), 0), 0)
    grid_spec = pltpu.PrefetchScalarGridSpec(
        num_scalar_prefetch=3,
        grid=(n_blocks,),
        in_specs=[
            pl.BlockSpec((blk, D_MODEL // 2), in_blk),
            pl.BlockSpec(memory_space=pl.ANY),
            pl.BlockSpec(memory_space=pl.ANY),
            pl.BlockSpec(memory_space=pl.ANY),
        ],
        out_specs=pl.BlockSpec((blk, D_MODEL), lambda i, be, nu, nx: (i, 0)),
        scratch_shapes=[
            pltpu.VMEM((2, D_MODEL, D_EXPERT), F32),
            pltpu.VMEM((2, D_MODEL, D_EXPERT), F32),
            pltpu.VMEM((2, D_EXPERT, D_MODEL), F32),
            pltpu.VMEM((D_MODEL, D_EXPERT), BF16),
            pltpu.VMEM((D_MODEL, D_EXPERT), BF16),
            pltpu.VMEM((D_EXPERT, D_MODEL), BF16),
            pltpu.SMEM((1,), jnp.int32),
            pltpu.SemaphoreType.DMA((2,)),
        ],
    )
    return pl.pallas_call(
        _moe_kernel,
        grid_spec=grid_spec,
        out_shape=jax.ShapeDtypeStruct((xs.shape[0], D_MODEL), F32),
        compiler_params=pltpu.CompilerParams(
            dimension_semantics=("arbitrary",), vmem_limit_bytes=VMEM_LIMIT),
        name="moe",
    )(block_e, n_used, next_e, xs, w_gate, w_up, w_down)


def _combine_kernel(dest_ref, dest_next_ref, *rest, tile_starts):
    n_parts = len(tile_starts)
    x2_refs, rt_refs = rest[:n_parts], rest[n_parts:2 * n_parts]
    yb_hbm = rest[2 * n_parts]
    o_refs = rest[2 * n_parts + 1:3 * n_parts + 1]
    buf, sem = rest[3 * n_parts + 1:]
    tm = x2_refs[0].shape[0]
    i = pl.program_id(0)
    slot = lax.rem(i, 2)

    def gather(d_ref, s):
        for k in range(2):
            for r in range(tm):
                pltpu.make_async_copy(yb_hbm.at[pl.ds(d_ref[0, 0, k * tm + r], 1), :],
                                      buf.at[s, k, pl.ds(r, 1), :],
                                      sem.at[s]).start(priority=r % 2)

    @pl.when(i == 0)
    def _():
        gather(dest_ref, 0)

    for s in range(2):
        @pl.when((i + 1 < pl.num_programs(0)) & (slot == 1 - s))
        def _():
            gather(dest_next_ref, s)

    for k in range(2):
        pltpu.make_async_copy(yb_hbm.at[pl.ds(0, tm), :], buf.at[slot, k], sem.at[slot]).wait()
    rt = _select_part(i, tile_starts, rt_refs)
    out = (_select_part(i, tile_starts, x2_refs) + rt[:, 2:3] * buf[slot, 0]
           + rt[:, 3:4] * buf[slot, 1])
    ends = tile_starts[1:] + (pl.num_programs(0),)
    for start, end, o_ref in zip(tile_starts, ends, o_refs):
        @pl.when((i >= start) & (i < end))
        def _():
            o_ref[...] = out


def _combine(dest_t, x2s, rts, yb, tm):
    counts, starts = _tile_layout(x2s, tm)
    nt = sum(counts)
    spec = lambda w: [_part_spec((tm, w), s, n) for s, n in zip(starts, counts)]
    return pl.pallas_call(
        functools.partial(_combine_kernel, tile_starts=tuple(starts)),
        grid=(nt,),
        in_specs=[
            pl.BlockSpec((1, 1, 2 * tm), lambda i: (i, 0, 0), memory_space=pltpu.SMEM),
            pl.BlockSpec((1, 1, 2 * tm), lambda i: (jnp.minimum(i + 1, nt - 1), 0, 0),
                         memory_space=pltpu.SMEM),
        ] + spec(D_MODEL) + spec(LANES) + [pl.BlockSpec(memory_space=pl.ANY)],
        out_specs=spec(D_MODEL),
        out_shape=[jax.ShapeDtypeStruct(x2.shape, F32) for x2 in x2s],
        scratch_shapes=[
            pltpu.VMEM((2, 2, tm, D_MODEL), F32),
            pltpu.SemaphoreType.DMA((2,)),
        ],
        compiler_params=pltpu.CompilerParams(
            dimension_semantics=("arbitrary",), vmem_limit_bytes=VMEM_LIMIT),
        name="combine",
    )(dest_t, dest_t, *x2s, *rts, yb)


def _hier_moe(parts, cnt, w_gate, w_up, w_down, tm, blk):
    counts = cnt[0, ROUTER_COL0:ROUTER_COL0 + N_EXPERTS].astype(jnp.int32)
    padded = (counts + blk - 1) // blk * blk
    pad_end = jnp.cumsum(padded)
    pad_start = pad_end - padded
    t_all = sum(p[0].shape[0] for p in parts)
    n_blocks = (2 * t_all + N_EXPERTS * (blk - 1)) // blk + 1
    rows = n_blocks * blk
    blk_row0 = jnp.arange(n_blocks, dtype=jnp.int32) * blk
    block_e = jnp.minimum(
        jnp.sum((pad_end[None, :] <= blk_row0[:, None]).astype(jnp.int32), axis=1),
        N_EXPERTS - 1)
    n_used = (pad_end[-1] // blk).astype(jnp.int32).reshape(1)
    ids = jnp.arange(N_EXPERTS, dtype=jnp.int32)
    later = (ids[None, :] > ids[:, None]) & (padded[None, :] > 0)
    next_e = jnp.where(jnp.any(later, axis=1),
                       jnp.min(jnp.where(later, ids[None, :], N_EXPERTS), axis=1), ids)
    experts = jnp.arange(N_EXPERTS, dtype=jnp.int32)[:, None, None]

    dests = []
    for x2, _, _, rtt in parts:
        nt = x2.shape[0] // tm
        eid = rtt[0:2].astype(jnp.int32)
        rank = rtt[4:6].astype(jnp.int32)
        dest = rank + jnp.sum(
            jnp.where(eid[None] == experts, pad_start[:, None, None], 0), axis=0)
        dests.append(jnp.concatenate([dest[0].reshape(nt, tm), dest[1].reshape(nt, tm)],
                                     axis=1).reshape(nt, 1, 2 * tm))
    dest_t = jnp.concatenate(dests, axis=0)
    xs = _dispatch(pad_end, padded, dest_t, [p[1] for p in parts], rows, tm, blk)
    yb = _moe(block_e, n_used, next_e.astype(jnp.int32), xs, w_gate, w_up, w_down, blk)
    return _combine(dest_t, [p[0] for p in parts], [p[2] for p in parts], yb, tm)


def _rope_table(pos):
    half = ROPE_DIM // 2
    d = jnp.arange(LANES, dtype=jnp.int32) % HEAD_DIM
    inv = ROPE_THETA ** (-(2 * (d % half)).astype(F32) / ROPE_DIM)
    ang = pos.astype(F32)[:, None] * inv[None, :]
    cos, sin = jnp.cos(ang), jnp.sin(ang)
    rotary = (d < ROPE_DIM)[None, :]
    first = (d < half)[None, :]
    return jnp.concatenate([jnp.where(rotary, cos, 1.0),
                            jnp.where(first, -sin, 0.0),
                            jnp.where(rotary & ~first, sin, 0.0)], axis=1)


def _mixers(x, pos_rope, kctx_prev, vctx_prev, h0r, h0i, mk, mv, wp, sp, cnt0, *,
            tm_in, tq, ssm_l, tm_mid):
    B, S, _ = x.shape
    T = B * S
    q, k3, v3, u_tm = _in_proj(x, wp["gmix"], wp["win"], wp["gq"], wp["gk"], pos_rope, *tm_in)
    if kctx_prev is None:
        kctx, vctx = k3, v3
    else:
        kctx = jnp.concatenate([kctx_prev, k3], axis=1)
        vctx = jnp.concatenate([vctx_prev, v3], axis=1)
    att = _swa(wp["sink"], q, kctx, vctx, tq, mask_context=kctx_prev is None)
    ssm_tm, hr, hi = _ssm(u_tm.reshape(S, B, SSM_WIDTH), h0r, h0i, sp, ssm_l)
    x2, hn, rt, rtt, cnt = _mid(x, att, ssm_tm.reshape(S, B * SSM_WIDTH), mk, mv, wp, cnt0,
                                *tm_mid)
    part = (x2.reshape(T, D_MODEL), hn.reshape(T, D_MODEL), rt.reshape(T, LANES), rtt)
    return part, cnt, k3, v3, hr, hi


def kernel(x_prompt, x_sample, cache_attn_k, cache_attn_v, state_ssm_re, state_ssm_im, cache_mem_k, cache_mem_v, mem_prompt, norm_mix, w_in, q_norm, k_norm, attn_sink, ssm_lambda_re, ssm_lambda_im, ssm_log_dt, ssm_b_re, ssm_b_im, ssm_c_re, ssm_c_im, ssm_d, ssm_w_glu, ssm_b_glu, norm_attn_out, norm_ssm_out, w_out, norm_cross, norm_mem, w_cq, w_ck, w_cv, cq_norm, ck_norm, w_co, norm_ffn, w_router_group, b_router_group, w_router_expert, b_router_expert, w_e_gate, w_e_up, w_e_down):
    depth = norm_mix.shape[0]
    Bp, Lp, _ = x_prompt.shape
    Bs, Ls, _ = x_sample.shape
    yp, ys = x_prompt, x_sample
    rope_p = _rope_table(jnp.arange(Lp, dtype=jnp.int32))
    rope_s = _rope_table(PAST_LEN + jnp.arange(Ls, dtype=jnp.int32))
    outs = [[] for _ in range(10)]
    n_router = N_EXPERT_GROUPS + N_EXPERTS
    for l in range(depth):
        row = lambda a: a[l].astype(F32).reshape(1, -1)
        w_r = jnp.pad(jnp.concatenate([w_router_group[l], w_router_expert[l]], axis=1).astype(F32),
                      ((0, 0), (0, LANES - n_router)))
        w_r_hi = w_r.astype(BF16)
        w_r_lo = (w_r - w_r_hi.astype(F32)).astype(BF16)
        b_r = jnp.pad(jnp.concatenate([b_router_group[l], b_router_expert[l]]).astype(F32),
                      (0, LANES - n_router)).reshape(1, LANES)
        wp = {
            "gmix": row(norm_mix), "win": w_in[l].astype(BF16),
            "gq": jnp.tile(row(q_norm), (1, LANES // HEAD_DIM)),
            "gk": jnp.tile(row(k_norm), (1, LANES // HEAD_DIM)),
            "sink": attn_sink[l].astype(F32),
            "gao": row(norm_attn_out), "gso": row(norm_ssm_out),
            "wout": w_out[l].astype(BF16), "gx": row(norm_cross),
            "wcq": w_cq[l].astype(BF16), "gcq": row(cq_norm),
            "wco": w_co[l].astype(BF16), "gffn": row(norm_ffn),
            "wr": jnp.concatenate([w_r_hi, w_r_lo], axis=1), "br": b_r,
        }
        sp = _ssm_params(ssm_lambda_re[l], ssm_lambda_im[l], ssm_log_dt[l], ssm_b_re[l],
                         ssm_b_im[l], ssm_c_re[l], ssm_c_im[l], ssm_d[l], ssm_w_glu[l],
                         ssm_b_glu[l])
        ew = (w_e_gate[l].astype(F32), w_e_up[l].astype(F32), w_e_down[l].astype(F32))

        w_ckv = jnp.concatenate([w_ck[l], w_cv[l]], axis=1).astype(BF16)
        mkp, mvp = _memkv(mem_prompt.reshape(Bp * N_MEM, D_MODEL), row(norm_mem), w_ckv,
                          row(ck_norm), 512)
        mkp = mkp.reshape(Bp, N_MEM, CA_WIDTH)
        mvp = mvp.reshape(Bp, N_MEM, CA_WIDTH)

        zst = jnp.zeros((Bp, SSM_COLS), F32)
        part_p, cnt_p, kp, vp, hpr, hpi = _mixers(
            yp, rope_p, None, None, zst, zst, mkp, mvp, wp, sp, jnp.zeros((1, LANES), F32),
            tm_in=(1, 512), tq=1024, ssm_l=64, tm_mid=(1, 512))
        part_s, cnt_s, kn, vn, hsr, hsi = _mixers(
            ys, rope_s, cache_attn_k[l].reshape(Bs, WINDOW, KV_WIDTH).astype(F32),
            cache_attn_v[l].reshape(Bs, WINDOW, KV_WIDTH).astype(F32),
            state_ssm_re[l].astype(F32).reshape(Bs, SSM_COLS),
            state_ssm_im[l].astype(F32).reshape(Bs, SSM_COLS),
            cache_mem_k[l].astype(F32).reshape(Bs, N_MEM, CA_WIDTH),
            cache_mem_v[l].astype(F32).reshape(Bs, N_MEM, CA_WIDTH), wp, sp, cnt_p,
            tm_in=(8, Ls), tq=CHUNK, ssm_l=Ls, tm_mid=(8, Ls))
        yp, ys = _hier_moe([part_p, part_s], cnt_s, *ew, 256, 2 * MOE_BLOCK)
        yp = yp.reshape(Bp, Lp, D_MODEL)
        ys = ys.reshape(Bs, Ls, D_MODEL)

        sg = (N_SSM_GROUPS, SSM_STATE)
        kvs = (N_KV_HEADS, HEAD_DIM)
        vals = (kp[:, Lp - WINDOW:].reshape(Bp, WINDOW, *kvs),
                vp[:, Lp - WINDOW:].reshape(Bp, WINDOW, *kvs),
                hpr.reshape(Bp, *sg), hpi.reshape(Bp, *sg),
                mkp.reshape(Bp, N_MEM, CA_HEADS, CA_HEAD_DIM),
                mvp.reshape(Bp, N_MEM, CA_HEADS, CA_HEAD_DIM),
                kn.reshape(Bs, Ls, *kvs), vn.reshape(Bs, Ls, *kvs),
                hsr.reshape(Bs, *sg), hsi.reshape(Bs, *sg))
        for lst, val in zip(outs, vals):
            lst.append(val)
    return (yp, ys) + tuple(jnp.stack(lst) for lst in outs)
```

```python
import functools
import math

import jax
import jax.numpy as jnp
from jax import lax
from jax.experimental import pallas as pl
from jax.experimental.pallas import tpu as pltpu

F32 = jnp.float32
BF16 = jnp.bfloat16

D_MODEL = 1024
CHUNK = 64
N_Q_HEADS = 8
N_KV_HEADS = 2
GQA = N_Q_HEADS // N_KV_HEADS
HEAD_DIM = 64
WINDOW = 128
BAND = WINDOW + CHUNK
ROPE_DIM = HEAD_DIM // 4
ROPE_THETA = 500000.0
ATT_WIDTH = N_Q_HEADS * HEAD_DIM
KV_WIDTH = N_KV_HEADS * HEAD_DIM
SSM_GROUP = 16
SSM_WIDTH = D_MODEL // 2
N_SSM_GROUPS = SSM_WIDTH // SSM_GROUP
SSM_STATE = 64
SSM_COLS = N_SSM_GROUPS * SSM_STATE
IN_WIDTH = ATT_WIDTH + 2 * KV_WIDTH + SSM_WIDTH
N_MEM = 256
CA_HEADS = 4
CA_HEAD_DIM = 128
CA_WIDTH = CA_HEADS * CA_HEAD_DIM
N_EXPERT_GROUPS = 4
EXPERTS_PER_GROUP = 8
N_EXPERTS = N_EXPERT_GROUPS * EXPERTS_PER_GROUP
D_EXPERT = 512
MOE_BLOCK = 256
EPS = 1e-6
NEG = -1e30
PAST_LEN = 4096

LANES = 128
ROUTER_COL0 = N_EXPERT_GROUPS
VMEM_LIMIT = 48 * 1024 * 1024


def _rms(x, g):
    ms = jnp.mean(x * x, axis=-1, keepdims=True)
    return (x * lax.rsqrt(ms + EPS)) * g


def _mm(a, b):
    return jnp.dot(a, b, preferred_element_type=F32)


_HI_HALF = 0xFFFF0000


def _pack_bf16_pairs(x):
    half = x.shape[1] // 2
    bits = lambda v: lax.bitcast_convert_type(v.astype(BF16).astype(F32), jnp.uint32)
    return (lax.shift_right_logical(bits(x[:, :half]), jnp.uint32(16))
            | (bits(x[:, half:]) & jnp.uint32(_HI_HALF)))


def _unpack_bf16_pairs(w):
    lo = lax.bitcast_convert_type(lax.shift_left(w, jnp.uint32(16)), F32)
    hi = lax.bitcast_convert_type(w & jnp.uint32(_HI_HALF), F32)
    return jnp.concatenate([lo.astype(BF16), hi.astype(BF16)], axis=1)


def _in_proj_kernel(x_ref, g_ref, w_ref, gq_ref, gk_ref, rope_ref,
                    q_ref, k_ref, v_ref, u_ref):
    nb, ts, _ = x_ref.shape
    tm = nb * ts
    h = _rms(x_ref[...].reshape(tm, D_MODEL), g_ref[...])
    hin = _mm(h.astype(BF16), w_ref[...])
    rope = jnp.concatenate([rope_ref[...]] * nb, axis=0)
    cos = rope[:, 0:LANES]
    sin_lo = rope[:, LANES:2 * LANES]
    sin_hi = rope[:, 2 * LANES:3 * LANES]
    lane = lax.broadcasted_iota(jnp.int32, (tm, LANES), 1)
    left = lane < HEAD_DIM

    def norm_rope(z, g):
        sq = z * z
        lsum = jnp.sum(jnp.where(left, sq, 0.0), axis=-1, keepdims=True)
        rsum = jnp.sum(jnp.where(left, 0.0, sq), axis=-1, keepdims=True)
        ms = jnp.where(left, lsum, rsum) * (1.0 / HEAD_DIM)
        zn = (z * lax.rsqrt(ms + EPS)) * g
        half = ROPE_DIM // 2
        return (zn * cos + pltpu.roll(zn, LANES - half, 1) * sin_lo
                + pltpu.roll(zn, half, 1) * sin_hi)

    for j in range(ATT_WIDTH // LANES):
        sl = slice(j * LANES, (j + 1) * LANES)
        q_ref[:, :, sl] = norm_rope(hin[:, sl], gq_ref[...]).reshape(nb, ts, LANES)
    k_ref[...] = norm_rope(hin[:, ATT_WIDTH:ATT_WIDTH + KV_WIDTH],
                           gk_ref[...]).reshape(nb, ts, KV_WIDTH)
    v_ref[...] = hin[:, ATT_WIDTH + KV_WIDTH:ATT_WIDTH + 2 * KV_WIDTH].reshape(nb, ts, KV_WIDTH)
    for b in range(nb):
        u_ref[:, b * SSM_WIDTH:(b + 1) * SSM_WIDTH] = (
            hin[b * ts:(b + 1) * ts, ATT_WIDTH + 2 * KV_WIDTH:])


def _in_proj(x, g, w_bf, gq, gk, rope, nb, ts):
    B, S, _ = x.shape
    full = lambda b, i: (0, 0)
    tile = lambda w: pl.BlockSpec((nb, ts, w), lambda b, i: (b, i, 0))
    return pl.pallas_call(
        _in_proj_kernel,
        grid=(B // nb, S // ts),
        in_specs=[
            tile(D_MODEL),
            pl.BlockSpec((1, D_MODEL), full),
            pl.BlockSpec((D_MODEL, IN_WIDTH), full),
            pl.BlockSpec((1, LANES), full),
            pl.BlockSpec((1, LANES), full),
            pl.BlockSpec((ts, 3 * LANES), lambda b, i: (i, 0)),
        ],
        out_specs=[
            tile(ATT_WIDTH), tile(KV_WIDTH), tile(KV_WIDTH),
            pl.BlockSpec((ts, nb * SSM_WIDTH), lambda b, i: (i, b)),
        ],
        out_shape=[
            jax.ShapeDtypeStruct((B, S, ATT_WIDTH), F32),
            jax.ShapeDtypeStruct((B, S, KV_WIDTH), F32),
            jax.ShapeDtypeStruct((B, S, KV_WIDTH), F32),
            jax.ShapeDtypeStruct((S, B * SSM_WIDTH), F32),
        ],
        compiler_params=pltpu.CompilerParams(
            dimension_semantics=("arbitrary", "arbitrary"),
            vmem_limit_bytes=VMEM_LIMIT),
        name="in_proj",
    )(x, g, w_bf, gq, gk, rope)


def _swa_kernel(sink_ref, q_ref, k_ref, v_ref, o_ref, *, mask_context):
    tq = q_ref.shape[1]
    i = pl.program_id(1)
    nch = tq // CHUNK
    lane = lax.broadcasted_iota(jnp.int32, (BAND, LANES), 1)
    lo_half = lane < HEAD_DIM
    vrow_lo = lax.broadcasted_iota(jnp.int32, (LANES, BAND), 0) < HEAD_DIM
    q_lo = lax.broadcasted_iota(jnp.int32, (1, LANES), 1) < CHUNK
    slabs_per_kv = GQA * HEAD_DIM // LANES

    units = []
    scores = []
    vpads = {}
    for c in range(nch):
        chunk = i * nch + c
        if mask_context:
            first = jnp.maximum(chunk - WINDOW // CHUNK, 0)
            start = pl.multiple_of(first * CHUNK, CHUNK)
            kidx = start + lax.broadcasted_iota(jnp.int32, (BAND, LANES), 0)
            valid = kidx < (chunk + 1) * CHUNK
        else:
            start = pl.multiple_of(chunk * CHUNK, CHUNK)
        kb = k_ref[0, pl.ds(start, BAND), :]
        kb_sw = pltpu.roll(kb, HEAD_DIM, 1)
        vt = v_ref[0, pl.ds(start, BAND), :].T
        vt_sw = jnp.concatenate([vt[HEAD_DIM:], vt[:HEAD_DIM]], axis=0)
        for kvh in range(N_KV_HEADS):
            k_own, k_oth = (kb, kb_sw) if kvh == 0 else (kb_sw, kb)
            v_own, v_oth = (vt, vt_sw) if kvh == 0 else (vt_sw, vt)
            kpad = (jnp.where(lo_half, k_own, 0.0).astype(BF16),
                    jnp.where(lo_half, 0.0, k_oth).astype(BF16))
            vpads[(c, kvh)] = (jnp.where(vrow_lo, v_own, 0.0).astype(BF16),
                               jnp.where(vrow_lo, 0.0, v_oth).astype(BF16))
            col0 = kvh * GQA * HEAD_DIM
            q2 = jnp.concatenate(
                [q_ref[0, c * CHUNK:(c + 1) * CHUNK, col0 + m * LANES:col0 + (m + 1) * LANES]
                 for m in range(slabs_per_kv)], axis=0).astype(BF16)
            for side in range(2):
                s = lax.dot_general(kpad[side], q2, (((1,), (1,)), ((), ())),
                                    preferred_element_type=F32) * (HEAD_DIM ** -0.5)
                if mask_context:
                    s = jnp.where(valid, s, NEG)
                units.append((c, kvh, side))
                scores.append(s)

    sinks = [jnp.where(q_lo, sink_ref[kvh * GQA + side], sink_ref[kvh * GQA + 2 + side])
             for (_, kvh, side) in units]
    maxes = [jnp.maximum(jnp.max(s, axis=0, keepdims=True), sk)
             for s, sk in zip(scores, sinks)]
    exps = [jnp.exp(s - mx) for s, mx in zip(scores, maxes)]
    dens = [jnp.sum(p, axis=0, keepdims=True) + jnp.exp(sk - mx)
            for p, sk, mx in zip(exps, sinks, maxes)]
    probs = [(p * (1.0 / den)).astype(BF16) for p, den in zip(exps, dens)]

    for n in range(0, len(units), 2):
        c, kvh, _ = units[n]
        vp = vpads[(c, kvh)]
        o = (_mm(vp[0], probs[n]) + _mm(vp[1], probs[n + 1])).T
        col0 = kvh * GQA * HEAD_DIM
        for m in range(slabs_per_kv):
            o_ref[0, c * CHUNK:(c + 1) * CHUNK, col0 + m * LANES:col0 + (m + 1) * LANES] = (
                o[m * CHUNK:(m + 1) * CHUNK])


def _swa(sink, q, kctx, vctx, tq, mask_context):
    B, Sq, _ = q.shape
    Sk = kctx.shape[1]
    return pl.pallas_call(
        functools.partial(_swa_kernel, mask_context=mask_context),
        grid=(B, Sq // tq),
        in_specs=[
            pl.BlockSpec(memory_space=pltpu.SMEM),
            pl.BlockSpec((1, tq, ATT_WIDTH), lambda b, i: (b, i, 0)),
            pl.BlockSpec((1, Sk, KV_WIDTH), lambda b, i: (b, 0, 0)),
            pl.BlockSpec((1, Sk, KV_WIDTH), lambda b, i: (b, 0, 0)),
        ],
        out_specs=pl.BlockSpec((1, tq, ATT_WIDTH), lambda b, i: (b, i, 0)),
        out_shape=jax.ShapeDtypeStruct((B, Sq, ATT_WIDTH), F32),
        compiler_params=pltpu.CompilerParams(
            dimension_semantics=("arbitrary", "arbitrary"),
            vmem_limit_bytes=VMEM_LIMIT),
        name="swa",
    )(sink, q, kctx, vctx)


def _ssm_kernel(u_ref, h0r_ref, h0i_ref, lam_ref, bre_ref, bim_ref, cre_ref, cim_ref,
                d_ref, wglu_ref, bglu_ref,
                y_ref, hr_out, hi_out, sr0, si0, sr1, si1, hr_s, hi_s):
    L, B, _ = u_ref.shape
    rows = L * B
    half_w = SSM_WIDTH // 2
    half_c = SSM_COLS // 2
    halves = ((sr0, si0), (sr1, si1))

    @pl.when(pl.program_id(0) == 0)
    def _():
        hr_s[...] = h0r_ref[...]
        hi_s[...] = h0i_ref[...]

    u = u_ref[...].reshape(rows, SSM_WIDTH)
    ub = u.astype(BF16)

    def project_in(hf):
        sr, si = halves[hf]
        uh = ub[:, hf * half_w:(hf + 1) * half_w]
        sr[...] = _mm(uh, bre_ref[hf])
        si[...] = _mm(uh, bim_ref[hf])

    def recur(hf):
        sr, si = halves[hf]
        cw = 4 * LANES
        for cc in range(half_c // cw):
            cols = slice(cc * cw, (cc + 1) * cw)
            gcols = slice(hf * half_c + cc * cw, hf * half_c + (cc + 1) * cw)
            lr = jnp.broadcast_to(lam_ref[0:1, gcols], (B, cw))
            li = jnp.broadcast_to(lam_ref[1:2, gcols], (B, cw))
            hr, hi = hr_s[:, gcols], hi_s[:, gcols]
            for t in range(L):
                at_t = slice(t * B, (t + 1) * B)
                hr, hi = (lr * hr - li * hi + sr[at_t, cols],
                          lr * hi + li * hr + si[at_t, cols])
                sr[at_t, cols] = hr
                si[at_t, cols] = hi
            hr_s[:, gcols] = hr
            hi_s[:, gcols] = hi

    def project_out(hf):
        sr, si = halves[hf]
        return (_mm(sr[...].astype(BF16), cre_ref[hf]) + _mm(si[...].astype(BF16), cim_ref[hf]))

    project_in(0)
    project_in(1)
    recur(0)
    y0 = project_out(0)
    recur(1)
    y1 = project_out(1)
    y = jnp.concatenate([y0, y1], axis=1) + d_ref[...] * u
    g = 0.5 * y * (1.0 + jnp.tanh(math.sqrt(2.0 / math.pi) * (y + 0.044715 * (y * y * y))))
    gb = g.astype(BF16)
    z = jnp.concatenate(
        [_mm(gb[:, hf * half_w:(hf + 1) * half_w], wglu_ref[hf]) for hf in range(2)],
        axis=1) + bglu_ref[...]
    out = g * (1.0 / (1.0 + jnp.exp(-z)))
    y_ref[...] = out.reshape(L, B, SSM_WIDTH)
    hr_out[...] = hr_s[...]
    hi_out[...] = hi_s[...]


def _ssm(u, h0r, h0i, sp, L):
    S, B, _ = u.shape
    c2 = lambda i: (0, 0)
    c3 = lambda i: (0, 0, 0)
    return pl.pallas_call(
        _ssm_kernel,
        grid=(S // L,),
        in_specs=[
            pl.BlockSpec((L, B, SSM_WIDTH), lambda i: (i, 0, 0)),
            pl.BlockSpec((B, SSM_COLS), c2),
            pl.BlockSpec((B, SSM_COLS), c2),
            pl.BlockSpec((2, SSM_COLS), c2),
            pl.BlockSpec((2, SSM_WIDTH // 2, SSM_COLS // 2), c3),
            pl.BlockSpec((2, SSM_WIDTH // 2, SSM_COLS // 2), c3),
            pl.BlockSpec((2, SSM_COLS // 2, SSM_WIDTH // 2), c3),
            pl.BlockSpec((2, SSM_COLS // 2, SSM_WIDTH // 2), c3),
            pl.BlockSpec((1, SSM_WIDTH), c2),
            pl.BlockSpec((2, SSM_WIDTH // 2, SSM_WIDTH // 2), c3),
            pl.BlockSpec((1, SSM_WIDTH), c2),
        ],
        out_specs=[
            pl.BlockSpec((L, B, SSM_WIDTH), lambda i: (i, 0, 0)),
            pl.BlockSpec((B, SSM_COLS), c2),
            pl.BlockSpec((B, SSM_COLS), c2),
        ],
        out_shape=[
            jax.ShapeDtypeStruct((S, B, SSM_WIDTH), F32),
            jax.ShapeDtypeStruct((B, SSM_COLS), F32),
            jax.ShapeDtypeStruct((B, SSM_COLS), F32),
        ],
        scratch_shapes=[
            pltpu.VMEM((L * B, SSM_COLS // 2), F32),
            pltpu.VMEM((L * B, SSM_COLS // 2), F32),
            pltpu.VMEM((L * B, SSM_COLS // 2), F32),
            pltpu.VMEM((L * B, SSM_COLS // 2), F32),
            pltpu.VMEM((B, SSM_COLS), F32),
            pltpu.VMEM((B, SSM_COLS), F32),
        ],
        compiler_params=pltpu.CompilerParams(
            dimension_semantics=("arbitrary",), vmem_limit_bytes=VMEM_LIMIT),
        name="ssm",
    )(u, h0r, h0i, sp["lam"], sp["bre"], sp["bim"], sp["cre"], sp["cim"],
      sp["d"], sp["wglu"], sp["bglu"])


def _block_diag(blocks):
    G, r, c = blocks.shape
    col = jnp.arange(G * c, dtype=jnp.int32)
    spread = (col[None, :] % c == jnp.arange(c, dtype=jnp.int32)[:, None]).astype(F32)
    same_group = (jnp.arange(G * r, dtype=jnp.int32)[:, None] // r) == (col[None, :] // c)
    tiled = jnp.dot(blocks.reshape(G * r, c), spread, precision=lax.Precision.HIGHEST)
    return jnp.where(same_group, tiled, 0.0)


def _ssm_params(lam_re, lam_im, log_dt, b_re, b_im, c_re, c_im, d, w_glu, b_glu):
    lam = lax.complex(lam_re.astype(F32), lam_im.astype(F32))
    dt = jnp.exp(log_dt.astype(F32))[:, None]
    lam_bar = jnp.exp(lam * dt)
    bmat = lax.complex(b_re.astype(F32), b_im.astype(F32))
    b_bar = ((lam_bar - 1.0) / lam)[..., None] * bmat
    lam2 = jnp.stack([lam_bar.real.reshape(-1), lam_bar.imag.reshape(-1)])
    hw, hc = SSM_WIDTH // 2, SSM_COLS // 2
    split_b = lambda m: jnp.stack([m[:hw, :hc], m[hw:, hc:]]).astype(BF16)
    split_c = lambda m: jnp.stack([m[:hc, :hw], m[hc:, hw:]]).astype(BF16)
    wg = _block_diag(w_glu.astype(F32))
    return {
        "lam": lam2,
        "bre": split_b(_block_diag(b_bar.real).T),
        "bim": split_b(_block_diag(b_bar.imag).T),
        "cre": split_c(_block_diag(c_re.astype(F32)).T),
        "cim": split_c(_block_diag(-c_im.astype(F32)).T),
        "d": d.astype(F32).reshape(1, SSM_WIDTH),
        "wglu": jnp.stack([wg[:hw, :hw], wg[hw:, hw:]]).astype(BF16),
        "bglu": b_glu.astype(F32).reshape(1, SSM_WIDTH),
    }


def _memkv_kernel(m_ref, g_ref, w_ref, gk_ref, k_ref, v_ref):
    m = _rms(m_ref[...], g_ref[...])
    kv = _mm(m.astype(BF16), w_ref[...])
    for h in range(CA_HEADS):
        sl = slice(h * CA_HEAD_DIM, (h + 1) * CA_HEAD_DIM)
        k_ref[:, sl] = _rms(kv[:, sl], gk_ref[...])
    v_ref[...] = kv[:, CA_WIDTH:]


def _memkv(mem2d, g, w_bf, gk, tm):
    T = mem2d.shape[0]
    full = lambda i: (0, 0)
    return pl.pallas_call(
        _memkv_kernel,
        grid=(T // tm,),
        in_specs=[
            pl.BlockSpec((tm, D_MODEL), lambda i: (i, 0)),
            pl.BlockSpec((1, D_MODEL), full),
            pl.BlockSpec((D_MODEL, 2 * CA_WIDTH), full),
            pl.BlockSpec((1, CA_HEAD_DIM), full),
        ],
        out_specs=[
            pl.BlockSpec((tm, CA_WIDTH), lambda i: (i, 0)),
            pl.BlockSpec((tm, CA_WIDTH), lambda i: (i, 0)),
        ],
        out_shape=[
            jax.ShapeDtypeStruct((T, CA_WIDTH), F32),
            jax.ShapeDtypeStruct((T, CA_WIDTH), F32),
        ],
        compiler_params=pltpu.CompilerParams(
            dimension_semantics=("arbitrary",), vmem_limit_bytes=VMEM_LIMIT),
        name="memkv",
    )(mem2d, g, w_bf, gk)


def _mid_kernel(x_ref, att_ref, ssm_ref, mk_ref, mv_ref,
                gao_ref, gso_ref, wout_ref, gx_ref, wcq_ref, gcq_ref, wco_ref,
                gffn_ref, wr_ref, br_ref, cnt0_ref, tri_ref,
                x2_ref, hn_ref, rt_ref, rtt_ref, cnt_ref, base_s):
    nb, ts, _ = x_ref.shape
    tm = nb * ts

    @pl.when((pl.program_id(0) == 0) & (pl.program_id(1) == 0))
    def _():
        base_s[...] = cnt0_ref[...]

    ssm = jnp.concatenate(
        [ssm_ref[:, b * SSM_WIDTH:(b + 1) * SSM_WIDTH] for b in range(nb)], axis=0)
    a = _rms(att_ref[...].reshape(tm, ATT_WIDTH), gao_ref[...]).astype(BF16)
    s = _rms(ssm, gso_ref[...]).astype(BF16)
    x1 = (x_ref[...].reshape(tm, D_MODEL) + _mm(a, wout_ref[0:ATT_WIDTH, :])
          + _mm(s, wout_ref[ATT_WIDTH:, :]))

    qx = _mm(_rms(x1, gx_ref[...]).astype(BF16), wcq_ref[...])
    heads = []
    for h in range(CA_HEADS):
        sl = slice(h * CA_HEAD_DIM, (h + 1) * CA_HEAD_DIM)
        qh = _rms(qx[:, sl], gcq_ref[...]).astype(BF16)
        per_batch = []
        for b in range(nb):
            kh = mk_ref[b, :, sl].astype(BF16)
            qb = qh[b * ts:(b + 1) * ts]
            if ts <= LANES:
                vt = mv_ref[b, :, sl].T.astype(BF16)
                sc = lax.dot_general(kh, qb, (((1,), (1,)), ((), ())),
                                     preferred_element_type=F32) * (CA_HEAD_DIM ** -0.5)
                p = jnp.exp(sc - jnp.max(sc, axis=0, keepdims=True))
                p = p * (1.0 / jnp.sum(p, axis=0, keepdims=True))
                per_batch.append(_mm(vt, p.astype(BF16)).T)
            else:
                vh = mv_ref[b, :, sl].astype(BF16)
                sc = lax.dot_general(qb, kh, (((1,), (1,)), ((), ())),
                                     preferred_element_type=F32) * (CA_HEAD_DIM ** -0.5)
                p = jnp.exp(sc - jnp.max(sc, axis=-1, keepdims=True))
                p = p / jnp.sum(p, axis=-1, keepdims=True)
                per_batch.append(_mm(p.astype(BF16), vh))
        heads.append(jnp.concatenate(per_batch, axis=0))
    o = jnp.concatenate(heads, axis=1).astype(BF16)
    x2 = x1 + _mm(o, wco_ref[...])
    x2_ref[...] = x2.reshape(nb, ts, D_MODEL)

    hn = _rms(x2, gffn_ref[...])
    hn_ref[...] = hn.reshape(nb, ts, D_MODEL)

    h_hi = hn.astype(BF16)
    h_lo = (hn - h_hi.astype(F32)).astype(BF16)
    r1 = _mm(h_hi, wr_ref[...])
    lg = (r1[:, :LANES] + r1[:, LANES:] + _mm(h_lo, wr_ref[:, 0:LANES])
          + br_ref[...])

    col = lax.broadcasted_iota(jnp.int32, (tm, LANES), 1)
    big = jnp.int32(4 * LANES)
    gmask = col < N_EXPERT_GROUPS
    lgg = jnp.where(gmask, lg, NEG)
    mg = jnp.max(lgg, axis=-1, keepdims=True)
    grp = jnp.min(jnp.where(gmask & (lgg == mg), col, big), axis=-1, keepdims=True)
    pg_top = 1.0 / jnp.sum(jnp.where(gmask, jnp.exp(lgg - mg), 0.0), axis=-1, keepdims=True)

    ecol = col - ROUTER_COL0
    emask = ((ecol >= 0) & (ecol < N_EXPERTS)
             & (lax.shift_right_arithmetic(ecol, 3) == grp))
    le = jnp.where(emask, lg, NEG)
    m1 = jnp.max(le, axis=-1, keepdims=True)
    i1 = jnp.min(jnp.where(emask & (le == m1), col, big), axis=-1, keepdims=True)
    rest = emask & (col != i1)
    le2 = jnp.where(rest, lg, NEG)
    m2 = jnp.max(le2, axis=-1, keepdims=True)
    i2 = jnp.min(jnp.where(rest & (le2 == m2), col, big), axis=-1, keepdims=True)
    den = jnp.sum(jnp.where(emask, jnp.exp(le - m1), 0.0), axis=-1, keepdims=True)
    p1 = 1.0 / den
    p2 = jnp.exp(m2 - m1) / den
    gate1 = pg_top * p1 / (p1 + p2)
    gate2 = pg_top * p2 / (p1 + p2)

    sel1 = col == i1
    sel2 = col == i2
    oh = jnp.where(sel1 | sel2, 1.0, 0.0)
    tot = base_s[...] + _mm(tri_ref[...], oh.astype(BF16))
    rank1 = jnp.sum(jnp.where(sel1, tot, 0.0), axis=-1, keepdims=True)
    rank2 = jnp.sum(jnp.where(sel2, tot, 0.0), axis=-1, keepdims=True)
    base_s[...] = base_s[...] + jnp.sum(oh, axis=0, keepdims=True)
    cnt_ref[...] = base_s[...]

    e1 = (i1 - ROUTER_COL0).astype(F32)
    e2 = (i2 - ROUTER_COL0).astype(F32)
    rt = jnp.zeros((tm, LANES), F32)
    for k, val in enumerate((e1, e2, gate1, gate2, rank1, rank2)):
        rt = jnp.where(col == k, val, rt)
    rt_ref[...] = rt.reshape(nb, ts, LANES)
    rtt_ref[...] = rt.T[0:8, :]


def _mid(x, att, ssm_tm, mk, mv, wp, cnt0, nb, ts):
    B, S, _ = x.shape
    assert nb == 1 or ts == S
    c2 = lambda b, i: (0, 0)
    tile = lambda w: pl.BlockSpec((nb, ts, w), lambda b, i: (b, i, 0))
    return pl.pallas_call(
        _mid_kernel,
        grid=(B // nb, S // ts),
        in_specs=[
            tile(D_MODEL), tile(ATT_WIDTH),
            pl.BlockSpec((ts, nb * SSM_WIDTH), lambda b, i: (i, b)),
            pl.BlockSpec((nb, N_MEM, CA_WIDTH), lambda b, i: (b, 0, 0)),
            pl.BlockSpec((nb, N_MEM, CA_WIDTH), lambda b, i: (b, 0, 0)),
            pl.BlockSpec((1, ATT_WIDTH), c2),
            pl.BlockSpec((1, SSM_WIDTH), c2),
            pl.BlockSpec((ATT_WIDTH + SSM_WIDTH, D_MODEL), c2),
            pl.BlockSpec((1, D_MODEL), c2),
            pl.BlockSpec((D_MODEL, CA_WIDTH), c2),
            pl.BlockSpec((1, CA_HEAD_DIM), c2),
            pl.BlockSpec((CA_WIDTH, D_MODEL), c2),
            pl.BlockSpec((1, D_MODEL), c2),
            pl.BlockSpec((D_MODEL, 2 * LANES), c2),
            pl.BlockSpec((1, LANES), c2),
            pl.BlockSpec((1, LANES), c2),
            pl.BlockSpec((nb * ts, nb * ts), c2),
        ],
        out_specs=[
            tile(D_MODEL), tile(D_MODEL), tile(LANES),
            pl.BlockSpec((8, nb * ts), lambda b, i: (0, b * (S // ts) + i)),
            pl.BlockSpec((1, LANES), c2),
        ],
        out_shape=[
            jax.ShapeDtypeStruct((B, S, D_MODEL), F32),
            jax.ShapeDtypeStruct((B, S, D_MODEL), F32),
            jax.ShapeDtypeStruct((B, S, LANES), F32),
            jax.ShapeDtypeStruct((8, B * S), F32),
            jax.ShapeDtypeStruct((1, LANES), F32),
        ],
        scratch_shapes=[pltpu.VMEM((1, LANES), F32)],
        compiler_params=pltpu.CompilerParams(
            dimension_semantics=("arbitrary", "arbitrary"),
            vmem_limit_bytes=VMEM_LIMIT),
        name="mid",
    )(x, att, ssm_tm, mk, mv, wp["gao"], wp["gso"], wp["wout"], wp["gx"], wp["wcq"],
      wp["gcq"], wp["wco"], wp["gffn"], wp["wr"], wp["br"], cnt0,
      jnp.tri(nb * ts, k=-1, dtype=BF16))


def _select_part(i, tile_starts, refs):
    x = refs[0][...]
    for start, ref in zip(tile_starts[1:], refs[1:]):
        x = jnp.where(i >= start, ref[...], x)
    return x


def _part_spec(shape, tile_start, n_tiles):
    def index(i, *_):
        return (jnp.clip(i - tile_start, 0, n_tiles - 1),) + (0,) * (len(shape) - 1)
    return pl.BlockSpec(shape, index)


def _dispatch_kernel(pend_ref, padded_ref, dest_ref, *rest, tile_starts):
    n_parts = len(tile_starts)
    hn_refs = rest[:n_parts]
    xs_hbm, stage, zbuf, sem = rest[n_parts:]
    tm = hn_refs[0].shape[0]
    i = pl.program_id(0)
    slot = lax.rem(i, 2)
    blk = zbuf.shape[0]

    def wait_rows(s):
        for _ in range(2):
            pltpu.make_async_copy(stage.at[s], xs_hbm.at[pl.ds(0, tm)], sem.at[s]).wait()

    @pl.when(i == 0)
    def _():
        zbuf[...] = jnp.zeros_like(zbuf)

        def fill(row0):
            return pltpu.make_async_copy(
                zbuf, xs_hbm.at[pl.ds(pl.multiple_of(row0, blk), blk)], sem.at[2])

        tail = (pend_ref[N_EXPERTS - 1] // blk, xs_hbm.shape[0] // blk)
        for start in (True, False):
            for e in range(N_EXPERTS):
                @pl.when(padded_ref[e] > 0)
                def _():
                    c = fill(pend_ref[e] - blk)
                    c.start() if start else c.wait()

            def tail_block(b, carry):
                c = fill(b * blk)
                c.start() if start else c.wait()
                return carry

            lax.fori_loop(*tail, tail_block, 0)

    @pl.when(i >= 2)
    def _():
        wait_rows(slot)

    tile = _pack_bf16_pairs(_select_part(i, tile_starts, hn_refs))
    for s in range(2):
        @pl.when(slot == s)
        def _():
            stage[s] = tile
            for k in range(2):
                for r in range(tm):
                    pltpu.make_async_copy(stage.at[s, pl.ds(r, 1), :],
                                          xs_hbm.at[pl.ds(dest_ref[0, 0, k * tm + r], 1), :],
                                          sem.at[s]).start(priority=r % 2)

    @pl.when(i == pl.num_programs(0) - 1)
    def _():
        wait_rows(slot)

        @pl.when(i >= 1)
        def _():
            wait_rows(1 - slot)


def _tile_layout(arrays, tm):
    counts = [a.shape[0] // tm for a in arrays]
    starts = [sum(counts[:p]) for p in range(len(counts))]
    return counts, starts


def _dispatch(pad_end, padded, dest_t, hns, rows, tm, blk):
    counts, starts = _tile_layout(hns, tm)
    grid_spec = pltpu.PrefetchScalarGridSpec(
        num_scalar_prefetch=2,
        grid=(sum(counts),),
        in_specs=[pl.BlockSpec((1, 1, 2 * tm), lambda i, pe, pd: (i, 0, 0),
                               memory_space=pltpu.SMEM)]
        + [_part_spec((tm, D_MODEL), s, n) for s, n in zip(starts, counts)],
        out_specs=pl.BlockSpec(memory_space=pl.ANY),
        scratch_shapes=[
            pltpu.VMEM((2, tm, D_MODEL // 2), jnp.uint32),
            pltpu.VMEM((blk, D_MODEL // 2), jnp.uint32),
            pltpu.SemaphoreType.DMA((3,)),
        ],
    )
    return pl.pallas_call(
        functools.partial(_dispatch_kernel, tile_starts=tuple(starts)),
        grid_spec=grid_spec,
        out_shape=jax.ShapeDtypeStruct((rows, D_MODEL // 2), jnp.uint32),
        compiler_params=pltpu.CompilerParams(
            dimension_semantics=("arbitrary",), vmem_limit_bytes=VMEM_LIMIT),
        name="dispatch",
    )(pad_end, padded, dest_t, *hns)


def _moe_kernel(be_ref, nu_ref, nxt_ref, xs_ref, wg_hbm, wu_hbm, wd_hbm, yb_ref,
                wg_f, wu_f, wd_f, wg_s, wu_s, wd_s, run_s, sem):
    i = pl.program_id(0)

    def fetch(e, slot):
        return [pltpu.make_async_copy(src.at[e], dst.at[slot], sem.at[slot])
                for src, dst in ((wg_hbm, wg_f), (wu_hbm, wu_f), (wd_hbm, wd_f))]

    @pl.when(i < nu_ref[0])
    def _():
        e = be_ref[i]

        @pl.when(i == 0)
        def _():
            run_s[0] = 0
            for c in fetch(e, 0):
                c.start()

        @pl.when((i == 0) | (e != be_ref[jnp.maximum(i - 1, 0)]))
        def _():
            slot = lax.rem(run_s[0], 2)
            run_s[0] = run_s[0] + 1
            for c in fetch(e, slot):
                c.wait()
            wg_s[...] = wg_f[slot].astype(BF16)
            wu_s[...] = wu_f[slot].astype(BF16)
            wd_s[...] = wd_f[slot].astype(BF16)

            @pl.when(nxt_ref[e] != e)
            def _():
                for c in fetch(nxt_ref[e], 1 - slot):
                    c.start()

        xe = _unpack_bf16_pairs(xs_ref[...])
        g = _mm(xe, wg_s[...])
        u = _mm(xe, wu_s[...])
        hmid = ((g * (1.0 / (1.0 + jnp.exp(-g)))) * u).astype(BF16)
        yb_ref[...] = _mm(hmid, wd_s[...])

    @pl.when(i >= nu_ref[0])
    def _():
        yb_ref[...] = jnp.zeros_like(yb_ref)


def _moe(block_e, n_used, next_e, xs, w_gate, w_up, w_down, blk):
    n_blocks = block_e.shape[0]
    in_blk = lambda i, be, nu, nx: (jnp.maximum(jnp.minimum(i, nu[0] - 1), 0), 0)
    grid_spec = pltpu.PrefetchScalarGridSpec(
        num_scalar_prefetch=3,
        grid=(n_blocks,),
        in_specs=[
            pl.BlockSpec((blk, D_MODEL // 2), in_blk),
            pl.BlockSpec(memory_space=pl.ANY),
            pl.BlockSpec(memory_space=pl.ANY),
            pl.BlockSpec(memory_space=pl.ANY),
        ],
        out_specs=pl.BlockSpec((blk, D_MODEL), lambda i, be, nu, nx: (i, 0)),
        scratch_shapes=[
            pltpu.VMEM((2, D_MODEL, D_EXPERT), F32),
            pltpu.VMEM((2, D_MODEL, D_EXPERT), F32),
            pltpu.VMEM((2, D_EXPERT, D_MODEL), F32),
            pltpu.VMEM((D_MODEL, D_EXPERT), BF16),
            pltpu.VMEM((D_MODEL, D_EXPERT), BF16),
            pltpu.VMEM((D_EXPERT, D_MODEL), BF16),
            pltpu.SMEM((1,), jnp.int32),
            pltpu.SemaphoreType.DMA((2,)),
        ],
    )
    return pl.pallas_call(
        _moe_kernel,
        grid_spec=grid_spec,
        out_shape=jax.ShapeDtypeStruct((xs.shape[0], D_MODEL), F32),
        compiler_params=pltpu.CompilerParams(
            dimension_semantics=("arbitrary",), vmem_limit_bytes=VMEM_LIMIT),
        name="moe",
    )(block_e, n_used, next_e, xs, w_gate, w_up, w_down)


def _combine_kernel(dest_ref, dest_next_ref, *rest, tile_starts):
    n_parts = len(tile_starts)
    x2_refs, rt_refs = rest[:n_parts], rest[n_parts:2 * n_parts]
    yb_hbm = rest[2 * n_parts]
    o_refs = rest[2 * n_parts + 1:3 * n_parts + 1]
    buf, sem = rest[3 * n_parts + 1:]
    tm = x2_refs[0].shape[0]
    i = pl.program_id(0)
    slot = lax.rem(i, 2)

    def gather(d_ref, s):
        for k in range(2):
            for r in range(tm):
                pltpu.make_async_copy(yb_hbm.at[pl.ds(d_ref[0, 0, k * tm + r], 1), :],
                                      buf.at[s, k, pl.ds(r, 1), :],
                                      sem.at[s]).start(priority=r % 2)

    @pl.when(i == 0)
    def _():
        gather(dest_ref, 0)

    for s in range(2):
        @pl.when((i + 1 < pl.num_programs(0)) & (slot == 1 - s))
        def _():
            gather(dest_next_ref, s)

    for k in range(2):
        pltpu.make_async_copy(yb_hbm.at[pl.ds(0, tm), :], buf.at[slot, k], sem.at[slot]).wait()
    rt = _select_part(i, tile_starts, rt_refs)
    out = (_select_part(i, tile_starts, x2_refs) + rt[:, 2:3] * buf[slot, 0]
           + rt[:, 3:4] * buf[slot, 1])
    ends = tile_starts[1:] + (pl.num_programs(0),)
    for start, end, o_ref in zip(tile_starts, ends, o_refs):
        @pl.when((i >= start) & (i < end))
        def _():
            o_ref[...] = out


def _combine(dest_t, x2s, rts, yb, tm):
    counts, starts = _tile_layout(x2s, tm)
    nt = sum(counts)
    spec = lambda w: [_part_spec((tm, w), s, n) for s, n in zip(starts, counts)]
    return pl.pallas_call(
        functools.partial(_combine_kernel, tile_starts=tuple(starts)),
        grid=(nt,),
        in_specs=[
            pl.BlockSpec((1, 1, 2 * tm), lambda i: (i, 0, 0), memory_space=pltpu.SMEM),
            pl.BlockSpec((1, 1, 2 * tm), lambda i: (jnp.minimum(i + 1, nt - 1), 0, 0),
                         memory_space=pltpu.SMEM),
        ] + spec(D_MODEL) + spec(LANES) + [pl.BlockSpec(memory_space=pl.ANY)],
        out_specs=spec(D_MODEL),
        out_shape=[jax.ShapeDtypeStruct(x2.shape, F32) for x2 in x2s],
        scratch_shapes=[
            pltpu.VMEM((2, 2, tm, D_MODEL), F32),
            pltpu.SemaphoreType.DMA((2,)),
        ],
        compiler_params=pltpu.CompilerParams(
            dimension_semantics=("arbitrary",), vmem_limit_bytes=VMEM_LIMIT),
        name="combine",
    )(dest_t, dest_t, *x2s, *rts, yb)


def _hier_moe(parts, cnt, w_gate, w_up, w_down, tm, blk):
    counts = cnt[0, ROUTER_COL0:ROUTER_COL0 + N_EXPERTS].astype(jnp.int32)
    padded = (counts + blk - 1) // blk * blk
    pad_end = jnp.cumsum(padded)
    pad_start = pad_end - padded
    t_all = sum(p[0].shape[0] for p in parts)
    n_blocks = (2 * t_all + N_EXPERTS * (blk - 1)) // blk + 1
    rows = n_blocks * blk
    blk_row0 = jnp.arange(n_blocks, dtype=jnp.int32) * blk
    block_e = jnp.minimum(
        jnp.sum((pad_end[None, :] <= blk_row0[:, None]).astype(jnp.int32), axis=1),
        N_EXPERTS - 1)
    n_used = (pad_end[-1] // blk).astype(jnp.int32).reshape(1)
    ids = jnp.arange(N_EXPERTS, dtype=jnp.int32)
    later = (ids[None, :] > ids[:, None]) & (padded[None, :] > 0)
    next_e = jnp.where(jnp.any(later, axis=1),
                       jnp.min(jnp.where(later, ids[None, :], N_EXPERTS), axis=1), ids)
    experts = jnp.arange(N_EXPERTS, dtype=jnp.int32)[:, None, None]

    dests = []
    for x2, _, _, rtt in parts:
        nt = x2.shape[0] // tm
        eid = rtt[0:2].astype(jnp.int32)
        rank = rtt[4:6].astype(jnp.int32)
        dest = rank + jnp.sum(
            jnp.where(eid[None] == experts, pad_start[:, None, None], 0), axis=0)
        dests.append(jnp.concatenate([dest[0].reshape(nt, tm), dest[1].reshape(nt, tm)],
                                     axis=1).reshape(nt, 1, 2 * tm))
    dest_t = jnp.concatenate(dests, axis=0)
    xs = _dispatch(pad_end, padded, dest_t, [p[1] for p in parts], rows, tm, blk)
    yb = _moe(block_e, n_used, next_e.astype(jnp.int32), xs, w_gate, w_up, w_down, blk)
    return _combine(dest_t, [p[0] for p in parts], [p[2] for p in parts], yb, tm)


def _rope_table(pos):
    half = ROPE_DIM // 2
    d = jnp.arange(LANES, dtype=jnp.int32) % HEAD_DIM
    inv = ROPE_THETA ** (-(2 * (d % half)).astype(F32) / ROPE_DIM)
    ang = pos.astype(F32)[:, None] * inv[None, :]
    cos, sin = jnp.cos(ang), jnp.sin(ang)
    rotary = (d < ROPE_DIM)[None, :]
    first = (d < half)[None, :]
    return jnp.concatenate([jnp.where(rotary, cos, 1.0),
                            jnp.where(first, -sin, 0.0),
                            jnp.where(rotary & ~first, sin, 0.0)], axis=1)


def _mixers(x, pos_rope, kctx_prev, vctx_prev, h0r, h0i, mk, mv, wp, sp, cnt0, *,
            tm_in, tq, ssm_l, tm_mid):
    B, S, _ = x.shape
    T = B * S
    q, k3, v3, u_tm = _in_proj(x, wp["gmix"], wp["win"], wp["gq"], wp["gk"], pos_rope, *tm_in)
    if kctx_prev is None:
        kctx, vctx = k3, v3
    else:
        kctx = jnp.concatenate([kctx_prev, k3], axis=1)
        vctx = jnp.concatenate([vctx_prev, v3], axis=1)
    att = _swa(wp["sink"], q, kctx, vctx, tq, mask_context=kctx_prev is None)
    ssm_tm, hr, hi = _ssm(u_tm.reshape(S, B, SSM_WIDTH), h0r, h0i, sp, ssm_l)
    x2, hn, rt, rtt, cnt = _mid(x, att, ssm_tm.reshape(S, B * SSM_WIDTH), mk, mv, wp, cnt0,
                                *tm_mid)
    part = (x2.reshape(T, D_MODEL), hn.reshape(T, D_MODEL), rt.reshape(T, LANES), rtt)
    return part, cnt, k3, v3, hr, hi


def kernel(x_prompt, x_sample, cache_attn_k, cache_attn_v, state_ssm_re, state_ssm_im, cache_mem_k, cache_mem_v, mem_prompt, norm_mix, w_in, q_norm, k_norm, attn_sink, ssm_lambda_re, ssm_lambda_im, ssm_log_dt, ssm_b_re, ssm_b_im, ssm_c_re, ssm_c_im, ssm_d, ssm_w_glu, ssm_b_glu, norm_attn_out, norm_ssm_out, w_out, norm_cross, norm_mem, w_cq, w_ck, w_cv, cq_norm, ck_norm, w_co, norm_ffn, w_router_group, b_router_group, w_router_expert, b_router_expert, w_e_gate, w_e_up, w_e_down):
    depth = norm_mix.shape[0]
    Bp, Lp, _ = x_prompt.shape
    Bs, Ls, _ = x_sample.shape
    yp, ys = x_prompt, x_sample
    rope_p = _rope_table(jnp.arange(Lp, dtype=jnp.int32))
    rope_s = _rope_table(PAST_LEN + jnp.arange(Ls, dtype=jnp.int32))
    outs = [[] for _ in range(10)]
    n_router = N_EXPERT_GROUPS + N_EXPERTS
    for l in range(depth):
        row = lambda a: a[l].astype(F32).reshape(1, -1)
        w_r = jnp.pad(jnp.concatenate([w_router_group[l], w_router_expert[l]], axis=1).astype(F32),
                      ((0, 0), (0, LANES - n_router)))
        w_r_hi = w_r.astype(BF16)
        w_r_lo = (w_r - w_r_hi.astype(F32)).astype(BF16)
        b_r = jnp.pad(jnp.concatenate([b_router_group[l], b_router_expert[l]]).astype(F32),
                      (0, LANES - n_router)).reshape(1, LANES)
        wp = {
            "gmix": row(norm_mix), "win": w_in[l].astype(BF16),
            "gq": jnp.tile(row(q_norm), (1, LANES // HEAD_DIM)),
            "gk": jnp.tile(row(k_norm), (1, LANES // HEAD_DIM)),
            "sink": attn_sink[l].astype(F32),
            "gao": row(norm_attn_out), "gso": row(norm_ssm_out),
            "wout": w_out[l].astype(BF16), "gx": row(norm_cross),
            "wcq": w_cq[l].astype(BF16), "gcq": row(cq_norm),
            "wco": w_co[l].astype(BF16), "gffn": row(norm_ffn),
            "wr": jnp.concatenate([w_r_hi, w_r_lo], axis=1), "br": b_r,
        }
        sp = _ssm_params(ssm_lambda_re[l], ssm_lambda_im[l], ssm_log_dt[l], ssm_b_re[l],
                         ssm_b_im[l], ssm_c_re[l], ssm_c_im[l], ssm_d[l], ssm_w_glu[l],
                         ssm_b_glu[l])
        ew = (w_e_gate[l].astype(F32), w_e_up[l].astype(F32), w_e_down[l].astype(F32))

        w_ckv = jnp.concatenate([w_ck[l], w_cv[l]], axis=1).astype(BF16)
        mkp, mvp = _memkv(mem_prompt.reshape(Bp * N_MEM, D_MODEL), row(norm_mem), w_ckv,
                          row(ck_norm), 512)
        mkp = mkp.reshape(Bp, N_MEM, CA_WIDTH)
        mvp = mvp.reshape(Bp, N_MEM, CA_WIDTH)

        zst = jnp.zeros((Bp, SSM_COLS), F32)
        part_p, cnt_p, kp, vp, hpr, hpi = _mixers(
            yp, rope_p, None, None, zst, zst, mkp, mvp, wp, sp, jnp.zeros((1, LANES), F32),
            tm_in=(1, 512), tq=2048, ssm_l=64, tm_mid=(1, 512))
        part_s, cnt_s, kn, vn, hsr, hsi = _mixers(
            ys, rope_s, cache_attn_k[l].reshape(Bs, WINDOW, KV_WIDTH).astype(F32),
            cache_attn_v[l].reshape(Bs, WINDOW, KV_WIDTH).astype(F32),
            state_ssm_re[l].astype(F32).reshape(Bs, SSM_COLS),
            state_ssm_im[l].astype(F32).reshape(Bs, SSM_COLS),
            cache_mem_k[l].astype(F32).reshape(Bs, N_MEM, CA_WIDTH),
            cache_mem_v[l].astype(F32).reshape(Bs, N_MEM, CA_WIDTH), wp, sp, cnt_p,
            tm_in=(8, Ls), tq=CHUNK, ssm_l=Ls, tm_mid=(8, Ls))
        yp, ys = _hier_moe([part_p, part_s], cnt_s, *ew, 256, 2 * MOE_BLOCK)
        yp = yp.reshape(Bp, Lp, D_MODEL)
        ys = ys.reshape(Bs, Ls, D_MODEL)

        sg = (N_SSM_GROUPS, SSM_STATE)
        kvs = (N_KV_HEADS, HEAD_DIM)
        vals = (kp[:, Lp - WINDOW:].reshape(Bp, WINDOW, *kvs),
                vp[:, Lp - WINDOW:].reshape(Bp, WINDOW, *kvs),
                hpr.reshape(Bp, *sg), hpi.reshape(Bp, *sg),
                mkp.reshape(Bp, N_MEM, CA_HEADS, CA_HEAD_DIM),
                mvp.reshape(Bp, N_MEM, CA_HEADS, CA_HEAD_DIM),
                kn.reshape(Bs, Ls, *kvs), vn.reshape(Bs, Ls, *kvs),
                hsr.reshape(Bs, *sg), hsi.reshape(Bs, *sg))
        for lst, val in zip(outs, vals):
            lst.append(val)
    return (yp, ys) + tuple(jnp.stack(lst) for lst in outs)
```
